```python
import jax, jax.numpy as jnp
from jax import lax
import numpy as np

D_MODEL = 1024
BATCH = 8
SEQ = 4096
DEPTH = 4

N_MIXERS = 3
SB_HEADS = 16
SB_HEAD_DIM = D_MODEL // SB_HEADS
SB_BLOCK = 128
SSD_EXPAND = 2
SSD_D_INNER = SSD_EXPAND * D_MODEL
SSD_HEAD_DIM = 64
SSD_HEADS = SSD_D_INNER // SSD_HEAD_DIM
SSD_GROUPS = 8
SSD_HPG = SSD_HEADS // SSD_GROUPS
SSD_STATE = 128
SSD_CONV = 4
SSD_CHUNK = 128
SSD_CONV_DIM = SSD_D_INNER + 2 * SSD_GROUPS * SSD_STATE
SSD_IN_DIM = SSD_D_INNER + SSD_CONV_DIM + SSD_HEADS
SSD_NORM_GROUP = SSD_D_INNER // SSD_GROUPS
SC_WIDTH = 3
D_FF = 2816
RMS_EPS = 1e-6
N_SB_LAYERS = (DEPTH + 2) // 3
N_SSD_LAYERS = (DEPTH + 1) // 3
N_SC_LAYERS = DEPTH // 3

kernel_name = "hybrid_sb_ssd_shortconv_macaron"


def rmsnorm(x, g):
    xf = x.astype(jnp.float32)
    y = xf * lax.rsqrt(jnp.mean(xf * xf, axis=-1, keepdims=True) + RMS_EPS)
    return (y * g.astype(jnp.float32)).astype(x.dtype)


def swiglu(x, w_gu, w_down):
    gate, up = jnp.split(x @ w_gu, 2, axis=-1)
    return (jax.nn.silu(gate) * up) @ w_down


def causal_depthwise_conv(x, w):
    k, c = w.shape
    return lax.conv_general_dilated(
        x, w[:, None, :].astype(x.dtype), window_strides=(1,), padding=[(k - 1, 0)],
        dimension_numbers=('NWC', 'WIO', 'NWC'), feature_group_count=c)


def stick_breaking_attention(x, w_qkv, w_o):
    b, l, _ = x.shape
    q, k, v = jnp.split(x @ w_qkv, 3, axis=-1)
    q = q.reshape(b, l, SB_HEADS, SB_HEAD_DIM)
    k = k.reshape(b, l, SB_HEADS, SB_HEAD_DIM)
    v = v.reshape(b, l, SB_HEADS, SB_HEAD_DIM)
    n_blk = l // SB_BLOCK
    q_blocks = jnp.moveaxis(q.reshape(b, n_blk, SB_BLOCK, SB_HEADS, SB_HEAD_DIM), 1, 0)
    key_pos = jnp.arange(l)
    scale = SB_HEAD_DIM ** -0.5

    def one_block(args):
        q_blk, blk = args
        z = jnp.einsum('bqhd,bkhd->bhqk', q_blk, k).astype(jnp.float32) * scale
        q_pos = blk * SB_BLOCK + jnp.arange(SB_BLOCK)
        mask = key_pos[None, :] < q_pos[:, None]
        log_beta = jax.nn.log_sigmoid(z)
        log_keep = jnp.where(mask, jax.nn.log_sigmoid(-z), 0.0)
        tail = lax.cumsum(log_keep, axis=3, reverse=True) - log_keep
        att = jnp.where(mask, jnp.exp(log_beta + tail), 0.0)
        return jnp.einsum('bhqk,bkhd->bqhd', att.astype(v.dtype), v)

    o = lax.map(one_block, (q_blocks, jnp.arange(n_blk)))
    o = jnp.moveaxis(o, 0, 1).reshape(b, l, D_MODEL)
    return o @ w_o


def ssd_chunked(xs, dt, a, bm, cm):
    b, l, g, r, p = xs.shape
    n = bm.shape[-1]
    nc, cl = l // SSD_CHUNK, SSD_CHUNK
    x_c = xs.reshape(b, nc, cl, g, r, p)
    dt_c = dt.reshape(b, nc, cl, g, r)
    b_c = bm.reshape(b, nc, cl, g, n)
    c_c = cm.reshape(b, nc, cl, g, n)
    a_cum = jnp.cumsum(dt_c * a, axis=2)
    seg = a_cum[:, :, :, None] - a_cum[:, :, None, :]
    causal = jnp.tril(jnp.ones((cl, cl), dtype=bool))[:, :, None, None]
    decay = jnp.exp(jnp.where(causal, seg, -jnp.inf))
    cb = jnp.einsum('bclgn,bcsgn->bclsg', c_c, b_c)
    w = cb[..., None] * decay * dt_c[:, :, None]
    y_diag = jnp.einsum('bclsgr,bcsgrp->bclgrp', w, x_c)
    a_last = a_cum[:, :, -1]
    to_end = jnp.exp(a_last[:, :, None] - a_cum) * dt_c
    states = jnp.einsum('bcsgn,bcsgr,bcsgrp->bcgrpn', b_c, to_end, x_c)

    def step(h, inp):
        st, al = inp
        return h * jnp.exp(al)[..., None, None] + st, h

    h0 = jnp.zeros((b, g, r, p, n), dtype=xs.dtype)
    _, h_prev = lax.scan(step, h0, (jnp.moveaxis(states, 1, 0), jnp.moveaxis(a_last, 1, 0)))
    h_prev = jnp.moveaxis(h_prev, 0, 1)
    y_off = jnp.einsum('bclgn,bcgrpn,bclgr->bclgrp', c_c, h_prev, jnp.exp(a_cum))
    return (y_diag + y_off).reshape(b, l, g, r, p)


def ssd_mixer(x, w_in, conv_w, conv_b, dt_bias, a_log, d_skip, norm_g, w_out):
    b, l, _ = x.shape
    z, xbc, dt = jnp.split(x @ w_in, [SSD_D_INNER, SSD_D_INNER + SSD_CONV_DIM], axis=-1)
    xbc = jax.nn.silu(causal_depthwise_conv(xbc, conv_w) + conv_b)
    xs, bm, cm = jnp.split(xbc, [SSD_D_INNER, SSD_D_INNER + SSD_GROUPS * SSD_STATE], axis=-1)
    xs = xs.astype(jnp.float32).reshape(b, l, SSD_GROUPS, SSD_HPG, SSD_HEAD_DIM)
    bm = bm.astype(jnp.float32).reshape(b, l, SSD_GROUPS, SSD_STATE)
    cm = cm.astype(jnp.float32).reshape(b, l, SSD_GROUPS, SSD_STATE)
    dt = jax.nn.softplus(dt.astype(jnp.float32) + dt_bias.astype(jnp.float32))
    dt = dt.reshape(b, l, SSD_GROUPS, SSD_HPG)
    a = -jnp.exp(a_log.astype(jnp.float32)).reshape(SSD_GROUPS, SSD_HPG)
    y = ssd_chunked(xs, dt, a, bm, cm)
    y = y + d_skip.astype(jnp.float32).reshape(SSD_GROUPS, SSD_HPG, 1) * xs
    y = y.reshape(b, l, SSD_D_INNER).astype(x.dtype) * jax.nn.silu(z)
    y = rmsnorm(y.reshape(b, l, SSD_GROUPS, SSD_NORM_GROUP),
                norm_g.reshape(SSD_GROUPS, SSD_NORM_GROUP)).reshape(b, l, SSD_D_INNER)
    return y @ w_out


def short_conv_mixer(x, w_in, conv_w, w_out):
    b_gate, c_gate, h = jnp.split(x @ w_in, 3, axis=-1)
    u = causal_depthwise_conv(c_gate * h, conv_w)
    return (b_gate * u) @ w_out


def _fwd_setup_inputs(seed: int = 0) -> dict:
    key = jax.random.key(seed)
    ks = jax.random.split(key, 24)
    f32 = jnp.float32

    def wn(k, shape, fan_in):
        return jax.random.normal(k, shape, f32) * (fan_in ** -0.5)

    def gain(k, shape):
        return 1.0 + 0.01 * jax.random.normal(k, shape, f32)

    dt0 = jnp.exp(jax.random.uniform(ks[14], (N_SSD_LAYERS, SSD_HEADS), f32)
                  * (np.log(0.1) - np.log(0.001)) + np.log(0.001))
    dt_bias = dt0 + jnp.log(-jnp.expm1(-dt0))
    return {
        "x": jax.random.normal(ks[0], (BATCH, SEQ, D_MODEL), f32),
        "ffn1_norm": gain(ks[1], (DEPTH, D_MODEL)),
        "ffn1_w_gu": wn(ks[2], (DEPTH, D_MODEL, 2 * D_FF), D_MODEL),
        "ffn1_w_down": wn(ks[3], (DEPTH, D_FF, D_MODEL), D_FF),
        "mix_norm": gain(ks[4], (DEPTH, D_MODEL)),
        "ffn2_norm": gain(ks[5], (DEPTH, D_MODEL)),
        "ffn2_w_gu": wn(ks[6], (DEPTH, D_MODEL, 2 * D_FF), D_MODEL),
        "ffn2_w_down": wn(ks[7], (DEPTH, D_FF, D_MODEL), D_FF),
        "sb_w_qkv": wn(ks[8], (N_SB_LAYERS, D_MODEL, 3 * D_MODEL), D_MODEL),
        "sb_w_o": wn(ks[9], (N_SB_LAYERS, D_MODEL, D_MODEL), D_MODEL),
        "ssd_w_in": wn(ks[10], (N_SSD_LAYERS, D_MODEL, SSD_IN_DIM), D_MODEL),
        "ssd_conv_w": wn(ks[11], (N_SSD_LAYERS, SSD_CONV, SSD_CONV_DIM), SSD_CONV),
        "ssd_conv_b": 0.01 * jax.random.normal(ks[12], (N_SSD_LAYERS, SSD_CONV_DIM), f32),
        "ssd_dt_bias": dt_bias,
        "ssd_a_log": jnp.log(jax.random.uniform(ks[15], (N_SSD_LAYERS, SSD_HEADS), f32, 1.0, 16.0)),
        "ssd_d": gain(ks[16], (N_SSD_LAYERS, SSD_HEADS)),
        "ssd_norm": gain(ks[17], (N_SSD_LAYERS, SSD_D_INNER)),
        "ssd_w_out": wn(ks[18], (N_SSD_LAYERS, SSD_D_INNER, D_MODEL), SSD_D_INNER),
        "sc_w_in": wn(ks[19], (N_SC_LAYERS, D_MODEL, 3 * D_MODEL), D_MODEL),
        "sc_conv_w": wn(ks[20], (N_SC_LAYERS, SC_WIDTH, D_MODEL), SC_WIDTH),
        "sc_w_out": wn(ks[21], (N_SC_LAYERS, D_MODEL, D_MODEL), D_MODEL),
        "final_norm": gain(ks[22], (D_MODEL,)),
    }


def _fwd_reference(x, ffn1_norm, ffn1_w_gu, ffn1_w_down, mix_norm, ffn2_norm, ffn2_w_gu, ffn2_w_down,
              sb_w_qkv, sb_w_o, ssd_w_in, ssd_conv_w, ssd_conv_b, ssd_dt_bias, ssd_a_log, ssd_d,
              ssd_norm, ssd_w_out, sc_w_in, sc_conv_w, sc_w_out, final_norm):
    for i in range(DEPTH):
        x = x + 0.5 * swiglu(rmsnorm(x, ffn1_norm[i]), ffn1_w_gu[i], ffn1_w_down[i])
        h = rmsnorm(x, mix_norm[i])
        kind, j = i % N_MIXERS, i // N_MIXERS
        if kind == 0:
            m = stick_breaking_attention(h, sb_w_qkv[j], sb_w_o[j])
        elif kind == 1:
            m = ssd_mixer(h, ssd_w_in[j], ssd_conv_w[j], ssd_conv_b[j], ssd_dt_bias[j],
                          ssd_a_log[j], ssd_d[j], ssd_norm[j], ssd_w_out[j])
        else:
            m = short_conv_mixer(h, sc_w_in[j], sc_conv_w[j], sc_w_out[j])
        x = x + m
        x = x + 0.5 * swiglu(rmsnorm(x, ffn2_norm[i]), ffn2_w_gu[i], ffn2_w_down[i])
    return rmsnorm(x, final_norm)


import jax as _jax
import jax.numpy as _jnp

TWIN_FORMAT = 'train_step'
FWD_PARAMS = ['x', 'ffn1_norm', 'ffn1_w_gu', 'ffn1_w_down', 'mix_norm', 'ffn2_norm', 'ffn2_w_gu', 'ffn2_w_down', 'sb_w_qkv', 'sb_w_o', 'ssd_w_in', 'ssd_conv_w', 'ssd_conv_b', 'ssd_dt_bias', 'ssd_a_log', 'ssd_d', 'ssd_norm', 'ssd_w_out', 'sc_w_in', 'sc_conv_w', 'sc_w_out', 'final_norm']
TWIN_WEIGHTS = ['ffn1_norm', 'ffn1_w_gu', 'ffn1_w_down', 'mix_norm', 'ffn2_norm', 'ffn2_w_gu', 'ffn2_w_down', 'sb_w_qkv', 'sb_w_o', 'ssd_w_in', 'ssd_conv_w', 'ssd_conv_b', 'ssd_dt_bias', 'ssd_a_log', 'ssd_d', 'ssd_norm', 'ssd_w_out', 'sc_w_in', 'sc_conv_w', 'sc_w_out', 'final_norm']
TWIN_DIFF_INPUT = 'x'
TWIN_INPUTS = ['x', 'ffn1_norm', 'ffn1_w_gu', 'ffn1_w_down', 'mix_norm', 'ffn2_norm', 'ffn2_w_gu', 'ffn2_w_down', 'sb_w_qkv', 'sb_w_o', 'ssd_w_in', 'ssd_conv_w', 'ssd_conv_b', 'ssd_dt_bias', 'ssd_a_log', 'ssd_d', 'ssd_norm', 'ssd_w_out', 'sc_w_in', 'sc_conv_w', 'sc_w_out', 'final_norm', 'loss_target', 'm_ffn1_norm', 'm_ffn1_w_gu', 'm_ffn1_w_down', 'm_mix_norm', 'm_ffn2_norm', 'm_ffn2_w_gu', 'm_ffn2_w_down', 'm_sb_w_qkv', 'm_sb_w_o', 'm_ssd_w_in', 'm_ssd_conv_w', 'm_ssd_conv_b', 'm_ssd_dt_bias', 'm_ssd_a_log', 'm_ssd_d', 'm_ssd_norm', 'm_ssd_w_out', 'm_sc_w_in', 'm_sc_conv_w', 'm_sc_w_out', 'm_final_norm', 'v_ffn1_norm', 'v_ffn1_w_gu', 'v_ffn1_w_down', 'v_mix_norm', 'v_ffn2_norm', 'v_ffn2_w_gu', 'v_ffn2_w_down', 'v_sb_w_qkv', 'v_sb_w_o', 'v_ssd_w_in', 'v_ssd_conv_w', 'v_ssd_conv_b', 'v_ssd_dt_bias', 'v_ssd_a_log', 'v_ssd_d', 'v_ssd_norm', 'v_ssd_w_out', 'v_sc_w_in', 'v_sc_conv_w', 'v_sc_w_out', 'v_final_norm']
TWIN_OUTPUTS = ['loss', 'grad_x', 'grad_ffn1_norm', 'grad_ffn1_w_gu', 'grad_ffn1_w_down', 'grad_mix_norm', 'grad_ffn2_norm', 'grad_ffn2_w_gu', 'grad_ffn2_w_down', 'grad_sb_w_qkv', 'grad_sb_w_o', 'grad_ssd_w_in', 'grad_ssd_conv_w', 'grad_ssd_conv_b', 'grad_ssd_dt_bias', 'grad_ssd_a_log', 'grad_ssd_d', 'grad_ssd_norm', 'grad_ssd_w_out', 'grad_sc_w_in', 'grad_sc_conv_w', 'grad_sc_w_out', 'grad_final_norm', 'delta_ffn1_norm', 'delta_ffn1_w_gu', 'delta_ffn1_w_down', 'delta_mix_norm', 'delta_ffn2_norm', 'delta_ffn2_w_gu', 'delta_ffn2_w_down', 'delta_sb_w_qkv', 'delta_sb_w_o', 'delta_ssd_w_in', 'delta_ssd_conv_w', 'delta_ssd_conv_b', 'delta_ssd_dt_bias', 'delta_ssd_a_log', 'delta_ssd_d', 'delta_ssd_norm', 'delta_ssd_w_out', 'delta_sc_w_in', 'delta_sc_conv_w', 'delta_sc_w_out', 'delta_final_norm', 'new_m_ffn1_norm', 'new_m_ffn1_w_gu', 'new_m_ffn1_w_down', 'new_m_mix_norm', 'new_m_ffn2_norm', 'new_m_ffn2_w_gu', 'new_m_ffn2_w_down', 'new_m_sb_w_qkv', 'new_m_sb_w_o', 'new_m_ssd_w_in', 'new_m_ssd_conv_w', 'new_m_ssd_conv_b', 'new_m_ssd_dt_bias', 'new_m_ssd_a_log', 'new_m_ssd_d', 'new_m_ssd_norm', 'new_m_ssd_w_out', 'new_m_sc_w_in', 'new_m_sc_conv_w', 'new_m_sc_w_out', 'new_m_final_norm', 'new_v_ffn1_norm', 'new_v_ffn1_w_gu', 'new_v_ffn1_w_down', 'new_v_mix_norm', 'new_v_ffn2_norm', 'new_v_ffn2_w_gu', 'new_v_ffn2_w_down', 'new_v_sb_w_qkv', 'new_v_sb_w_o', 'new_v_ssd_w_in', 'new_v_ssd_conv_w', 'new_v_ssd_conv_b', 'new_v_ssd_dt_bias', 'new_v_ssd_a_log', 'new_v_ssd_d', 'new_v_ssd_norm', 'new_v_ssd_w_out', 'new_v_sc_w_in', 'new_v_sc_conv_w', 'new_v_sc_w_out', 'new_v_final_norm']
TWIN_LEAF_KINDS = {'loss': 'loss', 'grad_x': 'grad_x', 'grad_ffn1_norm': 'grad_w', 'grad_ffn1_w_gu': 'grad_w', 'grad_ffn1_w_down': 'grad_w', 'grad_mix_norm': 'grad_w', 'grad_ffn2_norm': 'grad_w', 'grad_ffn2_w_gu': 'grad_w', 'grad_ffn2_w_down': 'grad_w', 'grad_sb_w_qkv': 'grad_w', 'grad_sb_w_o': 'grad_w', 'grad_ssd_w_in': 'grad_w', 'grad_ssd_conv_w': 'grad_w', 'grad_ssd_conv_b': 'grad_w', 'grad_ssd_dt_bias': 'grad_w', 'grad_ssd_a_log': 'grad_w', 'grad_ssd_d': 'grad_w', 'grad_ssd_norm': 'grad_w', 'grad_ssd_w_out': 'grad_w', 'grad_sc_w_in': 'grad_w', 'grad_sc_conv_w': 'grad_w', 'grad_sc_w_out': 'grad_w', 'grad_final_norm': 'grad_w', 'delta_ffn1_norm': 'delta_w', 'delta_ffn1_w_gu': 'delta_w', 'delta_ffn1_w_down': 'delta_w', 'delta_mix_norm': 'delta_w', 'delta_ffn2_norm': 'delta_w', 'delta_ffn2_w_gu': 'delta_w', 'delta_ffn2_w_down': 'delta_w', 'delta_sb_w_qkv': 'delta_w', 'delta_sb_w_o': 'delta_w', 'delta_ssd_w_in': 'delta_w', 'delta_ssd_conv_w': 'delta_w', 'delta_ssd_conv_b': 'delta_w', 'delta_ssd_dt_bias': 'delta_w', 'delta_ssd_a_log': 'delta_w', 'delta_ssd_d': 'delta_w', 'delta_ssd_norm': 'delta_w', 'delta_ssd_w_out': 'delta_w', 'delta_sc_w_in': 'delta_w', 'delta_sc_conv_w': 'delta_w', 'delta_sc_w_out': 'delta_w', 'delta_final_norm': 'delta_w', 'new_m_ffn1_norm': 'new_m', 'new_m_ffn1_w_gu': 'new_m', 'new_m_ffn1_w_down': 'new_m', 'new_m_mix_norm': 'new_m', 'new_m_ffn2_norm': 'new_m', 'new_m_ffn2_w_gu': 'new_m', 'new_m_ffn2_w_down': 'new_m', 'new_m_sb_w_qkv': 'new_m', 'new_m_sb_w_o': 'new_m', 'new_m_ssd_w_in': 'new_m', 'new_m_ssd_conv_w': 'new_m', 'new_m_ssd_conv_b': 'new_m', 'new_m_ssd_dt_bias': 'new_m', 'new_m_ssd_a_log': 'new_m', 'new_m_ssd_d': 'new_m', 'new_m_ssd_norm': 'new_m', 'new_m_ssd_w_out': 'new_m', 'new_m_sc_w_in': 'new_m', 'new_m_sc_conv_w': 'new_m', 'new_m_sc_w_out': 'new_m', 'new_m_final_norm': 'new_m', 'new_v_ffn1_norm': 'new_v', 'new_v_ffn1_w_gu': 'new_v', 'new_v_ffn1_w_down': 'new_v', 'new_v_mix_norm': 'new_v', 'new_v_ffn2_norm': 'new_v', 'new_v_ffn2_w_gu': 'new_v', 'new_v_ffn2_w_down': 'new_v', 'new_v_sb_w_qkv': 'new_v', 'new_v_sb_w_o': 'new_v', 'new_v_ssd_w_in': 'new_v', 'new_v_ssd_conv_w': 'new_v', 'new_v_ssd_conv_b': 'new_v', 'new_v_ssd_dt_bias': 'new_v', 'new_v_ssd_a_log': 'new_v', 'new_v_ssd_d': 'new_v', 'new_v_ssd_norm': 'new_v', 'new_v_ssd_w_out': 'new_v', 'new_v_sc_w_in': 'new_v', 'new_v_sc_conv_w': 'new_v', 'new_v_sc_w_out': 'new_v', 'new_v_final_norm': 'new_v'}


def _forward(args):
    return _fwd_reference(*[args[k] for k in FWD_PARAMS])


def _output_shape():
    def fwd():
        inp = _fwd_setup_inputs(0)
        return _fwd_reference(*[inp[k] for k in FWD_PARAMS])
    out = _jax.eval_shape(fwd)
    return out.shape, out.dtype

N_MICROBATCH = 1
ADAM_LR = 0.001
ADAM_B1 = 0.9
ADAM_B2 = 0.999
ADAM_EPS = 1e-08
ADAM_WD = 0.01
ADAM_STEP = 10
PER_EXAMPLE_BATCH_AXIS = {'x': 0, 'loss_target': 0}
SHARED_INPUTS = []
_WEIGHT_DTYPES = {'ffn1_norm': _jnp.float32, 'ffn1_w_gu': _jnp.float32, 'ffn1_w_down': _jnp.float32, 'mix_norm': _jnp.float32, 'ffn2_norm': _jnp.float32, 'ffn2_w_gu': _jnp.float32, 'ffn2_w_down': _jnp.float32, 'sb_w_qkv': _jnp.float32, 'sb_w_o': _jnp.float32, 'ssd_w_in': _jnp.float32, 'ssd_conv_w': _jnp.float32, 'ssd_conv_b': _jnp.float32, 'ssd_dt_bias': _jnp.float32, 'ssd_a_log': _jnp.float32, 'ssd_d': _jnp.float32, 'ssd_norm': _jnp.float32, 'ssd_w_out': _jnp.float32, 'sc_w_in': _jnp.float32, 'sc_conv_w': _jnp.float32, 'sc_w_out': _jnp.float32, 'final_norm': _jnp.float32}
MOMENT_SCALE = {'ffn1_norm': 8.941626e-02, 'ffn1_w_gu': 3.790463e-02, 'ffn1_w_down': 6.174497e-02, 'mix_norm': 1.773798e-01, 'ffn2_norm': 6.585967e-02, 'ffn2_w_gu': 2.746441e-02, 'ffn2_w_down': 4.474837e-02, 'sb_w_qkv': 8.291856e-02, 'sb_w_o': 1.206149e-01, 'ssd_w_in': 8.934234e-02, 'ssd_conv_w': 7.767621e-02, 'ssd_conv_b': 1.032815e-01, 'ssd_dt_bias': 2.180117e-01, 'ssd_a_log': 2.441993e-01, 'ssd_d': 6.684978e-01, 'ssd_norm': 1.107501e-01, 'ssd_w_out': 1.451771e-01, 'sc_w_in': 9.425488e-02, 'sc_conv_w': 9.854848e-02, 'sc_w_out': 9.485501e-02, 'final_norm': 3.200814e+01}


def _to_microbatches(a, axis):
    t = _jnp.moveaxis(a, axis, 0)
    t = t.reshape((N_MICROBATCH, t.shape[0] // N_MICROBATCH) + t.shape[1:])
    return _jnp.moveaxis(t, 1, axis + 1)


def setup_inputs(seed: int = 0) -> dict:
    inp = _fwd_setup_inputs(seed)
    key = _jax.random.fold_in(_jax.random.key(seed), 7919)
    shape, _ = _output_shape()
    out = dict(inp)
    out["loss_target"] = _jax.random.normal(_jax.random.fold_in(key, 0), shape, _jnp.float32)
    for i, name in enumerate(TWIN_WEIGHTS):
        w = inp[name].astype(_jnp.float32)
        if MOMENT_SCALE is None:
            s = _jnp.sqrt(_jnp.mean(_jnp.square(w)) + 1e-30)
        else:
            s = MOMENT_SCALE[name]
        km, kv = _jax.random.split(_jax.random.fold_in(key, i + 1))
        out[name] = w
        out["m_" + name] = s * _jax.random.normal(km, w.shape, _jnp.float32)
        out["v_" + name] = (s * s) * _jax.random.uniform(kv, w.shape, _jnp.float32, 0.5, 1.5)
    if N_MICROBATCH > 1:
        for name, axis in PER_EXAMPLE_BATCH_AXIS.items():
            out[name] = _to_microbatches(out[name], axis)
    return {'x': out['x'], 'ffn1_norm': out['ffn1_norm'], 'ffn1_w_gu': out['ffn1_w_gu'], 'ffn1_w_down': out['ffn1_w_down'], 'mix_norm': out['mix_norm'], 'ffn2_norm': out['ffn2_norm'], 'ffn2_w_gu': out['ffn2_w_gu'], 'ffn2_w_down': out['ffn2_w_down'], 'sb_w_qkv': out['sb_w_qkv'], 'sb_w_o': out['sb_w_o'], 'ssd_w_in': out['ssd_w_in'], 'ssd_conv_w': out['ssd_conv_w'], 'ssd_conv_b': out['ssd_conv_b'], 'ssd_dt_bias': out['ssd_dt_bias'], 'ssd_a_log': out['ssd_a_log'], 'ssd_d': out['ssd_d'], 'ssd_norm': out['ssd_norm'], 'ssd_w_out': out['ssd_w_out'], 'sc_w_in': out['sc_w_in'], 'sc_conv_w': out['sc_conv_w'], 'sc_w_out': out['sc_w_out'], 'final_norm': out['final_norm'], 'loss_target': out['loss_target'], 'm_ffn1_norm': out['m_ffn1_norm'], 'm_ffn1_w_gu': out['m_ffn1_w_gu'], 'm_ffn1_w_down': out['m_ffn1_w_down'], 'm_mix_norm': out['m_mix_norm'], 'm_ffn2_norm': out['m_ffn2_norm'], 'm_ffn2_w_gu': out['m_ffn2_w_gu'], 'm_ffn2_w_down': out['m_ffn2_w_down'], 'm_sb_w_qkv': out['m_sb_w_qkv'], 'm_sb_w_o': out['m_sb_w_o'], 'm_ssd_w_in': out['m_ssd_w_in'], 'm_ssd_conv_w': out['m_ssd_conv_w'], 'm_ssd_conv_b': out['m_ssd_conv_b'], 'm_ssd_dt_bias': out['m_ssd_dt_bias'], 'm_ssd_a_log': out['m_ssd_a_log'], 'm_ssd_d': out['m_ssd_d'], 'm_ssd_norm': out['m_ssd_norm'], 'm_ssd_w_out': out['m_ssd_w_out'], 'm_sc_w_in': out['m_sc_w_in'], 'm_sc_conv_w': out['m_sc_conv_w'], 'm_sc_w_out': out['m_sc_w_out'], 'm_final_norm': out['m_final_norm'], 'v_ffn1_norm': out['v_ffn1_norm'], 'v_ffn1_w_gu': out['v_ffn1_w_gu'], 'v_ffn1_w_down': out['v_ffn1_w_down'], 'v_mix_norm': out['v_mix_norm'], 'v_ffn2_norm': out['v_ffn2_norm'], 'v_ffn2_w_gu': out['v_ffn2_w_gu'], 'v_ffn2_w_down': out['v_ffn2_w_down'], 'v_sb_w_qkv': out['v_sb_w_qkv'], 'v_sb_w_o': out['v_sb_w_o'], 'v_ssd_w_in': out['v_ssd_w_in'], 'v_ssd_conv_w': out['v_ssd_conv_w'], 'v_ssd_conv_b': out['v_ssd_conv_b'], 'v_ssd_dt_bias': out['v_ssd_dt_bias'], 'v_ssd_a_log': out['v_ssd_a_log'], 'v_ssd_d': out['v_ssd_d'], 'v_ssd_norm': out['v_ssd_norm'], 'v_ssd_w_out': out['v_ssd_w_out'], 'v_sc_w_in': out['v_sc_w_in'], 'v_sc_conv_w': out['v_sc_conv_w'], 'v_sc_w_out': out['v_sc_w_out'], 'v_final_norm': out['v_final_norm']}


def _loss(weights, diff, rest, loss_target):
    with _jax.named_scope("forward"):
        args = {**rest, TWIN_DIFF_INPUT: diff, **{k: w.astype(_WEIGHT_DTYPES[k]) for k, w in weights.items()}}
        y = _forward(args)
    with _jax.named_scope("loss_head"):
        err = _jnp.square(y.astype(_jnp.float32) - loss_target)
        return 0.5 * _jnp.sum(_jnp.mean(err, axis=-1)) if err.ndim else 0.5 * err


def _adamw(w, g, m, v):
    m = ADAM_B1 * m + (1.0 - ADAM_B1) * g
    v = ADAM_B2 * v + (1.0 - ADAM_B2) * _jnp.square(g)
    m_hat = m / (1.0 - ADAM_B1 ** ADAM_STEP)
    v_hat = v / (1.0 - ADAM_B2 ** ADAM_STEP)
    delta = -ADAM_LR * (m_hat / (_jnp.sqrt(v_hat) + ADAM_EPS) + ADAM_WD * w)
    return delta, m, v


def reference(x, ffn1_norm, ffn1_w_gu, ffn1_w_down, mix_norm, ffn2_norm, ffn2_w_gu, ffn2_w_down, sb_w_qkv, sb_w_o, ssd_w_in, ssd_conv_w, ssd_conv_b, ssd_dt_bias, ssd_a_log, ssd_d, ssd_norm, ssd_w_out, sc_w_in, sc_conv_w, sc_w_out, final_norm, loss_target, m_ffn1_norm, m_ffn1_w_gu, m_ffn1_w_down, m_mix_norm, m_ffn2_norm, m_ffn2_w_gu, m_ffn2_w_down, m_sb_w_qkv, m_sb_w_o, m_ssd_w_in, m_ssd_conv_w, m_ssd_conv_b, m_ssd_dt_bias, m_ssd_a_log, m_ssd_d, m_ssd_norm, m_ssd_w_out, m_sc_w_in, m_sc_conv_w, m_sc_w_out, m_final_norm, v_ffn1_norm, v_ffn1_w_gu, v_ffn1_w_down, v_mix_norm, v_ffn2_norm, v_ffn2_w_gu, v_ffn2_w_down, v_sb_w_qkv, v_sb_w_o, v_ssd_w_in, v_ssd_conv_w, v_ssd_conv_b, v_ssd_dt_bias, v_ssd_a_log, v_ssd_d, v_ssd_norm, v_ssd_w_out, v_sc_w_in, v_sc_conv_w, v_sc_w_out, v_final_norm):
    given = dict(x=x, ffn1_norm=ffn1_norm, ffn1_w_gu=ffn1_w_gu, ffn1_w_down=ffn1_w_down, mix_norm=mix_norm, ffn2_norm=ffn2_norm, ffn2_w_gu=ffn2_w_gu, ffn2_w_down=ffn2_w_down, sb_w_qkv=sb_w_qkv, sb_w_o=sb_w_o, ssd_w_in=ssd_w_in, ssd_conv_w=ssd_conv_w, ssd_conv_b=ssd_conv_b, ssd_dt_bias=ssd_dt_bias, ssd_a_log=ssd_a_log, ssd_d=ssd_d, ssd_norm=ssd_norm, ssd_w_out=ssd_w_out, sc_w_in=sc_w_in, sc_conv_w=sc_conv_w, sc_w_out=sc_w_out, final_norm=final_norm, loss_target=loss_target, m_ffn1_norm=m_ffn1_norm, m_ffn1_w_gu=m_ffn1_w_gu, m_ffn1_w_down=m_ffn1_w_down, m_mix_norm=m_mix_norm, m_ffn2_norm=m_ffn2_norm, m_ffn2_w_gu=m_ffn2_w_gu, m_ffn2_w_down=m_ffn2_w_down, m_sb_w_qkv=m_sb_w_qkv, m_sb_w_o=m_sb_w_o, m_ssd_w_in=m_ssd_w_in, m_ssd_conv_w=m_ssd_conv_w, m_ssd_conv_b=m_ssd_conv_b, m_ssd_dt_bias=m_ssd_dt_bias, m_ssd_a_log=m_ssd_a_log, m_ssd_d=m_ssd_d, m_ssd_norm=m_ssd_norm, m_ssd_w_out=m_ssd_w_out, m_sc_w_in=m_sc_w_in, m_sc_conv_w=m_sc_conv_w, m_sc_w_out=m_sc_w_out, m_final_norm=m_final_norm, v_ffn1_norm=v_ffn1_norm, v_ffn1_w_gu=v_ffn1_w_gu, v_ffn1_w_down=v_ffn1_w_down, v_mix_norm=v_mix_norm, v_ffn2_norm=v_ffn2_norm, v_ffn2_w_gu=v_ffn2_w_gu, v_ffn2_w_down=v_ffn2_w_down, v_sb_w_qkv=v_sb_w_qkv, v_sb_w_o=v_sb_w_o, v_ssd_w_in=v_ssd_w_in, v_ssd_conv_w=v_ssd_conv_w, v_ssd_conv_b=v_ssd_conv_b, v_ssd_dt_bias=v_ssd_dt_bias, v_ssd_a_log=v_ssd_a_log, v_ssd_d=v_ssd_d, v_ssd_norm=v_ssd_norm, v_ssd_w_out=v_ssd_w_out, v_sc_w_in=v_sc_w_in, v_sc_conv_w=v_sc_conv_w, v_sc_w_out=v_sc_w_out, v_final_norm=v_final_norm)
    weights = {n: given[n] for n in TWIN_WEIGHTS}
    shared = {n: given[n] for n in SHARED_INPUTS}
    per_example = {n: given[n] for n in ['x']}
    grad_fn = _jax.value_and_grad(_loss, argnums=(0, 1))

    def one_microbatch(ex, loss_target):
        ex = dict(ex)
        diff = ex.pop(TWIN_DIFF_INPUT)
        return grad_fn(weights, diff, {**shared, **ex}, loss_target)

    if N_MICROBATCH == 1:
        loss, (grad_w, grad_x) = one_microbatch(per_example, given["loss_target"])
    else:
        def body(carry, xs):
            loss_sum, grad_sum = carry
            l_k, (gw_k, gx_k) = one_microbatch(xs[0], xs[1])
            with _jax.named_scope("update"):
                return (loss_sum + l_k, _jax.tree.map(_jnp.add, grad_sum, gw_k)), gx_k

        init = (_jnp.zeros((), _jnp.float32), _jax.tree.map(_jnp.zeros_like, weights))
        (loss, grad_w), grad_x = _jax.lax.scan(body, init, (per_example, given["loss_target"]))
    with _jax.named_scope("update"):
        delta_w, new_m, new_v = {}, {}, {}
        for n in TWIN_WEIGHTS:
            delta_w[n], new_m[n], new_v[n] = _adamw(weights[n], grad_w[n], given["m_" + n], given["v_" + n])
    return (loss, grad_x, *[grad_w[n] for n in TWIN_WEIGHTS], *[delta_w[n] for n in TWIN_WEIGHTS],
            *[new_m[n] for n in TWIN_WEIGHTS], *[new_v[n] for n in TWIN_WEIGHTS])
```

```python
import functools
import math

import jax
import jax.numpy as jnp
from jax import lax
from jax.experimental import pallas as pl
from jax.experimental.pallas import tpu as pltpu

F32 = jnp.float32
BF16 = jnp.bfloat16

D_MODEL = 1024
D_FF = 2816
DEPTH = 4
N_DEV = 8
SB_HEADS = 16
SB_HEAD_DIM = 64
SB_TILE = 256
SSD_HEADS = 32
SSD_HEAD_DIM = 64
SSD_GROUPS = 8
SSD_HPG = 4
SSD_STATE = 128
SSD_CHUNK = 128
SSD_D_INNER = 2048
SSD_CONV_DIM = 4096
SSD_IN_DIM = 6176
SSD_IN_PAD = 6272
SSD_NORM_GROUP = 256
RMS_EPS = 1e-6
ADAM_LR = 0.001
ADAM_B1 = 0.9
ADAM_B2 = 0.999
ADAM_EPS = 1e-08
ADAM_WD = 0.01
ADAM_STEP = 10
VMEM_LIMIT = 56 * 1024 * 1024

NT = (((1,), (1,)), ((), ()))
TN = (((0,), (0,)), ((), ()))
NN = (((1,), (0,)), ((), ()))


def _cparams(n_axes):
    return pltpu.CompilerParams(dimension_semantics=("arbitrary",) * n_axes, vmem_limit_bytes=VMEM_LIMIT)


def _tile(n, want, mult=8):
    if n <= want:
        return n
    for t in range(want, 0, -1):
        if n % t == 0 and t % mult == 0:
            return t
    return n


def _sigmoid(x):
    return 1.0 / (1.0 + jnp.exp(-x))


def _dot(a, b, dn=NN):
    return lax.dot_general(a, b, dn, preferred_element_type=F32)


def _split3(x):
    x1 = x.astype(BF16)
    r1 = x - x1.astype(F32)
    x2 = r1.astype(BF16)
    x3 = (r1 - x2.astype(F32)).astype(BF16)
    return x1, x2, x3


def _dot_exact(x, t):
    x1, x2, x3 = _split3(x)
    return _dot(x1, t) + _dot(x2, t) + _dot(x3, t)


def _dot_hilo(x, t):
    x1 = x.astype(BF16)
    x2 = (x - x1.astype(F32)).astype(BF16)
    return _dot(x1, t) + _dot(x2, t)


def _mm(a, b, *, name, ta=False, tb=False, sa=False, sb=False, so=False, tm=512, tn=1024, tk=1024,
        out_dtype=F32, epilogue=None, extras=(), outs=None, pair=None):
    ash, bsh = a.shape[-2:], b.shape[-2:]
    m, k = (ash[1], ash[0]) if ta else ash
    n = bsh[0] if tb else bsh[1]
    s_n = pair or (a.shape[0] if sa else (b.shape[0] if sb else 1))
    tm, tn, tk = _tile(m, tm), _tile(n, tn, 128), _tile(k, tk, 128)
    nk = k // tk
    if outs is None:
        outs = [(out_dtype, "stile" if so else "tile")]
    if epilogue is None:
        epilogue = lambda acc: (acc,)

    if ta:
        a_blk, a_idx = (tk, tm), (lambda j, i, kk: (kk, i))
    else:
        a_blk, a_idx = (tm, tk), (lambda j, i, kk: (i, kk))
    if tb:
        b_blk, b_idx = (tn, tk), (lambda j, i, kk: (j, kk))
    else:
        b_blk, b_idx = (tk, tn), (lambda j, i, kk: (kk, j))

    def lead(blk, idx, has_s):
        if not has_s:
            return pl.BlockSpec(blk, idx)
        return pl.BlockSpec((s_n,) + blk, lambda j, i, kk: (0,) + idx(j, i, kk))

    kinds = {
        "tile": lambda: pl.BlockSpec((tm, tn), lambda j, i, kk: (i, j)),
        "stile": lambda: pl.BlockSpec((s_n, tm, tn), lambda j, i, kk: (0, i, j)),
        "row": lambda: pl.BlockSpec((1, tn), lambda j, i, kk: (0, j)),
        "colsum": lambda: pl.BlockSpec((1, tn), lambda j, i, kk: (0, j)),
    }
    shapes = {"tile": (m, n), "stile": (s_n, m, n), "colsum": (1, n)}
    in_specs = [lead(a_blk, a_idx, sa), lead(b_blk, b_idx, sb)] + [kinds[kd]() for _, kd in extras]
    out_specs = [kinds[kd]() for _, kd in outs]
    out_shape = [jax.ShapeDtypeStruct(shapes[kd], dt) for dt, kd in outs]
    n_ex, n_out = len(extras), len(outs)
    dn = ((((0,) if ta else (1,)), ((1,) if tb else (0,))), ((), ()))
    acc_shape = (s_n, tm, tn) if so else (tm, tn)

    def body(*refs):
        a_ref, b_ref = refs[0], refs[1]
        ex_refs = refs[2:2 + n_ex]
        o_refs = refs[2 + n_ex:2 + n_ex + n_out]
        acc = refs[-1]
        i = pl.program_id(1)
        kk = pl.program_id(2)

        @pl.when(kk == 0)
        def _():
            acc[...] = jnp.zeros_like(acc)

        for s in range(s_n if (sa or sb) else 1):
            av = (a_ref[s] if sa else a_ref[...]).astype(BF16)
            bv = (b_ref[s] if sb else b_ref[...]).astype(BF16)
            d = lax.dot_general(av, bv, dn, preferred_element_type=F32)
            if so:
                acc[s] += d
            else:
                acc[...] += d

        @pl.when(kk == nk - 1)
        def _():
            accv = tuple(acc[s] for s in range(s_n)) if so else acc[...]
            vals = epilogue(accv, *[r[...] for r in ex_refs])
            for (dt, kd), o_ref, val in zip(outs, o_refs, vals):
                if kd == "colsum":
                    _accumulate(o_ref, val, i == 0)
                elif kd == "stile":
                    for s in range(s_n):
                        o_ref[s] = val[s].astype(dt)
                else:
                    o_ref[...] = val.astype(dt)

    res = pl.pallas_call(
        body, name=name, grid=(n // tn, m // tm, nk),
        in_specs=in_specs, out_specs=out_specs, out_shape=out_shape,
        scratch_shapes=[pltpu.VMEM(acc_shape, F32)], compiler_params=_cparams(3),
    )(a, b, *[e for e, _ in extras])
    return res[0] if len(res) == 1 else res


def _accumulate(o_ref, val, first):
    @pl.when(first)
    def _():
        o_ref[...] = val

    @pl.when(jnp.logical_not(first))
    def _():
        o_ref[...] += val


def _rmsnorm(x, g, *, name):
    l, d = x.shape
    tm = _tile(l, 512)

    def body(x_ref, g_ref, o_ref):
        xv = x_ref[...]
        r = lax.rsqrt(jnp.mean(xv * xv, axis=1, keepdims=True) + RMS_EPS)
        o_ref[...] = (xv * r * g_ref[...]).astype(BF16)

    return pl.pallas_call(
        body, name=name, grid=(l // tm,),
        in_specs=[pl.BlockSpec((tm, d), lambda i: (i, 0)), pl.BlockSpec((1, d), lambda i: (0, 0))],
        out_specs=pl.BlockSpec((tm, d), lambda i: (i, 0)),
        out_shape=jax.ShapeDtypeStruct((l, d), BF16), compiler_params=_cparams(1),
    )(x, g)


def _norm_bwd_epilogue(dh, x, g, dres):
    r = lax.rsqrt(jnp.mean(x * x, axis=1, keepdims=True) + RMS_EPS)
    xh = x * r
    dg = jnp.sum(dh * xh, axis=0, keepdims=True)
    dxh = dh * g
    dx = r * (dxh - xh * jnp.mean(dxh * xh, axis=1, keepdims=True))
    return dres + dx, dg


def _final_loss(x, g, tgt, *, name):
    l, d = x.shape
    tm = _tile(l, 512)

    def body(x_ref, g_ref, t_ref, loss_ref, dx_ref, dg_ref):
        i = pl.program_id(0)
        xv, gv = x_ref[...], g_ref[...]
        r = lax.rsqrt(jnp.mean(xv * xv, axis=1, keepdims=True) + RMS_EPS)
        xh = xv * r
        e = xh * gv - t_ref[...]
        part = 0.5 * jnp.sum(jnp.mean(e * e, axis=1, keepdims=True), axis=0, keepdims=True)
        dy = e * (1.0 / d)
        dg = jnp.sum(dy * xh, axis=0, keepdims=True)
        dxh = dy * gv
        dx_ref[...] = r * (dxh - xh * jnp.mean(dxh * xh, axis=1, keepdims=True))
        _accumulate(dg_ref, dg, i == 0)
        _accumulate(loss_ref, jnp.broadcast_to(part, (1, 128)), i == 0)

    return pl.pallas_call(
        body, name=name, grid=(l // tm,),
        in_specs=[pl.BlockSpec((tm, d), lambda i: (i, 0)), pl.BlockSpec((1, d), lambda i: (0, 0)),
                  pl.BlockSpec((tm, d), lambda i: (i, 0))],
        out_specs=[pl.BlockSpec((1, 128), lambda i: (0, 0)), pl.BlockSpec((tm, d), lambda i: (i, 0)),
                   pl.BlockSpec((1, d), lambda i: (0, 0))],
        out_shape=[jax.ShapeDtypeStruct((1, 128), F32), jax.ShapeDtypeStruct((l, d), F32),
                   jax.ShapeDtypeStruct((1, d), F32)],
        compiler_params=_cparams(1),
    )(x, g, tgt)


def _ffn_fwd(x, g, wgu, wd, tag):
    h = _rmsnorm(x, g, name=f"{tag}_norm")

    def act(acc):
        gate, up = acc
        return acc, gate * _sigmoid(gate) * up

    gu, a = _mm(h, wgu, sb=True, so=True, tm=256, tn=1408, tk=1024, name=f"{tag}_up",
                outs=[(BF16, "stile"), (BF16, "tile")], epilogue=act)
    xo = _mm(a, wd, tm=512, tn=1024, tk=2816, name=f"{tag}_down", extras=[(x, "tile")],
             epilogue=lambda acc, xt: (xt + 0.5 * acc,))
    return xo, (x, h, gu, a)


def _ffn_bwd(dout, saved, g, wgu, wd, tag):
    x, h, gu, a = saved

    def act_bwd(acc, guv):
        da = 0.5 * acc
        gate, up = guv[0].astype(F32), guv[1].astype(F32)
        s = _sigmoid(gate)
        return ((da * up * s * (1.0 + gate * (1.0 - s)), da * gate * s),)

    dgu = _mm(dout, wd, tb=True, pair=2, tm=256, tn=1408, tk=1024, name=f"{tag}_dact", extras=[(gu, "stile")],
              outs=[(BF16, "stile")], epilogue=act_bwd)
    dwd = _mm(a, dout, ta=True, tm=1408, tn=1024, tk=512, name=f"{tag}_dwd", out_dtype=BF16,
              epilogue=lambda acc: (0.5 * acc,))
    dwgu = _mm(h, dgu, ta=True, sb=True, so=True, tm=512, tn=1408, tk=512, name=f"{tag}_dwgu", out_dtype=BF16)
    dx, dg = _mm(dgu, wgu, tb=True, sa=True, sb=True, tm=256, tn=1024, tk=1408, name=f"{tag}_dx",
                 extras=[(x, "tile"), (g, "row"), (dout, "tile")], outs=[(F32, "tile"), (F32, "colsum")],
                 epilogue=_norm_bwd_epilogue)
    return dx, dg, dwgu, dwd


def _sb_scores(qb, ks, scale):
    z = _dot(qb, ks, NT) * scale
    lb = jnp.minimum(z, 0.0) - jnp.log(1.0 + jnp.exp(-jnp.abs(z)))
    return z, lb, lb - z


def _sb_fwd(q, k, v, *, name):
    h_n, l, dh = q.shape
    t = _tile(l, SB_TILE)
    scale = dh ** -0.5

    def body(q_ref, k_ref, v_ref, o_ref):
        i = pl.program_id(1)
        qb = q_ref[...]
        row = lax.broadcasted_iota(jnp.int32, (t, t), 0)
        col = lax.broadcasted_iota(jnp.int32, (t, t), 1)
        strict = col < row
        tri = strict.astype(BF16)

        def block(jb, c, o, masked):
            sl = pl.ds(pl.multiple_of(jb * t, t), t)
            _, lb, lk = _sb_scores(qb, k_ref[sl, :], scale)
            if masked:
                lk = jnp.where(strict, lk, 0.0)
            att = jnp.exp(lb + _dot_hilo(lk, tri) + c)
            if masked:
                att = jnp.where(strict, att, 0.0)
            o = o + _dot(att.astype(BF16), v_ref[sl, :])
            return c + jnp.sum(lk, axis=1, keepdims=True), o

        c, o = block(i, jnp.zeros((t, 1), F32), jnp.zeros((t, dh), F32), True)
        c, o = lax.fori_loop(0, i, lambda it, co: block(i - 1 - it, co[0], co[1], False), (c, o))
        o_ref[...] = o.astype(o_ref.dtype)

    qspec = pl.BlockSpec((None, t, dh), lambda h, i: (h, i, 0))
    kspec = pl.BlockSpec((None, l, dh), lambda h, i: (h, 0, 0))
    return pl.pallas_call(
        body, name=name, grid=(h_n, l // t), in_specs=[qspec, kspec, kspec], out_specs=qspec,
        out_shape=jax.ShapeDtypeStruct((h_n, l, dh), BF16), compiler_params=_cparams(2),
    )(q, k, v)


def _sb_bwd(q, k, v, do, *, name):
    h_n, l, dh = q.shape
    t = _tile(l, SB_TILE)
    nq = l // t
    scale = dh ** -0.5

    def body(q_ref, k_ref, v_ref, do_ref, dq_ref, dk_ref, dv_ref, e_scr, s_scr):
        i = pl.program_id(1)

        @pl.when(i == 0)
        def _():
            dk_ref[...] = jnp.zeros_like(dk_ref)
            dv_ref[...] = jnp.zeros_like(dv_ref)

        qb, dob = q_ref[...], do_ref[...]
        row = lax.broadcasted_iota(jnp.int32, (t, t), 0)
        col = lax.broadcasted_iota(jnp.int32, (t, t), 1)
        strict = col < row
        tri_suffix = strict.astype(BF16)
        tri_prefix = (row < col).astype(BF16)

        def sweep1(jb, c, masked):
            sl = pl.ds(pl.multiple_of(jb * t, t), t)
            _, lb, lk = _sb_scores(qb, k_ref[sl, :], scale)
            if masked:
                lk = jnp.where(strict, lk, 0.0)
            att = jnp.exp(lb + _dot_hilo(lk, tri_suffix) + c)
            if masked:
                att = jnp.where(strict, att, 0.0)
            de = att * _dot(dob, v_ref[sl, :], NT)
            dv_ref[sl, :] += _dot(att.astype(BF16), dob, TN)
            e_scr[jb] = de
            s_scr[jb] = jnp.exp(lb)
            return c + jnp.sum(lk, axis=1, keepdims=True)

        c = sweep1(i, jnp.zeros((t, 1), F32), True)
        lax.fori_loop(0, i, lambda it, cc: sweep1(i - 1 - it, cc, False), c)

        def sweep2(jb, p, dq, masked):
            sl = pl.ds(pl.multiple_of(jb * t, t), t)
            de, sg = e_scr[jb], s_scr[jb]
            dlk = p + _dot_hilo(de, tri_prefix)
            if masked:
                dlk = jnp.where(strict, dlk, 0.0)
            dz = ((de * (1.0 - sg) - dlk * sg) * scale).astype(BF16)
            dq = dq + _dot(dz, k_ref[sl, :])
            dk_ref[sl, :] += _dot(dz, qb, TN)
            return p + jnp.sum(de, axis=1, keepdims=True), dq

        p, dq = lax.fori_loop(0, i, lambda jb, pd: sweep2(jb, pd[0], pd[1], False),
                              (jnp.zeros((t, 1), F32), jnp.zeros((t, dh), F32)))
        _, dq = sweep2(i, p, dq, True)
        dq_ref[...] = dq.astype(dq_ref.dtype)

    qspec = pl.BlockSpec((None, t, dh), lambda h, i: (h, i, 0))
    kspec = pl.BlockSpec((None, l, dh), lambda h, i: (h, 0, 0))
    return pl.pallas_call(
        body, name=name, grid=(h_n, nq), in_specs=[qspec, kspec, kspec, qspec],
        out_specs=[qspec, kspec, kspec],
        out_shape=[jax.ShapeDtypeStruct((h_n, l, dh), BF16), jax.ShapeDtypeStruct((h_n, l, dh), F32),
                   jax.ShapeDtypeStruct((h_n, l, dh), F32)],
        scratch_shapes=[pltpu.VMEM((nq, t, t), F32), pltpu.VMEM((nq, t, t), F32)],
        compiler_params=_cparams(2),
    )(q, k, v, do)


def _to_heads(x, n):
    l = x.shape[0]
    return x.reshape(l, n, -1, SB_HEAD_DIM).transpose(1, 2, 0, 3)


def _from_heads(x):
    return x.transpose(1, 0, 2).reshape(x.shape[1], -1)


def _sb_layer_fwd(x, g, wqkv, wo, tag):
    h = _rmsnorm(x, g, name=f"{tag}_norm")
    qkv = _mm(h, wqkv, tm=512, tn=1024, tk=1024, name=f"{tag}_qkv", out_dtype=BF16)
    qkv_h = _to_heads(qkv, 3)
    o_h = _sb_fwd(qkv_h[0], qkv_h[1], qkv_h[2], name=f"{tag}_attn")
    o = _from_heads(o_h)
    xo = _mm(o, wo, tm=512, tn=1024, tk=1024, name=f"{tag}_out", extras=[(x, "tile")],
             epilogue=lambda acc, xt: (xt + acc,))
    return xo, (x, h, qkv_h, o)


def _sb_layer_bwd(dout, saved, g, wqkv, wo, tag):
    x, h, qkv_h, o = saved
    do = _mm(dout, wo, tb=True, tm=512, tn=1024, tk=1024, name=f"{tag}_do", out_dtype=BF16)
    dwo = _mm(o, dout, ta=True, tm=1024, tn=1024, tk=512, name=f"{tag}_dwo", out_dtype=BF16)
    dq, dk, dv = _sb_bwd(qkv_h[0], qkv_h[1], qkv_h[2], _to_heads(do, 1)[0], name=f"{tag}_attn_bwd")
    dqkv = jnp.concatenate([_from_heads(dq), _from_heads(dk.astype(BF16)), _from_heads(dv.astype(BF16))], axis=1)
    dwqkv = _mm(h, dqkv, ta=True, tm=1024, tn=1024, tk=512, name=f"{tag}_dwqkv", out_dtype=BF16)
    dx, dg = _mm(dqkv, wqkv, tb=True, tm=256, tn=1024, tk=1024, name=f"{tag}_dx",
                 extras=[(x, "tile"), (g, "row"), (dout, "tile")], outs=[(F32, "tile"), (F32, "colsum")],
                 epilogue=_norm_bwd_epilogue)
    return dx, dg, dwqkv, dwo


def _shift_down(x, s, t_idx):
    return jnp.where(t_idx >= s, pltpu.roll(x, s, 0), 0.0)


def _shift_up(x, s, t_idx):
    n = x.shape[0]
    return jnp.where(t_idx < n - s, pltpu.roll(x, n - s, 0), 0.0)


def _sc_fwd(p, cw, *, name):
    l = p.shape[0]
    d = cw.shape[1]
    tc = 128
    nb = d // tc

    def body(b_ref, c_ref, h_ref, w_ref, o_ref):
        v = c_ref[...] * h_ref[...]
        t_idx = lax.broadcasted_iota(jnp.int32, v.shape, 0)
        u = v * w_ref[2:3, :] + _shift_down(v, 1, t_idx) * w_ref[1:2, :] + _shift_down(v, 2, t_idx) * w_ref[0:1, :]
        o_ref[...] = (b_ref[...] * u).astype(BF16)

    return pl.pallas_call(
        body, name=name, grid=(nb,),
        in_specs=[pl.BlockSpec((l, tc), lambda j: (0, j)), pl.BlockSpec((l, tc), lambda j: (0, nb + j)),
                  pl.BlockSpec((l, tc), lambda j: (0, 2 * nb + j)), pl.BlockSpec((3, tc), lambda j: (0, j))],
        out_specs=pl.BlockSpec((l, tc), lambda j: (0, j)),
        out_shape=jax.ShapeDtypeStruct((l, d), BF16), compiler_params=_cparams(1),
    )(p, p, p, cw)


def _sc_bwd(p, cw, dbu, *, name):
    l = p.shape[0]
    d = cw.shape[1]
    tc = 128
    nb = d // tc

    def body(b_ref, c_ref, h_ref, w_ref, g_ref, db_ref, dc_ref, dh_ref, dw_ref):
        cv, hv = c_ref[...], h_ref[...]
        v = cv * hv
        t_idx = lax.broadcasted_iota(jnp.int32, v.shape, 0)
        v1, v2 = _shift_down(v, 1, t_idx), _shift_down(v, 2, t_idx)
        u = v * w_ref[2:3, :] + v1 * w_ref[1:2, :] + v2 * w_ref[0:1, :]
        dbu_v = g_ref[...]
        db_ref[...] = (dbu_v * u).astype(BF16)
        du = dbu_v * b_ref[...]
        dv = du * w_ref[2:3, :] + _shift_up(du, 1, t_idx) * w_ref[1:2, :] + _shift_up(du, 2, t_idx) * w_ref[0:1, :]
        dc_ref[...] = (dv * hv).astype(BF16)
        dh_ref[...] = (dv * cv).astype(BF16)
        dw_ref[...] = jnp.zeros_like(dw_ref)
        dw_ref[0:1, :] = jnp.sum(du * v2, axis=0, keepdims=True)
        dw_ref[1:2, :] = jnp.sum(du * v1, axis=0, keepdims=True)
        dw_ref[2:3, :] = jnp.sum(du * v, axis=0, keepdims=True)

    col = lambda off: pl.BlockSpec((l, tc), lambda j: (0, off + j))
    return pl.pallas_call(
        body, name=name, grid=(nb,),
        in_specs=[col(0), col(nb), col(2 * nb), pl.BlockSpec((3, tc), lambda j: (0, j)), col(0)],
        out_specs=[col(0), col(0), col(0), pl.BlockSpec((8, tc), lambda j: (0, j))],
        out_shape=[jax.ShapeDtypeStruct((l, d), BF16)] * 3 + [jax.ShapeDtypeStruct((8, d), F32)],
        compiler_params=_cparams(1),
    )(p, p, p, cw, dbu)


def _sc_layer_fwd(x, g, win, cw, wout, tag):
    h = _rmsnorm(x, g, name=f"{tag}_norm")
    p = _mm(h, win, tm=512, tn=1024, tk=1024, name=f"{tag}_in")
    bu = _sc_fwd(p, cw, name=f"{tag}_conv")
    xo = _mm(bu, wout, tm=512, tn=1024, tk=1024, name=f"{tag}_out", extras=[(x, "tile")],
             epilogue=lambda acc, xt: (xt + acc,))
    return xo, (x, h, p, bu)


def _sc_layer_bwd(dout, saved, g, win, cw, wout, tag):
    x, h, p, bu = saved
    dbu = _mm(dout, wout, tb=True, tm=512, tn=1024, tk=1024, name=f"{tag}_dbu")
    dwout = _mm(bu, dout, ta=True, tm=1024, tn=1024, tk=512, name=f"{tag}_dwout", out_dtype=BF16)
    db, dc, dh, dcw = _sc_bwd(p, cw, dbu, name=f"{tag}_conv_bwd")
    dp = jnp.concatenate([db, dc, dh], axis=1)
    dwin = _mm(h, dp, ta=True, tm=1024, tn=1024, tk=512, name=f"{tag}_dwin", out_dtype=BF16)
    dx, dg = _mm(dp, win, tb=True, tm=256, tn=1024, tk=1024, name=f"{tag}_dx",
                 extras=[(x, "tile"), (g, "row"), (dout, "tile")], outs=[(F32, "tile"), (F32, "colsum")],
                 epilogue=_norm_bwd_epilogue)
    return dx, dg, dwin, dcw[:3], dwout


def _ssd_conv_fwd(p, cw, cb, *, name):
    l = p.shape[0]
    tc = 128
    nb = SSD_CONV_DIM // tc
    off = SSD_D_INNER // tc

    def body(x_ref, w_ref, b_ref, o_ref):
        xv = x_ref[...]
        t_idx = lax.broadcasted_iota(jnp.int32, xv.shape, 0)
        pre = xv * w_ref[3:4, :] + b_ref[...]
        for s in (1, 2, 3):
            pre = pre + _shift_down(xv, s, t_idx) * w_ref[3 - s:4 - s, :]
        o_ref[...] = pre * _sigmoid(pre)

    return pl.pallas_call(
        body, name=name, grid=(nb,),
        in_specs=[pl.BlockSpec((l, tc), lambda j: (0, off + j)), pl.BlockSpec((4, tc), lambda j: (0, j)),
                  pl.BlockSpec((1, tc), lambda j: (0, j))],
        out_specs=pl.BlockSpec((l, tc), lambda j: (0, j)),
        out_shape=jax.ShapeDtypeStruct((l, SSD_CONV_DIM), F32), compiler_params=_cparams(1),
    )(p, cw, cb)


def _ssd_conv_bwd(p, cw, cb, dact, *, name):
    l = p.shape[0]
    tc = 128
    nb = SSD_CONV_DIM // tc
    off = SSD_D_INNER // tc

    def body(x_ref, w_ref, b_ref, g_ref, dx_ref, dw_ref):
        xv = x_ref[...]
        t_idx = lax.broadcasted_iota(jnp.int32, xv.shape, 0)
        xs = [xv] + [_shift_down(xv, s, t_idx) for s in (1, 2, 3)]
        pre = b_ref[...] + xs[0] * w_ref[3:4, :]
        for s in (1, 2, 3):
            pre = pre + xs[s] * w_ref[3 - s:4 - s, :]
        sg = _sigmoid(pre)
        dpre = g_ref[...] * sg * (1.0 + pre * (1.0 - sg))
        dx = dpre * w_ref[3:4, :]
        for s in (1, 2, 3):
            dx = dx + _shift_up(dpre, s, t_idx) * w_ref[3 - s:4 - s, :]
        dx_ref[...] = dx
        dw_ref[...] = jnp.zeros_like(dw_ref)
        for s in (0, 1, 2, 3):
            dw_ref[3 - s:4 - s, :] = jnp.sum(dpre * xs[s], axis=0, keepdims=True)
        dw_ref[4:5, :] = jnp.sum(dpre, axis=0, keepdims=True)

    return pl.pallas_call(
        body, name=name, grid=(nb,),
        in_specs=[pl.BlockSpec((l, tc), lambda j: (0, off + j)), pl.BlockSpec((4, tc), lambda j: (0, j)),
                  pl.BlockSpec((1, tc), lambda j: (0, j)), pl.BlockSpec((l, tc), lambda j: (0, j))],
        out_specs=[pl.BlockSpec((l, tc), lambda j: (0, j)), pl.BlockSpec((8, tc), lambda j: (0, j))],
        out_shape=[jax.ShapeDtypeStruct((l, SSD_CONV_DIM), F32), jax.ShapeDtypeStruct((8, SSD_CONV_DIM), F32)],
        compiler_params=_cparams(1),
    )(p, cw, cb, dact)


def _ssd_dt_fwd(p, bias, *, name):
    l = p.shape[0]
    tm = _tile(l, 1024)
    off = (SSD_D_INNER + SSD_CONV_DIM) // 128

    def body(x_ref, b_ref, o_ref):
        v = x_ref[...] + b_ref[...]
        o_ref[...] = jnp.maximum(v, 0.0) + jnp.log(1.0 + jnp.exp(-jnp.abs(v)))

    return pl.pallas_call(
        body, name=name, grid=(l // tm,),
        in_specs=[pl.BlockSpec((tm, 128), lambda i: (i, off)), pl.BlockSpec((1, 128), lambda i: (0, 0))],
        out_specs=pl.BlockSpec((tm, 128), lambda i: (i, 0)),
        out_shape=jax.ShapeDtypeStruct((l, 128), F32), compiler_params=_cparams(1),
    )(p, bias)


def _ssd_dt_bwd(p, bias, ddt, *, name):
    l = p.shape[0]
    tm = _tile(l, 1024)
    off = (SSD_D_INNER + SSD_CONV_DIM) // 128

    def body(x_ref, b_ref, g_ref, o_ref, db_ref):
        i = pl.program_id(0)
        d = g_ref[...] * _sigmoid(x_ref[...] + b_ref[...])
        o_ref[...] = d
        _accumulate(db_ref, jnp.sum(d, axis=0, keepdims=True), i == 0)

    return pl.pallas_call(
        body, name=name, grid=(l // tm,),
        in_specs=[pl.BlockSpec((tm, 128), lambda i: (i, off)), pl.BlockSpec((1, 128), lambda i: (0, 0)),
                  pl.BlockSpec((tm, 128), lambda i: (i, 0))],
        out_specs=[pl.BlockSpec((tm, 128), lambda i: (i, 0)), pl.BlockSpec((1, 128), lambda i: (0, 0))],
        out_shape=[jax.ShapeDtypeStruct((l, 128), F32), jax.ShapeDtypeStruct((1, 128), F32)],
        compiler_params=_cparams(1),
    )(p, bias, ddt)


def _row_to_col(r, eye):
    return jnp.sum(jnp.where(eye, r, 0.0), axis=1, keepdims=True)


def _col_to_row(c, eye):
    return jnp.sum(jnp.where(eye, c, 0.0), axis=0, keepdims=True)


def _ssd_chunk_common(b_ref, c_ref, dt_ref, a_ref, lam_scr):
    n = SSD_CHUNK
    row = lax.broadcasted_iota(jnp.int32, (n, n), 0)
    col = lax.broadcasted_iota(jnp.int32, (n, n), 1)
    bm, cm = b_ref[...].astype(BF16), c_ref[...].astype(BF16)
    g = _dot(cm, bm, NT)
    incl = (row <= col).astype(BF16)
    lam_scr[...] = _dot_exact(dt_ref[...] * a_ref[...], incl)
    return row, col, bm, cm, g


def _ssd_head_common(r, row, col, dt_ref, lam_scr):
    eye, tril = row == col, row >= col
    lam_r = lam_scr[r:r + 1, :]
    dt_r = dt_ref[r:r + 1, :]
    lam_c = _row_to_col(lam_r, eye)
    dt_c = _row_to_col(dt_r, eye)
    dk = jnp.where(tril, jnp.exp(jnp.minimum(lam_c - lam_r, 0.0)), 0.0)
    lam_last = jnp.sum(jnp.where(col[0:1, :] == SSD_CHUNK - 1, lam_r, 0.0), axis=1, keepdims=True)
    return eye, lam_r, dt_r, lam_c, dt_c, dk, lam_last


def _ssd_fwd(xh, act, dt_t, a_b, *, name):
    l = xh.shape[1]
    nc = l // SSD_CHUNK
    n, p_dim, hpg = SSD_CHUNK, SSD_HEAD_DIM, SSD_HPG

    def body(x_ref, b_ref, c_ref, dt_ref, a_ref, y_ref, hp_ref, h_scr, lam_scr):
        @pl.when(pl.program_id(1) == 0)
        def _():
            h_scr[...] = jnp.zeros_like(h_scr)

        row, col, bm, cm, g = _ssd_chunk_common(b_ref, c_ref, dt_ref, a_ref, lam_scr)
        for r in range(hpg):
            _, _, dt_r, lam_c, dt_c, dk, lam_last = _ssd_head_common(r, row, col, dt_ref, lam_scr)
            xr = x_ref[r]
            hr = h_scr[r]
            w = (g * dk * dt_r).astype(BF16)
            y = _dot(w, xr.astype(BF16)) + _dot(cm, hr.astype(BF16), NT) * jnp.exp(lam_c)
            y_ref[r] = y
            hp_ref[r] = hr
            xw = (xr * (jnp.exp(lam_last - lam_c) * dt_c)).astype(BF16)
            h_scr[r] = jnp.exp(lam_last) * hr + _dot(xw, bm, TN)

    g_off = SSD_D_INNER // SSD_STATE
    return pl.pallas_call(
        body, name=name, grid=(SSD_GROUPS, nc),
        in_specs=[pl.BlockSpec((hpg, n, p_dim), lambda g, c: (g, c, 0)),
                  pl.BlockSpec((n, SSD_STATE), lambda g, c: (c, g_off + g)),
                  pl.BlockSpec((n, SSD_STATE), lambda g, c: (c, g_off + SSD_GROUPS + g)),
                  pl.BlockSpec((None, 8, n), lambda g, c: (g, 0, c)),
                  pl.BlockSpec((None, 8, 128), lambda g, c: (g, 0, 0))],
        out_specs=[pl.BlockSpec((hpg, n, p_dim), lambda g, c: (g, c, 0)),
                   pl.BlockSpec((None, hpg, p_dim, SSD_STATE), lambda g, c: (c, g, 0, 0))],
        out_shape=[jax.ShapeDtypeStruct(xh.shape, F32),
                   jax.ShapeDtypeStruct((nc, SSD_HEADS, p_dim, SSD_STATE), F32)],
        scratch_shapes=[pltpu.VMEM((hpg, p_dim, SSD_STATE), F32), pltpu.VMEM((8, n), F32)],
        compiler_params=_cparams(2),
    )(xh, act, act, dt_t, a_b)


def _ssd_bwd(xh, act, dt_t, a_b, hprev, dyh, *, name):
    l = xh.shape[1]
    nc = l // SSD_CHUNK
    n, p_dim, hpg = SSD_CHUNK, SSD_HEAD_DIM, SSD_HPG

    def body(x_ref, b_ref, c_ref, dt_ref, a_ref, hp_ref, dy_ref,
             dx_ref, db_ref, dc_ref, ddt_ref, da_ref, dh_scr, lam_scr, dlam_scr, ddt_scr):
        ci = pl.program_id(1)

        @pl.when(ci == 0)
        def _():
            dh_scr[...] = jnp.zeros_like(dh_scr)

        row, col, bm, cm, g = _ssd_chunk_common(b_ref, c_ref, dt_ref, a_ref, lam_scr)
        dlam_scr[...] = jnp.zeros_like(dlam_scr)
        ddt_scr[...] = jnp.zeros_like(ddt_scr)
        dg_acc = jnp.zeros((n, n), F32)
        dc_acc = jnp.zeros((n, SSD_STATE), F32)
        db_acc = jnp.zeros((n, SSD_STATE), F32)
        for r in range(hpg):
            eye, _, dt_r, lam_c, dt_c, dk, lam_last = _ssd_head_common(r, row, col, dt_ref, lam_scr)
            xr, dyr, hr, dhr = x_ref[r], dy_ref[r], hp_ref[r], dh_scr[r]
            xb, dyb, hb, dhb = xr.astype(BF16), dyr.astype(BF16), hr.astype(BF16), dhr.astype(BF16)
            e_l = jnp.exp(lam_c)
            e_last = jnp.exp(lam_last)
            decay_c = jnp.exp(lam_last - lam_c)
            w_c = decay_c * dt_c
            m = g * dk * dt_r
            dm = _dot(dyb, xb, NT)
            bdh = _dot(bm, dhb, NT)
            dx_ref[r] = _dot(m.astype(BF16), dyb, TN) + w_c * bdh
            dg_acc = dg_acc + dm * dk * dt_r
            q_mat = dm * g * dk
            p_mat = q_mat * dt_r
            yoff = _dot(cm, hb, NT) * e_l
            q_c = jnp.sum(xr * bdh, axis=1, keepdims=True)
            dlam_c = (jnp.sum(p_mat, axis=1, keepdims=True) + jnp.sum(dyr * yoff, axis=1, keepdims=True)
                      - w_c * q_c)
            d_last = (jnp.sum(w_c * q_c, axis=0, keepdims=True)
                      + e_last * jnp.sum(jnp.sum(dhr * hr, axis=1, keepdims=True), axis=0, keepdims=True))
            dlam_scr[r:r + 1, :] = (_col_to_row(dlam_c, eye) - jnp.sum(p_mat, axis=0, keepdims=True)
                                    + jnp.where(col[0:1, :] == n - 1, d_last, 0.0))
            ddt_scr[r:r + 1, :] = jnp.sum(q_mat, axis=0, keepdims=True) + _col_to_row(decay_c * q_c, eye)
            dc_acc = dc_acc + e_l * _dot(dyb, hb)
            db_acc = db_acc + _dot((xr * w_c).astype(BF16), dhb)
            dh_scr[r] = e_last * dhr + _dot((dyr * e_l).astype(BF16), cm, TN)

        dgb = dg_acc.astype(BF16)
        dc_ref[...] = _dot(dgb, bm) + dc_acc
        db_ref[...] = _dot(dgb, cm, TN) + db_acc
        rev = (row >= col).astype(BF16)
        da = _dot_exact(dlam_scr[...], rev)
        ddt_ref[...] = ddt_scr[...] + da * a_ref[...]
        _accumulate(da_ref, da * dt_ref[...], ci == 0)

        @pl.when(ci == nc - 1)
        def _():
            da_ref[...] = jnp.broadcast_to(jnp.sum(da_ref[...], axis=1, keepdims=True), da_ref.shape)

    g_off = SSD_D_INNER // SSD_STATE
    rc = lambda c: nc - 1 - c
    hspec = pl.BlockSpec((hpg, n, p_dim), lambda g, c: (g, rc(c), 0))
    gspec = pl.BlockSpec((n, SSD_STATE), lambda g, c: (rc(c), g))
    return pl.pallas_call(
        body, name=name, grid=(SSD_GROUPS, nc),
        in_specs=[hspec,
                  pl.BlockSpec((n, SSD_STATE), lambda g, c: (rc(c), g_off + g)),
                  pl.BlockSpec((n, SSD_STATE), lambda g, c: (rc(c), g_off + SSD_GROUPS + g)),
                  pl.BlockSpec((None, 8, n), lambda g, c: (g, 0, rc(c))),
                  pl.BlockSpec((None, 8, 128), lambda g, c: (g, 0, 0)),
                  pl.BlockSpec((None, hpg, p_dim, SSD_STATE), lambda g, c: (rc(c), g, 0, 0)),
                  hspec],
        out_specs=[hspec, gspec, gspec,
                   pl.BlockSpec((None, 8, n), lambda g, c: (g, 0, rc(c))),
                   pl.BlockSpec((None, 8, 128), lambda g, c: (g, 0, 0))],
        out_shape=[jax.ShapeDtypeStruct(xh.shape, F32),
                   jax.ShapeDtypeStruct((l, SSD_GROUPS * SSD_STATE), F32),
                   jax.ShapeDtypeStruct((l, SSD_GROUPS * SSD_STATE), F32),
                   jax.ShapeDtypeStruct(dt_t.shape, F32),
                   jax.ShapeDtypeStruct(a_b.shape, F32)],
        scratch_shapes=[pltpu.VMEM((hpg, p_dim, SSD_STATE), F32), pltpu.VMEM((8, n), F32),
                        pltpu.VMEM((8, n), F32), pltpu.VMEM((8, n), F32)],
        compiler_params=_cparams(2),
    )(xh, act, act, dt_t, a_b, hprev, dyh)


def _ssd_gate_fwd(y, act, p, d_vec, gn, *, name):
    l = y.shape[0]
    w = SSD_D_INNER
    tm = _tile(l, 256)

    def body(y_ref, xs_ref, z_ref, d_ref, g_ref, o_ref):
        for gi in range(SSD_GROUPS):
            sl = slice(gi * SSD_NORM_GROUP, (gi + 1) * SSD_NORM_GROUP)
            z = z_ref[:, sl]
            y2 = (y_ref[:, sl] + d_ref[:, sl] * xs_ref[:, sl]) * (z * _sigmoid(z))
            r = lax.rsqrt(jnp.mean(y2 * y2, axis=1, keepdims=True) + RMS_EPS)
            o_ref[:, sl] = (y2 * r * g_ref[:, sl]).astype(BF16)

    rows = pl.BlockSpec((tm, w), lambda i: (i, 0))
    vec = pl.BlockSpec((1, w), lambda i: (0, 0))
    return pl.pallas_call(
        body, name=name, grid=(l // tm,), in_specs=[rows, rows, rows, vec, vec], out_specs=rows,
        out_shape=jax.ShapeDtypeStruct((l, w), BF16), compiler_params=_cparams(1),
    )(y, act, p, d_vec, gn)


def _ssd_gate_bwd(dyn, y, act, p, d_vec, gn, *, name):
    l = y.shape[0]
    w = SSD_D_INNER
    tm = _tile(l, 256)

    def body(dyn_ref, y_ref, xs_ref, z_ref, d_ref, g_ref, dy_ref, dz_ref, dxs_ref, dd_ref, dg_ref):
        i = pl.program_id(0)
        for gi in range(SSD_GROUPS):
            sl = slice(gi * SSD_NORM_GROUP, (gi + 1) * SSD_NORM_GROUP)
            z, xs, dv = z_ref[:, sl], xs_ref[:, sl], d_ref[:, sl]
            s = _sigmoid(z)
            sz = z * s
            y1 = y_ref[:, sl] + dv * xs
            y2 = y1 * sz
            r = lax.rsqrt(jnp.mean(y2 * y2, axis=1, keepdims=True) + RMS_EPS)
            y2h = y2 * r
            dyn_v = dyn_ref[:, sl]
            d2h = dyn_v * g_ref[:, sl]
            dy2 = r * (d2h - y2h * jnp.mean(d2h * y2h, axis=1, keepdims=True))
            dy1 = dy2 * sz
            dy_ref[:, sl] = dy1
            dz_ref[:, sl] = dy2 * y1 * s * (1.0 + z * (1.0 - s))
            dxs_ref[:, sl] = dv * dy1
            _accumulate(dd_ref.at[:, sl], jnp.sum(dy1 * xs, axis=0, keepdims=True), i == 0)
            _accumulate(dg_ref.at[:, sl], jnp.sum(dyn_v * y2h, axis=0, keepdims=True), i == 0)

    rows = pl.BlockSpec((tm, w), lambda i: (i, 0))
    vec = pl.BlockSpec((1, w), lambda i: (0, 0))
    return pl.pallas_call(
        body, name=name, grid=(l // tm,), in_specs=[rows, rows, rows, rows, vec, vec],
        out_specs=[rows, rows, rows, vec, vec],
        out_shape=[jax.ShapeDtypeStruct((l, w), F32)] * 3 + [jax.ShapeDtypeStruct((1, w), F32)] * 2,
        compiler_params=_cparams(1),
    )(dyn, y, act, p, d_vec, gn)


def _heads_major(x):
    return x.reshape(x.shape[0], SSD_HEADS, SSD_HEAD_DIM).transpose(1, 0, 2)


def _ssd_layer_fwd(x, g, win, cw, cb, dt_bias, a_log, d_skip, gn, wout, tag):
    l = x.shape[0]
    h = _rmsnorm(x, g, name=f"{tag}_norm")
    p = _mm(h, win, tm=512, tn=896, tk=1024, name=f"{tag}_in")
    act = _ssd_conv_fwd(p, cw, cb, name=f"{tag}_conv")
    bias = jnp.pad(dt_bias, (0, 128 - SSD_HEADS)).reshape(1, 128)
    dt = _ssd_dt_fwd(p, bias, name=f"{tag}_dt")
    xh = _heads_major(act[:, :SSD_D_INNER])
    dt_t = jnp.pad(dt[:, :SSD_HEADS].T.reshape(SSD_GROUPS, SSD_HPG, l), ((0, 0), (0, 8 - SSD_HPG), (0, 0)))
    a = -jnp.exp(a_log).reshape(SSD_GROUPS, SSD_HPG, 1)
    a_b = jnp.broadcast_to(jnp.pad(a, ((0, 0), (0, 8 - SSD_HPG), (0, 0))), (SSD_GROUPS, 8, 128))
    yh, hprev = _ssd_fwd(xh, act, dt_t, a_b, name=f"{tag}_scan")
    y = yh.transpose(1, 0, 2).reshape(l, SSD_D_INNER)
    d_vec = jnp.repeat(d_skip, SSD_HEAD_DIM).reshape(1, SSD_D_INNER)
    yn = _ssd_gate_fwd(y, act, p, d_vec, gn, name=f"{tag}_gate")
    xo = _mm(yn, wout, tm=512, tn=1024, tk=2048, name=f"{tag}_out", extras=[(x, "tile")],
             epilogue=lambda acc, xt: (xt + acc,))
    return xo, (x, h, p, act, bias, xh, dt_t, a_b, hprev, y, d_vec, yn)


def _ssd_layer_bwd(dout, saved, g, win, cw, cb, gn, wout, tag):
    x, h, p, act, bias, xh, dt_t, a_b, hprev, y, d_vec, yn = saved
    l = x.shape[0]
    dyn = _mm(dout, wout, tb=True, tm=512, tn=1024, tk=1024, name=f"{tag}_dyn")
    dwout = _mm(yn, dout, ta=True, tm=1024, tn=1024, tk=512, name=f"{tag}_dwout", out_dtype=BF16)
    dy, dz, dxs_d, dd_vec, dgn = _ssd_gate_bwd(dyn, y, act, p, d_vec, gn, name=f"{tag}_gate_bwd")
    dxh, dbm, dcm, ddt_t, da_b = _ssd_bwd(xh, act, dt_t, a_b, hprev, _heads_major(dy), name=f"{tag}_scan_bwd")
    dxs = dxh.transpose(1, 0, 2).reshape(l, SSD_D_INNER) + dxs_d
    dact = jnp.concatenate([dxs, dbm, dcm], axis=1)
    dxbc, dcw8 = _ssd_conv_bwd(p, cw, cb, dact, name=f"{tag}_conv_bwd")
    ddt = jnp.pad(ddt_t[:, :SSD_HPG, :].reshape(SSD_HEADS, l).T, ((0, 0), (0, 128 - SSD_HEADS)))
    ddt_raw, dbias = _ssd_dt_bwd(p, bias, ddt, name=f"{tag}_dt_bwd")
    dp = jnp.concatenate([dz, dxbc, ddt_raw], axis=1)
    dwin = _mm(h, dp, ta=True, tm=1024, tn=896, tk=512, name=f"{tag}_dwin", out_dtype=BF16)
    dx, dg = _mm(dp, win, tb=True, tm=256, tn=1024, tk=896, name=f"{tag}_dx",
                 extras=[(x, "tile"), (g, "row"), (dout, "tile")], outs=[(F32, "tile"), (F32, "colsum")],
                 epilogue=_norm_bwd_epilogue)
    a_heads = a_b[:, :SSD_HPG, 0].reshape(SSD_HEADS)
    grads = dict(
        ssd_w_in=dwin[:, :SSD_IN_DIM], ssd_conv_w=dcw8[:4], ssd_conv_b=dcw8[4],
        ssd_dt_bias=dbias[0, :SSD_HEADS], ssd_a_log=da_b[:, :SSD_HPG, 0].reshape(SSD_HEADS) * a_heads,
        ssd_d=dd_vec.reshape(SSD_HEADS, SSD_HEAD_DIM).sum(axis=1), ssd_norm=dgn[0], ssd_w_out=dwout)
    return dx, dg, grads


def _local_step(x, tgt, w):
    row = lambda v: v.reshape(1, -1)
    saved = []
    for i in range(DEPTH):
        kind, j = i % 3, i // 3
        x, s1 = _ffn_fwd(x, row(w["ffn1_norm"][i]), w["ffn1_w_gu"][i], w["ffn1_w_down"][i], f"l{i}f1")
        gm = row(w["mix_norm"][i])
        if kind == 0:
            x, sm = _sb_layer_fwd(x, gm, w["sb_w_qkv"][j], w["sb_w_o"][j], f"l{i}sb")
        elif kind == 1:
            x, sm = _ssd_layer_fwd(x, gm, w["ssd_w_in"][j], w["ssd_conv_w"][j], row(w["ssd_conv_b"][j]),
                                   w["ssd_dt_bias"][j], w["ssd_a_log"][j], w["ssd_d"][j], row(w["ssd_norm"][j]),
                                   w["ssd_w_out"][j], f"l{i}ssd")
        else:
            x, sm = _sc_layer_fwd(x, gm, w["sc_w_in"][j], w["sc_conv_w"][j], w["sc_w_out"][j], f"l{i}sc")
        x, s2 = _ffn_fwd(x, row(w["ffn2_norm"][i]), w["ffn2_w_gu"][i], w["ffn2_w_down"][i], f"l{i}f2")
        saved.append((s1, sm, s2))

    loss, dx, dfinal = _final_loss(x, row(w["final_norm"]), tgt, name="final_loss")
    per_layer = {k: [None] * DEPTH for k in ("ffn1_norm", "ffn1_w_gu", "ffn1_w_down", "mix_norm",
                                             "ffn2_norm", "ffn2_w_gu", "ffn2_w_down")}
    sb_g = {"sb_w_qkv": [None, None], "sb_w_o": [None, None]}
    grads = {"final_norm": dfinal[0]}
    for i in reversed(range(DEPTH)):
        kind, j = i % 3, i // 3
        s1, sm, s2 = saved[i]
        dx, dg, dwgu, dwd = _ffn_bwd(dx, s2, row(w["ffn2_norm"][i]), w["ffn2_w_gu"][i], w["ffn2_w_down"][i], f"l{i}f2")
        per_layer["ffn2_norm"][i], per_layer["ffn2_w_gu"][i], per_layer["ffn2_w_down"][i] = dg[0], dwgu, dwd
        gm = row(w["mix_norm"][i])
        if kind == 0:
            dx, dg, dwqkv, dwo = _sb_layer_bwd(dx, sm, gm, w["sb_w_qkv"][j], w["sb_w_o"][j], f"l{i}sb")
            sb_g["sb_w_qkv"][j], sb_g["sb_w_o"][j] = dwqkv, dwo
        elif kind == 1:
            dx, dg, sg = _ssd_layer_bwd(dx, sm, gm, w["ssd_w_in"][j], w["ssd_conv_w"][j], row(w["ssd_conv_b"][j]),
                                        row(w["ssd_norm"][j]), w["ssd_w_out"][j], f"l{i}ssd")
            grads.update({k: v[None] for k, v in sg.items()})
        else:
            dx, dg, dwin, dcw, dwout = _sc_layer_bwd(dx, sm, gm, w["sc_w_in"][j], w["sc_conv_w"][j],
                                                     w["sc_w_out"][j], f"l{i}sc")
            grads.update(sc_w_in=dwin[None], sc_conv_w=dcw[None], sc_w_out=dwout[None])
        per_layer["mix_norm"][i] = dg[0]
        dx, dg, dwgu, dwd = _ffn_bwd(dx, s1, row(w["ffn1_norm"][i]), w["ffn1_w_gu"][i], w["ffn1_w_down"][i], f"l{i}f1")
        per_layer["ffn1_norm"][i], per_layer["ffn1_w_gu"][i], per_layer["ffn1_w_down"][i] = dg[0], dwgu, dwd
    for k, v in {**per_layer, **sb_g}.items():
        grads[k] = jnp.stack(v)
    return loss, dx, grads


_HBM = pl.BlockSpec(memory_space=pltpu.HBM)
_MASKS = tuple((m >> 2 & 1, m >> 1 & 1, m & 1) for m in range(1, N_DEV))


def _flip(v, bit):
    return 1 - v if bit else v


def _exchange(x, *, scatter, name):
    shape = x.shape[1:] if scatter else x.shape

    def body(x_ref, o_ref, send_sems, recv_sems, local_sem):
        mx, my, mc = lax.axis_index("x"), lax.axis_index("y"), lax.axis_index("c")
        me = 4 * mx + 2 * my + mc

        def src(idx):
            return x_ref.at[idx] if scatter else x_ref

        local = pltpu.make_async_copy(src(me), o_ref.at[me], local_sem)
        local.start()
        sends, recvs = [], []
        for k, (bx, by, bc) in enumerate(_MASKS):
            px, py, pc = _flip(mx, bx), _flip(my, by), _flip(mc, bc)
            peer = 4 * px + 2 * py + pc
            common = dict(send_sem=send_sems.at[k], recv_sem=recv_sems.at[k], device_id=(px, py, pc),
                          device_id_type=pl.DeviceIdType.MESH)
            sends.append(pltpu.make_async_remote_copy(src_ref=src(peer), dst_ref=o_ref.at[me], **common))
            recvs.append(pltpu.make_async_remote_copy(src_ref=src(peer), dst_ref=o_ref.at[peer], **common))
        for cp in sends:
            cp.start()
        for cp in recvs:
            cp.wait_recv()
        for cp in sends:
            cp.wait_send()
        local.wait()

    return pl.pallas_call(
        body, name=name, in_specs=[_HBM], out_specs=_HBM,
        out_shape=jax.ShapeDtypeStruct((N_DEV,) + shape, x.dtype),
        scratch_shapes=[pltpu.SemaphoreType.DMA((N_DEV - 1,)), pltpu.SemaphoreType.DMA((N_DEV - 1,)),
                        pltpu.SemaphoreType.DMA],
    )(x)


def _adamw_reduce(parts, w, m, v, *, name):
    r, c = w.shape
    tr = _tile(r, 256)
    bc1 = 1.0 - ADAM_B1 ** ADAM_STEP
    bc2 = 1.0 - ADAM_B2 ** ADAM_STEP

    def body(p_ref, w_ref, m_ref, v_ref, g_ref, d_ref, nm_ref, nv_ref):
        g = p_ref[0].astype(F32)
        for q in range(1, N_DEV):
            g = g + p_ref[q].astype(F32)
        nm = ADAM_B1 * m_ref[...] + (1.0 - ADAM_B1) * g
        nv = ADAM_B2 * v_ref[...] + (1.0 - ADAM_B2) * (g * g)
        g_ref[...] = g
        nm_ref[...] = nm
        nv_ref[...] = nv
        d_ref[...] = -ADAM_LR * ((nm / bc1) / (jnp.sqrt(nv / bc2) + ADAM_EPS) + ADAM_WD * w_ref[...])

    blk = pl.BlockSpec((tr, c), lambda i: (i, 0))
    return pl.pallas_call(
        body, name=name, grid=(r // tr,),
        in_specs=[pl.BlockSpec((N_DEV, tr, c), lambda i: (0, i, 0)), blk, blk, blk], out_specs=[blk] * 4,
        out_shape=[jax.ShapeDtypeStruct((r, c), F32)] * 4, compiler_params=_cparams(1),
    )(parts, w, m, v)


def _col_full(g):
    return g.transpose(1, 2, 0, 3).reshape(g.shape[1], g.shape[2], -1)


def _col_parts(f):
    n, k, c8 = f.shape
    return f.reshape(n, k, N_DEV, c8 // N_DEV).transpose(2, 0, 1, 3)


def _row_full(g):
    return g.transpose(1, 0, 2, 3).reshape(g.shape[1], -1, g.shape[3])


def _row_parts(f):
    n, r8, c = f.shape
    return f.reshape(n, N_DEV, r8 // N_DEV, c).transpose(1, 0, 2, 3)


def _gu_full(g):
    n, d, c = g.shape[1:]
    return g.reshape(2, 4, n, d, c).transpose(2, 0, 3, 1, 4).reshape(n, 2, d, 4 * c)


def _gu_parts(f):
    n, _, d, c4 = f.shape
    return f.reshape(n, 2, d, 4, c4 // 4).transpose(1, 3, 0, 2, 4).reshape(N_DEV, n, d, c4 // 4)


def _ssd_in_full(g):
    return jnp.pad(_col_full(g), ((0, 0), (0, 0), (0, SSD_IN_PAD - SSD_IN_DIM)))


_MATMUL_WEIGHTS = (
    ("ffn1_w_gu", _gu_full, _gu_parts), ("ffn1_w_down", _row_full, _row_parts),
    ("ffn2_w_gu", _gu_full, _gu_parts), ("ffn2_w_down", _row_full, _row_parts),
    ("sb_w_qkv", _col_full, _col_parts), ("sb_w_o", _row_full, _row_parts),
    ("ssd_w_in", _ssd_in_full, _col_parts), ("ssd_w_out", _row_full, _row_parts),
    ("sc_w_in", _col_full, _col_parts), ("sc_w_out", _row_full, _row_parts),
)
_CONV_WEIGHTS = ("ssd_conv_w", "sc_conv_w")
_REPLICATED = ("ffn1_norm", "mix_norm", "ffn2_norm", "final_norm", "ssd_conv_b", "ssd_norm",
               "ssd_dt_bias", "ssd_a_log", "ssd_d")
_ORDER = ("ffn1_norm", "ffn1_w_gu", "ffn1_w_down", "mix_norm", "ffn2_norm", "ffn2_w_gu", "ffn2_w_down",
          "sb_w_qkv", "sb_w_o", "ssd_w_in", "ssd_conv_w", "ssd_conv_b", "ssd_dt_bias", "ssd_a_log", "ssd_d",
          "ssd_norm", "ssd_w_out", "sc_w_in", "sc_conv_w", "sc_w_out", "final_norm")
_LANES = 1024


def _rows_of(a):
    flat = a.reshape(-1)
    pad = -flat.shape[0] % _LANES
    return jnp.pad(flat, (0, pad)).reshape(-1, _LANES)


def _pack_rows(arrays, mult):
    rows = [_rows_of(a) for a in arrays]
    packed = jnp.concatenate(rows, axis=0)
    pad = -packed.shape[0] % mult
    return jnp.pad(packed, ((0, pad), (0, 0))), [r.shape[0] for r in rows]


def _unpack_rows(packed, counts, shapes, lead=()):
    out, off = [], 0
    for n, shp in zip(counts, shapes):
        size = math.prod(shp)
        seg = packed[..., off:off + n, :].reshape(lead + (n * _LANES,))[..., :size]
        out.append(seg.reshape(lead + tuple(shp)))
        off += n
    return out


def kernel(x, ffn1_norm, ffn1_w_gu, ffn1_w_down, mix_norm, ffn2_norm, ffn2_w_gu, ffn2_w_down, sb_w_qkv, sb_w_o, ssd_w_in, ssd_conv_w, ssd_conv_b, ssd_dt_bias, ssd_a_log, ssd_d, ssd_norm, ssd_w_out, sc_w_in, sc_conv_w, sc_w_out, final_norm, loss_target, m_ffn1_norm, m_ffn1_w_gu, m_ffn1_w_down, m_mix_norm, m_ffn2_norm, m_ffn2_w_gu, m_ffn2_w_down, m_sb_w_qkv, m_sb_w_o, m_ssd_w_in, m_ssd_conv_w, m_ssd_conv_b, m_ssd_dt_bias, m_ssd_a_log, m_ssd_d, m_ssd_norm, m_ssd_w_out, m_sc_w_in, m_sc_conv_w, m_sc_w_out, m_final_norm, v_ffn1_norm, v_ffn1_w_gu, v_ffn1_w_down, v_mix_norm, v_ffn2_norm, v_ffn2_w_gu, v_ffn2_w_down, v_sb_w_qkv, v_sb_w_o, v_ssd_w_in, v_ssd_conv_w, v_ssd_conv_b, v_ssd_dt_bias, v_ssd_a_log, v_ssd_d, v_ssd_norm, v_ssd_w_out, v_sc_w_in, v_sc_conv_w, v_sc_w_out, v_final_norm):
    w = dict(ffn1_norm=ffn1_norm, ffn1_w_gu=ffn1_w_gu, ffn1_w_down=ffn1_w_down, mix_norm=mix_norm, ffn2_norm=ffn2_norm, ffn2_w_gu=ffn2_w_gu, ffn2_w_down=ffn2_w_down, sb_w_qkv=sb_w_qkv, sb_w_o=sb_w_o, ssd_w_in=ssd_w_in, ssd_conv_w=ssd_conv_w, ssd_conv_b=ssd_conv_b, ssd_dt_bias=ssd_dt_bias, ssd_a_log=ssd_a_log, ssd_d=ssd_d, ssd_norm=ssd_norm, ssd_w_out=ssd_w_out, sc_w_in=sc_w_in, sc_conv_w=sc_conv_w, sc_w_out=sc_w_out, final_norm=final_norm)
    mom = dict(ffn1_norm=m_ffn1_norm, ffn1_w_gu=m_ffn1_w_gu, ffn1_w_down=m_ffn1_w_down, mix_norm=m_mix_norm, ffn2_norm=m_ffn2_norm, ffn2_w_gu=m_ffn2_w_gu, ffn2_w_down=m_ffn2_w_down, sb_w_qkv=m_sb_w_qkv, sb_w_o=m_sb_w_o, ssd_w_in=m_ssd_w_in, ssd_conv_w=m_ssd_conv_w, ssd_conv_b=m_ssd_conv_b, ssd_dt_bias=m_ssd_dt_bias, ssd_a_log=m_ssd_a_log, ssd_d=m_ssd_d, ssd_norm=m_ssd_norm, ssd_w_out=m_ssd_w_out, sc_w_in=m_sc_w_in, sc_conv_w=m_sc_conv_w, sc_w_out=m_sc_w_out, final_norm=m_final_norm)
    var = dict(ffn1_norm=v_ffn1_norm, ffn1_w_gu=v_ffn1_w_gu, ffn1_w_down=v_ffn1_w_down, mix_norm=v_mix_norm, ffn2_norm=v_ffn2_norm, ffn2_w_gu=v_ffn2_w_gu, ffn2_w_down=v_ffn2_w_down, sb_w_qkv=v_sb_w_qkv, sb_w_o=v_sb_w_o, ssd_w_in=v_ssd_w_in, ssd_conv_w=v_ssd_conv_w, ssd_conv_b=v_ssd_conv_b, ssd_dt_bias=v_ssd_dt_bias, ssd_a_log=v_ssd_a_log, ssd_d=v_ssd_d, ssd_norm=v_ssd_norm, ssd_w_out=v_ssd_w_out, sc_w_in=v_sc_w_in, sc_conv_w=v_sc_conv_w, sc_w_out=v_sc_w_out, final_norm=v_final_norm)
    me = 4 * lax.axis_index("x") + 2 * lax.axis_index("y") + lax.axis_index("c")
    big = [n for n, _, _ in _MATMUL_WEIGHTS]

    shard_shapes = [w[n].shape for n in big]
    packed, counts = _pack_rows([w[n].astype(BF16) for n in big], 16)
    gathered = _unpack_rows(_exchange(packed, scatter=False, name="gather_weights"), counts, shard_shapes, (N_DEV,))
    full = dict(w)
    for (n, to_full, _), g in zip(_MATMUL_WEIGHTS, gathered):
        full[n] = to_full(g)
    conv_shapes = [w[n].shape for n in _CONV_WEIGHTS]
    cpacked, ccounts = _pack_rows([w[n] for n in _CONV_WEIGHTS], 8)
    cg = _unpack_rows(_exchange(cpacked, scatter=False, name="gather_conv"), ccounts, conv_shapes, (N_DEV,))
    for n, g in zip(_CONV_WEIGHTS, cg):
        full[n] = _col_full(g)

    loss_part, dx, grads = _local_step(x[0], loss_target[0], full)
    loss = lax.psum(loss_part[0, 0], ("x", "y", "c"))

    parts = []
    for n, _, to_parts in _MATMUL_WEIGHTS:
        g = grads[n]
        if n == "ssd_w_in":
            g = g[..., :SSD_IN_DIM]
        parts.append(to_parts(g.astype(BF16)).reshape(N_DEV, -1, _LANES))
    send = jnp.concatenate(parts, axis=1)
    send = jnp.pad(send, ((0, 0), (0, -send.shape[1] % 16), (0, 0)))
    recv = _exchange(send, scatter=True, name="scatter_grads")
    out_g, out_d, out_m, out_v = {}, {}, {}, {}

    def update(n, contrib):
        shp = w[n].shape
        two_d = (-1, shp[-1])
        res = _adamw_reduce(contrib.reshape((N_DEV,) + w[n].reshape(two_d).shape), w[n].reshape(two_d),
                            mom[n].reshape(two_d), var[n].reshape(two_d), name=f"adamw_{n}")
        out_g[n], out_d[n], out_m[n], out_v[n] = (r.reshape(shp) for r in res)

    off = 0
    for n, cnt in zip(big, counts):
        update(n, recv[:, off:off + cnt, :])
        off += cnt

    small = list(_REPLICATED) + list(_CONV_WEIGHTS)
    small_shapes = [grads[n].shape for n in small]
    spacked, scounts = _pack_rows([grads[n].astype(F32) for n in small], 8)
    sg = _unpack_rows(_exchange(spacked, scatter=False, name="gather_small_grads"), scounts, small_shapes, (N_DEV,))
    sg = dict(zip(small, sg))
    rep_w, rcounts = _pack_rows([w[n] for n in _REPLICATED], 8)
    rep_m, _ = _pack_rows([mom[n] for n in _REPLICATED], 8)
    rep_v, _ = _pack_rows([var[n] for n in _REPLICATED], 8)
    rep_p = jnp.concatenate([_rows_of(sg[n].reshape(N_DEV, -1)[q]) for q in range(N_DEV) for n in _REPLICATED], axis=0)
    rep_p = rep_p.reshape(N_DEV, -1, _LANES)
    rep_p = jnp.pad(rep_p, ((0, 0), (0, rep_w.shape[0] - rep_p.shape[1]), (0, 0)))
    res = _adamw_reduce(rep_p, rep_w, rep_m, rep_v, name="adamw_replicated")
    rep_shapes = [w[n].shape for n in _REPLICATED]
    for tgt, r in zip((out_g, out_d, out_m, out_v), res):
        for n, a in zip(_REPLICATED, _unpack_rows(r, rcounts, rep_shapes)):
            tgt[n] = a
    for n in _CONV_WEIGHTS:
        c = w[n].shape[-1]
        update(n, lax.dynamic_slice_in_dim(sg[n], me * c, c, axis=sg[n].ndim - 1))

    return (loss, dx[None], *[out_g[n] for n in _ORDER], *[out_d[n] for n in _ORDER],
            *[out_m[n] for n in _ORDER], *[out_v[n] for n in _ORDER])
```

```python
import functools
import math

import jax
import jax.numpy as jnp
from jax import lax
from jax.experimental import pallas as pl
from jax.experimental.pallas import tpu as pltpu

F32 = jnp.float32
BF16 = jnp.bfloat16

D_MODEL = 1024
D_FF = 2816
DEPTH = 4
N_DEV = 8
SB_HEADS = 16
SB_HEAD_DIM = 64
SB_TILE = 256
SB_HEADS_PER_STEP = 2
SSD_HEADS = 32
SSD_HEAD_DIM = 64
SSD_GROUPS = 8
SSD_HPG = 4
SSD_STATE = 128
SSD_CHUNK = 128
SSD_D_INNER = 2048
SSD_CONV_DIM = 4096
SSD_IN_DIM = 6176
SSD_IN_PAD = 6272
SSD_NORM_GROUP = 256
RMS_EPS = 1e-6
ADAM_LR = 0.001
ADAM_B1 = 0.9
ADAM_B2 = 0.999
ADAM_EPS = 1e-08
ADAM_WD = 0.01
ADAM_STEP = 10
VMEM_LIMIT = 56 * 1024 * 1024

NT = (((1,), (1,)), ((), ()))
TN = (((0,), (0,)), ((), ()))
NN = (((1,), (0,)), ((), ()))


def _cparams(n_axes):
    return pltpu.CompilerParams(dimension_semantics=("arbitrary",) * n_axes, vmem_limit_bytes=VMEM_LIMIT)


def _tile(n, want, mult=8):
    if n <= want:
        return n
    for t in range(want, 0, -1):
        if n % t == 0 and t % mult == 0:
            return t
    return n


def _sigmoid(x):
    return 1.0 / (1.0 + jnp.exp(-x))


def _dot(a, b, dn=NN):
    return lax.dot_general(a, b, dn, preferred_element_type=F32)


def _split3(x):
    x1 = x.astype(BF16)
    r1 = x - x1.astype(F32)
    x2 = r1.astype(BF16)
    x3 = (r1 - x2.astype(F32)).astype(BF16)
    return x1, x2, x3


def _dot_exact(x, t):
    x1, x2, x3 = _split3(x)
    return _dot(x1, t) + _dot(x2, t) + _dot(x3, t)


def _dot_hilo(x, t):
    x1 = x.astype(BF16)
    x2 = (x - x1.astype(F32)).astype(BF16)
    return _dot(x1, t) + _dot(x2, t)


def _mm(a, b, *, name, ta=False, tb=False, sa=False, sb=False, so=False, tm=512, tn=1024, tk=1024,
        out_dtype=F32, epilogue=None, extras=(), outs=None, pair=None):
    ash, bsh = a.shape[-2:], b.shape[-2:]
    m, k = (ash[1], ash[0]) if ta else ash
    n = bsh[0] if tb else bsh[1]
    s_n = pair or (a.shape[0] if sa else (b.shape[0] if sb else 1))
    tm, tn, tk = _tile(m, tm), _tile(n, tn, 128), _tile(k, tk, 128)
    nk = k // tk
    if outs is None:
        outs = [(out_dtype, "stile" if so else "tile")]
    if epilogue is None:
        epilogue = lambda acc: (acc,)

    if ta:
        a_blk, a_idx = (tk, tm), (lambda j, i, kk: (kk, i))
    else:
        a_blk, a_idx = (tm, tk), (lambda j, i, kk: (i, kk))
    if tb:
        b_blk, b_idx = (tn, tk), (lambda j, i, kk: (j, kk))
    else:
        b_blk, b_idx = (tk, tn), (lambda j, i, kk: (kk, j))

    def lead(blk, idx, has_s):
        if not has_s:
            return pl.BlockSpec(blk, idx)
        return pl.BlockSpec((s_n,) + blk, lambda j, i, kk: (0,) + idx(j, i, kk))

    kinds = {
        "tile": lambda: pl.BlockSpec((tm, tn), lambda j, i, kk: (i, j)),
        "stile": lambda: pl.BlockSpec((s_n, tm, tn), lambda j, i, kk: (0, i, j)),
        "row": lambda: pl.BlockSpec((1, tn), lambda j, i, kk: (0, j)),
        "colsum": lambda: pl.BlockSpec((1, tn), lambda j, i, kk: (0, j)),
    }
    shapes = {"tile": (m, n), "stile": (s_n, m, n), "colsum": (1, n)}
    in_specs = [lead(a_blk, a_idx, sa), lead(b_blk, b_idx, sb)] + [kinds[kd]() for _, kd in extras]
    out_specs = [kinds[kd]() for _, kd in outs]
    out_shape = [jax.ShapeDtypeStruct(shapes[kd], dt) for dt, kd in outs]
    n_ex, n_out = len(extras), len(outs)
    dn = ((((0,) if ta else (1,)), ((1,) if tb else (0,))), ((), ()))
    acc_shape = (s_n, tm, tn) if so else (tm, tn)

    def body(*refs):
        a_ref, b_ref = refs[0], refs[1]
        ex_refs = refs[2:2 + n_ex]
        o_refs = refs[2 + n_ex:2 + n_ex + n_out]
        acc = refs[-1]
        i = pl.program_id(1)
        kk = pl.program_id(2)

        @pl.when(kk == 0)
        def _():
            acc[...] = jnp.zeros_like(acc)

        for s in range(s_n if (sa or sb) else 1):
            av = (a_ref[s] if sa else a_ref[...]).astype(BF16)
            bv = (b_ref[s] if sb else b_ref[...]).astype(BF16)
            d = lax.dot_general(av, bv, dn, preferred_element_type=F32)
            if so:
                acc[s] += d
            else:
                acc[...] += d

        @pl.when(kk == nk - 1)
        def _():
            accv = tuple(acc[s] for s in range(s_n)) if so else acc[...]
            vals = epilogue(accv, *[r[...] for r in ex_refs])
            for (dt, kd), o_ref, val in zip(outs, o_refs, vals):
                if kd == "colsum":
                    _accumulate(o_ref, val, i == 0)
                elif kd == "stile":
                    for s in range(s_n):
                        o_ref[s] = val[s].astype(dt)
                else:
                    o_ref[...] = val.astype(dt)

    res = pl.pallas_call(
        body, name=name, grid=(n // tn, m // tm, nk),
        in_specs=in_specs, out_specs=out_specs, out_shape=out_shape,
        scratch_shapes=[pltpu.VMEM(acc_shape, F32)], compiler_params=_cparams(3),
    )(a, b, *[e for e, _ in extras])
    return res[0] if len(res) == 1 else res


def _accumulate(o_ref, val, first):
    @pl.when(first)
    def _():
        o_ref[...] = val

    @pl.when(jnp.logical_not(first))
    def _():
        o_ref[...] += val


def _rmsnorm(x, g, *, name):
    l, d = x.shape
    tm = _tile(l, 512)

    def body(x_ref, g_ref, o_ref):
        xv = x_ref[...]
        r = lax.rsqrt(jnp.mean(xv * xv, axis=1, keepdims=True) + RMS_EPS)
        o_ref[...] = (xv * r * g_ref[...]).astype(BF16)

    return pl.pallas_call(
        body, name=name, grid=(l // tm,),
        in_specs=[pl.BlockSpec((tm, d), lambda i: (i, 0)), pl.BlockSpec((1, d), lambda i: (0, 0))],
        out_specs=pl.BlockSpec((tm, d), lambda i: (i, 0)),
        out_shape=jax.ShapeDtypeStruct((l, d), BF16), compiler_params=_cparams(1),
    )(x, g)


def _norm_bwd_epilogue(dh, x, g, dres):
    r = lax.rsqrt(jnp.mean(x * x, axis=1, keepdims=True) + RMS_EPS)
    xh = x * r
    dg = jnp.sum(dh * xh, axis=0, keepdims=True)
    dxh = dh * g
    dx = r * (dxh - xh * jnp.mean(dxh * xh, axis=1, keepdims=True))
    return dres + dx, dg


def _final_loss(x, g, tgt, *, name):
    l, d = x.shape
    tm = _tile(l, 512)

    def body(x_ref, g_ref, t_ref, loss_ref, dx_ref, dg_ref):
        i = pl.program_id(0)
        xv, gv = x_ref[...], g_ref[...]
        r = lax.rsqrt(jnp.mean(xv * xv, axis=1, keepdims=True) + RMS_EPS)
        xh = xv * r
        e = xh * gv - t_ref[...]
        part = 0.5 * jnp.sum(jnp.mean(e * e, axis=1, keepdims=True), axis=0, keepdims=True)
        dy = e * (1.0 / d)
        dg = jnp.sum(dy * xh, axis=0, keepdims=True)
        dxh = dy * gv
        dx_ref[...] = r * (dxh - xh * jnp.mean(dxh * xh, axis=1, keepdims=True))
        _accumulate(dg_ref, dg, i == 0)
        _accumulate(loss_ref, jnp.broadcast_to(part, (1, 128)), i == 0)

    return pl.pallas_call(
        body, name=name, grid=(l // tm,),
        in_specs=[pl.BlockSpec((tm, d), lambda i: (i, 0)), pl.BlockSpec((1, d), lambda i: (0, 0)),
                  pl.BlockSpec((tm, d), lambda i: (i, 0))],
        out_specs=[pl.BlockSpec((1, 128), lambda i: (0, 0)), pl.BlockSpec((tm, d), lambda i: (i, 0)),
                   pl.BlockSpec((1, d), lambda i: (0, 0))],
        out_shape=[jax.ShapeDtypeStruct((1, 128), F32), jax.ShapeDtypeStruct((l, d), F32),
                   jax.ShapeDtypeStruct((1, d), F32)],
        compiler_params=_cparams(1),
    )(x, g, tgt)


def _ffn_fwd(x, g, wgu, wd, tag):
    h = _rmsnorm(x, g, name=f"{tag}_norm")

    def act(acc):
        gate, up = acc
        return acc, gate * _sigmoid(gate) * up

    gu, a = _mm(h, wgu, sb=True, so=True, tm=256, tn=1408, tk=1024, name=f"{tag}_up",
                outs=[(BF16, "stile"), (BF16, "tile")], epilogue=act)
    xo = _mm(a, wd, tm=512, tn=1024, tk=2816, name=f"{tag}_down", extras=[(x, "tile")],
             epilogue=lambda acc, xt: (xt + 0.5 * acc,))
    return xo, (x, h, gu, a)


def _ffn_bwd(dout, saved, g, wgu, wd, tag):
    x, h, gu, a = saved

    def act_bwd(acc, guv):
        da = 0.5 * acc
        gate, up = guv[0].astype(F32), guv[1].astype(F32)
        s = _sigmoid(gate)
        return ((da * up * s * (1.0 + gate * (1.0 - s)), da * gate * s),)

    dgu = _mm(dout, wd, tb=True, pair=2, tm=256, tn=1408, tk=1024, name=f"{tag}_dact", extras=[(gu, "stile")],
              outs=[(BF16, "stile")], epilogue=act_bwd)
    dwd = _mm(a, dout, ta=True, tm=1408, tn=1024, tk=512, name=f"{tag}_dwd", out_dtype=BF16,
              epilogue=lambda acc: (0.5 * acc,))
    dwgu = _mm(h, dgu, ta=True, sb=True, so=True, tm=512, tn=1408, tk=512, name=f"{tag}_dwgu", out_dtype=BF16)
    dx, dg = _mm(dgu, wgu, tb=True, sa=True, sb=True, tm=256, tn=1024, tk=1408, name=f"{tag}_dx",
                 extras=[(x, "tile"), (g, "row"), (dout, "tile")], outs=[(F32, "tile"), (F32, "colsum")],
                 epilogue=_norm_bwd_epilogue)
    return dx, dg, dwgu, dwd


def _sb_logs(z):
    lb = jnp.minimum(z, 0.0) - jnp.log(1.0 + jnp.exp(-jnp.abs(z)))
    return lb, lb - z


def _sb_fwd(q, k, v, *, name):
    h_n, l, dh = q.shape
    t = _tile(l, SB_TILE)
    hs = 2 * SB_HEADS_PER_STEP
    scale = dh ** -0.5

    def body(q_ref, k_ref, v_ref, o_ref):
        i = pl.program_id(1)
        qs = [(q_ref[hh].astype(F32) * scale).astype(BF16) for hh in range(hs)]
        row = lax.broadcasted_iota(jnp.int32, (t, t), 0)
        col = lax.broadcasted_iota(jnp.int32, (t, t), 1)
        strict = col < row
        tri = strict.astype(BF16)

        def block(jb, carry, masked):
            sl = pl.ds(pl.multiple_of(jb * t, t), t)
            zs = [_dot(qs[hh], k_ref[hh, sl, :], NT) for hh in range(hs)]
            lbs, tails, sums = [], [], []
            for hh in range(hs):
                lb, lk = _sb_logs(zs[hh])
                if masked:
                    lk = jnp.where(strict, lk, 0.0)
                lbs.append(lb)
                tails.append(_dot_hilo(lk, tri))
                sums.append(jnp.sum(lk, axis=1, keepdims=True))
            out = []
            for hh in range(hs):
                c, o = carry[hh]
                att = jnp.exp(lbs[hh] + tails[hh] + c)
                if masked:
                    att = jnp.where(strict, att, 0.0)
                out.append((c + sums[hh], o + _dot(att.astype(BF16), v_ref[hh, sl, :])))
            return tuple(out)

        carry = block(i, tuple((jnp.zeros((t, 1), F32), jnp.zeros((t, dh), F32)) for _ in range(hs)), True)
        carry = lax.fori_loop(0, i, lambda it, cr: block(i - 1 - it, cr, False), carry)
        for hh in range(hs):
            o_ref[hh] = carry[hh][1].astype(o_ref.dtype)

    qspec = pl.BlockSpec((hs, t, dh), lambda h, i: (h, i, 0))
    kspec = pl.BlockSpec((hs, l, dh), lambda h, i: (h, 0, 0))
    return pl.pallas_call(
        body, name=name, grid=(h_n // hs, l // t), in_specs=[qspec, kspec, kspec], out_specs=qspec,
        out_shape=jax.ShapeDtypeStruct((h_n, l, dh), BF16), compiler_params=_cparams(2),
    )(q, k, v)


def _sb_bwd(q, k, v, do, *, name):
    h_n, l, dh = q.shape
    t = _tile(l, SB_TILE)
    nq = l // t
    hs = SB_HEADS_PER_STEP
    scale = dh ** -0.5

    def body(q_ref, k_ref, v_ref, do_ref, dq_ref, dk_ref, dv_ref, e_scr, s_scr):
        i = pl.program_id(1)

        @pl.when(i == 0)
        def _():
            dk_ref[...] = jnp.zeros_like(dk_ref)
            dv_ref[...] = jnp.zeros_like(dv_ref)

        qs = [(q_ref[hh].astype(F32) * scale).astype(BF16) for hh in range(hs)]
        dos = [do_ref[hh] for hh in range(hs)]
        row = lax.broadcasted_iota(jnp.int32, (t, t), 0)
        col = lax.broadcasted_iota(jnp.int32, (t, t), 1)
        strict = col < row
        tri_suffix = strict.astype(BF16)
        tri_prefix = (row < col).astype(BF16)

        def sweep1(jb, cs, masked):
            sl = pl.ds(pl.multiple_of(jb * t, t), t)
            zs = [_dot(qs[hh], k_ref[hh, sl, :], NT) for hh in range(hs)]
            datts = [_dot(dos[hh], v_ref[hh, sl, :], NT) for hh in range(hs)]
            lbs, tails, out = [], [], []
            for hh in range(hs):
                lb, lk = _sb_logs(zs[hh])
                if masked:
                    lk = jnp.where(strict, lk, 0.0)
                lbs.append(lb)
                tails.append(_dot_hilo(lk, tri_suffix))
                s_scr[hh, jb] = jnp.exp(lb)
                out.append(cs[hh] + jnp.sum(lk, axis=1, keepdims=True))
            for hh in range(hs):
                att = jnp.exp(lbs[hh] + tails[hh] + cs[hh])
                if masked:
                    att = jnp.where(strict, att, 0.0)
                e_scr[hh, jb] = att * datts[hh]
                dv_ref[hh, sl, :] += _dot(att.astype(BF16), dos[hh], TN)
            return tuple(out)

        cs = sweep1(i, tuple(jnp.zeros((t, 1), F32) for _ in range(hs)), True)
        lax.fori_loop(0, i, lambda it, cc: sweep1(i - 1 - it, cc, False), cs)

        def sweep2(jb, carry, masked):
            sl = pl.ds(pl.multiple_of(jb * t, t), t)
            des = [e_scr[hh, jb] for hh in range(hs)]
            pres = [_dot_hilo(des[hh], tri_prefix) for hh in range(hs)]
            out = []
            for hh in range(hs):
                p, dq = carry[hh]
                de, sg = des[hh], s_scr[hh, jb]
                dlk = p + pres[hh]
                if masked:
                    dlk = jnp.where(strict, dlk, 0.0)
                dz = (de * (1.0 - sg) - dlk * sg).astype(BF16)
                dk_ref[hh, sl, :] += _dot(dz, qs[hh], TN)
                out.append((p + jnp.sum(de, axis=1, keepdims=True), dq + _dot(dz, k_ref[hh, sl, :])))
            return tuple(out)

        carry = lax.fori_loop(0, i, lambda jb, cr: sweep2(jb, cr, False),
                              tuple((jnp.zeros((t, 1), F32), jnp.zeros((t, dh), F32)) for _ in range(hs)))
        carry = sweep2(i, carry, True)
        for hh in range(hs):
            dq_ref[hh] = (carry[hh][1] * scale).astype(dq_ref.dtype)

    qspec = pl.BlockSpec((hs, t, dh), lambda h, i: (h, i, 0))
    kspec = pl.BlockSpec((hs, l, dh), lambda h, i: (h, 0, 0))
    return pl.pallas_call(
        body, name=name, grid=(h_n // hs, nq), in_specs=[qspec, kspec, kspec, qspec],
        out_specs=[qspec, kspec, kspec],
        out_shape=[jax.ShapeDtypeStruct((h_n, l, dh), BF16), jax.ShapeDtypeStruct((h_n, l, dh), F32),
                   jax.ShapeDtypeStruct((h_n, l, dh), F32)],
        scratch_shapes=[pltpu.VMEM((hs, nq, t, t), F32), pltpu.VMEM((hs, nq, t, t), F32)],
        compiler_params=_cparams(2),
    )(q, k, v, do)


def _to_heads(x, n):
    l = x.shape[0]
    return x.reshape(l, n, -1, SB_HEAD_DIM).transpose(1, 2, 0, 3)


def _from_heads(x):
    return x.transpose(1, 0, 2).reshape(x.shape[1], -1)


def _sb_layer_fwd(x, g, wqkv, wo, tag):
    h = _rmsnorm(x, g, name=f"{tag}_norm")
    qkv = _mm(h, wqkv, tm=512, tn=1024, tk=1024, name=f"{tag}_qkv", out_dtype=BF16)
    qkv_h = _to_heads(qkv, 3)
    o_h = _sb_fwd(qkv_h[0], qkv_h[1], qkv_h[2], name=f"{tag}_attn")
    o = _from_heads(o_h)
    xo = _mm(o, wo, tm=512, tn=1024, tk=1024, name=f"{tag}_out", extras=[(x, "tile")],
             epilogue=lambda acc, xt: (xt + acc,))
    return xo, (x, h, qkv_h, o)


def _sb_layer_bwd(dout, saved, g, wqkv, wo, tag):
    x, h, qkv_h, o = saved
    do = _mm(dout, wo, tb=True, tm=512, tn=1024, tk=1024, name=f"{tag}_do", out_dtype=BF16)
    dwo = _mm(o, dout, ta=True, tm=1024, tn=1024, tk=512, name=f"{tag}_dwo", out_dtype=BF16)
    dq, dk, dv = _sb_bwd(qkv_h[0], qkv_h[1], qkv_h[2], _to_heads(do, 1)[0], name=f"{tag}_attn_bwd")
    dqkv = jnp.concatenate([_from_heads(dq), _from_heads(dk.astype(BF16)), _from_heads(dv.astype(BF16))], axis=1)
    dwqkv = _mm(h, dqkv, ta=True, tm=1024, tn=1024, tk=512, name=f"{tag}_dwqkv", out_dtype=BF16)
    dx, dg = _mm(dqkv, wqkv, tb=True, tm=256, tn=1024, tk=1024, name=f"{tag}_dx",
                 extras=[(x, "tile"), (g, "row"), (dout, "tile")], outs=[(F32, "tile"), (F32, "colsum")],
                 epilogue=_norm_bwd_epilogue)
    return dx, dg, dwqkv, dwo


def _shift_down(x, s, t_idx):
    return jnp.where(t_idx >= s, pltpu.roll(x, s, 0), 0.0)


def _shift_up(x, s, t_idx):
    n = x.shape[0]
    return jnp.where(t_idx < n - s, pltpu.roll(x, n - s, 0), 0.0)


def _sc_fwd(p, cw, *, name):
    l = p.shape[0]
    d = cw.shape[1]
    tc = 128
    nb = d // tc

    def body(b_ref, c_ref, h_ref, w_ref, o_ref):
        v = c_ref[...] * h_ref[...]
        t_idx = lax.broadcasted_iota(jnp.int32, v.shape, 0)
        u = v * w_ref[2:3, :] + _shift_down(v, 1, t_idx) * w_ref[1:2, :] + _shift_down(v, 2, t_idx) * w_ref[0:1, :]
        o_ref[...] = (b_ref[...] * u).astype(BF16)

    return pl.pallas_call(
        body, name=name, grid=(nb,),
        in_specs=[pl.BlockSpec((l, tc), lambda j: (0, j)), pl.BlockSpec((l, tc), lambda j: (0, nb + j)),
                  pl.BlockSpec((l, tc), lambda j: (0, 2 * nb + j)), pl.BlockSpec((3, tc), lambda j: (0, j))],
        out_specs=pl.BlockSpec((l, tc), lambda j: (0, j)),
        out_shape=jax.ShapeDtypeStruct((l, d), BF16), compiler_params=_cparams(1),
    )(p, p, p, cw)


def _sc_bwd(p, cw, dbu, *, name):
    l = p.shape[0]
    d = cw.shape[1]
    tc = 128
    nb = d // tc

    def body(b_ref, c_ref, h_ref, w_ref, g_ref, db_ref, dc_ref, dh_ref, dw_ref):
        cv, hv = c_ref[...], h_ref[...]
        v = cv * hv
        t_idx = lax.broadcasted_iota(jnp.int32, v.shape, 0)
        v1, v2 = _shift_down(v, 1, t_idx), _shift_down(v, 2, t_idx)
        u = v * w_ref[2:3, :] + v1 * w_ref[1:2, :] + v2 * w_ref[0:1, :]
        dbu_v = g_ref[...]
        db_ref[...] = (dbu_v * u).astype(BF16)
        du = dbu_v * b_ref[...]
        dv = du * w_ref[2:3, :] + _shift_up(du, 1, t_idx) * w_ref[1:2, :] + _shift_up(du, 2, t_idx) * w_ref[0:1, :]
        dc_ref[...] = (dv * hv).astype(BF16)
        dh_ref[...] = (dv * cv).astype(BF16)
        dw_ref[...] = jnp.zeros_like(dw_ref)
        dw_ref[0:1, :] = jnp.sum(du * v2, axis=0, keepdims=True)
        dw_ref[1:2, :] = jnp.sum(du * v1, axis=0, keepdims=True)
        dw_ref[2:3, :] = jnp.sum(du * v, axis=0, keepdims=True)

    col = lambda off: pl.BlockSpec((l, tc), lambda j: (0, off + j))
    return pl.pallas_call(
        body, name=name, grid=(nb,),
        in_specs=[col(0), col(nb), col(2 * nb), pl.BlockSpec((3, tc), lambda j: (0, j)), col(0)],
        out_specs=[col(0), col(0), col(0), pl.BlockSpec((8, tc), lambda j: (0, j))],
        out_shape=[jax.ShapeDtypeStruct((l, d), BF16)] * 3 + [jax.ShapeDtypeStruct((8, d), F32)],
        compiler_params=_cparams(1),
    )(p, p, p, cw, dbu)


def _sc_layer_fwd(x, g, win, cw, wout, tag):
    h = _rmsnorm(x, g, name=f"{tag}_norm")
    p = _mm(h, win, tm=512, tn=1024, tk=1024, name=f"{tag}_in")
    bu = _sc_fwd(p, cw, name=f"{tag}_conv")
    xo = _mm(bu, wout, tm=512, tn=1024, tk=1024, name=f"{tag}_out", extras=[(x, "tile")],
             epilogue=lambda acc, xt: (xt + acc,))
    return xo, (x, h, p, bu)


def _sc_layer_bwd(dout, saved, g, win, cw, wout, tag):
    x, h, p, bu = saved
    dbu = _mm(dout, wout, tb=True, tm=512, tn=1024, tk=1024, name=f"{tag}_dbu")
    dwout = _mm(bu, dout, ta=True, tm=1024, tn=1024, tk=512, name=f"{tag}_dwout", out_dtype=BF16)
    db, dc, dh, dcw = _sc_bwd(p, cw, dbu, name=f"{tag}_conv_bwd")
    dp = jnp.concatenate([db, dc, dh], axis=1)
    dwin = _mm(h, dp, ta=True, tm=1024, tn=1024, tk=512, name=f"{tag}_dwin", out_dtype=BF16)
    dx, dg = _mm(dp, win, tb=True, tm=256, tn=1024, tk=1024, name=f"{tag}_dx",
                 extras=[(x, "tile"), (g, "row"), (dout, "tile")], outs=[(F32, "tile"), (F32, "colsum")],
                 epilogue=_norm_bwd_epilogue)
    return dx, dg, dwin, dcw[:3], dwout


def _ssd_conv_fwd(p, cw, cb, *, name):
    l = p.shape[0]
    tc = 128
    nb = SSD_CONV_DIM // tc
    off = SSD_D_INNER // tc

    def body(x_ref, w_ref, b_ref, o_ref):
        xv = x_ref[...]
        t_idx = lax.broadcasted_iota(jnp.int32, xv.shape, 0)
        pre = xv * w_ref[3:4, :] + b_ref[...]
        for s in (1, 2, 3):
            pre = pre + _shift_down(xv, s, t_idx) * w_ref[3 - s:4 - s, :]
        o_ref[...] = pre * _sigmoid(pre)

    return pl.pallas_call(
        body, name=name, grid=(nb,),
        in_specs=[pl.BlockSpec((l, tc), lambda j: (0, off + j)), pl.BlockSpec((4, tc), lambda j: (0, j)),
                  pl.BlockSpec((1, tc), lambda j: (0, j))],
        out_specs=pl.BlockSpec((l, tc), lambda j: (0, j)),
        out_shape=jax.ShapeDtypeStruct((l, SSD_CONV_DIM), F32), compiler_params=_cparams(1),
    )(p, cw, cb)


def _ssd_conv_bwd(p, cw, cb, dact, *, name):
    l = p.shape[0]
    tc = 128
    nb = SSD_CONV_DIM // tc
    off = SSD_D_INNER // tc

    def body(x_ref, w_ref, b_ref, g_ref, dx_ref, dw_ref):
        xv = x_ref[...]
        t_idx = lax.broadcasted_iota(jnp.int32, xv.shape, 0)
        xs = [xv] + [_shift_down(xv, s, t_idx) for s in (1, 2, 3)]
        pre = b_ref[...] + xs[0] * w_ref[3:4, :]
        for s in (1, 2, 3):
            pre = pre + xs[s] * w_ref[3 - s:4 - s, :]
        sg = _sigmoid(pre)
        dpre = g_ref[...] * sg * (1.0 + pre * (1.0 - sg))
        dx = dpre * w_ref[3:4, :]
        for s in (1, 2, 3):
            dx = dx + _shift_up(dpre, s, t_idx) * w_ref[3 - s:4 - s, :]
        dx_ref[...] = dx
        dw_ref[...] = jnp.zeros_like(dw_ref)
        for s in (0, 1, 2, 3):
            dw_ref[3 - s:4 - s, :] = jnp.sum(dpre * xs[s], axis=0, keepdims=True)
        dw_ref[4:5, :] = jnp.sum(dpre, axis=0, keepdims=True)

    return pl.pallas_call(
        body, name=name, grid=(nb,),
        in_specs=[pl.BlockSpec((l, tc), lambda j: (0, off + j)), pl.BlockSpec((4, tc), lambda j: (0, j)),
                  pl.BlockSpec((1, tc), lambda j: (0, j)), pl.BlockSpec((l, tc), lambda j: (0, j))],
        out_specs=[pl.BlockSpec((l, tc), lambda j: (0, j)), pl.BlockSpec((8, tc), lambda j: (0, j))],
        out_shape=[jax.ShapeDtypeStruct((l, SSD_CONV_DIM), F32), jax.ShapeDtypeStruct((8, SSD_CONV_DIM), F32)],
        compiler_params=_cparams(1),
    )(p, cw, cb, dact)


def _ssd_dt_fwd(p, bias, *, name):
    l = p.shape[0]
    tm = _tile(l, 1024)
    off = (SSD_D_INNER + SSD_CONV_DIM) // 128

    def body(x_ref, b_ref, o_ref):
        v = x_ref[...] + b_ref[...]
        o_ref[...] = jnp.maximum(v, 0.0) + jnp.log(1.0 + jnp.exp(-jnp.abs(v)))

    return pl.pallas_call(
        body, name=name, grid=(l // tm,),
        in_specs=[pl.BlockSpec((tm, 128), lambda i: (i, off)), pl.BlockSpec((1, 128), lambda i: (0, 0))],
        out_specs=pl.BlockSpec((tm, 128), lambda i: (i, 0)),
        out_shape=jax.ShapeDtypeStruct((l, 128), F32), compiler_params=_cparams(1),
    )(p, bias)


def _ssd_dt_bwd(p, bias, ddt, *, name):
    l = p.shape[0]
    tm = _tile(l, 1024)
    off = (SSD_D_INNER + SSD_CONV_DIM) // 128

    def body(x_ref, b_ref, g_ref, o_ref, db_ref):
        i = pl.program_id(0)
        d = g_ref[...] * _sigmoid(x_ref[...] + b_ref[...])
        o_ref[...] = d
        _accumulate(db_ref, jnp.sum(d, axis=0, keepdims=True), i == 0)

    return pl.pallas_call(
        body, name=name, grid=(l // tm,),
        in_specs=[pl.BlockSpec((tm, 128), lambda i: (i, off)), pl.BlockSpec((1, 128), lambda i: (0, 0)),
                  pl.BlockSpec((tm, 128), lambda i: (i, 0))],
        out_specs=[pl.BlockSpec((tm, 128), lambda i: (i, 0)), pl.BlockSpec((1, 128), lambda i: (0, 0))],
        out_shape=[jax.ShapeDtypeStruct((l, 128), F32), jax.ShapeDtypeStruct((1, 128), F32)],
        compiler_params=_cparams(1),
    )(p, bias, ddt)


def _row_to_col(r, eye):
    return jnp.sum(jnp.where(eye, r, 0.0), axis=1, keepdims=True)


def _col_to_row(c, eye):
    return jnp.sum(jnp.where(eye, c, 0.0), axis=0, keepdims=True)


def _ssd_chunk_common(b_ref, c_ref, dt_ref, a_ref, lam_scr):
    n = SSD_CHUNK
    row = lax.broadcasted_iota(jnp.int32, (n, n), 0)
    col = lax.broadcasted_iota(jnp.int32, (n, n), 1)
    bm, cm = b_ref[...].astype(BF16), c_ref[...].astype(BF16)
    g = _dot(cm, bm, NT)
    incl = (row <= col).astype(BF16)
    lam_scr[...] = _dot_exact(dt_ref[...] * a_ref[...], incl)
    return row, col, bm, cm, g


def _ssd_head_common(r, row, col, dt_ref, lam_scr):
    eye, tril = row == col, row >= col
    lam_r = lam_scr[r:r + 1, :]
    dt_r = dt_ref[r:r + 1, :]
    lam_c = _row_to_col(lam_r, eye)
    dt_c = _row_to_col(dt_r, eye)
    dk = jnp.where(tril, jnp.exp(jnp.minimum(lam_c - lam_r, 0.0)), 0.0)
    lam_last = jnp.sum(jnp.where(col[0:1, :] == SSD_CHUNK - 1, lam_r, 0.0), axis=1, keepdims=True)
    return eye, lam_r, dt_r, lam_c, dt_c, dk, lam_last


def _ssd_fwd(xh, act, dt_t, a_b, *, name):
    l = xh.shape[1]
    nc = l // SSD_CHUNK
    n, p_dim, hpg = SSD_CHUNK, SSD_HEAD_DIM, SSD_HPG

    def body(x_ref, b_ref, c_ref, dt_ref, a_ref, y_ref, hp_ref, h_scr, lam_scr):
        @pl.when(pl.program_id(1) == 0)
        def _():
            h_scr[...] = jnp.zeros_like(h_scr)

        row, col, bm, cm, g = _ssd_chunk_common(b_ref, c_ref, dt_ref, a_ref, lam_scr)
        for r in range(hpg):
            _, _, dt_r, lam_c, dt_c, dk, lam_last = _ssd_head_common(r, row, col, dt_ref, lam_scr)
            xr = x_ref[r]
            hr = h_scr[r]
            w = (g * dk * dt_r).astype(BF16)
            y = _dot(w, xr.astype(BF16)) + _dot(cm, hr.astype(BF16), NT) * jnp.exp(lam_c)
            y_ref[r] = y
            hp_ref[r] = hr
            xw = (xr * (jnp.exp(lam_last - lam_c) * dt_c)).astype(BF16)
            h_scr[r] = jnp.exp(lam_last) * hr + _dot(xw, bm, TN)

    g_off = SSD_D_INNER // SSD_STATE
    return pl.pallas_call(
        body, name=name, grid=(SSD_GROUPS, nc),
        in_specs=[pl.BlockSpec((hpg, n, p_dim), lambda g, c: (g, c, 0)),
                  pl.BlockSpec((n, SSD_STATE), lambda g, c: (c, g_off + g)),
                  pl.BlockSpec((n, SSD_STATE), lambda g, c: (c, g_off + SSD_GROUPS + g)),
                  pl.BlockSpec((None, 8, n), lambda g, c: (g, 0, c)),
                  pl.BlockSpec((None, 8, 128), lambda g, c: (g, 0, 0))],
        out_specs=[pl.BlockSpec((hpg, n, p_dim), lambda g, c: (g, c, 0)),
                   pl.BlockSpec((None, hpg, p_dim, SSD_STATE), lambda g, c: (c, g, 0, 0))],
        out_shape=[jax.ShapeDtypeStruct(xh.shape, F32),
                   jax.ShapeDtypeStruct((nc, SSD_HEADS, p_dim, SSD_STATE), F32)],
        scratch_shapes=[pltpu.VMEM((hpg, p_dim, SSD_STATE), F32), pltpu.VMEM((8, n), F32)],
        compiler_params=_cparams(2),
    )(xh, act, act, dt_t, a_b)


def _ssd_bwd(xh, act, dt_t, a_b, hprev, dyh, *, name):
    l = xh.shape[1]
    nc = l // SSD_CHUNK
    n, p_dim, hpg = SSD_CHUNK, SSD_HEAD_DIM, SSD_HPG

    def body(x_ref, b_ref, c_ref, dt_ref, a_ref, hp_ref, dy_ref,
             dx_ref, db_ref, dc_ref, ddt_ref, da_ref, dh_scr, lam_scr, dlam_scr, ddt_scr):
        ci = pl.program_id(1)

        @pl.when(ci == 0)
        def _():
            dh_scr[...] = jnp.zeros_like(dh_scr)

        row, col, bm, cm, g = _ssd_chunk_common(b_ref, c_ref, dt_ref, a_ref, lam_scr)
        dlam_scr[...] = jnp.zeros_like(dlam_scr)
        ddt_scr[...] = jnp.zeros_like(ddt_scr)
        dg_acc = jnp.zeros((n, n), F32)
        dc_acc = jnp.zeros((n, SSD_STATE), F32)
        db_acc = jnp.zeros((n, SSD_STATE), F32)
        for r in range(hpg):
            eye, _, dt_r, lam_c, dt_c, dk, lam_last = _ssd_head_common(r, row, col, dt_ref, lam_scr)
            xr, dyr, hr, dhr = x_ref[r], dy_ref[r], hp_ref[r], dh_scr[r]
            xb, dyb, hb, dhb = xr.astype(BF16), dyr.astype(BF16), hr.astype(BF16), dhr.astype(BF16)
            e_l = jnp.exp(lam_c)
            e_last = jnp.exp(lam_last)
            decay_c = jnp.exp(lam_last - lam_c)
            w_c = decay_c * dt_c
            m = g * dk * dt_r
            dm = _dot(dyb, xb, NT)
            bdh = _dot(bm, dhb, NT)
            dx_ref[r] = _dot(m.astype(BF16), dyb, TN) + w_c * bdh
            dg_acc = dg_acc + dm * dk * dt_r
            q_mat = dm * g * dk
            p_mat = q_mat * dt_r
            yoff = _dot(cm, hb, NT) * e_l
            q_c = jnp.sum(xr * bdh, axis=1, keepdims=True)
            dlam_c = (jnp.sum(p_mat, axis=1, keepdims=True) + jnp.sum(dyr * yoff, axis=1, keepdims=True)
                      - w_c * q_c)
            d_last = (jnp.sum(w_c * q_c, axis=0, keepdims=True)
                      + e_last * jnp.sum(jnp.sum(dhr * hr, axis=1, keepdims=True), axis=0, keepdims=True))
            dlam_scr[r:r + 1, :] = (_col_to_row(dlam_c, eye) - jnp.sum(p_mat, axis=0, keepdims=True)
                                    + jnp.where(col[0:1, :] == n - 1, d_last, 0.0))
            ddt_scr[r:r + 1, :] = jnp.sum(q_mat, axis=0, keepdims=True) + _col_to_row(decay_c * q_c, eye)
            dc_acc = dc_acc + e_l * _dot(dyb, hb)
            db_acc = db_acc + _dot((xr * w_c).astype(BF16), dhb)
            dh_scr[r] = e_last * dhr + _dot((dyr * e_l).astype(BF16), cm, TN)

        dgb = dg_acc.astype(BF16)
        dc_ref[...] = _dot(dgb, bm) + dc_acc
        db_ref[...] = _dot(dgb, cm, TN) + db_acc
        rev = (row >= col).astype(BF16)
        da = _dot_exact(dlam_scr[...], rev)
        ddt_ref[...] = ddt_scr[...] + da * a_ref[...]
        _accumulate(da_ref, da * dt_ref[...], ci == 0)

        @pl.when(ci == nc - 1)
        def _():
            da_ref[...] = jnp.broadcast_to(jnp.sum(da_ref[...], axis=1, keepdims=True), da_ref.shape)

    g_off = SSD_D_INNER // SSD_STATE
    rc = lambda c: nc - 1 - c
    hspec = pl.BlockSpec((hpg, n, p_dim), lambda g, c: (g, rc(c), 0))
    gspec = pl.BlockSpec((n, SSD_STATE), lambda g, c: (rc(c), g))
    return pl.pallas_call(
        body, name=name, grid=(SSD_GROUPS, nc),
        in_specs=[hspec,
                  pl.BlockSpec((n, SSD_STATE), lambda g, c: (rc(c), g_off + g)),
                  pl.BlockSpec((n, SSD_STATE), lambda g, c: (rc(c), g_off + SSD_GROUPS + g)),
                  pl.BlockSpec((None, 8, n), lambda g, c: (g, 0, rc(c))),
                  pl.BlockSpec((None, 8, 128), lambda g, c: (g, 0, 0)),
                  pl.BlockSpec((None, hpg, p_dim, SSD_STATE), lambda g, c: (rc(c), g, 0, 0)),
                  hspec],
        out_specs=[hspec, gspec, gspec,
                   pl.BlockSpec((None, 8, n), lambda g, c: (g, 0, rc(c))),
                   pl.BlockSpec((None, 8, 128), lambda g, c: (g, 0, 0))],
        out_shape=[jax.ShapeDtypeStruct(xh.shape, F32),
                   jax.ShapeDtypeStruct((l, SSD_GROUPS * SSD_STATE), F32),
                   jax.ShapeDtypeStruct((l, SSD_GROUPS * SSD_STATE), F32),
                   jax.ShapeDtypeStruct(dt_t.shape, F32),
                   jax.ShapeDtypeStruct(a_b.shape, F32)],
        scratch_shapes=[pltpu.VMEM((hpg, p_dim, SSD_STATE), F32), pltpu.VMEM((8, n), F32),
                        pltpu.VMEM((8, n), F32), pltpu.VMEM((8, n), F32)],
        compiler_params=_cparams(2),
    )(xh, act, act, dt_t, a_b, hprev, dyh)


def _ssd_gate_fwd(y, act, p, d_vec, gn, *, name):
    l = y.shape[0]
    w = SSD_D_INNER
    tm = _tile(l, 256)

    def body(y_ref, xs_ref, z_ref, d_ref, g_ref, o_ref):
        for gi in range(SSD_GROUPS):
            sl = slice(gi * SSD_NORM_GROUP, (gi + 1) * SSD_NORM_GROUP)
            z = z_ref[:, sl]
            y2 = (y_ref[:, sl] + d_ref[:, sl] * xs_ref[:, sl]) * (z * _sigmoid(z))
            r = lax.rsqrt(jnp.mean(y2 * y2, axis=1, keepdims=True) + RMS_EPS)
            o_ref[:, sl] = (y2 * r * g_ref[:, sl]).astype(BF16)

    rows = pl.BlockSpec((tm, w), lambda i: (i, 0))
    vec = pl.BlockSpec((1, w), lambda i: (0, 0))
    return pl.pallas_call(
        body, name=name, grid=(l // tm,), in_specs=[rows, rows, rows, vec, vec], out_specs=rows,
        out_shape=jax.ShapeDtypeStruct((l, w), BF16), compiler_params=_cparams(1),
    )(y, act, p, d_vec, gn)


def _ssd_gate_bwd(dyn, y, act, p, d_vec, gn, *, name):
    l = y.shape[0]
    w = SSD_D_INNER
    tm = _tile(l, 256)

    def body(dyn_ref, y_ref, xs_ref, z_ref, d_ref, g_ref, dy_ref, dz_ref, dxs_ref, dd_ref, dg_ref):
        i = pl.program_id(0)
        for gi in range(SSD_GROUPS):
            sl = slice(gi * SSD_NORM_GROUP, (gi + 1) * SSD_NORM_GROUP)
            z, xs, dv = z_ref[:, sl], xs_ref[:, sl], d_ref[:, sl]
            s = _sigmoid(z)
            sz = z * s
            y1 = y_ref[:, sl] + dv * xs
            y2 = y1 * sz
            r = lax.rsqrt(jnp.mean(y2 * y2, axis=1, keepdims=True) + RMS_EPS)
            y2h = y2 * r
            dyn_v = dyn_ref[:, sl]
            d2h = dyn_v * g_ref[:, sl]
            dy2 = r * (d2h - y2h * jnp.mean(d2h * y2h, axis=1, keepdims=True))
            dy1 = dy2 * sz
            dy_ref[:, sl] = dy1
            dz_ref[:, sl] = dy2 * y1 * s * (1.0 + z * (1.0 - s))
            dxs_ref[:, sl] = dv * dy1
            _accumulate(dd_ref.at[:, sl], jnp.sum(dy1 * xs, axis=0, keepdims=True), i == 0)
            _accumulate(dg_ref.at[:, sl], jnp.sum(dyn_v * y2h, axis=0, keepdims=True), i == 0)

    rows = pl.BlockSpec((tm, w), lambda i: (i, 0))
    vec = pl.BlockSpec((1, w), lambda i: (0, 0))
    return pl.pallas_call(
        body, name=name, grid=(l // tm,), in_specs=[rows, rows, rows, rows, vec, vec],
        out_specs=[rows, rows, rows, vec, vec],
        out_shape=[jax.ShapeDtypeStruct((l, w), F32)] * 3 + [jax.ShapeDtypeStruct((1, w), F32)] * 2,
        compiler_params=_cparams(1),
    )(dyn, y, act, p, d_vec, gn)


def _heads_major(x):
    return x.reshape(x.shape[0], SSD_HEADS, SSD_HEAD_DIM).transpose(1, 0, 2)


def _ssd_layer_fwd(x, g, win, cw, cb, dt_bias, a_log, d_skip, gn, wout, tag):
    l = x.shape[0]
    h = _rmsnorm(x, g, name=f"{tag}_norm")
    p = _mm(h, win, tm=512, tn=896, tk=1024, name=f"{tag}_in")
    act = _ssd_conv_fwd(p, cw, cb, name=f"{tag}_conv")
    bias = jnp.pad(dt_bias, (0, 128 - SSD_HEADS)).reshape(1, 128)
    dt = _ssd_dt_fwd(p, bias, name=f"{tag}_dt")
    xh = _heads_major(act[:, :SSD_D_INNER])
    dt_t = jnp.pad(dt[:, :SSD_HEADS].T.reshape(SSD_GROUPS, SSD_HPG, l), ((0, 0), (0, 8 - SSD_HPG), (0, 0)))
    a = -jnp.exp(a_log).reshape(SSD_GROUPS, SSD_HPG, 1)
    a_b = jnp.broadcast_to(jnp.pad(a, ((0, 0), (0, 8 - SSD_HPG), (0, 0))), (SSD_GROUPS, 8, 128))
    yh, hprev = _ssd_fwd(xh, act, dt_t, a_b, name=f"{tag}_scan")
    y = yh.transpose(1, 0, 2).reshape(l, SSD_D_INNER)
    d_vec = jnp.repeat(d_skip, SSD_HEAD_DIM).reshape(1, SSD_D_INNER)
    yn = _ssd_gate_fwd(y, act, p, d_vec, gn, name=f"{tag}_gate")
    xo = _mm(yn, wout, tm=512, tn=1024, tk=2048, name=f"{tag}_out", extras=[(x, "tile")],
             epilogue=lambda acc, xt: (xt + acc,))
    return xo, (x, h, p, act, bias, xh, dt_t, a_b, hprev, y, d_vec, yn)


def _ssd_layer_bwd(dout, saved, g, win, cw, cb, gn, wout, tag):
    x, h, p, act, bias, xh, dt_t, a_b, hprev, y, d_vec, yn = saved
    l = x.shape[0]
    dyn = _mm(dout, wout, tb=True, tm=512, tn=1024, tk=1024, name=f"{tag}_dyn")
    dwout = _mm(yn, dout, ta=True, tm=1024, tn=1024, tk=512, name=f"{tag}_dwout", out_dtype=BF16)
    dy, dz, dxs_d, dd_vec, dgn = _ssd_gate_bwd(dyn, y, act, p, d_vec, gn, name=f"{tag}_gate_bwd")
    dxh, dbm, dcm, ddt_t, da_b = _ssd_bwd(xh, act, dt_t, a_b, hprev, _heads_major(dy), name=f"{tag}_scan_bwd")
    dxs = dxh.transpose(1, 0, 2).reshape(l, SSD_D_INNER) + dxs_d
    dact = jnp.concatenate([dxs, dbm, dcm], axis=1)
    dxbc, dcw8 = _ssd_conv_bwd(p, cw, cb, dact, name=f"{tag}_conv_bwd")
    ddt = jnp.pad(ddt_t[:, :SSD_HPG, :].reshape(SSD_HEADS, l).T, ((0, 0), (0, 128 - SSD_HEADS)))
    ddt_raw, dbias = _ssd_dt_bwd(p, bias, ddt, name=f"{tag}_dt_bwd")
    dp = jnp.concatenate([dz, dxbc, ddt_raw], axis=1)
    dwin = _mm(h, dp, ta=True, tm=1024, tn=896, tk=512, name=f"{tag}_dwin", out_dtype=BF16)
    dx, dg = _mm(dp, win, tb=True, tm=256, tn=1024, tk=896, name=f"{tag}_dx",
                 extras=[(x, "tile"), (g, "row"), (dout, "tile")], outs=[(F32, "tile"), (F32, "colsum")],
                 epilogue=_norm_bwd_epilogue)
    a_heads = a_b[:, :SSD_HPG, 0].reshape(SSD_HEADS)
    grads = dict(
        ssd_w_in=dwin[:, :SSD_IN_DIM], ssd_conv_w=dcw8[:4], ssd_conv_b=dcw8[4],
        ssd_dt_bias=dbias[0, :SSD_HEADS], ssd_a_log=da_b[:, :SSD_HPG, 0].reshape(SSD_HEADS) * a_heads,
        ssd_d=dd_vec.reshape(SSD_HEADS, SSD_HEAD_DIM).sum(axis=1), ssd_norm=dgn[0], ssd_w_out=dwout)
    return dx, dg, grads


def _local_step(x, tgt, w):
    row = lambda v: v.reshape(1, -1)
    saved = []
    for i in range(DEPTH):
        kind, j = i % 3, i // 3
        x, s1 = _ffn_fwd(x, row(w["ffn1_norm"][i]), w["ffn1_w_gu"][i], w["ffn1_w_down"][i], f"l{i}f1")
        gm = row(w["mix_norm"][i])
        if kind == 0:
            x, sm = _sb_layer_fwd(x, gm, w["sb_w_qkv"][j], w["sb_w_o"][j], f"l{i}sb")
        elif kind == 1:
            x, sm = _ssd_layer_fwd(x, gm, w["ssd_w_in"][j], w["ssd_conv_w"][j], row(w["ssd_conv_b"][j]),
                                   w["ssd_dt_bias"][j], w["ssd_a_log"][j], w["ssd_d"][j], row(w["ssd_norm"][j]),
                                   w["ssd_w_out"][j], f"l{i}ssd")
        else:
            x, sm = _sc_layer_fwd(x, gm, w["sc_w_in"][j], w["sc_conv_w"][j], w["sc_w_out"][j], f"l{i}sc")
        x, s2 = _ffn_fwd(x, row(w["ffn2_norm"][i]), w["ffn2_w_gu"][i], w["ffn2_w_down"][i], f"l{i}f2")
        saved.append((s1, sm, s2))

    loss, dx, dfinal = _final_loss(x, row(w["final_norm"]), tgt, name="final_loss")
    per_layer = {k: [None] * DEPTH for k in ("ffn1_norm", "ffn1_w_gu", "ffn1_w_down", "mix_norm",
                                             "ffn2_norm", "ffn2_w_gu", "ffn2_w_down")}
    sb_g = {"sb_w_qkv": [None, None], "sb_w_o": [None, None]}
    grads = {"final_norm": dfinal[0]}
    for i in reversed(range(DEPTH)):
        kind, j = i % 3, i // 3
        s1, sm, s2 = saved[i]
        dx, dg, dwgu, dwd = _ffn_bwd(dx, s2, row(w["ffn2_norm"][i]), w["ffn2_w_gu"][i], w["ffn2_w_down"][i], f"l{i}f2")
        per_layer["ffn2_norm"][i], per_layer["ffn2_w_gu"][i], per_layer["ffn2_w_down"][i] = dg[0], dwgu, dwd
        gm = row(w["mix_norm"][i])
        if kind == 0:
            dx, dg, dwqkv, dwo = _sb_layer_bwd(dx, sm, gm, w["sb_w_qkv"][j], w["sb_w_o"][j], f"l{i}sb")
            sb_g["sb_w_qkv"][j], sb_g["sb_w_o"][j] = dwqkv, dwo
        elif kind == 1:
            dx, dg, sg = _ssd_layer_bwd(dx, sm, gm, w["ssd_w_in"][j], w["ssd_conv_w"][j], row(w["ssd_conv_b"][j]),
                                        row(w["ssd_norm"][j]), w["ssd_w_out"][j], f"l{i}ssd")
            grads.update({k: v[None] for k, v in sg.items()})
        else:
            dx, dg, dwin, dcw, dwout = _sc_layer_bwd(dx, sm, gm, w["sc_w_in"][j], w["sc_conv_w"][j],
                                                     w["sc_w_out"][j], f"l{i}sc")
            grads.update(sc_w_in=dwin[None], sc_conv_w=dcw[None], sc_w_out=dwout[None])
        per_layer["mix_norm"][i] = dg[0]
        dx, dg, dwgu, dwd = _ffn_bwd(dx, s1, row(w["ffn1_norm"][i]), w["ffn1_w_gu"][i], w["ffn1_w_down"][i], f"l{i}f1")
        per_layer["ffn1_norm"][i], per_layer["ffn1_w_gu"][i], per_layer["ffn1_w_down"][i] = dg[0], dwgu, dwd
    for k, v in {**per_layer, **sb_g}.items():
        grads[k] = jnp.stack(v)
    return loss, dx, grads


_HBM = pl.BlockSpec(memory_space=pltpu.HBM)


def _remote(src, dst, send_sems, recv_sems, idx, dev):
    return pltpu.make_async_remote_copy(src_ref=src, dst_ref=dst, send_sem=send_sems.at[idx], recv_sem=recv_sems.at[idx],
                                        device_id=dev, device_id_type=pl.DeviceIdType.MESH)


def _exchange_call(body, xs, out_shapes, n_copies, name):
    n = len(xs)
    return pl.pallas_call(
        body, name=name, in_specs=[_HBM] * n, out_specs=[_HBM] * n,
        out_shape=[jax.ShapeDtypeStruct(s, x.dtype) for s, x in zip(out_shapes, xs)],
        scratch_shapes=[pltpu.SemaphoreType.DMA((n, n_copies)), pltpu.SemaphoreType.DMA((n, n_copies)),
                        pltpu.SemaphoreType.DMA((n,))],
    )(*xs)


def _gather(xs, *, name):
    n = len(xs)

    def body(*refs):
        x_refs, o_refs = refs[:n], refs[n:2 * n]
        send_sems, recv_sems, local_sems = refs[2 * n:]
        mx, my, mc = lax.axis_index("x"), lax.axis_index("y"), lax.axis_index("c")
        slot = lambda px, py, pc: 4 * px + 2 * py + pc
        me, sibling = (mx, my, mc), (mx, my, 1 - mc)
        chips = [(1 - mx, my), (mx, 1 - my), (1 - mx, 1 - my)]
        locals_, first, passed = [], [], []
        for a in range(n):
            x_ref, o_ref = x_refs[a], o_refs[a]
            locals_.append(pltpu.make_async_copy(x_ref, o_ref.at[slot(*me)], local_sems.at[a]))
            first.append(_remote(x_ref, o_ref.at[slot(*me)], send_sems, recv_sems, (a, 0), sibling))
            for j, chip in enumerate(chips):
                first.append(_remote(x_ref, o_ref.at[slot(*me)], send_sems, recv_sems, (a, 1 + j), (*chip, mc)))
        for cp in locals_ + first:
            cp.start()
        for j, chip in enumerate(chips):
            for a in range(n):
                o_ref = o_refs[a]
                landed = o_ref.at[slot(*chip, mc)]
                _remote(landed, landed, send_sems, recv_sems, (a, 1 + j), me).wait_recv()
                fwd = _remote(landed, landed, send_sems, recv_sems, (a, 4 + j), sibling)
                fwd.start()
                passed.append(fwd)
        for a in range(n):
            o_ref = o_refs[a]
            from_sib = o_ref.at[slot(*sibling)]
            _remote(from_sib, from_sib, send_sems, recv_sems, (a, 0), me).wait_recv()
            for j, chip in enumerate(chips):
                via_sib = o_ref.at[slot(*chip, 1 - mc)]
                _remote(via_sib, via_sib, send_sems, recv_sems, (a, 4 + j), me).wait_recv()
        for cp in first + passed:
            cp.wait_send()
        for cp in locals_:
            cp.wait()

    return _exchange_call(body, xs, [(N_DEV,) + x.shape for x in xs], 7, name)


def _scatter_sibling(xs, *, name):
    n = len(xs)

    def body(*refs):
        x_refs, o_refs = refs[:n], refs[n:2 * n]
        send_sems, recv_sems, _ = refs[2 * n:]
        mx, my, mc = lax.axis_index("x"), lax.axis_index("y"), lax.axis_index("c")
        sibling = (mx, my, 1 - mc)
        sends = []
        for a in range(n):
            for ch in range(4):
                sends.append(_remote(x_refs[a].at[ch, 1 - mc], o_refs[a].at[ch], send_sems, recv_sems, (a, ch), sibling))
        for cp in sends:
            cp.start()
        for cp in sends:
            cp.wait_recv()
        for cp in sends:
            cp.wait_send()

    return _exchange_call(body, xs, [(4,) + x.shape[2:] for x in xs], 4, name)


def _scatter_chips(ys, *, name):
    n = len(ys)

    def body(*refs):
        y_refs, o_refs = refs[:n], refs[n:2 * n]
        send_sems, recv_sems, local_sems = refs[2 * n:]
        mx, my, mc = lax.axis_index("x"), lax.axis_index("y"), lax.axis_index("c")
        mine = 2 * mx + my
        chips = [(1 - mx, my), (mx, 1 - my), (1 - mx, 1 - my)]
        locals_, sends, recvs = [], [], []
        for a in range(n):
            locals_.append(pltpu.make_async_copy(y_refs[a].at[mine], o_refs[a].at[mine], local_sems.at[a]))
            for j, (px, py) in enumerate(chips):
                theirs = 2 * px + py
                sends.append(_remote(y_refs[a].at[theirs], o_refs[a].at[mine], send_sems, recv_sems, (a, j), (px, py, mc)))
                recvs.append(_remote(y_refs[a].at[theirs], o_refs[a].at[theirs], send_sems, recv_sems, (a, j), (px, py, mc)))
        for cp in locals_ + sends:
            cp.start()
        for cp in recvs:
            cp.wait_recv()
        for cp in sends:
            cp.wait_send()
        for cp in locals_:
            cp.wait()

    return _exchange_call(body, ys, [y.shape for y in ys], 3, name)


def _pair_add(x, r, *, name):
    _, _, rows, c = x.shape
    tr = _tile(rows, 256, 16)

    def body(x_ref, r_ref, o_ref):
        mc = lax.axis_index("c")
        o_ref[...] = (x_ref[mc].astype(F32) + r_ref[...].astype(F32)).astype(o_ref.dtype)

    return pl.pallas_call(
        body, name=name, grid=(4, rows // tr),
        in_specs=[pl.BlockSpec((None, 2, tr, c), lambda ch, i: (ch, 0, i, 0)),
                  pl.BlockSpec((None, tr, c), lambda ch, i: (ch, i, 0))],
        out_specs=pl.BlockSpec((None, tr, c), lambda ch, i: (ch, i, 0)),
        out_shape=jax.ShapeDtypeStruct((4, rows, c), x.dtype), compiler_params=_cparams(2),
    )(x, r)


def _adamw_reduce(parts, w, m, v, *, name):
    r, c = w.shape
    n_parts = parts.shape[0]
    tr = _tile(r, 256, 16)
    bc1 = 1.0 - ADAM_B1 ** ADAM_STEP
    bc2 = 1.0 - ADAM_B2 ** ADAM_STEP

    def body(p_ref, w_ref, m_ref, v_ref, g_ref, d_ref, nm_ref, nv_ref):
        g = p_ref[0].astype(F32)
        for q in range(1, n_parts):
            g = g + p_ref[q].astype(F32)
        nm = ADAM_B1 * m_ref[...] + (1.0 - ADAM_B1) * g
        nv = ADAM_B2 * v_ref[...] + (1.0 - ADAM_B2) * (g * g)
        g_ref[...] = g
        nm_ref[...] = nm
        nv_ref[...] = nv
        d_ref[...] = -ADAM_LR * ((nm / bc1) / (jnp.sqrt(nv / bc2) + ADAM_EPS) + ADAM_WD * w_ref[...])

    blk = pl.BlockSpec((tr, c), lambda i: (i, 0))
    return pl.pallas_call(
        body, name=name, grid=(r // tr,),
        in_specs=[pl.BlockSpec((n_parts, tr, c), lambda i: (0, i, 0)), blk, blk, blk], out_specs=[blk] * 4,
        out_shape=[jax.ShapeDtypeStruct((r, c), F32)] * 4, compiler_params=_cparams(1),
    )(parts, w, m, v)


def _col_full(g):
    return g.transpose(1, 2, 0, 3).reshape(g.shape[1], g.shape[2], -1)


def _col_parts(f):
    n, k, c8 = f.shape
    return f.reshape(n, k, N_DEV, c8 // N_DEV).transpose(2, 0, 1, 3)


def _row_full(g):
    return g.transpose(1, 0, 2, 3).reshape(g.shape[1], -1, g.shape[3])


def _row_parts(f):
    n, r8, c = f.shape
    return f.reshape(n, N_DEV, r8 // N_DEV, c).transpose(1, 0, 2, 3)


def _gu_full(g):
    n, d, c = g.shape[1:]
    return g.reshape(2, 4, n, d, c).transpose(2, 0, 3, 1, 4).reshape(n, 2, d, 4 * c)


def _gu_parts(f):
    n, _, d, c4 = f.shape
    return f.reshape(n, 2, d, 4, c4 // 4).transpose(1, 3, 0, 2, 4).reshape(N_DEV, n, d, c4 // 4)


def _ssd_in_full(g):
    return jnp.pad(_col_full(g), ((0, 0), (0, 0), (0, SSD_IN_PAD - SSD_IN_DIM)))


_MATMUL_WEIGHTS = (
    ("ffn1_w_gu", _gu_full, _gu_parts), ("ffn1_w_down", _row_full, _row_parts),
    ("ffn2_w_gu", _gu_full, _gu_parts), ("ffn2_w_down", _row_full, _row_parts),
    ("sb_w_qkv", _col_full, _col_parts), ("sb_w_o", _row_full, _row_parts),
    ("ssd_w_in", _ssd_in_full, _col_parts), ("ssd_w_out", _row_full, _row_parts),
    ("sc_w_in", _col_full, _col_parts), ("sc_w_out", _row_full, _row_parts),
)
_CONV_WEIGHTS = ("ssd_conv_w", "sc_conv_w")
_REPLICATED = ("ffn1_norm", "mix_norm", "ffn2_norm", "final_norm", "ssd_conv_b", "ssd_norm",
               "ssd_dt_bias", "ssd_a_log", "ssd_d")
_ORDER = ("ffn1_norm", "ffn1_w_gu", "ffn1_w_down", "mix_norm", "ffn2_norm", "ffn2_w_gu", "ffn2_w_down",
          "sb_w_qkv", "sb_w_o", "ssd_w_in", "ssd_conv_w", "ssd_conv_b", "ssd_dt_bias", "ssd_a_log", "ssd_d",
          "ssd_norm", "ssd_w_out", "sc_w_in", "sc_conv_w", "sc_w_out", "final_norm")
_LANES = 1024


def _rows_of(a):
    flat = a.reshape(-1)
    pad = -flat.shape[0] % _LANES
    return jnp.pad(flat, (0, pad)).reshape(-1, _LANES)


def _pack_rows(arrays, mult):
    rows = [_rows_of(a) for a in arrays]
    packed = jnp.concatenate(rows, axis=0)
    pad = -packed.shape[0] % mult
    return jnp.pad(packed, ((0, pad), (0, 0))), [r.shape[0] for r in rows]


def _unpack_rows(packed, counts, shapes, lead=()):
    out, off = [], 0
    for n, shp in zip(counts, shapes):
        size = math.prod(shp)
        seg = packed[..., off:off + n, :].reshape(lead + (n * _LANES,))[..., :size]
        out.append(seg.reshape(lead + tuple(shp)))
        off += n
    return out


def kernel(x, ffn1_norm, ffn1_w_gu, ffn1_w_down, mix_norm, ffn2_norm, ffn2_w_gu, ffn2_w_down, sb_w_qkv, sb_w_o, ssd_w_in, ssd_conv_w, ssd_conv_b, ssd_dt_bias, ssd_a_log, ssd_d, ssd_norm, ssd_w_out, sc_w_in, sc_conv_w, sc_w_out, final_norm, loss_target, m_ffn1_norm, m_ffn1_w_gu, m_ffn1_w_down, m_mix_norm, m_ffn2_norm, m_ffn2_w_gu, m_ffn2_w_down, m_sb_w_qkv, m_sb_w_o, m_ssd_w_in, m_ssd_conv_w, m_ssd_conv_b, m_ssd_dt_bias, m_ssd_a_log, m_ssd_d, m_ssd_norm, m_ssd_w_out, m_sc_w_in, m_sc_conv_w, m_sc_w_out, m_final_norm, v_ffn1_norm, v_ffn1_w_gu, v_ffn1_w_down, v_mix_norm, v_ffn2_norm, v_ffn2_w_gu, v_ffn2_w_down, v_sb_w_qkv, v_sb_w_o, v_ssd_w_in, v_ssd_conv_w, v_ssd_conv_b, v_ssd_dt_bias, v_ssd_a_log, v_ssd_d, v_ssd_norm, v_ssd_w_out, v_sc_w_in, v_sc_conv_w, v_sc_w_out, v_final_norm):
    w = dict(ffn1_norm=ffn1_norm, ffn1_w_gu=ffn1_w_gu, ffn1_w_down=ffn1_w_down, mix_norm=mix_norm, ffn2_norm=ffn2_norm, ffn2_w_gu=ffn2_w_gu, ffn2_w_down=ffn2_w_down, sb_w_qkv=sb_w_qkv, sb_w_o=sb_w_o, ssd_w_in=ssd_w_in, ssd_conv_w=ssd_conv_w, ssd_conv_b=ssd_conv_b, ssd_dt_bias=ssd_dt_bias, ssd_a_log=ssd_a_log, ssd_d=ssd_d, ssd_norm=ssd_norm, ssd_w_out=ssd_w_out, sc_w_in=sc_w_in, sc_conv_w=sc_conv_w, sc_w_out=sc_w_out, final_norm=final_norm)
    mom = dict(ffn1_norm=m_ffn1_norm, ffn1_w_gu=m_ffn1_w_gu, ffn1_w_down=m_ffn1_w_down, mix_norm=m_mix_norm, ffn2_norm=m_ffn2_norm, ffn2_w_gu=m_ffn2_w_gu, ffn2_w_down=m_ffn2_w_down, sb_w_qkv=m_sb_w_qkv, sb_w_o=m_sb_w_o, ssd_w_in=m_ssd_w_in, ssd_conv_w=m_ssd_conv_w, ssd_conv_b=m_ssd_conv_b, ssd_dt_bias=m_ssd_dt_bias, ssd_a_log=m_ssd_a_log, ssd_d=m_ssd_d, ssd_norm=m_ssd_norm, ssd_w_out=m_ssd_w_out, sc_w_in=m_sc_w_in, sc_conv_w=m_sc_conv_w, sc_w_out=m_sc_w_out, final_norm=m_final_norm)
    var = dict(ffn1_norm=v_ffn1_norm, ffn1_w_gu=v_ffn1_w_gu, ffn1_w_down=v_ffn1_w_down, mix_norm=v_mix_norm, ffn2_norm=v_ffn2_norm, ffn2_w_gu=v_ffn2_w_gu, ffn2_w_down=v_ffn2_w_down, sb_w_qkv=v_sb_w_qkv, sb_w_o=v_sb_w_o, ssd_w_in=v_ssd_w_in, ssd_conv_w=v_ssd_conv_w, ssd_conv_b=v_ssd_conv_b, ssd_dt_bias=v_ssd_dt_bias, ssd_a_log=v_ssd_a_log, ssd_d=v_ssd_d, ssd_norm=v_ssd_norm, ssd_w_out=v_ssd_w_out, sc_w_in=v_sc_w_in, sc_conv_w=v_sc_conv_w, sc_w_out=v_sc_w_out, final_norm=v_final_norm)
    me = 4 * lax.axis_index("x") + 2 * lax.axis_index("y") + lax.axis_index("c")
    big = [n for n, _, _ in _MATMUL_WEIGHTS]
    two_d = lambda a: a.reshape(-1, a.shape[-1])

    gathered = _gather([two_d(w[n].astype(BF16)) for n in big] + [two_d(w[n]) for n in _CONV_WEIGHTS],
                       name="gather_weights")
    full = dict(w)
    for (n, to_full, _), g in zip(_MATMUL_WEIGHTS, gathered):
        full[n] = to_full(g.reshape((N_DEV,) + w[n].shape))
    for n, g in zip(_CONV_WEIGHTS, gathered[len(big):]):
        full[n] = _col_full(g.reshape((N_DEV,) + w[n].shape))

    loss_part, dx, grads = _local_step(x[0], loss_target[0], full)
    loss = lax.psum(loss_part[0, 0], ("x", "y", "c"))

    parts = []
    for n, _, to_parts in _MATMUL_WEIGHTS:
        g = grads[n]
        if n == "ssd_w_in":
            g = g[..., :SSD_IN_DIM]
        p8 = to_parts(g.astype(BF16))
        parts.append(p8.reshape(4, 2, -1, p8.shape[-1]))
    from_sibling = _scatter_sibling(parts, name="scatter_sibling")
    chip_sums = [_pair_add(p, r, name=f"pair_add_{n}") for n, p, r in zip(big, parts, from_sibling)]
    recv = _scatter_chips(chip_sums, name="scatter_chips")
    out_g, out_d, out_m, out_v = {}, {}, {}, {}

    def update(n, contrib):
        res = _adamw_reduce(contrib, two_d(w[n]), two_d(mom[n]), two_d(var[n]), name=f"adamw_{n}")
        out_g[n], out_d[n], out_m[n], out_v[n] = (r.reshape(w[n].shape) for r in res)

    for n, r in zip(big, recv):
        update(n, r)

    small = list(_REPLICATED) + list(_CONV_WEIGHTS)
    small_shapes = [grads[n].shape for n in small]
    spacked, scounts = _pack_rows([grads[n].astype(F32) for n in small], 8)
    sg = _unpack_rows(_gather([spacked], name="gather_small_grads")[0], scounts, small_shapes, (N_DEV,))
    sg = dict(zip(small, sg))
    rep_w, rcounts = _pack_rows([w[n] for n in _REPLICATED], 8)
    rep_m, _ = _pack_rows([mom[n] for n in _REPLICATED], 8)
    rep_v, _ = _pack_rows([var[n] for n in _REPLICATED], 8)
    rep_p = jnp.concatenate([_rows_of(sg[n].reshape(N_DEV, -1)[q]) for q in range(N_DEV) for n in _REPLICATED], axis=0)
    rep_p = rep_p.reshape(N_DEV, -1, _LANES)
    rep_p = jnp.pad(rep_p, ((0, 0), (0, rep_w.shape[0] - rep_p.shape[1]), (0, 0)))
    res = _adamw_reduce(rep_p, rep_w, rep_m, rep_v, name="adamw_replicated")
    rep_shapes = [w[n].shape for n in _REPLICATED]
    for tgt, r in zip((out_g, out_d, out_m, out_v), res):
        for n, a in zip(_REPLICATED, _unpack_rows(r, rcounts, rep_shapes)):
            tgt[n] = a
    for n in _CONV_WEIGHTS:
        c = w[n].shape[-1]
        mine = lax.dynamic_slice_in_dim(sg[n], me * c, c, axis=sg[n].ndim - 1)
        update(n, mine.reshape(N_DEV, -1, c))

    return (loss, dx[None], *[out_g[n] for n in _ORDER], *[out_d[n] for n in _ORDER],
            *[out_m[n] for n in _ORDER], *[out_v[n] for n in _ORDER])
```

```python
import functools
import math

import jax
import jax.numpy as jnp
from jax import lax
from jax.experimental import pallas as pl
from jax.experimental.pallas import tpu as pltpu

F32 = jnp.float32
BF16 = jnp.bfloat16

D_MODEL = 1024
D_FF = 2816
DEPTH = 4
N_DEV = 8
SB_HEADS = 16
SB_HEAD_DIM = 64
SB_TILE = 256
SB_HEADS_PER_STEP = 2
SSD_HEADS = 32
SSD_HEAD_DIM = 64
SSD_GROUPS = 8
SSD_HPG = 4
SSD_STATE = 128
SSD_CHUNK = 128
SSD_D_INNER = 2048
SSD_CONV_DIM = 4096
SSD_IN_DIM = 6176
SSD_IN_PAD = 6272
SSD_NORM_GROUP = 256
RMS_EPS = 1e-6
ADAM_LR = 0.001
ADAM_B1 = 0.9
ADAM_B2 = 0.999
ADAM_EPS = 1e-08
ADAM_WD = 0.01
ADAM_STEP = 10
VMEM_LIMIT = 56 * 1024 * 1024

NT = (((1,), (1,)), ((), ()))
TN = (((0,), (0,)), ((), ()))
NN = (((1,), (0,)), ((), ()))


def _cparams(n_axes):
    return pltpu.CompilerParams(dimension_semantics=("arbitrary",) * n_axes, vmem_limit_bytes=VMEM_LIMIT)


def _tile(n, want, mult=8):
    if n <= want:
        return n
    for t in range(want, 0, -1):
        if n % t == 0 and t % mult == 0:
            return t
    return n


def _sigmoid(x):
    return 1.0 / (1.0 + jnp.exp(-x))


def _dot(a, b, dn=NN):
    return lax.dot_general(a, b, dn, preferred_element_type=F32)


def _split3(x):
    x1 = x.astype(BF16)
    r1 = x - x1.astype(F32)
    x2 = r1.astype(BF16)
    x3 = (r1 - x2.astype(F32)).astype(BF16)
    return x1, x2, x3


def _dot_exact(x, t):
    x1, x2, x3 = _split3(x)
    return _dot(x1, t) + _dot(x2, t) + _dot(x3, t)


def _dot_hilo(x, t):
    x1 = x.astype(BF16)
    x2 = (x - x1.astype(F32)).astype(BF16)
    return _dot(x1, t) + _dot(x2, t)


def _mm(a, b, *, name, ta=False, tb=False, sa=False, sb=False, so=False, tm=512, tn=1024, tk=1024,
        out_dtype=F32, epilogue=None, extras=(), outs=None, pair=None):
    ash, bsh = a.shape[-2:], b.shape[-2:]
    m, k = (ash[1], ash[0]) if ta else ash
    n = bsh[0] if tb else bsh[1]
    s_n = pair or (a.shape[0] if sa else (b.shape[0] if sb else 1))
    tm, tn, tk = _tile(m, tm), _tile(n, tn, 128), _tile(k, tk, 128)
    nk = k // tk
    if outs is None:
        outs = [(out_dtype, "stile" if so else "tile")]
    if epilogue is None:
        epilogue = lambda acc: (acc,)

    if ta:
        a_blk, a_idx = (tk, tm), (lambda j, i, kk: (kk, i))
    else:
        a_blk, a_idx = (tm, tk), (lambda j, i, kk: (i, kk))
    if tb:
        b_blk, b_idx = (tn, tk), (lambda j, i, kk: (j, kk))
    else:
        b_blk, b_idx = (tk, tn), (lambda j, i, kk: (kk, j))

    def lead(blk, idx, has_s):
        if not has_s:
            return pl.BlockSpec(blk, idx)
        return pl.BlockSpec((s_n,) + blk, lambda j, i, kk: (0,) + idx(j, i, kk))

    kinds = {
        "tile": lambda: pl.BlockSpec((tm, tn), lambda j, i, kk: (i, j)),
        "stile": lambda: pl.BlockSpec((s_n, tm, tn), lambda j, i, kk: (0, i, j)),
        "row": lambda: pl.BlockSpec((1, tn), lambda j, i, kk: (0, j)),
        "colsum": lambda: pl.BlockSpec((1, tn), lambda j, i, kk: (0, j)),
    }
    shapes = {"tile": (m, n), "stile": (s_n, m, n), "colsum": (1, n)}
    in_specs = [lead(a_blk, a_idx, sa), lead(b_blk, b_idx, sb)] + [kinds[kd]() for _, kd in extras]
    out_specs = [kinds[kd]() for _, kd in outs]
    out_shape = [jax.ShapeDtypeStruct(shapes[kd], dt) for dt, kd in outs]
    n_ex, n_out = len(extras), len(outs)
    dn = ((((0,) if ta else (1,)), ((1,) if tb else (0,))), ((), ()))
    acc_shape = (s_n, tm, tn) if so else (tm, tn)

    def body(*refs):
        a_ref, b_ref = refs[0], refs[1]
        ex_refs = refs[2:2 + n_ex]
        o_refs = refs[2 + n_ex:2 + n_ex + n_out]
        acc = refs[-1]
        i = pl.program_id(1)
        kk = pl.program_id(2)

        @pl.when(kk == 0)
        def _():
            acc[...] = jnp.zeros_like(acc)

        for s in range(s_n if (sa or sb) else 1):
            av = (a_ref[s] if sa else a_ref[...]).astype(BF16)
            bv = (b_ref[s] if sb else b_ref[...]).astype(BF16)
            d = lax.dot_general(av, bv, dn, preferred_element_type=F32)
            if so:
                acc[s] += d
            else:
                acc[...] += d

        @pl.when(kk == nk - 1)
        def _():
            accv = tuple(acc[s] for s in range(s_n)) if so else acc[...]
            vals = epilogue(accv, *[r[...] for r in ex_refs])
            for (dt, kd), o_ref, val in zip(outs, o_refs, vals):
                if kd == "colsum":
                    _accumulate(o_ref, val, i == 0)
                elif kd == "stile":
                    for s in range(s_n):
                        o_ref[s] = val[s].astype(dt)
                else:
                    o_ref[...] = val.astype(dt)

    res = pl.pallas_call(
        body, name=name, grid=(n // tn, m // tm, nk),
        in_specs=in_specs, out_specs=out_specs, out_shape=out_shape,
        scratch_shapes=[pltpu.VMEM(acc_shape, F32)], compiler_params=_cparams(3),
    )(a, b, *[e for e, _ in extras])
    return res[0] if len(res) == 1 else res


def _accumulate(o_ref, val, first):
    @pl.when(first)
    def _():
        o_ref[...] = val

    @pl.when(jnp.logical_not(first))
    def _():
        o_ref[...] += val


def _rmsnorm(x, g, *, name):
    l, d = x.shape
    tm = _tile(l, 512)

    def body(x_ref, g_ref, o_ref):
        xv = x_ref[...]
        r = lax.rsqrt(jnp.mean(xv * xv, axis=1, keepdims=True) + RMS_EPS)
        o_ref[...] = (xv * r * g_ref[...]).astype(BF16)

    return pl.pallas_call(
        body, name=name, grid=(l // tm,),
        in_specs=[pl.BlockSpec((tm, d), lambda i: (i, 0)), pl.BlockSpec((1, d), lambda i: (0, 0))],
        out_specs=pl.BlockSpec((tm, d), lambda i: (i, 0)),
        out_shape=jax.ShapeDtypeStruct((l, d), BF16), compiler_params=_cparams(1),
    )(x, g)


def _norm_bwd_epilogue(dh, x, g, dres):
    r = lax.rsqrt(jnp.mean(x * x, axis=1, keepdims=True) + RMS_EPS)
    xh = x * r
    dg = jnp.sum(dh * xh, axis=0, keepdims=True)
    dxh = dh * g
    dx = r * (dxh - xh * jnp.mean(dxh * xh, axis=1, keepdims=True))
    return dres + dx, dg


def _final_loss(x, g, tgt, *, name):
    l, d = x.shape
    tm = _tile(l, 512)

    def body(x_ref, g_ref, t_ref, loss_ref, dx_ref, dg_ref):
        i = pl.program_id(0)
        xv, gv = x_ref[...], g_ref[...]
        r = lax.rsqrt(jnp.mean(xv * xv, axis=1, keepdims=True) + RMS_EPS)
        xh = xv * r
        e = xh * gv - t_ref[...]
        part = 0.5 * jnp.sum(jnp.mean(e * e, axis=1, keepdims=True), axis=0, keepdims=True)
        dy = e * (1.0 / d)
        dg = jnp.sum(dy * xh, axis=0, keepdims=True)
        dxh = dy * gv
        dx_ref[...] = r * (dxh - xh * jnp.mean(dxh * xh, axis=1, keepdims=True))
        _accumulate(dg_ref, dg, i == 0)
        _accumulate(loss_ref, jnp.broadcast_to(part, (1, 128)), i == 0)

    return pl.pallas_call(
        body, name=name, grid=(l // tm,),
        in_specs=[pl.BlockSpec((tm, d), lambda i: (i, 0)), pl.BlockSpec((1, d), lambda i: (0, 0)),
                  pl.BlockSpec((tm, d), lambda i: (i, 0))],
        out_specs=[pl.BlockSpec((1, 128), lambda i: (0, 0)), pl.BlockSpec((tm, d), lambda i: (i, 0)),
                   pl.BlockSpec((1, d), lambda i: (0, 0))],
        out_shape=[jax.ShapeDtypeStruct((1, 128), F32), jax.ShapeDtypeStruct((l, d), F32),
                   jax.ShapeDtypeStruct((1, d), F32)],
        compiler_params=_cparams(1),
    )(x, g, tgt)


def _ffn_fwd(x, g, wgu, wd, tag):
    h = _rmsnorm(x, g, name=f"{tag}_norm")

    def act(acc):
        gate, up = acc
        return acc, gate * _sigmoid(gate) * up

    gu, a = _mm(h, wgu, sb=True, so=True, tm=256, tn=1408, tk=1024, name=f"{tag}_up",
                outs=[(BF16, "stile"), (BF16, "tile")], epilogue=act)
    xo = _mm(a, wd, tm=512, tn=1024, tk=2816, name=f"{tag}_down", extras=[(x, "tile")],
             epilogue=lambda acc, xt: (xt + 0.5 * acc,))
    return xo, (x, h, gu, a)


def _ffn_bwd(dout, saved, g, wgu, wd, tag):
    x, h, gu, a = saved

    def act_bwd(acc, guv):
        da = 0.5 * acc
        gate, up = guv[0].astype(F32), guv[1].astype(F32)
        s = _sigmoid(gate)
        return ((da * up * s * (1.0 + gate * (1.0 - s)), da * gate * s),)

    dgu = _mm(dout, wd, tb=True, pair=2, tm=256, tn=1408, tk=1024, name=f"{tag}_dact", extras=[(gu, "stile")],
              outs=[(BF16, "stile")], epilogue=act_bwd)
    dwd = _mm(a, dout, ta=True, tm=1408, tn=1024, tk=512, name=f"{tag}_dwd", out_dtype=BF16,
              epilogue=lambda acc: (0.5 * acc,))
    dwgu = _mm(h, dgu, ta=True, sb=True, so=True, tm=512, tn=1408, tk=512, name=f"{tag}_dwgu", out_dtype=BF16)
    dx, dg = _mm(dgu, wgu, tb=True, sa=True, sb=True, tm=256, tn=1024, tk=1408, name=f"{tag}_dx",
                 extras=[(x, "tile"), (g, "row"), (dout, "tile")], outs=[(F32, "tile"), (F32, "colsum")],
                 epilogue=_norm_bwd_epilogue)
    return dx, dg, dwgu, dwd


def _sb_logs(z):
    lb = jnp.minimum(z, 0.0) - jnp.log(1.0 + jnp.exp(-jnp.abs(z)))
    return lb, lb - z


class _Hosted:
    def __init__(self, steps, xs, out_shapes, copies):
        self.steps, self.xs, self.n, self.copies = steps, list(xs), len(xs), copies
        self.out_shape = [jax.ShapeDtypeStruct(s, x.dtype) for s, x in zip(out_shapes, xs)]
        self.specs = [_HBM] * self.n
        self.sems = [pltpu.SemaphoreType.DMA((self.n, copies)), pltpu.SemaphoreType.DMA((self.n, copies)),
                     pltpu.SemaphoreType.DMA((self.n,))]

    def run(self, x_refs, o_refs, sems, grid):
        ids = [pl.program_id(a) for a in range(len(grid))]
        first = functools.reduce(jnp.logical_and, [p == 0 for p in ids])
        last = functools.reduce(jnp.logical_and, [p == g - 1 for p, g in zip(ids, grid)])
        start, finish = self.steps(x_refs, o_refs, *sems)
        pl.when(first)(start)
        return lambda: pl.when(last)(finish)


def _sb_fwd(q, k, v, *, name, hosted=None):
    h_n, l, dh = q.shape
    t = _tile(l, SB_TILE)
    hs = 2 * SB_HEADS_PER_STEP
    scale = dh ** -0.5
    grid = (h_n // hs, l // t)
    nh = hosted.n if hosted else 0

    def body(q_ref, k_ref, v_ref, *rest):
        o_ref = rest[nh]
        at_end = hosted.run(rest[:nh], rest[nh + 1:2 * nh + 1], rest[2 * nh + 1:], grid) if hosted else None
        i = pl.program_id(1)
        qs = [(q_ref[hh].astype(F32) * scale).astype(BF16) for hh in range(hs)]
        row = lax.broadcasted_iota(jnp.int32, (t, t), 0)
        col = lax.broadcasted_iota(jnp.int32, (t, t), 1)
        strict = col < row
        tri = strict.astype(BF16)

        def block(jb, carry, masked):
            sl = pl.ds(pl.multiple_of(jb * t, t), t)
            zs = [_dot(qs[hh], k_ref[hh, sl, :], NT) for hh in range(hs)]
            lbs, tails, sums = [], [], []
            for hh in range(hs):
                lb, lk = _sb_logs(zs[hh])
                if masked:
                    lk = jnp.where(strict, lk, 0.0)
                lbs.append(lb)
                tails.append(_dot_hilo(lk, tri))
                sums.append(jnp.sum(lk, axis=1, keepdims=True))
            out = []
            for hh in range(hs):
                c, o = carry[hh]
                att = jnp.exp(lbs[hh] + tails[hh] + c)
                if masked:
                    att = jnp.where(strict, att, 0.0)
                out.append((c + sums[hh], o + _dot(att.astype(BF16), v_ref[hh, sl, :])))
            return tuple(out)

        carry = block(i, tuple((jnp.zeros((t, 1), F32), jnp.zeros((t, dh), F32)) for _ in range(hs)), True)
        carry = lax.fori_loop(0, i, lambda it, cr: block(i - 1 - it, cr, False), carry)
        for hh in range(hs):
            o_ref[hh] = carry[hh][1].astype(o_ref.dtype)
        if hosted:
            at_end()

    qspec = pl.BlockSpec((hs, t, dh), lambda h, i: (h, i, 0))
    kspec = pl.BlockSpec((hs, l, dh), lambda h, i: (h, 0, 0))
    res = pl.pallas_call(
        body, name=name, grid=grid, in_specs=[qspec, kspec, kspec] + (hosted.specs if hosted else []),
        out_specs=[qspec] + (hosted.specs if hosted else []),
        out_shape=[jax.ShapeDtypeStruct((h_n, l, dh), BF16)] + (hosted.out_shape if hosted else []),
        scratch_shapes=hosted.sems if hosted else [], compiler_params=_cparams(2),
    )(q, k, v, *(hosted.xs if hosted else []))
    return (res[0], res[1:]) if hosted else res[0]


def _sb_bwd(q, k, v, do, *, name, hosted=None):
    h_n, l, dh = q.shape
    t = _tile(l, SB_TILE)
    nq = l // t
    hs = SB_HEADS_PER_STEP
    scale = dh ** -0.5
    grid = (h_n // hs, nq)
    nh = hosted.n if hosted else 0

    def body(q_ref, k_ref, v_ref, do_ref, *rest):
        dq_ref, dk_ref, dv_ref = rest[nh:nh + 3]
        e_scr, s_scr = rest[2 * nh + 3:2 * nh + 5]
        at_end = hosted.run(rest[:nh], rest[nh + 3:2 * nh + 3], rest[2 * nh + 5:], grid) if hosted else None
        i = pl.program_id(1)

        @pl.when(i == 0)
        def _():
            dk_ref[...] = jnp.zeros_like(dk_ref)
            dv_ref[...] = jnp.zeros_like(dv_ref)

        qs = [(q_ref[hh].astype(F32) * scale).astype(BF16) for hh in range(hs)]
        dos = [do_ref[hh] for hh in range(hs)]
        row = lax.broadcasted_iota(jnp.int32, (t, t), 0)
        col = lax.broadcasted_iota(jnp.int32, (t, t), 1)
        strict = col < row
        tri_suffix = strict.astype(BF16)
        tri_prefix = (row < col).astype(BF16)

        def sweep1(jb, cs, masked):
            sl = pl.ds(pl.multiple_of(jb * t, t), t)
            zs = [_dot(qs[hh], k_ref[hh, sl, :], NT) for hh in range(hs)]
            datts = [_dot(dos[hh], v_ref[hh, sl, :], NT) for hh in range(hs)]
            lbs, tails, out = [], [], []
            for hh in range(hs):
                lb, lk = _sb_logs(zs[hh])
                if masked:
                    lk = jnp.where(strict, lk, 0.0)
                lbs.append(lb)
                tails.append(_dot_hilo(lk, tri_suffix))
                s_scr[hh, jb] = jnp.exp(lb)
                out.append(cs[hh] + jnp.sum(lk, axis=1, keepdims=True))
            for hh in range(hs):
                att = jnp.exp(lbs[hh] + tails[hh] + cs[hh])
                if masked:
                    att = jnp.where(strict, att, 0.0)
                e_scr[hh, jb] = att * datts[hh]
                dv_ref[hh, sl, :] += _dot(att.astype(BF16), dos[hh], TN)
            return tuple(out)

        cs = sweep1(i, tuple(jnp.zeros((t, 1), F32) for _ in range(hs)), True)
        lax.fori_loop(0, i, lambda it, cc: sweep1(i - 1 - it, cc, False), cs)

        def sweep2(jb, carry, masked):
            sl = pl.ds(pl.multiple_of(jb * t, t), t)
            des = [e_scr[hh, jb] for hh in range(hs)]
            pres = [_dot_hilo(des[hh], tri_prefix) for hh in range(hs)]
            out = []
            for hh in range(hs):
                p, dq = carry[hh]
                de, sg = des[hh], s_scr[hh, jb]
                dlk = p + pres[hh]
                if masked:
                    dlk = jnp.where(strict, dlk, 0.0)
                dz = (de * (1.0 - sg) - dlk * sg).astype(BF16)
                dk_ref[hh, sl, :] += _dot(dz, qs[hh], TN)
                out.append((p + jnp.sum(de, axis=1, keepdims=True), dq + _dot(dz, k_ref[hh, sl, :])))
            return tuple(out)

        carry = lax.fori_loop(0, i, lambda jb, cr: sweep2(jb, cr, False),
                              tuple((jnp.zeros((t, 1), F32), jnp.zeros((t, dh), F32)) for _ in range(hs)))
        carry = sweep2(i, carry, True)
        for hh in range(hs):
            dq_ref[hh] = (carry[hh][1] * scale).astype(dq_ref.dtype)
        if hosted:
            at_end()

    qspec = pl.BlockSpec((hs, t, dh), lambda h, i: (h, i, 0))
    kspec = pl.BlockSpec((hs, l, dh), lambda h, i: (h, 0, 0))
    res = pl.pallas_call(
        body, name=name, grid=grid, in_specs=[qspec, kspec, kspec, qspec] + (hosted.specs if hosted else []),
        out_specs=[qspec, kspec, kspec] + (hosted.specs if hosted else []),
        out_shape=[jax.ShapeDtypeStruct((h_n, l, dh), BF16), jax.ShapeDtypeStruct((h_n, l, dh), F32),
                   jax.ShapeDtypeStruct((h_n, l, dh), F32)] + (hosted.out_shape if hosted else []),
        scratch_shapes=[pltpu.VMEM((hs, nq, t, t), F32), pltpu.VMEM((hs, nq, t, t), F32)]
        + (hosted.sems if hosted else []),
        compiler_params=_cparams(2),
    )(q, k, v, do, *(hosted.xs if hosted else []))
    return (res[0], res[1], res[2], res[3:]) if hosted else res


def _to_heads(x, n):
    l = x.shape[0]
    return x.reshape(l, n, -1, SB_HEAD_DIM).transpose(1, 2, 0, 3)


def _from_heads(x):
    return x.transpose(1, 0, 2).reshape(x.shape[1], -1)


def _sb_layer_fwd(x, g, wqkv, wo, tag, hosted=None):
    h = _rmsnorm(x, g, name=f"{tag}_norm")
    qkv = _mm(h, wqkv, tm=512, tn=1024, tk=1024, name=f"{tag}_qkv", out_dtype=BF16)
    qkv_h = _to_heads(qkv, 3)
    o_h = _sb_fwd(qkv_h[0], qkv_h[1], qkv_h[2], name=f"{tag}_attn", hosted=hosted)
    carried = None
    if hosted:
        o_h, carried = o_h
    o = _from_heads(o_h)
    xo = _mm(o, wo, tm=512, tn=1024, tk=1024, name=f"{tag}_out", extras=[(x, "tile")],
             epilogue=lambda acc, xt: (xt + acc,))
    return xo, (x, h, qkv_h, o), carried


def _sb_layer_bwd(dout, saved, g, wqkv, wo, tag, hosted=None):
    x, h, qkv_h, o = saved
    do = _mm(dout, wo, tb=True, tm=512, tn=1024, tk=1024, name=f"{tag}_do", out_dtype=BF16)
    dwo = _mm(o, dout, ta=True, tm=1024, tn=1024, tk=512, name=f"{tag}_dwo", out_dtype=BF16)
    res = _sb_bwd(qkv_h[0], qkv_h[1], qkv_h[2], _to_heads(do, 1)[0], name=f"{tag}_attn_bwd", hosted=hosted)
    dq, dk, dv = res[:3]
    carried = res[3] if hosted else None
    dqkv = jnp.concatenate([_from_heads(dq), _from_heads(dk.astype(BF16)), _from_heads(dv.astype(BF16))], axis=1)
    dwqkv = _mm(h, dqkv, ta=True, tm=1024, tn=1024, tk=512, name=f"{tag}_dwqkv", out_dtype=BF16)
    dx, dg = _mm(dqkv, wqkv, tb=True, tm=256, tn=1024, tk=1024, name=f"{tag}_dx",
                 extras=[(x, "tile"), (g, "row"), (dout, "tile")], outs=[(F32, "tile"), (F32, "colsum")],
                 epilogue=_norm_bwd_epilogue)
    return dx, dg, dwqkv, dwo, carried


def _shift_down(x, s, t_idx):
    return jnp.where(t_idx >= s, pltpu.roll(x, s, 0), 0.0)


def _shift_up(x, s, t_idx):
    n = x.shape[0]
    return jnp.where(t_idx < n - s, pltpu.roll(x, n - s, 0), 0.0)


def _sc_fwd(p, cw, *, name):
    l = p.shape[0]
    d = cw.shape[1]
    tc = 128
    nb = d // tc

    def body(b_ref, c_ref, h_ref, w_ref, o_ref):
        v = c_ref[...] * h_ref[...]
        t_idx = lax.broadcasted_iota(jnp.int32, v.shape, 0)
        u = v * w_ref[2:3, :] + _shift_down(v, 1, t_idx) * w_ref[1:2, :] + _shift_down(v, 2, t_idx) * w_ref[0:1, :]
        o_ref[...] = (b_ref[...] * u).astype(BF16)

    return pl.pallas_call(
        body, name=name, grid=(nb,),
        in_specs=[pl.BlockSpec((l, tc), lambda j: (0, j)), pl.BlockSpec((l, tc), lambda j: (0, nb + j)),
                  pl.BlockSpec((l, tc), lambda j: (0, 2 * nb + j)), pl.BlockSpec((3, tc), lambda j: (0, j))],
        out_specs=pl.BlockSpec((l, tc), lambda j: (0, j)),
        out_shape=jax.ShapeDtypeStruct((l, d), BF16), compiler_params=_cparams(1),
    )(p, p, p, cw)


def _sc_bwd(p, cw, dbu, *, name):
    l = p.shape[0]
    d = cw.shape[1]
    tc = 128
    nb = d // tc

    def body(b_ref, c_ref, h_ref, w_ref, g_ref, db_ref, dc_ref, dh_ref, dw_ref):
        cv, hv = c_ref[...], h_ref[...]
        v = cv * hv
        t_idx = lax.broadcasted_iota(jnp.int32, v.shape, 0)
        v1, v2 = _shift_down(v, 1, t_idx), _shift_down(v, 2, t_idx)
        u = v * w_ref[2:3, :] + v1 * w_ref[1:2, :] + v2 * w_ref[0:1, :]
        dbu_v = g_ref[...]
        db_ref[...] = (dbu_v * u).astype(BF16)
        du = dbu_v * b_ref[...]
        dv = du * w_ref[2:3, :] + _shift_up(du, 1, t_idx) * w_ref[1:2, :] + _shift_up(du, 2, t_idx) * w_ref[0:1, :]
        dc_ref[...] = (dv * hv).astype(BF16)
        dh_ref[...] = (dv * cv).astype(BF16)
        dw_ref[...] = jnp.zeros_like(dw_ref)
        dw_ref[0:1, :] = jnp.sum(du * v2, axis=0, keepdims=True)
        dw_ref[1:2, :] = jnp.sum(du * v1, axis=0, keepdims=True)
        dw_ref[2:3, :] = jnp.sum(du * v, axis=0, keepdims=True)

    col = lambda off: pl.BlockSpec((l, tc), lambda j: (0, off + j))
    return pl.pallas_call(
        body, name=name, grid=(nb,),
        in_specs=[col(0), col(nb), col(2 * nb), pl.BlockSpec((3, tc), lambda j: (0, j)), col(0)],
        out_specs=[col(0), col(0), col(0), pl.BlockSpec((8, tc), lambda j: (0, j))],
        out_shape=[jax.ShapeDtypeStruct((l, d), BF16)] * 3 + [jax.ShapeDtypeStruct((8, d), F32)],
        compiler_params=_cparams(1),
    )(p, p, p, cw, dbu)


def _sc_layer_fwd(x, g, win, cw, wout, tag):
    h = _rmsnorm(x, g, name=f"{tag}_norm")
    p = _mm(h, win, tm=512, tn=1024, tk=1024, name=f"{tag}_in")
    bu = _sc_fwd(p, cw, name=f"{tag}_conv")
    xo = _mm(bu, wout, tm=512, tn=1024, tk=1024, name=f"{tag}_out", extras=[(x, "tile")],
             epilogue=lambda acc, xt: (xt + acc,))
    return xo, (x, h, p, bu)


def _sc_layer_bwd(dout, saved, g, win, cw, wout, tag):
    x, h, p, bu = saved
    dbu = _mm(dout, wout, tb=True, tm=512, tn=1024, tk=1024, name=f"{tag}_dbu")
    dwout = _mm(bu, dout, ta=True, tm=1024, tn=1024, tk=512, name=f"{tag}_dwout", out_dtype=BF16)
    db, dc, dh, dcw = _sc_bwd(p, cw, dbu, name=f"{tag}_conv_bwd")
    dp = jnp.concatenate([db, dc, dh], axis=1)
    dwin = _mm(h, dp, ta=True, tm=1024, tn=1024, tk=512, name=f"{tag}_dwin", out_dtype=BF16)
    dx, dg = _mm(dp, win, tb=True, tm=256, tn=1024, tk=1024, name=f"{tag}_dx",
                 extras=[(x, "tile"), (g, "row"), (dout, "tile")], outs=[(F32, "tile"), (F32, "colsum")],
                 epilogue=_norm_bwd_epilogue)
    return dx, dg, dwin, dcw[:3], dwout


def _ssd_conv_fwd(p, cw, cb, *, name):
    l = p.shape[0]
    tc = 128
    nb = SSD_CONV_DIM // tc
    off = SSD_D_INNER // tc

    def body(x_ref, w_ref, b_ref, o_ref):
        xv = x_ref[...]
        t_idx = lax.broadcasted_iota(jnp.int32, xv.shape, 0)
        pre = xv * w_ref[3:4, :] + b_ref[...]
        for s in (1, 2, 3):
            pre = pre + _shift_down(xv, s, t_idx) * w_ref[3 - s:4 - s, :]
        o_ref[...] = pre * _sigmoid(pre)

    return pl.pallas_call(
        body, name=name, grid=(nb,),
        in_specs=[pl.BlockSpec((l, tc), lambda j: (0, off + j)), pl.BlockSpec((4, tc), lambda j: (0, j)),
                  pl.BlockSpec((1, tc), lambda j: (0, j))],
        out_specs=pl.BlockSpec((l, tc), lambda j: (0, j)),
        out_shape=jax.ShapeDtypeStruct((l, SSD_CONV_DIM), F32), compiler_params=_cparams(1),
    )(p, cw, cb)


def _ssd_conv_bwd(p, cw, cb, dact, *, name):
    l = p.shape[0]
    tc = 128
    nb = SSD_CONV_DIM // tc
    off = SSD_D_INNER // tc

    def body(x_ref, w_ref, b_ref, g_ref, dx_ref, dw_ref):
        xv = x_ref[...]
        t_idx = lax.broadcasted_iota(jnp.int32, xv.shape, 0)
        xs = [xv] + [_shift_down(xv, s, t_idx) for s in (1, 2, 3)]
        pre = b_ref[...] + xs[0] * w_ref[3:4, :]
        for s in (1, 2, 3):
            pre = pre + xs[s] * w_ref[3 - s:4 - s, :]
        sg = _sigmoid(pre)
        dpre = g_ref[...] * sg * (1.0 + pre * (1.0 - sg))
        dx = dpre * w_ref[3:4, :]
        for s in (1, 2, 3):
            dx = dx + _shift_up(dpre, s, t_idx) * w_ref[3 - s:4 - s, :]
        dx_ref[...] = dx
        dw_ref[...] = jnp.zeros_like(dw_ref)
        for s in (0, 1, 2, 3):
            dw_ref[3 - s:4 - s, :] = jnp.sum(dpre * xs[s], axis=0, keepdims=True)
        dw_ref[4:5, :] = jnp.sum(dpre, axis=0, keepdims=True)

    return pl.pallas_call(
        body, name=name, grid=(nb,),
        in_specs=[pl.BlockSpec((l, tc), lambda j: (0, off + j)), pl.BlockSpec((4, tc), lambda j: (0, j)),
                  pl.BlockSpec((1, tc), lambda j: (0, j)), pl.BlockSpec((l, tc), lambda j: (0, j))],
        out_specs=[pl.BlockSpec((l, tc), lambda j: (0, j)), pl.BlockSpec((8, tc), lambda j: (0, j))],
        out_shape=[jax.ShapeDtypeStruct((l, SSD_CONV_DIM), F32), jax.ShapeDtypeStruct((8, SSD_CONV_DIM), F32)],
        compiler_params=_cparams(1),
    )(p, cw, cb, dact)


def _ssd_dt_fwd(p, bias, *, name):
    l = p.shape[0]
    tm = _tile(l, 1024)
    off = (SSD_D_INNER + SSD_CONV_DIM) // 128

    def body(x_ref, b_ref, o_ref):
        v = x_ref[...] + b_ref[...]
        o_ref[...] = jnp.maximum(v, 0.0) + jnp.log(1.0 + jnp.exp(-jnp.abs(v)))

    return pl.pallas_call(
        body, name=name, grid=(l // tm,),
        in_specs=[pl.BlockSpec((tm, 128), lambda i: (i, off)), pl.BlockSpec((1, 128), lambda i: (0, 0))],
        out_specs=pl.BlockSpec((tm, 128), lambda i: (i, 0)),
        out_shape=jax.ShapeDtypeStruct((l, 128), F32), compiler_params=_cparams(1),
    )(p, bias)


def _ssd_dt_bwd(p, bias, ddt, *, name):
    l = p.shape[0]
    tm = _tile(l, 1024)
    off = (SSD_D_INNER + SSD_CONV_DIM) // 128

    def body(x_ref, b_ref, g_ref, o_ref, db_ref):
        i = pl.program_id(0)
        d = g_ref[...] * _sigmoid(x_ref[...] + b_ref[...])
        o_ref[...] = d
        _accumulate(db_ref, jnp.sum(d, axis=0, keepdims=True), i == 0)

    return pl.pallas_call(
        body, name=name, grid=(l // tm,),
        in_specs=[pl.BlockSpec((tm, 128), lambda i: (i, off)), pl.BlockSpec((1, 128), lambda i: (0, 0)),
                  pl.BlockSpec((tm, 128), lambda i: (i, 0))],
        out_specs=[pl.BlockSpec((tm, 128), lambda i: (i, 0)), pl.BlockSpec((1, 128), lambda i: (0, 0))],
        out_shape=[jax.ShapeDtypeStruct((l, 128), F32), jax.ShapeDtypeStruct((1, 128), F32)],
        compiler_params=_cparams(1),
    )(p, bias, ddt)


def _row_to_col(r, eye):
    return jnp.sum(jnp.where(eye, r, 0.0), axis=1, keepdims=True)


def _col_to_row(c, eye):
    return jnp.sum(jnp.where(eye, c, 0.0), axis=0, keepdims=True)


def _ssd_chunk_common(b_ref, c_ref, dt_ref, a_ref, lam_scr):
    n = SSD_CHUNK
    row = lax.broadcasted_iota(jnp.int32, (n, n), 0)
    col = lax.broadcasted_iota(jnp.int32, (n, n), 1)
    bm, cm = b_ref[...].astype(BF16), c_ref[...].astype(BF16)
    g = _dot(cm, bm, NT)
    incl = (row <= col).astype(BF16)
    lam_scr[...] = _dot_exact(dt_ref[...] * a_ref[...], incl)
    return row, col, bm, cm, g


def _ssd_head_common(r, row, col, dt_ref, lam_scr):
    eye, tril = row == col, row >= col
    lam_r = lam_scr[r:r + 1, :]
    dt_r = dt_ref[r:r + 1, :]
    lam_c = _row_to_col(lam_r, eye)
    dt_c = _row_to_col(dt_r, eye)
    dk = jnp.where(tril, jnp.exp(jnp.minimum(lam_c - lam_r, 0.0)), 0.0)
    lam_last = jnp.sum(jnp.where(col[0:1, :] == SSD_CHUNK - 1, lam_r, 0.0), axis=1, keepdims=True)
    return eye, lam_r, dt_r, lam_c, dt_c, dk, lam_last


def _ssd_fwd(xh, act, dt_t, a_b, *, name):
    l = xh.shape[1]
    nc = l // SSD_CHUNK
    n, p_dim, hpg = SSD_CHUNK, SSD_HEAD_DIM, SSD_HPG

    def body(x_ref, b_ref, c_ref, dt_ref, a_ref, y_ref, hp_ref, h_scr, lam_scr):
        @pl.when(pl.program_id(1) == 0)
        def _():
            h_scr[...] = jnp.zeros_like(h_scr)

        row, col, bm, cm, g = _ssd_chunk_common(b_ref, c_ref, dt_ref, a_ref, lam_scr)
        for r in range(hpg):
            _, _, dt_r, lam_c, dt_c, dk, lam_last = _ssd_head_common(r, row, col, dt_ref, lam_scr)
            xr = x_ref[r]
            hr = h_scr[r]
            w = (g * dk * dt_r).astype(BF16)
            y = _dot(w, xr.astype(BF16)) + _dot(cm, hr.astype(BF16), NT) * jnp.exp(lam_c)
            y_ref[r] = y
            hp_ref[r] = hr
            xw = (xr * (jnp.exp(lam_last - lam_c) * dt_c)).astype(BF16)
            h_scr[r] = jnp.exp(lam_last) * hr + _dot(xw, bm, TN)

    g_off = SSD_D_INNER // SSD_STATE
    return pl.pallas_call(
        body, name=name, grid=(SSD_GROUPS, nc),
        in_specs=[pl.BlockSpec((hpg, n, p_dim), lambda g, c: (g, c, 0)),
                  pl.BlockSpec((n, SSD_STATE), lambda g, c: (c, g_off + g)),
                  pl.BlockSpec((n, SSD_STATE), lambda g, c: (c, g_off + SSD_GROUPS + g)),
                  pl.BlockSpec((None, 8, n), lambda g, c: (g, 0, c)),
                  pl.BlockSpec((None, 8, 128), lambda g, c: (g, 0, 0))],
        out_specs=[pl.BlockSpec((hpg, n, p_dim), lambda g, c: (g, c, 0)),
                   pl.BlockSpec((None, hpg, p_dim, SSD_STATE), lambda g, c: (c, g, 0, 0))],
        out_shape=[jax.ShapeDtypeStruct(xh.shape, F32),
                   jax.ShapeDtypeStruct((nc, SSD_HEADS, p_dim, SSD_STATE), F32)],
        scratch_shapes=[pltpu.VMEM((hpg, p_dim, SSD_STATE), F32), pltpu.VMEM((8, n), F32)],
        compiler_params=_cparams(2),
    )(xh, act, act, dt_t, a_b)


def _ssd_bwd(xh, act, dt_t, a_b, hprev, dyh, *, name):
    l = xh.shape[1]
    nc = l // SSD_CHUNK
    n, p_dim, hpg = SSD_CHUNK, SSD_HEAD_DIM, SSD_HPG

    def body(x_ref, b_ref, c_ref, dt_ref, a_ref, hp_ref, dy_ref,
             dx_ref, db_ref, dc_ref, ddt_ref, da_ref, dh_scr, lam_scr, dlam_scr, ddt_scr):
        ci = pl.program_id(1)

        @pl.when(ci == 0)
        def _():
            dh_scr[...] = jnp.zeros_like(dh_scr)

        row, col, bm, cm, g = _ssd_chunk_common(b_ref, c_ref, dt_ref, a_ref, lam_scr)
        dlam_scr[...] = jnp.zeros_like(dlam_scr)
        ddt_scr[...] = jnp.zeros_like(ddt_scr)
        dg_acc = jnp.zeros((n, n), F32)
        dc_acc = jnp.zeros((n, SSD_STATE), F32)
        db_acc = jnp.zeros((n, SSD_STATE), F32)
        for r in range(hpg):
            eye, _, dt_r, lam_c, dt_c, dk, lam_last = _ssd_head_common(r, row, col, dt_ref, lam_scr)
            xr, dyr, hr, dhr = x_ref[r], dy_ref[r], hp_ref[r], dh_scr[r]
            xb, dyb, hb, dhb = xr.astype(BF16), dyr.astype(BF16), hr.astype(BF16), dhr.astype(BF16)
            e_l = jnp.exp(lam_c)
            e_last = jnp.exp(lam_last)
            decay_c = jnp.exp(lam_last - lam_c)
            w_c = decay_c * dt_c
            m = g * dk * dt_r
            dm = _dot(dyb, xb, NT)
            bdh = _dot(bm, dhb, NT)
            dx_ref[r] = _dot(m.astype(BF16), dyb, TN) + w_c * bdh
            dg_acc = dg_acc + dm * dk * dt_r
            q_mat = dm * g * dk
            p_mat = q_mat * dt_r
            yoff = _dot(cm, hb, NT) * e_l
            q_c = jnp.sum(xr * bdh, axis=1, keepdims=True)
            dlam_c = (jnp.sum(p_mat, axis=1, keepdims=True) + jnp.sum(dyr * yoff, axis=1, keepdims=True)
                      - w_c * q_c)
            d_last = (jnp.sum(w_c * q_c, axis=0, keepdims=True)
                      + e_last * jnp.sum(jnp.sum(dhr * hr, axis=1, keepdims=True), axis=0, keepdims=True))
            dlam_scr[r:r + 1, :] = (_col_to_row(dlam_c, eye) - jnp.sum(p_mat, axis=0, keepdims=True)
                                    + jnp.where(col[0:1, :] == n - 1, d_last, 0.0))
            ddt_scr[r:r + 1, :] = jnp.sum(q_mat, axis=0, keepdims=True) + _col_to_row(decay_c * q_c, eye)
            dc_acc = dc_acc + e_l * _dot(dyb, hb)
            db_acc = db_acc + _dot((xr * w_c).astype(BF16), dhb)
            dh_scr[r] = e_last * dhr + _dot((dyr * e_l).astype(BF16), cm, TN)

        dgb = dg_acc.astype(BF16)
        dc_ref[...] = _dot(dgb, bm) + dc_acc
        db_ref[...] = _dot(dgb, cm, TN) + db_acc
        rev = (row >= col).astype(BF16)
        da = _dot_exact(dlam_scr[...], rev)
        ddt_ref[...] = ddt_scr[...] + da * a_ref[...]
        _accumulate(da_ref, da * dt_ref[...], ci == 0)

        @pl.when(ci == nc - 1)
        def _():
            da_ref[...] = jnp.broadcast_to(jnp.sum(da_ref[...], axis=1, keepdims=True), da_ref.shape)

    g_off = SSD_D_INNER // SSD_STATE
    rc = lambda c: nc - 1 - c
    hspec = pl.BlockSpec((hpg, n, p_dim), lambda g, c: (g, rc(c), 0))
    gspec = pl.BlockSpec((n, SSD_STATE), lambda g, c: (rc(c), g))
    return pl.pallas_call(
        body, name=name, grid=(SSD_GROUPS, nc),
        in_specs=[hspec,
                  pl.BlockSpec((n, SSD_STATE), lambda g, c: (rc(c), g_off + g)),
                  pl.BlockSpec((n, SSD_STATE), lambda g, c: (rc(c), g_off + SSD_GROUPS + g)),
                  pl.BlockSpec((None, 8, n), lambda g, c: (g, 0, rc(c))),
                  pl.BlockSpec((None, 8, 128), lambda g, c: (g, 0, 0)),
                  pl.BlockSpec((None, hpg, p_dim, SSD_STATE), lambda g, c: (rc(c), g, 0, 0)),
                  hspec],
        out_specs=[hspec, gspec, gspec,
                   pl.BlockSpec((None, 8, n), lambda g, c: (g, 0, rc(c))),
                   pl.BlockSpec((None, 8, 128), lambda g, c: (g, 0, 0))],
        out_shape=[jax.ShapeDtypeStruct(xh.shape, F32),
                   jax.ShapeDtypeStruct((l, SSD_GROUPS * SSD_STATE), F32),
                   jax.ShapeDtypeStruct((l, SSD_GROUPS * SSD_STATE), F32),
                   jax.ShapeDtypeStruct(dt_t.shape, F32),
                   jax.ShapeDtypeStruct(a_b.shape, F32)],
        scratch_shapes=[pltpu.VMEM((hpg, p_dim, SSD_STATE), F32), pltpu.VMEM((8, n), F32),
                        pltpu.VMEM((8, n), F32), pltpu.VMEM((8, n), F32)],
        compiler_params=_cparams(2),
    )(xh, act, act, dt_t, a_b, hprev, dyh)


def _ssd_gate_fwd(y, act, p, d_vec, gn, *, name):
    l = y.shape[0]
    w = SSD_D_INNER
    tm = _tile(l, 256)

    def body(y_ref, xs_ref, z_ref, d_ref, g_ref, o_ref):
        for gi in range(SSD_GROUPS):
            sl = slice(gi * SSD_NORM_GROUP, (gi + 1) * SSD_NORM_GROUP)
            z = z_ref[:, sl]
            y2 = (y_ref[:, sl] + d_ref[:, sl] * xs_ref[:, sl]) * (z * _sigmoid(z))
            r = lax.rsqrt(jnp.mean(y2 * y2, axis=1, keepdims=True) + RMS_EPS)
            o_ref[:, sl] = (y2 * r * g_ref[:, sl]).astype(BF16)

    rows = pl.BlockSpec((tm, w), lambda i: (i, 0))
    vec = pl.BlockSpec((1, w), lambda i: (0, 0))
    return pl.pallas_call(
        body, name=name, grid=(l // tm,), in_specs=[rows, rows, rows, vec, vec], out_specs=rows,
        out_shape=jax.ShapeDtypeStruct((l, w), BF16), compiler_params=_cparams(1),
    )(y, act, p, d_vec, gn)


def _ssd_gate_bwd(dyn, y, act, p, d_vec, gn, *, name):
    l = y.shape[0]
    w = SSD_D_INNER
    tm = _tile(l, 256)

    def body(dyn_ref, y_ref, xs_ref, z_ref, d_ref, g_ref, dy_ref, dz_ref, dxs_ref, dd_ref, dg_ref):
        i = pl.program_id(0)
        for gi in range(SSD_GROUPS):
            sl = slice(gi * SSD_NORM_GROUP, (gi + 1) * SSD_NORM_GROUP)
            z, xs, dv = z_ref[:, sl], xs_ref[:, sl], d_ref[:, sl]
            s = _sigmoid(z)
            sz = z * s
            y1 = y_ref[:, sl] + dv * xs
            y2 = y1 * sz
            r = lax.rsqrt(jnp.mean(y2 * y2, axis=1, keepdims=True) + RMS_EPS)
            y2h = y2 * r
            dyn_v = dyn_ref[:, sl]
            d2h = dyn_v * g_ref[:, sl]
            dy2 = r * (d2h - y2h * jnp.mean(d2h * y2h, axis=1, keepdims=True))
            dy1 = dy2 * sz
            dy_ref[:, sl] = dy1
            dz_ref[:, sl] = dy2 * y1 * s * (1.0 + z * (1.0 - s))
            dxs_ref[:, sl] = dv * dy1
            _accumulate(dd_ref.at[:, sl], jnp.sum(dy1 * xs, axis=0, keepdims=True), i == 0)
            _accumulate(dg_ref.at[:, sl], jnp.sum(dyn_v * y2h, axis=0, keepdims=True), i == 0)

    rows = pl.BlockSpec((tm, w), lambda i: (i, 0))
    vec = pl.BlockSpec((1, w), lambda i: (0, 0))
    return pl.pallas_call(
        body, name=name, grid=(l // tm,), in_specs=[rows, rows, rows, rows, vec, vec],
        out_specs=[rows, rows, rows, vec, vec],
        out_shape=[jax.ShapeDtypeStruct((l, w), F32)] * 3 + [jax.ShapeDtypeStruct((1, w), F32)] * 2,
        compiler_params=_cparams(1),
    )(dyn, y, act, p, d_vec, gn)


def _heads_major(x):
    return x.reshape(x.shape[0], SSD_HEADS, SSD_HEAD_DIM).transpose(1, 0, 2)


def _ssd_layer_fwd(x, g, win, cw, cb, dt_bias, a_log, d_skip, gn, wout, tag):
    l = x.shape[0]
    h = _rmsnorm(x, g, name=f"{tag}_norm")
    p = _mm(h, win, tm=512, tn=896, tk=1024, name=f"{tag}_in")
    act = _ssd_conv_fwd(p, cw, cb, name=f"{tag}_conv")
    bias = jnp.pad(dt_bias, (0, 128 - SSD_HEADS)).reshape(1, 128)
    dt = _ssd_dt_fwd(p, bias, name=f"{tag}_dt")
    xh = _heads_major(act[:, :SSD_D_INNER])
    dt_t = jnp.pad(dt[:, :SSD_HEADS].T.reshape(SSD_GROUPS, SSD_HPG, l), ((0, 0), (0, 8 - SSD_HPG), (0, 0)))
    a = -jnp.exp(a_log).reshape(SSD_GROUPS, SSD_HPG, 1)
    a_b = jnp.broadcast_to(jnp.pad(a, ((0, 0), (0, 8 - SSD_HPG), (0, 0))), (SSD_GROUPS, 8, 128))
    yh, hprev = _ssd_fwd(xh, act, dt_t, a_b, name=f"{tag}_scan")
    y = yh.transpose(1, 0, 2).reshape(l, SSD_D_INNER)
    d_vec = jnp.repeat(d_skip, SSD_HEAD_DIM).reshape(1, SSD_D_INNER)
    yn = _ssd_gate_fwd(y, act, p, d_vec, gn, name=f"{tag}_gate")
    xo = _mm(yn, wout, tm=512, tn=1024, tk=2048, name=f"{tag}_out", extras=[(x, "tile")],
             epilogue=lambda acc, xt: (xt + acc,))
    return xo, (x, h, p, act, bias, xh, dt_t, a_b, hprev, y, d_vec, yn)


def _ssd_layer_bwd(dout, saved, g, win, cw, cb, gn, wout, tag):
    x, h, p, act, bias, xh, dt_t, a_b, hprev, y, d_vec, yn = saved
    l = x.shape[0]
    dyn = _mm(dout, wout, tb=True, tm=512, tn=1024, tk=1024, name=f"{tag}_dyn")
    dwout = _mm(yn, dout, ta=True, tm=1024, tn=1024, tk=512, name=f"{tag}_dwout", out_dtype=BF16)
    dy, dz, dxs_d, dd_vec, dgn = _ssd_gate_bwd(dyn, y, act, p, d_vec, gn, name=f"{tag}_gate_bwd")
    dxh, dbm, dcm, ddt_t, da_b = _ssd_bwd(xh, act, dt_t, a_b, hprev, _heads_major(dy), name=f"{tag}_scan_bwd")
    dxs = dxh.transpose(1, 0, 2).reshape(l, SSD_D_INNER) + dxs_d
    dact = jnp.concatenate([dxs, dbm, dcm], axis=1)
    dxbc, dcw8 = _ssd_conv_bwd(p, cw, cb, dact, name=f"{tag}_conv_bwd")
    ddt = jnp.pad(ddt_t[:, :SSD_HPG, :].reshape(SSD_HEADS, l).T, ((0, 0), (0, 128 - SSD_HEADS)))
    ddt_raw, dbias = _ssd_dt_bwd(p, bias, ddt, name=f"{tag}_dt_bwd")
    dp = jnp.concatenate([dz, dxbc, ddt_raw], axis=1)
    dwin = _mm(h, dp, ta=True, tm=1024, tn=896, tk=512, name=f"{tag}_dwin", out_dtype=BF16)
    dx, dg = _mm(dp, win, tb=True, tm=256, tn=1024, tk=896, name=f"{tag}_dx",
                 extras=[(x, "tile"), (g, "row"), (dout, "tile")], outs=[(F32, "tile"), (F32, "colsum")],
                 epilogue=_norm_bwd_epilogue)
    a_heads = a_b[:, :SSD_HPG, 0].reshape(SSD_HEADS)
    grads = dict(
        ssd_w_in=dwin[:, :SSD_IN_DIM], ssd_conv_w=dcw8[:4], ssd_conv_b=dcw8[4],
        ssd_dt_bias=dbias[0, :SSD_HEADS], ssd_a_log=da_b[:, :SSD_HPG, 0].reshape(SSD_HEADS) * a_heads,
        ssd_d=dd_vec.reshape(SSD_HEADS, SSD_HEAD_DIM).sum(axis=1), ssd_norm=dgn[0], ssd_w_out=dwout)
    return dx, dg, grads


def _local_step(x, tgt, w, gather_later=None, scatter_early=None):
    row = lambda v: v.reshape(1, -1)
    saved = []
    for i in range(DEPTH):
        kind, j = i % 3, i // 3
        x, s1 = _ffn_fwd(x, row(w["ffn1_norm"][i]), w["ffn1_w_gu"][i], w["ffn1_w_down"][i], f"l{i}f1")
        gm = row(w["mix_norm"][i])
        if kind == 0:
            hosted = gather_later[0] if (gather_later and i == 0) else None
            x, sm, carried = _sb_layer_fwd(x, gm, w["sb_w_qkv"][j], w["sb_w_o"][j], f"l{i}sb", hosted=hosted)
            if hosted:
                w = gather_later[1](w, carried)
        elif kind == 1:
            x, sm = _ssd_layer_fwd(x, gm, w["ssd_w_in"][j], w["ssd_conv_w"][j], row(w["ssd_conv_b"][j]),
                                   w["ssd_dt_bias"][j], w["ssd_a_log"][j], w["ssd_d"][j], row(w["ssd_norm"][j]),
                                   w["ssd_w_out"][j], f"l{i}ssd")
        else:
            x, sm = _sc_layer_fwd(x, gm, w["sc_w_in"][j], w["sc_conv_w"][j], w["sc_w_out"][j], f"l{i}sc")
        x, s2 = _ffn_fwd(x, row(w["ffn2_norm"][i]), w["ffn2_w_gu"][i], w["ffn2_w_down"][i], f"l{i}f2")
        saved.append((s1, sm, s2))

    loss, dx, dfinal = _final_loss(x, row(w["final_norm"]), tgt, name="final_loss")
    per_layer = {k: [None] * DEPTH for k in ("ffn1_norm", "ffn1_w_gu", "ffn1_w_down", "mix_norm",
                                             "ffn2_norm", "ffn2_w_gu", "ffn2_w_down")}
    per_layer.update({"sb_w_qkv": [None, None], "sb_w_o": [None, None]})
    grads = {"final_norm": dfinal[0]}
    early = None
    for i in reversed(range(DEPTH)):
        kind, j = i % 3, i // 3
        s1, sm, s2 = saved[i]
        dx, dg, dwgu, dwd = _ffn_bwd(dx, s2, row(w["ffn2_norm"][i]), w["ffn2_w_gu"][i], w["ffn2_w_down"][i], f"l{i}f2")
        per_layer["ffn2_norm"][i], per_layer["ffn2_w_gu"][i], per_layer["ffn2_w_down"][i] = dg[0], dwgu, dwd
        gm = row(w["mix_norm"][i])
        if kind == 0:
            hosted = scatter_early({**grads, **per_layer}) if (scatter_early and i == 0) else None
            dx, dg, dwqkv, dwo, carried = _sb_layer_bwd(dx, sm, gm, w["sb_w_qkv"][j], w["sb_w_o"][j], f"l{i}sb",
                                                        hosted=hosted)
            per_layer["sb_w_qkv"][j], per_layer["sb_w_o"][j] = dwqkv, dwo
            if hosted:
                early = carried
        elif kind == 1:
            dx, dg, sg = _ssd_layer_bwd(dx, sm, gm, w["ssd_w_in"][j], w["ssd_conv_w"][j], row(w["ssd_conv_b"][j]),
                                        row(w["ssd_norm"][j]), w["ssd_w_out"][j], f"l{i}ssd")
            sg["ssd_w_in"], sg["ssd_w_out"] = [sg["ssd_w_in"]], [sg["ssd_w_out"]]
            grads.update({k: (v if isinstance(v, list) else v[None]) for k, v in sg.items()})
        else:
            dx, dg, dwin, dcw, dwout = _sc_layer_bwd(dx, sm, gm, w["sc_w_in"][j], w["sc_conv_w"][j],
                                                     w["sc_w_out"][j], f"l{i}sc")
            grads.update(sc_w_in=[dwin], sc_conv_w=dcw[None], sc_w_out=[dwout])
        per_layer["mix_norm"][i] = dg[0]
        dx, dg, dwgu, dwd = _ffn_bwd(dx, s1, row(w["ffn1_norm"][i]), w["ffn1_w_gu"][i], w["ffn1_w_down"][i], f"l{i}f1")
        per_layer["ffn1_norm"][i], per_layer["ffn1_w_gu"][i], per_layer["ffn1_w_down"][i] = dg[0], dwgu, dwd
    for k, v in per_layer.items():
        grads[k] = jnp.stack(v) if k.endswith("_norm") else v
    return loss, dx, grads, early


_HBM = pl.BlockSpec(memory_space=pltpu.HBM)


def _remote(src, dst, send_sems, recv_sems, idx, dev):
    return pltpu.make_async_remote_copy(src_ref=src, dst_ref=dst, send_sem=send_sems.at[idx], recv_sem=recv_sems.at[idx],
                                        device_id=dev, device_id_type=pl.DeviceIdType.MESH)


def _exchange_call(body, xs, out_shapes, n_copies, name):
    n = len(xs)
    return pl.pallas_call(
        body, name=name, in_specs=[_HBM] * n, out_specs=[_HBM] * n,
        out_shape=[jax.ShapeDtypeStruct(s, x.dtype) for s, x in zip(out_shapes, xs)],
        scratch_shapes=[pltpu.SemaphoreType.DMA((n, n_copies)), pltpu.SemaphoreType.DMA((n, n_copies)),
                        pltpu.SemaphoreType.DMA((n,))],
    )(*xs)


def _gather(xs, *, name):
    n = len(xs)

    def body(*refs):
        start, finish = _gather_steps(refs[:n], refs[n:2 * n], *refs[2 * n:])
        start()
        finish()

    return _exchange_call(body, xs, _gather_shapes(xs), _GATHER_COPIES, name)


_GATHER_COPIES = 7


def _gather_shapes(xs):
    return [(N_DEV,) + x.shape for x in xs]


def _gather_steps(x_refs, o_refs, send_sems, recv_sems, local_sems):
    n = len(x_refs)

    def plan():
        mx, my, mc = lax.axis_index("x"), lax.axis_index("y"), lax.axis_index("c")
        slot = lambda px, py, pc: 4 * px + 2 * py + pc
        me, sibling = (mx, my, mc), (mx, my, 1 - mc)
        chips = [(1 - mx, my), (mx, 1 - my), (1 - mx, 1 - my)]
        locals_, first = [], []
        for a in range(n):
            x_ref, o_ref = x_refs[a], o_refs[a]
            locals_.append(pltpu.make_async_copy(x_ref, o_ref.at[slot(*me)], local_sems.at[a]))
            first.append(_remote(x_ref, o_ref.at[slot(*me)], send_sems, recv_sems, (a, 0), sibling))
            for j, chip in enumerate(chips):
                first.append(_remote(x_ref, o_ref.at[slot(*me)], send_sems, recv_sems, (a, 1 + j), (*chip, mc)))
        return locals_, first, slot, me, sibling, chips, mc

    def start():
        locals_, first = plan()[:2]
        for cp in locals_ + first:
            cp.start()

    def finish():
        locals_, first, slot, me, sibling, chips, mc = plan()
        passed = []
        for j, chip in enumerate(chips):
            for a in range(n):
                landed = o_refs[a].at[slot(*chip, mc)]
                _remote(landed, landed, send_sems, recv_sems, (a, 1 + j), me).wait_recv()
                fwd = _remote(landed, landed, send_sems, recv_sems, (a, 4 + j), sibling)
                fwd.start()
                passed.append(fwd)
        for a in range(n):
            from_sib = o_refs[a].at[slot(*sibling)]
            _remote(from_sib, from_sib, send_sems, recv_sems, (a, 0), me).wait_recv()
            for j, chip in enumerate(chips):
                via_sib = o_refs[a].at[slot(*chip, 1 - mc)]
                _remote(via_sib, via_sib, send_sems, recv_sems, (a, 4 + j), me).wait_recv()
        for cp in first + passed:
            cp.wait_send()
        for cp in locals_:
            cp.wait()

    return start, finish


def _scatter_sibling(xs, *, name):
    n = len(xs)

    def body(*refs):
        x_refs, o_refs = refs[:n], refs[n:2 * n]
        send_sems, recv_sems, _ = refs[2 * n:]
        mx, my, mc = lax.axis_index("x"), lax.axis_index("y"), lax.axis_index("c")
        sibling = (mx, my, 1 - mc)
        sends = []
        for a in range(n):
            for ch in range(4):
                sends.append(_remote(x_refs[a].at[ch, 1 - mc], o_refs[a].at[ch], send_sems, recv_sems, (a, ch), sibling))
        for cp in sends:
            cp.start()
        for cp in sends:
            cp.wait_recv()
        for cp in sends:
            cp.wait_send()

    return _exchange_call(body, xs, [(4,) + x.shape[2:] for x in xs], 4, name)


def _scatter_chips(ys, *, name):
    n = len(ys)

    def body(*refs):
        start, finish = _chip_scatter_steps(refs[:n], refs[n:2 * n], *refs[2 * n:])
        start()
        finish()

    return _exchange_call(body, ys, _chip_scatter_shapes(ys), _CHIP_SCATTER_COPIES, name)


_CHIP_SCATTER_COPIES = 3


def _chip_scatter_shapes(ys):
    return [y.shape for y in ys]


def _chip_scatter_steps(y_refs, o_refs, send_sems, recv_sems, local_sems):
    n = len(y_refs)

    def plan():
        mx, my, mc = lax.axis_index("x"), lax.axis_index("y"), lax.axis_index("c")
        mine = 2 * mx + my
        chips = [(1 - mx, my), (mx, 1 - my), (1 - mx, 1 - my)]
        locals_, sends, recvs = [], [], []
        for a in range(n):
            locals_.append(pltpu.make_async_copy(y_refs[a].at[mine], o_refs[a].at[mine], local_sems.at[a]))
            for j, (px, py) in enumerate(chips):
                theirs = 2 * px + py
                sends.append(_remote(y_refs[a].at[theirs], o_refs[a].at[mine], send_sems, recv_sems, (a, j), (px, py, mc)))
                recvs.append(_remote(y_refs[a].at[theirs], o_refs[a].at[theirs], send_sems, recv_sems, (a, j), (px, py, mc)))
        return locals_, sends, recvs

    def start():
        locals_, sends, _ = plan()
        for cp in locals_ + sends:
            cp.start()

    def finish():
        locals_, sends, recvs = plan()
        for cp in recvs:
            cp.wait_recv()
        for cp in sends:
            cp.wait_send()
        for cp in locals_:
            cp.wait()

    return start, finish


def _pair_add(x, r, *, name):
    _, _, rows, c = x.shape
    tr = _tile(rows, 256, 16)

    def body(x_ref, r_ref, o_ref):
        mc = lax.axis_index("c")
        o_ref[...] = (x_ref[mc].astype(F32) + r_ref[...].astype(F32)).astype(o_ref.dtype)

    return pl.pallas_call(
        body, name=name, grid=(4, rows // tr),
        in_specs=[pl.BlockSpec((None, 2, tr, c), lambda ch, i: (ch, 0, i, 0)),
                  pl.BlockSpec((None, tr, c), lambda ch, i: (ch, i, 0))],
        out_specs=pl.BlockSpec((None, tr, c), lambda ch, i: (ch, i, 0)),
        out_shape=jax.ShapeDtypeStruct((4, rows, c), x.dtype), compiler_params=_cparams(2),
    )(x, r)


def _adamw_reduce(parts, w, m, v, *, name):
    r, c = w.shape
    n_parts = parts.shape[0]
    tr = _tile(r, 256, 16)
    bc1 = 1.0 - ADAM_B1 ** ADAM_STEP
    bc2 = 1.0 - ADAM_B2 ** ADAM_STEP

    def body(p_ref, w_ref, m_ref, v_ref, g_ref, d_ref, nm_ref, nv_ref):
        g = p_ref[0].astype(F32)
        for q in range(1, n_parts):
            g = g + p_ref[q].astype(F32)
        nm = ADAM_B1 * m_ref[...] + (1.0 - ADAM_B1) * g
        nv = ADAM_B2 * v_ref[...] + (1.0 - ADAM_B2) * (g * g)
        g_ref[...] = g
        nm_ref[...] = nm
        nv_ref[...] = nv
        d_ref[...] = -ADAM_LR * ((nm / bc1) / (jnp.sqrt(nv / bc2) + ADAM_EPS) + ADAM_WD * w_ref[...])

    blk = pl.BlockSpec((tr, c), lambda i: (i, 0))
    return pl.pallas_call(
        body, name=name, grid=(r // tr,),
        in_specs=[pl.BlockSpec((n_parts, tr, c), lambda i: (0, i, 0)), blk, blk, blk], out_specs=[blk] * 4,
        out_shape=[jax.ShapeDtypeStruct((r, c), F32)] * 4, compiler_params=_cparams(1),
    )(parts, w, m, v)


def _col_full(g):
    return g.transpose(1, 2, 0, 3).reshape(g.shape[1], g.shape[2], -1)


def _col_parts(f):
    n, k, c8 = f.shape
    return f.reshape(n, k, N_DEV, c8 // N_DEV).transpose(2, 0, 1, 3)


def _row_full(g):
    return g.transpose(1, 0, 2, 3).reshape(g.shape[1], -1, g.shape[3])


def _row_parts(f):
    n, r8, c = f.shape
    return f.reshape(n, N_DEV, r8 // N_DEV, c).transpose(1, 0, 2, 3)


def _gu_full(g):
    n, d, c = g.shape[1:]
    return g.reshape(2, 4, n, d, c).transpose(2, 0, 3, 1, 4).reshape(n, 2, d, 4 * c)


def _gu_parts(f):
    n, _, d, c4 = f.shape
    return f.reshape(n, 2, d, 4, c4 // 4).transpose(1, 3, 0, 2, 4).reshape(N_DEV, n, d, c4 // 4)


def _ssd_in_full(g):
    return jnp.pad(_col_full(g), ((0, 0), (0, 0), (0, SSD_IN_PAD - SSD_IN_DIM)))


_MATMUL_WEIGHTS = (
    ("ffn1_w_gu", _gu_full, _gu_parts), ("ffn1_w_down", _row_full, _row_parts),
    ("ffn2_w_gu", _gu_full, _gu_parts), ("ffn2_w_down", _row_full, _row_parts),
    ("sb_w_qkv", _col_full, _col_parts), ("sb_w_o", _row_full, _row_parts),
    ("ssd_w_in", _ssd_in_full, _col_parts), ("ssd_w_out", _row_full, _row_parts),
    ("sc_w_in", _col_full, _col_parts), ("sc_w_out", _row_full, _row_parts),
)
_FIRST_WEIGHTS = ("ffn1_w_gu", "ffn1_w_down", "sb_w_qkv", "sb_w_o")
_CONV_WEIGHTS = ("ssd_conv_w", "sc_conv_w")
_REPLICATED = ("ffn1_norm", "mix_norm", "ffn2_norm", "final_norm", "ssd_conv_b", "ssd_norm",
               "ssd_dt_bias", "ssd_a_log", "ssd_d")
_ORDER = ("ffn1_norm", "ffn1_w_gu", "ffn1_w_down", "mix_norm", "ffn2_norm", "ffn2_w_gu", "ffn2_w_down",
          "sb_w_qkv", "sb_w_o", "ssd_w_in", "ssd_conv_w", "ssd_conv_b", "ssd_dt_bias", "ssd_a_log", "ssd_d",
          "ssd_norm", "ssd_w_out", "sc_w_in", "sc_conv_w", "sc_w_out", "final_norm")
_LANES = 1024


def _rows_of(a):
    flat = a.reshape(-1)
    pad = -flat.shape[0] % _LANES
    return jnp.pad(flat, (0, pad)).reshape(-1, _LANES)


def _pack_rows(arrays, mult):
    rows = [_rows_of(a) for a in arrays]
    packed = jnp.concatenate(rows, axis=0)
    pad = -packed.shape[0] % mult
    return jnp.pad(packed, ((0, pad), (0, 0))), [r.shape[0] for r in rows]


def _unpack_rows(packed, counts, shapes, lead=()):
    out, off = [], 0
    for n, shp in zip(counts, shapes):
        size = math.prod(shp)
        seg = packed[..., off:off + n, :].reshape(lead + (n * _LANES,))[..., :size]
        out.append(seg.reshape(lead + tuple(shp)))
        off += n
    return out


def kernel(x, ffn1_norm, ffn1_w_gu, ffn1_w_down, mix_norm, ffn2_norm, ffn2_w_gu, ffn2_w_down, sb_w_qkv, sb_w_o, ssd_w_in, ssd_conv_w, ssd_conv_b, ssd_dt_bias, ssd_a_log, ssd_d, ssd_norm, ssd_w_out, sc_w_in, sc_conv_w, sc_w_out, final_norm, loss_target, m_ffn1_norm, m_ffn1_w_gu, m_ffn1_w_down, m_mix_norm, m_ffn2_norm, m_ffn2_w_gu, m_ffn2_w_down, m_sb_w_qkv, m_sb_w_o, m_ssd_w_in, m_ssd_conv_w, m_ssd_conv_b, m_ssd_dt_bias, m_ssd_a_log, m_ssd_d, m_ssd_norm, m_ssd_w_out, m_sc_w_in, m_sc_conv_w, m_sc_w_out, m_final_norm, v_ffn1_norm, v_ffn1_w_gu, v_ffn1_w_down, v_mix_norm, v_ffn2_norm, v_ffn2_w_gu, v_ffn2_w_down, v_sb_w_qkv, v_sb_w_o, v_ssd_w_in, v_ssd_conv_w, v_ssd_conv_b, v_ssd_dt_bias, v_ssd_a_log, v_ssd_d, v_ssd_norm, v_ssd_w_out, v_sc_w_in, v_sc_conv_w, v_sc_w_out, v_final_norm):
    w = dict(ffn1_norm=ffn1_norm, ffn1_w_gu=ffn1_w_gu, ffn1_w_down=ffn1_w_down, mix_norm=mix_norm, ffn2_norm=ffn2_norm, ffn2_w_gu=ffn2_w_gu, ffn2_w_down=ffn2_w_down, sb_w_qkv=sb_w_qkv, sb_w_o=sb_w_o, ssd_w_in=ssd_w_in, ssd_conv_w=ssd_conv_w, ssd_conv_b=ssd_conv_b, ssd_dt_bias=ssd_dt_bias, ssd_a_log=ssd_a_log, ssd_d=ssd_d, ssd_norm=ssd_norm, ssd_w_out=ssd_w_out, sc_w_in=sc_w_in, sc_conv_w=sc_conv_w, sc_w_out=sc_w_out, final_norm=final_norm)
    mom = dict(ffn1_norm=m_ffn1_norm, ffn1_w_gu=m_ffn1_w_gu, ffn1_w_down=m_ffn1_w_down, mix_norm=m_mix_norm, ffn2_norm=m_ffn2_norm, ffn2_w_gu=m_ffn2_w_gu, ffn2_w_down=m_ffn2_w_down, sb_w_qkv=m_sb_w_qkv, sb_w_o=m_sb_w_o, ssd_w_in=m_ssd_w_in, ssd_conv_w=m_ssd_conv_w, ssd_conv_b=m_ssd_conv_b, ssd_dt_bias=m_ssd_dt_bias, ssd_a_log=m_ssd_a_log, ssd_d=m_ssd_d, ssd_norm=m_ssd_norm, ssd_w_out=m_ssd_w_out, sc_w_in=m_sc_w_in, sc_conv_w=m_sc_conv_w, sc_w_out=m_sc_w_out, final_norm=m_final_norm)
    var = dict(ffn1_norm=v_ffn1_norm, ffn1_w_gu=v_ffn1_w_gu, ffn1_w_down=v_ffn1_w_down, mix_norm=v_mix_norm, ffn2_norm=v_ffn2_norm, ffn2_w_gu=v_ffn2_w_gu, ffn2_w_down=v_ffn2_w_down, sb_w_qkv=v_sb_w_qkv, sb_w_o=v_sb_w_o, ssd_w_in=v_ssd_w_in, ssd_conv_w=v_ssd_conv_w, ssd_conv_b=v_ssd_conv_b, ssd_dt_bias=v_ssd_dt_bias, ssd_a_log=v_ssd_a_log, ssd_d=v_ssd_d, ssd_norm=v_ssd_norm, ssd_w_out=v_ssd_w_out, sc_w_in=v_sc_w_in, sc_conv_w=v_sc_conv_w, sc_w_out=v_sc_w_out, final_norm=v_final_norm)
    me = 4 * lax.axis_index("x") + 2 * lax.axis_index("y") + lax.axis_index("c")
    big = [n for n, _, _ in _MATMUL_WEIGHTS]
    two_d = lambda a: a.reshape(-1, a.shape[-1])

    to_full = {n: f for n, f, _ in _MATMUL_WEIGHTS}
    to_parts = {n: f for n, _, f in _MATMUL_WEIGHTS}
    first = [(n, (0,)) for n in _FIRST_WEIGHTS]
    later = [(n, tuple(range(1 if n in _FIRST_WEIGHTS else 0, w[n].shape[0]))) for n in big]

    def shards(group):
        return [two_d(w[n][idx[0]:idx[-1] + 1].astype(BF16)) for n, idx in group]

    def layers(group, gathered):
        out = {}
        for (n, idx), g in zip(group, gathered):
            f = to_full[n](g.reshape((N_DEV, len(idx)) + w[n].shape[1:]))
            out[n] = [f[i] for i in range(len(idx))]
        return out

    gathered = _gather(shards(first) + [two_d(w[n]) for n in _CONV_WEIGHTS], name="gather_first")
    full = dict(w)
    full.update(layers(first, gathered))
    for n, g in zip(_CONV_WEIGHTS, gathered[len(first):]):
        full[n] = _col_full(g.reshape((N_DEV,) + w[n].shape))

    def with_later(wd, gathered_later):
        wd = dict(wd)
        for n, ls in layers(later, gathered_later).items():
            wd[n] = (wd[n] if n in _FIRST_WEIGHTS else []) + ls
        return wd

    later_shards = shards(later)
    gather_later = (_Hosted(_gather_steps, later_shards, _gather_shapes(later_shards), _GATHER_COPIES), with_later)

    def chip_sums(group, grads, tag):
        parts = []
        for n, idx in group:
            p8 = to_parts[n](jnp.stack([grads[n][i] for i in idx]).astype(BF16))
            parts.append(p8.reshape(4, 2, -1, p8.shape[-1]))
        from_sibling = _scatter_sibling(parts, name=f"scatter_sibling_{tag}")
        return [_pair_add(p, r, name=f"pair_add_{tag}_{n}") for (n, _), p, r in zip(group, parts, from_sibling)]

    def scatter_early(grads):
        ys = chip_sums(later, grads, "later")
        return _Hosted(_chip_scatter_steps, ys, _chip_scatter_shapes(ys), _CHIP_SCATTER_COPIES)

    loss_part, dx, grads, recv_later = _local_step(x[0], loss_target[0], full, gather_later, scatter_early)
    loss = lax.psum(loss_part[0, 0], ("x", "y", "c"))

    recv_first = _scatter_chips(chip_sums(first, grads, "first"), name="scatter_chips_first")
    contrib = {n: [r] for (n, _), r in zip(first, recv_first)}
    for (n, _), r in zip(later, recv_later):
        contrib.setdefault(n, []).append(r)
    out_g, out_d, out_m, out_v = {}, {}, {}, {}

    def update(n, parts):
        res = _adamw_reduce(parts, two_d(w[n]), two_d(mom[n]), two_d(var[n]), name=f"adamw_{n}")
        out_g[n], out_d[n], out_m[n], out_v[n] = (r.reshape(w[n].shape) for r in res)

    for n in big:
        update(n, contrib[n][0] if len(contrib[n]) == 1 else jnp.concatenate(contrib[n], axis=1))

    small = list(_REPLICATED) + list(_CONV_WEIGHTS)
    small_shapes = [grads[n].shape for n in small]
    spacked, scounts = _pack_rows([grads[n].astype(F32) for n in small], 8)
    sg = _unpack_rows(_gather([spacked], name="gather_small_grads")[0], scounts, small_shapes, (N_DEV,))
    sg = dict(zip(small, sg))
    rep_w, rcounts = _pack_rows([w[n] for n in _REPLICATED], 8)
    rep_m, _ = _pack_rows([mom[n] for n in _REPLICATED], 8)
    rep_v, _ = _pack_rows([var[n] for n in _REPLICATED], 8)
    rep_p = jnp.concatenate([_rows_of(sg[n].reshape(N_DEV, -1)[q]) for q in range(N_DEV) for n in _REPLICATED], axis=0)
    rep_p = rep_p.reshape(N_DEV, -1, _LANES)
    rep_p = jnp.pad(rep_p, ((0, 0), (0, rep_w.shape[0] - rep_p.shape[1]), (0, 0)))
    res = _adamw_reduce(rep_p, rep_w, rep_m, rep_v, name="adamw_replicated")
    rep_shapes = [w[n].shape for n in _REPLICATED]
    for tgt, r in zip((out_g, out_d, out_m, out_v), res):
        for n, a in zip(_REPLICATED, _unpack_rows(r, rcounts, rep_shapes)):
            tgt[n] = a
    for n in _CONV_WEIGHTS:
        c = w[n].shape[-1]
        mine = lax.dynamic_slice_in_dim(sg[n], me * c, c, axis=sg[n].ndim - 1)
        update(n, mine.reshape(N_DEV, -1, c))

    return (loss, dx[None], *[out_g[n] for n in _ORDER], *[out_d[n] for n in _ORDER],
            *[out_m[n] for n in _ORDER], *[out_v[n] for n in _ORDER])
```

```python
import functools
import math

import jax
import jax.numpy as jnp
from jax import lax
from jax.experimental import pallas as pl
from jax.experimental.pallas import tpu as pltpu

F32 = jnp.float32
BF16 = jnp.bfloat16

D_MODEL = 1024
D_FF = 2816
DEPTH = 4
N_DEV = 8
SB_HEADS = 16
SB_HEAD_DIM = 64
SB_TILE = 256
SB_HEADS_PER_STEP = 2
SB_FWD_GROUPS = (2, 1)
SB_BWD_GROUPS = (4, 2, 1)
SSD_HEADS = 32
SSD_HEAD_DIM = 64
SSD_GROUPS = 8
SSD_HPG = 4
SSD_STATE = 128
SSD_CHUNK = 128
SSD_D_INNER = 2048
SSD_CONV_DIM = 4096
SSD_IN_DIM = 6176
SSD_IN_PAD = 6272
SSD_NORM_GROUP = 256
RMS_EPS = 1e-6
ADAM_LR = 0.001
ADAM_B1 = 0.9
ADAM_B2 = 0.999
ADAM_EPS = 1e-08
ADAM_WD = 0.01
ADAM_STEP = 10
VMEM_LIMIT = 56 * 1024 * 1024

NT = (((1,), (1,)), ((), ()))
TN = (((0,), (0,)), ((), ()))
NN = (((1,), (0,)), ((), ()))


def _cparams(n_axes):
    return pltpu.CompilerParams(dimension_semantics=("arbitrary",) * n_axes, vmem_limit_bytes=VMEM_LIMIT)


def _tile(n, want, mult=8):
    if n <= want:
        return n
    for t in range(want, 0, -1):
        if n % t == 0 and t % mult == 0:
            return t
    return n


def _sigmoid(x):
    return 1.0 / (1.0 + jnp.exp(-x))


def _dot(a, b, dn=NN):
    return lax.dot_general(a, b, dn, preferred_element_type=F32)


def _split3(x):
    x1 = x.astype(BF16)
    r1 = x - x1.astype(F32)
    x2 = r1.astype(BF16)
    x3 = (r1 - x2.astype(F32)).astype(BF16)
    return x1, x2, x3


def _dot_exact(x, t):
    x1, x2, x3 = _split3(x)
    return _dot(x1, t) + _dot(x2, t) + _dot(x3, t)


def _dot_hilo(x, t):
    x1 = x.astype(BF16)
    x2 = (x - x1.astype(F32)).astype(BF16)
    return _dot(x1, t) + _dot(x2, t)


def _mm(a, b, *, name, ta=False, tb=False, sa=False, sb=False, so=False, tm=512, tn=1024, tk=1024,
        out_dtype=F32, epilogue=None, extras=(), outs=None, pair=None):
    ash, bsh = a.shape[-2:], b.shape[-2:]
    m, k = (ash[1], ash[0]) if ta else ash
    n = bsh[0] if tb else bsh[1]
    s_n = pair or (a.shape[0] if sa else (b.shape[0] if sb else 1))
    tm, tn, tk = _tile(m, tm), _tile(n, tn, 128), _tile(k, tk, 128)
    nk = k // tk
    if outs is None:
        outs = [(out_dtype, "stile" if so else "tile")]
    if epilogue is None:
        epilogue = lambda acc: (acc,)

    if ta:
        a_blk, a_idx = (tk, tm), (lambda j, i, kk: (kk, i))
    else:
        a_blk, a_idx = (tm, tk), (lambda j, i, kk: (i, kk))
    if tb:
        b_blk, b_idx = (tn, tk), (lambda j, i, kk: (j, kk))
    else:
        b_blk, b_idx = (tk, tn), (lambda j, i, kk: (kk, j))

    def lead(blk, idx, has_s):
        if not has_s:
            return pl.BlockSpec(blk, idx)
        return pl.BlockSpec((s_n,) + blk, lambda j, i, kk: (0,) + idx(j, i, kk))

    kinds = {
        "tile": lambda: pl.BlockSpec((tm, tn), lambda j, i, kk: (i, j)),
        "stile": lambda: pl.BlockSpec((s_n, tm, tn), lambda j, i, kk: (0, i, j)),
        "row": lambda: pl.BlockSpec((1, tn), lambda j, i, kk: (0, j)),
        "colsum": lambda: pl.BlockSpec((1, tn), lambda j, i, kk: (0, j)),
    }
    shapes = {"tile": (m, n), "stile": (s_n, m, n), "colsum": (1, n)}
    in_specs = [lead(a_blk, a_idx, sa), lead(b_blk, b_idx, sb)] + [kinds[kd]() for _, kd in extras]
    out_specs = [kinds[kd]() for _, kd in outs]
    out_shape = [jax.ShapeDtypeStruct(shapes[kd], dt) for dt, kd in outs]
    n_ex, n_out = len(extras), len(outs)
    dn = ((((0,) if ta else (1,)), ((1,) if tb else (0,))), ((), ()))
    acc_shape = (s_n, tm, tn) if so else (tm, tn)

    def body(*refs):
        a_ref, b_ref = refs[0], refs[1]
        ex_refs = refs[2:2 + n_ex]
        o_refs = refs[2 + n_ex:2 + n_ex + n_out]
        acc = refs[-1]
        i = pl.program_id(1)
        kk = pl.program_id(2)

        @pl.when(kk == 0)
        def _():
            acc[...] = jnp.zeros_like(acc)

        for s in range(s_n if (sa or sb) else 1):
            av = (a_ref[s] if sa else a_ref[...]).astype(BF16)
            bv = (b_ref[s] if sb else b_ref[...]).astype(BF16)
            d = lax.dot_general(av, bv, dn, preferred_element_type=F32)
            if so:
                acc[s] += d
            else:
                acc[...] += d

        @pl.when(kk == nk - 1)
        def _():
            accv = tuple(acc[s] for s in range(s_n)) if so else acc[...]
            vals = epilogue(accv, *[r[...] for r in ex_refs])
            for (dt, kd), o_ref, val in zip(outs, o_refs, vals):
                if kd == "colsum":
                    _accumulate(o_ref, val, i == 0)
                elif kd == "stile":
                    for s in range(s_n):
                        o_ref[s] = val[s].astype(dt)
                else:
                    o_ref[...] = val.astype(dt)

    res = pl.pallas_call(
        body, name=name, grid=(n // tn, m // tm, nk),
        in_specs=in_specs, out_specs=out_specs, out_shape=out_shape,
        scratch_shapes=[pltpu.VMEM(acc_shape, F32)], compiler_params=_cparams(3),
    )(a, b, *[e for e, _ in extras])
    return res[0] if len(res) == 1 else res


def _accumulate(o_ref, val, first):
    @pl.when(first)
    def _():
        o_ref[...] = val

    @pl.when(jnp.logical_not(first))
    def _():
        o_ref[...] += val


def _rmsnorm(x, g, *, name):
    l, d = x.shape
    tm = _tile(l, 512)

    def body(x_ref, g_ref, o_ref):
        xv = x_ref[...]
        r = lax.rsqrt(jnp.mean(xv * xv, axis=1, keepdims=True) + RMS_EPS)
        o_ref[...] = (xv * r * g_ref[...]).astype(BF16)

    return pl.pallas_call(
        body, name=name, grid=(l // tm,),
        in_specs=[pl.BlockSpec((tm, d), lambda i: (i, 0)), pl.BlockSpec((1, d), lambda i: (0, 0))],
        out_specs=pl.BlockSpec((tm, d), lambda i: (i, 0)),
        out_shape=jax.ShapeDtypeStruct((l, d), BF16), compiler_params=_cparams(1),
    )(x, g)


def _norm_bwd_epilogue(dh, x, g, dres):
    r = lax.rsqrt(jnp.mean(x * x, axis=1, keepdims=True) + RMS_EPS)
    xh = x * r
    dg = jnp.sum(dh * xh, axis=0, keepdims=True)
    dxh = dh * g
    dx = r * (dxh - xh * jnp.mean(dxh * xh, axis=1, keepdims=True))
    return dres + dx, dg


def _final_loss(x, g, tgt, *, name):
    l, d = x.shape
    tm = _tile(l, 512)

    def body(x_ref, g_ref, t_ref, loss_ref, dx_ref, dg_ref):
        i = pl.program_id(0)
        xv, gv = x_ref[...], g_ref[...]
        r = lax.rsqrt(jnp.mean(xv * xv, axis=1, keepdims=True) + RMS_EPS)
        xh = xv * r
        e = xh * gv - t_ref[...]
        part = 0.5 * jnp.sum(jnp.mean(e * e, axis=1, keepdims=True), axis=0, keepdims=True)
        dy = e * (1.0 / d)
        dg = jnp.sum(dy * xh, axis=0, keepdims=True)
        dxh = dy * gv
        dx_ref[...] = r * (dxh - xh * jnp.mean(dxh * xh, axis=1, keepdims=True))
        _accumulate(dg_ref, dg, i == 0)
        _accumulate(loss_ref, jnp.broadcast_to(part, (1, 128)), i == 0)

    return pl.pallas_call(
        body, name=name, grid=(l // tm,),
        in_specs=[pl.BlockSpec((tm, d), lambda i: (i, 0)), pl.BlockSpec((1, d), lambda i: (0, 0)),
                  pl.BlockSpec((tm, d), lambda i: (i, 0))],
        out_specs=[pl.BlockSpec((1, 128), lambda i: (0, 0)), pl.BlockSpec((tm, d), lambda i: (i, 0)),
                   pl.BlockSpec((1, d), lambda i: (0, 0))],
        out_shape=[jax.ShapeDtypeStruct((1, 128), F32), jax.ShapeDtypeStruct((l, d), F32),
                   jax.ShapeDtypeStruct((1, d), F32)],
        compiler_params=_cparams(1),
    )(x, g, tgt)


def _ffn_fwd(x, g, wgu, wd, tag):
    h = _rmsnorm(x, g, name=f"{tag}_norm")

    def act(acc):
        gate, up = acc
        return acc, gate * _sigmoid(gate) * up

    gu, a = _mm(h, wgu, sb=True, so=True, tm=256, tn=1408, tk=1024, name=f"{tag}_up",
                outs=[(BF16, "stile"), (BF16, "tile")], epilogue=act)
    xo = _mm(a, wd, tm=512, tn=1024, tk=2816, name=f"{tag}_down", extras=[(x, "tile")],
             epilogue=lambda acc, xt: (xt + 0.5 * acc,))
    return xo, (x, h, gu, a)


def _ffn_bwd(dout, saved, g, wgu, wd, tag):
    x, h, gu, a = saved

    def act_bwd(acc, guv):
        da = 0.5 * acc
        gate, up = guv[0].astype(F32), guv[1].astype(F32)
        s = _sigmoid(gate)
        return ((da * up * s * (1.0 + gate * (1.0 - s)), da * gate * s),)

    dgu = _mm(dout, wd, tb=True, pair=2, tm=256, tn=1408, tk=1024, name=f"{tag}_dact", extras=[(gu, "stile")],
              outs=[(BF16, "stile")], epilogue=act_bwd)
    dwd = _mm(a, dout, ta=True, tm=1408, tn=1024, tk=512, name=f"{tag}_dwd", out_dtype=BF16,
              epilogue=lambda acc: (0.5 * acc,))
    dwgu = _mm(h, dgu, ta=True, sb=True, so=True, tm=512, tn=1408, tk=512, name=f"{tag}_dwgu", out_dtype=BF16)
    dx, dg = _mm(dgu, wgu, tb=True, sa=True, sb=True, tm=256, tn=1024, tk=1408, name=f"{tag}_dx",
                 extras=[(x, "tile"), (g, "row"), (dout, "tile")], outs=[(F32, "tile"), (F32, "colsum")],
                 epilogue=_norm_bwd_epilogue)
    return dx, dg, dwgu, dwd


def _sb_plan(n, sizes):
    digits = [n // sizes[0]] + [(n // s) % 2 for s in sizes[1:]]
    plan, none_smaller = [], 1
    for size, d in reversed(list(zip(sizes, digits))):
        has = jnp.minimum(d, 1)
        with_diag = none_smaller * has
        plan.append((size, True, with_diag))
        if size > 1:
            plan.append((size, False, d - with_diag))
        none_smaller = none_smaller * (1 - has)
    return plan


def _sb_sweep(plan, start, step, fn, carry):
    pos = start
    for size, with_diag, trips in plan:
        diag = 0 if step < 0 else size - 1

        def trip(it, cr, size=size, with_diag=with_diag, pos=pos, diag=diag):
            base = pos + step * size * it
            return fn([base + step * b for b in range(size)], cr, [with_diag and b == diag for b in range(size)])

        carry = lax.fori_loop(0, trips, trip, carry)
        pos = pos + step * size * trips
    return carry


def _sb_logs(z):
    lb = jnp.minimum(z, 0.0) - jnp.log(1.0 + jnp.exp(-jnp.abs(z)))
    return lb, lb - z


class _Hosted:
    def __init__(self, steps, xs, out_shapes, copies):
        self.steps, self.xs, self.n, self.copies = steps, list(xs), len(xs), copies
        self.out_shape = [jax.ShapeDtypeStruct(s, x.dtype) for s, x in zip(out_shapes, xs)]
        self.specs = [_HBM] * self.n
        self.sems = [pltpu.SemaphoreType.DMA((self.n, copies)), pltpu.SemaphoreType.DMA((self.n, copies)),
                     pltpu.SemaphoreType.DMA((self.n,))]

    def run(self, x_refs, o_refs, sems, grid):
        ids = [pl.program_id(a) for a in range(len(grid))]
        first = functools.reduce(jnp.logical_and, [p == 0 for p in ids])
        last = functools.reduce(jnp.logical_and, [p == g - 1 for p, g in zip(ids, grid)])
        start, finish = self.steps(x_refs, o_refs, *sems)
        pl.when(first)(start)
        return lambda: pl.when(last)(finish)


def _sb_fwd(q, k, v, *, name, hosted=None):
    h_n, l, dh = q.shape
    t = _tile(l, SB_TILE)
    hs = 2 * SB_HEADS_PER_STEP
    scale = dh ** -0.5
    grid = (h_n // hs, l // t)
    nh = hosted.n if hosted else 0

    def body(q_ref, k_ref, v_ref, *rest):
        o_ref = rest[nh]
        at_end = hosted.run(rest[:nh], rest[nh + 1:2 * nh + 1], rest[2 * nh + 1:], grid) if hosted else None
        i = pl.program_id(1)
        qs = [(q_ref[hh].astype(F32) * scale).astype(BF16) for hh in range(hs)]
        row = lax.broadcasted_iota(jnp.int32, (t, t), 0)
        col = lax.broadcasted_iota(jnp.int32, (t, t), 1)
        strict = col < row
        tri = strict.astype(BF16)

        def block(jbs, carry, masks):
            sls = [pl.ds(pl.multiple_of(jb * t, t), t) for jb in jbs]
            chains = [(hh, b) for b in range(len(jbs)) for hh in range(hs)]
            zs = {(hh, b): _dot(qs[hh], k_ref[hh, sls[b], :], NT) for hh, b in chains}
            lbs, tails, sums = {}, {}, {}
            for hh, b in chains:
                lb, lk = _sb_logs(zs[hh, b])
                if masks[b]:
                    lk = jnp.where(strict, lk, 0.0)
                lbs[hh, b] = lb
                tails[hh, b] = _dot_hilo(lk, tri)
                sums[hh, b] = jnp.sum(lk, axis=1, keepdims=True)
            carry = [list(c) for c in carry]
            for hh, b in chains:
                c, o = carry[hh]
                att = jnp.exp(lbs[hh, b] + tails[hh, b] + c)
                if masks[b]:
                    att = jnp.where(strict, att, 0.0)
                carry[hh] = [c + sums[hh, b], o + _dot(att.astype(BF16), v_ref[hh, sls[b], :])]
            return tuple(tuple(c) for c in carry)

        carry = tuple((jnp.zeros((t, 1), F32), jnp.zeros((t, dh), F32)) for _ in range(hs))
        carry = _sb_sweep(_sb_plan(i + 1, SB_FWD_GROUPS), i, -1, block, carry)
        for hh in range(hs):
            o_ref[hh] = carry[hh][1].astype(o_ref.dtype)
        if hosted:
            at_end()

    qspec = pl.BlockSpec((hs, t, dh), lambda h, i: (h, i, 0))
    kspec = pl.BlockSpec((hs, l, dh), lambda h, i: (h, 0, 0))
    res = pl.pallas_call(
        body, name=name, grid=grid, in_specs=[qspec, kspec, kspec] + (hosted.specs if hosted else []),
        out_specs=[qspec] + (hosted.specs if hosted else []),
        out_shape=[jax.ShapeDtypeStruct((h_n, l, dh), BF16)] + (hosted.out_shape if hosted else []),
        scratch_shapes=hosted.sems if hosted else [], compiler_params=_cparams(2),
    )(q, k, v, *(hosted.xs if hosted else []))
    return (res[0], res[1:]) if hosted else res[0]


def _sb_bwd(q, k, v, do, *, name, hosted=None):
    h_n, l, dh = q.shape
    t = _tile(l, SB_TILE)
    nq = l // t
    hs = SB_HEADS_PER_STEP
    scale = dh ** -0.5
    grid = (h_n // hs, nq)
    nh = hosted.n if hosted else 0

    def body(q_ref, k_ref, v_ref, do_ref, *rest):
        dq_ref, dk_ref, dv_ref = rest[nh:nh + 3]
        e_scr, s_scr = rest[2 * nh + 3:2 * nh + 5]
        at_end = hosted.run(rest[:nh], rest[nh + 3:2 * nh + 3], rest[2 * nh + 5:], grid) if hosted else None
        i = pl.program_id(1)

        @pl.when(i == 0)
        def _():
            dk_ref[...] = jnp.zeros_like(dk_ref)
            dv_ref[...] = jnp.zeros_like(dv_ref)

        qs = [(q_ref[hh].astype(F32) * scale).astype(BF16) for hh in range(hs)]
        dos = [do_ref[hh] for hh in range(hs)]
        row = lax.broadcasted_iota(jnp.int32, (t, t), 0)
        col = lax.broadcasted_iota(jnp.int32, (t, t), 1)
        strict = col < row
        tri_suffix = strict.astype(BF16)
        tri_prefix = (row < col).astype(BF16)

        def sweep1(jbs, cs, masks):
            sls = [pl.ds(pl.multiple_of(jb * t, t), t) for jb in jbs]
            chains = [(hh, b) for b in range(len(jbs)) for hh in range(hs)]
            zs = {(hh, b): _dot(qs[hh], k_ref[hh, sls[b], :], NT) for hh, b in chains}
            datts = {(hh, b): _dot(dos[hh], v_ref[hh, sls[b], :], NT) for hh, b in chains}
            lbs, tails, sums = {}, {}, {}
            for hh, b in chains:
                lb, lk = _sb_logs(zs[hh, b])
                if masks[b]:
                    lk = jnp.where(strict, lk, 0.0)
                lbs[hh, b] = lb
                tails[hh, b] = _dot_hilo(lk, tri_suffix)
                s_scr[hh, jbs[b]] = jnp.exp(lb)
                sums[hh, b] = jnp.sum(lk, axis=1, keepdims=True)
            cs = list(cs)
            for hh, b in chains:
                att = jnp.exp(lbs[hh, b] + tails[hh, b] + cs[hh])
                if masks[b]:
                    att = jnp.where(strict, att, 0.0)
                e_scr[hh, jbs[b]] = att * datts[hh, b]
                dv_ref[hh, sls[b], :] += _dot(att.astype(BF16), dos[hh], TN)
                cs[hh] = cs[hh] + sums[hh, b]
            return tuple(cs)

        plan = _sb_plan(i + 1, SB_BWD_GROUPS)
        _sb_sweep(plan, i, -1, sweep1, tuple(jnp.zeros((t, 1), F32) for _ in range(hs)))

        def sweep2(jbs, carry, masks):
            sls = [pl.ds(pl.multiple_of(jb * t, t), t) for jb in jbs]
            chains = [(hh, b) for b in range(len(jbs)) for hh in range(hs)]
            des = {(hh, b): e_scr[hh, jbs[b]] for hh, b in chains}
            pres = {(hh, b): _dot_hilo(des[hh, b], tri_prefix) for hh, b in chains}
            carry = [list(c) for c in carry]
            for hh, b in chains:
                p, dq = carry[hh]
                de, sg = des[hh, b], s_scr[hh, jbs[b]]
                dlk = p + pres[hh, b]
                if masks[b]:
                    dlk = jnp.where(strict, dlk, 0.0)
                dz = (de - sg * (de + dlk)).astype(BF16)
                dk_ref[hh, sls[b], :] += _dot(dz, qs[hh], TN)
                carry[hh] = [p + jnp.sum(de, axis=1, keepdims=True), dq + _dot(dz, k_ref[hh, sls[b], :])]
            return tuple(tuple(c) for c in carry)

        carry = tuple((jnp.zeros((t, 1), F32), jnp.zeros((t, dh), F32)) for _ in range(hs))
        carry = _sb_sweep(plan[::-1], 0, 1, sweep2, carry)
        for hh in range(hs):
            dq_ref[hh] = (carry[hh][1] * scale).astype(dq_ref.dtype)
        if hosted:
            at_end()

    qspec = pl.BlockSpec((hs, t, dh), lambda h, i: (h, i, 0))
    kspec = pl.BlockSpec((hs, l, dh), lambda h, i: (h, 0, 0))
    res = pl.pallas_call(
        body, name=name, grid=grid, in_specs=[qspec, kspec, kspec, qspec] + (hosted.specs if hosted else []),
        out_specs=[qspec, kspec, kspec] + (hosted.specs if hosted else []),
        out_shape=[jax.ShapeDtypeStruct((h_n, l, dh), BF16), jax.ShapeDtypeStruct((h_n, l, dh), F32),
                   jax.ShapeDtypeStruct((h_n, l, dh), F32)] + (hosted.out_shape if hosted else []),
        scratch_shapes=[pltpu.VMEM((hs, nq, t, t), F32), pltpu.VMEM((hs, nq, t, t), F32)]
        + (hosted.sems if hosted else []),
        compiler_params=_cparams(2),
    )(q, k, v, do, *(hosted.xs if hosted else []))
    return (res[0], res[1], res[2], res[3:]) if hosted else res


def _to_heads(x, n):
    l = x.shape[0]
    return x.reshape(l, n, -1, SB_HEAD_DIM).transpose(1, 2, 0, 3)


def _from_heads(x):
    return x.transpose(1, 0, 2).reshape(x.shape[1], -1)


def _sb_layer_fwd(x, g, wqkv, wo, tag, hosted=None):
    h = _rmsnorm(x, g, name=f"{tag}_norm")
    qkv = _mm(h, wqkv, tm=512, tn=1024, tk=1024, name=f"{tag}_qkv", out_dtype=BF16)
    qkv_h = _to_heads(qkv, 3)
    o_h = _sb_fwd(qkv_h[0], qkv_h[1], qkv_h[2], name=f"{tag}_attn", hosted=hosted)
    carried = None
    if hosted:
        o_h, carried = o_h
    o = _from_heads(o_h)
    xo = _mm(o, wo, tm=512, tn=1024, tk=1024, name=f"{tag}_out", extras=[(x, "tile")],
             epilogue=lambda acc, xt: (xt + acc,))
    return xo, (x, h, qkv_h, o), carried


def _sb_layer_bwd(dout, saved, g, wqkv, wo, tag, hosted=None):
    x, h, qkv_h, o = saved
    do = _mm(dout, wo, tb=True, tm=512, tn=1024, tk=1024, name=f"{tag}_do", out_dtype=BF16)
    dwo = _mm(o, dout, ta=True, tm=1024, tn=1024, tk=512, name=f"{tag}_dwo", out_dtype=BF16)
    res = _sb_bwd(qkv_h[0], qkv_h[1], qkv_h[2], _to_heads(do, 1)[0], name=f"{tag}_attn_bwd", hosted=hosted)
    dq, dk, dv = res[:3]
    carried = res[3] if hosted else None
    dqkv = jnp.concatenate([_from_heads(dq), _from_heads(dk.astype(BF16)), _from_heads(dv.astype(BF16))], axis=1)
    dwqkv = _mm(h, dqkv, ta=True, tm=1024, tn=1024, tk=512, name=f"{tag}_dwqkv", out_dtype=BF16)
    dx, dg = _mm(dqkv, wqkv, tb=True, tm=256, tn=1024, tk=1024, name=f"{tag}_dx",
                 extras=[(x, "tile"), (g, "row"), (dout, "tile")], outs=[(F32, "tile"), (F32, "colsum")],
                 epilogue=_norm_bwd_epilogue)
    return dx, dg, dwqkv, dwo, carried


def _shift_down(x, s, t_idx):
    return jnp.where(t_idx >= s, pltpu.roll(x, s, 0), 0.0)


def _shift_up(x, s, t_idx):
    n = x.shape[0]
    return jnp.where(t_idx < n - s, pltpu.roll(x, n - s, 0), 0.0)


def _sc_fwd(p, cw, *, name):
    l = p.shape[0]
    d = cw.shape[1]
    tc = 128
    nb = d // tc

    def body(b_ref, c_ref, h_ref, w_ref, o_ref):
        v = c_ref[...] * h_ref[...]
        t_idx = lax.broadcasted_iota(jnp.int32, v.shape, 0)
        u = v * w_ref[2:3, :] + _shift_down(v, 1, t_idx) * w_ref[1:2, :] + _shift_down(v, 2, t_idx) * w_ref[0:1, :]
        o_ref[...] = (b_ref[...] * u).astype(BF16)

    return pl.pallas_call(
        body, name=name, grid=(nb,),
        in_specs=[pl.BlockSpec((l, tc), lambda j: (0, j)), pl.BlockSpec((l, tc), lambda j: (0, nb + j)),
                  pl.BlockSpec((l, tc), lambda j: (0, 2 * nb + j)), pl.BlockSpec((3, tc), lambda j: (0, j))],
        out_specs=pl.BlockSpec((l, tc), lambda j: (0, j)),
        out_shape=jax.ShapeDtypeStruct((l, d), BF16), compiler_params=_cparams(1),
    )(p, p, p, cw)


def _sc_bwd(p, cw, dbu, *, name):
    l = p.shape[0]
    d = cw.shape[1]
    tc = 128
    nb = d // tc

    def body(b_ref, c_ref, h_ref, w_ref, g_ref, db_ref, dc_ref, dh_ref, dw_ref):
        cv, hv = c_ref[...], h_ref[...]
        v = cv * hv
        t_idx = lax.broadcasted_iota(jnp.int32, v.shape, 0)
        v1, v2 = _shift_down(v, 1, t_idx), _shift_down(v, 2, t_idx)
        u = v * w_ref[2:3, :] + v1 * w_ref[1:2, :] + v2 * w_ref[0:1, :]
        dbu_v = g_ref[...]
        db_ref[...] = (dbu_v * u).astype(BF16)
        du = dbu_v * b_ref[...]
        dv = du * w_ref[2:3, :] + _shift_up(du, 1, t_idx) * w_ref[1:2, :] + _shift_up(du, 2, t_idx) * w_ref[0:1, :]
        dc_ref[...] = (dv * hv).astype(BF16)
        dh_ref[...] = (dv * cv).astype(BF16)
        dw_ref[...] = jnp.zeros_like(dw_ref)
        dw_ref[0:1, :] = jnp.sum(du * v2, axis=0, keepdims=True)
        dw_ref[1:2, :] = jnp.sum(du * v1, axis=0, keepdims=True)
        dw_ref[2:3, :] = jnp.sum(du * v, axis=0, keepdims=True)

    col = lambda off: pl.BlockSpec((l, tc), lambda j: (0, off + j))
    return pl.pallas_call(
        body, name=name, grid=(nb,),
        in_specs=[col(0), col(nb), col(2 * nb), pl.BlockSpec((3, tc), lambda j: (0, j)), col(0)],
        out_specs=[col(0), col(0), col(0), pl.BlockSpec((8, tc), lambda j: (0, j))],
        out_shape=[jax.ShapeDtypeStruct((l, d), BF16)] * 3 + [jax.ShapeDtypeStruct((8, d), F32)],
        compiler_params=_cparams(1),
    )(p, p, p, cw, dbu)


def _sc_layer_fwd(x, g, win, cw, wout, tag):
    h = _rmsnorm(x, g, name=f"{tag}_norm")
    p = _mm(h, win, tm=512, tn=1024, tk=1024, name=f"{tag}_in")
    bu = _sc_fwd(p, cw, name=f"{tag}_conv")
    xo = _mm(bu, wout, tm=512, tn=1024, tk=1024, name=f"{tag}_out", extras=[(x, "tile")],
             epilogue=lambda acc, xt: (xt + acc,))
    return xo, (x, h, p, bu)


def _sc_layer_bwd(dout, saved, g, win, cw, wout, tag):
    x, h, p, bu = saved
    dbu = _mm(dout, wout, tb=True, tm=512, tn=1024, tk=1024, name=f"{tag}_dbu")
    dwout = _mm(bu, dout, ta=True, tm=1024, tn=1024, tk=512, name=f"{tag}_dwout", out_dtype=BF16)
    db, dc, dh, dcw = _sc_bwd(p, cw, dbu, name=f"{tag}_conv_bwd")
    dp = jnp.concatenate([db, dc, dh], axis=1)
    dwin = _mm(h, dp, ta=True, tm=1024, tn=1024, tk=512, name=f"{tag}_dwin", out_dtype=BF16)
    dx, dg = _mm(dp, win, tb=True, tm=256, tn=1024, tk=1024, name=f"{tag}_dx",
                 extras=[(x, "tile"), (g, "row"), (dout, "tile")], outs=[(F32, "tile"), (F32, "colsum")],
                 epilogue=_norm_bwd_epilogue)
    return dx, dg, dwin, dcw[:3], dwout


def _ssd_conv_fwd(p, cw, cb, *, name):
    l = p.shape[0]
    tc = 128
    nb = SSD_CONV_DIM // tc
    off = SSD_D_INNER // tc

    def body(x_ref, w_ref, b_ref, o_ref):
        xv = x_ref[...]
        t_idx = lax.broadcasted_iota(jnp.int32, xv.shape, 0)
        pre = xv * w_ref[3:4, :] + b_ref[...]
        for s in (1, 2, 3):
            pre = pre + _shift_down(xv, s, t_idx) * w_ref[3 - s:4 - s, :]
        o_ref[...] = pre * _sigmoid(pre)

    return pl.pallas_call(
        body, name=name, grid=(nb,),
        in_specs=[pl.BlockSpec((l, tc), lambda j: (0, off + j)), pl.BlockSpec((4, tc), lambda j: (0, j)),
                  pl.BlockSpec((1, tc), lambda j: (0, j))],
        out_specs=pl.BlockSpec((l, tc), lambda j: (0, j)),
        out_shape=jax.ShapeDtypeStruct((l, SSD_CONV_DIM), F32), compiler_params=_cparams(1),
    )(p, cw, cb)


def _ssd_conv_bwd(p, cw, cb, dact, *, name):
    l = p.shape[0]
    tc = 128
    nb = SSD_CONV_DIM // tc
    off = SSD_D_INNER // tc

    def body(x_ref, w_ref, b_ref, g_ref, dx_ref, dw_ref):
        xv = x_ref[...]
        t_idx = lax.broadcasted_iota(jnp.int32, xv.shape, 0)
        xs = [xv] + [_shift_down(xv, s, t_idx) for s in (1, 2, 3)]
        pre = b_ref[...] + xs[0] * w_ref[3:4, :]
        for s in (1, 2, 3):
            pre = pre + xs[s] * w_ref[3 - s:4 - s, :]
        sg = _sigmoid(pre)
        dpre = g_ref[...] * sg * (1.0 + pre * (1.0 - sg))
        dx = dpre * w_ref[3:4, :]
        for s in (1, 2, 3):
            dx = dx + _shift_up(dpre, s, t_idx) * w_ref[3 - s:4 - s, :]
        dx_ref[...] = dx
        dw_ref[...] = jnp.zeros_like(dw_ref)
        for s in (0, 1, 2, 3):
            dw_ref[3 - s:4 - s, :] = jnp.sum(dpre * xs[s], axis=0, keepdims=True)
        dw_ref[4:5, :] = jnp.sum(dpre, axis=0, keepdims=True)

    return pl.pallas_call(
        body, name=name, grid=(nb,),
        in_specs=[pl.BlockSpec((l, tc), lambda j: (0, off + j)), pl.BlockSpec((4, tc), lambda j: (0, j)),
                  pl.BlockSpec((1, tc), lambda j: (0, j)), pl.BlockSpec((l, tc), lambda j: (0, j))],
        out_specs=[pl.BlockSpec((l, tc), lambda j: (0, j)), pl.BlockSpec((8, tc), lambda j: (0, j))],
        out_shape=[jax.ShapeDtypeStruct((l, SSD_CONV_DIM), F32), jax.ShapeDtypeStruct((8, SSD_CONV_DIM), F32)],
        compiler_params=_cparams(1),
    )(p, cw, cb, dact)


def _ssd_dt_fwd(p, bias, *, name):
    l = p.shape[0]
    tm = _tile(l, 1024)
    off = (SSD_D_INNER + SSD_CONV_DIM) // 128

    def body(x_ref, b_ref, o_ref):
        v = x_ref[...] + b_ref[...]
        o_ref[...] = jnp.maximum(v, 0.0) + jnp.log(1.0 + jnp.exp(-jnp.abs(v)))

    return pl.pallas_call(
        body, name=name, grid=(l // tm,),
        in_specs=[pl.BlockSpec((tm, 128), lambda i: (i, off)), pl.BlockSpec((1, 128), lambda i: (0, 0))],
        out_specs=pl.BlockSpec((tm, 128), lambda i: (i, 0)),
        out_shape=jax.ShapeDtypeStruct((l, 128), F32), compiler_params=_cparams(1),
    )(p, bias)


def _ssd_dt_bwd(p, bias, ddt, *, name):
    l = p.shape[0]
    tm = _tile(l, 1024)
    off = (SSD_D_INNER + SSD_CONV_DIM) // 128

    def body(x_ref, b_ref, g_ref, o_ref, db_ref):
        i = pl.program_id(0)
        d = g_ref[...] * _sigmoid(x_ref[...] + b_ref[...])
        o_ref[...] = d
        _accumulate(db_ref, jnp.sum(d, axis=0, keepdims=True), i == 0)

    return pl.pallas_call(
        body, name=name, grid=(l // tm,),
        in_specs=[pl.BlockSpec((tm, 128), lambda i: (i, off)), pl.BlockSpec((1, 128), lambda i: (0, 0)),
                  pl.BlockSpec((tm, 128), lambda i: (i, 0))],
        out_specs=[pl.BlockSpec((tm, 128), lambda i: (i, 0)), pl.BlockSpec((1, 128), lambda i: (0, 0))],
        out_shape=[jax.ShapeDtypeStruct((l, 128), F32), jax.ShapeDtypeStruct((1, 128), F32)],
        compiler_params=_cparams(1),
    )(p, bias, ddt)


def _row_to_col(r, eye):
    return jnp.sum(jnp.where(eye, r, 0.0), axis=1, keepdims=True)


def _col_to_row(c, eye):
    return jnp.sum(jnp.where(eye, c, 0.0), axis=0, keepdims=True)


def _ssd_chunk_common(b_ref, c_ref, dt_ref, a_ref, lam_scr):
    n = SSD_CHUNK
    row = lax.broadcasted_iota(jnp.int32, (n, n), 0)
    col = lax.broadcasted_iota(jnp.int32, (n, n), 1)
    bm, cm = b_ref[...].astype(BF16), c_ref[...].astype(BF16)
    g = _dot(cm, bm, NT)
    incl = (row <= col).astype(BF16)
    lam_scr[...] = _dot_exact(dt_ref[...] * a_ref[...], incl)
    return row, col, bm, cm, g


def _ssd_head_common(r, row, col, dt_ref, lam_scr):
    eye, tril = row == col, row >= col
    lam_r = lam_scr[r:r + 1, :]
    dt_r = dt_ref[r:r + 1, :]
    lam_c = _row_to_col(lam_r, eye)
    dt_c = _row_to_col(dt_r, eye)
    dk = jnp.where(tril, jnp.exp(jnp.minimum(lam_c - lam_r, 0.0)), 0.0)
    lam_last = jnp.sum(jnp.where(col[0:1, :] == SSD_CHUNK - 1, lam_r, 0.0), axis=1, keepdims=True)
    return eye, lam_r, dt_r, lam_c, dt_c, dk, lam_last


def _ssd_fwd(xh, act, dt_t, a_b, *, name):
    l = xh.shape[1]
    nc = l // SSD_CHUNK
    n, p_dim, hpg = SSD_CHUNK, SSD_HEAD_DIM, SSD_HPG

    def body(x_ref, b_ref, c_ref, dt_ref, a_ref, y_ref, hp_ref, h_scr, lam_scr):
        @pl.when(pl.program_id(1) == 0)
        def _():
            h_scr[...] = jnp.zeros_like(h_scr)

        row, col, bm, cm, g = _ssd_chunk_common(b_ref, c_ref, dt_ref, a_ref, lam_scr)
        for r in range(hpg):
            _, _, dt_r, lam_c, dt_c, dk, lam_last = _ssd_head_common(r, row, col, dt_ref, lam_scr)
            xr = x_ref[r]
            hr = h_scr[r]
            w = (g * dk * dt_r).astype(BF16)
            y = _dot(w, xr.astype(BF16)) + _dot(cm, hr.astype(BF16), NT) * jnp.exp(lam_c)
            y_ref[r] = y
            hp_ref[r] = hr
            xw = (xr * (jnp.exp(lam_last - lam_c) * dt_c)).astype(BF16)
            h_scr[r] = jnp.exp(lam_last) * hr + _dot(xw, bm, TN)

    g_off = SSD_D_INNER // SSD_STATE
    return pl.pallas_call(
        body, name=name, grid=(SSD_GROUPS, nc),
        in_specs=[pl.BlockSpec((hpg, n, p_dim), lambda g, c: (g, c, 0)),
                  pl.BlockSpec((n, SSD_STATE), lambda g, c: (c, g_off + g)),
                  pl.BlockSpec((n, SSD_STATE), lambda g, c: (c, g_off + SSD_GROUPS + g)),
                  pl.BlockSpec((None, 8, n), lambda g, c: (g, 0, c)),
                  pl.BlockSpec((None, 8, 128), lambda g, c: (g, 0, 0))],
        out_specs=[pl.BlockSpec((hpg, n, p_dim), lambda g, c: (g, c, 0)),
                   pl.BlockSpec((None, hpg, p_dim, SSD_STATE), lambda g, c: (c, g, 0, 0))],
        out_shape=[jax.ShapeDtypeStruct(xh.shape, F32),
                   jax.ShapeDtypeStruct((nc, SSD_HEADS, p_dim, SSD_STATE), F32)],
        scratch_shapes=[pltpu.VMEM((hpg, p_dim, SSD_STATE), F32), pltpu.VMEM((8, n), F32)],
        compiler_params=_cparams(2),
    )(xh, act, act, dt_t, a_b)


def _ssd_bwd(xh, act, dt_t, a_b, hprev, dyh, *, name):
    l = xh.shape[1]
    nc = l // SSD_CHUNK
    n, p_dim, hpg = SSD_CHUNK, SSD_HEAD_DIM, SSD_HPG

    def body(x_ref, b_ref, c_ref, dt_ref, a_ref, hp_ref, dy_ref,
             dx_ref, db_ref, dc_ref, ddt_ref, da_ref, dh_scr, lam_scr, dlam_scr, ddt_scr):
        ci = pl.program_id(1)

        @pl.when(ci == 0)
        def _():
            dh_scr[...] = jnp.zeros_like(dh_scr)

        row, col, bm, cm, g = _ssd_chunk_common(b_ref, c_ref, dt_ref, a_ref, lam_scr)
        dlam_scr[...] = jnp.zeros_like(dlam_scr)
        ddt_scr[...] = jnp.zeros_like(ddt_scr)
        dg_acc = jnp.zeros((n, n), F32)
        dc_acc = jnp.zeros((n, SSD_STATE), F32)
        db_acc = jnp.zeros((n, SSD_STATE), F32)
        for r in range(hpg):
            eye, _, dt_r, lam_c, dt_c, dk, lam_last = _ssd_head_common(r, row, col, dt_ref, lam_scr)
            xr, dyr, hr, dhr = x_ref[r], dy_ref[r], hp_ref[r], dh_scr[r]
            xb, dyb, hb, dhb = xr.astype(BF16), dyr.astype(BF16), hr.astype(BF16), dhr.astype(BF16)
            e_l = jnp.exp(lam_c)
            e_last = jnp.exp(lam_last)
            decay_c = jnp.exp(lam_last - lam_c)
            w_c = decay_c * dt_c
            m = g * dk * dt_r
            dm = _dot(dyb, xb, NT)
            bdh = _dot(bm, dhb, NT)
            dx_ref[r] = _dot(m.astype(BF16), dyb, TN) + w_c * bdh
            dg_acc = dg_acc + dm * dk * dt_r
            q_mat = dm * g * dk
            p_mat = q_mat * dt_r
            yoff = _dot(cm, hb, NT) * e_l
            q_c = jnp.sum(xr * bdh, axis=1, keepdims=True)
            dlam_c = (jnp.sum(p_mat, axis=1, keepdims=True) + jnp.sum(dyr * yoff, axis=1, keepdims=True)
                      - w_c * q_c)
            d_last = (jnp.sum(w_c * q_c, axis=0, keepdims=True)
                      + e_last * jnp.sum(jnp.sum(dhr * hr, axis=1, keepdims=True), axis=0, keepdims=True))
            dlam_scr[r:r + 1, :] = (_col_to_row(dlam_c, eye) - jnp.sum(p_mat, axis=0, keepdims=True)
                                    + jnp.where(col[0:1, :] == n - 1, d_last, 0.0))
            ddt_scr[r:r + 1, :] = jnp.sum(q_mat, axis=0, keepdims=True) + _col_to_row(decay_c * q_c, eye)
            dc_acc = dc_acc + e_l * _dot(dyb, hb)
            db_acc = db_acc + _dot((xr * w_c).astype(BF16), dhb)
            dh_scr[r] = e_last * dhr + _dot((dyr * e_l).astype(BF16), cm, TN)

        dgb = dg_acc.astype(BF16)
        dc_ref[...] = _dot(dgb, bm) + dc_acc
        db_ref[...] = _dot(dgb, cm, TN) + db_acc
        rev = (row >= col).astype(BF16)
        da = _dot_exact(dlam_scr[...], rev)
        ddt_ref[...] = ddt_scr[...] + da * a_ref[...]
        _accumulate(da_ref, da * dt_ref[...], ci == 0)

        @pl.when(ci == nc - 1)
        def _():
            da_ref[...] = jnp.broadcast_to(jnp.sum(da_ref[...], axis=1, keepdims=True), da_ref.shape)

    g_off = SSD_D_INNER // SSD_STATE
    rc = lambda c: nc - 1 - c
    hspec = pl.BlockSpec((hpg, n, p_dim), lambda g, c: (g, rc(c), 0))
    gspec = pl.BlockSpec((n, SSD_STATE), lambda g, c: (rc(c), g))
    return pl.pallas_call(
        body, name=name, grid=(SSD_GROUPS, nc),
        in_specs=[hspec,
                  pl.BlockSpec((n, SSD_STATE), lambda g, c: (rc(c), g_off + g)),
                  pl.BlockSpec((n, SSD_STATE), lambda g, c: (rc(c), g_off + SSD_GROUPS + g)),
                  pl.BlockSpec((None, 8, n), lambda g, c: (g, 0, rc(c))),
                  pl.BlockSpec((None, 8, 128), lambda g, c: (g, 0, 0)),
                  pl.BlockSpec((None, hpg, p_dim, SSD_STATE), lambda g, c: (rc(c), g, 0, 0)),
                  hspec],
        out_specs=[hspec, gspec, gspec,
                   pl.BlockSpec((None, 8, n), lambda g, c: (g, 0, rc(c))),
                   pl.BlockSpec((None, 8, 128), lambda g, c: (g, 0, 0))],
        out_shape=[jax.ShapeDtypeStruct(xh.shape, F32),
                   jax.ShapeDtypeStruct((l, SSD_GROUPS * SSD_STATE), F32),
                   jax.ShapeDtypeStruct((l, SSD_GROUPS * SSD_STATE), F32),
                   jax.ShapeDtypeStruct(dt_t.shape, F32),
                   jax.ShapeDtypeStruct(a_b.shape, F32)],
        scratch_shapes=[pltpu.VMEM((hpg, p_dim, SSD_STATE), F32), pltpu.VMEM((8, n), F32),
                        pltpu.VMEM((8, n), F32), pltpu.VMEM((8, n), F32)],
        compiler_params=_cparams(2),
    )(xh, act, act, dt_t, a_b, hprev, dyh)


def _ssd_gate_fwd(y, act, p, d_vec, gn, *, name):
    l = y.shape[0]
    w = SSD_D_INNER
    tm = _tile(l, 256)

    def body(y_ref, xs_ref, z_ref, d_ref, g_ref, o_ref):
        for gi in range(SSD_GROUPS):
            sl = slice(gi * SSD_NORM_GROUP, (gi + 1) * SSD_NORM_GROUP)
            z = z_ref[:, sl]
            y2 = (y_ref[:, sl] + d_ref[:, sl] * xs_ref[:, sl]) * (z * _sigmoid(z))
            r = lax.rsqrt(jnp.mean(y2 * y2, axis=1, keepdims=True) + RMS_EPS)
            o_ref[:, sl] = (y2 * r * g_ref[:, sl]).astype(BF16)

    rows = pl.BlockSpec((tm, w), lambda i: (i, 0))
    vec = pl.BlockSpec((1, w), lambda i: (0, 0))
    return pl.pallas_call(
        body, name=name, grid=(l // tm,), in_specs=[rows, rows, rows, vec, vec], out_specs=rows,
        out_shape=jax.ShapeDtypeStruct((l, w), BF16), compiler_params=_cparams(1),
    )(y, act, p, d_vec, gn)


def _ssd_gate_bwd(dyn, y, act, p, d_vec, gn, *, name):
    l = y.shape[0]
    w = SSD_D_INNER
    tm = _tile(l, 256)

    def body(dyn_ref, y_ref, xs_ref, z_ref, d_ref, g_ref, dy_ref, dz_ref, dxs_ref, dd_ref, dg_ref):
        i = pl.program_id(0)
        for gi in range(SSD_GROUPS):
            sl = slice(gi * SSD_NORM_GROUP, (gi + 1) * SSD_NORM_GROUP)
            z, xs, dv = z_ref[:, sl], xs_ref[:, sl], d_ref[:, sl]
            s = _sigmoid(z)
            sz = z * s
            y1 = y_ref[:, sl] + dv * xs
            y2 = y1 * sz
            r = lax.rsqrt(jnp.mean(y2 * y2, axis=1, keepdims=True) + RMS_EPS)
            y2h = y2 * r
            dyn_v = dyn_ref[:, sl]
            d2h = dyn_v * g_ref[:, sl]
            dy2 = r * (d2h - y2h * jnp.mean(d2h * y2h, axis=1, keepdims=True))
            dy1 = dy2 * sz
            dy_ref[:, sl] = dy1
            dz_ref[:, sl] = dy2 * y1 * s * (1.0 + z * (1.0 - s))
            dxs_ref[:, sl] = dv * dy1
            _accumulate(dd_ref.at[:, sl], jnp.sum(dy1 * xs, axis=0, keepdims=True), i == 0)
            _accumulate(dg_ref.at[:, sl], jnp.sum(dyn_v * y2h, axis=0, keepdims=True), i == 0)

    rows = pl.BlockSpec((tm, w), lambda i: (i, 0))
    vec = pl.BlockSpec((1, w), lambda i: (0, 0))
    return pl.pallas_call(
        body, name=name, grid=(l // tm,), in_specs=[rows, rows, rows, rows, vec, vec],
        out_specs=[rows, rows, rows, vec, vec],
        out_shape=[jax.ShapeDtypeStruct((l, w), F32)] * 3 + [jax.ShapeDtypeStruct((1, w), F32)] * 2,
        compiler_params=_cparams(1),
    )(dyn, y, act, p, d_vec, gn)


def _heads_major(x):
    return x.reshape(x.shape[0], SSD_HEADS, SSD_HEAD_DIM).transpose(1, 0, 2)


def _ssd_layer_fwd(x, g, win, cw, cb, dt_bias, a_log, d_skip, gn, wout, tag):
    l = x.shape[0]
    h = _rmsnorm(x, g, name=f"{tag}_norm")
    p = _mm(h, win, tm=512, tn=896, tk=1024, name=f"{tag}_in")
    act = _ssd_conv_fwd(p, cw, cb, name=f"{tag}_conv")
    bias = jnp.pad(dt_bias, (0, 128 - SSD_HEADS)).reshape(1, 128)
    dt = _ssd_dt_fwd(p, bias, name=f"{tag}_dt")
    xh = _heads_major(act[:, :SSD_D_INNER])
    dt_t = jnp.pad(dt[:, :SSD_HEADS].T.reshape(SSD_GROUPS, SSD_HPG, l), ((0, 0), (0, 8 - SSD_HPG), (0, 0)))
    a = -jnp.exp(a_log).reshape(SSD_GROUPS, SSD_HPG, 1)
    a_b = jnp.broadcast_to(jnp.pad(a, ((0, 0), (0, 8 - SSD_HPG), (0, 0))), (SSD_GROUPS, 8, 128))
    yh, hprev = _ssd_fwd(xh, act, dt_t, a_b, name=f"{tag}_scan")
    y = yh.transpose(1, 0, 2).reshape(l, SSD_D_INNER)
    d_vec = jnp.repeat(d_skip, SSD_HEAD_DIM).reshape(1, SSD_D_INNER)
    yn = _ssd_gate_fwd(y, act, p, d_vec, gn, name=f"{tag}_gate")
    xo = _mm(yn, wout, tm=512, tn=1024, tk=2048, name=f"{tag}_out", extras=[(x, "tile")],
             epilogue=lambda acc, xt: (xt + acc,))
    return xo, (x, h, p, act, bias, xh, dt_t, a_b, hprev, y, d_vec, yn)


def _ssd_layer_bwd(dout, saved, g, win, cw, cb, gn, wout, tag):
    x, h, p, act, bias, xh, dt_t, a_b, hprev, y, d_vec, yn = saved
    l = x.shape[0]
    dyn = _mm(dout, wout, tb=True, tm=512, tn=1024, tk=1024, name=f"{tag}_dyn")
    dwout = _mm(yn, dout, ta=True, tm=1024, tn=1024, tk=512, name=f"{tag}_dwout", out_dtype=BF16)
    dy, dz, dxs_d, dd_vec, dgn = _ssd_gate_bwd(dyn, y, act, p, d_vec, gn, name=f"{tag}_gate_bwd")
    dxh, dbm, dcm, ddt_t, da_b = _ssd_bwd(xh, act, dt_t, a_b, hprev, _heads_major(dy), name=f"{tag}_scan_bwd")
    dxs = dxh.transpose(1, 0, 2).reshape(l, SSD_D_INNER) + dxs_d
    dact = jnp.concatenate([dxs, dbm, dcm], axis=1)
    dxbc, dcw8 = _ssd_conv_bwd(p, cw, cb, dact, name=f"{tag}_conv_bwd")
    ddt = jnp.pad(ddt_t[:, :SSD_HPG, :].reshape(SSD_HEADS, l).T, ((0, 0), (0, 128 - SSD_HEADS)))
    ddt_raw, dbias = _ssd_dt_bwd(p, bias, ddt, name=f"{tag}_dt_bwd")
    dp = jnp.concatenate([dz, dxbc, ddt_raw], axis=1)
    dwin = _mm(h, dp, ta=True, tm=1024, tn=896, tk=512, name=f"{tag}_dwin", out_dtype=BF16)
    dx, dg = _mm(dp, win, tb=True, tm=256, tn=1024, tk=896, name=f"{tag}_dx",
                 extras=[(x, "tile"), (g, "row"), (dout, "tile")], outs=[(F32, "tile"), (F32, "colsum")],
                 epilogue=_norm_bwd_epilogue)
    a_heads = a_b[:, :SSD_HPG, 0].reshape(SSD_HEADS)
    grads = dict(
        ssd_w_in=dwin[:, :SSD_IN_DIM], ssd_conv_w=dcw8[:4], ssd_conv_b=dcw8[4],
        ssd_dt_bias=dbias[0, :SSD_HEADS], ssd_a_log=da_b[:, :SSD_HPG, 0].reshape(SSD_HEADS) * a_heads,
        ssd_d=dd_vec.reshape(SSD_HEADS, SSD_HEAD_DIM).sum(axis=1), ssd_norm=dgn[0], ssd_w_out=dwout)
    return dx, dg, grads


def _local_step(x, tgt, w, gather_later=None, scatter_early=None):
    row = lambda v: v.reshape(1, -1)
    saved = []
    for i in range(DEPTH):
        kind, j = i % 3, i // 3
        x, s1 = _ffn_fwd(x, row(w["ffn1_norm"][i]), w["ffn1_w_gu"][i], w["ffn1_w_down"][i], f"l{i}f1")
        gm = row(w["mix_norm"][i])
        if kind == 0:
            hosted = gather_later[0] if (gather_later and i == 0) else None
            x, sm, carried = _sb_layer_fwd(x, gm, w["sb_w_qkv"][j], w["sb_w_o"][j], f"l{i}sb", hosted=hosted)
            if hosted:
                w = gather_later[1](w, carried)
        elif kind == 1:
            x, sm = _ssd_layer_fwd(x, gm, w["ssd_w_in"][j], w["ssd_conv_w"][j], row(w["ssd_conv_b"][j]),
                                   w["ssd_dt_bias"][j], w["ssd_a_log"][j], w["ssd_d"][j], row(w["ssd_norm"][j]),
                                   w["ssd_w_out"][j], f"l{i}ssd")
        else:
            x, sm = _sc_layer_fwd(x, gm, w["sc_w_in"][j], w["sc_conv_w"][j], w["sc_w_out"][j], f"l{i}sc")
        x, s2 = _ffn_fwd(x, row(w["ffn2_norm"][i]), w["ffn2_w_gu"][i], w["ffn2_w_down"][i], f"l{i}f2")
        saved.append((s1, sm, s2))

    loss, dx, dfinal = _final_loss(x, row(w["final_norm"]), tgt, name="final_loss")
    per_layer = {k: [None] * DEPTH for k in ("ffn1_norm", "ffn1_w_gu", "ffn1_w_down", "mix_norm",
                                             "ffn2_norm", "ffn2_w_gu", "ffn2_w_down")}
    per_layer.update({"sb_w_qkv": [None, None], "sb_w_o": [None, None]})
    grads = {"final_norm": dfinal[0]}
    early = None
    for i in reversed(range(DEPTH)):
        kind, j = i % 3, i // 3
        s1, sm, s2 = saved[i]
        dx, dg, dwgu, dwd = _ffn_bwd(dx, s2, row(w["ffn2_norm"][i]), w["ffn2_w_gu"][i], w["ffn2_w_down"][i], f"l{i}f2")
        per_layer["ffn2_norm"][i], per_layer["ffn2_w_gu"][i], per_layer["ffn2_w_down"][i] = dg[0], dwgu, dwd
        gm = row(w["mix_norm"][i])
        if kind == 0:
            hosted = scatter_early({**grads, **per_layer}) if (scatter_early and i == 0) else None
            dx, dg, dwqkv, dwo, carried = _sb_layer_bwd(dx, sm, gm, w["sb_w_qkv"][j], w["sb_w_o"][j], f"l{i}sb",
                                                        hosted=hosted)
            per_layer["sb_w_qkv"][j], per_layer["sb_w_o"][j] = dwqkv, dwo
            if hosted:
                early = carried
        elif kind == 1:
            dx, dg, sg = _ssd_layer_bwd(dx, sm, gm, w["ssd_w_in"][j], w["ssd_conv_w"][j], row(w["ssd_conv_b"][j]),
                                        row(w["ssd_norm"][j]), w["ssd_w_out"][j], f"l{i}ssd")
            sg["ssd_w_in"], sg["ssd_w_out"] = [sg["ssd_w_in"]], [sg["ssd_w_out"]]
            grads.update({k: (v if isinstance(v, list) else v[None]) for k, v in sg.items()})
        else:
            dx, dg, dwin, dcw, dwout = _sc_layer_bwd(dx, sm, gm, w["sc_w_in"][j], w["sc_conv_w"][j],
                                                     w["sc_w_out"][j], f"l{i}sc")
            grads.update(sc_w_in=[dwin], sc_conv_w=dcw[None], sc_w_out=[dwout])
        per_layer["mix_norm"][i] = dg[0]
        dx, dg, dwgu, dwd = _ffn_bwd(dx, s1, row(w["ffn1_norm"][i]), w["ffn1_w_gu"][i], w["ffn1_w_down"][i], f"l{i}f1")
        per_layer["ffn1_norm"][i], per_layer["ffn1_w_gu"][i], per_layer["ffn1_w_down"][i] = dg[0], dwgu, dwd
    for k, v in per_layer.items():
        grads[k] = jnp.stack(v) if k.endswith("_norm") else v
    return loss, dx, grads, early


_HBM = pl.BlockSpec(memory_space=pltpu.HBM)


def _remote(src, dst, send_sems, recv_sems, idx, dev):
    return pltpu.make_async_remote_copy(src_ref=src, dst_ref=dst, send_sem=send_sems.at[idx], recv_sem=recv_sems.at[idx],
                                        device_id=dev, device_id_type=pl.DeviceIdType.MESH)


def _exchange_call(body, xs, out_shapes, n_copies, name):
    n = len(xs)
    return pl.pallas_call(
        body, name=name, in_specs=[_HBM] * n, out_specs=[_HBM] * n,
        out_shape=[jax.ShapeDtypeStruct(s, x.dtype) for s, x in zip(out_shapes, xs)],
        scratch_shapes=[pltpu.SemaphoreType.DMA((n, n_copies)), pltpu.SemaphoreType.DMA((n, n_copies)),
                        pltpu.SemaphoreType.DMA((n,))],
    )(*xs)


def _gather(xs, *, name):
    n = len(xs)

    def body(*refs):
        start, finish = _gather_steps(refs[:n], refs[n:2 * n], *refs[2 * n:])
        start()
        finish()

    return _exchange_call(body, xs, _gather_shapes(xs), _GATHER_COPIES, name)


_GATHER_COPIES = 7


def _gather_shapes(xs):
    return [(N_DEV,) + x.shape for x in xs]


def _gather_steps(x_refs, o_refs, send_sems, recv_sems, local_sems):
    n = len(x_refs)

    def plan():
        mx, my, mc = lax.axis_index("x"), lax.axis_index("y"), lax.axis_index("c")
        slot = lambda px, py, pc: 4 * px + 2 * py + pc
        me, sibling = (mx, my, mc), (mx, my, 1 - mc)
        chips = [(1 - mx, my), (mx, 1 - my), (1 - mx, 1 - my)]
        locals_, first = [], []
        for a in range(n):
            x_ref, o_ref = x_refs[a], o_refs[a]
            locals_.append(pltpu.make_async_copy(x_ref, o_ref.at[slot(*me)], local_sems.at[a]))
            first.append(_remote(x_ref, o_ref.at[slot(*me)], send_sems, recv_sems, (a, 0), sibling))
            for j, chip in enumerate(chips):
                first.append(_remote(x_ref, o_ref.at[slot(*me)], send_sems, recv_sems, (a, 1 + j), (*chip, mc)))
        return locals_, first, slot, me, sibling, chips, mc

    def start():
        locals_, first = plan()[:2]
        for cp in locals_ + first:
            cp.start()

    def finish():
        locals_, first, slot, me, sibling, chips, mc = plan()
        passed = []
        for j, chip in enumerate(chips):
            for a in range(n):
                landed = o_refs[a].at[slot(*chip, mc)]
                _remote(landed, landed, send_sems, recv_sems, (a, 1 + j), me).wait_recv()
                fwd = _remote(landed, landed, send_sems, recv_sems, (a, 4 + j), sibling)
                fwd.start()
                passed.append(fwd)
        for a in range(n):
            from_sib = o_refs[a].at[slot(*sibling)]
            _remote(from_sib, from_sib, send_sems, recv_sems, (a, 0), me).wait_recv()
            for j, chip in enumerate(chips):
                via_sib = o_refs[a].at[slot(*chip, 1 - mc)]
                _remote(via_sib, via_sib, send_sems, recv_sems, (a, 4 + j), me).wait_recv()
        for cp in first + passed:
            cp.wait_send()
        for cp in locals_:
            cp.wait()

    return start, finish


def _scatter_sibling(xs, *, name):
    n = len(xs)

    def body(*refs):
        x_refs, o_refs = refs[:n], refs[n:2 * n]
        send_sems, recv_sems, _ = refs[2 * n:]
        mx, my, mc = lax.axis_index("x"), lax.axis_index("y"), lax.axis_index("c")
        sibling = (mx, my, 1 - mc)
        sends = []
        for a in range(n):
            for ch in range(4):
                sends.append(_remote(x_refs[a].at[ch, 1 - mc], o_refs[a].at[ch], send_sems, recv_sems, (a, ch), sibling))
        for cp in sends:
            cp.start()
        for cp in sends:
            cp.wait_recv()
        for cp in sends:
            cp.wait_send()

    return _exchange_call(body, xs, [(4,) + x.shape[2:] for x in xs], 4, name)


def _scatter_chips(ys, *, name):
    n = len(ys)

    def body(*refs):
        start, finish = _chip_scatter_steps(refs[:n], refs[n:2 * n], *refs[2 * n:])
        start()
        finish()

    return _exchange_call(body, ys, _chip_scatter_shapes(ys), _CHIP_SCATTER_COPIES, name)


_CHIP_SCATTER_COPIES = 3


def _chip_scatter_shapes(ys):
    return [y.shape for y in ys]


def _chip_scatter_steps(y_refs, o_refs, send_sems, recv_sems, local_sems):
    n = len(y_refs)

    def plan():
        mx, my, mc = lax.axis_index("x"), lax.axis_index("y"), lax.axis_index("c")
        mine = 2 * mx + my
        chips = [(1 - mx, my), (mx, 1 - my), (1 - mx, 1 - my)]
        locals_, sends, recvs = [], [], []
        for a in range(n):
            locals_.append(pltpu.make_async_copy(y_refs[a].at[mine], o_refs[a].at[mine], local_sems.at[a]))
            for j, (px, py) in enumerate(chips):
                theirs = 2 * px + py
                sends.append(_remote(y_refs[a].at[theirs], o_refs[a].at[mine], send_sems, recv_sems, (a, j), (px, py, mc)))
                recvs.append(_remote(y_refs[a].at[theirs], o_refs[a].at[theirs], send_sems, recv_sems, (a, j), (px, py, mc)))
        return locals_, sends, recvs

    def start():
        locals_, sends, _ = plan()
        for cp in locals_ + sends:
            cp.start()

    def finish():
        locals_, sends, recvs = plan()
        for cp in recvs:
            cp.wait_recv()
        for cp in sends:
            cp.wait_send()
        for cp in locals_:
            cp.wait()

    return start, finish


def _pair_add(x, r, *, name):
    _, _, rows, c = x.shape
    tr = _tile(rows, 256, 16)

    def body(x_ref, r_ref, o_ref):
        mc = lax.axis_index("c")
        o_ref[...] = (x_ref[mc].astype(F32) + r_ref[...].astype(F32)).astype(o_ref.dtype)

    return pl.pallas_call(
        body, name=name, grid=(4, rows // tr),
        in_specs=[pl.BlockSpec((None, 2, tr, c), lambda ch, i: (ch, 0, i, 0)),
                  pl.BlockSpec((None, tr, c), lambda ch, i: (ch, i, 0))],
        out_specs=pl.BlockSpec((None, tr, c), lambda ch, i: (ch, i, 0)),
        out_shape=jax.ShapeDtypeStruct((4, rows, c), x.dtype), compiler_params=_cparams(2),
    )(x, r)


def _adamw_reduce(parts, w, m, v, *, name):
    r, c = w.shape
    n_parts = parts.shape[0]
    tr = _tile(r, 256, 16)
    bc1 = 1.0 - ADAM_B1 ** ADAM_STEP
    bc2 = 1.0 - ADAM_B2 ** ADAM_STEP

    def body(p_ref, w_ref, m_ref, v_ref, g_ref, d_ref, nm_ref, nv_ref):
        g = p_ref[0].astype(F32)
        for q in range(1, n_parts):
            g = g + p_ref[q].astype(F32)
        nm = ADAM_B1 * m_ref[...] + (1.0 - ADAM_B1) * g
        nv = ADAM_B2 * v_ref[...] + (1.0 - ADAM_B2) * (g * g)
        g_ref[...] = g
        nm_ref[...] = nm
        nv_ref[...] = nv
        d_ref[...] = -ADAM_LR * ((nm / bc1) / (jnp.sqrt(nv / bc2) + ADAM_EPS) + ADAM_WD * w_ref[...])

    blk = pl.BlockSpec((tr, c), lambda i: (i, 0))
    return pl.pallas_call(
        body, name=name, grid=(r // tr,),
        in_specs=[pl.BlockSpec((n_parts, tr, c), lambda i: (0, i, 0)), blk, blk, blk], out_specs=[blk] * 4,
        out_shape=[jax.ShapeDtypeStruct((r, c), F32)] * 4, compiler_params=_cparams(1),
    )(parts, w, m, v)


def _col_full(g):
    return g.transpose(1, 2, 0, 3).reshape(g.shape[1], g.shape[2], -1)


def _col_parts(f):
    n, k, c8 = f.shape
    return f.reshape(n, k, N_DEV, c8 // N_DEV).transpose(2, 0, 1, 3)


def _row_full(g):
    return g.transpose(1, 0, 2, 3).reshape(g.shape[1], -1, g.shape[3])


def _row_parts(f):
    n, r8, c = f.shape
    return f.reshape(n, N_DEV, r8 // N_DEV, c).transpose(1, 0, 2, 3)


def _gu_full(g):
    n, d, c = g.shape[1:]
    return g.reshape(2, 4, n, d, c).transpose(2, 0, 3, 1, 4).reshape(n, 2, d, 4 * c)


def _gu_parts(f):
    n, _, d, c4 = f.shape
    return f.reshape(n, 2, d, 4, c4 // 4).transpose(1, 3, 0, 2, 4).reshape(N_DEV, n, d, c4 // 4)


def _ssd_in_full(g):
    return jnp.pad(_col_full(g), ((0, 0), (0, 0), (0, SSD_IN_PAD - SSD_IN_DIM)))


_MATMUL_WEIGHTS = (
    ("ffn1_w_gu", _gu_full, _gu_parts), ("ffn1_w_down", _row_full, _row_parts),
    ("ffn2_w_gu", _gu_full, _gu_parts), ("ffn2_w_down", _row_full, _row_parts),
    ("sb_w_qkv", _col_full, _col_parts), ("sb_w_o", _row_full, _row_parts),
    ("ssd_w_in", _ssd_in_full, _col_parts), ("ssd_w_out", _row_full, _row_parts),
    ("sc_w_in", _col_full, _col_parts), ("sc_w_out", _row_full, _row_parts),
)
_FIRST_WEIGHTS = ("ffn1_w_gu", "ffn1_w_down", "sb_w_qkv", "sb_w_o")
_CONV_WEIGHTS = ("ssd_conv_w", "sc_conv_w")
_REPLICATED = ("ffn1_norm", "mix_norm", "ffn2_norm", "final_norm", "ssd_conv_b", "ssd_norm",
               "ssd_dt_bias", "ssd_a_log", "ssd_d")
_ORDER = ("ffn1_norm", "ffn1_w_gu", "ffn1_w_down", "mix_norm", "ffn2_norm", "ffn2_w_gu", "ffn2_w_down",
          "sb_w_qkv", "sb_w_o", "ssd_w_in", "ssd_conv_w", "ssd_conv_b", "ssd_dt_bias", "ssd_a_log", "ssd_d",
          "ssd_norm", "ssd_w_out", "sc_w_in", "sc_conv_w", "sc_w_out", "final_norm")
_LANES = 1024


def _rows_of(a):
    flat = a.reshape(-1)
    pad = -flat.shape[0] % _LANES
    return jnp.pad(flat, (0, pad)).reshape(-1, _LANES)


def _pack_rows(arrays, mult):
    rows = [_rows_of(a) for a in arrays]
    packed = jnp.concatenate(rows, axis=0)
    pad = -packed.shape[0] % mult
    return jnp.pad(packed, ((0, pad), (0, 0))), [r.shape[0] for r in rows]


def _unpack_rows(packed, counts, shapes, lead=()):
    out, off = [], 0
    for n, shp in zip(counts, shapes):
        size = math.prod(shp)
        seg = packed[..., off:off + n, :].reshape(lead + (n * _LANES,))[..., :size]
        out.append(seg.reshape(lead + tuple(shp)))
        off += n
    return out


def kernel(x, ffn1_norm, ffn1_w_gu, ffn1_w_down, mix_norm, ffn2_norm, ffn2_w_gu, ffn2_w_down, sb_w_qkv, sb_w_o, ssd_w_in, ssd_conv_w, ssd_conv_b, ssd_dt_bias, ssd_a_log, ssd_d, ssd_norm, ssd_w_out, sc_w_in, sc_conv_w, sc_w_out, final_norm, loss_target, m_ffn1_norm, m_ffn1_w_gu, m_ffn1_w_down, m_mix_norm, m_ffn2_norm, m_ffn2_w_gu, m_ffn2_w_down, m_sb_w_qkv, m_sb_w_o, m_ssd_w_in, m_ssd_conv_w, m_ssd_conv_b, m_ssd_dt_bias, m_ssd_a_log, m_ssd_d, m_ssd_norm, m_ssd_w_out, m_sc_w_in, m_sc_conv_w, m_sc_w_out, m_final_norm, v_ffn1_norm, v_ffn1_w_gu, v_ffn1_w_down, v_mix_norm, v_ffn2_norm, v_ffn2_w_gu, v_ffn2_w_down, v_sb_w_qkv, v_sb_w_o, v_ssd_w_in, v_ssd_conv_w, v_ssd_conv_b, v_ssd_dt_bias, v_ssd_a_log, v_ssd_d, v_ssd_norm, v_ssd_w_out, v_sc_w_in, v_sc_conv_w, v_sc_w_out, v_final_norm):
    w = dict(ffn1_norm=ffn1_norm, ffn1_w_gu=ffn1_w_gu, ffn1_w_down=ffn1_w_down, mix_norm=mix_norm, ffn2_norm=ffn2_norm, ffn2_w_gu=ffn2_w_gu, ffn2_w_down=ffn2_w_down, sb_w_qkv=sb_w_qkv, sb_w_o=sb_w_o, ssd_w_in=ssd_w_in, ssd_conv_w=ssd_conv_w, ssd_conv_b=ssd_conv_b, ssd_dt_bias=ssd_dt_bias, ssd_a_log=ssd_a_log, ssd_d=ssd_d, ssd_norm=ssd_norm, ssd_w_out=ssd_w_out, sc_w_in=sc_w_in, sc_conv_w=sc_conv_w, sc_w_out=sc_w_out, final_norm=final_norm)
    mom = dict(ffn1_norm=m_ffn1_norm, ffn1_w_gu=m_ffn1_w_gu, ffn1_w_down=m_ffn1_w_down, mix_norm=m_mix_norm, ffn2_norm=m_ffn2_norm, ffn2_w_gu=m_ffn2_w_gu, ffn2_w_down=m_ffn2_w_down, sb_w_qkv=m_sb_w_qkv, sb_w_o=m_sb_w_o, ssd_w_in=m_ssd_w_in, ssd_conv_w=m_ssd_conv_w, ssd_conv_b=m_ssd_conv_b, ssd_dt_bias=m_ssd_dt_bias, ssd_a_log=m_ssd_a_log, ssd_d=m_ssd_d, ssd_norm=m_ssd_norm, ssd_w_out=m_ssd_w_out, sc_w_in=m_sc_w_in, sc_conv_w=m_sc_conv_w, sc_w_out=m_sc_w_out, final_norm=m_final_norm)
    var = dict(ffn1_norm=v_ffn1_norm, ffn1_w_gu=v_ffn1_w_gu, ffn1_w_down=v_ffn1_w_down, mix_norm=v_mix_norm, ffn2_norm=v_ffn2_norm, ffn2_w_gu=v_ffn2_w_gu, ffn2_w_down=v_ffn2_w_down, sb_w_qkv=v_sb_w_qkv, sb_w_o=v_sb_w_o, ssd_w_in=v_ssd_w_in, ssd_conv_w=v_ssd_conv_w, ssd_conv_b=v_ssd_conv_b, ssd_dt_bias=v_ssd_dt_bias, ssd_a_log=v_ssd_a_log, ssd_d=v_ssd_d, ssd_norm=v_ssd_norm, ssd_w_out=v_ssd_w_out, sc_w_in=v_sc_w_in, sc_conv_w=v_sc_conv_w, sc_w_out=v_sc_w_out, final_norm=v_final_norm)
    me = 4 * lax.axis_index("x") + 2 * lax.axis_index("y") + lax.axis_index("c")
    big = [n for n, _, _ in _MATMUL_WEIGHTS]
    two_d = lambda a: a.reshape(-1, a.shape[-1])

    to_full = {n: f for n, f, _ in _MATMUL_WEIGHTS}
    to_parts = {n: f for n, _, f in _MATMUL_WEIGHTS}
    first = [(n, (0,)) for n in _FIRST_WEIGHTS]
    later = [(n, tuple(range(1 if n in _FIRST_WEIGHTS else 0, w[n].shape[0]))) for n in big]

    def shards(group):
        return [two_d(w[n][idx[0]:idx[-1] + 1].astype(BF16)) for n, idx in group]

    def layers(group, gathered):
        out = {}
        for (n, idx), g in zip(group, gathered):
            f = to_full[n](g.reshape((N_DEV, len(idx)) + w[n].shape[1:]))
            out[n] = [f[i] for i in range(len(idx))]
        return out

    gathered = _gather(shards(first) + [two_d(w[n]) for n in _CONV_WEIGHTS], name="gather_first")
    full = dict(w)
    full.update(layers(first, gathered))
    for n, g in zip(_CONV_WEIGHTS, gathered[len(first):]):
        full[n] = _col_full(g.reshape((N_DEV,) + w[n].shape))

    def with_later(wd, gathered_later):
        wd = dict(wd)
        for n, ls in layers(later, gathered_later).items():
            wd[n] = (wd[n] if n in _FIRST_WEIGHTS else []) + ls
        return wd

    later_shards = shards(later)
    gather_later = (_Hosted(_gather_steps, later_shards, _gather_shapes(later_shards), _GATHER_COPIES), with_later)

    def chip_sums(group, grads, tag):
        parts = []
        for n, idx in group:
            p8 = to_parts[n](jnp.stack([grads[n][i] for i in idx]).astype(BF16))
            parts.append(p8.reshape(4, 2, -1, p8.shape[-1]))
        from_sibling = _scatter_sibling(parts, name=f"scatter_sibling_{tag}")
        return [_pair_add(p, r, name=f"pair_add_{tag}_{n}") for (n, _), p, r in zip(group, parts, from_sibling)]

    def scatter_early(grads):
        ys = chip_sums(later, grads, "later")
        return _Hosted(_chip_scatter_steps, ys, _chip_scatter_shapes(ys), _CHIP_SCATTER_COPIES)

    loss_part, dx, grads, recv_later = _local_step(x[0], loss_target[0], full, gather_later, scatter_early)
    loss = lax.psum(loss_part[0, 0], ("x", "y", "c"))

    recv_first = _scatter_chips(chip_sums(first, grads, "first"), name="scatter_chips_first")
    contrib = {n: [r] for (n, _), r in zip(first, recv_first)}
    for (n, _), r in zip(later, recv_later):
        contrib.setdefault(n, []).append(r)
    out_g, out_d, out_m, out_v = {}, {}, {}, {}

    def update(n, parts):
        res = _adamw_reduce(parts, two_d(w[n]), two_d(mom[n]), two_d(var[n]), name=f"adamw_{n}")
        out_g[n], out_d[n], out_m[n], out_v[n] = (r.reshape(w[n].shape) for r in res)

    for n in big:
        update(n, contrib[n][0] if len(contrib[n]) == 1 else jnp.concatenate(contrib[n], axis=1))

    small = list(_REPLICATED) + list(_CONV_WEIGHTS)
    small_shapes = [grads[n].shape for n in small]
    spacked, scounts = _pack_rows([grads[n].astype(F32) for n in small], 8)
    sg = _unpack_rows(_gather([spacked], name="gather_small_grads")[0], scounts, small_shapes, (N_DEV,))
    sg = dict(zip(small, sg))
    rep_w, rcounts = _pack_rows([w[n] for n in _REPLICATED], 8)
    rep_m, _ = _pack_rows([mom[n] for n in _REPLICATED], 8)
    rep_v, _ = _pack_rows([var[n] for n in _REPLICATED], 8)
    rep_p = jnp.concatenate([_rows_of(sg[n].reshape(N_DEV, -1)[q]) for q in range(N_DEV) for n in _REPLICATED], axis=0)
    rep_p = rep_p.reshape(N_DEV, -1, _LANES)
    rep_p = jnp.pad(rep_p, ((0, 0), (0, rep_w.shape[0] - rep_p.shape[1]), (0, 0)))
    res = _adamw_reduce(rep_p, rep_w, rep_m, rep_v, name="adamw_replicated")
    rep_shapes = [w[n].shape for n in _REPLICATED]
    for tgt, r in zip((out_g, out_d, out_m, out_v), res):
        for n, a in zip(_REPLICATED, _unpack_rows(r, rcounts, rep_shapes)):
            tgt[n] = a
    for n in _CONV_WEIGHTS:
        c = w[n].shape[-1]
        mine = lax.dynamic_slice_in_dim(sg[n], me * c, c, axis=sg[n].ndim - 1)
        update(n, mine.reshape(N_DEV, -1, c))

    return (loss, dx[None], *[out_g[n] for n in _ORDER], *[out_d[n] for n in _ORDER],
            *[out_m[n] for n in _ORDER], *[out_v[n] for n in _ORDER])
```

```python
import functools
import math

import jax
import jax.numpy as jnp
from jax import lax
from jax.experimental import pallas as pl
from jax.experimental.pallas import tpu as pltpu

F32 = jnp.float32
BF16 = jnp.bfloat16

D_MODEL = 1024
D_FF = 2816
DEPTH = 4
N_DEV = 8
SB_HEADS = 16
SB_HEAD_DIM = 64
SB_TILE = 256
SB_HEADS_PER_STEP = 2
SB_FWD_GROUPS = (2, 1)
SB_BWD_GROUPS = (4, 2, 1)
SSD_HEADS = 32
SSD_HEAD_DIM = 64
SSD_GROUPS = 8
SSD_HPG = 4
SSD_STATE = 128
SSD_CHUNK = 128
SSD_D_INNER = 2048
SSD_CONV_DIM = 4096
SSD_IN_DIM = 6176
SSD_IN_PAD = 6272
SSD_NORM_GROUP = 256
RMS_EPS = 1e-6
ADAM_LR = 0.001
ADAM_B1 = 0.9
ADAM_B2 = 0.999
ADAM_EPS = 1e-08
ADAM_WD = 0.01
ADAM_STEP = 10
VMEM_LIMIT = 60 * 1024 * 1024

NT = (((1,), (1,)), ((), ()))
TN = (((0,), (0,)), ((), ()))
NN = (((1,), (0,)), ((), ()))


def _cparams(n_axes):
    return pltpu.CompilerParams(dimension_semantics=("arbitrary",) * n_axes, vmem_limit_bytes=VMEM_LIMIT)


def _tile(n, want, mult=8):
    if n <= want:
        return n
    for t in range(want, 0, -1):
        if n % t == 0 and t % mult == 0:
            return t
    return n


def _sigmoid(x):
    return 1.0 / (1.0 + jnp.exp(-x))


def _dot(a, b, dn=NN):
    return lax.dot_general(a, b, dn, preferred_element_type=F32)


def _split3(x):
    x1 = x.astype(BF16)
    r1 = x - x1.astype(F32)
    x2 = r1.astype(BF16)
    x3 = (r1 - x2.astype(F32)).astype(BF16)
    return x1, x2, x3


def _dot_exact(x, t):
    x1, x2, x3 = _split3(x)
    return _dot(x1, t) + _dot(x2, t) + _dot(x3, t)


def _dot_hilo(x, t):
    x1 = x.astype(BF16)
    x2 = (x - x1.astype(F32)).astype(BF16)
    return _dot(x1, t) + _dot(x2, t)


def _mm(a, b, *, name, ta=False, tb=False, sa=False, sb=False, so=False, tm=512, tn=1024, tk=1024,
        out_dtype=F32, epilogue=None, extras=(), outs=None, pair=None):
    ash, bsh = a.shape[-2:], b.shape[-2:]
    m, k = (ash[1], ash[0]) if ta else ash
    n = bsh[0] if tb else bsh[1]
    s_n = pair or (a.shape[0] if sa else (b.shape[0] if sb else 1))
    tm, tn, tk = _tile(m, tm), _tile(n, tn, 128), _tile(k, tk, 128)
    nk = k // tk
    if outs is None:
        outs = [(out_dtype, "stile" if so else "tile")]
    if epilogue is None:
        epilogue = lambda acc: (acc,)

    if ta:
        a_blk, a_idx = (tk, tm), (lambda j, i, kk: (kk, i))
    else:
        a_blk, a_idx = (tm, tk), (lambda j, i, kk: (i, kk))
    if tb:
        b_blk, b_idx = (tn, tk), (lambda j, i, kk: (j, kk))
    else:
        b_blk, b_idx = (tk, tn), (lambda j, i, kk: (kk, j))

    def lead(blk, idx, has_s):
        if not has_s:
            return pl.BlockSpec(blk, idx)
        return pl.BlockSpec((s_n,) + blk, lambda j, i, kk: (0,) + idx(j, i, kk))

    kinds = {
        "tile": lambda: pl.BlockSpec((tm, tn), lambda j, i, kk: (i, j)),
        "stile": lambda: pl.BlockSpec((s_n, tm, tn), lambda j, i, kk: (0, i, j)),
        "row": lambda: pl.BlockSpec((1, tn), lambda j, i, kk: (0, j)),
        "colsum": lambda: pl.BlockSpec((1, tn), lambda j, i, kk: (0, j)),
    }
    shapes = {"tile": (m, n), "stile": (s_n, m, n), "colsum": (1, n)}
    in_specs = [lead(a_blk, a_idx, sa), lead(b_blk, b_idx, sb)] + [kinds[kd]() for _, kd in extras]
    out_specs = [kinds[kd]() for _, kd in outs]
    out_shape = [jax.ShapeDtypeStruct(shapes[kd], dt) for dt, kd in outs]
    n_ex, n_out = len(extras), len(outs)
    dn = ((((0,) if ta else (1,)), ((1,) if tb else (0,))), ((), ()))
    acc_shape = (s_n, tm, tn) if so else (tm, tn)

    def body(*refs):
        a_ref, b_ref = refs[0], refs[1]
        ex_refs = refs[2:2 + n_ex]
        o_refs = refs[2 + n_ex:2 + n_ex + n_out]
        i = pl.program_id(1)
        kk = pl.program_id(2)

        def products():
            for s in range(s_n if (sa or sb) else 1):
                av = (a_ref[s] if sa else a_ref[...]).astype(BF16)
                bv = (b_ref[s] if sb else b_ref[...]).astype(BF16)
                yield s, lax.dot_general(av, bv, dn, preferred_element_type=F32)

        def finish(accv):
            vals = epilogue(accv, *[r[...] for r in ex_refs])
            for (dt, kd), o_ref, val in zip(outs, o_refs, vals):
                if kd == "colsum":
                    _accumulate(o_ref, val, i == 0)
                elif kd == "stile":
                    for s in range(s_n):
                        o_ref[s] = val[s].astype(dt)
                else:
                    o_ref[...] = val.astype(dt)

        if nk == 1:
            ds = [d for _, d in products()]
            finish(tuple(ds) if so else functools.reduce(jnp.add, ds))
            return

        acc = refs[-1]

        @pl.when(kk == 0)
        def _():
            acc[...] = jnp.zeros_like(acc)

        for s, d in products():
            if so:
                acc[s] += d
            else:
                acc[...] += d

        @pl.when(kk == nk - 1)
        def _():
            finish(tuple(acc[s] for s in range(s_n)) if so else acc[...])

    res = pl.pallas_call(
        body, name=name, grid=(n // tn, m // tm, nk),
        in_specs=in_specs, out_specs=out_specs, out_shape=out_shape,
        scratch_shapes=[pltpu.VMEM(acc_shape, F32)] if nk > 1 else [], compiler_params=_cparams(3),
    )(a, b, *[e for e, _ in extras])
    return res[0] if len(res) == 1 else res


def _accumulate(o_ref, val, first):
    @pl.when(first)
    def _():
        o_ref[...] = val

    @pl.when(jnp.logical_not(first))
    def _():
        o_ref[...] += val


def _rmsnorm(x, g, *, name):
    l, d = x.shape
    tm = _tile(l, 512)

    def body(x_ref, g_ref, o_ref):
        xv = x_ref[...]
        r = lax.rsqrt(jnp.mean(xv * xv, axis=1, keepdims=True) + RMS_EPS)
        o_ref[...] = (xv * r * g_ref[...]).astype(BF16)

    return pl.pallas_call(
        body, name=name, grid=(l // tm,),
        in_specs=[pl.BlockSpec((tm, d), lambda i: (i, 0)), pl.BlockSpec((1, d), lambda i: (0, 0))],
        out_specs=pl.BlockSpec((tm, d), lambda i: (i, 0)),
        out_shape=jax.ShapeDtypeStruct((l, d), BF16), compiler_params=_cparams(1),
    )(x, g)


def _norm_bwd_epilogue(dh, x, g, dres):
    r = lax.rsqrt(jnp.mean(x * x, axis=1, keepdims=True) + RMS_EPS)
    xh = x * r
    dg = jnp.sum(dh * xh, axis=0, keepdims=True)
    dxh = dh * g
    dx = r * (dxh - xh * jnp.mean(dxh * xh, axis=1, keepdims=True))
    return dres + dx, dg


def _final_loss(x, g, tgt, *, name):
    l, d = x.shape
    tm = _tile(l, 512)

    def body(x_ref, g_ref, t_ref, loss_ref, dx_ref, dg_ref):
        i = pl.program_id(0)
        xv, gv = x_ref[...], g_ref[...]
        r = lax.rsqrt(jnp.mean(xv * xv, axis=1, keepdims=True) + RMS_EPS)
        xh = xv * r
        e = xh * gv - t_ref[...]
        part = 0.5 * jnp.sum(jnp.mean(e * e, axis=1, keepdims=True), axis=0, keepdims=True)
        dy = e * (1.0 / d)
        dg = jnp.sum(dy * xh, axis=0, keepdims=True)
        dxh = dy * gv
        dx_ref[...] = r * (dxh - xh * jnp.mean(dxh * xh, axis=1, keepdims=True))
        _accumulate(dg_ref, dg, i == 0)
        _accumulate(loss_ref, jnp.broadcast_to(part, (1, 128)), i == 0)

    return pl.pallas_call(
        body, name=name, grid=(l // tm,),
        in_specs=[pl.BlockSpec((tm, d), lambda i: (i, 0)), pl.BlockSpec((1, d), lambda i: (0, 0)),
                  pl.BlockSpec((tm, d), lambda i: (i, 0))],
        out_specs=[pl.BlockSpec((1, 128), lambda i: (0, 0)), pl.BlockSpec((tm, d), lambda i: (i, 0)),
                   pl.BlockSpec((1, d), lambda i: (0, 0))],
        out_shape=[jax.ShapeDtypeStruct((1, 128), F32), jax.ShapeDtypeStruct((l, d), F32),
                   jax.ShapeDtypeStruct((1, d), F32)],
        compiler_params=_cparams(1),
    )(x, g, tgt)


def _ffn_fwd(x, g, wgu, wd, tag):
    h = _rmsnorm(x, g, name=f"{tag}_norm")

    def act(acc):
        gate, up = acc
        return acc, gate * _sigmoid(gate) * up

    gu, a = _mm(h, wgu, sb=True, so=True, tm=512, tn=1408, tk=1024, name=f"{tag}_up",
                outs=[(BF16, "stile"), (BF16, "tile")], epilogue=act)
    xo = _mm(a, wd, tm=1024, tn=1024, tk=2816, name=f"{tag}_down", extras=[(x, "tile")],
             epilogue=lambda acc, xt: (xt + 0.5 * acc,))
    return xo, (x, h, gu, a)


def _ffn_bwd(dout, saved, g, wgu, wd, tag):
    x, h, gu, a = saved

    def act_bwd(acc, guv):
        da = 0.5 * acc
        gate, up = guv[0].astype(F32), guv[1].astype(F32)
        s = _sigmoid(gate)
        return ((da * up * s * (1.0 + gate * (1.0 - s)), da * gate * s),)

    dgu = _mm(dout, wd, tb=True, pair=2, tm=512, tn=1408, tk=1024, name=f"{tag}_dact", extras=[(gu, "stile")],
              outs=[(BF16, "stile")], epilogue=act_bwd)
    dwd = _mm(a, dout, ta=True, tm=1408, tn=1024, tk=2048, name=f"{tag}_dwd", out_dtype=BF16,
              epilogue=lambda acc: (0.5 * acc,))
    dwgu = _mm(h, dgu, ta=True, sb=True, so=True, tm=512, tn=1408, tk=2048, name=f"{tag}_dwgu", out_dtype=BF16)
    dx, dg = _mm(dgu, wgu, tb=True, sa=True, sb=True, tm=512, tn=1024, tk=2816, name=f"{tag}_dx",
                 extras=[(x, "tile"), (g, "row"), (dout, "tile")], outs=[(F32, "tile"), (F32, "colsum")],
                 epilogue=_norm_bwd_epilogue)
    return dx, dg, dwgu, dwd


def _sb_plan(n, sizes):
    digits = [n // sizes[0]] + [(n // s) % 2 for s in sizes[1:]]
    plan, none_smaller = [], 1
    for size, d in reversed(list(zip(sizes, digits))):
        has = jnp.minimum(d, 1)
        with_diag = none_smaller * has
        plan.append((size, True, with_diag))
        if size > 1:
            plan.append((size, False, d - with_diag))
        none_smaller = none_smaller * (1 - has)
    return plan


def _sb_sweep(plan, start, step, fn, carry):
    pos = start
    for size, with_diag, trips in plan:
        diag = 0 if step < 0 else size - 1

        def trip(it, cr, size=size, with_diag=with_diag, pos=pos, diag=diag):
            base = pos + step * size * it
            return fn([base + step * b for b in range(size)], cr, [with_diag and b == diag for b in range(size)])

        carry = lax.fori_loop(0, trips, trip, carry)
        pos = pos + step * size * trips
    return carry


def _sb_logs(z):
    lb = jnp.minimum(z, 0.0) - jnp.log(1.0 + jnp.exp(-jnp.abs(z)))
    return lb, lb - z


class _Hosted:
    def __init__(self, steps, xs, out_shapes, copies):
        self.steps, self.xs, self.n, self.copies = steps, list(xs), len(xs), copies
        self.out_shape = [jax.ShapeDtypeStruct(s, x.dtype) for s, x in zip(out_shapes, xs)]
        self.specs = [_HBM] * self.n
        self.sems = [pltpu.SemaphoreType.DMA((self.n, copies)), pltpu.SemaphoreType.DMA((self.n, copies)),
                     pltpu.SemaphoreType.DMA((self.n,))]

    def run(self, x_refs, o_refs, sems, grid):
        ids = [pl.program_id(a) for a in range(len(grid))]
        first = functools.reduce(jnp.logical_and, [p == 0 for p in ids])
        last = functools.reduce(jnp.logical_and, [p == g - 1 for p, g in zip(ids, grid)])
        start, finish = self.steps(x_refs, o_refs, *sems)
        pl.when(first)(start)
        return lambda: pl.when(last)(finish)


def _head_masks(hs):
    lane = lax.broadcasted_iota(jnp.int32, (1, hs * SB_HEAD_DIM), 1)
    return [jnp.logical_and(lane >= hh * SB_HEAD_DIM, lane < (hh + 1) * SB_HEAD_DIM) for hh in range(hs)]


def _sb_fwd(qkv, *, name, hosted=None):
    l = qkv.shape[0]
    d_model = qkv.shape[1] // 3
    dh = SB_HEAD_DIM
    t = _tile(l, SB_TILE)
    hs = 2 * SB_HEADS_PER_STEP
    w = hs * dh
    n_grp = d_model // w
    scale = dh ** -0.5
    grid = (n_grp, l // t)
    nh = hosted.n if hosted else 0

    def body(q_ref, k_ref, v_ref, *rest):
        o_ref = rest[nh]
        at_end = hosted.run(rest[:nh], rest[nh + 1:2 * nh + 1], rest[2 * nh + 1:], grid) if hosted else None
        i = pl.program_id(1)
        heads = _head_masks(hs)
        q_all = (q_ref[...].astype(F32) * scale).astype(BF16)
        qs = [jnp.where(heads[hh], q_all, jnp.zeros_like(q_all)) for hh in range(hs)]
        row = lax.broadcasted_iota(jnp.int32, (t, t), 0)
        col = lax.broadcasted_iota(jnp.int32, (t, t), 1)
        strict = col < row
        tri = strict.astype(BF16)

        def block(jbs, carry, masks):
            sls = [pl.ds(pl.multiple_of(jb * t, t), t) for jb in jbs]
            chains = [(hh, b) for b in range(len(jbs)) for hh in range(hs)]
            ks = [k_ref[sl, :] for sl in sls]
            zs = {(hh, b): _dot(qs[hh], ks[b], NT) for hh, b in chains}
            lbs, tails, sums = {}, {}, {}
            for hh, b in chains:
                lb, lk = _sb_logs(zs[hh, b])
                if masks[b]:
                    lk = jnp.where(strict, lk, 0.0)
                lbs[hh, b] = lb
                tails[hh, b] = _dot_hilo(lk, tri)
                sums[hh, b] = jnp.sum(lk, axis=1, keepdims=True)
            cs, o = list(carry[0]), carry[1]
            for b in range(len(jbs)):
                atts = []
                for hh in range(hs):
                    att = jnp.exp(lbs[hh, b] + tails[hh, b] + cs[hh])
                    if masks[b]:
                        att = jnp.where(strict, att, 0.0)
                    atts.append(att.astype(BF16))
                    cs[hh] = cs[hh] + sums[hh, b]
                vb = v_ref[sls[b], :]
                v_heads = jnp.concatenate([jnp.where(heads[hh], vb, jnp.zeros_like(vb)) for hh in range(hs)], axis=0)
                o = o + _dot(jnp.concatenate(atts, axis=1), v_heads)
            return tuple(cs), o

        carry = (tuple(jnp.zeros((t, 1), F32) for _ in range(hs)), jnp.zeros((t, w), F32))
        carry = _sb_sweep(_sb_plan(i + 1, SB_FWD_GROUPS), i, -1, block, carry)
        o_ref[...] = carry[1].astype(o_ref.dtype)
        if hosted:
            at_end()

    blocks = d_model // w
    res = pl.pallas_call(
        body, name=name, grid=grid,
        in_specs=[pl.BlockSpec((t, w), lambda g, i: (i, g)), pl.BlockSpec((l, w), lambda g, i: (0, blocks + g)),
                  pl.BlockSpec((l, w), lambda g, i: (0, 2 * blocks + g))] + (hosted.specs if hosted else []),
        out_specs=[pl.BlockSpec((t, w), lambda g, i: (i, g))] + (hosted.specs if hosted else []),
        out_shape=[jax.ShapeDtypeStruct((l, d_model), BF16)] + (hosted.out_shape if hosted else []),
        scratch_shapes=hosted.sems if hosted else [], compiler_params=_cparams(2),
    )(qkv, qkv, qkv, *(hosted.xs if hosted else []))
    return (res[0], res[1:]) if hosted else res[0]


def _sb_bwd(qkv, do, *, name, hosted=None):
    l = qkv.shape[0]
    d_model = qkv.shape[1] // 3
    dh = SB_HEAD_DIM
    t = _tile(l, SB_TILE)
    nq = l // t
    hs = SB_HEADS_PER_STEP
    w = hs * dh
    blocks = d_model // w
    scale = dh ** -0.5
    grid = (blocks, nq)
    nh = hosted.n if hosted else 0

    def body(q_ref, k_ref, v_ref, do_ref, *rest):
        dq_ref, dk_ref, dv_ref = rest[nh:nh + 3]
        e_scr, s_scr = rest[2 * nh + 3:2 * nh + 5]
        at_end = hosted.run(rest[:nh], rest[nh + 3:2 * nh + 3], rest[2 * nh + 5:], grid) if hosted else None
        i = pl.program_id(1)

        @pl.when(i == 0)
        def _():
            dk_ref[...] = jnp.zeros_like(dk_ref)
            dv_ref[...] = jnp.zeros_like(dv_ref)

        heads = _head_masks(hs)
        q_all = (q_ref[...].astype(F32) * scale).astype(BF16)
        do_all = do_ref[...]
        qs = [jnp.where(heads[hh], q_all, jnp.zeros_like(q_all)) for hh in range(hs)]
        dos = [jnp.where(heads[hh], do_all, jnp.zeros_like(do_all)) for hh in range(hs)]
        row = lax.broadcasted_iota(jnp.int32, (t, t), 0)
        col = lax.broadcasted_iota(jnp.int32, (t, t), 1)
        strict = col < row
        tri_suffix = strict.astype(BF16)
        tri_prefix = (row < col).astype(BF16)

        def sweep1(jbs, cs, masks):
            sls = [pl.ds(pl.multiple_of(jb * t, t), t) for jb in jbs]
            chains = [(hh, b) for b in range(len(jbs)) for hh in range(hs)]
            zs = {(hh, b): _dot(qs[hh], k_ref[sls[b], :], NT) for hh, b in chains}
            datts = {(hh, b): _dot(dos[hh], v_ref[sls[b], :], NT) for hh, b in chains}
            lbs, tails, sums = {}, {}, {}
            for hh, b in chains:
                lb, lk = _sb_logs(zs[hh, b])
                if masks[b]:
                    lk = jnp.where(strict, lk, 0.0)
                lbs[hh, b] = lb
                tails[hh, b] = _dot_hilo(lk, tri_suffix)
                s_scr[hh, jbs[b]] = jnp.exp(lb)
                sums[hh, b] = jnp.sum(lk, axis=1, keepdims=True)
            cs = list(cs)
            for hh, b in chains:
                att = jnp.exp(lbs[hh, b] + tails[hh, b] + cs[hh])
                if masks[b]:
                    att = jnp.where(strict, att, 0.0)
                e_scr[hh, jbs[b]] = att * datts[hh, b]
                dv_ref[sls[b], :] += _dot(att.astype(BF16), dos[hh], TN)
                cs[hh] = cs[hh] + sums[hh, b]
            return tuple(cs)

        plan = _sb_plan(i + 1, SB_BWD_GROUPS)
        _sb_sweep(plan, i, -1, sweep1, tuple(jnp.zeros((t, 1), F32) for _ in range(hs)))

        def sweep2(jbs, carry, masks):
            sls = [pl.ds(pl.multiple_of(jb * t, t), t) for jb in jbs]
            chains = [(hh, b) for b in range(len(jbs)) for hh in range(hs)]
            des = {(hh, b): e_scr[hh, jbs[b]] for hh, b in chains}
            pres = {(hh, b): _dot_hilo(des[hh, b], tri_prefix) for hh, b in chains}
            carry = [list(c) for c in carry]
            for hh, b in chains:
                p, dq = carry[hh]
                de, sg = des[hh, b], s_scr[hh, jbs[b]]
                dlk = p + pres[hh, b]
                if masks[b]:
                    dlk = jnp.where(strict, dlk, 0.0)
                dz = (de - sg * (de + dlk)).astype(BF16)
                dk_ref[sls[b], :] += _dot(dz, qs[hh], TN)
                carry[hh] = [p + jnp.sum(de, axis=1, keepdims=True), dq + _dot(dz, k_ref[sls[b], :])]
            return tuple(tuple(c) for c in carry)

        carry = tuple((jnp.zeros((t, 1), F32), jnp.zeros((t, w), F32)) for _ in range(hs))
        carry = _sb_sweep(plan[::-1], 0, 1, sweep2, carry)
        dq = jnp.zeros((t, w), F32)
        for hh in range(hs):
            dq = jnp.where(heads[hh], carry[hh][1], dq)
        dq_ref[...] = (dq * scale).astype(dq_ref.dtype)
        if hosted:
            at_end()

    qspec = pl.BlockSpec((t, w), lambda g, i: (i, g))
    cols = lambda off: pl.BlockSpec((l, w), lambda g, i: (0, off + g))
    res = pl.pallas_call(
        body, name=name, grid=grid,
        in_specs=[qspec, cols(blocks), cols(2 * blocks), qspec] + (hosted.specs if hosted else []),
        out_specs=[qspec, cols(0), cols(0)] + (hosted.specs if hosted else []),
        out_shape=[jax.ShapeDtypeStruct((l, d_model), BF16), jax.ShapeDtypeStruct((l, d_model), F32),
                   jax.ShapeDtypeStruct((l, d_model), F32)] + (hosted.out_shape if hosted else []),
        scratch_shapes=[pltpu.VMEM((hs, nq, t, t), F32), pltpu.VMEM((hs, nq, t, t), F32)]
        + (hosted.sems if hosted else []),
        compiler_params=_cparams(2),
    )(qkv, qkv, qkv, do, *(hosted.xs if hosted else []))
    return (res[0], res[1], res[2], res[3:]) if hosted else res


def _sb_layer_fwd(x, g, wqkv, wo, tag, hosted=None):
    h = _rmsnorm(x, g, name=f"{tag}_norm")
    qkv = _mm(h, wqkv, tm=1024, tn=1024, tk=1024, name=f"{tag}_qkv", out_dtype=BF16)
    o = _sb_fwd(qkv, name=f"{tag}_attn", hosted=hosted)
    carried = None
    if hosted:
        o, carried = o
    xo = _mm(o, wo, tm=1024, tn=1024, tk=1024, name=f"{tag}_out", extras=[(x, "tile")],
             epilogue=lambda acc, xt: (xt + acc,))
    return xo, (x, h, qkv, o), carried


def _sb_layer_bwd(dout, saved, g, wqkv, wo, tag, hosted=None):
    x, h, qkv, o = saved
    do = _mm(dout, wo, tb=True, tm=1024, tn=1024, tk=1024, name=f"{tag}_do", out_dtype=BF16)
    dwo = _mm(o, dout, ta=True, tm=1024, tn=1024, tk=2048, name=f"{tag}_dwo", out_dtype=BF16)
    res = _sb_bwd(qkv, do, name=f"{tag}_attn_bwd", hosted=hosted)
    dq, dk, dv = res[:3]
    carried = res[3] if hosted else None
    dqkv = jnp.concatenate([dq, dk.astype(BF16), dv.astype(BF16)], axis=1)
    dwqkv = _mm(h, dqkv, ta=True, tm=1024, tn=1024, tk=2048, name=f"{tag}_dwqkv", out_dtype=BF16)
    dx, dg = _mm(dqkv, wqkv, tb=True, tm=512, tn=1024, tk=3072, name=f"{tag}_dx",
                 extras=[(x, "tile"), (g, "row"), (dout, "tile")], outs=[(F32, "tile"), (F32, "colsum")],
                 epilogue=_norm_bwd_epilogue)
    return dx, dg, dwqkv, dwo, carried


def _shift_down(x, s, t_idx):
    return jnp.where(t_idx >= s, pltpu.roll(x, s, 0), 0.0)


def _shift_up(x, s, t_idx):
    n = x.shape[0]
    return jnp.where(t_idx < n - s, pltpu.roll(x, n - s, 0), 0.0)


def _sc_fwd(p, cw, *, name):
    l = p.shape[0]
    d = cw.shape[1]
    tc = 128
    nb = d // tc

    def body(b_ref, c_ref, h_ref, w_ref, o_ref):
        v = c_ref[...] * h_ref[...]
        t_idx = lax.broadcasted_iota(jnp.int32, v.shape, 0)
        u = v * w_ref[2:3, :] + _shift_down(v, 1, t_idx) * w_ref[1:2, :] + _shift_down(v, 2, t_idx) * w_ref[0:1, :]
        o_ref[...] = (b_ref[...] * u).astype(BF16)

    return pl.pallas_call(
        body, name=name, grid=(nb,),
        in_specs=[pl.BlockSpec((l, tc), lambda j: (0, j)), pl.BlockSpec((l, tc), lambda j: (0, nb + j)),
                  pl.BlockSpec((l, tc), lambda j: (0, 2 * nb + j)), pl.BlockSpec((3, tc), lambda j: (0, j))],
        out_specs=pl.BlockSpec((l, tc), lambda j: (0, j)),
        out_shape=jax.ShapeDtypeStruct((l, d), BF16), compiler_params=_cparams(1),
    )(p, p, p, cw)


def _sc_bwd(p, cw, dbu, *, name):
    l = p.shape[0]
    d = cw.shape[1]
    tc = 128
    nb = d // tc

    def body(b_ref, c_ref, h_ref, w_ref, g_ref, db_ref, dc_ref, dh_ref, dw_ref):
        cv, hv = c_ref[...], h_ref[...]
        v = cv * hv
        t_idx = lax.broadcasted_iota(jnp.int32, v.shape, 0)
        v1, v2 = _shift_down(v, 1, t_idx), _shift_down(v, 2, t_idx)
        u = v * w_ref[2:3, :] + v1 * w_ref[1:2, :] + v2 * w_ref[0:1, :]
        dbu_v = g_ref[...]
        db_ref[...] = (dbu_v * u).astype(BF16)
        du = dbu_v * b_ref[...]
        dv = du * w_ref[2:3, :] + _shift_up(du, 1, t_idx) * w_ref[1:2, :] + _shift_up(du, 2, t_idx) * w_ref[0:1, :]
        dc_ref[...] = (dv * hv).astype(BF16)
        dh_ref[...] = (dv * cv).astype(BF16)
        dw_ref[...] = jnp.zeros_like(dw_ref)
        dw_ref[0:1, :] = jnp.sum(du * v2, axis=0, keepdims=True)
        dw_ref[1:2, :] = jnp.sum(du * v1, axis=0, keepdims=True)
        dw_ref[2:3, :] = jnp.sum(du * v, axis=0, keepdims=True)

    col = lambda off: pl.BlockSpec((l, tc), lambda j: (0, off + j))
    return pl.pallas_call(
        body, name=name, grid=(nb,),
        in_specs=[col(0), col(nb), col(2 * nb), pl.BlockSpec((3, tc), lambda j: (0, j)), col(0)],
        out_specs=[col(0), col(0), col(0), pl.BlockSpec((8, tc), lambda j: (0, j))],
        out_shape=[jax.ShapeDtypeStruct((l, d), BF16)] * 3 + [jax.ShapeDtypeStruct((8, d), F32)],
        compiler_params=_cparams(1),
    )(p, p, p, cw, dbu)


def _sc_layer_fwd(x, g, win, cw, wout, tag):
    h = _rmsnorm(x, g, name=f"{tag}_norm")
    p = _mm(h, win, tm=1024, tn=1024, tk=1024, name=f"{tag}_in")
    bu = _sc_fwd(p, cw, name=f"{tag}_conv")
    xo = _mm(bu, wout, tm=1024, tn=1024, tk=1024, name=f"{tag}_out", extras=[(x, "tile")],
             epilogue=lambda acc, xt: (xt + acc,))
    return xo, (x, h, p, bu)


def _sc_layer_bwd(dout, saved, g, win, cw, wout, tag):
    x, h, p, bu = saved
    dbu = _mm(dout, wout, tb=True, tm=1024, tn=1024, tk=1024, name=f"{tag}_dbu")
    dwout = _mm(bu, dout, ta=True, tm=1024, tn=1024, tk=2048, name=f"{tag}_dwout", out_dtype=BF16)
    db, dc, dh, dcw = _sc_bwd(p, cw, dbu, name=f"{tag}_conv_bwd")
    dp = jnp.concatenate([db, dc, dh], axis=1)
    dwin = _mm(h, dp, ta=True, tm=1024, tn=1024, tk=2048, name=f"{tag}_dwin", out_dtype=BF16)
    dx, dg = _mm(dp, win, tb=True, tm=512, tn=1024, tk=3072, name=f"{tag}_dx",
                 extras=[(x, "tile"), (g, "row"), (dout, "tile")], outs=[(F32, "tile"), (F32, "colsum")],
                 epilogue=_norm_bwd_epilogue)
    return dx, dg, dwin, dcw[:3], dwout


def _ssd_conv_fwd(p, cw, cb, *, name):
    l = p.shape[0]
    tc = 128
    nb = SSD_CONV_DIM // tc
    off = SSD_D_INNER // tc

    def body(x_ref, w_ref, b_ref, o_ref):
        xv = x_ref[...]
        t_idx = lax.broadcasted_iota(jnp.int32, xv.shape, 0)
        pre = xv * w_ref[3:4, :] + b_ref[...]
        for s in (1, 2, 3):
            pre = pre + _shift_down(xv, s, t_idx) * w_ref[3 - s:4 - s, :]
        o_ref[...] = pre * _sigmoid(pre)

    return pl.pallas_call(
        body, name=name, grid=(nb,),
        in_specs=[pl.BlockSpec((l, tc), lambda j: (0, off + j)), pl.BlockSpec((4, tc), lambda j: (0, j)),
                  pl.BlockSpec((1, tc), lambda j: (0, j))],
        out_specs=pl.BlockSpec((l, tc), lambda j: (0, j)),
        out_shape=jax.ShapeDtypeStruct((l, SSD_CONV_DIM), F32), compiler_params=_cparams(1),
    )(p, cw, cb)


def _ssd_conv_bwd(p, cw, cb, dact, *, name):
    l = p.shape[0]
    tc = 128
    nb = SSD_CONV_DIM // tc
    off = SSD_D_INNER // tc

    def body(x_ref, w_ref, b_ref, g_ref, dx_ref, dw_ref):
        xv = x_ref[...]
        t_idx = lax.broadcasted_iota(jnp.int32, xv.shape, 0)
        xs = [xv] + [_shift_down(xv, s, t_idx) for s in (1, 2, 3)]
        pre = b_ref[...] + xs[0] * w_ref[3:4, :]
        for s in (1, 2, 3):
            pre = pre + xs[s] * w_ref[3 - s:4 - s, :]
        sg = _sigmoid(pre)
        dpre = g_ref[...] * sg * (1.0 + pre * (1.0 - sg))
        dx = dpre * w_ref[3:4, :]
        for s in (1, 2, 3):
            dx = dx + _shift_up(dpre, s, t_idx) * w_ref[3 - s:4 - s, :]
        dx_ref[...] = dx
        dw_ref[...] = jnp.zeros_like(dw_ref)
        for s in (0, 1, 2, 3):
            dw_ref[3 - s:4 - s, :] = jnp.sum(dpre * xs[s], axis=0, keepdims=True)
        dw_ref[4:5, :] = jnp.sum(dpre, axis=0, keepdims=True)

    return pl.pallas_call(
        body, name=name, grid=(nb,),
        in_specs=[pl.BlockSpec((l, tc), lambda j: (0, off + j)), pl.BlockSpec((4, tc), lambda j: (0, j)),
                  pl.BlockSpec((1, tc), lambda j: (0, j)), pl.BlockSpec((l, tc), lambda j: (0, j))],
        out_specs=[pl.BlockSpec((l, tc), lambda j: (0, j)), pl.BlockSpec((8, tc), lambda j: (0, j))],
        out_shape=[jax.ShapeDtypeStruct((l, SSD_CONV_DIM), F32), jax.ShapeDtypeStruct((8, SSD_CONV_DIM), F32)],
        compiler_params=_cparams(1),
    )(p, cw, cb, dact)


def _ssd_dt_fwd(p, bias, *, name):
    l = p.shape[0]
    tm = _tile(l, 1024)
    off = (SSD_D_INNER + SSD_CONV_DIM) // 128

    def body(x_ref, b_ref, o_ref):
        v = x_ref[...] + b_ref[...]
        o_ref[...] = jnp.maximum(v, 0.0) + jnp.log(1.0 + jnp.exp(-jnp.abs(v)))

    return pl.pallas_call(
        body, name=name, grid=(l // tm,),
        in_specs=[pl.BlockSpec((tm, 128), lambda i: (i, off)), pl.BlockSpec((1, 128), lambda i: (0, 0))],
        out_specs=pl.BlockSpec((tm, 128), lambda i: (i, 0)),
        out_shape=jax.ShapeDtypeStruct((l, 128), F32), compiler_params=_cparams(1),
    )(p, bias)


def _ssd_dt_bwd(p, bias, ddt, *, name):
    l = p.shape[0]
    tm = _tile(l, 1024)
    off = (SSD_D_INNER + SSD_CONV_DIM) // 128

    def body(x_ref, b_ref, g_ref, o_ref, db_ref):
        i = pl.program_id(0)
        d = g_ref[...] * _sigmoid(x_ref[...] + b_ref[...])
        o_ref[...] = d
        _accumulate(db_ref, jnp.sum(d, axis=0, keepdims=True), i == 0)

    return pl.pallas_call(
        body, name=name, grid=(l // tm,),
        in_specs=[pl.BlockSpec((tm, 128), lambda i: (i, off)), pl.BlockSpec((1, 128), lambda i: (0, 0)),
                  pl.BlockSpec((tm, 128), lambda i: (i, 0))],
        out_specs=[pl.BlockSpec((tm, 128), lambda i: (i, 0)), pl.BlockSpec((1, 128), lambda i: (0, 0))],
        out_shape=[jax.ShapeDtypeStruct((l, 128), F32), jax.ShapeDtypeStruct((1, 128), F32)],
        compiler_params=_cparams(1),
    )(p, bias, ddt)


def _row_to_col(r, eye):
    return jnp.sum(jnp.where(eye, r, 0.0), axis=1, keepdims=True)


def _col_to_row(c, eye):
    return jnp.sum(jnp.where(eye, c, 0.0), axis=0, keepdims=True)


def _ssd_chunk_common(b_ref, c_ref, dt_ref, a_ref, lam_scr):
    n = SSD_CHUNK
    row = lax.broadcasted_iota(jnp.int32, (n, n), 0)
    col = lax.broadcasted_iota(jnp.int32, (n, n), 1)
    bm, cm = b_ref[...].astype(BF16), c_ref[...].astype(BF16)
    g = _dot(cm, bm, NT)
    incl = (row <= col).astype(BF16)
    lam_scr[...] = _dot_exact(dt_ref[...] * a_ref[...], incl)
    return row, col, bm, cm, g


def _ssd_head_common(r, row, col, dt_ref, lam_scr):
    eye, tril = row == col, row >= col
    lam_r = lam_scr[r:r + 1, :]
    dt_r = dt_ref[r:r + 1, :]
    lam_c = _row_to_col(lam_r, eye)
    dt_c = _row_to_col(dt_r, eye)
    dk = jnp.where(tril, jnp.exp(jnp.minimum(lam_c - lam_r, 0.0)), 0.0)
    lam_last = jnp.sum(jnp.where(col[0:1, :] == SSD_CHUNK - 1, lam_r, 0.0), axis=1, keepdims=True)
    return eye, lam_r, dt_r, lam_c, dt_c, dk, lam_last


def _ssd_fwd(xh, act, dt_t, a_b, *, name):
    l = xh.shape[1]
    nc = l // SSD_CHUNK
    n, p_dim, hpg = SSD_CHUNK, SSD_HEAD_DIM, SSD_HPG

    def body(x_ref, b_ref, c_ref, dt_ref, a_ref, y_ref, hp_ref, h_scr, lam_scr):
        @pl.when(pl.program_id(1) == 0)
        def _():
            h_scr[...] = jnp.zeros_like(h_scr)

        row, col, bm, cm, g = _ssd_chunk_common(b_ref, c_ref, dt_ref, a_ref, lam_scr)
        for r in range(hpg):
            _, _, dt_r, lam_c, dt_c, dk, lam_last = _ssd_head_common(r, row, col, dt_ref, lam_scr)
            xr = x_ref[r]
            hr = h_scr[r]
            w = (g * dk * dt_r).astype(BF16)
            y = _dot(w, xr.astype(BF16)) + _dot(cm, hr.astype(BF16), NT) * jnp.exp(lam_c)
            y_ref[r] = y
            hp_ref[r] = hr
            xw = (xr * (jnp.exp(lam_last - lam_c) * dt_c)).astype(BF16)
            h_scr[r] = jnp.exp(lam_last) * hr + _dot(xw, bm, TN)

    g_off = SSD_D_INNER // SSD_STATE
    return pl.pallas_call(
        body, name=name, grid=(SSD_GROUPS, nc),
        in_specs=[pl.BlockSpec((hpg, n, p_dim), lambda g, c: (g, c, 0)),
                  pl.BlockSpec((n, SSD_STATE), lambda g, c: (c, g_off + g)),
                  pl.BlockSpec((n, SSD_STATE), lambda g, c: (c, g_off + SSD_GROUPS + g)),
                  pl.BlockSpec((None, 8, n), lambda g, c: (g, 0, c)),
                  pl.BlockSpec((None, 8, 128), lambda g, c: (g, 0, 0))],
        out_specs=[pl.BlockSpec((hpg, n, p_dim), lambda g, c: (g, c, 0)),
                   pl.BlockSpec((None, hpg, p_dim, SSD_STATE), lambda g, c: (c, g, 0, 0))],
        out_shape=[jax.ShapeDtypeStruct(xh.shape, F32),
                   jax.ShapeDtypeStruct((nc, SSD_HEADS, p_dim, SSD_STATE), F32)],
        scratch_shapes=[pltpu.VMEM((hpg, p_dim, SSD_STATE), F32), pltpu.VMEM((8, n), F32)],
        compiler_params=_cparams(2),
    )(xh, act, act, dt_t, a_b)


def _ssd_bwd(xh, act, dt_t, a_b, hprev, dyh, *, name):
    l = xh.shape[1]
    nc = l // SSD_CHUNK
    n, p_dim, hpg = SSD_CHUNK, SSD_HEAD_DIM, SSD_HPG

    def body(x_ref, b_ref, c_ref, dt_ref, a_ref, hp_ref, dy_ref,
             dx_ref, db_ref, dc_ref, ddt_ref, da_ref, dh_scr, lam_scr, dlam_scr, ddt_scr):
        ci = pl.program_id(1)

        @pl.when(ci == 0)
        def _():
            dh_scr[...] = jnp.zeros_like(dh_scr)

        row, col, bm, cm, g = _ssd_chunk_common(b_ref, c_ref, dt_ref, a_ref, lam_scr)
        dlam_scr[...] = jnp.zeros_like(dlam_scr)
        ddt_scr[...] = jnp.zeros_like(ddt_scr)
        dg_acc = jnp.zeros((n, n), F32)
        dc_acc = jnp.zeros((n, SSD_STATE), F32)
        db_acc = jnp.zeros((n, SSD_STATE), F32)
        for r in range(hpg):
            eye, _, dt_r, lam_c, dt_c, dk, lam_last = _ssd_head_common(r, row, col, dt_ref, lam_scr)
            xr, dyr, hr, dhr = x_ref[r], dy_ref[r], hp_ref[r], dh_scr[r]
            xb, dyb, hb, dhb = xr.astype(BF16), dyr.astype(BF16), hr.astype(BF16), dhr.astype(BF16)
            e_l = jnp.exp(lam_c)
            e_last = jnp.exp(lam_last)
            decay_c = jnp.exp(lam_last - lam_c)
            w_c = decay_c * dt_c
            m = g * dk * dt_r
            dm = _dot(dyb, xb, NT)
            bdh = _dot(bm, dhb, NT)
            dx_ref[r] = _dot(m.astype(BF16), dyb, TN) + w_c * bdh
            dg_acc = dg_acc + dm * dk * dt_r
            q_mat = dm * g * dk
            p_mat = q_mat * dt_r
            yoff = _dot(cm, hb, NT) * e_l
            q_c = jnp.sum(xr * bdh, axis=1, keepdims=True)
            dlam_c = (jnp.sum(p_mat, axis=1, keepdims=True) + jnp.sum(dyr * yoff, axis=1, keepdims=True)
                      - w_c * q_c)
            d_last = (jnp.sum(w_c * q_c, axis=0, keepdims=True)
                      + e_last * jnp.sum(jnp.sum(dhr * hr, axis=1, keepdims=True), axis=0, keepdims=True))
            dlam_scr[r:r + 1, :] = (_col_to_row(dlam_c, eye) - jnp.sum(p_mat, axis=0, keepdims=True)
                                    + jnp.where(col[0:1, :] == n - 1, d_last, 0.0))
            ddt_scr[r:r + 1, :] = jnp.sum(q_mat, axis=0, keepdims=True) + _col_to_row(decay_c * q_c, eye)
            dc_acc = dc_acc + e_l * _dot(dyb, hb)
            db_acc = db_acc + _dot((xr * w_c).astype(BF16), dhb)
            dh_scr[r] = e_last * dhr + _dot((dyr * e_l).astype(BF16), cm, TN)

        dgb = dg_acc.astype(BF16)
        dc_ref[...] = _dot(dgb, bm) + dc_acc
        db_ref[...] = _dot(dgb, cm, TN) + db_acc
        rev = (row >= col).astype(BF16)
        da = _dot_exact(dlam_scr[...], rev)
        ddt_ref[...] = ddt_scr[...] + da * a_ref[...]
        _accumulate(da_ref, da * dt_ref[...], ci == 0)

        @pl.when(ci == nc - 1)
        def _():
            da_ref[...] = jnp.broadcast_to(jnp.sum(da_ref[...], axis=1, keepdims=True), da_ref.shape)

    g_off = SSD_D_INNER // SSD_STATE
    rc = lambda c: nc - 1 - c
    hspec = pl.BlockSpec((hpg, n, p_dim), lambda g, c: (g, rc(c), 0))
    gspec = pl.BlockSpec((n, SSD_STATE), lambda g, c: (rc(c), g))
    return pl.pallas_call(
        body, name=name, grid=(SSD_GROUPS, nc),
        in_specs=[hspec,
                  pl.BlockSpec((n, SSD_STATE), lambda g, c: (rc(c), g_off + g)),
                  pl.BlockSpec((n, SSD_STATE), lambda g, c: (rc(c), g_off + SSD_GROUPS + g)),
                  pl.BlockSpec((None, 8, n), lambda g, c: (g, 0, rc(c))),
                  pl.BlockSpec((None, 8, 128), lambda g, c: (g, 0, 0)),
                  pl.BlockSpec((None, hpg, p_dim, SSD_STATE), lambda g, c: (rc(c), g, 0, 0)),
                  hspec],
        out_specs=[hspec, gspec, gspec,
                   pl.BlockSpec((None, 8, n), lambda g, c: (g, 0, rc(c))),
                   pl.BlockSpec((None, 8, 128), lambda g, c: (g, 0, 0))],
        out_shape=[jax.ShapeDtypeStruct(xh.shape, F32),
                   jax.ShapeDtypeStruct((l, SSD_GROUPS * SSD_STATE), F32),
                   jax.ShapeDtypeStruct((l, SSD_GROUPS * SSD_STATE), F32),
                   jax.ShapeDtypeStruct(dt_t.shape, F32),
                   jax.ShapeDtypeStruct(a_b.shape, F32)],
        scratch_shapes=[pltpu.VMEM((hpg, p_dim, SSD_STATE), F32), pltpu.VMEM((8, n), F32),
                        pltpu.VMEM((8, n), F32), pltpu.VMEM((8, n), F32)],
        compiler_params=_cparams(2),
    )(xh, act, act, dt_t, a_b, hprev, dyh)


def _ssd_gate_fwd(y, act, p, d_vec, gn, *, name):
    l = y.shape[0]
    w = SSD_D_INNER
    tm = _tile(l, 256)

    def body(y_ref, xs_ref, z_ref, d_ref, g_ref, o_ref):
        for gi in range(SSD_GROUPS):
            sl = slice(gi * SSD_NORM_GROUP, (gi + 1) * SSD_NORM_GROUP)
            z = z_ref[:, sl]
            y2 = (y_ref[:, sl] + d_ref[:, sl] * xs_ref[:, sl]) * (z * _sigmoid(z))
            r = lax.rsqrt(jnp.mean(y2 * y2, axis=1, keepdims=True) + RMS_EPS)
            o_ref[:, sl] = (y2 * r * g_ref[:, sl]).astype(BF16)

    rows = pl.BlockSpec((tm, w), lambda i: (i, 0))
    vec = pl.BlockSpec((1, w), lambda i: (0, 0))
    return pl.pallas_call(
        body, name=name, grid=(l // tm,), in_specs=[rows, rows, rows, vec, vec], out_specs=rows,
        out_shape=jax.ShapeDtypeStruct((l, w), BF16), compiler_params=_cparams(1),
    )(y, act, p, d_vec, gn)


def _ssd_gate_bwd(dyn, y, act, p, d_vec, gn, *, name):
    l = y.shape[0]
    w = SSD_D_INNER
    tm = _tile(l, 256)

    def body(dyn_ref, y_ref, xs_ref, z_ref, d_ref, g_ref, dy_ref, dz_ref, dxs_ref, dd_ref, dg_ref):
        i = pl.program_id(0)
        for gi in range(SSD_GROUPS):
            sl = slice(gi * SSD_NORM_GROUP, (gi + 1) * SSD_NORM_GROUP)
            z, xs, dv = z_ref[:, sl], xs_ref[:, sl], d_ref[:, sl]
            s = _sigmoid(z)
            sz = z * s
            y1 = y_ref[:, sl] + dv * xs
            y2 = y1 * sz
            r = lax.rsqrt(jnp.mean(y2 * y2, axis=1, keepdims=True) + RMS_EPS)
            y2h = y2 * r
            dyn_v = dyn_ref[:, sl]
            d2h = dyn_v * g_ref[:, sl]
            dy2 = r * (d2h - y2h * jnp.mean(d2h * y2h, axis=1, keepdims=True))
            dy1 = dy2 * sz
            dy_ref[:, sl] = dy1
            dz_ref[:, sl] = dy2 * y1 * s * (1.0 + z * (1.0 - s))
            dxs_ref[:, sl] = dv * dy1
            _accumulate(dd_ref.at[:, sl], jnp.sum(dy1 * xs, axis=0, keepdims=True), i == 0)
            _accumulate(dg_ref.at[:, sl], jnp.sum(dyn_v * y2h, axis=0, keepdims=True), i == 0)

    rows = pl.BlockSpec((tm, w), lambda i: (i, 0))
    vec = pl.BlockSpec((1, w), lambda i: (0, 0))
    return pl.pallas_call(
        body, name=name, grid=(l // tm,), in_specs=[rows, rows, rows, rows, vec, vec],
        out_specs=[rows, rows, rows, vec, vec],
        out_shape=[jax.ShapeDtypeStruct((l, w), F32)] * 3 + [jax.ShapeDtypeStruct((1, w), F32)] * 2,
        compiler_params=_cparams(1),
    )(dyn, y, act, p, d_vec, gn)


def _heads_major(x):
    return x.reshape(x.shape[0], SSD_HEADS, SSD_HEAD_DIM).transpose(1, 0, 2)


def _ssd_layer_fwd(x, g, win, cw, cb, dt_bias, a_log, d_skip, gn, wout, tag):
    l = x.shape[0]
    h = _rmsnorm(x, g, name=f"{tag}_norm")
    p = _mm(h, win, tm=1024, tn=896, tk=1024, name=f"{tag}_in")
    act = _ssd_conv_fwd(p, cw, cb, name=f"{tag}_conv")
    bias = jnp.pad(dt_bias, (0, 128 - SSD_HEADS)).reshape(1, 128)
    dt = _ssd_dt_fwd(p, bias, name=f"{tag}_dt")
    xh = _heads_major(act[:, :SSD_D_INNER])
    dt_t = jnp.pad(dt[:, :SSD_HEADS].T.reshape(SSD_GROUPS, SSD_HPG, l), ((0, 0), (0, 8 - SSD_HPG), (0, 0)))
    a = -jnp.exp(a_log).reshape(SSD_GROUPS, SSD_HPG, 1)
    a_b = jnp.broadcast_to(jnp.pad(a, ((0, 0), (0, 8 - SSD_HPG), (0, 0))), (SSD_GROUPS, 8, 128))
    yh, hprev = _ssd_fwd(xh, act, dt_t, a_b, name=f"{tag}_scan")
    y = yh.transpose(1, 0, 2).reshape(l, SSD_D_INNER)
    d_vec = jnp.repeat(d_skip, SSD_HEAD_DIM).reshape(1, SSD_D_INNER)
    yn = _ssd_gate_fwd(y, act, p, d_vec, gn, name=f"{tag}_gate")
    xo = _mm(yn, wout, tm=1024, tn=1024, tk=2048, name=f"{tag}_out", extras=[(x, "tile")],
             epilogue=lambda acc, xt: (xt + acc,))
    return xo, (x, h, p, act, bias, xh, dt_t, a_b, hprev, y, d_vec, yn)


def _ssd_layer_bwd(dout, saved, g, win, cw, cb, gn, wout, tag):
    x, h, p, act, bias, xh, dt_t, a_b, hprev, y, d_vec, yn = saved
    l = x.shape[0]
    dyn = _mm(dout, wout, tb=True, tm=1024, tn=1024, tk=1024, name=f"{tag}_dyn")
    dwout = _mm(yn, dout, ta=True, tm=1024, tn=1024, tk=2048, name=f"{tag}_dwout", out_dtype=BF16)
    dy, dz, dxs_d, dd_vec, dgn = _ssd_gate_bwd(dyn, y, act, p, d_vec, gn, name=f"{tag}_gate_bwd")
    dxh, dbm, dcm, ddt_t, da_b = _ssd_bwd(xh, act, dt_t, a_b, hprev, _heads_major(dy), name=f"{tag}_scan_bwd")
    dxs = dxh.transpose(1, 0, 2).reshape(l, SSD_D_INNER) + dxs_d
    dact = jnp.concatenate([dxs, dbm, dcm], axis=1)
    dxbc, dcw8 = _ssd_conv_bwd(p, cw, cb, dact, name=f"{tag}_conv_bwd")
    ddt = jnp.pad(ddt_t[:, :SSD_HPG, :].reshape(SSD_HEADS, l).T, ((0, 0), (0, 128 - SSD_HEADS)))
    ddt_raw, dbias = _ssd_dt_bwd(p, bias, ddt, name=f"{tag}_dt_bwd")
    dp = jnp.concatenate([dz, dxbc, ddt_raw], axis=1)
    dwin = _mm(h, dp, ta=True, tm=1024, tn=896, tk=2048, name=f"{tag}_dwin", out_dtype=BF16)
    dx, dg = _mm(dp, win, tb=True, tm=256, tn=1024, tk=6272, name=f"{tag}_dx",
                 extras=[(x, "tile"), (g, "row"), (dout, "tile")], outs=[(F32, "tile"), (F32, "colsum")],
                 epilogue=_norm_bwd_epilogue)
    a_heads = a_b[:, :SSD_HPG, 0].reshape(SSD_HEADS)
    grads = dict(
        ssd_w_in=dwin[:, :SSD_IN_DIM], ssd_conv_w=dcw8[:4], ssd_conv_b=dcw8[4],
        ssd_dt_bias=dbias[0, :SSD_HEADS], ssd_a_log=da_b[:, :SSD_HPG, 0].reshape(SSD_HEADS) * a_heads,
        ssd_d=dd_vec.reshape(SSD_HEADS, SSD_HEAD_DIM).sum(axis=1), ssd_norm=dgn[0], ssd_w_out=dwout)
    return dx, dg, grads


def _local_step(x, tgt, w, gather_later=None, scatter_early=None):
    row = lambda v: v.reshape(1, -1)
    saved = []
    for i in range(DEPTH):
        kind, j = i % 3, i // 3
        x, s1 = _ffn_fwd(x, row(w["ffn1_norm"][i]), w["ffn1_w_gu"][i], w["ffn1_w_down"][i], f"l{i}f1")
        gm = row(w["mix_norm"][i])
        if kind == 0:
            hosted = gather_later[0] if (gather_later and i == 0) else None
            x, sm, carried = _sb_layer_fwd(x, gm, w["sb_w_qkv"][j], w["sb_w_o"][j], f"l{i}sb", hosted=hosted)
            if hosted:
                w = gather_later[1](w, carried)
        elif kind == 1:
            x, sm = _ssd_layer_fwd(x, gm, w["ssd_w_in"][j], w["ssd_conv_w"][j], row(w["ssd_conv_b"][j]),
                                   w["ssd_dt_bias"][j], w["ssd_a_log"][j], w["ssd_d"][j], row(w["ssd_norm"][j]),
                                   w["ssd_w_out"][j], f"l{i}ssd")
        else:
            x, sm = _sc_layer_fwd(x, gm, w["sc_w_in"][j], w["sc_conv_w"][j], w["sc_w_out"][j], f"l{i}sc")
        x, s2 = _ffn_fwd(x, row(w["ffn2_norm"][i]), w["ffn2_w_gu"][i], w["ffn2_w_down"][i], f"l{i}f2")
        saved.append((s1, sm, s2))

    loss, dx, dfinal = _final_loss(x, row(w["final_norm"]), tgt, name="final_loss")
    per_layer = {k: [None] * DEPTH for k in ("ffn1_norm", "ffn1_w_gu", "ffn1_w_down", "mix_norm",
                                             "ffn2_norm", "ffn2_w_gu", "ffn2_w_down")}
    per_layer.update({"sb_w_qkv": [None, None], "sb_w_o": [None, None]})
    grads = {"final_norm": dfinal[0]}
    early = None
    for i in reversed(range(DEPTH)):
        kind, j = i % 3, i // 3
        s1, sm, s2 = saved[i]
        dx, dg, dwgu, dwd = _ffn_bwd(dx, s2, row(w["ffn2_norm"][i]), w["ffn2_w_gu"][i], w["ffn2_w_down"][i], f"l{i}f2")
        per_layer["ffn2_norm"][i], per_layer["ffn2_w_gu"][i], per_layer["ffn2_w_down"][i] = dg[0], dwgu, dwd
        gm = row(w["mix_norm"][i])
        if kind == 0:
            hosted = scatter_early({**grads, **per_layer}) if (scatter_early and i == 0) else None
            dx, dg, dwqkv, dwo, carried = _sb_layer_bwd(dx, sm, gm, w["sb_w_qkv"][j], w["sb_w_o"][j], f"l{i}sb",
                                                        hosted=hosted)
            per_layer["sb_w_qkv"][j], per_layer["sb_w_o"][j] = dwqkv, dwo
            if hosted:
                early = carried
        elif kind == 1:
            dx, dg, sg = _ssd_layer_bwd(dx, sm, gm, w["ssd_w_in"][j], w["ssd_conv_w"][j], row(w["ssd_conv_b"][j]),
                                        row(w["ssd_norm"][j]), w["ssd_w_out"][j], f"l{i}ssd")
            sg["ssd_w_in"], sg["ssd_w_out"] = [sg["ssd_w_in"]], [sg["ssd_w_out"]]
            grads.update({k: (v if isinstance(v, list) else v[None]) for k, v in sg.items()})
        else:
            dx, dg, dwin, dcw, dwout = _sc_layer_bwd(dx, sm, gm, w["sc_w_in"][j], w["sc_conv_w"][j],
                                                     w["sc_w_out"][j], f"l{i}sc")
            grads.update(sc_w_in=[dwin], sc_conv_w=dcw[None], sc_w_out=[dwout])
        per_layer["mix_norm"][i] = dg[0]
        dx, dg, dwgu, dwd = _ffn_bwd(dx, s1, row(w["ffn1_norm"][i]), w["ffn1_w_gu"][i], w["ffn1_w_down"][i], f"l{i}f1")
        per_layer["ffn1_norm"][i], per_layer["ffn1_w_gu"][i], per_layer["ffn1_w_down"][i] = dg[0], dwgu, dwd
    for k, v in per_layer.items():
        grads[k] = jnp.stack(v) if k.endswith("_norm") else v
    return loss, dx, grads, early


_HBM = pl.BlockSpec(memory_space=pltpu.HBM)


def _remote(src, dst, send_sems, recv_sems, idx, dev):
    return pltpu.make_async_remote_copy(src_ref=src, dst_ref=dst, send_sem=send_sems.at[idx], recv_sem=recv_sems.at[idx],
                                        device_id=dev, device_id_type=pl.DeviceIdType.MESH)


def _exchange_call(body, xs, out_shapes, n_copies, name):
    n = len(xs)
    return pl.pallas_call(
        body, name=name, in_specs=[_HBM] * n, out_specs=[_HBM] * n,
        out_shape=[jax.ShapeDtypeStruct(s, x.dtype) for s, x in zip(out_shapes, xs)],
        scratch_shapes=[pltpu.SemaphoreType.DMA((n, n_copies)), pltpu.SemaphoreType.DMA((n, n_copies)),
                        pltpu.SemaphoreType.DMA((n,))],
    )(*xs)


def _gather(xs, *, name):
    n = len(xs)

    def body(*refs):
        start, finish = _gather_steps(refs[:n], refs[n:2 * n], *refs[2 * n:])
        start()
        finish()

    return _exchange_call(body, xs, _gather_shapes(xs), _GATHER_COPIES, name)


_GATHER_COPIES = 7


def _gather_shapes(xs):
    return [(N_DEV,) + x.shape for x in xs]


def _gather_steps(x_refs, o_refs, send_sems, recv_sems, local_sems):
    n = len(x_refs)

    def plan():
        mx, my, mc = lax.axis_index("x"), lax.axis_index("y"), lax.axis_index("c")
        slot = lambda px, py, pc: 4 * px + 2 * py + pc
        me, sibling = (mx, my, mc), (mx, my, 1 - mc)
        chips = [(1 - mx, my), (mx, 1 - my), (1 - mx, 1 - my)]
        locals_, first = [], []
        for a in range(n):
            x_ref, o_ref = x_refs[a], o_refs[a]
            locals_.append(pltpu.make_async_copy(x_ref, o_ref.at[slot(*me)], local_sems.at[a]))
            first.append(_remote(x_ref, o_ref.at[slot(*me)], send_sems, recv_sems, (a, 0), sibling))
            for j, chip in enumerate(chips):
                first.append(_remote(x_ref, o_ref.at[slot(*me)], send_sems, recv_sems, (a, 1 + j), (*chip, mc)))
        return locals_, first, slot, me, sibling, chips, mc

    def start():
        locals_, first = plan()[:2]
        for cp in locals_ + first:
            cp.start()

    def finish():
        locals_, first, slot, me, sibling, chips, mc = plan()
        passed = []
        for j, chip in enumerate(chips):
            for a in range(n):
                landed = o_refs[a].at[slot(*chip, mc)]
                _remote(landed, landed, send_sems, recv_sems, (a, 1 + j), me).wait_recv()
                fwd = _remote(landed, landed, send_sems, recv_sems, (a, 4 + j), sibling)
                fwd.start()
                passed.append(fwd)
        for a in range(n):
            from_sib = o_refs[a].at[slot(*sibling)]
            _remote(from_sib, from_sib, send_sems, recv_sems, (a, 0), me).wait_recv()
            for j, chip in enumerate(chips):
                via_sib = o_refs[a].at[slot(*chip, 1 - mc)]
                _remote(via_sib, via_sib, send_sems, recv_sems, (a, 4 + j), me).wait_recv()
        for cp in first + passed:
            cp.wait_send()
        for cp in locals_:
            cp.wait()

    return start, finish


def _scatter_sibling(xs, *, name):
    n = len(xs)

    def body(*refs):
        x_refs, o_refs = refs[:n], refs[n:2 * n]
        send_sems, recv_sems, _ = refs[2 * n:]
        mx, my, mc = lax.axis_index("x"), lax.axis_index("y"), lax.axis_index("c")
        sibling = (mx, my, 1 - mc)
        sends = []
        for a in range(n):
            for ch in range(4):
                sends.append(_remote(x_refs[a].at[ch, 1 - mc], o_refs[a].at[ch], send_sems, recv_sems, (a, ch), sibling))
        for cp in sends:
            cp.start()
        for cp in sends:
            cp.wait_recv()
        for cp in sends:
            cp.wait_send()

    return _exchange_call(body, xs, [(4,) + x.shape[2:] for x in xs], 4, name)


def _scatter_chips(ys, *, name):
    n = len(ys)

    def body(*refs):
        start, finish = _chip_scatter_steps(refs[:n], refs[n:2 * n], *refs[2 * n:])
        start()
        finish()

    return _exchange_call(body, ys, _chip_scatter_shapes(ys), _CHIP_SCATTER_COPIES, name)


_CHIP_SCATTER_COPIES = 3


def _chip_scatter_shapes(ys):
    return [y.shape for y in ys]


def _chip_scatter_steps(y_refs, o_refs, send_sems, recv_sems, local_sems):
    n = len(y_refs)

    def plan():
        mx, my, mc = lax.axis_index("x"), lax.axis_index("y"), lax.axis_index("c")
        mine = 2 * mx + my
        chips = [(1 - mx, my), (mx, 1 - my), (1 - mx, 1 - my)]
        locals_, sends, recvs = [], [], []
        for a in range(n):
            locals_.append(pltpu.make_async_copy(y_refs[a].at[mine], o_refs[a].at[mine], local_sems.at[a]))
            for j, (px, py) in enumerate(chips):
                theirs = 2 * px + py
                sends.append(_remote(y_refs[a].at[theirs], o_refs[a].at[mine], send_sems, recv_sems, (a, j), (px, py, mc)))
                recvs.append(_remote(y_refs[a].at[theirs], o_refs[a].at[theirs], send_sems, recv_sems, (a, j), (px, py, mc)))
        return locals_, sends, recvs

    def start():
        locals_, sends, _ = plan()
        for cp in locals_ + sends:
            cp.start()

    def finish():
        locals_, sends, recvs = plan()
        for cp in recvs:
            cp.wait_recv()
        for cp in sends:
            cp.wait_send()
        for cp in locals_:
            cp.wait()

    return start, finish


def _pair_add(x, r, *, name):
    _, _, rows, c = x.shape
    tr = _tile(rows, 512, 16)

    def body(core_ref, x_ref, r_ref, o_ref):
        o_ref[...] = (x_ref[...].astype(F32) + r_ref[...].astype(F32)).astype(o_ref.dtype)

    core = lax.axis_index("c").astype(jnp.int32).reshape(1)
    return pl.pallas_call(
        body, name=name,
        grid_spec=pltpu.PrefetchScalarGridSpec(
            num_scalar_prefetch=1, grid=(4, rows // tr),
            in_specs=[pl.BlockSpec((None, None, tr, c), lambda ch, i, core: (ch, core[0], i, 0)),
                      pl.BlockSpec((None, tr, c), lambda ch, i, core: (ch, i, 0))],
            out_specs=pl.BlockSpec((None, tr, c), lambda ch, i, core: (ch, i, 0))),
        out_shape=jax.ShapeDtypeStruct((4, rows, c), x.dtype), compiler_params=_cparams(2),
    )(core, x, r)


def _adamw_reduce(parts, w, m, v, *, name):
    r, c = w.shape
    n_parts = parts.shape[0]
    tr = _tile(r, 256, 16)
    bc1 = 1.0 - ADAM_B1 ** ADAM_STEP
    bc2 = 1.0 - ADAM_B2 ** ADAM_STEP

    def body(p_ref, w_ref, m_ref, v_ref, g_ref, d_ref, nm_ref, nv_ref):
        g = p_ref[0].astype(F32)
        for q in range(1, n_parts):
            g = g + p_ref[q].astype(F32)
        nm = ADAM_B1 * m_ref[...] + (1.0 - ADAM_B1) * g
        nv = ADAM_B2 * v_ref[...] + (1.0 - ADAM_B2) * (g * g)
        g_ref[...] = g
        nm_ref[...] = nm
        nv_ref[...] = nv
        d_ref[...] = -ADAM_LR * ((nm / bc1) / (jnp.sqrt(nv / bc2) + ADAM_EPS) + ADAM_WD * w_ref[...])

    blk = pl.BlockSpec((tr, c), lambda i: (i, 0))
    return pl.pallas_call(
        body, name=name, grid=(r // tr,),
        in_specs=[pl.BlockSpec((n_parts, tr, c), lambda i: (0, i, 0)), blk, blk, blk], out_specs=[blk] * 4,
        out_shape=[jax.ShapeDtypeStruct((r, c), F32)] * 4, compiler_params=_cparams(1),
    )(parts, w, m, v)


def _col_full(g):
    return g.transpose(1, 2, 0, 3).reshape(g.shape[1], g.shape[2], -1)


def _col_parts(f):
    n, k, c8 = f.shape
    return f.reshape(n, k, N_DEV, c8 // N_DEV).transpose(2, 0, 1, 3)


def _row_full(g):
    return g.transpose(1, 0, 2, 3).reshape(g.shape[1], -1, g.shape[3])


def _row_parts(f):
    n, r8, c = f.shape
    return f.reshape(n, N_DEV, r8 // N_DEV, c).transpose(1, 0, 2, 3)


def _gu_full(g):
    n, d, c = g.shape[1:]
    return g.reshape(2, 4, n, d, c).transpose(2, 0, 3, 1, 4).reshape(n, 2, d, 4 * c)


def _gu_parts(f):
    n, _, d, c4 = f.shape
    return f.reshape(n, 2, d, 4, c4 // 4).transpose(1, 3, 0, 2, 4).reshape(N_DEV, n, d, c4 // 4)


def _ssd_in_full(g):
    return jnp.pad(_col_full(g), ((0, 0), (0, 0), (0, SSD_IN_PAD - SSD_IN_DIM)))


_MATMUL_WEIGHTS = (
    ("ffn1_w_gu", _gu_full, _gu_parts), ("ffn1_w_down", _row_full, _row_parts),
    ("ffn2_w_gu", _gu_full, _gu_parts), ("ffn2_w_down", _row_full, _row_parts),
    ("sb_w_qkv", _col_full, _col_parts), ("sb_w_o", _row_full, _row_parts),
    ("ssd_w_in", _ssd_in_full, _col_parts), ("ssd_w_out", _row_full, _row_parts),
    ("sc_w_in", _col_full, _col_parts), ("sc_w_out", _row_full, _row_parts),
)
_FIRST_WEIGHTS = ("ffn1_w_gu", "ffn1_w_down", "sb_w_qkv", "sb_w_o")
_CONV_WEIGHTS = ("ssd_conv_w", "sc_conv_w")
_REPLICATED = ("ffn1_norm", "mix_norm", "ffn2_norm", "final_norm", "ssd_conv_b", "ssd_norm",
               "ssd_dt_bias", "ssd_a_log", "ssd_d")
_ORDER = ("ffn1_norm", "ffn1_w_gu", "ffn1_w_down", "mix_norm", "ffn2_norm", "ffn2_w_gu", "ffn2_w_down",
          "sb_w_qkv", "sb_w_o", "ssd_w_in", "ssd_conv_w", "ssd_conv_b", "ssd_dt_bias", "ssd_a_log", "ssd_d",
          "ssd_norm", "ssd_w_out", "sc_w_in", "sc_conv_w", "sc_w_out", "final_norm")
_LANES = 1024


def _rows_of(a):
    flat = a.reshape(-1)
    pad = -flat.shape[0] % _LANES
    return jnp.pad(flat, (0, pad)).reshape(-1, _LANES)


def _pack_rows(arrays, mult):
    rows = [_rows_of(a) for a in arrays]
    packed = jnp.concatenate(rows, axis=0)
    pad = -packed.shape[0] % mult
    return jnp.pad(packed, ((0, pad), (0, 0))), [r.shape[0] for r in rows]


def _unpack_rows(packed, counts, shapes, lead=()):
    out, off = [], 0
    for n, shp in zip(counts, shapes):
        size = math.prod(shp)
        seg = packed[..., off:off + n, :].reshape(lead + (n * _LANES,))[..., :size]
        out.append(seg.reshape(lead + tuple(shp)))
        off += n
    return out


def kernel(x, ffn1_norm, ffn1_w_gu, ffn1_w_down, mix_norm, ffn2_norm, ffn2_w_gu, ffn2_w_down, sb_w_qkv, sb_w_o, ssd_w_in, ssd_conv_w, ssd_conv_b, ssd_dt_bias, ssd_a_log, ssd_d, ssd_norm, ssd_w_out, sc_w_in, sc_conv_w, sc_w_out, final_norm, loss_target, m_ffn1_norm, m_ffn1_w_gu, m_ffn1_w_down, m_mix_norm, m_ffn2_norm, m_ffn2_w_gu, m_ffn2_w_down, m_sb_w_qkv, m_sb_w_o, m_ssd_w_in, m_ssd_conv_w, m_ssd_conv_b, m_ssd_dt_bias, m_ssd_a_log, m_ssd_d, m_ssd_norm, m_ssd_w_out, m_sc_w_in, m_sc_conv_w, m_sc_w_out, m_final_norm, v_ffn1_norm, v_ffn1_w_gu, v_ffn1_w_down, v_mix_norm, v_ffn2_norm, v_ffn2_w_gu, v_ffn2_w_down, v_sb_w_qkv, v_sb_w_o, v_ssd_w_in, v_ssd_conv_w, v_ssd_conv_b, v_ssd_dt_bias, v_ssd_a_log, v_ssd_d, v_ssd_norm, v_ssd_w_out, v_sc_w_in, v_sc_conv_w, v_sc_w_out, v_final_norm):
    w = dict(ffn1_norm=ffn1_norm, ffn1_w_gu=ffn1_w_gu, ffn1_w_down=ffn1_w_down, mix_norm=mix_norm, ffn2_norm=ffn2_norm, ffn2_w_gu=ffn2_w_gu, ffn2_w_down=ffn2_w_down, sb_w_qkv=sb_w_qkv, sb_w_o=sb_w_o, ssd_w_in=ssd_w_in, ssd_conv_w=ssd_conv_w, ssd_conv_b=ssd_conv_b, ssd_dt_bias=ssd_dt_bias, ssd_a_log=ssd_a_log, ssd_d=ssd_d, ssd_norm=ssd_norm, ssd_w_out=ssd_w_out, sc_w_in=sc_w_in, sc_conv_w=sc_conv_w, sc_w_out=sc_w_out, final_norm=final_norm)
    mom = dict(ffn1_norm=m_ffn1_norm, ffn1_w_gu=m_ffn1_w_gu, ffn1_w_down=m_ffn1_w_down, mix_norm=m_mix_norm, ffn2_norm=m_ffn2_norm, ffn2_w_gu=m_ffn2_w_gu, ffn2_w_down=m_ffn2_w_down, sb_w_qkv=m_sb_w_qkv, sb_w_o=m_sb_w_o, ssd_w_in=m_ssd_w_in, ssd_conv_w=m_ssd_conv_w, ssd_conv_b=m_ssd_conv_b, ssd_dt_bias=m_ssd_dt_bias, ssd_a_log=m_ssd_a_log, ssd_d=m_ssd_d, ssd_norm=m_ssd_norm, ssd_w_out=m_ssd_w_out, sc_w_in=m_sc_w_in, sc_conv_w=m_sc_conv_w, sc_w_out=m_sc_w_out, final_norm=m_final_norm)
    var = dict(ffn1_norm=v_ffn1_norm, ffn1_w_gu=v_ffn1_w_gu, ffn1_w_down=v_ffn1_w_down, mix_norm=v_mix_norm, ffn2_norm=v_ffn2_norm, ffn2_w_gu=v_ffn2_w_gu, ffn2_w_down=v_ffn2_w_down, sb_w_qkv=v_sb_w_qkv, sb_w_o=v_sb_w_o, ssd_w_in=v_ssd_w_in, ssd_conv_w=v_ssd_conv_w, ssd_conv_b=v_ssd_conv_b, ssd_dt_bias=v_ssd_dt_bias, ssd_a_log=v_ssd_a_log, ssd_d=v_ssd_d, ssd_norm=v_ssd_norm, ssd_w_out=v_ssd_w_out, sc_w_in=v_sc_w_in, sc_conv_w=v_sc_conv_w, sc_w_out=v_sc_w_out, final_norm=v_final_norm)
    me = 4 * lax.axis_index("x") + 2 * lax.axis_index("y") + lax.axis_index("c")
    big = [n for n, _, _ in _MATMUL_WEIGHTS]
    two_d = lambda a: a.reshape(-1, a.shape[-1])

    to_full = {n: f for n, f, _ in _MATMUL_WEIGHTS}
    to_parts = {n: f for n, _, f in _MATMUL_WEIGHTS}
    first = [(n, (0,)) for n in _FIRST_WEIGHTS]
    later = [(n, tuple(range(1 if n in _FIRST_WEIGHTS else 0, w[n].shape[0]))) for n in big]

    def shards(group):
        return [two_d(w[n][idx[0]:idx[-1] + 1].astype(BF16)) for n, idx in group]

    def layers(group, gathered):
        out = {}
        for (n, idx), g in zip(group, gathered):
            f = to_full[n](g.reshape((N_DEV, len(idx)) + w[n].shape[1:]))
            out[n] = [f[i] for i in range(len(idx))]
        return out

    gathered = _gather(shards(first) + [two_d(w[n]) for n in _CONV_WEIGHTS], name="gather_first")
    full = dict(w)
    full.update(layers(first, gathered))
    for n, g in zip(_CONV_WEIGHTS, gathered[len(first):]):
        full[n] = _col_full(g.reshape((N_DEV,) + w[n].shape))

    def with_later(wd, gathered_later):
        wd = dict(wd)
        for n, ls in layers(later, gathered_later).items():
            wd[n] = (wd[n] if n in _FIRST_WEIGHTS else []) + ls
        return wd

    later_shards = shards(later)
    gather_later = (_Hosted(_gather_steps, later_shards, _gather_shapes(later_shards), _GATHER_COPIES), with_later)

    def chip_sums(group, grads, tag):
        parts = []
        for n, idx in group:
            p8 = to_parts[n](jnp.stack([grads[n][i] for i in idx]).astype(BF16))
            parts.append(p8.reshape(4, 2, -1, p8.shape[-1]))
        from_sibling = _scatter_sibling(parts, name=f"scatter_sibling_{tag}")
        return [_pair_add(p, r, name=f"pair_add_{tag}_{n}") for (n, _), p, r in zip(group, parts, from_sibling)]

    def scatter_early(grads):
        ys = chip_sums(later, grads, "later")
        return _Hosted(_chip_scatter_steps, ys, _chip_scatter_shapes(ys), _CHIP_SCATTER_COPIES)

    loss_part, dx, grads, recv_later = _local_step(x[0], loss_target[0], full, gather_later, scatter_early)
    loss = lax.psum(loss_part[0, 0], ("x", "y", "c"))

    recv_first = _scatter_chips(chip_sums(first, grads, "first"), name="scatter_chips_first")
    contrib = {n: [r] for (n, _), r in zip(first, recv_first)}
    for (n, _), r in zip(later, recv_later):
        contrib.setdefault(n, []).append(r)
    out_g, out_d, out_m, out_v = {}, {}, {}, {}

    def update(n, parts):
        res = _adamw_reduce(parts, two_d(w[n]), two_d(mom[n]), two_d(var[n]), name=f"adamw_{n}")
        out_g[n], out_d[n], out_m[n], out_v[n] = (r.reshape(w[n].shape) for r in res)

    for n in big:
        update(n, contrib[n][0] if len(contrib[n]) == 1 else jnp.concatenate(contrib[n], axis=1))

    small = list(_REPLICATED) + list(_CONV_WEIGHTS)
    small_shapes = [grads[n].shape for n in small]
    spacked, scounts = _pack_rows([grads[n].astype(F32) for n in small], 8)
    sg = _unpack_rows(_gather([spacked], name="gather_small_grads")[0], scounts, small_shapes, (N_DEV,))
    sg = dict(zip(small, sg))
    rep_w, rcounts = _pack_rows([w[n] for n in _REPLICATED], 8)
    rep_m, _ = _pack_rows([mom[n] for n in _REPLICATED], 8)
    rep_v, _ = _pack_rows([var[n] for n in _REPLICATED], 8)
    rep_p = jnp.concatenate([_rows_of(sg[n].reshape(N_DEV, -1)[q]) for q in range(N_DEV) for n in _REPLICATED], axis=0)
    rep_p = rep_p.reshape(N_DEV, -1, _LANES)
    rep_p = jnp.pad(rep_p, ((0, 0), (0, rep_w.shape[0] - rep_p.shape[1]), (0, 0)))
    res = _adamw_reduce(rep_p, rep_w, rep_m, rep_v, name="adamw_replicated")
    rep_shapes = [w[n].shape for n in _REPLICATED]
    for tgt, r in zip((out_g, out_d, out_m, out_v), res):
        for n, a in zip(_REPLICATED, _unpack_rows(r, rcounts, rep_shapes)):
            tgt[n] = a
    for n in _CONV_WEIGHTS:
        c = w[n].shape[-1]
        mine = lax.dynamic_slice_in_dim(sg[n], me * c, c, axis=sg[n].ndim - 1)
        update(n, mine.reshape(N_DEV, -1, c))

    return (loss, dx[None], *[out_g[n] for n in _ORDER], *[out_d[n] for n in _ORDER],
            *[out_m[n] for n in _ORDER], *[out_v[n] for n in _ORDER])
```

```python
import functools
import math

import jax
import jax.numpy as jnp
from jax import lax
from jax.experimental import pallas as pl
from jax.experimental.pallas import tpu as pltpu

F32 = jnp.float32
BF16 = jnp.bfloat16

D_MODEL = 1024
D_FF = 2816
DEPTH = 4
N_DEV = 8
SB_HEADS = 16
SB_HEAD_DIM = 64
SB_TILE = 256
SB_HEADS_PER_STEP = 2
SB_FWD_GROUPS = (2, 1)
SB_BWD_GROUPS = (4, 2, 1)
SSD_HEADS = 32
SSD_HEAD_DIM = 64
SSD_GROUPS = 8
SSD_HPG = 4
SSD_STATE = 128
SSD_CHUNK = 128
SSD_GROUPS_PER_STEP = 2
SSD_D_INNER = 2048
SSD_CONV_DIM = 4096
SSD_IN_DIM = 6176
SSD_IN_PAD = 6272
SSD_NORM_GROUP = 256
RMS_EPS = 1e-6
ADAM_LR = 0.001
ADAM_B1 = 0.9
ADAM_B2 = 0.999
ADAM_EPS = 1e-08
ADAM_WD = 0.01
ADAM_STEP = 10
VMEM_LIMIT = 60 * 1024 * 1024

NT = (((1,), (1,)), ((), ()))
TN = (((0,), (0,)), ((), ()))
NN = (((1,), (0,)), ((), ()))


def _cparams(n_axes):
    return pltpu.CompilerParams(dimension_semantics=("arbitrary",) * n_axes, vmem_limit_bytes=VMEM_LIMIT)


def _tile(n, want, mult=8):
    if n <= want:
        return n
    for t in range(want, 0, -1):
        if n % t == 0 and t % mult == 0:
            return t
    return n


def _sigmoid(x):
    return 1.0 / (1.0 + jnp.exp(-x))


def _dot(a, b, dn=NN):
    return lax.dot_general(a, b, dn, preferred_element_type=F32)


def _split3(x):
    x1 = x.astype(BF16)
    r1 = x - x1.astype(F32)
    x2 = r1.astype(BF16)
    x3 = (r1 - x2.astype(F32)).astype(BF16)
    return x1, x2, x3


def _dot_exact(x, t):
    x1, x2, x3 = _split3(x)
    return _dot(x1, t) + _dot(x2, t) + _dot(x3, t)


ROW_SUM_LANES = 1


def _cumsum_operand(tri):
    return jnp.concatenate([tri, tri], axis=0)


def _cumsum_rowsum(x, tri2):
    x1 = x.astype(BF16)
    x2 = (x - x1.astype(F32)).astype(BF16)
    return _dot(jnp.concatenate([x1, x2], axis=1), tri2), jnp.sum(x, axis=1, keepdims=True)


def _across_lanes(c, t):
    return c


def _mm(a, b, *, name, ta=False, tb=False, sa=False, sb=False, so=False, tm=512, tn=1024, tk=1024,
        out_dtype=F32, epilogue=None, extras=(), outs=None, pair=None):
    ash, bsh = a.shape[-2:], b.shape[-2:]
    m, k = (ash[1], ash[0]) if ta else ash
    n = bsh[0] if tb else bsh[1]
    s_n = pair or (a.shape[0] if sa else (b.shape[0] if sb else 1))
    tm, tn, tk = _tile(m, tm), _tile(n, tn, 128), _tile(k, tk, 128)
    nk = k // tk
    if outs is None:
        outs = [(out_dtype, "stile" if so else "tile")]
    if epilogue is None:
        epilogue = lambda acc: (acc,)

    if ta:
        a_blk, a_idx = (tk, tm), (lambda j, i, kk: (kk, i))
    else:
        a_blk, a_idx = (tm, tk), (lambda j, i, kk: (i, kk))
    if tb:
        b_blk, b_idx = (tn, tk), (lambda j, i, kk: (j, kk))
    else:
        b_blk, b_idx = (tk, tn), (lambda j, i, kk: (kk, j))

    def lead(blk, idx, has_s):
        if not has_s:
            return pl.BlockSpec(blk, idx)
        return pl.BlockSpec((s_n,) + blk, lambda j, i, kk: (0,) + idx(j, i, kk))

    kinds = {
        "tile": lambda: pl.BlockSpec((tm, tn), lambda j, i, kk: (i, j)),
        "stile": lambda: pl.BlockSpec((s_n, tm, tn), lambda j, i, kk: (0, i, j)),
        "row": lambda: pl.BlockSpec((1, tn), lambda j, i, kk: (0, j)),
        "colsum": lambda: pl.BlockSpec((1, tn), lambda j, i, kk: (0, j)),
    }
    shapes = {"tile": (m, n), "stile": (s_n, m, n), "colsum": (1, n)}
    in_specs = [lead(a_blk, a_idx, sa), lead(b_blk, b_idx, sb)] + [kinds[kd]() for _, kd in extras]
    out_specs = [kinds[kd]() for _, kd in outs]
    out_shape = [jax.ShapeDtypeStruct(shapes[kd], dt) for dt, kd in outs]
    n_ex, n_out = len(extras), len(outs)
    dn = ((((0,) if ta else (1,)), ((1,) if tb else (0,))), ((), ()))
    acc_shape = (s_n, tm, tn) if so else (tm, tn)

    def body(*refs):
        a_ref, b_ref = refs[0], refs[1]
        ex_refs = refs[2:2 + n_ex]
        o_refs = refs[2 + n_ex:2 + n_ex + n_out]
        i = pl.program_id(1)
        kk = pl.program_id(2)

        def products():
            for s in range(s_n if (sa or sb) else 1):
                av = (a_ref[s] if sa else a_ref[...]).astype(BF16)
                bv = (b_ref[s] if sb else b_ref[...]).astype(BF16)
                yield s, lax.dot_general(av, bv, dn, preferred_element_type=F32)

        def finish(accv):
            vals = epilogue(accv, *[r[...] for r in ex_refs])
            for (dt, kd), o_ref, val in zip(outs, o_refs, vals):
                if kd == "colsum":
                    _accumulate(o_ref, val, i == 0)
                elif kd == "stile":
                    for s in range(s_n):
                        o_ref[s] = val[s].astype(dt)
                else:
                    o_ref[...] = val.astype(dt)

        if nk == 1:
            ds = [d for _, d in products()]
            finish(tuple(ds) if so else functools.reduce(jnp.add, ds))
            return

        acc = refs[-1]

        @pl.when(kk == 0)
        def _():
            acc[...] = jnp.zeros_like(acc)

        for s, d in products():
            if so:
                acc[s] += d
            else:
                acc[...] += d

        @pl.when(kk == nk - 1)
        def _():
            finish(tuple(acc[s] for s in range(s_n)) if so else acc[...])

    res = pl.pallas_call(
        body, name=name, grid=(n // tn, m // tm, nk),
        in_specs=in_specs, out_specs=out_specs, out_shape=out_shape,
        scratch_shapes=[pltpu.VMEM(acc_shape, F32)] if nk > 1 else [], compiler_params=_cparams(3),
    )(a, b, *[e for e, _ in extras])
    return res[0] if len(res) == 1 else res


def _accumulate(o_ref, val, first):
    @pl.when(first)
    def _():
        o_ref[...] = val

    @pl.when(jnp.logical_not(first))
    def _():
        o_ref[...] += val


def _rmsnorm(x, g, *, name):
    l, d = x.shape
    tm = _tile(l, 512)

    def body(x_ref, g_ref, o_ref):
        xv = x_ref[...]
        r = lax.rsqrt(jnp.mean(xv * xv, axis=1, keepdims=True) + RMS_EPS)
        o_ref[...] = (xv * r * g_ref[...]).astype(BF16)

    return pl.pallas_call(
        body, name=name, grid=(l // tm,),
        in_specs=[pl.BlockSpec((tm, d), lambda i: (i, 0)), pl.BlockSpec((1, d), lambda i: (0, 0))],
        out_specs=pl.BlockSpec((tm, d), lambda i: (i, 0)),
        out_shape=jax.ShapeDtypeStruct((l, d), BF16), compiler_params=_cparams(1),
    )(x, g)


def _norm_bwd_epilogue(dh, x, g, dres):
    r = lax.rsqrt(jnp.mean(x * x, axis=1, keepdims=True) + RMS_EPS)
    xh = x * r
    dg = jnp.sum(dh * xh, axis=0, keepdims=True)
    dxh = dh * g
    dx = r * (dxh - xh * jnp.mean(dxh * xh, axis=1, keepdims=True))
    return dres + dx, dg


def _final_loss(x, g, tgt, *, name):
    l, d = x.shape
    tm = _tile(l, 512)

    def body(x_ref, g_ref, t_ref, loss_ref, dx_ref, dg_ref):
        i = pl.program_id(0)
        xv, gv = x_ref[...], g_ref[...]
        r = lax.rsqrt(jnp.mean(xv * xv, axis=1, keepdims=True) + RMS_EPS)
        xh = xv * r
        e = xh * gv - t_ref[...]
        part = 0.5 * jnp.sum(jnp.mean(e * e, axis=1, keepdims=True), axis=0, keepdims=True)
        dy = e * (1.0 / d)
        dg = jnp.sum(dy * xh, axis=0, keepdims=True)
        dxh = dy * gv
        dx_ref[...] = r * (dxh - xh * jnp.mean(dxh * xh, axis=1, keepdims=True))
        _accumulate(dg_ref, dg, i == 0)
        _accumulate(loss_ref, jnp.broadcast_to(part, (1, 128)), i == 0)

    return pl.pallas_call(
        body, name=name, grid=(l // tm,),
        in_specs=[pl.BlockSpec((tm, d), lambda i: (i, 0)), pl.BlockSpec((1, d), lambda i: (0, 0)),
                  pl.BlockSpec((tm, d), lambda i: (i, 0))],
        out_specs=[pl.BlockSpec((1, 128), lambda i: (0, 0)), pl.BlockSpec((tm, d), lambda i: (i, 0)),
                   pl.BlockSpec((1, d), lambda i: (0, 0))],
        out_shape=[jax.ShapeDtypeStruct((1, 128), F32), jax.ShapeDtypeStruct((l, d), F32),
                   jax.ShapeDtypeStruct((1, d), F32)],
        compiler_params=_cparams(1),
    )(x, g, tgt)


def _ffn_fwd(x, g, wgu, wd, tag):
    h = _rmsnorm(x, g, name=f"{tag}_norm")

    def act(acc):
        gate, up = acc
        return acc, gate * _sigmoid(gate) * up

    gu, a = _mm(h, wgu, sb=True, so=True, tm=512, tn=1408, tk=1024, name=f"{tag}_up",
                outs=[(BF16, "stile"), (BF16, "tile")], epilogue=act)
    xo = _mm(a, wd, tm=512, tn=1024, tk=2816, name=f"{tag}_down", extras=[(x, "tile")],
             epilogue=lambda acc, xt: (xt + 0.5 * acc,))
    return xo, (x, h, gu, a)


def _ffn_bwd(dout, saved, g, wgu, wd, tag):
    x, h, gu, a = saved

    def act_bwd(acc, guv):
        da = 0.5 * acc
        gate, up = guv[0].astype(F32), guv[1].astype(F32)
        s = _sigmoid(gate)
        return ((da * up * s * (1.0 + gate * (1.0 - s)), da * gate * s),)

    dgu = _mm(dout, wd, tb=True, pair=2, tm=512, tn=1408, tk=1024, name=f"{tag}_dact", extras=[(gu, "stile")],
              outs=[(BF16, "stile")], epilogue=act_bwd)
    dwd = _mm(a, dout, ta=True, tm=1408, tn=1024, tk=2048, name=f"{tag}_dwd", out_dtype=BF16,
              epilogue=lambda acc: (0.5 * acc,))
    dwgu = _mm(h, dgu, ta=True, sb=True, so=True, tm=512, tn=1408, tk=2048, name=f"{tag}_dwgu", out_dtype=BF16)
    dx, dg = _mm(dgu, wgu, tb=True, sa=True, sb=True, tm=512, tn=1024, tk=2816, name=f"{tag}_dx",
                 extras=[(x, "tile"), (g, "row"), (dout, "tile")], outs=[(F32, "tile"), (F32, "colsum")],
                 epilogue=_norm_bwd_epilogue)
    return dx, dg, dwgu, dwd


def _sb_plan(n, sizes):
    digits = [n // sizes[0]] + [(n // s) % 2 for s in sizes[1:]]
    plan, none_smaller = [], 1
    for size, d in reversed(list(zip(sizes, digits))):
        has = jnp.minimum(d, 1)
        with_diag = none_smaller * has
        plan.append((size, True, with_diag))
        if size > 1:
            plan.append((size, False, d - with_diag))
        none_smaller = none_smaller * (1 - has)
    return plan


def _sb_sweep(plan, start, step, fn, carry):
    pos = start
    for size, with_diag, trips in plan:
        diag = 0 if step < 0 else size - 1

        def trip(it, cr, size=size, with_diag=with_diag, pos=pos, diag=diag):
            base = pos + step * size * it
            return fn([base + step * b for b in range(size)], cr, [with_diag and b == diag for b in range(size)])

        carry = lax.fori_loop(0, trips, trip, carry)
        pos = pos + step * size * trips
    return carry


def _sb_logs(z):
    lb = jnp.minimum(z, 0.0) - jnp.log(1.0 + jnp.exp(-jnp.abs(z)))
    return lb, lb - z


class _Hosted:
    def __init__(self, steps, xs, out_shapes, copies):
        self.steps, self.xs, self.n, self.copies = steps, list(xs), len(xs), copies
        self.out_shape = [jax.ShapeDtypeStruct(s, x.dtype) for s, x in zip(out_shapes, xs)]
        self.specs = [_HBM] * self.n
        self.sems = [pltpu.SemaphoreType.DMA((self.n, copies)), pltpu.SemaphoreType.DMA((self.n, copies)),
                     pltpu.SemaphoreType.DMA((self.n,))]

    def run(self, x_refs, o_refs, sems, grid):
        ids = [pl.program_id(a) for a in range(len(grid))]
        first = functools.reduce(jnp.logical_and, [p == 0 for p in ids])
        last = functools.reduce(jnp.logical_and, [p == g - 1 for p, g in zip(ids, grid)])
        start, finish = self.steps(x_refs, o_refs, *sems)
        pl.when(first)(start)
        return lambda: pl.when(last)(finish)


def _head_masks(hs):
    lane = lax.broadcasted_iota(jnp.int32, (1, hs * SB_HEAD_DIM), 1)
    return [jnp.logical_and(lane >= hh * SB_HEAD_DIM, lane < (hh + 1) * SB_HEAD_DIM) for hh in range(hs)]


def _sb_fwd(qkv, *, name, hosted=None):
    l = qkv.shape[0]
    d_model = qkv.shape[1] // 3
    dh = SB_HEAD_DIM
    t = _tile(l, SB_TILE)
    hs = 2 * SB_HEADS_PER_STEP
    w = hs * dh
    n_grp = d_model // w
    scale = dh ** -0.5
    grid = (n_grp, l // t)
    nh = hosted.n if hosted else 0

    def body(q_ref, k_ref, v_ref, *rest):
        o_ref = rest[nh]
        at_end = hosted.run(rest[:nh], rest[nh + 1:2 * nh + 1], rest[2 * nh + 1:], grid) if hosted else None
        i = pl.program_id(1)
        heads = _head_masks(hs)
        q_all = (q_ref[...].astype(F32) * scale).astype(BF16)
        qs = [jnp.where(heads[hh], q_all, jnp.zeros_like(q_all)) for hh in range(hs)]
        row = lax.broadcasted_iota(jnp.int32, (t, t), 0)
        col = lax.broadcasted_iota(jnp.int32, (t, t), 1)
        strict = col < row
        tri = _cumsum_operand(strict.astype(BF16))

        def block(jbs, carry, masks):
            sls = [pl.ds(pl.multiple_of(jb * t, t), t) for jb in jbs]
            chains = [(hh, b) for b in range(len(jbs)) for hh in range(hs)]
            ks = [k_ref[sl, :] for sl in sls]
            zs = {(hh, b): _dot(qs[hh], ks[b], NT) for hh, b in chains}
            lbs, tails, sums = {}, {}, {}
            for hh, b in chains:
                lb, lk = _sb_logs(zs[hh, b])
                if masks[b]:
                    lk = jnp.where(strict, lk, 0.0)
                lbs[hh, b] = lb
                tails[hh, b], sums[hh, b] = _cumsum_rowsum(lk, tri)
            cs, o = list(carry[0]), carry[1]
            for b in range(len(jbs)):
                atts = []
                for hh in range(hs):
                    att = jnp.exp(lbs[hh, b] + tails[hh, b] + _across_lanes(cs[hh], t))
                    if masks[b]:
                        att = jnp.where(strict, att, 0.0)
                    atts.append(att.astype(BF16))
                    cs[hh] = cs[hh] + sums[hh, b]
                vb = v_ref[sls[b], :]
                v_heads = jnp.concatenate([jnp.where(heads[hh], vb, jnp.zeros_like(vb)) for hh in range(hs)], axis=0)
                o = o + _dot(jnp.concatenate(atts, axis=1), v_heads)
            return tuple(cs), o

        carry = (tuple(jnp.zeros((t, ROW_SUM_LANES), F32) for _ in range(hs)), jnp.zeros((t, w), F32))
        carry = _sb_sweep(_sb_plan(i + 1, SB_FWD_GROUPS), i, -1, block, carry)
        o_ref[...] = carry[1].astype(o_ref.dtype)
        if hosted:
            at_end()

    blocks = d_model // w
    res = pl.pallas_call(
        body, name=name, grid=grid,
        in_specs=[pl.BlockSpec((t, w), lambda g, i: (i, g)), pl.BlockSpec((l, w), lambda g, i: (0, blocks + g)),
                  pl.BlockSpec((l, w), lambda g, i: (0, 2 * blocks + g))] + (hosted.specs if hosted else []),
        out_specs=[pl.BlockSpec((t, w), lambda g, i: (i, g))] + (hosted.specs if hosted else []),
        out_shape=[jax.ShapeDtypeStruct((l, d_model), BF16)] + (hosted.out_shape if hosted else []),
        scratch_shapes=hosted.sems if hosted else [], compiler_params=_cparams(2),
    )(qkv, qkv, qkv, *(hosted.xs if hosted else []))
    return (res[0], res[1:]) if hosted else res[0]


def _sb_bwd(qkv, do, *, name, hosted=None):
    l = qkv.shape[0]
    d_model = qkv.shape[1] // 3
    dh = SB_HEAD_DIM
    t = _tile(l, SB_TILE)
    nq = l // t
    hs = SB_HEADS_PER_STEP
    w = hs * dh
    blocks = d_model // w
    scale = dh ** -0.5
    grid = (blocks, nq)
    nh = hosted.n if hosted else 0

    def body(q_ref, k_ref, v_ref, do_ref, *rest):
        dq_ref, dk_ref, dv_ref = rest[nh:nh + 3]
        e_scr, s_scr = rest[2 * nh + 3:2 * nh + 5]
        at_end = hosted.run(rest[:nh], rest[nh + 3:2 * nh + 3], rest[2 * nh + 5:], grid) if hosted else None
        i = pl.program_id(1)

        @pl.when(i == 0)
        def _():
            dk_ref[...] = jnp.zeros_like(dk_ref)
            dv_ref[...] = jnp.zeros_like(dv_ref)

        heads = _head_masks(hs)
        q_all = (q_ref[...].astype(F32) * scale).astype(BF16)
        do_all = do_ref[...]
        qs = [jnp.where(heads[hh], q_all, jnp.zeros_like(q_all)) for hh in range(hs)]
        dos = [jnp.where(heads[hh], do_all, jnp.zeros_like(do_all)) for hh in range(hs)]
        row = lax.broadcasted_iota(jnp.int32, (t, t), 0)
        col = lax.broadcasted_iota(jnp.int32, (t, t), 1)
        strict = col < row
        tri_suffix = _cumsum_operand(strict.astype(BF16))
        tri_prefix = _cumsum_operand((row < col).astype(BF16))

        def sweep1(jbs, cs, masks):
            sls = [pl.ds(pl.multiple_of(jb * t, t), t) for jb in jbs]
            chains = [(hh, b) for b in range(len(jbs)) for hh in range(hs)]
            zs = {(hh, b): _dot(qs[hh], k_ref[sls[b], :], NT) for hh, b in chains}
            datts = {(hh, b): _dot(dos[hh], v_ref[sls[b], :], NT) for hh, b in chains}
            lbs, tails, sums = {}, {}, {}
            for hh, b in chains:
                lb, lk = _sb_logs(zs[hh, b])
                if masks[b]:
                    lk = jnp.where(strict, lk, 0.0)
                lbs[hh, b] = lb
                tails[hh, b], sums[hh, b] = _cumsum_rowsum(lk, tri_suffix)
                s_scr[hh, jbs[b]] = jnp.exp(lb)
            cs = list(cs)
            for hh, b in chains:
                att = jnp.exp(lbs[hh, b] + tails[hh, b] + _across_lanes(cs[hh], t))
                if masks[b]:
                    att = jnp.where(strict, att, 0.0)
                e_scr[hh, jbs[b]] = att * datts[hh, b]
                dv_ref[sls[b], :] += _dot(att.astype(BF16), dos[hh], TN)
                cs[hh] = cs[hh] + sums[hh, b]
            return tuple(cs)

        plan = _sb_plan(i + 1, SB_BWD_GROUPS)
        _sb_sweep(plan, i, -1, sweep1, tuple(jnp.zeros((t, ROW_SUM_LANES), F32) for _ in range(hs)))

        def sweep2(jbs, carry, masks):
            sls = [pl.ds(pl.multiple_of(jb * t, t), t) for jb in jbs]
            chains = [(hh, b) for b in range(len(jbs)) for hh in range(hs)]
            des = {(hh, b): e_scr[hh, jbs[b]] for hh, b in chains}
            pres = {(hh, b): _cumsum_rowsum(des[hh, b], tri_prefix) for hh, b in chains}
            carry = [list(c) for c in carry]
            for hh, b in chains:
                p, dq = carry[hh]
                de, sg = des[hh, b], s_scr[hh, jbs[b]]
                dlk = _across_lanes(p, t) + pres[hh, b][0]
                if masks[b]:
                    dlk = jnp.where(strict, dlk, 0.0)
                dz = (de - sg * (de + dlk)).astype(BF16)
                dk_ref[sls[b], :] += _dot(dz, qs[hh], TN)
                carry[hh] = [p + pres[hh, b][1], dq + _dot(dz, k_ref[sls[b], :])]
            return tuple(tuple(c) for c in carry)

        carry = tuple((jnp.zeros((t, ROW_SUM_LANES), F32), jnp.zeros((t, w), F32)) for _ in range(hs))
        carry = _sb_sweep(plan[::-1], 0, 1, sweep2, carry)
        dq = jnp.zeros((t, w), F32)
        for hh in range(hs):
            dq = jnp.where(heads[hh], carry[hh][1], dq)
        dq_ref[...] = (dq * scale).astype(dq_ref.dtype)
        if hosted:
            at_end()

    qspec = pl.BlockSpec((t, w), lambda g, i: (i, g))
    cols = lambda off: pl.BlockSpec((l, w), lambda g, i: (0, off + g))
    res = pl.pallas_call(
        body, name=name, grid=grid,
        in_specs=[qspec, cols(blocks), cols(2 * blocks), qspec] + (hosted.specs if hosted else []),
        out_specs=[qspec, cols(0), cols(0)] + (hosted.specs if hosted else []),
        out_shape=[jax.ShapeDtypeStruct((l, d_model), BF16), jax.ShapeDtypeStruct((l, d_model), F32),
                   jax.ShapeDtypeStruct((l, d_model), F32)] + (hosted.out_shape if hosted else []),
        scratch_shapes=[pltpu.VMEM((hs, nq, t, t), F32), pltpu.VMEM((hs, nq, t, t), F32)]
        + (hosted.sems if hosted else []),
        compiler_params=_cparams(2),
    )(qkv, qkv, qkv, do, *(hosted.xs if hosted else []))
    return (res[0], res[1], res[2], res[3:]) if hosted else res


def _sb_layer_fwd(x, g, wqkv, wo, tag, hosted=None):
    h = _rmsnorm(x, g, name=f"{tag}_norm")
    qkv = _mm(h, wqkv, tm=1024, tn=1024, tk=1024, name=f"{tag}_qkv", out_dtype=BF16)
    o = _sb_fwd(qkv, name=f"{tag}_attn", hosted=hosted)
    carried = None
    if hosted:
        o, carried = o
    xo = _mm(o, wo, tm=1024, tn=1024, tk=1024, name=f"{tag}_out", extras=[(x, "tile")],
             epilogue=lambda acc, xt: (xt + acc,))
    return xo, (x, h, qkv, o), carried


def _sb_layer_bwd(dout, saved, g, wqkv, wo, tag, hosted=None):
    x, h, qkv, o = saved
    do = _mm(dout, wo, tb=True, tm=1024, tn=1024, tk=1024, name=f"{tag}_do", out_dtype=BF16)
    dwo = _mm(o, dout, ta=True, tm=1024, tn=1024, tk=2048, name=f"{tag}_dwo", out_dtype=BF16)
    res = _sb_bwd(qkv, do, name=f"{tag}_attn_bwd", hosted=hosted)
    dq, dk, dv = res[:3]
    carried = res[3] if hosted else None
    dqkv = jnp.concatenate([dq, dk.astype(BF16), dv.astype(BF16)], axis=1)
    dwqkv = _mm(h, dqkv, ta=True, tm=1024, tn=1024, tk=2048, name=f"{tag}_dwqkv", out_dtype=BF16)
    dx, dg = _mm(dqkv, wqkv, tb=True, tm=512, tn=1024, tk=3072, name=f"{tag}_dx",
                 extras=[(x, "tile"), (g, "row"), (dout, "tile")], outs=[(F32, "tile"), (F32, "colsum")],
                 epilogue=_norm_bwd_epilogue)
    return dx, dg, dwqkv, dwo, carried


def _shift_down(x, s, t_idx):
    return jnp.where(t_idx >= s, pltpu.roll(x, s, 0), 0.0)


def _shift_up(x, s, t_idx):
    n = x.shape[0]
    return jnp.where(t_idx < n - s, pltpu.roll(x, n - s, 0), 0.0)


def _sc_fwd(p, cw, *, name):
    l = p.shape[0]
    d = cw.shape[1]
    tc = 128
    nb = d // tc

    def body(b_ref, c_ref, h_ref, w_ref, o_ref):
        v = c_ref[...] * h_ref[...]
        t_idx = lax.broadcasted_iota(jnp.int32, v.shape, 0)
        u = v * w_ref[2:3, :] + _shift_down(v, 1, t_idx) * w_ref[1:2, :] + _shift_down(v, 2, t_idx) * w_ref[0:1, :]
        o_ref[...] = (b_ref[...] * u).astype(BF16)

    return pl.pallas_call(
        body, name=name, grid=(nb,),
        in_specs=[pl.BlockSpec((l, tc), lambda j: (0, j)), pl.BlockSpec((l, tc), lambda j: (0, nb + j)),
                  pl.BlockSpec((l, tc), lambda j: (0, 2 * nb + j)), pl.BlockSpec((3, tc), lambda j: (0, j))],
        out_specs=pl.BlockSpec((l, tc), lambda j: (0, j)),
        out_shape=jax.ShapeDtypeStruct((l, d), BF16), compiler_params=_cparams(1),
    )(p, p, p, cw)


def _sc_bwd(p, cw, dbu, *, name):
    l = p.shape[0]
    d = cw.shape[1]
    tc = 128
    nb = d // tc

    def body(b_ref, c_ref, h_ref, w_ref, g_ref, db_ref, dc_ref, dh_ref, dw_ref):
        cv, hv = c_ref[...], h_ref[...]
        v = cv * hv
        t_idx = lax.broadcasted_iota(jnp.int32, v.shape, 0)
        v1, v2 = _shift_down(v, 1, t_idx), _shift_down(v, 2, t_idx)
        u = v * w_ref[2:3, :] + v1 * w_ref[1:2, :] + v2 * w_ref[0:1, :]
        dbu_v = g_ref[...]
        db_ref[...] = (dbu_v * u).astype(BF16)
        du = dbu_v * b_ref[...]
        dv = du * w_ref[2:3, :] + _shift_up(du, 1, t_idx) * w_ref[1:2, :] + _shift_up(du, 2, t_idx) * w_ref[0:1, :]
        dc_ref[...] = (dv * hv).astype(BF16)
        dh_ref[...] = (dv * cv).astype(BF16)
        dw_ref[...] = jnp.zeros_like(dw_ref)
        dw_ref[0:1, :] = jnp.sum(du * v2, axis=0, keepdims=True)
        dw_ref[1:2, :] = jnp.sum(du * v1, axis=0, keepdims=True)
        dw_ref[2:3, :] = jnp.sum(du * v, axis=0, keepdims=True)

    col = lambda off: pl.BlockSpec((l, tc), lambda j: (0, off + j))
    return pl.pallas_call(
        body, name=name, grid=(nb,),
        in_specs=[col(0), col(nb), col(2 * nb), pl.BlockSpec((3, tc), lambda j: (0, j)), col(0)],
        out_specs=[col(0), col(0), col(0), pl.BlockSpec((8, tc), lambda j: (0, j))],
        out_shape=[jax.ShapeDtypeStruct((l, d), BF16)] * 3 + [jax.ShapeDtypeStruct((8, d), F32)],
        compiler_params=_cparams(1),
    )(p, p, p, cw, dbu)


def _sc_layer_fwd(x, g, win, cw, wout, tag):
    h = _rmsnorm(x, g, name=f"{tag}_norm")
    p = _mm(h, win, tm=1024, tn=1024, tk=1024, name=f"{tag}_in")
    bu = _sc_fwd(p, cw, name=f"{tag}_conv")
    xo = _mm(bu, wout, tm=1024, tn=1024, tk=1024, name=f"{tag}_out", extras=[(x, "tile")],
             epilogue=lambda acc, xt: (xt + acc,))
    return xo, (x, h, p, bu)


def _sc_layer_bwd(dout, saved, g, win, cw, wout, tag):
    x, h, p, bu = saved
    dbu = _mm(dout, wout, tb=True, tm=1024, tn=1024, tk=1024, name=f"{tag}_dbu")
    dwout = _mm(bu, dout, ta=True, tm=1024, tn=1024, tk=2048, name=f"{tag}_dwout", out_dtype=BF16)
    db, dc, dh, dcw = _sc_bwd(p, cw, dbu, name=f"{tag}_conv_bwd")
    dp = jnp.concatenate([db, dc, dh], axis=1)
    dwin = _mm(h, dp, ta=True, tm=1024, tn=1024, tk=2048, name=f"{tag}_dwin", out_dtype=BF16)
    dx, dg = _mm(dp, win, tb=True, tm=512, tn=1024, tk=3072, name=f"{tag}_dx",
                 extras=[(x, "tile"), (g, "row"), (dout, "tile")], outs=[(F32, "tile"), (F32, "colsum")],
                 epilogue=_norm_bwd_epilogue)
    return dx, dg, dwin, dcw[:3], dwout


def _ssd_conv_fwd(p, cw, cb, *, name):
    l = p.shape[0]
    tc = 128
    nb = SSD_CONV_DIM // tc
    off = SSD_D_INNER // tc

    def body(x_ref, w_ref, b_ref, o_ref):
        xv = x_ref[...]
        t_idx = lax.broadcasted_iota(jnp.int32, xv.shape, 0)
        pre = xv * w_ref[3:4, :] + b_ref[...]
        for s in (1, 2, 3):
            pre = pre + _shift_down(xv, s, t_idx) * w_ref[3 - s:4 - s, :]
        o_ref[...] = pre * _sigmoid(pre)

    return pl.pallas_call(
        body, name=name, grid=(nb,),
        in_specs=[pl.BlockSpec((l, tc), lambda j: (0, off + j)), pl.BlockSpec((4, tc), lambda j: (0, j)),
                  pl.BlockSpec((1, tc), lambda j: (0, j))],
        out_specs=pl.BlockSpec((l, tc), lambda j: (0, j)),
        out_shape=jax.ShapeDtypeStruct((l, SSD_CONV_DIM), F32), compiler_params=_cparams(1),
    )(p, cw, cb)


def _ssd_conv_bwd(p, cw, cb, dact, *, name):
    l = p.shape[0]
    tc = 128
    nb = SSD_CONV_DIM // tc
    off = SSD_D_INNER // tc

    def body(x_ref, w_ref, b_ref, g_ref, dx_ref, dw_ref):
        xv = x_ref[...]
        t_idx = lax.broadcasted_iota(jnp.int32, xv.shape, 0)
        xs = [xv] + [_shift_down(xv, s, t_idx) for s in (1, 2, 3)]
        pre = b_ref[...] + xs[0] * w_ref[3:4, :]
        for s in (1, 2, 3):
            pre = pre + xs[s] * w_ref[3 - s:4 - s, :]
        sg = _sigmoid(pre)
        dpre = g_ref[...] * sg * (1.0 + pre * (1.0 - sg))
        dx = dpre * w_ref[3:4, :]
        for s in (1, 2, 3):
            dx = dx + _shift_up(dpre, s, t_idx) * w_ref[3 - s:4 - s, :]
        dx_ref[...] = dx
        dw_ref[...] = jnp.zeros_like(dw_ref)
        for s in (0, 1, 2, 3):
            dw_ref[3 - s:4 - s, :] = jnp.sum(dpre * xs[s], axis=0, keepdims=True)
        dw_ref[4:5, :] = jnp.sum(dpre, axis=0, keepdims=True)

    return pl.pallas_call(
        body, name=name, grid=(nb,),
        in_specs=[pl.BlockSpec((l, tc), lambda j: (0, off + j)), pl.BlockSpec((4, tc), lambda j: (0, j)),
                  pl.BlockSpec((1, tc), lambda j: (0, j)), pl.BlockSpec((l, tc), lambda j: (0, j))],
        out_specs=[pl.BlockSpec((l, tc), lambda j: (0, j)), pl.BlockSpec((8, tc), lambda j: (0, j))],
        out_shape=[jax.ShapeDtypeStruct((l, SSD_CONV_DIM), F32), jax.ShapeDtypeStruct((8, SSD_CONV_DIM), F32)],
        compiler_params=_cparams(1),
    )(p, cw, cb, dact)


def _ssd_dt_fwd(p, bias, *, name):
    l = p.shape[0]
    tm = _tile(l, 1024)
    off = (SSD_D_INNER + SSD_CONV_DIM) // 128

    def body(x_ref, b_ref, o_ref):
        v = x_ref[...] + b_ref[...]
        o_ref[...] = jnp.maximum(v, 0.0) + jnp.log(1.0 + jnp.exp(-jnp.abs(v)))

    return pl.pallas_call(
        body, name=name, grid=(l // tm,),
        in_specs=[pl.BlockSpec((tm, 128), lambda i: (i, off)), pl.BlockSpec((1, 128), lambda i: (0, 0))],
        out_specs=pl.BlockSpec((tm, 128), lambda i: (i, 0)),
        out_shape=jax.ShapeDtypeStruct((l, 128), F32), compiler_params=_cparams(1),
    )(p, bias)


def _ssd_dt_bwd(p, bias, ddt, *, name):
    l = p.shape[0]
    tm = _tile(l, 1024)
    off = (SSD_D_INNER + SSD_CONV_DIM) // 128

    def body(x_ref, b_ref, g_ref, o_ref, db_ref):
        i = pl.program_id(0)
        d = g_ref[...] * _sigmoid(x_ref[...] + b_ref[...])
        o_ref[...] = d
        _accumulate(db_ref, jnp.sum(d, axis=0, keepdims=True), i == 0)

    return pl.pallas_call(
        body, name=name, grid=(l // tm,),
        in_specs=[pl.BlockSpec((tm, 128), lambda i: (i, off)), pl.BlockSpec((1, 128), lambda i: (0, 0)),
                  pl.BlockSpec((tm, 128), lambda i: (i, 0))],
        out_specs=[pl.BlockSpec((tm, 128), lambda i: (i, 0)), pl.BlockSpec((1, 128), lambda i: (0, 0))],
        out_shape=[jax.ShapeDtypeStruct((l, 128), F32), jax.ShapeDtypeStruct((1, 128), F32)],
        compiler_params=_cparams(1),
    )(p, bias, ddt)


def _row_to_col(r, eye):
    return jnp.sum(jnp.where(eye, r, 0.0), axis=1, keepdims=True)


def _col_to_row(c, eye):
    return jnp.sum(jnp.where(eye, c, 0.0), axis=0, keepdims=True)


def _ssd_chunk_common(b_ref, c_ref, dt_ref, a_ref, lam_scr):
    n = SSD_CHUNK
    row = lax.broadcasted_iota(jnp.int32, (n, n), 0)
    col = lax.broadcasted_iota(jnp.int32, (n, n), 1)
    bm, cm = b_ref[...].astype(BF16), c_ref[...].astype(BF16)
    g = _dot(cm, bm, NT)
    incl = (row <= col).astype(BF16)
    lam_scr[...] = _dot_exact(dt_ref[...] * a_ref[...], incl)
    return row, col, bm, cm, g


def _ssd_head_common(r, row, col, dt_ref, lam_scr):
    eye, tril = row == col, row >= col
    lam_r = lam_scr[r:r + 1, :]
    dt_r = dt_ref[r:r + 1, :]
    lam_c = _row_to_col(lam_r, eye)
    dt_c = _row_to_col(dt_r, eye)
    dk = jnp.where(tril, jnp.exp(jnp.minimum(lam_c - lam_r, 0.0)), 0.0)
    lam_last = jnp.sum(jnp.where(col[0:1, :] == SSD_CHUNK - 1, lam_r, 0.0), axis=1, keepdims=True)
    return eye, lam_r, dt_r, lam_c, dt_c, dk, lam_last


def _ssd_fwd(xh, act, dt_t, a_b, *, name):
    l = xh.shape[1]
    nc = l // SSD_CHUNK
    n, p_dim, hpg = SSD_CHUNK, SSD_HEAD_DIM, SSD_HPG

    gps = SSD_GROUPS_PER_STEP

    def body(x_ref, b_ref, c_ref, dt_ref, a_ref, y_ref, hp_ref, h_scr, lam_scr):
        @pl.when(pl.program_id(1) == 0)
        def _():
            h_scr[...] = jnp.zeros_like(h_scr)

        lanes = [pl.ds(gg * SSD_STATE, SSD_STATE) for gg in range(gps)]
        common = [_ssd_chunk_common(b_ref.at[:, lanes[gg]], c_ref.at[:, lanes[gg]], dt_ref.at[gg], a_ref.at[gg],
                                    lam_scr.at[gg]) for gg in range(gps)]
        for gg in range(gps):
            row, col, bm, cm, g = common[gg]
            for r in range(hpg):
                hd = gg * hpg + r
                _, _, dt_r, lam_c, dt_c, dk, lam_last = _ssd_head_common(r, row, col, dt_ref.at[gg], lam_scr.at[gg])
                xr = x_ref[hd]
                hr = h_scr[hd]
                w = (g * dk * dt_r).astype(BF16)
                y = _dot(w, xr.astype(BF16)) + _dot(cm, hr.astype(BF16), NT) * jnp.exp(lam_c)
                y_ref[hd] = y
                hp_ref[hd] = hr
                xw = (xr * (jnp.exp(lam_last - lam_c) * dt_c)).astype(BF16)
                h_scr[hd] = jnp.exp(lam_last) * hr + _dot(xw, bm, TN)

    g_off = SSD_D_INNER // (gps * SSD_STATE)
    n_grp = SSD_GROUPS // gps
    return pl.pallas_call(
        body, name=name, grid=(n_grp, nc),
        in_specs=[pl.BlockSpec((gps * hpg, n, p_dim), lambda g, c: (g, c, 0)),
                  pl.BlockSpec((n, gps * SSD_STATE), lambda g, c: (c, g_off + g)),
                  pl.BlockSpec((n, gps * SSD_STATE), lambda g, c: (c, g_off + n_grp + g)),
                  pl.BlockSpec((gps, 8, n), lambda g, c: (g, 0, c)),
                  pl.BlockSpec((gps, 8, 128), lambda g, c: (g, 0, 0))],
        out_specs=[pl.BlockSpec((gps * hpg, n, p_dim), lambda g, c: (g, c, 0)),
                   pl.BlockSpec((None, gps * hpg, p_dim, SSD_STATE), lambda g, c: (c, g, 0, 0))],
        out_shape=[jax.ShapeDtypeStruct(xh.shape, F32),
                   jax.ShapeDtypeStruct((nc, SSD_HEADS, p_dim, SSD_STATE), F32)],
        scratch_shapes=[pltpu.VMEM((gps * hpg, p_dim, SSD_STATE), F32), pltpu.VMEM((gps, 8, n), F32)],
        compiler_params=_cparams(2),
    )(xh, act, act, dt_t, a_b)


def _ssd_bwd(xh, act, dt_t, a_b, hprev, dyh, *, name):
    l = xh.shape[1]
    nc = l // SSD_CHUNK
    n, p_dim, hpg = SSD_CHUNK, SSD_HEAD_DIM, SSD_HPG
    gps = SSD_GROUPS_PER_STEP

    def body(x_ref, b_ref, c_ref, dt_ref, a_ref, hp_ref, dy_ref,
             dx_ref, db_ref, dc_ref, ddt_ref, da_ref, dh_scr, lam_scr, dlam_scr, ddt_scr):
        ci = pl.program_id(1)

        @pl.when(ci == 0)
        def _():
            dh_scr[...] = jnp.zeros_like(dh_scr)

        lanes = [pl.ds(gg * SSD_STATE, SSD_STATE) for gg in range(gps)]
        common = [_ssd_chunk_common(b_ref.at[:, lanes[gg]], c_ref.at[:, lanes[gg]], dt_ref.at[gg], a_ref.at[gg],
                                    lam_scr.at[gg]) for gg in range(gps)]
        dlam_scr[...] = jnp.zeros_like(dlam_scr)
        ddt_scr[...] = jnp.zeros_like(ddt_scr)
        for gg in range(gps):
            row, col, bm, cm, g = common[gg]
            dt_g, lam_g, dlam_g, ddt_g = dt_ref.at[gg], lam_scr.at[gg], dlam_scr.at[gg], ddt_scr.at[gg]
            dg_acc = jnp.zeros((n, n), F32)
            dc_acc = jnp.zeros((n, SSD_STATE), F32)
            db_acc = jnp.zeros((n, SSD_STATE), F32)
            for r in range(hpg):
                hd = gg * hpg + r
                eye, _, dt_r, lam_c, dt_c, dk, lam_last = _ssd_head_common(r, row, col, dt_g, lam_g)
                xr, dyr, hr, dhr = x_ref[hd], dy_ref[hd], hp_ref[hd], dh_scr[hd]
                xb, dyb, hb, dhb = xr.astype(BF16), dyr.astype(BF16), hr.astype(BF16), dhr.astype(BF16)
                e_l = jnp.exp(lam_c)
                e_last = jnp.exp(lam_last)
                decay_c = jnp.exp(lam_last - lam_c)
                w_c = decay_c * dt_c
                m = g * dk * dt_r
                dm = _dot(dyb, xb, NT)
                bdh = _dot(bm, dhb, NT)
                dx_ref[hd] = _dot(m.astype(BF16), dyb, TN) + w_c * bdh
                dg_acc = dg_acc + dm * dk * dt_r
                q_mat = dm * g * dk
                p_mat = q_mat * dt_r
                yoff = _dot(cm, hb, NT) * e_l
                q_c = jnp.sum(xr * bdh, axis=1, keepdims=True)
                dlam_c = (jnp.sum(p_mat, axis=1, keepdims=True) + jnp.sum(dyr * yoff, axis=1, keepdims=True)
                          - w_c * q_c)
                d_last = (jnp.sum(w_c * q_c, axis=0, keepdims=True)
                          + e_last * jnp.sum(jnp.sum(dhr * hr, axis=1, keepdims=True), axis=0, keepdims=True))
                dlam_g[r:r + 1, :] = (_col_to_row(dlam_c, eye) - jnp.sum(p_mat, axis=0, keepdims=True)
                                      + jnp.where(col[0:1, :] == n - 1, d_last, 0.0))
                ddt_g[r:r + 1, :] = jnp.sum(q_mat, axis=0, keepdims=True) + _col_to_row(decay_c * q_c, eye)
                dc_acc = dc_acc + e_l * _dot(dyb, hb)
                db_acc = db_acc + _dot((xr * w_c).astype(BF16), dhb)
                dh_scr[hd] = e_last * dhr + _dot((dyr * e_l).astype(BF16), cm, TN)

            dgb = dg_acc.astype(BF16)
            dc_ref[:, lanes[gg]] = _dot(dgb, bm) + dc_acc
            db_ref[:, lanes[gg]] = _dot(dgb, cm, TN) + db_acc
            rev = (row >= col).astype(BF16)
            da = _dot_exact(dlam_g[...], rev)
            ddt_ref[gg] = ddt_g[...] + da * a_ref[gg]
            _accumulate(da_ref.at[gg], da * dt_g[...], ci == 0)

        @pl.when(ci == nc - 1)
        def _():
            for gg in range(gps):
                da_ref[gg] = jnp.broadcast_to(jnp.sum(da_ref[gg], axis=1, keepdims=True), da_ref.shape[1:])

    g_off = SSD_D_INNER // (gps * SSD_STATE)
    n_grp = SSD_GROUPS // gps
    rc = lambda c: nc - 1 - c
    hspec = pl.BlockSpec((gps * hpg, n, p_dim), lambda g, c: (g, rc(c), 0))
    gspec = pl.BlockSpec((n, gps * SSD_STATE), lambda g, c: (rc(c), g))
    return pl.pallas_call(
        body, name=name, grid=(n_grp, nc),
        in_specs=[hspec,
                  pl.BlockSpec((n, gps * SSD_STATE), lambda g, c: (rc(c), g_off + g)),
                  pl.BlockSpec((n, gps * SSD_STATE), lambda g, c: (rc(c), g_off + n_grp + g)),
                  pl.BlockSpec((gps, 8, n), lambda g, c: (g, 0, rc(c))),
                  pl.BlockSpec((gps, 8, 128), lambda g, c: (g, 0, 0)),
                  pl.BlockSpec((None, gps * hpg, p_dim, SSD_STATE), lambda g, c: (rc(c), g, 0, 0)),
                  hspec],
        out_specs=[hspec, gspec, gspec,
                   pl.BlockSpec((gps, 8, n), lambda g, c: (g, 0, rc(c))),
                   pl.BlockSpec((gps, 8, 128), lambda g, c: (g, 0, 0))],
        out_shape=[jax.ShapeDtypeStruct(xh.shape, F32),
                   jax.ShapeDtypeStruct((l, SSD_GROUPS * SSD_STATE), F32),
                   jax.ShapeDtypeStruct((l, SSD_GROUPS * SSD_STATE), F32),
                   jax.ShapeDtypeStruct(dt_t.shape, F32),
                   jax.ShapeDtypeStruct(a_b.shape, F32)],
        scratch_shapes=[pltpu.VMEM((gps * hpg, p_dim, SSD_STATE), F32), pltpu.VMEM((gps, 8, n), F32),
                        pltpu.VMEM((gps, 8, n), F32), pltpu.VMEM((gps, 8, n), F32)],
        compiler_params=_cparams(2),
    )(xh, act, act, dt_t, a_b, hprev, dyh)


def _ssd_gate_fwd(y, act, p, d_vec, gn, *, name):
    l = y.shape[0]
    w = SSD_D_INNER
    tm = _tile(l, 256)

    def body(y_ref, xs_ref, z_ref, d_ref, g_ref, o_ref):
        for gi in range(SSD_GROUPS):
            sl = slice(gi * SSD_NORM_GROUP, (gi + 1) * SSD_NORM_GROUP)
            z = z_ref[:, sl]
            y2 = (y_ref[:, sl] + d_ref[:, sl] * xs_ref[:, sl]) * (z * _sigmoid(z))
            r = lax.rsqrt(jnp.mean(y2 * y2, axis=1, keepdims=True) + RMS_EPS)
            o_ref[:, sl] = (y2 * r * g_ref[:, sl]).astype(BF16)

    rows = pl.BlockSpec((tm, w), lambda i: (i, 0))
    vec = pl.BlockSpec((1, w), lambda i: (0, 0))
    return pl.pallas_call(
        body, name=name, grid=(l // tm,), in_specs=[rows, rows, rows, vec, vec], out_specs=rows,
        out_shape=jax.ShapeDtypeStruct((l, w), BF16), compiler_params=_cparams(1),
    )(y, act, p, d_vec, gn)


def _ssd_gate_bwd(dyn, y, act, p, d_vec, gn, *, name):
    l = y.shape[0]
    w = SSD_D_INNER
    tm = _tile(l, 256)

    def body(dyn_ref, y_ref, xs_ref, z_ref, d_ref, g_ref, dy_ref, dz_ref, dxs_ref, dd_ref, dg_ref):
        i = pl.program_id(0)
        for gi in range(SSD_GROUPS):
            sl = slice(gi * SSD_NORM_GROUP, (gi + 1) * SSD_NORM_GROUP)
            z, xs, dv = z_ref[:, sl], xs_ref[:, sl], d_ref[:, sl]
            s = _sigmoid(z)
            sz = z * s
            y1 = y_ref[:, sl] + dv * xs
            y2 = y1 * sz
            r = lax.rsqrt(jnp.mean(y2 * y2, axis=1, keepdims=True) + RMS_EPS)
            y2h = y2 * r
            dyn_v = dyn_ref[:, sl]
            d2h = dyn_v * g_ref[:, sl]
            dy2 = r * (d2h - y2h * jnp.mean(d2h * y2h, axis=1, keepdims=True))
            dy1 = dy2 * sz
            dy_ref[:, sl] = dy1
            dz_ref[:, sl] = dy2 * y1 * s * (1.0 + z * (1.0 - s))
            dxs_ref[:, sl] = dv * dy1
            _accumulate(dd_ref.at[:, sl], jnp.sum(dy1 * xs, axis=0, keepdims=True), i == 0)
            _accumulate(dg_ref.at[:, sl], jnp.sum(dyn_v * y2h, axis=0, keepdims=True), i == 0)

    rows = pl.BlockSpec((tm, w), lambda i: (i, 0))
    vec = pl.BlockSpec((1, w), lambda i: (0, 0))
    return pl.pallas_call(
        body, name=name, grid=(l // tm,), in_specs=[rows, rows, rows, rows, vec, vec],
        out_specs=[rows, rows, rows, vec, vec],
        out_shape=[jax.ShapeDtypeStruct((l, w), F32)] * 3 + [jax.ShapeDtypeStruct((1, w), F32)] * 2,
        compiler_params=_cparams(1),
    )(dyn, y, act, p, d_vec, gn)


def _heads_major(x):
    return x.reshape(x.shape[0], SSD_HEADS, SSD_HEAD_DIM).transpose(1, 0, 2)


def _ssd_layer_fwd(x, g, win, cw, cb, dt_bias, a_log, d_skip, gn, wout, tag):
    l = x.shape[0]
    h = _rmsnorm(x, g, name=f"{tag}_norm")
    p = _mm(h, win, tm=1024, tn=896, tk=1024, name=f"{tag}_in")
    act = _ssd_conv_fwd(p, cw, cb, name=f"{tag}_conv")
    bias = jnp.pad(dt_bias, (0, 128 - SSD_HEADS)).reshape(1, 128)
    dt = _ssd_dt_fwd(p, bias, name=f"{tag}_dt")
    xh = _heads_major(act[:, :SSD_D_INNER])
    dt_t = jnp.pad(dt[:, :SSD_HEADS].T.reshape(SSD_GROUPS, SSD_HPG, l), ((0, 0), (0, 8 - SSD_HPG), (0, 0)))
    a = -jnp.exp(a_log).reshape(SSD_GROUPS, SSD_HPG, 1)
    a_b = jnp.broadcast_to(jnp.pad(a, ((0, 0), (0, 8 - SSD_HPG), (0, 0))), (SSD_GROUPS, 8, 128))
    yh, hprev = _ssd_fwd(xh, act, dt_t, a_b, name=f"{tag}_scan")
    y = yh.transpose(1, 0, 2).reshape(l, SSD_D_INNER)
    d_vec = jnp.repeat(d_skip, SSD_HEAD_DIM).reshape(1, SSD_D_INNER)
    yn = _ssd_gate_fwd(y, act, p, d_vec, gn, name=f"{tag}_gate")
    xo = _mm(yn, wout, tm=1024, tn=1024, tk=2048, name=f"{tag}_out", extras=[(x, "tile")],
             epilogue=lambda acc, xt: (xt + acc,))
    return xo, (x, h, p, act, bias, xh, dt_t, a_b, hprev, y, d_vec, yn)


def _ssd_layer_bwd(dout, saved, g, win, cw, cb, gn, wout, tag):
    x, h, p, act, bias, xh, dt_t, a_b, hprev, y, d_vec, yn = saved
    l = x.shape[0]
    dyn = _mm(dout, wout, tb=True, tm=1024, tn=1024, tk=1024, name=f"{tag}_dyn")
    dwout = _mm(yn, dout, ta=True, tm=1024, tn=1024, tk=2048, name=f"{tag}_dwout", out_dtype=BF16)
    dy, dz, dxs_d, dd_vec, dgn = _ssd_gate_bwd(dyn, y, act, p, d_vec, gn, name=f"{tag}_gate_bwd")
    dxh, dbm, dcm, ddt_t, da_b = _ssd_bwd(xh, act, dt_t, a_b, hprev, _heads_major(dy), name=f"{tag}_scan_bwd")
    dxs = dxh.transpose(1, 0, 2).reshape(l, SSD_D_INNER) + dxs_d
    dact = jnp.concatenate([dxs, dbm, dcm], axis=1)
    dxbc, dcw8 = _ssd_conv_bwd(p, cw, cb, dact, name=f"{tag}_conv_bwd")
    ddt = jnp.pad(ddt_t[:, :SSD_HPG, :].reshape(SSD_HEADS, l).T, ((0, 0), (0, 128 - SSD_HEADS)))
    ddt_raw, dbias = _ssd_dt_bwd(p, bias, ddt, name=f"{tag}_dt_bwd")
    dp = jnp.concatenate([dz, dxbc, ddt_raw], axis=1)
    dwin = _mm(h, dp, ta=True, tm=1024, tn=896, tk=2048, name=f"{tag}_dwin", out_dtype=BF16)
    dx, dg = _mm(dp, win, tb=True, tm=256, tn=1024, tk=6272, name=f"{tag}_dx",
                 extras=[(x, "tile"), (g, "row"), (dout, "tile")], outs=[(F32, "tile"), (F32, "colsum")],
                 epilogue=_norm_bwd_epilogue)
    a_heads = a_b[:, :SSD_HPG, 0].reshape(SSD_HEADS)
    grads = dict(
        ssd_w_in=dwin[:, :SSD_IN_DIM], ssd_conv_w=dcw8[:4], ssd_conv_b=dcw8[4],
        ssd_dt_bias=dbias[0, :SSD_HEADS], ssd_a_log=da_b[:, :SSD_HPG, 0].reshape(SSD_HEADS) * a_heads,
        ssd_d=dd_vec.reshape(SSD_HEADS, SSD_HEAD_DIM).sum(axis=1), ssd_norm=dgn[0], ssd_w_out=dwout)
    return dx, dg, grads


def _local_step(x, tgt, w, gather_later=None, scatter_early=None):
    row = lambda v: v.reshape(1, -1)
    saved = []
    for i in range(DEPTH):
        kind, j = i % 3, i // 3
        x, s1 = _ffn_fwd(x, row(w["ffn1_norm"][i]), w["ffn1_w_gu"][i], w["ffn1_w_down"][i], f"l{i}f1")
        gm = row(w["mix_norm"][i])
        if kind == 0:
            hosted = gather_later[0] if (gather_later and i == 0) else None
            x, sm, carried = _sb_layer_fwd(x, gm, w["sb_w_qkv"][j], w["sb_w_o"][j], f"l{i}sb", hosted=hosted)
            if hosted:
                w = gather_later[1](w, carried)
        elif kind == 1:
            x, sm = _ssd_layer_fwd(x, gm, w["ssd_w_in"][j], w["ssd_conv_w"][j], row(w["ssd_conv_b"][j]),
                                   w["ssd_dt_bias"][j], w["ssd_a_log"][j], w["ssd_d"][j], row(w["ssd_norm"][j]),
                                   w["ssd_w_out"][j], f"l{i}ssd")
        else:
            x, sm = _sc_layer_fwd(x, gm, w["sc_w_in"][j], w["sc_conv_w"][j], w["sc_w_out"][j], f"l{i}sc")
        x, s2 = _ffn_fwd(x, row(w["ffn2_norm"][i]), w["ffn2_w_gu"][i], w["ffn2_w_down"][i], f"l{i}f2")
        saved.append((s1, sm, s2))

    loss, dx, dfinal = _final_loss(x, row(w["final_norm"]), tgt, name="final_loss")
    per_layer = {k: [None] * DEPTH for k in ("ffn1_norm", "ffn1_w_gu", "ffn1_w_down", "mix_norm",
                                             "ffn2_norm", "ffn2_w_gu", "ffn2_w_down")}
    per_layer.update({"sb_w_qkv": [None, None], "sb_w_o": [None, None]})
    grads = {"final_norm": dfinal[0]}
    early = None
    for i in reversed(range(DEPTH)):
        kind, j = i % 3, i // 3
        s1, sm, s2 = saved[i]
        dx, dg, dwgu, dwd = _ffn_bwd(dx, s2, row(w["ffn2_norm"][i]), w["ffn2_w_gu"][i], w["ffn2_w_down"][i], f"l{i}f2")
        per_layer["ffn2_norm"][i], per_layer["ffn2_w_gu"][i], per_layer["ffn2_w_down"][i] = dg[0], dwgu, dwd
        gm = row(w["mix_norm"][i])
        if kind == 0:
            hosted = scatter_early({**grads, **per_layer}) if (scatter_early and i == 0) else None
            dx, dg, dwqkv, dwo, carried = _sb_layer_bwd(dx, sm, gm, w["sb_w_qkv"][j], w["sb_w_o"][j], f"l{i}sb",
                                                        hosted=hosted)
            per_layer["sb_w_qkv"][j], per_layer["sb_w_o"][j] = dwqkv, dwo
            if hosted:
                early = carried
        elif kind == 1:
            dx, dg, sg = _ssd_layer_bwd(dx, sm, gm, w["ssd_w_in"][j], w["ssd_conv_w"][j], row(w["ssd_conv_b"][j]),
                                        row(w["ssd_norm"][j]), w["ssd_w_out"][j], f"l{i}ssd")
            sg["ssd_w_in"], sg["ssd_w_out"] = [sg["ssd_w_in"]], [sg["ssd_w_out"]]
            grads.update({k: (v if isinstance(v, list) else v[None]) for k, v in sg.items()})
        else:
            dx, dg, dwin, dcw, dwout = _sc_layer_bwd(dx, sm, gm, w["sc_w_in"][j], w["sc_conv_w"][j],
                                                     w["sc_w_out"][j], f"l{i}sc")
            grads.update(sc_w_in=[dwin], sc_conv_w=dcw[None], sc_w_out=[dwout])
        per_layer["mix_norm"][i] = dg[0]
        dx, dg, dwgu, dwd = _ffn_bwd(dx, s1, row(w["ffn1_norm"][i]), w["ffn1_w_gu"][i], w["ffn1_w_down"][i], f"l{i}f1")
        per_layer["ffn1_norm"][i], per_layer["ffn1_w_gu"][i], per_layer["ffn1_w_down"][i] = dg[0], dwgu, dwd
    for k, v in per_layer.items():
        grads[k] = jnp.stack(v) if k.endswith("_norm") else v
    return loss, dx, grads, early


_HBM = pl.BlockSpec(memory_space=pltpu.HBM)


def _remote(src, dst, send_sems, recv_sems, idx, dev):
    return pltpu.make_async_remote_copy(src_ref=src, dst_ref=dst, send_sem=send_sems.at[idx], recv_sem=recv_sems.at[idx],
                                        device_id=dev, device_id_type=pl.DeviceIdType.MESH)


def _exchange_call(body, xs, out_shapes, n_copies, name):
    n = len(xs)
    return pl.pallas_call(
        body, name=name, in_specs=[_HBM] * n, out_specs=[_HBM] * n,
        out_shape=[jax.ShapeDtypeStruct(s, x.dtype) for s, x in zip(out_shapes, xs)],
        scratch_shapes=[pltpu.SemaphoreType.DMA((n, n_copies)), pltpu.SemaphoreType.DMA((n, n_copies)),
                        pltpu.SemaphoreType.DMA((n,))],
    )(*xs)


def _gather(xs, *, name):
    n = len(xs)

    def body(*refs):
        start, finish = _gather_steps(refs[:n], refs[n:2 * n], *refs[2 * n:])
        start()
        finish()

    return _exchange_call(body, xs, _gather_shapes(xs), _GATHER_COPIES, name)


_GATHER_COPIES = 7


def _gather_shapes(xs):
    return [(N_DEV,) + x.shape for x in xs]


def _gather_steps(x_refs, o_refs, send_sems, recv_sems, local_sems):
    n = len(x_refs)

    def plan():
        mx, my, mc = lax.axis_index("x"), lax.axis_index("y"), lax.axis_index("c")
        slot = lambda px, py, pc: 4 * px + 2 * py + pc
        me, sibling = (mx, my, mc), (mx, my, 1 - mc)
        chips = [(1 - mx, my), (mx, 1 - my), (1 - mx, 1 - my)]
        locals_, first = [], []
        for a in range(n):
            x_ref, o_ref = x_refs[a], o_refs[a]
            locals_.append(pltpu.make_async_copy(x_ref, o_ref.at[slot(*me)], local_sems.at[a]))
            first.append(_remote(x_ref, o_ref.at[slot(*me)], send_sems, recv_sems, (a, 0), sibling))
            for j, chip in enumerate(chips):
                first.append(_remote(x_ref, o_ref.at[slot(*me)], send_sems, recv_sems, (a, 1 + j), (*chip, mc)))
        return locals_, first, slot, me, sibling, chips, mc

    def start():
        locals_, first = plan()[:2]
        for cp in locals_ + first:
            cp.start()

    def finish():
        locals_, first, slot, me, sibling, chips, mc = plan()
        passed = []
        for j, chip in enumerate(chips):
            for a in range(n):
                landed = o_refs[a].at[slot(*chip, mc)]
                _remote(landed, landed, send_sems, recv_sems, (a, 1 + j), me).wait_recv()
                fwd = _remote(landed, landed, send_sems, recv_sems, (a, 4 + j), sibling)
                fwd.start()
                passed.append(fwd)
        for a in range(n):
            from_sib = o_refs[a].at[slot(*sibling)]
            _remote(from_sib, from_sib, send_sems, recv_sems, (a, 0), me).wait_recv()
            for j, chip in enumerate(chips):
                via_sib = o_refs[a].at[slot(*chip, 1 - mc)]
                _remote(via_sib, via_sib, send_sems, recv_sems, (a, 4 + j), me).wait_recv()
        for cp in first + passed:
            cp.wait_send()
        for cp in locals_:
            cp.wait()

    return start, finish


def _scatter_sibling(xs, *, name):
    n = len(xs)

    def body(*refs):
        x_refs, o_refs = refs[:n], refs[n:2 * n]
        send_sems, recv_sems, _ = refs[2 * n:]
        mx, my, mc = lax.axis_index("x"), lax.axis_index("y"), lax.axis_index("c")
        sibling = (mx, my, 1 - mc)
        sends = []
        for a in range(n):
            for ch in range(4):
                sends.append(_remote(x_refs[a].at[ch, 1 - mc], o_refs[a].at[ch], send_sems, recv_sems, (a, ch), sibling))
        for cp in sends:
            cp.start()
        for cp in sends:
            cp.wait_recv()
        for cp in sends:
            cp.wait_send()

    return _exchange_call(body, xs, [(4,) + x.shape[2:] for x in xs], 4, name)


def _scatter_chips(ys, *, name):
    n = len(ys)

    def body(*refs):
        start, finish = _chip_scatter_steps(refs[:n], refs[n:2 * n], *refs[2 * n:])
        start()
        finish()

    return _exchange_call(body, ys, _chip_scatter_shapes(ys), _CHIP_SCATTER_COPIES, name)


_CHIP_SCATTER_COPIES = 3


def _chip_scatter_shapes(ys):
    return [y.shape for y in ys]


def _chip_scatter_steps(y_refs, o_refs, send_sems, recv_sems, local_sems):
    n = len(y_refs)

    def plan():
        mx, my, mc = lax.axis_index("x"), lax.axis_index("y"), lax.axis_index("c")
        mine = 2 * mx + my
        chips = [(1 - mx, my), (mx, 1 - my), (1 - mx, 1 - my)]
        locals_, sends, recvs = [], [], []
        for a in range(n):
            locals_.append(pltpu.make_async_copy(y_refs[a].at[mine], o_refs[a].at[mine], local_sems.at[a]))
            for j, (px, py) in enumerate(chips):
                theirs = 2 * px + py
                sends.append(_remote(y_refs[a].at[theirs], o_refs[a].at[mine], send_sems, recv_sems, (a, j), (px, py, mc)))
                recvs.append(_remote(y_refs[a].at[theirs], o_refs[a].at[theirs], send_sems, recv_sems, (a, j), (px, py, mc)))
        return locals_, sends, recvs

    def start():
        locals_, sends, _ = plan()
        for cp in locals_ + sends:
            cp.start()

    def finish():
        locals_, sends, recvs = plan()
        for cp in recvs:
            cp.wait_recv()
        for cp in sends:
            cp.wait_send()
        for cp in locals_:
            cp.wait()

    return start, finish


def _pair_add(x, r, *, name):
    _, _, rows, c = x.shape
    tr = _tile(rows, 512, 16)

    def body(core_ref, x_ref, r_ref, o_ref):
        o_ref[...] = (x_ref[...].astype(F32) + r_ref[...].astype(F32)).astype(o_ref.dtype)

    core = lax.axis_index("c").astype(jnp.int32).reshape(1)
    return pl.pallas_call(
        body, name=name,
        grid_spec=pltpu.PrefetchScalarGridSpec(
            num_scalar_prefetch=1, grid=(4, rows // tr),
            in_specs=[pl.BlockSpec((None, None, tr, c), lambda ch, i, core: (ch, core[0], i, 0)),
                      pl.BlockSpec((None, tr, c), lambda ch, i, core: (ch, i, 0))],
            out_specs=pl.BlockSpec((None, tr, c), lambda ch, i, core: (ch, i, 0))),
        out_shape=jax.ShapeDtypeStruct((4, rows, c), x.dtype), compiler_params=_cparams(2),
    )(core, x, r)


def _adamw_reduce(parts, w, m, v, *, name):
    r, c = w.shape
    n_parts = parts.shape[0]
    tr = _tile(r, 256, 16)
    bc1 = 1.0 - ADAM_B1 ** ADAM_STEP
    bc2 = 1.0 - ADAM_B2 ** ADAM_STEP

    def body(p_ref, w_ref, m_ref, v_ref, g_ref, d_ref, nm_ref, nv_ref):
        g = p_ref[0].astype(F32)
        for q in range(1, n_parts):
            g = g + p_ref[q].astype(F32)
        nm = ADAM_B1 * m_ref[...] + (1.0 - ADAM_B1) * g
        nv = ADAM_B2 * v_ref[...] + (1.0 - ADAM_B2) * (g * g)
        g_ref[...] = g
        nm_ref[...] = nm
        nv_ref[...] = nv
        d_ref[...] = -ADAM_LR * ((nm / bc1) / (jnp.sqrt(nv / bc2) + ADAM_EPS) + ADAM_WD * w_ref[...])

    blk = pl.BlockSpec((tr, c), lambda i: (i, 0))
    return pl.pallas_call(
        body, name=name, grid=(r // tr,),
        in_specs=[pl.BlockSpec((n_parts, tr, c), lambda i: (0, i, 0)), blk, blk, blk], out_specs=[blk] * 4,
        out_shape=[jax.ShapeDtypeStruct((r, c), F32)] * 4, compiler_params=_cparams(1),
    )(parts, w, m, v)


def _col_full(g):
    return g.transpose(1, 2, 0, 3).reshape(g.shape[1], g.shape[2], -1)


def _col_parts(f):
    n, k, c8 = f.shape
    return f.reshape(n, k, N_DEV, c8 // N_DEV).transpose(2, 0, 1, 3)


def _row_full(g):
    return g.transpose(1, 0, 2, 3).reshape(g.shape[1], -1, g.shape[3])


def _row_parts(f):
    n, r8, c = f.shape
    return f.reshape(n, N_DEV, r8 // N_DEV, c).transpose(1, 0, 2, 3)


def _gu_full(g):
    n, d, c = g.shape[1:]
    return g.reshape(2, 4, n, d, c).transpose(2, 0, 3, 1, 4).reshape(n, 2, d, 4 * c)


def _gu_parts(f):
    n, _, d, c4 = f.shape
    return f.reshape(n, 2, d, 4, c4 // 4).transpose(1, 3, 0, 2, 4).reshape(N_DEV, n, d, c4 // 4)


def _ssd_in_full(g):
    return jnp.pad(_col_full(g), ((0, 0), (0, 0), (0, SSD_IN_PAD - SSD_IN_DIM)))


_MATMUL_WEIGHTS = (
    ("ffn1_w_gu", _gu_full, _gu_parts), ("ffn1_w_down", _row_full, _row_parts),
    ("ffn2_w_gu", _gu_full, _gu_parts), ("ffn2_w_down", _row_full, _row_parts),
    ("sb_w_qkv", _col_full, _col_parts), ("sb_w_o", _row_full, _row_parts),
    ("ssd_w_in", _ssd_in_full, _col_parts), ("ssd_w_out", _row_full, _row_parts),
    ("sc_w_in", _col_full, _col_parts), ("sc_w_out", _row_full, _row_parts),
)
_FIRST_WEIGHTS = ("ffn1_w_gu", "ffn1_w_down", "sb_w_qkv", "sb_w_o")
_CONV_WEIGHTS = ("ssd_conv_w", "sc_conv_w")
_REPLICATED = ("ffn1_norm", "mix_norm", "ffn2_norm", "final_norm", "ssd_conv_b", "ssd_norm",
               "ssd_dt_bias", "ssd_a_log", "ssd_d")
_ORDER = ("ffn1_norm", "ffn1_w_gu", "ffn1_w_down", "mix_norm", "ffn2_norm", "ffn2_w_gu", "ffn2_w_down",
          "sb_w_qkv", "sb_w_o", "ssd_w_in", "ssd_conv_w", "ssd_conv_b", "ssd_dt_bias", "ssd_a_log", "ssd_d",
          "ssd_norm", "ssd_w_out", "sc_w_in", "sc_conv_w", "sc_w_out", "final_norm")
_LANES = 1024


def _rows_of(a):
    flat = a.reshape(-1)
    pad = -flat.shape[0] % _LANES
    return jnp.pad(flat, (0, pad)).reshape(-1, _LANES)


def _pack_rows(arrays, mult):
    rows = [_rows_of(a) for a in arrays]
    packed = jnp.concatenate(rows, axis=0)
    pad = -packed.shape[0] % mult
    return jnp.pad(packed, ((0, pad), (0, 0))), [r.shape[0] for r in rows]


def _unpack_rows(packed, counts, shapes, lead=()):
    out, off = [], 0
    for n, shp in zip(counts, shapes):
        size = math.prod(shp)
        seg = packed[..., off:off + n, :].reshape(lead + (n * _LANES,))[..., :size]
        out.append(seg.reshape(lead + tuple(shp)))
        off += n
    return out


def kernel(x, ffn1_norm, ffn1_w_gu, ffn1_w_down, mix_norm, ffn2_norm, ffn2_w_gu, ffn2_w_down, sb_w_qkv, sb_w_o, ssd_w_in, ssd_conv_w, ssd_conv_b, ssd_dt_bias, ssd_a_log, ssd_d, ssd_norm, ssd_w_out, sc_w_in, sc_conv_w, sc_w_out, final_norm, loss_target, m_ffn1_norm, m_ffn1_w_gu, m_ffn1_w_down, m_mix_norm, m_ffn2_norm, m_ffn2_w_gu, m_ffn2_w_down, m_sb_w_qkv, m_sb_w_o, m_ssd_w_in, m_ssd_conv_w, m_ssd_conv_b, m_ssd_dt_bias, m_ssd_a_log, m_ssd_d, m_ssd_norm, m_ssd_w_out, m_sc_w_in, m_sc_conv_w, m_sc_w_out, m_final_norm, v_ffn1_norm, v_ffn1_w_gu, v_ffn1_w_down, v_mix_norm, v_ffn2_norm, v_ffn2_w_gu, v_ffn2_w_down, v_sb_w_qkv, v_sb_w_o, v_ssd_w_in, v_ssd_conv_w, v_ssd_conv_b, v_ssd_dt_bias, v_ssd_a_log, v_ssd_d, v_ssd_norm, v_ssd_w_out, v_sc_w_in, v_sc_conv_w, v_sc_w_out, v_final_norm):
    w = dict(ffn1_norm=ffn1_norm, ffn1_w_gu=ffn1_w_gu, ffn1_w_down=ffn1_w_down, mix_norm=mix_norm, ffn2_norm=ffn2_norm, ffn2_w_gu=ffn2_w_gu, ffn2_w_down=ffn2_w_down, sb_w_qkv=sb_w_qkv, sb_w_o=sb_w_o, ssd_w_in=ssd_w_in, ssd_conv_w=ssd_conv_w, ssd_conv_b=ssd_conv_b, ssd_dt_bias=ssd_dt_bias, ssd_a_log=ssd_a_log, ssd_d=ssd_d, ssd_norm=ssd_norm, ssd_w_out=ssd_w_out, sc_w_in=sc_w_in, sc_conv_w=sc_conv_w, sc_w_out=sc_w_out, final_norm=final_norm)
    mom = dict(ffn1_norm=m_ffn1_norm, ffn1_w_gu=m_ffn1_w_gu, ffn1_w_down=m_ffn1_w_down, mix_norm=m_mix_norm, ffn2_norm=m_ffn2_norm, ffn2_w_gu=m_ffn2_w_gu, ffn2_w_down=m_ffn2_w_down, sb_w_qkv=m_sb_w_qkv, sb_w_o=m_sb_w_o, ssd_w_in=m_ssd_w_in, ssd_conv_w=m_ssd_conv_w, ssd_conv_b=m_ssd_conv_b, ssd_dt_bias=m_ssd_dt_bias, ssd_a_log=m_ssd_a_log, ssd_d=m_ssd_d, ssd_norm=m_ssd_norm, ssd_w_out=m_ssd_w_out, sc_w_in=m_sc_w_in, sc_conv_w=m_sc_conv_w, sc_w_out=m_sc_w_out, final_norm=m_final_norm)
    var = dict(ffn1_norm=v_ffn1_norm, ffn1_w_gu=v_ffn1_w_gu, ffn1_w_down=v_ffn1_w_down, mix_norm=v_mix_norm, ffn2_norm=v_ffn2_norm, ffn2_w_gu=v_ffn2_w_gu, ffn2_w_down=v_ffn2_w_down, sb_w_qkv=v_sb_w_qkv, sb_w_o=v_sb_w_o, ssd_w_in=v_ssd_w_in, ssd_conv_w=v_ssd_conv_w, ssd_conv_b=v_ssd_conv_b, ssd_dt_bias=v_ssd_dt_bias, ssd_a_log=v_ssd_a_log, ssd_d=v_ssd_d, ssd_norm=v_ssd_norm, ssd_w_out=v_ssd_w_out, sc_w_in=v_sc_w_in, sc_conv_w=v_sc_conv_w, sc_w_out=v_sc_w_out, final_norm=v_final_norm)
    me = 4 * lax.axis_index("x") + 2 * lax.axis_index("y") + lax.axis_index("c")
    big = [n for n, _, _ in _MATMUL_WEIGHTS]
    two_d = lambda a: a.reshape(-1, a.shape[-1])

    to_full = {n: f for n, f, _ in _MATMUL_WEIGHTS}
    to_parts = {n: f for n, _, f in _MATMUL_WEIGHTS}
    first = [(n, 0) for n in _FIRST_WEIGHTS]
    later = [(n, i) for n in big for i in range(1 if n in _FIRST_WEIGHTS else 0, w[n].shape[0])]

    def shards(group):
        return [two_d(w[n][i].astype(BF16)) for n, i in group]

    def layers(group, gathered):
        out = {}
        for (n, i), g in zip(group, gathered):
            out.setdefault(n, []).append(to_full[n](g.reshape((N_DEV, 1) + w[n].shape[1:]))[0])
        return out

    gathered = _gather(shards(first) + [two_d(w[n]) for n in _CONV_WEIGHTS], name="gather_first")
    full = dict(w)
    full.update(layers(first, gathered))
    for n, g in zip(_CONV_WEIGHTS, gathered[len(first):]):
        full[n] = _col_full(g.reshape((N_DEV,) + w[n].shape))

    def with_later(wd, gathered_later):
        wd = dict(wd)
        for n, ls in layers(later, gathered_later).items():
            wd[n] = (wd[n] if n in _FIRST_WEIGHTS else []) + ls
        return wd

    later_shards = shards(later)
    gather_later = (_Hosted(_gather_steps, later_shards, _gather_shapes(later_shards), _GATHER_COPIES), with_later)

    def chip_sums(group, grads, tag):
        parts = []
        for n, i in group:
            p8 = to_parts[n](grads[n][i][None].astype(BF16))
            parts.append(p8.reshape(4, 2, -1, p8.shape[-1]))
        from_sibling = _scatter_sibling(parts, name=f"scatter_sibling_{tag}")
        return [_pair_add(p, r, name=f"pair_add_{tag}_{n}{i}") for (n, i), p, r in zip(group, parts, from_sibling)]

    def scatter_early(grads):
        ys = chip_sums(later, grads, "later")
        return _Hosted(_chip_scatter_steps, ys, _chip_scatter_shapes(ys), _CHIP_SCATTER_COPIES)

    loss_part, dx, grads, recv_later = _local_step(x[0], loss_target[0], full, gather_later, scatter_early)
    loss = lax.psum(loss_part[0, 0], ("x", "y", "c"))

    recv_first = _scatter_chips(chip_sums(first, grads, "first"), name="scatter_chips_first")
    contrib = {n: [r] for (n, _), r in zip(first, recv_first)}
    for (n, _), r in zip(later, recv_later):
        contrib.setdefault(n, []).append(r)
    out_g, out_d, out_m, out_v = {}, {}, {}, {}

    def update(n, parts):
        res = _adamw_reduce(parts, two_d(w[n]), two_d(mom[n]), two_d(var[n]), name=f"adamw_{n}")
        out_g[n], out_d[n], out_m[n], out_v[n] = (r.reshape(w[n].shape) for r in res)

    for n in big:
        update(n, contrib[n][0] if len(contrib[n]) == 1 else jnp.concatenate(contrib[n], axis=1))

    small = list(_REPLICATED) + list(_CONV_WEIGHTS)
    small_shapes = [grads[n].shape for n in small]
    spacked, scounts = _pack_rows([grads[n].astype(F32) for n in small], 8)
    sg = _unpack_rows(_gather([spacked], name="gather_small_grads")[0], scounts, small_shapes, (N_DEV,))
    sg = dict(zip(small, sg))
    rep_w, rcounts = _pack_rows([w[n] for n in _REPLICATED], 8)
    rep_m, _ = _pack_rows([mom[n] for n in _REPLICATED], 8)
    rep_v, _ = _pack_rows([var[n] for n in _REPLICATED], 8)
    rep_p = jnp.concatenate([_rows_of(sg[n].reshape(N_DEV, -1)[q]) for q in range(N_DEV) for n in _REPLICATED], axis=0)
    rep_p = rep_p.reshape(N_DEV, -1, _LANES)
    rep_p = jnp.pad(rep_p, ((0, 0), (0, rep_w.shape[0] - rep_p.shape[1]), (0, 0)))
    res = _adamw_reduce(rep_p, rep_w, rep_m, rep_v, name="adamw_replicated")
    rep_shapes = [w[n].shape for n in _REPLICATED]
    for tgt, r in zip((out_g, out_d, out_m, out_v), res):
        for n, a in zip(_REPLICATED, _unpack_rows(r, rcounts, rep_shapes)):
            tgt[n] = a
    for n in _CONV_WEIGHTS:
        c = w[n].shape[-1]
        mine = lax.dynamic_slice_in_dim(sg[n], me * c, c, axis=sg[n].ndim - 1)
        update(n, mine.reshape(N_DEV, -1, c))

    return (loss, dx[None], *[out_g[n] for n in _ORDER], *[out_d[n] for n in _ORDER],
            *[out_m[n] for n in _ORDER], *[out_v[n] for n in _ORDER])
```

```python
import functools
import math

import jax
import jax.numpy as jnp
from jax import lax
from jax.experimental import pallas as pl
from jax.experimental.pallas import tpu as pltpu

F32 = jnp.float32
BF16 = jnp.bfloat16

D_MODEL = 1024
D_FF = 2816
DEPTH = 4
N_DEV = 8
SB_HEADS = 16
SB_HEAD_DIM = 64
SB_TILE = 256
SB_HEADS_PER_STEP = 2
SB_FWD_GROUPS = (2, 1)
SB_BWD_GROUPS = (4, 2, 1)
SSD_HEADS = 32
SSD_HEAD_DIM = 64
SSD_GROUPS = 8
SSD_HPG = 4
SSD_STATE = 128
SSD_CHUNK = 128
SSD_GROUPS_PER_STEP = 2
SSD_D_INNER = 2048
SSD_CONV_DIM = 4096
SSD_IN_DIM = 6176
SSD_IN_PAD = 6272
SSD_NORM_GROUP = 256
RMS_EPS = 1e-6
ADAM_LR = 0.001
ADAM_B1 = 0.9
ADAM_B2 = 0.999
ADAM_EPS = 1e-08
ADAM_WD = 0.01
ADAM_STEP = 10
VMEM_LIMIT = 60 * 1024 * 1024

NT = (((1,), (1,)), ((), ()))
TN = (((0,), (0,)), ((), ()))
NN = (((1,), (0,)), ((), ()))


def _cparams(n_axes):
    return pltpu.CompilerParams(dimension_semantics=("arbitrary",) * n_axes, vmem_limit_bytes=VMEM_LIMIT)


def _tile(n, want, mult=8):
    if n <= want:
        return n
    for t in range(want, 0, -1):
        if n % t == 0 and t % mult == 0:
            return t
    return n


def _sigmoid(x):
    return 1.0 / (1.0 + jnp.exp(-x))


def _dot(a, b, dn=NN):
    return lax.dot_general(a, b, dn, preferred_element_type=F32)


def _split3(x):
    x1 = x.astype(BF16)
    r1 = x - x1.astype(F32)
    x2 = r1.astype(BF16)
    x3 = (r1 - x2.astype(F32)).astype(BF16)
    return x1, x2, x3


def _dot_exact(x, t):
    x1, x2, x3 = _split3(x)
    return _dot(x1, t) + _dot(x2, t) + _dot(x3, t)


ROW_SUM_LANES = 1


def _cumsum_operand(tri):
    return jnp.concatenate([tri, tri], axis=0)


def _cumsum_rowsum(x, tri2):
    x1 = x.astype(BF16)
    x2 = (x - x1.astype(F32)).astype(BF16)
    return _dot(jnp.concatenate([x1, x2], axis=1), tri2), jnp.sum(x, axis=1, keepdims=True)


def _across_lanes(c, t):
    return c


def _mm(a, b, *, name, ta=False, tb=False, sa=False, sb=False, so=False, tm=512, tn=1024, tk=1024,
        out_dtype=F32, epilogue=None, extras=(), outs=None, pair=None, col_chunk=None):
    ash, bsh = a.shape[-2:], b.shape[-2:]
    m, k = (ash[1], ash[0]) if ta else ash
    n = bsh[0] if tb else bsh[1]
    s_n = pair or (a.shape[0] if sa else (b.shape[0] if sb else 1))
    tm, tn, tk = _tile(m, tm), _tile(n, tn, 128), _tile(k, tk, 128)
    nk = k // tk
    if outs is None:
        outs = [(out_dtype, "stile" if so else "tile")]
    if epilogue is None:
        epilogue = lambda acc: (acc,)

    if ta:
        a_blk, a_idx = (tk, tm), (lambda j, i, kk: (kk, i))
    else:
        a_blk, a_idx = (tm, tk), (lambda j, i, kk: (i, kk))
    if tb:
        b_blk, b_idx = (tn, tk), (lambda j, i, kk: (j, kk))
    else:
        b_blk, b_idx = (tk, tn), (lambda j, i, kk: (kk, j))

    def lead(blk, idx, has_s):
        if not has_s:
            return pl.BlockSpec(blk, idx)
        return pl.BlockSpec((s_n,) + blk, lambda j, i, kk: (0,) + idx(j, i, kk))

    kinds = {
        "tile": lambda: pl.BlockSpec((tm, tn), lambda j, i, kk: (i, j)),
        "stile": lambda: pl.BlockSpec((s_n, tm, tn), lambda j, i, kk: (0, i, j)),
        "row": lambda: pl.BlockSpec((1, tn), lambda j, i, kk: (0, j)),
        "colsum": lambda: pl.BlockSpec((1, tn), lambda j, i, kk: (0, j)),
    }
    shapes = {"tile": (m, n), "stile": (s_n, m, n), "colsum": (1, n)}
    in_specs = [lead(a_blk, a_idx, sa), lead(b_blk, b_idx, sb)] + [kinds[kd]() for _, kd in extras]
    out_specs = [kinds[kd]() for _, kd in outs]
    out_shape = [jax.ShapeDtypeStruct(shapes[kd], dt) for dt, kd in outs]
    n_ex, n_out = len(extras), len(outs)
    dn = ((((0,) if ta else (1,)), ((1,) if tb else (0,))), ((), ()))
    acc_shape = (s_n, tm, tn) if so else (tm, tn)

    def body(*refs):
        a_ref, b_ref = refs[0], refs[1]
        ex_refs = refs[2:2 + n_ex]
        o_refs = refs[2 + n_ex:2 + n_ex + n_out]
        i = pl.program_id(1)
        kk = pl.program_id(2)

        def products():
            for s in range(s_n if (sa or sb) else 1):
                av = (a_ref[s] if sa else a_ref[...]).astype(BF16)
                bv = (b_ref[s] if sb else b_ref[...]).astype(BF16)
                yield s, lax.dot_general(av, bv, dn, preferred_element_type=F32)

        def finish(accv):
            vals = epilogue(accv, *[r[...] for r in ex_refs])
            for (dt, kd), o_ref, val in zip(outs, o_refs, vals):
                if kd == "colsum":
                    _accumulate(o_ref, val, i == 0)
                elif kd == "stile":
                    for s in range(s_n):
                        o_ref[s] = val[s].astype(dt)
                else:
                    o_ref[...] = val.astype(dt)

        if nk == 1 and col_chunk:
            bounds = [(c0, min(col_chunk, tn - c0)) for c0 in range(0, tn, col_chunk)]
            n_s = s_n if (sa or sb) else 1
            a_vals = [(a_ref[s] if sa else a_ref[...]).astype(BF16) for s in range(n_s if sa else 1)]
            accs = []
            for c0, cw in bounds:
                ds = []
                for s in range(n_s):
                    idx = ((s,) if sb else ()) + ((pl.ds(c0, cw), slice(None)) if tb else (slice(None), pl.ds(c0, cw)))
                    ds.append(lax.dot_general(a_vals[s if sa else 0], b_ref[idx].astype(BF16), dn,
                                              preferred_element_type=F32))
                accs.append(tuple(ds) if so else functools.reduce(jnp.add, ds))
            for (c0, cw), accv in zip(bounds, accs):
                cols = pl.ds(c0, cw)
                exv = [r[:, :, cols] if kd == "stile" else r[:, cols] for r, (_, kd) in zip(ex_refs, extras)]
                vals = epilogue(accv, *exv)
                for (dt, kd), o_ref, val in zip(outs, o_refs, vals):
                    if kd == "stile":
                        for s in range(s_n):
                            o_ref[s, :, cols] = val[s].astype(dt)
                    else:
                        o_ref[:, cols] = val.astype(dt)
            return

        if nk == 1:
            ds = [d for _, d in products()]
            finish(tuple(ds) if so else functools.reduce(jnp.add, ds))
            return

        acc = refs[-1]

        @pl.when(kk == 0)
        def _():
            acc[...] = jnp.zeros_like(acc)

        for s, d in products():
            if so:
                acc[s] += d
            else:
                acc[...] += d

        @pl.when(kk == nk - 1)
        def _():
            finish(tuple(acc[s] for s in range(s_n)) if so else acc[...])

    res = pl.pallas_call(
        body, name=name, grid=(n // tn, m // tm, nk),
        in_specs=in_specs, out_specs=out_specs, out_shape=out_shape,
        scratch_shapes=[pltpu.VMEM(acc_shape, F32)] if nk > 1 else [], compiler_params=_cparams(3),
    )(a, b, *[e for e, _ in extras])
    return res[0] if len(res) == 1 else res


def _accumulate(o_ref, val, first):
    @pl.when(first)
    def _():
        o_ref[...] = val

    @pl.when(jnp.logical_not(first))
    def _():
        o_ref[...] += val


def _rmsnorm(x, g, *, name):
    l, d = x.shape
    tm = _tile(l, 512)

    def body(x_ref, g_ref, o_ref):
        xv = x_ref[...]
        r = lax.rsqrt(jnp.mean(xv * xv, axis=1, keepdims=True) + RMS_EPS)
        o_ref[...] = (xv * r * g_ref[...]).astype(BF16)

    return pl.pallas_call(
        body, name=name, grid=(l // tm,),
        in_specs=[pl.BlockSpec((tm, d), lambda i: (i, 0)), pl.BlockSpec((1, d), lambda i: (0, 0))],
        out_specs=pl.BlockSpec((tm, d), lambda i: (i, 0)),
        out_shape=jax.ShapeDtypeStruct((l, d), BF16), compiler_params=_cparams(1),
    )(x, g)


def _norm_bwd_epilogue(dh, x, g, dres):
    r = lax.rsqrt(jnp.mean(x * x, axis=1, keepdims=True) + RMS_EPS)
    xh = x * r
    dg = jnp.sum(dh * xh, axis=0, keepdims=True)
    dxh = dh * g
    dx = r * (dxh - xh * jnp.mean(dxh * xh, axis=1, keepdims=True))
    return dres + dx, dg


def _final_loss(x, g, tgt, *, name):
    l, d = x.shape
    tm = _tile(l, 512)

    def body(x_ref, g_ref, t_ref, loss_ref, dx_ref, dg_ref):
        i = pl.program_id(0)
        xv, gv = x_ref[...], g_ref[...]
        r = lax.rsqrt(jnp.mean(xv * xv, axis=1, keepdims=True) + RMS_EPS)
        xh = xv * r
        e = xh * gv - t_ref[...]
        part = 0.5 * jnp.sum(jnp.mean(e * e, axis=1, keepdims=True), axis=0, keepdims=True)
        dy = e * (1.0 / d)
        dg = jnp.sum(dy * xh, axis=0, keepdims=True)
        dxh = dy * gv
        dx_ref[...] = r * (dxh - xh * jnp.mean(dxh * xh, axis=1, keepdims=True))
        _accumulate(dg_ref, dg, i == 0)
        _accumulate(loss_ref, jnp.broadcast_to(part, (1, 128)), i == 0)

    return pl.pallas_call(
        body, name=name, grid=(l // tm,),
        in_specs=[pl.BlockSpec((tm, d), lambda i: (i, 0)), pl.BlockSpec((1, d), lambda i: (0, 0)),
                  pl.BlockSpec((tm, d), lambda i: (i, 0))],
        out_specs=[pl.BlockSpec((1, 128), lambda i: (0, 0)), pl.BlockSpec((tm, d), lambda i: (i, 0)),
                   pl.BlockSpec((1, d), lambda i: (0, 0))],
        out_shape=[jax.ShapeDtypeStruct((1, 128), F32), jax.ShapeDtypeStruct((l, d), F32),
                   jax.ShapeDtypeStruct((1, d), F32)],
        compiler_params=_cparams(1),
    )(x, g, tgt)


def _ffn_fwd(x, g, wgu, wd, tag):
    h = _rmsnorm(x, g, name=f"{tag}_norm")

    def act(acc):
        gate, up = acc
        return acc, gate * _sigmoid(gate) * up

    gu, a = _mm(h, wgu, sb=True, so=True, tm=512, tn=1408, tk=1024, name=f"{tag}_up",
                outs=[(BF16, "stile"), (BF16, "tile")], epilogue=act)
    xo = _mm(a, wd, tm=512, tn=1024, tk=2816, name=f"{tag}_down", extras=[(x, "tile")],
             epilogue=lambda acc, xt: (xt + 0.5 * acc,))
    return xo, (x, h, gu, a)


def _ffn_bwd(dout, saved, g, wgu, wd, tag):
    x, h, gu, a = saved

    def act_bwd(acc, guv):
        da = 0.5 * acc
        gate, up = guv[0].astype(F32), guv[1].astype(F32)
        s = _sigmoid(gate)
        return ((da * up * s * (1.0 + gate * (1.0 - s)), da * gate * s),)

    dgu = _mm(dout, wd, tb=True, pair=2, tm=512, tn=1408, tk=1024, col_chunk=384, name=f"{tag}_dact", extras=[(gu, "stile")],
              outs=[(BF16, "stile")], epilogue=act_bwd)
    dwd = _mm(a, dout, ta=True, tm=1408, tn=1024, tk=2048, name=f"{tag}_dwd", out_dtype=BF16,
              epilogue=lambda acc: (0.5 * acc,))
    dwgu = _mm(h, dgu, ta=True, sb=True, so=True, tm=512, tn=1408, tk=2048, name=f"{tag}_dwgu", out_dtype=BF16)
    dx, dg = _mm(dgu, wgu, tb=True, sa=True, sb=True, tm=512, tn=1024, tk=2816, name=f"{tag}_dx",
                 extras=[(x, "tile"), (g, "row"), (dout, "tile")], outs=[(F32, "tile"), (F32, "colsum")],
                 epilogue=_norm_bwd_epilogue)
    return dx, dg, dwgu, dwd


def _sb_plan(n, sizes):
    digits = [n // sizes[0]] + [(n // s) % 2 for s in sizes[1:]]
    plan, none_smaller = [], 1
    for size, d in reversed(list(zip(sizes, digits))):
        has = jnp.minimum(d, 1)
        with_diag = none_smaller * has
        plan.append((size, True, with_diag))
        if size > 1:
            plan.append((size, False, d - with_diag))
        none_smaller = none_smaller * (1 - has)
    return plan


def _sb_sweep(plan, start, step, fn, carry):
    pos = start
    for size, with_diag, trips in plan:
        diag = 0 if step < 0 else size - 1

        def trip(it, cr, size=size, with_diag=with_diag, pos=pos, diag=diag):
            base = pos + step * size * it
            return fn([base + step * b for b in range(size)], cr, [with_diag and b == diag for b in range(size)])

        carry = lax.fori_loop(0, trips, trip, carry)
        pos = pos + step * size * trips
    return carry


def _sb_logs(z):
    lb = jnp.minimum(z, 0.0) - jnp.log(1.0 + jnp.exp(-jnp.abs(z)))
    return lb, lb - z


class _Hosted:
    def __init__(self, steps, xs, out_shapes, copies):
        self.steps, self.xs, self.n, self.copies = steps, list(xs), len(xs), copies
        self.out_shape = [jax.ShapeDtypeStruct(s, x.dtype) for s, x in zip(out_shapes, xs)]
        self.specs = [_HBM] * self.n
        self.sems = [pltpu.SemaphoreType.DMA((self.n, copies)), pltpu.SemaphoreType.DMA((self.n, copies)),
                     pltpu.SemaphoreType.DMA((self.n,))]

    def run(self, x_refs, o_refs, sems, grid):
        ids = [pl.program_id(a) for a in range(len(grid))]
        first = functools.reduce(jnp.logical_and, [p == 0 for p in ids])
        last = functools.reduce(jnp.logical_and, [p == g - 1 for p, g in zip(ids, grid)])
        start, finish = self.steps(x_refs, o_refs, *sems)
        pl.when(first)(start)
        return lambda: pl.when(last)(finish)


def _head_masks(hs):
    lane = lax.broadcasted_iota(jnp.int32, (1, hs * SB_HEAD_DIM), 1)
    return [jnp.logical_and(lane >= hh * SB_HEAD_DIM, lane < (hh + 1) * SB_HEAD_DIM) for hh in range(hs)]


def _sb_fwd(qkv, *, name, hosted=None):
    l = qkv.shape[0]
    d_model = qkv.shape[1] // 3
    dh = SB_HEAD_DIM
    t = _tile(l, SB_TILE)
    hs = 2 * SB_HEADS_PER_STEP
    w = hs * dh
    n_grp = d_model // w
    scale = dh ** -0.5
    grid = (n_grp, l // t)
    nh = hosted.n if hosted else 0

    def body(q_ref, k_ref, v_ref, *rest):
        o_ref = rest[nh]
        at_end = hosted.run(rest[:nh], rest[nh + 1:2 * nh + 1], rest[2 * nh + 1:], grid) if hosted else None
        i = pl.program_id(1)
        heads = _head_masks(hs)
        q_all = (q_ref[...].astype(F32) * scale).astype(BF16)
        qs = [jnp.where(heads[hh], q_all, jnp.zeros_like(q_all)) for hh in range(hs)]
        row = lax.broadcasted_iota(jnp.int32, (t, t), 0)
        col = lax.broadcasted_iota(jnp.int32, (t, t), 1)
        strict = col < row
        tri = _cumsum_operand(strict.astype(BF16))

        def block(jbs, carry, masks):
            sls = [pl.ds(pl.multiple_of(jb * t, t), t) for jb in jbs]
            chains = [(hh, b) for b in range(len(jbs)) for hh in range(hs)]
            ks = [k_ref[sl, :] for sl in sls]
            zs = {(hh, b): _dot(qs[hh], ks[b], NT) for hh, b in chains}
            lbs, tails, sums = {}, {}, {}
            for hh, b in chains:
                lb, lk = _sb_logs(zs[hh, b])
                if masks[b]:
                    lk = jnp.where(strict, lk, 0.0)
                lbs[hh, b] = lb
                tails[hh, b], sums[hh, b] = _cumsum_rowsum(lk, tri)
            cs, o = list(carry[0]), carry[1]
            for b in range(len(jbs)):
                atts = []
                for hh in range(hs):
                    att = jnp.exp(lbs[hh, b] + tails[hh, b] + _across_lanes(cs[hh], t))
                    if masks[b]:
                        att = jnp.where(strict, att, 0.0)
                    atts.append(att.astype(BF16))
                    cs[hh] = cs[hh] + sums[hh, b]
                vb = v_ref[sls[b], :]
                v_heads = jnp.concatenate([jnp.where(heads[hh], vb, jnp.zeros_like(vb)) for hh in range(hs)], axis=0)
                o = o + _dot(jnp.concatenate(atts, axis=1), v_heads)
            return tuple(cs), o

        carry = (tuple(jnp.zeros((t, ROW_SUM_LANES), F32) for _ in range(hs)), jnp.zeros((t, w), F32))
        carry = _sb_sweep(_sb_plan(i + 1, SB_FWD_GROUPS), i, -1, block, carry)
        o_ref[...] = carry[1].astype(o_ref.dtype)
        if hosted:
            at_end()

    blocks = d_model // w
    res = pl.pallas_call(
        body, name=name, grid=grid,
        in_specs=[pl.BlockSpec((t, w), lambda g, i: (i, g)), pl.BlockSpec((l, w), lambda g, i: (0, blocks + g)),
                  pl.BlockSpec((l, w), lambda g, i: (0, 2 * blocks + g))] + (hosted.specs if hosted else []),
        out_specs=[pl.BlockSpec((t, w), lambda g, i: (i, g))] + (hosted.specs if hosted else []),
        out_shape=[jax.ShapeDtypeStruct((l, d_model), BF16)] + (hosted.out_shape if hosted else []),
        scratch_shapes=hosted.sems if hosted else [], compiler_params=_cparams(2),
    )(qkv, qkv, qkv, *(hosted.xs if hosted else []))
    return (res[0], res[1:]) if hosted else res[0]


def _sb_bwd(qkv, do, *, name, hosted=None):
    l = qkv.shape[0]
    d_model = qkv.shape[1] // 3
    dh = SB_HEAD_DIM
    t = _tile(l, SB_TILE)
    nq = l // t
    hs = SB_HEADS_PER_STEP
    w = hs * dh
    blocks = d_model // w
    scale = dh ** -0.5
    grid = (blocks, nq)
    nh = hosted.n if hosted else 0

    def body(q_ref, k_ref, v_ref, do_ref, *rest):
        dq_ref, dk_ref, dv_ref = rest[nh:nh + 3]
        e_scr, s_scr = rest[2 * nh + 3:2 * nh + 5]
        at_end = hosted.run(rest[:nh], rest[nh + 3:2 * nh + 3], rest[2 * nh + 5:], grid) if hosted else None
        i = pl.program_id(1)

        @pl.when(i == 0)
        def _():
            dk_ref[...] = jnp.zeros_like(dk_ref)
            dv_ref[...] = jnp.zeros_like(dv_ref)

        heads = _head_masks(hs)
        q_all = (q_ref[...].astype(F32) * scale).astype(BF16)
        do_all = do_ref[...]
        qs = [jnp.where(heads[hh], q_all, jnp.zeros_like(q_all)) for hh in range(hs)]
        dos = [jnp.where(heads[hh], do_all, jnp.zeros_like(do_all)) for hh in range(hs)]
        row = lax.broadcasted_iota(jnp.int32, (t, t), 0)
        col = lax.broadcasted_iota(jnp.int32, (t, t), 1)
        strict = col < row
        tri_suffix = _cumsum_operand(strict.astype(BF16))
        tri_prefix = _cumsum_operand((row < col).astype(BF16))

        def sweep1(jbs, cs, masks):
            sls = [pl.ds(pl.multiple_of(jb * t, t), t) for jb in jbs]
            chains = [(hh, b) for b in range(len(jbs)) for hh in range(hs)]
            zs = {(hh, b): _dot(qs[hh], k_ref[sls[b], :], NT) for hh, b in chains}
            datts = {(hh, b): _dot(dos[hh], v_ref[sls[b], :], NT) for hh, b in chains}
            lbs, tails, sums = {}, {}, {}
            for hh, b in chains:
                lb, lk = _sb_logs(zs[hh, b])
                if masks[b]:
                    lk = jnp.where(strict, lk, 0.0)
                lbs[hh, b] = lb
                tails[hh, b], sums[hh, b] = _cumsum_rowsum(lk, tri_suffix)
                s_scr[hh, jbs[b]] = jnp.exp(lb)
            cs = list(cs)
            for hh, b in chains:
                att = jnp.exp(lbs[hh, b] + tails[hh, b] + _across_lanes(cs[hh], t))
                if masks[b]:
                    att = jnp.where(strict, att, 0.0)
                e_scr[hh, jbs[b]] = att * datts[hh, b]
                dv_ref[sls[b], :] += _dot(att.astype(BF16), dos[hh], TN)
                cs[hh] = cs[hh] + sums[hh, b]
            return tuple(cs)

        plan = _sb_plan(i + 1, SB_BWD_GROUPS)
        _sb_sweep(plan, i, -1, sweep1, tuple(jnp.zeros((t, ROW_SUM_LANES), F32) for _ in range(hs)))

        def sweep2(jbs, carry, masks):
            sls = [pl.ds(pl.multiple_of(jb * t, t), t) for jb in jbs]
            chains = [(hh, b) for b in range(len(jbs)) for hh in range(hs)]
            des = {(hh, b): e_scr[hh, jbs[b]] for hh, b in chains}
            pres = {(hh, b): _cumsum_rowsum(des[hh, b], tri_prefix) for hh, b in chains}
            carry = [list(c) for c in carry]
            for hh, b in chains:
                p, dq = carry[hh]
                de, sg = des[hh, b], s_scr[hh, jbs[b]]
                dlk = _across_lanes(p, t) + pres[hh, b][0]
                if masks[b]:
                    dlk = jnp.where(strict, dlk, 0.0)
                dz = (de - sg * (de + dlk)).astype(BF16)
                dk_ref[sls[b], :] += _dot(dz, qs[hh], TN)
                carry[hh] = [p + pres[hh, b][1], dq + _dot(dz, k_ref[sls[b], :])]
            return tuple(tuple(c) for c in carry)

        carry = tuple((jnp.zeros((t, ROW_SUM_LANES), F32), jnp.zeros((t, w), F32)) for _ in range(hs))
        carry = _sb_sweep(plan[::-1], 0, 1, sweep2, carry)
        dq = jnp.zeros((t, w), F32)
        for hh in range(hs):
            dq = jnp.where(heads[hh], carry[hh][1], dq)
        dq_ref[...] = (dq * scale).astype(dq_ref.dtype)
        if hosted:
            at_end()

    qspec = pl.BlockSpec((t, w), lambda g, i: (i, g))
    cols = lambda off: pl.BlockSpec((l, w), lambda g, i: (0, off + g))
    res = pl.pallas_call(
        body, name=name, grid=grid,
        in_specs=[qspec, cols(blocks), cols(2 * blocks), qspec] + (hosted.specs if hosted else []),
        out_specs=[qspec, cols(0), cols(0)] + (hosted.specs if hosted else []),
        out_shape=[jax.ShapeDtypeStruct((l, d_model), BF16), jax.ShapeDtypeStruct((l, d_model), F32),
                   jax.ShapeDtypeStruct((l, d_model), F32)] + (hosted.out_shape if hosted else []),
        scratch_shapes=[pltpu.VMEM((hs, nq, t, t), F32), pltpu.VMEM((hs, nq, t, t), F32)]
        + (hosted.sems if hosted else []),
        compiler_params=_cparams(2),
    )(qkv, qkv, qkv, do, *(hosted.xs if hosted else []))
    return (res[0], res[1], res[2], res[3:]) if hosted else res


def _sb_layer_fwd(x, g, wqkv, wo, tag, hosted=None):
    h = _rmsnorm(x, g, name=f"{tag}_norm")
    qkv = _mm(h, wqkv, tm=1024, tn=1024, tk=1024, name=f"{tag}_qkv", out_dtype=BF16)
    o = _sb_fwd(qkv, name=f"{tag}_attn", hosted=hosted)
    carried = None
    if hosted:
        o, carried = o
    xo = _mm(o, wo, tm=1024, tn=1024, tk=1024, name=f"{tag}_out", extras=[(x, "tile")],
             epilogue=lambda acc, xt: (xt + acc,))
    return xo, (x, h, qkv, o), carried


def _sb_layer_bwd(dout, saved, g, wqkv, wo, tag, hosted=None):
    x, h, qkv, o = saved
    do = _mm(dout, wo, tb=True, tm=1024, tn=1024, tk=1024, name=f"{tag}_do", out_dtype=BF16)
    dwo = _mm(o, dout, ta=True, tm=1024, tn=1024, tk=2048, name=f"{tag}_dwo", out_dtype=BF16)
    res = _sb_bwd(qkv, do, name=f"{tag}_attn_bwd", hosted=hosted)
    dq, dk, dv = res[:3]
    carried = res[3] if hosted else None
    dqkv = jnp.concatenate([dq, dk.astype(BF16), dv.astype(BF16)], axis=1)
    dwqkv = _mm(h, dqkv, ta=True, tm=1024, tn=1024, tk=2048, name=f"{tag}_dwqkv", out_dtype=BF16)
    dx, dg = _mm(dqkv, wqkv, tb=True, tm=512, tn=1024, tk=3072, name=f"{tag}_dx",
                 extras=[(x, "tile"), (g, "row"), (dout, "tile")], outs=[(F32, "tile"), (F32, "colsum")],
                 epilogue=_norm_bwd_epilogue)
    return dx, dg, dwqkv, dwo, carried


def _shift_down(x, s, t_idx):
    return jnp.where(t_idx >= s, pltpu.roll(x, s, 0), 0.0)


def _shift_up(x, s, t_idx):
    n = x.shape[0]
    return jnp.where(t_idx < n - s, pltpu.roll(x, n - s, 0), 0.0)


def _sc_fwd(p, cw, *, name):
    l = p.shape[0]
    d = cw.shape[1]
    tc = 128
    nb = d // tc

    def body(b_ref, c_ref, h_ref, w_ref, o_ref):
        v = c_ref[...] * h_ref[...]
        t_idx = lax.broadcasted_iota(jnp.int32, v.shape, 0)
        u = v * w_ref[2:3, :] + _shift_down(v, 1, t_idx) * w_ref[1:2, :] + _shift_down(v, 2, t_idx) * w_ref[0:1, :]
        o_ref[...] = (b_ref[...] * u).astype(BF16)

    return pl.pallas_call(
        body, name=name, grid=(nb,),
        in_specs=[pl.BlockSpec((l, tc), lambda j: (0, j)), pl.BlockSpec((l, tc), lambda j: (0, nb + j)),
                  pl.BlockSpec((l, tc), lambda j: (0, 2 * nb + j)), pl.BlockSpec((3, tc), lambda j: (0, j))],
        out_specs=pl.BlockSpec((l, tc), lambda j: (0, j)),
        out_shape=jax.ShapeDtypeStruct((l, d), BF16), compiler_params=_cparams(1),
    )(p, p, p, cw)


def _sc_bwd(p, cw, dbu, *, name):
    l = p.shape[0]
    d = cw.shape[1]
    tc = 128
    nb = d // tc

    def body(b_ref, c_ref, h_ref, w_ref, g_ref, db_ref, dc_ref, dh_ref, dw_ref):
        cv, hv = c_ref[...], h_ref[...]
        v = cv * hv
        t_idx = lax.broadcasted_iota(jnp.int32, v.shape, 0)
        v1, v2 = _shift_down(v, 1, t_idx), _shift_down(v, 2, t_idx)
        u = v * w_ref[2:3, :] + v1 * w_ref[1:2, :] + v2 * w_ref[0:1, :]
        dbu_v = g_ref[...]
        db_ref[...] = (dbu_v * u).astype(BF16)
        du = dbu_v * b_ref[...]
        dv = du * w_ref[2:3, :] + _shift_up(du, 1, t_idx) * w_ref[1:2, :] + _shift_up(du, 2, t_idx) * w_ref[0:1, :]
        dc_ref[...] = (dv * hv).astype(BF16)
        dh_ref[...] = (dv * cv).astype(BF16)
        dw_ref[...] = jnp.zeros_like(dw_ref)
        dw_ref[0:1, :] = jnp.sum(du * v2, axis=0, keepdims=True)
        dw_ref[1:2, :] = jnp.sum(du * v1, axis=0, keepdims=True)
        dw_ref[2:3, :] = jnp.sum(du * v, axis=0, keepdims=True)

    col = lambda off: pl.BlockSpec((l, tc), lambda j: (0, off + j))
    return pl.pallas_call(
        body, name=name, grid=(nb,),
        in_specs=[col(0), col(nb), col(2 * nb), pl.BlockSpec((3, tc), lambda j: (0, j)), col(0)],
        out_specs=[col(0), col(0), col(0), pl.BlockSpec((8, tc), lambda j: (0, j))],
        out_shape=[jax.ShapeDtypeStruct((l, d), BF16)] * 3 + [jax.ShapeDtypeStruct((8, d), F32)],
        compiler_params=_cparams(1),
    )(p, p, p, cw, dbu)


def _sc_layer_fwd(x, g, win, cw, wout, tag):
    h = _rmsnorm(x, g, name=f"{tag}_norm")
    p = _mm(h, win, tm=1024, tn=1024, tk=1024, name=f"{tag}_in")
    bu = _sc_fwd(p, cw, name=f"{tag}_conv")
    xo = _mm(bu, wout, tm=1024, tn=1024, tk=1024, name=f"{tag}_out", extras=[(x, "tile")],
             epilogue=lambda acc, xt: (xt + acc,))
    return xo, (x, h, p, bu)


def _sc_layer_bwd(dout, saved, g, win, cw, wout, tag):
    x, h, p, bu = saved
    dbu = _mm(dout, wout, tb=True, tm=1024, tn=1024, tk=1024, name=f"{tag}_dbu")
    dwout = _mm(bu, dout, ta=True, tm=1024, tn=1024, tk=2048, name=f"{tag}_dwout", out_dtype=BF16)
    db, dc, dh, dcw = _sc_bwd(p, cw, dbu, name=f"{tag}_conv_bwd")
    dp = jnp.concatenate([db, dc, dh], axis=1)
    dwin = _mm(h, dp, ta=True, tm=1024, tn=1024, tk=2048, name=f"{tag}_dwin", out_dtype=BF16)
    dx, dg = _mm(dp, win, tb=True, tm=512, tn=1024, tk=3072, name=f"{tag}_dx",
                 extras=[(x, "tile"), (g, "row"), (dout, "tile")], outs=[(F32, "tile"), (F32, "colsum")],
                 epilogue=_norm_bwd_epilogue)
    return dx, dg, dwin, dcw[:3], dwout


def _ssd_conv_fwd(p, cw, cb, *, name):
    l = p.shape[0]
    tc = 128
    nb = SSD_CONV_DIM // tc
    off = SSD_D_INNER // tc

    def body(x_ref, w_ref, b_ref, o_ref):
        xv = x_ref[...]
        t_idx = lax.broadcasted_iota(jnp.int32, xv.shape, 0)
        pre = xv * w_ref[3:4, :] + b_ref[...]
        for s in (1, 2, 3):
            pre = pre + _shift_down(xv, s, t_idx) * w_ref[3 - s:4 - s, :]
        o_ref[...] = pre * _sigmoid(pre)

    return pl.pallas_call(
        body, name=name, grid=(nb,),
        in_specs=[pl.BlockSpec((l, tc), lambda j: (0, off + j)), pl.BlockSpec((4, tc), lambda j: (0, j)),
                  pl.BlockSpec((1, tc), lambda j: (0, j))],
        out_specs=pl.BlockSpec((l, tc), lambda j: (0, j)),
        out_shape=jax.ShapeDtypeStruct((l, SSD_CONV_DIM), F32), compiler_params=_cparams(1),
    )(p, cw, cb)


def _ssd_conv_bwd(p, cw, cb, dact, *, name):
    l = p.shape[0]
    tc = 128
    nb = SSD_CONV_DIM // tc
    off = SSD_D_INNER // tc

    def body(x_ref, w_ref, b_ref, g_ref, dx_ref, dw_ref):
        xv = x_ref[...]
        t_idx = lax.broadcasted_iota(jnp.int32, xv.shape, 0)
        xs = [xv] + [_shift_down(xv, s, t_idx) for s in (1, 2, 3)]
        pre = b_ref[...] + xs[0] * w_ref[3:4, :]
        for s in (1, 2, 3):
            pre = pre + xs[s] * w_ref[3 - s:4 - s, :]
        sg = _sigmoid(pre)
        dpre = g_ref[...] * sg * (1.0 + pre * (1.0 - sg))
        dx = dpre * w_ref[3:4, :]
        for s in (1, 2, 3):
            dx = dx + _shift_up(dpre, s, t_idx) * w_ref[3 - s:4 - s, :]
        dx_ref[...] = dx
        dw_ref[...] = jnp.zeros_like(dw_ref)
        for s in (0, 1, 2, 3):
            dw_ref[3 - s:4 - s, :] = jnp.sum(dpre * xs[s], axis=0, keepdims=True)
        dw_ref[4:5, :] = jnp.sum(dpre, axis=0, keepdims=True)

    return pl.pallas_call(
        body, name=name, grid=(nb,),
        in_specs=[pl.BlockSpec((l, tc), lambda j: (0, off + j)), pl.BlockSpec((4, tc), lambda j: (0, j)),
                  pl.BlockSpec((1, tc), lambda j: (0, j)), pl.BlockSpec((l, tc), lambda j: (0, j))],
        out_specs=[pl.BlockSpec((l, tc), lambda j: (0, j)), pl.BlockSpec((8, tc), lambda j: (0, j))],
        out_shape=[jax.ShapeDtypeStruct((l, SSD_CONV_DIM), F32), jax.ShapeDtypeStruct((8, SSD_CONV_DIM), F32)],
        compiler_params=_cparams(1),
    )(p, cw, cb, dact)


def _ssd_dt_fwd(p, bias, *, name):
    l = p.shape[0]
    tm = _tile(l, 1024)
    off = (SSD_D_INNER + SSD_CONV_DIM) // 128

    def body(x_ref, b_ref, o_ref):
        v = x_ref[...] + b_ref[...]
        o_ref[...] = jnp.maximum(v, 0.0) + jnp.log(1.0 + jnp.exp(-jnp.abs(v)))

    return pl.pallas_call(
        body, name=name, grid=(l // tm,),
        in_specs=[pl.BlockSpec((tm, 128), lambda i: (i, off)), pl.BlockSpec((1, 128), lambda i: (0, 0))],
        out_specs=pl.BlockSpec((tm, 128), lambda i: (i, 0)),
        out_shape=jax.ShapeDtypeStruct((l, 128), F32), compiler_params=_cparams(1),
    )(p, bias)


def _ssd_dt_bwd(p, bias, ddt, *, name):
    l = p.shape[0]
    tm = _tile(l, 1024)
    off = (SSD_D_INNER + SSD_CONV_DIM) // 128

    def body(x_ref, b_ref, g_ref, o_ref, db_ref):
        i = pl.program_id(0)
        d = g_ref[...] * _sigmoid(x_ref[...] + b_ref[...])
        o_ref[...] = d
        _accumulate(db_ref, jnp.sum(d, axis=0, keepdims=True), i == 0)

    return pl.pallas_call(
        body, name=name, grid=(l // tm,),
        in_specs=[pl.BlockSpec((tm, 128), lambda i: (i, off)), pl.BlockSpec((1, 128), lambda i: (0, 0)),
                  pl.BlockSpec((tm, 128), lambda i: (i, 0))],
        out_specs=[pl.BlockSpec((tm, 128), lambda i: (i, 0)), pl.BlockSpec((1, 128), lambda i: (0, 0))],
        out_shape=[jax.ShapeDtypeStruct((l, 128), F32), jax.ShapeDtypeStruct((1, 128), F32)],
        compiler_params=_cparams(1),
    )(p, bias, ddt)


def _row_to_col(r, eye):
    return jnp.sum(jnp.where(eye, r, 0.0), axis=1, keepdims=True)


def _col_to_row(c, eye):
    return jnp.sum(jnp.where(eye, c, 0.0), axis=0, keepdims=True)


def _ssd_chunk_common(b_ref, c_ref, dt_ref, a_ref, lam_scr):
    n = SSD_CHUNK
    row = lax.broadcasted_iota(jnp.int32, (n, n), 0)
    col = lax.broadcasted_iota(jnp.int32, (n, n), 1)
    bm, cm = b_ref[...].astype(BF16), c_ref[...].astype(BF16)
    g = _dot(cm, bm, NT)
    incl = (row <= col).astype(BF16)
    lam_scr[...] = _dot_exact(dt_ref[...] * a_ref[...], incl)
    return row, col, bm, cm, g


def _ssd_head_common(r, row, col, dt_ref, lam_scr):
    eye, tril = row == col, row >= col
    lam_r = lam_scr[r:r + 1, :]
    dt_r = dt_ref[r:r + 1, :]
    lam_c = _row_to_col(lam_r, eye)
    dt_c = _row_to_col(dt_r, eye)
    dk = jnp.where(tril, jnp.exp(jnp.minimum(lam_c - lam_r, 0.0)), 0.0)
    lam_last = jnp.sum(jnp.where(col[0:1, :] == SSD_CHUNK - 1, lam_r, 0.0), axis=1, keepdims=True)
    return eye, lam_r, dt_r, lam_c, dt_c, dk, lam_last


def _ssd_fwd(xh, act, dt_t, a_b, *, name, hosted=None):
    l = xh.shape[1]
    nc = l // SSD_CHUNK
    n, p_dim, hpg = SSD_CHUNK, SSD_HEAD_DIM, SSD_HPG

    gps = SSD_GROUPS_PER_STEP
    n_grp = SSD_GROUPS // gps
    grid = (n_grp, nc)
    nh = hosted.n if hosted else 0

    def body(x_ref, b_ref, c_ref, dt_ref, a_ref, *rest):
        y_ref, hp_ref = rest[nh:nh + 2]
        h_scr, lam_scr = rest[2 * nh + 2:2 * nh + 4]
        at_end = hosted.run(rest[:nh], rest[nh + 2:2 * nh + 2], rest[2 * nh + 4:], grid) if hosted else None

        @pl.when(pl.program_id(1) == 0)
        def _():
            h_scr[...] = jnp.zeros_like(h_scr)

        lanes = [pl.ds(gg * SSD_STATE, SSD_STATE) for gg in range(gps)]
        common = [_ssd_chunk_common(b_ref.at[:, lanes[gg]], c_ref.at[:, lanes[gg]], dt_ref.at[gg], a_ref.at[gg],
                                    lam_scr.at[gg]) for gg in range(gps)]
        for gg in range(gps):
            row, col, bm, cm, g = common[gg]
            for r in range(hpg):
                hd = gg * hpg + r
                _, _, dt_r, lam_c, dt_c, dk, lam_last = _ssd_head_common(r, row, col, dt_ref.at[gg], lam_scr.at[gg])
                xr = x_ref[hd]
                hr = h_scr[hd]
                w = (g * dk * dt_r).astype(BF16)
                y = _dot(w, xr.astype(BF16)) + _dot(cm, hr.astype(BF16), NT) * jnp.exp(lam_c)
                y_ref[hd] = y
                hp_ref[hd] = hr
                xw = (xr * (jnp.exp(lam_last - lam_c) * dt_c)).astype(BF16)
                h_scr[hd] = jnp.exp(lam_last) * hr + _dot(xw, bm, TN)
        if hosted:
            at_end()

    g_off = SSD_D_INNER // (gps * SSD_STATE)
    res = pl.pallas_call(
        body, name=name, grid=grid,
        in_specs=[pl.BlockSpec((gps * hpg, n, p_dim), lambda g, c: (g, c, 0)),
                  pl.BlockSpec((n, gps * SSD_STATE), lambda g, c: (c, g_off + g)),
                  pl.BlockSpec((n, gps * SSD_STATE), lambda g, c: (c, g_off + n_grp + g)),
                  pl.BlockSpec((gps, 8, n), lambda g, c: (g, 0, c)),
                  pl.BlockSpec((gps, 8, 128), lambda g, c: (g, 0, 0))] + (hosted.specs if hosted else []),
        out_specs=[pl.BlockSpec((gps * hpg, n, p_dim), lambda g, c: (g, c, 0)),
                   pl.BlockSpec((None, gps * hpg, p_dim, SSD_STATE), lambda g, c: (c, g, 0, 0))]
        + (hosted.specs if hosted else []),
        out_shape=[jax.ShapeDtypeStruct(xh.shape, F32),
                   jax.ShapeDtypeStruct((nc, SSD_HEADS, p_dim, SSD_STATE), F32)] + (hosted.out_shape if hosted else []),
        scratch_shapes=[pltpu.VMEM((gps * hpg, p_dim, SSD_STATE), F32), pltpu.VMEM((gps, 8, n), F32)]
        + (hosted.sems if hosted else []),
        compiler_params=_cparams(2),
    )(xh, act, act, dt_t, a_b, *(hosted.xs if hosted else []))
    return (res[0], res[1], res[2:]) if hosted else res


def _ssd_bwd(xh, act, dt_t, a_b, hprev, dyh, *, name):
    l = xh.shape[1]
    nc = l // SSD_CHUNK
    n, p_dim, hpg = SSD_CHUNK, SSD_HEAD_DIM, SSD_HPG
    gps = SSD_GROUPS_PER_STEP

    def body(x_ref, b_ref, c_ref, dt_ref, a_ref, hp_ref, dy_ref,
             dx_ref, db_ref, dc_ref, ddt_ref, da_ref, dh_scr, lam_scr, dlam_scr, ddt_scr):
        ci = pl.program_id(1)

        @pl.when(ci == 0)
        def _():
            dh_scr[...] = jnp.zeros_like(dh_scr)

        lanes = [pl.ds(gg * SSD_STATE, SSD_STATE) for gg in range(gps)]
        common = [_ssd_chunk_common(b_ref.at[:, lanes[gg]], c_ref.at[:, lanes[gg]], dt_ref.at[gg], a_ref.at[gg],
                                    lam_scr.at[gg]) for gg in range(gps)]
        dlam_scr[...] = jnp.zeros_like(dlam_scr)
        ddt_scr[...] = jnp.zeros_like(ddt_scr)
        for gg in range(gps):
            row, col, bm, cm, g = common[gg]
            dt_g, lam_g, dlam_g, ddt_g = dt_ref.at[gg], lam_scr.at[gg], dlam_scr.at[gg], ddt_scr.at[gg]
            dg_acc = jnp.zeros((n, n), F32)
            dc_acc = jnp.zeros((n, SSD_STATE), F32)
            db_acc = jnp.zeros((n, SSD_STATE), F32)
            for r in range(hpg):
                hd = gg * hpg + r
                eye, _, dt_r, lam_c, dt_c, dk, lam_last = _ssd_head_common(r, row, col, dt_g, lam_g)
                xr, dyr, hr, dhr = x_ref[hd], dy_ref[hd], hp_ref[hd], dh_scr[hd]
                xb, dyb, hb, dhb = xr.astype(BF16), dyr.astype(BF16), hr.astype(BF16), dhr.astype(BF16)
                e_l = jnp.exp(lam_c)
                e_last = jnp.exp(lam_last)
                decay_c = jnp.exp(lam_last - lam_c)
                w_c = decay_c * dt_c
                m = g * dk * dt_r
                dm = _dot(dyb, xb, NT)
                bdh = _dot(bm, dhb, NT)
                dx_ref[hd] = _dot(m.astype(BF16), dyb, TN) + w_c * bdh
                dg_acc = dg_acc + dm * dk * dt_r
                q_mat = dm * g * dk
                p_mat = q_mat * dt_r
                yoff = _dot(cm, hb, NT) * e_l
                q_c = jnp.sum(xr * bdh, axis=1, keepdims=True)
                dlam_c = (jnp.sum(p_mat, axis=1, keepdims=True) + jnp.sum(dyr * yoff, axis=1, keepdims=True)
                          - w_c * q_c)
                d_last = (jnp.sum(w_c * q_c, axis=0, keepdims=True)
                          + e_last * jnp.sum(jnp.sum(dhr * hr, axis=1, keepdims=True), axis=0, keepdims=True))
                dlam_g[r:r + 1, :] = (_col_to_row(dlam_c, eye) - jnp.sum(p_mat, axis=0, keepdims=True)
                                      + jnp.where(col[0:1, :] == n - 1, d_last, 0.0))
                ddt_g[r:r + 1, :] = jnp.sum(q_mat, axis=0, keepdims=True) + _col_to_row(decay_c * q_c, eye)
                dc_acc = dc_acc + e_l * _dot(dyb, hb)
                db_acc = db_acc + _dot((xr * w_c).astype(BF16), dhb)
                dh_scr[hd] = e_last * dhr + _dot((dyr * e_l).astype(BF16), cm, TN)

            dgb = dg_acc.astype(BF16)
            dc_ref[:, lanes[gg]] = _dot(dgb, bm) + dc_acc
            db_ref[:, lanes[gg]] = _dot(dgb, cm, TN) + db_acc
            rev = (row >= col).astype(BF16)
            da = _dot_exact(dlam_g[...], rev)
            ddt_ref[gg] = ddt_g[...] + da * a_ref[gg]
            _accumulate(da_ref.at[gg], da * dt_g[...], ci == 0)

        @pl.when(ci == nc - 1)
        def _():
            for gg in range(gps):
                da_ref[gg] = jnp.broadcast_to(jnp.sum(da_ref[gg], axis=1, keepdims=True), da_ref.shape[1:])

    g_off = SSD_D_INNER // (gps * SSD_STATE)
    n_grp = SSD_GROUPS // gps
    rc = lambda c: nc - 1 - c
    hspec = pl.BlockSpec((gps * hpg, n, p_dim), lambda g, c: (g, rc(c), 0))
    gspec = pl.BlockSpec((n, gps * SSD_STATE), lambda g, c: (rc(c), g))
    return pl.pallas_call(
        body, name=name, grid=(n_grp, nc),
        in_specs=[hspec,
                  pl.BlockSpec((n, gps * SSD_STATE), lambda g, c: (rc(c), g_off + g)),
                  pl.BlockSpec((n, gps * SSD_STATE), lambda g, c: (rc(c), g_off + n_grp + g)),
                  pl.BlockSpec((gps, 8, n), lambda g, c: (g, 0, rc(c))),
                  pl.BlockSpec((gps, 8, 128), lambda g, c: (g, 0, 0)),
                  pl.BlockSpec((None, gps * hpg, p_dim, SSD_STATE), lambda g, c: (rc(c), g, 0, 0)),
                  hspec],
        out_specs=[hspec, gspec, gspec,
                   pl.BlockSpec((gps, 8, n), lambda g, c: (g, 0, rc(c))),
                   pl.BlockSpec((gps, 8, 128), lambda g, c: (g, 0, 0))],
        out_shape=[jax.ShapeDtypeStruct(xh.shape, F32),
                   jax.ShapeDtypeStruct((l, SSD_GROUPS * SSD_STATE), F32),
                   jax.ShapeDtypeStruct((l, SSD_GROUPS * SSD_STATE), F32),
                   jax.ShapeDtypeStruct(dt_t.shape, F32),
                   jax.ShapeDtypeStruct(a_b.shape, F32)],
        scratch_shapes=[pltpu.VMEM((gps * hpg, p_dim, SSD_STATE), F32), pltpu.VMEM((gps, 8, n), F32),
                        pltpu.VMEM((gps, 8, n), F32), pltpu.VMEM((gps, 8, n), F32)],
        compiler_params=_cparams(2),
    )(xh, act, act, dt_t, a_b, hprev, dyh)


def _ssd_gate_fwd(y, act, p, d_vec, gn, *, name):
    l = y.shape[0]
    w = SSD_D_INNER
    tm = _tile(l, 256)

    def body(y_ref, xs_ref, z_ref, d_ref, g_ref, o_ref):
        for gi in range(SSD_GROUPS):
            sl = slice(gi * SSD_NORM_GROUP, (gi + 1) * SSD_NORM_GROUP)
            z = z_ref[:, sl]
            y2 = (y_ref[:, sl] + d_ref[:, sl] * xs_ref[:, sl]) * (z * _sigmoid(z))
            r = lax.rsqrt(jnp.mean(y2 * y2, axis=1, keepdims=True) + RMS_EPS)
            o_ref[:, sl] = (y2 * r * g_ref[:, sl]).astype(BF16)

    rows = pl.BlockSpec((tm, w), lambda i: (i, 0))
    vec = pl.BlockSpec((1, w), lambda i: (0, 0))
    return pl.pallas_call(
        body, name=name, grid=(l // tm,), in_specs=[rows, rows, rows, vec, vec], out_specs=rows,
        out_shape=jax.ShapeDtypeStruct((l, w), BF16), compiler_params=_cparams(1),
    )(y, act, p, d_vec, gn)


def _ssd_gate_bwd(dyn, y, act, p, d_vec, gn, *, name):
    l = y.shape[0]
    w = SSD_D_INNER
    tm = _tile(l, 256)

    def body(dyn_ref, y_ref, xs_ref, z_ref, d_ref, g_ref, dy_ref, dz_ref, dxs_ref, dd_ref, dg_ref):
        i = pl.program_id(0)
        for gi in range(SSD_GROUPS):
            sl = slice(gi * SSD_NORM_GROUP, (gi + 1) * SSD_NORM_GROUP)
            z, xs, dv = z_ref[:, sl], xs_ref[:, sl], d_ref[:, sl]
            s = _sigmoid(z)
            sz = z * s
            y1 = y_ref[:, sl] + dv * xs
            y2 = y1 * sz
            r = lax.rsqrt(jnp.mean(y2 * y2, axis=1, keepdims=True) + RMS_EPS)
            y2h = y2 * r
            dyn_v = dyn_ref[:, sl]
            d2h = dyn_v * g_ref[:, sl]
            dy2 = r * (d2h - y2h * jnp.mean(d2h * y2h, axis=1, keepdims=True))
            dy1 = dy2 * sz
            dy_ref[:, sl] = dy1
            dz_ref[:, sl] = dy2 * y1 * s * (1.0 + z * (1.0 - s))
            dxs_ref[:, sl] = dv * dy1
            _accumulate(dd_ref.at[:, sl], jnp.sum(dy1 * xs, axis=0, keepdims=True), i == 0)
            _accumulate(dg_ref.at[:, sl], jnp.sum(dyn_v * y2h, axis=0, keepdims=True), i == 0)

    rows = pl.BlockSpec((tm, w), lambda i: (i, 0))
    vec = pl.BlockSpec((1, w), lambda i: (0, 0))
    return pl.pallas_call(
        body, name=name, grid=(l // tm,), in_specs=[rows, rows, rows, rows, vec, vec],
        out_specs=[rows, rows, rows, vec, vec],
        out_shape=[jax.ShapeDtypeStruct((l, w), F32)] * 3 + [jax.ShapeDtypeStruct((1, w), F32)] * 2,
        compiler_params=_cparams(1),
    )(dyn, y, act, p, d_vec, gn)


def _heads_major(x):
    return x.reshape(x.shape[0], SSD_HEADS, SSD_HEAD_DIM).transpose(1, 0, 2)


def _ssd_layer_fwd(x, g, win, cw, cb, dt_bias, a_log, d_skip, gn, wout, tag, hosted=None):
    l = x.shape[0]
    h = _rmsnorm(x, g, name=f"{tag}_norm")
    p = _mm(h, win, tm=1024, tn=896, tk=1024, name=f"{tag}_in")
    act = _ssd_conv_fwd(p, cw, cb, name=f"{tag}_conv")
    bias = jnp.pad(dt_bias, (0, 128 - SSD_HEADS)).reshape(1, 128)
    dt = _ssd_dt_fwd(p, bias, name=f"{tag}_dt")
    xh = _heads_major(act[:, :SSD_D_INNER])
    dt_t = jnp.pad(dt[:, :SSD_HEADS].T.reshape(SSD_GROUPS, SSD_HPG, l), ((0, 0), (0, 8 - SSD_HPG), (0, 0)))
    a = -jnp.exp(a_log).reshape(SSD_GROUPS, SSD_HPG, 1)
    a_b = jnp.broadcast_to(jnp.pad(a, ((0, 0), (0, 8 - SSD_HPG), (0, 0))), (SSD_GROUPS, 8, 128))
    res = _ssd_fwd(xh, act, dt_t, a_b, name=f"{tag}_scan", hosted=hosted)
    yh, hprev = res[:2]
    carried = res[2] if hosted else None
    y = yh.transpose(1, 0, 2).reshape(l, SSD_D_INNER)
    d_vec = jnp.repeat(d_skip, SSD_HEAD_DIM).reshape(1, SSD_D_INNER)
    yn = _ssd_gate_fwd(y, act, p, d_vec, gn, name=f"{tag}_gate")
    xo = _mm(yn, wout, tm=1024, tn=1024, tk=2048, name=f"{tag}_out", extras=[(x, "tile")],
             epilogue=lambda acc, xt: (xt + acc,))
    return xo, (x, h, p, act, bias, xh, dt_t, a_b, hprev, y, d_vec, yn), carried


def _ssd_layer_bwd(dout, saved, g, win, cw, cb, gn, wout, tag):
    x, h, p, act, bias, xh, dt_t, a_b, hprev, y, d_vec, yn = saved
    l = x.shape[0]
    dyn = _mm(dout, wout, tb=True, tm=1024, tn=1024, tk=1024, name=f"{tag}_dyn")
    dwout = _mm(yn, dout, ta=True, tm=1024, tn=1024, tk=2048, name=f"{tag}_dwout", out_dtype=BF16)
    dy, dz, dxs_d, dd_vec, dgn = _ssd_gate_bwd(dyn, y, act, p, d_vec, gn, name=f"{tag}_gate_bwd")
    dxh, dbm, dcm, ddt_t, da_b = _ssd_bwd(xh, act, dt_t, a_b, hprev, _heads_major(dy), name=f"{tag}_scan_bwd")
    dxs = dxh.transpose(1, 0, 2).reshape(l, SSD_D_INNER) + dxs_d
    dact = jnp.concatenate([dxs, dbm, dcm], axis=1)
    dxbc, dcw8 = _ssd_conv_bwd(p, cw, cb, dact, name=f"{tag}_conv_bwd")
    ddt = jnp.pad(ddt_t[:, :SSD_HPG, :].reshape(SSD_HEADS, l).T, ((0, 0), (0, 128 - SSD_HEADS)))
    ddt_raw, dbias = _ssd_dt_bwd(p, bias, ddt, name=f"{tag}_dt_bwd")
    dp = jnp.concatenate([dz, dxbc, ddt_raw], axis=1)
    dwin = _mm(h, dp, ta=True, tm=1024, tn=896, tk=2048, name=f"{tag}_dwin", out_dtype=BF16)
    dx, dg = _mm(dp, win, tb=True, tm=256, tn=1024, tk=6272, name=f"{tag}_dx",
                 extras=[(x, "tile"), (g, "row"), (dout, "tile")], outs=[(F32, "tile"), (F32, "colsum")],
                 epilogue=_norm_bwd_epilogue)
    a_heads = a_b[:, :SSD_HPG, 0].reshape(SSD_HEADS)
    grads = dict(
        ssd_w_in=dwin[:, :SSD_IN_DIM], ssd_conv_w=dcw8[:4], ssd_conv_b=dcw8[4],
        ssd_dt_bias=dbias[0, :SSD_HEADS], ssd_a_log=da_b[:, :SSD_HPG, 0].reshape(SSD_HEADS) * a_heads,
        ssd_d=dd_vec.reshape(SSD_HEADS, SSD_HEAD_DIM).sum(axis=1), ssd_norm=dgn[0], ssd_w_out=dwout)
    return dx, dg, grads


def _local_step(x, tgt, w, gather_later=None, scatter_early=None):
    row = lambda v: v.reshape(1, -1)
    saved = []
    for i in range(DEPTH):
        kind, j = i % 3, i // 3
        x, s1 = _ffn_fwd(x, row(w["ffn1_norm"][i]), w["ffn1_w_gu"][i], w["ffn1_w_down"][i], f"l{i}f1")
        gm = row(w["mix_norm"][i])
        hook = (gather_later or {}).get(i)
        hosted = hook[0] if hook else None
        if kind == 0:
            x, sm, carried = _sb_layer_fwd(x, gm, w["sb_w_qkv"][j], w["sb_w_o"][j], f"l{i}sb", hosted=hosted)
        elif kind == 1:
            x, sm, carried = _ssd_layer_fwd(x, gm, w["ssd_w_in"][j], w["ssd_conv_w"][j], row(w["ssd_conv_b"][j]),
                                            w["ssd_dt_bias"][j], w["ssd_a_log"][j], w["ssd_d"][j],
                                            row(w["ssd_norm"][j]), w["ssd_w_out"][j], f"l{i}ssd", hosted=hosted)
        if hook:
            w = hook[1](w, carried)
        if kind == 2:
            x, sm = _sc_layer_fwd(x, gm, w["sc_w_in"][j], w["sc_conv_w"][j], w["sc_w_out"][j], f"l{i}sc")
        x, s2 = _ffn_fwd(x, row(w["ffn2_norm"][i]), w["ffn2_w_gu"][i], w["ffn2_w_down"][i], f"l{i}f2")
        saved.append((s1, sm, s2))

    loss, dx, dfinal = _final_loss(x, row(w["final_norm"]), tgt, name="final_loss")
    per_layer = {k: [None] * DEPTH for k in ("ffn1_norm", "ffn1_w_gu", "ffn1_w_down", "mix_norm",
                                             "ffn2_norm", "ffn2_w_gu", "ffn2_w_down")}
    per_layer.update({"sb_w_qkv": [None, None], "sb_w_o": [None, None]})
    grads = {"final_norm": dfinal[0]}
    early = None
    for i in reversed(range(DEPTH)):
        kind, j = i % 3, i // 3
        s1, sm, s2 = saved[i]
        dx, dg, dwgu, dwd = _ffn_bwd(dx, s2, row(w["ffn2_norm"][i]), w["ffn2_w_gu"][i], w["ffn2_w_down"][i], f"l{i}f2")
        per_layer["ffn2_norm"][i], per_layer["ffn2_w_gu"][i], per_layer["ffn2_w_down"][i] = dg[0], dwgu, dwd
        gm = row(w["mix_norm"][i])
        if kind == 0:
            hosted = scatter_early({**grads, **per_layer}) if (scatter_early and i == 0) else None
            dx, dg, dwqkv, dwo, carried = _sb_layer_bwd(dx, sm, gm, w["sb_w_qkv"][j], w["sb_w_o"][j], f"l{i}sb",
                                                        hosted=hosted)
            per_layer["sb_w_qkv"][j], per_layer["sb_w_o"][j] = dwqkv, dwo
            if hosted:
                early = carried
        elif kind == 1:
            dx, dg, sg = _ssd_layer_bwd(dx, sm, gm, w["ssd_w_in"][j], w["ssd_conv_w"][j], row(w["ssd_conv_b"][j]),
                                        row(w["ssd_norm"][j]), w["ssd_w_out"][j], f"l{i}ssd")
            sg["ssd_w_in"], sg["ssd_w_out"] = [sg["ssd_w_in"]], [sg["ssd_w_out"]]
            grads.update({k: (v if isinstance(v, list) else v[None]) for k, v in sg.items()})
        else:
            dx, dg, dwin, dcw, dwout = _sc_layer_bwd(dx, sm, gm, w["sc_w_in"][j], w["sc_conv_w"][j],
                                                     w["sc_w_out"][j], f"l{i}sc")
            grads.update(sc_w_in=[dwin], sc_conv_w=dcw[None], sc_w_out=[dwout])
        per_layer["mix_norm"][i] = dg[0]
        dx, dg, dwgu, dwd = _ffn_bwd(dx, s1, row(w["ffn1_norm"][i]), w["ffn1_w_gu"][i], w["ffn1_w_down"][i], f"l{i}f1")
        per_layer["ffn1_norm"][i], per_layer["ffn1_w_gu"][i], per_layer["ffn1_w_down"][i] = dg[0], dwgu, dwd
    for k, v in per_layer.items():
        grads[k] = jnp.stack(v) if k.endswith("_norm") else v
    return loss, dx, grads, early


_HBM = pl.BlockSpec(memory_space=pltpu.HBM)


def _remote(src, dst, send_sems, recv_sems, idx, dev):
    return pltpu.make_async_remote_copy(src_ref=src, dst_ref=dst, send_sem=send_sems.at[idx], recv_sem=recv_sems.at[idx],
                                        device_id=dev, device_id_type=pl.DeviceIdType.MESH)


def _exchange_call(body, xs, out_shapes, n_copies, name):
    n = len(xs)
    return pl.pallas_call(
        body, name=name, in_specs=[_HBM] * n, out_specs=[_HBM] * n,
        out_shape=[jax.ShapeDtypeStruct(s, x.dtype) for s, x in zip(out_shapes, xs)],
        scratch_shapes=[pltpu.SemaphoreType.DMA((n, n_copies)), pltpu.SemaphoreType.DMA((n, n_copies)),
                        pltpu.SemaphoreType.DMA((n,))],
    )(*xs)


def _gather(xs, *, name):
    n = len(xs)

    def body(*refs):
        start, finish = _gather_steps(refs[:n], refs[n:2 * n], *refs[2 * n:])
        start()
        finish()

    return _exchange_call(body, xs, _gather_shapes(xs), _GATHER_COPIES, name)


_GATHER_COPIES = 7


def _gather_shapes(xs):
    return [(N_DEV,) + x.shape for x in xs]


def _gather_steps(x_refs, o_refs, send_sems, recv_sems, local_sems):
    n = len(x_refs)

    def plan():
        mx, my, mc = lax.axis_index("x"), lax.axis_index("y"), lax.axis_index("c")
        slot = lambda px, py, pc: 4 * px + 2 * py + pc
        me, sibling = (mx, my, mc), (mx, my, 1 - mc)
        chips = [(1 - mx, my), (mx, 1 - my), (1 - mx, 1 - my)]
        locals_, first = [], []
        for a in range(n):
            x_ref, o_ref = x_refs[a], o_refs[a]
            locals_.append(pltpu.make_async_copy(x_ref, o_ref.at[slot(*me)], local_sems.at[a]))
            first.append(_remote(x_ref, o_ref.at[slot(*me)], send_sems, recv_sems, (a, 0), sibling))
            for j, chip in enumerate(chips):
                first.append(_remote(x_ref, o_ref.at[slot(*me)], send_sems, recv_sems, (a, 1 + j), (*chip, mc)))
        return locals_, first, slot, me, sibling, chips, mc

    def start():
        locals_, first = plan()[:2]
        for cp in locals_ + first:
            cp.start()

    def finish():
        locals_, first, slot, me, sibling, chips, mc = plan()
        passed = []
        for j, chip in enumerate(chips):
            for a in range(n):
                landed = o_refs[a].at[slot(*chip, mc)]
                _remote(landed, landed, send_sems, recv_sems, (a, 1 + j), me).wait_recv()
                fwd = _remote(landed, landed, send_sems, recv_sems, (a, 4 + j), sibling)
                fwd.start()
                passed.append(fwd)
        for a in range(n):
            from_sib = o_refs[a].at[slot(*sibling)]
            _remote(from_sib, from_sib, send_sems, recv_sems, (a, 0), me).wait_recv()
            for j, chip in enumerate(chips):
                via_sib = o_refs[a].at[slot(*chip, 1 - mc)]
                _remote(via_sib, via_sib, send_sems, recv_sems, (a, 4 + j), me).wait_recv()
        for cp in first + passed:
            cp.wait_send()
        for cp in locals_:
            cp.wait()

    return start, finish


def _scatter_sibling(xs, *, name):
    n = len(xs)

    def body(*refs):
        x_refs, o_refs = refs[:n], refs[n:2 * n]
        send_sems, recv_sems, _ = refs[2 * n:]
        mx, my, mc = lax.axis_index("x"), lax.axis_index("y"), lax.axis_index("c")
        sibling = (mx, my, 1 - mc)
        sends = []
        for a in range(n):
            for ch in range(4):
                sends.append(_remote(x_refs[a].at[ch, 1 - mc], o_refs[a].at[ch], send_sems, recv_sems, (a, ch), sibling))
        for cp in sends:
            cp.start()
        for cp in sends:
            cp.wait_recv()
        for cp in sends:
            cp.wait_send()

    return _exchange_call(body, xs, [(4,) + x.shape[2:] for x in xs], 4, name)


def _scatter_chips(ys, *, name):
    n = len(ys)

    def body(*refs):
        start, finish = _chip_scatter_steps(refs[:n], refs[n:2 * n], *refs[2 * n:])
        start()
        finish()

    return _exchange_call(body, ys, _chip_scatter_shapes(ys), _CHIP_SCATTER_COPIES, name)


_CHIP_SCATTER_COPIES = 3


def _chip_scatter_shapes(ys):
    return [y.shape for y in ys]


def _chip_scatter_steps(y_refs, o_refs, send_sems, recv_sems, local_sems):
    n = len(y_refs)

    def plan():
        mx, my, mc = lax.axis_index("x"), lax.axis_index("y"), lax.axis_index("c")
        mine = 2 * mx + my
        chips = [(1 - mx, my), (mx, 1 - my), (1 - mx, 1 - my)]
        locals_, sends, recvs = [], [], []
        for a in range(n):
            locals_.append(pltpu.make_async_copy(y_refs[a].at[mine], o_refs[a].at[mine], local_sems.at[a]))
            for j, (px, py) in enumerate(chips):
                theirs = 2 * px + py
                sends.append(_remote(y_refs[a].at[theirs], o_refs[a].at[mine], send_sems, recv_sems, (a, j), (px, py, mc)))
                recvs.append(_remote(y_refs[a].at[theirs], o_refs[a].at[theirs], send_sems, recv_sems, (a, j), (px, py, mc)))
        return locals_, sends, recvs

    def start():
        locals_, sends, _ = plan()
        for cp in locals_ + sends:
            cp.start()

    def finish():
        locals_, sends, recvs = plan()
        for cp in recvs:
            cp.wait_recv()
        for cp in sends:
            cp.wait_send()
        for cp in locals_:
            cp.wait()

    return start, finish


def _pair_add(x, r, *, name):
    _, _, rows, c = x.shape
    tr = _tile(rows, 512, 16)

    def body(core_ref, x_ref, r_ref, o_ref):
        o_ref[...] = (x_ref[...].astype(F32) + r_ref[...].astype(F32)).astype(o_ref.dtype)

    core = lax.axis_index("c").astype(jnp.int32).reshape(1)
    return pl.pallas_call(
        body, name=name,
        grid_spec=pltpu.PrefetchScalarGridSpec(
            num_scalar_prefetch=1, grid=(4, rows // tr),
            in_specs=[pl.BlockSpec((None, None, tr, c), lambda ch, i, core: (ch, core[0], i, 0)),
                      pl.BlockSpec((None, tr, c), lambda ch, i, core: (ch, i, 0))],
            out_specs=pl.BlockSpec((None, tr, c), lambda ch, i, core: (ch, i, 0))),
        out_shape=jax.ShapeDtypeStruct((4, rows, c), x.dtype), compiler_params=_cparams(2),
    )(core, x, r)


def _adamw_reduce(parts, w, m, v, *, name):
    r, c = w.shape
    n_parts = parts.shape[0]
    tr = _tile(r, 256, 16)
    bc1 = 1.0 - ADAM_B1 ** ADAM_STEP
    bc2 = 1.0 - ADAM_B2 ** ADAM_STEP

    def body(p_ref, w_ref, m_ref, v_ref, g_ref, d_ref, nm_ref, nv_ref):
        g = p_ref[0].astype(F32)
        for q in range(1, n_parts):
            g = g + p_ref[q].astype(F32)
        nm = ADAM_B1 * m_ref[...] + (1.0 - ADAM_B1) * g
        nv = ADAM_B2 * v_ref[...] + (1.0 - ADAM_B2) * (g * g)
        g_ref[...] = g
        nm_ref[...] = nm
        nv_ref[...] = nv
        d_ref[...] = -ADAM_LR * ((nm / bc1) / (jnp.sqrt(nv / bc2) + ADAM_EPS) + ADAM_WD * w_ref[...])

    blk = pl.BlockSpec((tr, c), lambda i: (i, 0))
    return pl.pallas_call(
        body, name=name, grid=(r // tr,),
        in_specs=[pl.BlockSpec((n_parts, tr, c), lambda i: (0, i, 0)), blk, blk, blk], out_specs=[blk] * 4,
        out_shape=[jax.ShapeDtypeStruct((r, c), F32)] * 4, compiler_params=_cparams(1),
    )(parts, w, m, v)


def _col_full(g):
    return g.transpose(1, 2, 0, 3).reshape(g.shape[1], g.shape[2], -1)


def _col_parts(f):
    n, k, c8 = f.shape
    return f.reshape(n, k, N_DEV, c8 // N_DEV).transpose(2, 0, 1, 3)


def _row_full(g):
    return g.transpose(1, 0, 2, 3).reshape(g.shape[1], -1, g.shape[3])


def _row_parts(f):
    n, r8, c = f.shape
    return f.reshape(n, N_DEV, r8 // N_DEV, c).transpose(1, 0, 2, 3)


def _gu_full(g):
    n, d, c = g.shape[1:]
    return g.reshape(2, 4, n, d, c).transpose(2, 0, 3, 1, 4).reshape(n, 2, d, 4 * c)


def _gu_parts(f):
    n, _, d, c4 = f.shape
    return f.reshape(n, 2, d, 4, c4 // 4).transpose(1, 3, 0, 2, 4).reshape(N_DEV, n, d, c4 // 4)


def _ssd_in_full(g):
    return jnp.pad(_col_full(g), ((0, 0), (0, 0), (0, SSD_IN_PAD - SSD_IN_DIM)))


_MATMUL_WEIGHTS = (
    ("ffn1_w_gu", _gu_full, _gu_parts), ("ffn1_w_down", _row_full, _row_parts),
    ("ffn2_w_gu", _gu_full, _gu_parts), ("ffn2_w_down", _row_full, _row_parts),
    ("sb_w_qkv", _col_full, _col_parts), ("sb_w_o", _row_full, _row_parts),
    ("ssd_w_in", _ssd_in_full, _col_parts), ("ssd_w_out", _row_full, _row_parts),
    ("sc_w_in", _col_full, _col_parts), ("sc_w_out", _row_full, _row_parts),
)
_FIRST_WEIGHTS = ("ffn1_w_gu", "ffn1_w_down", "sb_w_qkv", "sb_w_o")
_CONV_WEIGHTS = ("ssd_conv_w", "sc_conv_w")
_REPLICATED = ("ffn1_norm", "mix_norm", "ffn2_norm", "final_norm", "ssd_conv_b", "ssd_norm",
               "ssd_dt_bias", "ssd_a_log", "ssd_d")
_ORDER = ("ffn1_norm", "ffn1_w_gu", "ffn1_w_down", "mix_norm", "ffn2_norm", "ffn2_w_gu", "ffn2_w_down",
          "sb_w_qkv", "sb_w_o", "ssd_w_in", "ssd_conv_w", "ssd_conv_b", "ssd_dt_bias", "ssd_a_log", "ssd_d",
          "ssd_norm", "ssd_w_out", "sc_w_in", "sc_conv_w", "sc_w_out", "final_norm")
_LANES = 1024


def _rows_of(a):
    flat = a.reshape(-1)
    pad = -flat.shape[0] % _LANES
    return jnp.pad(flat, (0, pad)).reshape(-1, _LANES)


def _pack_rows(arrays, mult):
    rows = [_rows_of(a) for a in arrays]
    packed = jnp.concatenate(rows, axis=0)
    pad = -packed.shape[0] % mult
    return jnp.pad(packed, ((0, pad), (0, 0))), [r.shape[0] for r in rows]


def _unpack_rows(packed, counts, shapes, lead=()):
    out, off = [], 0
    for n, shp in zip(counts, shapes):
        size = math.prod(shp)
        seg = packed[..., off:off + n, :].reshape(lead + (n * _LANES,))[..., :size]
        out.append(seg.reshape(lead + tuple(shp)))
        off += n
    return out


def kernel(x, ffn1_norm, ffn1_w_gu, ffn1_w_down, mix_norm, ffn2_norm, ffn2_w_gu, ffn2_w_down, sb_w_qkv, sb_w_o, ssd_w_in, ssd_conv_w, ssd_conv_b, ssd_dt_bias, ssd_a_log, ssd_d, ssd_norm, ssd_w_out, sc_w_in, sc_conv_w, sc_w_out, final_norm, loss_target, m_ffn1_norm, m_ffn1_w_gu, m_ffn1_w_down, m_mix_norm, m_ffn2_norm, m_ffn2_w_gu, m_ffn2_w_down, m_sb_w_qkv, m_sb_w_o, m_ssd_w_in, m_ssd_conv_w, m_ssd_conv_b, m_ssd_dt_bias, m_ssd_a_log, m_ssd_d, m_ssd_norm, m_ssd_w_out, m_sc_w_in, m_sc_conv_w, m_sc_w_out, m_final_norm, v_ffn1_norm, v_ffn1_w_gu, v_ffn1_w_down, v_mix_norm, v_ffn2_norm, v_ffn2_w_gu, v_ffn2_w_down, v_sb_w_qkv, v_sb_w_o, v_ssd_w_in, v_ssd_conv_w, v_ssd_conv_b, v_ssd_dt_bias, v_ssd_a_log, v_ssd_d, v_ssd_norm, v_ssd_w_out, v_sc_w_in, v_sc_conv_w, v_sc_w_out, v_final_norm):
    w = dict(ffn1_norm=ffn1_norm, ffn1_w_gu=ffn1_w_gu, ffn1_w_down=ffn1_w_down, mix_norm=mix_norm, ffn2_norm=ffn2_norm, ffn2_w_gu=ffn2_w_gu, ffn2_w_down=ffn2_w_down, sb_w_qkv=sb_w_qkv, sb_w_o=sb_w_o, ssd_w_in=ssd_w_in, ssd_conv_w=ssd_conv_w, ssd_conv_b=ssd_conv_b, ssd_dt_bias=ssd_dt_bias, ssd_a_log=ssd_a_log, ssd_d=ssd_d, ssd_norm=ssd_norm, ssd_w_out=ssd_w_out, sc_w_in=sc_w_in, sc_conv_w=sc_conv_w, sc_w_out=sc_w_out, final_norm=final_norm)
    mom = dict(ffn1_norm=m_ffn1_norm, ffn1_w_gu=m_ffn1_w_gu, ffn1_w_down=m_ffn1_w_down, mix_norm=m_mix_norm, ffn2_norm=m_ffn2_norm, ffn2_w_gu=m_ffn2_w_gu, ffn2_w_down=m_ffn2_w_down, sb_w_qkv=m_sb_w_qkv, sb_w_o=m_sb_w_o, ssd_w_in=m_ssd_w_in, ssd_conv_w=m_ssd_conv_w, ssd_conv_b=m_ssd_conv_b, ssd_dt_bias=m_ssd_dt_bias, ssd_a_log=m_ssd_a_log, ssd_d=m_ssd_d, ssd_norm=m_ssd_norm, ssd_w_out=m_ssd_w_out, sc_w_in=m_sc_w_in, sc_conv_w=m_sc_conv_w, sc_w_out=m_sc_w_out, final_norm=m_final_norm)
    var = dict(ffn1_norm=v_ffn1_norm, ffn1_w_gu=v_ffn1_w_gu, ffn1_w_down=v_ffn1_w_down, mix_norm=v_mix_norm, ffn2_norm=v_ffn2_norm, ffn2_w_gu=v_ffn2_w_gu, ffn2_w_down=v_ffn2_w_down, sb_w_qkv=v_sb_w_qkv, sb_w_o=v_sb_w_o, ssd_w_in=v_ssd_w_in, ssd_conv_w=v_ssd_conv_w, ssd_conv_b=v_ssd_conv_b, ssd_dt_bias=v_ssd_dt_bias, ssd_a_log=v_ssd_a_log, ssd_d=v_ssd_d, ssd_norm=v_ssd_norm, ssd_w_out=v_ssd_w_out, sc_w_in=v_sc_w_in, sc_conv_w=v_sc_conv_w, sc_w_out=v_sc_w_out, final_norm=v_final_norm)
    me = 4 * lax.axis_index("x") + 2 * lax.axis_index("y") + lax.axis_index("c")
    big = [n for n, _, _ in _MATMUL_WEIGHTS]
    two_d = lambda a: a.reshape(-1, a.shape[-1])

    to_full = {n: f for n, f, _ in _MATMUL_WEIGHTS}
    to_parts = {n: f for n, _, f in _MATMUL_WEIGHTS}
    first = [(n, 0) for n in _FIRST_WEIGHTS]
    later = [(n, i) for n in big for i in range(1 if n in _FIRST_WEIGHTS else 0, w[n].shape[0])]

    def shards(group):
        return [two_d(w[n][i].astype(BF16)) for n, i in group]

    def layers(group, gathered):
        out = {}
        for (n, i), g in zip(group, gathered):
            out.setdefault(n, []).append(to_full[n](g.reshape((N_DEV, 1) + w[n].shape[1:]))[0])
        return out

    gathered = _gather(shards(first) + [two_d(w[n]) for n in _CONV_WEIGHTS], name="gather_first")
    full = dict(w)
    full.update(layers(first, gathered))
    for n, g in zip(_CONV_WEIGHTS, gathered[len(first):]):
        full[n] = _col_full(g.reshape((N_DEV,) + w[n].shape))

    def host_of(n, i):
        if n.startswith("ffn2") and i == DEPTH - 1:
            return DEPTH - 1
        if (n.startswith("ffn1") and i == DEPTH - 1) or (n.startswith("sb_") and i == 1):
            return 1
        return 0

    gather_later = {}
    for host in (0, 1, DEPTH - 1):
        group = [(n, i) for n, i in later if host_of(n, i) == host]

        def merge(wd, gathered_group, group=group):
            wd = dict(wd)
            for n, ls in layers(group, gathered_group).items():
                have = wd[n] if isinstance(wd[n], list) else []
                wd[n] = have + ls
            return wd

        xs = shards(group)
        gather_later[host] = (_Hosted(_gather_steps, xs, _gather_shapes(xs), _GATHER_COPIES), merge)

    def chip_sums(group, grads, tag):
        parts = []
        for n, i in group:
            p8 = to_parts[n](grads[n][i][None].astype(BF16))
            parts.append(p8.reshape(4, 2, -1, p8.shape[-1]))
        from_sibling = _scatter_sibling(parts, name=f"scatter_sibling_{tag}")
        return [_pair_add(p, r, name=f"pair_add_{tag}_{n}{i}") for (n, i), p, r in zip(group, parts, from_sibling)]

    def scatter_early(grads):
        ys = chip_sums(later, grads, "later")
        return _Hosted(_chip_scatter_steps, ys, _chip_scatter_shapes(ys), _CHIP_SCATTER_COPIES)

    loss_part, dx, grads, recv_later = _local_step(x[0], loss_target[0], full, gather_later, scatter_early)
    loss = lax.psum(loss_part[0, 0], ("x", "y", "c"))

    recv_first = _scatter_chips(chip_sums(first, grads, "first"), name="scatter_chips_first")
    contrib = {n: [r] for (n, _), r in zip(first, recv_first)}
    for (n, _), r in zip(later, recv_later):
        contrib.setdefault(n, []).append(r)
    out_g, out_d, out_m, out_v = {}, {}, {}, {}

    def update(n, parts):
        res = _adamw_reduce(parts, two_d(w[n]), two_d(mom[n]), two_d(var[n]), name=f"adamw_{n}")
        out_g[n], out_d[n], out_m[n], out_v[n] = (r.reshape(w[n].shape) for r in res)

    for n in big:
        update(n, contrib[n][0] if len(contrib[n]) == 1 else jnp.concatenate(contrib[n], axis=1))

    small = list(_REPLICATED) + list(_CONV_WEIGHTS)
    small_shapes = [grads[n].shape for n in small]
    spacked, scounts = _pack_rows([grads[n].astype(F32) for n in small], 8)
    sg = _unpack_rows(_gather([spacked], name="gather_small_grads")[0], scounts, small_shapes, (N_DEV,))
    sg = dict(zip(small, sg))
    rep_w, rcounts = _pack_rows([w[n] for n in _REPLICATED], 8)
    rep_m, _ = _pack_rows([mom[n] for n in _REPLICATED], 8)
    rep_v, _ = _pack_rows([var[n] for n in _REPLICATED], 8)
    rep_p = jnp.concatenate([_rows_of(sg[n].reshape(N_DEV, -1)[q]) for q in range(N_DEV) for n in _REPLICATED], axis=0)
    rep_p = rep_p.reshape(N_DEV, -1, _LANES)
    rep_p = jnp.pad(rep_p, ((0, 0), (0, rep_w.shape[0] - rep_p.shape[1]), (0, 0)))
    res = _adamw_reduce(rep_p, rep_w, rep_m, rep_v, name="adamw_replicated")
    rep_shapes = [w[n].shape for n in _REPLICATED]
    for tgt, r in zip((out_g, out_d, out_m, out_v), res):
        for n, a in zip(_REPLICATED, _unpack_rows(r, rcounts, rep_shapes)):
            tgt[n] = a
    for n in _CONV_WEIGHTS:
        c = w[n].shape[-1]
        mine = lax.dynamic_slice_in_dim(sg[n], me * c, c, axis=sg[n].ndim - 1)
        update(n, mine.reshape(N_DEV, -1, c))

    return (loss, dx[None], *[out_g[n] for n in _ORDER], *[out_d[n] for n in _ORDER],
            *[out_m[n] for n in _ORDER], *[out_v[n] for n in _ORDER])
```

```python
import functools
import math

import jax
import jax.numpy as jnp
from jax import lax
from jax.experimental import pallas as pl
from jax.experimental.pallas import tpu as pltpu

F32 = jnp.float32
BF16 = jnp.bfloat16

D_MODEL = 1024
D_FF = 2816
DEPTH = 4
N_DEV = 8
SB_HEADS = 16
SB_HEAD_DIM = 64
SB_TILE = 256
SB_HEADS_PER_STEP = 2
SB_FWD_GROUPS = (2, 1)
SB_BWD_GROUPS = (4, 2, 1)
SSD_HEADS = 32
SSD_HEAD_DIM = 64
SSD_GROUPS = 8
SSD_HPG = 4
SSD_STATE = 128
SSD_CHUNK = 128
SSD_GROUPS_PER_STEP = 2
SSD_D_INNER = 2048
SSD_CONV_DIM = 4096
SSD_IN_DIM = 6176
SSD_IN_PAD = 6272
SSD_NORM_GROUP = 256
RMS_EPS = 1e-6
ADAM_LR = 0.001
ADAM_B1 = 0.9
ADAM_B2 = 0.999
ADAM_EPS = 1e-08
ADAM_WD = 0.01
ADAM_STEP = 10
VMEM_LIMIT = 60 * 1024 * 1024

NT = (((1,), (1,)), ((), ()))
TN = (((0,), (0,)), ((), ()))
NN = (((1,), (0,)), ((), ()))


def _cparams(n_axes):
    return pltpu.CompilerParams(dimension_semantics=("arbitrary",) * n_axes, vmem_limit_bytes=VMEM_LIMIT)


def _tile(n, want, mult=8):
    if n <= want:
        return n
    for t in range(want, 0, -1):
        if n % t == 0 and t % mult == 0:
            return t
    return n


def _sigmoid(x):
    return 1.0 / (1.0 + jnp.exp(-x))


def _dot(a, b, dn=NN):
    return lax.dot_general(a, b, dn, preferred_element_type=F32)


def _split3(x):
    x1 = x.astype(BF16)
    r1 = x - x1.astype(F32)
    x2 = r1.astype(BF16)
    x3 = (r1 - x2.astype(F32)).astype(BF16)
    return x1, x2, x3


def _dot_exact(x, t):
    x1, x2, x3 = _split3(x)
    return _dot(x1, t) + _dot(x2, t) + _dot(x3, t)


ROW_SUM_LANES = 1


def _cumsum_operand(tri):
    return jnp.concatenate([tri, tri], axis=0)


def _cumsum_rowsum(x, tri2):
    x1 = x.astype(BF16)
    x2 = (x - x1.astype(F32)).astype(BF16)
    return _dot(jnp.concatenate([x1, x2], axis=1), tri2), jnp.sum(x, axis=1, keepdims=True)


def _across_lanes(c, t):
    return c


def _mm(a, b, *, name, ta=False, tb=False, sa=False, sb=False, so=False, tm=512, tn=1024, tk=1024,
        out_dtype=F32, epilogue=None, extras=(), outs=None, pair=None, col_chunk=None,
        dims=None, a_spec=None, b_spec=None, extra_specs=None, out_defs=None):
    s_n = pair or (a.shape[0] if sa else (b.shape[0] if sb else 1))
    if dims:
        m, n, k = dims
    else:
        ash, bsh = a.shape[-2:], b.shape[-2:]
        m, k = (ash[1], ash[0]) if ta else ash
        n = bsh[0] if tb else bsh[1]
        tm, tn, tk = _tile(m, tm), _tile(n, tn, 128), _tile(k, tk, 128)
    nk = k // tk
    if outs is None:
        outs = [(out_dtype, "stile" if so else "tile")]
    if epilogue is None:
        epilogue = lambda acc: (acc,)

    if ta:
        a_blk, a_idx = (tk, tm), (lambda j, i, kk: (kk, i))
    else:
        a_blk, a_idx = (tm, tk), (lambda j, i, kk: (i, kk))
    if tb:
        b_blk, b_idx = (tn, tk), (lambda j, i, kk: (j, kk))
    else:
        b_blk, b_idx = (tk, tn), (lambda j, i, kk: (kk, j))

    def lead(blk, idx, has_s):
        if not has_s:
            return pl.BlockSpec(blk, idx)
        return pl.BlockSpec((s_n,) + blk, lambda j, i, kk: (0,) + idx(j, i, kk))

    kinds = {
        "tile": lambda: pl.BlockSpec((tm, tn), lambda j, i, kk: (i, j)),
        "stile": lambda: pl.BlockSpec((s_n, tm, tn), lambda j, i, kk: (0, i, j)),
        "row": lambda: pl.BlockSpec((1, tn), lambda j, i, kk: (0, j)),
        "colsum": lambda: pl.BlockSpec((1, tn), lambda j, i, kk: (0, j)),
    }
    shapes = {"tile": (m, n), "stile": (s_n, m, n), "colsum": (1, n)}
    in_specs = [a_spec or lead(a_blk, a_idx, sa), b_spec or lead(b_blk, b_idx, sb)]
    in_specs += [(extra_specs[e] if extra_specs and extra_specs[e] else kinds[kd]()) for e, (_, kd) in enumerate(extras)]
    out_specs = [(out_defs[o][0] if out_defs and out_defs[o] else kinds[kd]()) for o, (_, kd) in enumerate(outs)]
    out_shape = [jax.ShapeDtypeStruct(out_defs[o][1] if out_defs and out_defs[o] else shapes[kd], dt)
                 for o, (dt, kd) in enumerate(outs)]
    n_ex, n_out = len(extras), len(outs)
    dn = ((((0,) if ta else (1,)), ((1,) if tb else (0,))), ((), ()))
    acc_shape = (s_n, tm, tn) if so else (tm, tn)

    def body(*refs):
        a_ref, b_ref = refs[0], refs[1]
        ex_refs = refs[2:2 + n_ex]
        o_refs = refs[2 + n_ex:2 + n_ex + n_out]
        i = pl.program_id(1)
        kk = pl.program_id(2)

        def products():
            for s in range(s_n if (sa or sb) else 1):
                av = (a_ref[s] if sa else a_ref[...]).astype(BF16)
                bv = (b_ref[s] if sb else b_ref[...]).astype(BF16)
                yield s, lax.dot_general(av, bv, dn, preferred_element_type=F32)

        def finish(accv):
            vals = epilogue(accv, *[r[...] for r in ex_refs])
            for (dt, kd), o_ref, val in zip(outs, o_refs, vals):
                if kd == "colsum":
                    _accumulate(o_ref, val, i == 0)
                elif kd == "stile":
                    for s in range(s_n):
                        o_ref[s] = val[s].astype(dt)
                else:
                    o_ref[...] = val.astype(dt)

        if nk == 1 and col_chunk:
            bounds = [(c0, min(col_chunk, tn - c0)) for c0 in range(0, tn, col_chunk)]
            n_s = s_n if (sa or sb) else 1
            a_vals = [(a_ref[s] if sa else a_ref[...]).astype(BF16) for s in range(n_s if sa else 1)]
            accs = []
            for c0, cw in bounds:
                ds = []
                for s in range(n_s):
                    idx = ((s,) if sb else ()) + ((pl.ds(c0, cw), slice(None)) if tb else (slice(None), pl.ds(c0, cw)))
                    ds.append(lax.dot_general(a_vals[s if sa else 0], b_ref[idx].astype(BF16), dn,
                                              preferred_element_type=F32))
                accs.append(tuple(ds) if so else functools.reduce(jnp.add, ds))
            for (c0, cw), accv in zip(bounds, accs):
                cols = pl.ds(c0, cw)
                exv = [r[:, :, cols] if kd == "stile" else r[:, cols] for r, (_, kd) in zip(ex_refs, extras)]
                vals = epilogue(accv, *exv)
                for (dt, kd), o_ref, val in zip(outs, o_refs, vals):
                    if kd == "stile":
                        for s in range(s_n):
                            o_ref[s, :, cols] = val[s].astype(dt)
                    else:
                        o_ref[:, cols] = val.astype(dt)
            return

        if nk == 1:
            ds = [d for _, d in products()]
            finish(tuple(ds) if so else functools.reduce(jnp.add, ds))
            return

        acc = refs[-1]

        @pl.when(kk == 0)
        def _():
            acc[...] = jnp.zeros_like(acc)

        for s, d in products():
            if so:
                acc[s] += d
            else:
                acc[...] += d

        @pl.when(kk == nk - 1)
        def _():
            finish(tuple(acc[s] for s in range(s_n)) if so else acc[...])

    res = pl.pallas_call(
        body, name=name, grid=(n // tn, m // tm, nk),
        in_specs=in_specs, out_specs=out_specs, out_shape=out_shape,
        scratch_shapes=[pltpu.VMEM(acc_shape, F32)] if nk > 1 else [], compiler_params=_cparams(3),
    )(a, b, *[e for e, _ in extras])
    return res[0] if len(res) == 1 else res


def _accumulate(o_ref, val, first):
    @pl.when(first)
    def _():
        o_ref[...] = val

    @pl.when(jnp.logical_not(first))
    def _():
        o_ref[...] += val


def _rmsnorm(x, g, *, name):
    l, d = x.shape
    tm = _tile(l, 512)

    def body(x_ref, g_ref, o_ref):
        xv = x_ref[...]
        r = lax.rsqrt(jnp.mean(xv * xv, axis=1, keepdims=True) + RMS_EPS)
        o_ref[...] = (xv * r * g_ref[...]).astype(BF16)

    return pl.pallas_call(
        body, name=name, grid=(l // tm,),
        in_specs=[pl.BlockSpec((tm, d), lambda i: (i, 0)), pl.BlockSpec((1, d), lambda i: (0, 0))],
        out_specs=pl.BlockSpec((tm, d), lambda i: (i, 0)),
        out_shape=jax.ShapeDtypeStruct((l, d), BF16), compiler_params=_cparams(1),
    )(x, g)


def _norm_bwd_epilogue(dh, x, g, dres):
    r = lax.rsqrt(jnp.mean(x * x, axis=1, keepdims=True) + RMS_EPS)
    xh = x * r
    dg = jnp.sum(dh * xh, axis=0, keepdims=True)
    dxh = dh * g
    dx = r * (dxh - xh * jnp.mean(dxh * xh, axis=1, keepdims=True))
    return dres + dx, dg


def _final_loss(x, g, tgt, *, name):
    l, d = x.shape
    tm = _tile(l, 512)

    def body(x_ref, g_ref, t_ref, loss_ref, dx_ref, dg_ref):
        i = pl.program_id(0)
        xv, gv = x_ref[...], g_ref[...]
        r = lax.rsqrt(jnp.mean(xv * xv, axis=1, keepdims=True) + RMS_EPS)
        xh = xv * r
        e = xh * gv - t_ref[...]
        part = 0.5 * jnp.sum(jnp.mean(e * e, axis=1, keepdims=True), axis=0, keepdims=True)
        dy = e * (1.0 / d)
        dg = jnp.sum(dy * xh, axis=0, keepdims=True)
        dxh = dy * gv
        dx_ref[...] = r * (dxh - xh * jnp.mean(dxh * xh, axis=1, keepdims=True))
        _accumulate(dg_ref, dg, i == 0)
        _accumulate(loss_ref, jnp.broadcast_to(part, (1, 128)), i == 0)

    return pl.pallas_call(
        body, name=name, grid=(l // tm,),
        in_specs=[pl.BlockSpec((tm, d), lambda i: (i, 0)), pl.BlockSpec((1, d), lambda i: (0, 0)),
                  pl.BlockSpec((tm, d), lambda i: (i, 0))],
        out_specs=[pl.BlockSpec((1, 128), lambda i: (0, 0)), pl.BlockSpec((tm, d), lambda i: (i, 0)),
                   pl.BlockSpec((1, d), lambda i: (0, 0))],
        out_shape=[jax.ShapeDtypeStruct((1, 128), F32), jax.ShapeDtypeStruct((l, d), F32),
                   jax.ShapeDtypeStruct((1, d), F32)],
        compiler_params=_cparams(1),
    )(x, g, tgt)


def _ffn_fwd(x, g, wgu, wd, tag):
    l, d = x.shape
    q_n, c = wgu.shape[1], wgu.shape[3]
    tm = _tile(l, 512)
    h = _rmsnorm(x, g, name=f"{tag}_norm")

    def act(acc):
        gate, up = acc
        return acc, gate * _sigmoid(gate) * up

    gu, a = _mm(h, wgu, sb=True, so=True, dims=(l, q_n * c, d), tm=tm, tn=c, tk=d, name=f"{tag}_up",
                b_spec=pl.BlockSpec((2, None, d, c), lambda j, i, kk: (0, j, 0, 0)),
                outs=[(BF16, "stile"), (BF16, "tile")], epilogue=act,
                out_defs=[(pl.BlockSpec((2, None, tm, c), lambda j, i, kk: (0, j, i, 0)), (2, q_n, l, c)),
                          (pl.BlockSpec((None, tm, c), lambda j, i, kk: (j, i, 0)), (q_n, l, c))])
    xo = _mm(a, wd, dims=(l, d, q_n * c), tm=tm, tn=d, tk=c, name=f"{tag}_down",
             a_spec=pl.BlockSpec((None, tm, c), lambda j, i, kk: (kk, i, 0)), extras=[(x, "tile")],
             epilogue=lambda acc, xt: (xt + 0.5 * acc,))
    return xo, (x, h, gu, a)


def _ffn_bwd(dout, saved, g, wgu, wd, tag):
    x, h, gu, a = saved
    l, d = x.shape
    q_n, c = wgu.shape[1], wgu.shape[3]
    f = q_n * c
    tm = _tile(l, 512)
    tk_l = _tile(l, 2048)
    blocked = lambda rows: pl.BlockSpec((2, None, rows, c), lambda j, i, kk: (0, j, i, 0))

    def act_bwd(acc, guv):
        da = 0.5 * acc
        gate, up = guv[0].astype(F32), guv[1].astype(F32)
        s = _sigmoid(gate)
        return ((da * up * s * (1.0 + gate * (1.0 - s)), da * gate * s),)

    dgu = _mm(dout, wd, tb=True, pair=2, dims=(l, f, d), tm=tm, tn=c, tk=d, name=f"{tag}_dact",
              extras=[(gu, "stile")], extra_specs=[blocked(tm)], outs=[(BF16, "stile")],
              out_defs=[(blocked(tm), (2, q_n, l, c))], epilogue=act_bwd)
    dwd = _mm(a, dout, ta=True, dims=(f, d, l), tm=c, tn=d, tk=tk_l, name=f"{tag}_dwd", out_dtype=BF16,
              a_spec=pl.BlockSpec((None, tk_l, c), lambda j, i, kk: (i, kk, 0)), epilogue=lambda acc: (0.5 * acc,))
    tm_w = _tile(d, 512, 128)
    dwgu = _mm(h, dgu, ta=True, sb=True, so=True, dims=(d, f, l), tm=tm_w, tn=c, tk=tk_l, name=f"{tag}_dwgu",
               b_spec=pl.BlockSpec((2, None, tk_l, c), lambda j, i, kk: (0, j, kk, 0)),
               outs=[(BF16, "stile")], out_defs=[(blocked(tm_w), (2, q_n, d, c))])
    dx, dg = _mm(dgu, wgu, tb=True, sa=True, sb=True, dims=(l, d, f), tm=tm, tn=d, tk=c, name=f"{tag}_dx",
                 a_spec=pl.BlockSpec((2, None, tm, c), lambda j, i, kk: (0, kk, i, 0)),
                 b_spec=pl.BlockSpec((2, None, d, c), lambda j, i, kk: (0, kk, 0, 0)),
                 extras=[(x, "tile"), (g, "row"), (dout, "tile")], outs=[(F32, "tile"), (F32, "colsum")],
                 epilogue=_norm_bwd_epilogue)
    return dx, dg, dwgu, dwd


def _sb_plan(n, sizes):
    digits = [n // sizes[0]] + [(n // s) % 2 for s in sizes[1:]]
    plan, none_smaller = [], 1
    for size, d in reversed(list(zip(sizes, digits))):
        has = jnp.minimum(d, 1)
        with_diag = none_smaller * has
        plan.append((size, True, with_diag))
        if size > 1:
            plan.append((size, False, d - with_diag))
        none_smaller = none_smaller * (1 - has)
    return plan


def _sb_sweep(plan, start, step, fn, carry):
    pos = start
    for size, with_diag, trips in plan:
        diag = 0 if step < 0 else size - 1

        def trip(it, cr, size=size, with_diag=with_diag, pos=pos, diag=diag):
            base = pos + step * size * it
            return fn([base + step * b for b in range(size)], cr, [with_diag and b == diag for b in range(size)])

        carry = lax.fori_loop(0, trips, trip, carry)
        pos = pos + step * size * trips
    return carry


def _sb_logs(z):
    lb = jnp.minimum(z, 0.0) - jnp.log(1.0 + jnp.exp(-jnp.abs(z)))
    return lb, lb - z


class _Hosted:
    def __init__(self, steps, xs, out_shapes, copies):
        self.steps, self.xs, self.n, self.copies = steps, list(xs), len(xs), copies
        self.out_shape = [jax.ShapeDtypeStruct(s, x.dtype) for s, x in zip(out_shapes, xs)]
        self.specs = [_HBM] * self.n
        self.sems = [pltpu.SemaphoreType.DMA((self.n, copies)), pltpu.SemaphoreType.DMA((self.n, copies)),
                     pltpu.SemaphoreType.DMA((self.n,))]

    def run(self, x_refs, o_refs, sems, grid):
        ids = [pl.program_id(a) for a in range(len(grid))]
        first = functools.reduce(jnp.logical_and, [p == 0 for p in ids])
        last = functools.reduce(jnp.logical_and, [p == g - 1 for p, g in zip(ids, grid)])
        start, finish = self.steps(x_refs, o_refs, *sems)
        pl.when(first)(start)
        return lambda: pl.when(last)(finish)


def _head_masks(hs):
    lane = lax.broadcasted_iota(jnp.int32, (1, hs * SB_HEAD_DIM), 1)
    return [jnp.logical_and(lane >= hh * SB_HEAD_DIM, lane < (hh + 1) * SB_HEAD_DIM) for hh in range(hs)]


def _sb_fwd(qkv, *, name, hosted=None):
    l = qkv.shape[0]
    d_model = qkv.shape[1] // 3
    dh = SB_HEAD_DIM
    t = _tile(l, SB_TILE)
    hs = 2 * SB_HEADS_PER_STEP
    w = hs * dh
    n_grp = d_model // w
    scale = dh ** -0.5
    grid = (n_grp, l // t)
    nh = hosted.n if hosted else 0

    def body(q_ref, k_ref, v_ref, *rest):
        o_ref = rest[nh]
        at_end = hosted.run(rest[:nh], rest[nh + 1:2 * nh + 1], rest[2 * nh + 1:], grid) if hosted else None
        i = pl.program_id(1)
        heads = _head_masks(hs)
        q_all = (q_ref[...].astype(F32) * scale).astype(BF16)
        qs = [jnp.where(heads[hh], q_all, jnp.zeros_like(q_all)) for hh in range(hs)]
        row = lax.broadcasted_iota(jnp.int32, (t, t), 0)
        col = lax.broadcasted_iota(jnp.int32, (t, t), 1)
        strict = col < row
        tri = _cumsum_operand(strict.astype(BF16))

        def block(jbs, carry, masks):
            sls = [pl.ds(pl.multiple_of(jb * t, t), t) for jb in jbs]
            chains = [(hh, b) for b in range(len(jbs)) for hh in range(hs)]
            ks = [k_ref[sl, :] for sl in sls]
            zs = {(hh, b): _dot(qs[hh], ks[b], NT) for hh, b in chains}
            lbs, tails, sums = {}, {}, {}
            for hh, b in chains:
                lb, lk = _sb_logs(zs[hh, b])
                if masks[b]:
                    lk = jnp.where(strict, lk, 0.0)
                lbs[hh, b] = lb
                tails[hh, b], sums[hh, b] = _cumsum_rowsum(lk, tri)
            cs, o = list(carry[0]), carry[1]
            for b in range(len(jbs)):
                atts = []
                for hh in range(hs):
                    att = jnp.exp(lbs[hh, b] + tails[hh, b] + _across_lanes(cs[hh], t))
                    if masks[b]:
                        att = jnp.where(strict, att, 0.0)
                    atts.append(att.astype(BF16))
                    cs[hh] = cs[hh] + sums[hh, b]
                vb = v_ref[sls[b], :]
                v_heads = jnp.concatenate([jnp.where(heads[hh], vb, jnp.zeros_like(vb)) for hh in range(hs)], axis=0)
                o = o + _dot(jnp.concatenate(atts, axis=1), v_heads)
            return tuple(cs), o

        carry = (tuple(jnp.zeros((t, ROW_SUM_LANES), F32) for _ in range(hs)), jnp.zeros((t, w), F32))
        carry = _sb_sweep(_sb_plan(i + 1, SB_FWD_GROUPS), i, -1, block, carry)
        o_ref[...] = carry[1].astype(o_ref.dtype)
        if hosted:
            at_end()

    blocks = d_model // w
    res = pl.pallas_call(
        body, name=name, grid=grid,
        in_specs=[pl.BlockSpec((t, w), lambda g, i: (i, g)), pl.BlockSpec((l, w), lambda g, i: (0, blocks + g)),
                  pl.BlockSpec((l, w), lambda g, i: (0, 2 * blocks + g))] + (hosted.specs if hosted else []),
        out_specs=[pl.BlockSpec((t, w), lambda g, i: (i, g))] + (hosted.specs if hosted else []),
        out_shape=[jax.ShapeDtypeStruct((l, d_model), BF16)] + (hosted.out_shape if hosted else []),
        scratch_shapes=hosted.sems if hosted else [], compiler_params=_cparams(2),
    )(qkv, qkv, qkv, *(hosted.xs if hosted else []))
    return (res[0], res[1:]) if hosted else res[0]


def _sb_bwd(qkv, do, *, name, hosted=None):
    l = qkv.shape[0]
    d_model = qkv.shape[1] // 3
    dh = SB_HEAD_DIM
    t = _tile(l, SB_TILE)
    nq = l // t
    hs = SB_HEADS_PER_STEP
    w = hs * dh
    blocks = d_model // w
    scale = dh ** -0.5
    grid = (blocks, nq)
    nh = hosted.n if hosted else 0

    def body(q_ref, k_ref, v_ref, do_ref, *rest):
        dq_ref, dk_ref, dv_ref = rest[nh:nh + 3]
        e_scr, s_scr = rest[2 * nh + 3:2 * nh + 5]
        at_end = hosted.run(rest[:nh], rest[nh + 3:2 * nh + 3], rest[2 * nh + 5:], grid) if hosted else None
        i = pl.program_id(1)

        @pl.when(i == 0)
        def _():
            dk_ref[...] = jnp.zeros_like(dk_ref)
            dv_ref[...] = jnp.zeros_like(dv_ref)

        heads = _head_masks(hs)
        q_all = (q_ref[...].astype(F32) * scale).astype(BF16)
        do_all = do_ref[...]
        qs = [jnp.where(heads[hh], q_all, jnp.zeros_like(q_all)) for hh in range(hs)]
        dos = [jnp.where(heads[hh], do_all, jnp.zeros_like(do_all)) for hh in range(hs)]
        row = lax.broadcasted_iota(jnp.int32, (t, t), 0)
        col = lax.broadcasted_iota(jnp.int32, (t, t), 1)
        strict = col < row
        tri_suffix = _cumsum_operand(strict.astype(BF16))
        tri_prefix = _cumsum_operand((row < col).astype(BF16))

        def sweep1(jbs, cs, masks):
            sls = [pl.ds(pl.multiple_of(jb * t, t), t) for jb in jbs]
            chains = [(hh, b) for b in range(len(jbs)) for hh in range(hs)]
            zs = {(hh, b): _dot(qs[hh], k_ref[sls[b], :], NT) for hh, b in chains}
            datts = {(hh, b): _dot(dos[hh], v_ref[sls[b], :], NT) for hh, b in chains}
            lbs, tails, sums = {}, {}, {}
            for hh, b in chains:
                lb, lk = _sb_logs(zs[hh, b])
                if masks[b]:
                    lk = jnp.where(strict, lk, 0.0)
                lbs[hh, b] = lb
                tails[hh, b], sums[hh, b] = _cumsum_rowsum(lk, tri_suffix)
                s_scr[hh, jbs[b]] = jnp.exp(lb)
            cs = list(cs)
            for hh, b in chains:
                att = jnp.exp(lbs[hh, b] + tails[hh, b] + _across_lanes(cs[hh], t))
                if masks[b]:
                    att = jnp.where(strict, att, 0.0)
                e_scr[hh, jbs[b]] = att * datts[hh, b]
                dv_ref[sls[b], :] += _dot(att.astype(BF16), dos[hh], TN)
                cs[hh] = cs[hh] + sums[hh, b]
            return tuple(cs)

        plan = _sb_plan(i + 1, SB_BWD_GROUPS)
        _sb_sweep(plan, i, -1, sweep1, tuple(jnp.zeros((t, ROW_SUM_LANES), F32) for _ in range(hs)))

        def sweep2(jbs, carry, masks):
            sls = [pl.ds(pl.multiple_of(jb * t, t), t) for jb in jbs]
            chains = [(hh, b) for b in range(len(jbs)) for hh in range(hs)]
            des = {(hh, b): e_scr[hh, jbs[b]] for hh, b in chains}
            pres = {(hh, b): _cumsum_rowsum(des[hh, b], tri_prefix) for hh, b in chains}
            carry = [list(c) for c in carry]
            for hh, b in chains:
                p, dq = carry[hh]
                de, sg = des[hh, b], s_scr[hh, jbs[b]]
                dlk = _across_lanes(p, t) + pres[hh, b][0]
                if masks[b]:
                    dlk = jnp.where(strict, dlk, 0.0)
                dz = (de - sg * (de + dlk)).astype(BF16)
                dk_ref[sls[b], :] += _dot(dz, qs[hh], TN)
                carry[hh] = [p + pres[hh, b][1], dq + _dot(dz, k_ref[sls[b], :])]
            return tuple(tuple(c) for c in carry)

        carry = tuple((jnp.zeros((t, ROW_SUM_LANES), F32), jnp.zeros((t, w), F32)) for _ in range(hs))
        carry = _sb_sweep(plan[::-1], 0, 1, sweep2, carry)
        dq = jnp.zeros((t, w), F32)
        for hh in range(hs):
            dq = jnp.where(heads[hh], carry[hh][1], dq)
        dq_ref[...] = (dq * scale).astype(dq_ref.dtype)
        if hosted:
            at_end()

    qspec = pl.BlockSpec((t, w), lambda g, i: (i, g))
    cols = lambda off: pl.BlockSpec((l, w), lambda g, i: (0, off + g))
    res = pl.pallas_call(
        body, name=name, grid=grid,
        in_specs=[qspec, cols(blocks), cols(2 * blocks), qspec] + (hosted.specs if hosted else []),
        out_specs=[qspec, cols(0), cols(0)] + (hosted.specs if hosted else []),
        out_shape=[jax.ShapeDtypeStruct((l, d_model), BF16), jax.ShapeDtypeStruct((l, d_model), F32),
                   jax.ShapeDtypeStruct((l, d_model), F32)] + (hosted.out_shape if hosted else []),
        scratch_shapes=[pltpu.VMEM((hs, nq, t, t), F32), pltpu.VMEM((hs, nq, t, t), F32)]
        + (hosted.sems if hosted else []),
        compiler_params=_cparams(2),
    )(qkv, qkv, qkv, do, *(hosted.xs if hosted else []))
    return (res[0], res[1], res[2], res[3:]) if hosted else res


def _sb_layer_fwd(x, g, wqkv, wo, tag, hosted=None):
    h = _rmsnorm(x, g, name=f"{tag}_norm")
    qkv = _mm(h, wqkv, tm=1024, tn=1024, tk=1024, name=f"{tag}_qkv", out_dtype=BF16)
    o = _sb_fwd(qkv, name=f"{tag}_attn", hosted=hosted)
    carried = None
    if hosted:
        o, carried = o
    xo = _mm(o, wo, tm=1024, tn=1024, tk=1024, name=f"{tag}_out", extras=[(x, "tile")],
             epilogue=lambda acc, xt: (xt + acc,))
    return xo, (x, h, qkv, o), carried


def _sb_layer_bwd(dout, saved, g, wqkv, wo, tag, hosted=None):
    x, h, qkv, o = saved
    do = _mm(dout, wo, tb=True, tm=1024, tn=1024, tk=1024, name=f"{tag}_do", out_dtype=BF16)
    dwo = _mm(o, dout, ta=True, tm=1024, tn=1024, tk=2048, name=f"{tag}_dwo", out_dtype=BF16)
    res = _sb_bwd(qkv, do, name=f"{tag}_attn_bwd", hosted=hosted)
    dq, dk, dv = res[:3]
    carried = res[3] if hosted else None
    dqkv = jnp.concatenate([dq, dk.astype(BF16), dv.astype(BF16)], axis=1)
    dwqkv = _mm(h, dqkv, ta=True, tm=1024, tn=1024, tk=2048, name=f"{tag}_dwqkv", out_dtype=BF16)
    dx, dg = _mm(dqkv, wqkv, tb=True, tm=512, tn=1024, tk=3072, name=f"{tag}_dx",
                 extras=[(x, "tile"), (g, "row"), (dout, "tile")], outs=[(F32, "tile"), (F32, "colsum")],
                 epilogue=_norm_bwd_epilogue)
    return dx, dg, dwqkv, dwo, carried


def _shift_down(x, s, t_idx):
    return jnp.where(t_idx >= s, pltpu.roll(x, s, 0), 0.0)


def _shift_up(x, s, t_idx):
    n = x.shape[0]
    return jnp.where(t_idx < n - s, pltpu.roll(x, n - s, 0), 0.0)


def _sc_fwd(p, cw, *, name):
    l = p.shape[0]
    d = cw.shape[1]
    tc = 128
    nb = d // tc

    def body(b_ref, c_ref, h_ref, w_ref, o_ref):
        v = c_ref[...] * h_ref[...]
        t_idx = lax.broadcasted_iota(jnp.int32, v.shape, 0)
        u = v * w_ref[2:3, :] + _shift_down(v, 1, t_idx) * w_ref[1:2, :] + _shift_down(v, 2, t_idx) * w_ref[0:1, :]
        o_ref[...] = (b_ref[...] * u).astype(BF16)

    return pl.pallas_call(
        body, name=name, grid=(nb,),
        in_specs=[pl.BlockSpec((l, tc), lambda j: (0, j)), pl.BlockSpec((l, tc), lambda j: (0, nb + j)),
                  pl.BlockSpec((l, tc), lambda j: (0, 2 * nb + j)), pl.BlockSpec((3, tc), lambda j: (0, j))],
        out_specs=pl.BlockSpec((l, tc), lambda j: (0, j)),
        out_shape=jax.ShapeDtypeStruct((l, d), BF16), compiler_params=_cparams(1),
    )(p, p, p, cw)


def _sc_bwd(p, cw, dbu, *, name):
    l = p.shape[0]
    d = cw.shape[1]
    tc = 128
    nb = d // tc

    def body(b_ref, c_ref, h_ref, w_ref, g_ref, db_ref, dc_ref, dh_ref, dw_ref):
        cv, hv = c_ref[...], h_ref[...]
        v = cv * hv
        t_idx = lax.broadcasted_iota(jnp.int32, v.shape, 0)
        v1, v2 = _shift_down(v, 1, t_idx), _shift_down(v, 2, t_idx)
        u = v * w_ref[2:3, :] + v1 * w_ref[1:2, :] + v2 * w_ref[0:1, :]
        dbu_v = g_ref[...]
        db_ref[...] = (dbu_v * u).astype(BF16)
        du = dbu_v * b_ref[...]
        dv = du * w_ref[2:3, :] + _shift_up(du, 1, t_idx) * w_ref[1:2, :] + _shift_up(du, 2, t_idx) * w_ref[0:1, :]
        dc_ref[...] = (dv * hv).astype(BF16)
        dh_ref[...] = (dv * cv).astype(BF16)
        dw_ref[...] = jnp.zeros_like(dw_ref)
        dw_ref[0:1, :] = jnp.sum(du * v2, axis=0, keepdims=True)
        dw_ref[1:2, :] = jnp.sum(du * v1, axis=0, keepdims=True)
        dw_ref[2:3, :] = jnp.sum(du * v, axis=0, keepdims=True)

    col = lambda off: pl.BlockSpec((l, tc), lambda j: (0, off + j))
    return pl.pallas_call(
        body, name=name, grid=(nb,),
        in_specs=[col(0), col(nb), col(2 * nb), pl.BlockSpec((3, tc), lambda j: (0, j)), col(0)],
        out_specs=[col(0), col(0), col(0), pl.BlockSpec((8, tc), lambda j: (0, j))],
        out_shape=[jax.ShapeDtypeStruct((l, d), BF16)] * 3 + [jax.ShapeDtypeStruct((8, d), F32)],
        compiler_params=_cparams(1),
    )(p, p, p, cw, dbu)


def _sc_layer_fwd(x, g, win, cw, wout, tag):
    h = _rmsnorm(x, g, name=f"{tag}_norm")
    p = _mm(h, win, tm=1024, tn=1024, tk=1024, name=f"{tag}_in")
    bu = _sc_fwd(p, cw, name=f"{tag}_conv")
    xo = _mm(bu, wout, tm=1024, tn=1024, tk=1024, name=f"{tag}_out", extras=[(x, "tile")],
             epilogue=lambda acc, xt: (xt + acc,))
    return xo, (x, h, p, bu)


def _sc_layer_bwd(dout, saved, g, win, cw, wout, tag):
    x, h, p, bu = saved
    dbu = _mm(dout, wout, tb=True, tm=1024, tn=1024, tk=1024, name=f"{tag}_dbu")
    dwout = _mm(bu, dout, ta=True, tm=1024, tn=1024, tk=2048, name=f"{tag}_dwout", out_dtype=BF16)
    db, dc, dh, dcw = _sc_bwd(p, cw, dbu, name=f"{tag}_conv_bwd")
    dp = jnp.concatenate([db, dc, dh], axis=1)
    dwin = _mm(h, dp, ta=True, tm=1024, tn=1024, tk=2048, name=f"{tag}_dwin", out_dtype=BF16)
    dx, dg = _mm(dp, win, tb=True, tm=512, tn=1024, tk=3072, name=f"{tag}_dx",
                 extras=[(x, "tile"), (g, "row"), (dout, "tile")], outs=[(F32, "tile"), (F32, "colsum")],
                 epilogue=_norm_bwd_epilogue)
    return dx, dg, dwin, dcw[:3], dwout


def _ssd_conv_fwd(p, cw, cb, *, name):
    l = p.shape[0]
    tc = 128
    nb = SSD_CONV_DIM // tc
    off = SSD_D_INNER // tc

    def body(x_ref, w_ref, b_ref, o_ref):
        xv = x_ref[...]
        t_idx = lax.broadcasted_iota(jnp.int32, xv.shape, 0)
        pre = xv * w_ref[3:4, :] + b_ref[...]
        for s in (1, 2, 3):
            pre = pre + _shift_down(xv, s, t_idx) * w_ref[3 - s:4 - s, :]
        o_ref[...] = pre * _sigmoid(pre)

    return pl.pallas_call(
        body, name=name, grid=(nb,),
        in_specs=[pl.BlockSpec((l, tc), lambda j: (0, off + j)), pl.BlockSpec((4, tc), lambda j: (0, j)),
                  pl.BlockSpec((1, tc), lambda j: (0, j))],
        out_specs=pl.BlockSpec((l, tc), lambda j: (0, j)),
        out_shape=jax.ShapeDtypeStruct((l, SSD_CONV_DIM), F32), compiler_params=_cparams(1),
    )(p, cw, cb)


def _ssd_conv_bwd(p, cw, cb, dact, *, name):
    l = p.shape[0]
    tc = 128
    nb = SSD_CONV_DIM // tc
    off = SSD_D_INNER // tc

    def body(x_ref, w_ref, b_ref, g_ref, dx_ref, dw_ref):
        xv = x_ref[...]
        t_idx = lax.broadcasted_iota(jnp.int32, xv.shape, 0)
        xs = [xv] + [_shift_down(xv, s, t_idx) for s in (1, 2, 3)]
        pre = b_ref[...] + xs[0] * w_ref[3:4, :]
        for s in (1, 2, 3):
            pre = pre + xs[s] * w_ref[3 - s:4 - s, :]
        sg = _sigmoid(pre)
        dpre = g_ref[...] * sg * (1.0 + pre * (1.0 - sg))
        dx = dpre * w_ref[3:4, :]
        for s in (1, 2, 3):
            dx = dx + _shift_up(dpre, s, t_idx) * w_ref[3 - s:4 - s, :]
        dx_ref[...] = dx
        dw_ref[...] = jnp.zeros_like(dw_ref)
        for s in (0, 1, 2, 3):
            dw_ref[3 - s:4 - s, :] = jnp.sum(dpre * xs[s], axis=0, keepdims=True)
        dw_ref[4:5, :] = jnp.sum(dpre, axis=0, keepdims=True)

    return pl.pallas_call(
        body, name=name, grid=(nb,),
        in_specs=[pl.BlockSpec((l, tc), lambda j: (0, off + j)), pl.BlockSpec((4, tc), lambda j: (0, j)),
                  pl.BlockSpec((1, tc), lambda j: (0, j)), pl.BlockSpec((l, tc), lambda j: (0, j))],
        out_specs=[pl.BlockSpec((l, tc), lambda j: (0, j)), pl.BlockSpec((8, tc), lambda j: (0, j))],
        out_shape=[jax.ShapeDtypeStruct((l, SSD_CONV_DIM), F32), jax.ShapeDtypeStruct((8, SSD_CONV_DIM), F32)],
        compiler_params=_cparams(1),
    )(p, cw, cb, dact)


def _ssd_dt_fwd(p, bias, *, name):
    l = p.shape[0]
    tm = _tile(l, 1024)
    off = (SSD_D_INNER + SSD_CONV_DIM) // 128

    def body(x_ref, b_ref, o_ref):
        v = x_ref[...] + b_ref[...]
        o_ref[...] = jnp.maximum(v, 0.0) + jnp.log(1.0 + jnp.exp(-jnp.abs(v)))

    return pl.pallas_call(
        body, name=name, grid=(l // tm,),
        in_specs=[pl.BlockSpec((tm, 128), lambda i: (i, off)), pl.BlockSpec((1, 128), lambda i: (0, 0))],
        out_specs=pl.BlockSpec((tm, 128), lambda i: (i, 0)),
        out_shape=jax.ShapeDtypeStruct((l, 128), F32), compiler_params=_cparams(1),
    )(p, bias)


def _ssd_dt_bwd(p, bias, ddt, *, name):
    l = p.shape[0]
    tm = _tile(l, 1024)
    off = (SSD_D_INNER + SSD_CONV_DIM) // 128

    def body(x_ref, b_ref, g_ref, o_ref, db_ref):
        i = pl.program_id(0)
        d = g_ref[...] * _sigmoid(x_ref[...] + b_ref[...])
        o_ref[...] = d
        _accumulate(db_ref, jnp.sum(d, axis=0, keepdims=True), i == 0)

    return pl.pallas_call(
        body, name=name, grid=(l // tm,),
        in_specs=[pl.BlockSpec((tm, 128), lambda i: (i, off)), pl.BlockSpec((1, 128), lambda i: (0, 0)),
                  pl.BlockSpec((tm, 128), lambda i: (i, 0))],
        out_specs=[pl.BlockSpec((tm, 128), lambda i: (i, 0)), pl.BlockSpec((1, 128), lambda i: (0, 0))],
        out_shape=[jax.ShapeDtypeStruct((l, 128), F32), jax.ShapeDtypeStruct((1, 128), F32)],
        compiler_params=_cparams(1),
    )(p, bias, ddt)


def _row_to_col(r, eye):
    return jnp.sum(jnp.where(eye, r, 0.0), axis=1, keepdims=True)


def _col_to_row(c, eye):
    return jnp.sum(jnp.where(eye, c, 0.0), axis=0, keepdims=True)


def _ssd_chunk_common(b_ref, c_ref, dt_ref, a_ref, lam_scr):
    n = SSD_CHUNK
    row = lax.broadcasted_iota(jnp.int32, (n, n), 0)
    col = lax.broadcasted_iota(jnp.int32, (n, n), 1)
    bm, cm = b_ref[...].astype(BF16), c_ref[...].astype(BF16)
    g = _dot(cm, bm, NT)
    incl = (row <= col).astype(BF16)
    lam_scr[...] = _dot_exact(dt_ref[...] * a_ref[...], incl)
    return row, col, bm, cm, g


def _ssd_head_common(r, row, col, dt_ref, lam_scr):
    eye, tril = row == col, row >= col
    lam_r = lam_scr[r:r + 1, :]
    dt_r = dt_ref[r:r + 1, :]
    lam_c = _row_to_col(lam_r, eye)
    dt_c = _row_to_col(dt_r, eye)
    dk = jnp.where(tril, jnp.exp(jnp.minimum(lam_c - lam_r, 0.0)), 0.0)
    lam_last = jnp.sum(jnp.where(col[0:1, :] == SSD_CHUNK - 1, lam_r, 0.0), axis=1, keepdims=True)
    return eye, lam_r, dt_r, lam_c, dt_c, dk, lam_last


def _ssd_fwd(xh, act, dt_t, a_b, *, name, hosted=None):
    l = xh.shape[1]
    nc = l // SSD_CHUNK
    n, p_dim, hpg = SSD_CHUNK, SSD_HEAD_DIM, SSD_HPG

    gps = SSD_GROUPS_PER_STEP
    n_grp = SSD_GROUPS // gps
    grid = (n_grp, nc)
    nh = hosted.n if hosted else 0

    def body(x_ref, b_ref, c_ref, dt_ref, a_ref, *rest):
        y_ref, hp_ref = rest[nh:nh + 2]
        h_scr, lam_scr = rest[2 * nh + 2:2 * nh + 4]
        at_end = hosted.run(rest[:nh], rest[nh + 2:2 * nh + 2], rest[2 * nh + 4:], grid) if hosted else None

        @pl.when(pl.program_id(1) == 0)
        def _():
            h_scr[...] = jnp.zeros_like(h_scr)

        lanes = [pl.ds(gg * SSD_STATE, SSD_STATE) for gg in range(gps)]
        common = [_ssd_chunk_common(b_ref.at[:, lanes[gg]], c_ref.at[:, lanes[gg]], dt_ref.at[gg], a_ref.at[gg],
                                    lam_scr.at[gg]) for gg in range(gps)]
        for gg in range(gps):
            row, col, bm, cm, g = common[gg]
            for r in range(hpg):
                hd = gg * hpg + r
                _, _, dt_r, lam_c, dt_c, dk, lam_last = _ssd_head_common(r, row, col, dt_ref.at[gg], lam_scr.at[gg])
                xr = x_ref[hd]
                hr = h_scr[hd]
                w = (g * dk * dt_r).astype(BF16)
                y = _dot(w, xr.astype(BF16)) + _dot(cm, hr.astype(BF16), NT) * jnp.exp(lam_c)
                y_ref[hd] = y
                hp_ref[hd] = hr
                xw = (xr * (jnp.exp(lam_last - lam_c) * dt_c)).astype(BF16)
                h_scr[hd] = jnp.exp(lam_last) * hr + _dot(xw, bm, TN)
        if hosted:
            at_end()

    g_off = SSD_D_INNER // (gps * SSD_STATE)
    res = pl.pallas_call(
        body, name=name, grid=grid,
        in_specs=[pl.BlockSpec((gps * hpg, n, p_dim), lambda g, c: (g, c, 0)),
                  pl.BlockSpec((n, gps * SSD_STATE), lambda g, c: (c, g_off + g)),
                  pl.BlockSpec((n, gps * SSD_STATE), lambda g, c: (c, g_off + n_grp + g)),
                  pl.BlockSpec((gps, 8, n), lambda g, c: (g, 0, c)),
                  pl.BlockSpec((gps, 8, 128), lambda g, c: (g, 0, 0))] + (hosted.specs if hosted else []),
        out_specs=[pl.BlockSpec((gps * hpg, n, p_dim), lambda g, c: (g, c, 0)),
                   pl.BlockSpec((None, gps * hpg, p_dim, SSD_STATE), lambda g, c: (c, g, 0, 0))]
        + (hosted.specs if hosted else []),
        out_shape=[jax.ShapeDtypeStruct(xh.shape, F32),
                   jax.ShapeDtypeStruct((nc, SSD_HEADS, p_dim, SSD_STATE), F32)] + (hosted.out_shape if hosted else []),
        scratch_shapes=[pltpu.VMEM((gps * hpg, p_dim, SSD_STATE), F32), pltpu.VMEM((gps, 8, n), F32)]
        + (hosted.sems if hosted else []),
        compiler_params=_cparams(2),
    )(xh, act, act, dt_t, a_b, *(hosted.xs if hosted else []))
    return (res[0], res[1], res[2:]) if hosted else res


def _ssd_bwd(xh, act, dt_t, a_b, hprev, dyh, *, name):
    l = xh.shape[1]
    nc = l // SSD_CHUNK
    n, p_dim, hpg = SSD_CHUNK, SSD_HEAD_DIM, SSD_HPG
    gps = SSD_GROUPS_PER_STEP

    def body(x_ref, b_ref, c_ref, dt_ref, a_ref, hp_ref, dy_ref,
             dx_ref, db_ref, dc_ref, ddt_ref, da_ref, dh_scr, lam_scr, dlam_scr, ddt_scr):
        ci = pl.program_id(1)

        @pl.when(ci == 0)
        def _():
            dh_scr[...] = jnp.zeros_like(dh_scr)

        lanes = [pl.ds(gg * SSD_STATE, SSD_STATE) for gg in range(gps)]
        common = [_ssd_chunk_common(b_ref.at[:, lanes[gg]], c_ref.at[:, lanes[gg]], dt_ref.at[gg], a_ref.at[gg],
                                    lam_scr.at[gg]) for gg in range(gps)]
        dlam_scr[...] = jnp.zeros_like(dlam_scr)
        ddt_scr[...] = jnp.zeros_like(ddt_scr)
        for gg in range(gps):
            row, col, bm, cm, g = common[gg]
            dt_g, lam_g, dlam_g, ddt_g = dt_ref.at[gg], lam_scr.at[gg], dlam_scr.at[gg], ddt_scr.at[gg]
            dg_acc = jnp.zeros((n, n), F32)
            dc_acc = jnp.zeros((n, SSD_STATE), F32)
            db_acc = jnp.zeros((n, SSD_STATE), F32)
            for r in range(hpg):
                hd = gg * hpg + r
                eye, _, dt_r, lam_c, dt_c, dk, lam_last = _ssd_head_common(r, row, col, dt_g, lam_g)
                xr, dyr, hr, dhr = x_ref[hd], dy_ref[hd], hp_ref[hd], dh_scr[hd]
                xb, dyb, hb, dhb = xr.astype(BF16), dyr.astype(BF16), hr.astype(BF16), dhr.astype(BF16)
                e_l = jnp.exp(lam_c)
                e_last = jnp.exp(lam_last)
                decay_c = jnp.exp(lam_last - lam_c)
                w_c = decay_c * dt_c
                m = g * dk * dt_r
                dm = _dot(dyb, xb, NT)
                bdh = _dot(bm, dhb, NT)
                dx_ref[hd] = _dot(m.astype(BF16), dyb, TN) + w_c * bdh
                dg_acc = dg_acc + dm * dk * dt_r
                q_mat = dm * g * dk
                p_mat = q_mat * dt_r
                yoff = _dot(cm, hb, NT) * e_l
                q_c = jnp.sum(xr * bdh, axis=1, keepdims=True)
                dlam_c = (jnp.sum(p_mat, axis=1, keepdims=True) + jnp.sum(dyr * yoff, axis=1, keepdims=True)
                          - w_c * q_c)
                d_last = (jnp.sum(w_c * q_c, axis=0, keepdims=True)
                          + e_last * jnp.sum(jnp.sum(dhr * hr, axis=1, keepdims=True), axis=0, keepdims=True))
                dlam_g[r:r + 1, :] = (_col_to_row(dlam_c, eye) - jnp.sum(p_mat, axis=0, keepdims=True)
                                      + jnp.where(col[0:1, :] == n - 1, d_last, 0.0))
                ddt_g[r:r + 1, :] = jnp.sum(q_mat, axis=0, keepdims=True) + _col_to_row(decay_c * q_c, eye)
                dc_acc = dc_acc + e_l * _dot(dyb, hb)
                db_acc = db_acc + _dot((xr * w_c).astype(BF16), dhb)
                dh_scr[hd] = e_last * dhr + _dot((dyr * e_l).astype(BF16), cm, TN)

            dgb = dg_acc.astype(BF16)
            dc_ref[:, lanes[gg]] = _dot(dgb, bm) + dc_acc
            db_ref[:, lanes[gg]] = _dot(dgb, cm, TN) + db_acc
            rev = (row >= col).astype(BF16)
            da = _dot_exact(dlam_g[...], rev)
            ddt_ref[gg] = ddt_g[...] + da * a_ref[gg]
            _accumulate(da_ref.at[gg], da * dt_g[...], ci == 0)

        @pl.when(ci == nc - 1)
        def _():
            for gg in range(gps):
                da_ref[gg] = jnp.broadcast_to(jnp.sum(da_ref[gg], axis=1, keepdims=True), da_ref.shape[1:])

    g_off = SSD_D_INNER // (gps * SSD_STATE)
    n_grp = SSD_GROUPS // gps
    rc = lambda c: nc - 1 - c
    hspec = pl.BlockSpec((gps * hpg, n, p_dim), lambda g, c: (g, rc(c), 0))
    gspec = pl.BlockSpec((n, gps * SSD_STATE), lambda g, c: (rc(c), g))
    return pl.pallas_call(
        body, name=name, grid=(n_grp, nc),
        in_specs=[hspec,
                  pl.BlockSpec((n, gps * SSD_STATE), lambda g, c: (rc(c), g_off + g)),
                  pl.BlockSpec((n, gps * SSD_STATE), lambda g, c: (rc(c), g_off + n_grp + g)),
                  pl.BlockSpec((gps, 8, n), lambda g, c: (g, 0, rc(c))),
                  pl.BlockSpec((gps, 8, 128), lambda g, c: (g, 0, 0)),
                  pl.BlockSpec((None, gps * hpg, p_dim, SSD_STATE), lambda g, c: (rc(c), g, 0, 0)),
                  hspec],
        out_specs=[hspec, gspec, gspec,
                   pl.BlockSpec((gps, 8, n), lambda g, c: (g, 0, rc(c))),
                   pl.BlockSpec((gps, 8, 128), lambda g, c: (g, 0, 0))],
        out_shape=[jax.ShapeDtypeStruct(xh.shape, F32),
                   jax.ShapeDtypeStruct((l, SSD_GROUPS * SSD_STATE), F32),
                   jax.ShapeDtypeStruct((l, SSD_GROUPS * SSD_STATE), F32),
                   jax.ShapeDtypeStruct(dt_t.shape, F32),
                   jax.ShapeDtypeStruct(a_b.shape, F32)],
        scratch_shapes=[pltpu.VMEM((gps * hpg, p_dim, SSD_STATE), F32), pltpu.VMEM((gps, 8, n), F32),
                        pltpu.VMEM((gps, 8, n), F32), pltpu.VMEM((gps, 8, n), F32)],
        compiler_params=_cparams(2),
    )(xh, act, act, dt_t, a_b, hprev, dyh)


def _ssd_gate_fwd(y, act, p, d_vec, gn, *, name):
    l = y.shape[0]
    w = SSD_D_INNER
    tm = _tile(l, 256)

    def body(y_ref, xs_ref, z_ref, d_ref, g_ref, o_ref):
        for gi in range(SSD_GROUPS):
            sl = slice(gi * SSD_NORM_GROUP, (gi + 1) * SSD_NORM_GROUP)
            z = z_ref[:, sl]
            y2 = (y_ref[:, sl] + d_ref[:, sl] * xs_ref[:, sl]) * (z * _sigmoid(z))
            r = lax.rsqrt(jnp.mean(y2 * y2, axis=1, keepdims=True) + RMS_EPS)
            o_ref[:, sl] = (y2 * r * g_ref[:, sl]).astype(BF16)

    rows = pl.BlockSpec((tm, w), lambda i: (i, 0))
    vec = pl.BlockSpec((1, w), lambda i: (0, 0))
    return pl.pallas_call(
        body, name=name, grid=(l // tm,), in_specs=[rows, rows, rows, vec, vec], out_specs=rows,
        out_shape=jax.ShapeDtypeStruct((l, w), BF16), compiler_params=_cparams(1),
    )(y, act, p, d_vec, gn)


def _ssd_gate_bwd(dyn, y, act, p, d_vec, gn, *, name):
    l = y.shape[0]
    w = SSD_D_INNER
    tm = _tile(l, 256)

    def body(dyn_ref, y_ref, xs_ref, z_ref, d_ref, g_ref, dy_ref, dz_ref, dxs_ref, dd_ref, dg_ref):
        i = pl.program_id(0)
        for gi in range(SSD_GROUPS):
            sl = slice(gi * SSD_NORM_GROUP, (gi + 1) * SSD_NORM_GROUP)
            z, xs, dv = z_ref[:, sl], xs_ref[:, sl], d_ref[:, sl]
            s = _sigmoid(z)
            sz = z * s
            y1 = y_ref[:, sl] + dv * xs
            y2 = y1 * sz
            r = lax.rsqrt(jnp.mean(y2 * y2, axis=1, keepdims=True) + RMS_EPS)
            y2h = y2 * r
            dyn_v = dyn_ref[:, sl]
            d2h = dyn_v * g_ref[:, sl]
            dy2 = r * (d2h - y2h * jnp.mean(d2h * y2h, axis=1, keepdims=True))
            dy1 = dy2 * sz
            dy_ref[:, sl] = dy1
            dz_ref[:, sl] = dy2 * y1 * s * (1.0 + z * (1.0 - s))
            dxs_ref[:, sl] = dv * dy1
            _accumulate(dd_ref.at[:, sl], jnp.sum(dy1 * xs, axis=0, keepdims=True), i == 0)
            _accumulate(dg_ref.at[:, sl], jnp.sum(dyn_v * y2h, axis=0, keepdims=True), i == 0)

    rows = pl.BlockSpec((tm, w), lambda i: (i, 0))
    vec = pl.BlockSpec((1, w), lambda i: (0, 0))
    return pl.pallas_call(
        body, name=name, grid=(l // tm,), in_specs=[rows, rows, rows, rows, vec, vec],
        out_specs=[rows, rows, rows, vec, vec],
        out_shape=[jax.ShapeDtypeStruct((l, w), F32)] * 3 + [jax.ShapeDtypeStruct((1, w), F32)] * 2,
        compiler_params=_cparams(1),
    )(dyn, y, act, p, d_vec, gn)


def _heads_major(x):
    return x.reshape(x.shape[0], SSD_HEADS, SSD_HEAD_DIM).transpose(1, 0, 2)


def _ssd_layer_fwd(x, g, win, cw, cb, dt_bias, a_log, d_skip, gn, wout, tag, hosted=None):
    l = x.shape[0]
    h = _rmsnorm(x, g, name=f"{tag}_norm")
    p = _mm(h, win, tm=1024, tn=896, tk=1024, name=f"{tag}_in")
    act = _ssd_conv_fwd(p, cw, cb, name=f"{tag}_conv")
    bias = jnp.pad(dt_bias, (0, 128 - SSD_HEADS)).reshape(1, 128)
    dt = _ssd_dt_fwd(p, bias, name=f"{tag}_dt")
    xh = _heads_major(act[:, :SSD_D_INNER])
    dt_t = jnp.pad(dt[:, :SSD_HEADS].T.reshape(SSD_GROUPS, SSD_HPG, l), ((0, 0), (0, 8 - SSD_HPG), (0, 0)))
    a = -jnp.exp(a_log).reshape(SSD_GROUPS, SSD_HPG, 1)
    a_b = jnp.broadcast_to(jnp.pad(a, ((0, 0), (0, 8 - SSD_HPG), (0, 0))), (SSD_GROUPS, 8, 128))
    res = _ssd_fwd(xh, act, dt_t, a_b, name=f"{tag}_scan", hosted=hosted)
    yh, hprev = res[:2]
    carried = res[2] if hosted else None
    y = yh.transpose(1, 0, 2).reshape(l, SSD_D_INNER)
    d_vec = jnp.repeat(d_skip, SSD_HEAD_DIM).reshape(1, SSD_D_INNER)
    yn = _ssd_gate_fwd(y, act, p, d_vec, gn, name=f"{tag}_gate")
    xo = _mm(yn, wout, tm=1024, tn=1024, tk=2048, name=f"{tag}_out", extras=[(x, "tile")],
             epilogue=lambda acc, xt: (xt + acc,))
    return xo, (x, h, p, act, bias, xh, dt_t, a_b, hprev, y, d_vec, yn), carried


def _ssd_layer_bwd(dout, saved, g, win, cw, cb, gn, wout, tag):
    x, h, p, act, bias, xh, dt_t, a_b, hprev, y, d_vec, yn = saved
    l = x.shape[0]
    dyn = _mm(dout, wout, tb=True, tm=1024, tn=1024, tk=1024, name=f"{tag}_dyn")
    dwout = _mm(yn, dout, ta=True, tm=1024, tn=1024, tk=2048, name=f"{tag}_dwout", out_dtype=BF16)
    dy, dz, dxs_d, dd_vec, dgn = _ssd_gate_bwd(dyn, y, act, p, d_vec, gn, name=f"{tag}_gate_bwd")
    dxh, dbm, dcm, ddt_t, da_b = _ssd_bwd(xh, act, dt_t, a_b, hprev, _heads_major(dy), name=f"{tag}_scan_bwd")
    dxs = dxh.transpose(1, 0, 2).reshape(l, SSD_D_INNER) + dxs_d
    dact = jnp.concatenate([dxs, dbm, dcm], axis=1)
    dxbc, dcw8 = _ssd_conv_bwd(p, cw, cb, dact, name=f"{tag}_conv_bwd")
    ddt = jnp.pad(ddt_t[:, :SSD_HPG, :].reshape(SSD_HEADS, l).T, ((0, 0), (0, 128 - SSD_HEADS)))
    ddt_raw, dbias = _ssd_dt_bwd(p, bias, ddt, name=f"{tag}_dt_bwd")
    dp = jnp.concatenate([dz, dxbc, ddt_raw], axis=1)
    dwin = _mm(h, dp, ta=True, tm=1024, tn=896, tk=2048, name=f"{tag}_dwin", out_dtype=BF16)
    dx, dg = _mm(dp, win, tb=True, tm=256, tn=1024, tk=6272, name=f"{tag}_dx",
                 extras=[(x, "tile"), (g, "row"), (dout, "tile")], outs=[(F32, "tile"), (F32, "colsum")],
                 epilogue=_norm_bwd_epilogue)
    a_heads = a_b[:, :SSD_HPG, 0].reshape(SSD_HEADS)
    grads = dict(
        ssd_w_in=dwin[:, :SSD_IN_DIM], ssd_conv_w=dcw8[:4], ssd_conv_b=dcw8[4],
        ssd_dt_bias=dbias[0, :SSD_HEADS], ssd_a_log=da_b[:, :SSD_HPG, 0].reshape(SSD_HEADS) * a_heads,
        ssd_d=dd_vec.reshape(SSD_HEADS, SSD_HEAD_DIM).sum(axis=1), ssd_norm=dgn[0], ssd_w_out=dwout)
    return dx, dg, grads


def _local_step(x, tgt, w, gather_later=None, scatter_early=None):
    row = lambda v: v.reshape(1, -1)
    saved = []
    for i in range(DEPTH):
        kind, j = i % 3, i // 3
        x, s1 = _ffn_fwd(x, row(w["ffn1_norm"][i]), w["ffn1_w_gu"][i], w["ffn1_w_down"][i], f"l{i}f1")
        gm = row(w["mix_norm"][i])
        hook = (gather_later or {}).get(i)
        hosted = hook[0] if hook else None
        if kind == 0:
            x, sm, carried = _sb_layer_fwd(x, gm, w["sb_w_qkv"][j], w["sb_w_o"][j], f"l{i}sb", hosted=hosted)
        elif kind == 1:
            x, sm, carried = _ssd_layer_fwd(x, gm, w["ssd_w_in"][j], w["ssd_conv_w"][j], row(w["ssd_conv_b"][j]),
                                            w["ssd_dt_bias"][j], w["ssd_a_log"][j], w["ssd_d"][j],
                                            row(w["ssd_norm"][j]), w["ssd_w_out"][j], f"l{i}ssd", hosted=hosted)
        if hook:
            w = hook[1](w, carried)
        if kind == 2:
            x, sm = _sc_layer_fwd(x, gm, w["sc_w_in"][j], w["sc_conv_w"][j], w["sc_w_out"][j], f"l{i}sc")
        x, s2 = _ffn_fwd(x, row(w["ffn2_norm"][i]), w["ffn2_w_gu"][i], w["ffn2_w_down"][i], f"l{i}f2")
        saved.append((s1, sm, s2))

    loss, dx, dfinal = _final_loss(x, row(w["final_norm"]), tgt, name="final_loss")
    per_layer = {k: [None] * DEPTH for k in ("ffn1_norm", "ffn1_w_gu", "ffn1_w_down", "mix_norm",
                                             "ffn2_norm", "ffn2_w_gu", "ffn2_w_down")}
    per_layer.update({"sb_w_qkv": [None, None], "sb_w_o": [None, None]})
    grads = {"final_norm": dfinal[0]}
    early = None
    for i in reversed(range(DEPTH)):
        kind, j = i % 3, i // 3
        s1, sm, s2 = saved[i]
        dx, dg, dwgu, dwd = _ffn_bwd(dx, s2, row(w["ffn2_norm"][i]), w["ffn2_w_gu"][i], w["ffn2_w_down"][i], f"l{i}f2")
        per_layer["ffn2_norm"][i], per_layer["ffn2_w_gu"][i], per_layer["ffn2_w_down"][i] = dg[0], dwgu, dwd
        gm = row(w["mix_norm"][i])
        if kind == 0:
            hosted = scatter_early({**grads, **per_layer}) if (scatter_early and i == 0) else None
            dx, dg, dwqkv, dwo, carried = _sb_layer_bwd(dx, sm, gm, w["sb_w_qkv"][j], w["sb_w_o"][j], f"l{i}sb",
                                                        hosted=hosted)
            per_layer["sb_w_qkv"][j], per_layer["sb_w_o"][j] = dwqkv, dwo
            if hosted:
                early = carried
        elif kind == 1:
            dx, dg, sg = _ssd_layer_bwd(dx, sm, gm, w["ssd_w_in"][j], w["ssd_conv_w"][j], row(w["ssd_conv_b"][j]),
                                        row(w["ssd_norm"][j]), w["ssd_w_out"][j], f"l{i}ssd")
            sg["ssd_w_in"], sg["ssd_w_out"] = [sg["ssd_w_in"]], [sg["ssd_w_out"]]
            grads.update({k: (v if isinstance(v, list) else v[None]) for k, v in sg.items()})
        else:
            dx, dg, dwin, dcw, dwout = _sc_layer_bwd(dx, sm, gm, w["sc_w_in"][j], w["sc_conv_w"][j],
                                                     w["sc_w_out"][j], f"l{i}sc")
            grads.update(sc_w_in=[dwin], sc_conv_w=dcw[None], sc_w_out=[dwout])
        per_layer["mix_norm"][i] = dg[0]
        dx, dg, dwgu, dwd = _ffn_bwd(dx, s1, row(w["ffn1_norm"][i]), w["ffn1_w_gu"][i], w["ffn1_w_down"][i], f"l{i}f1")
        per_layer["ffn1_norm"][i], per_layer["ffn1_w_gu"][i], per_layer["ffn1_w_down"][i] = dg[0], dwgu, dwd
    for k, v in per_layer.items():
        grads[k] = jnp.stack(v) if k.endswith("_norm") else v
    return loss, dx, grads, early


_HBM = pl.BlockSpec(memory_space=pltpu.HBM)


def _remote(src, dst, send_sems, recv_sems, idx, dev):
    return pltpu.make_async_remote_copy(src_ref=src, dst_ref=dst, send_sem=send_sems.at[idx], recv_sem=recv_sems.at[idx],
                                        device_id=dev, device_id_type=pl.DeviceIdType.MESH)


def _exchange_call(body, xs, out_shapes, n_copies, name):
    n = len(xs)
    return pl.pallas_call(
        body, name=name, in_specs=[_HBM] * n, out_specs=[_HBM] * n,
        out_shape=[jax.ShapeDtypeStruct(s, x.dtype) for s, x in zip(out_shapes, xs)],
        scratch_shapes=[pltpu.SemaphoreType.DMA((n, n_copies)), pltpu.SemaphoreType.DMA((n, n_copies)),
                        pltpu.SemaphoreType.DMA((n,))],
    )(*xs)


def _gather(xs, *, name):
    n = len(xs)

    def body(*refs):
        start, finish = _gather_steps(refs[:n], refs[n:2 * n], *refs[2 * n:])
        start()
        finish()

    return _exchange_call(body, xs, _gather_shapes(xs), _GATHER_COPIES, name)


_GATHER_COPIES = 7


def _gather_shapes(xs):
    return [(N_DEV,) + x.shape for x in xs]


def _gather_steps(x_refs, o_refs, send_sems, recv_sems, local_sems):
    n = len(x_refs)

    def plan():
        mx, my, mc = lax.axis_index("x"), lax.axis_index("y"), lax.axis_index("c")
        slot = lambda px, py, pc: 4 * px + 2 * py + pc
        me, sibling = (mx, my, mc), (mx, my, 1 - mc)
        chips = [(1 - mx, my), (mx, 1 - my), (1 - mx, 1 - my)]
        locals_, first = [], []
        for a in range(n):
            x_ref, o_ref = x_refs[a], o_refs[a]
            locals_.append(pltpu.make_async_copy(x_ref, o_ref.at[slot(*me)], local_sems.at[a]))
            first.append(_remote(x_ref, o_ref.at[slot(*me)], send_sems, recv_sems, (a, 0), sibling))
            for j, chip in enumerate(chips):
                first.append(_remote(x_ref, o_ref.at[slot(*me)], send_sems, recv_sems, (a, 1 + j), (*chip, mc)))
        return locals_, first, slot, me, sibling, chips, mc

    def start():
        locals_, first = plan()[:2]
        for cp in locals_ + first:
            cp.start()

    def finish():
        locals_, first, slot, me, sibling, chips, mc = plan()
        passed = []
        for j, chip in enumerate(chips):
            for a in range(n):
                landed = o_refs[a].at[slot(*chip, mc)]
                _remote(landed, landed, send_sems, recv_sems, (a, 1 + j), me).wait_recv()
                fwd = _remote(landed, landed, send_sems, recv_sems, (a, 4 + j), sibling)
                fwd.start()
                passed.append(fwd)
        for a in range(n):
            from_sib = o_refs[a].at[slot(*sibling)]
            _remote(from_sib, from_sib, send_sems, recv_sems, (a, 0), me).wait_recv()
            for j, chip in enumerate(chips):
                via_sib = o_refs[a].at[slot(*chip, 1 - mc)]
                _remote(via_sib, via_sib, send_sems, recv_sems, (a, 4 + j), me).wait_recv()
        for cp in first + passed:
            cp.wait_send()
        for cp in locals_:
            cp.wait()

    return start, finish


def _scatter_sibling(xs, *, name):
    n = len(xs)

    def body(*refs):
        x_refs, o_refs = refs[:n], refs[n:2 * n]
        send_sems, recv_sems, _ = refs[2 * n:]
        mx, my, mc = lax.axis_index("x"), lax.axis_index("y"), lax.axis_index("c")
        sibling = (mx, my, 1 - mc)
        sends = []
        for a in range(n):
            for ch in range(4):
                sends.append(_remote(x_refs[a].at[ch, 1 - mc], o_refs[a].at[ch], send_sems, recv_sems, (a, ch), sibling))
        for cp in sends:
            cp.start()
        for cp in sends:
            cp.wait_recv()
        for cp in sends:
            cp.wait_send()

    return _exchange_call(body, xs, [(4,) + x.shape[2:] for x in xs], 4, name)


def _scatter_chips(ys, *, name):
    n = len(ys)

    def body(*refs):
        start, finish = _chip_scatter_steps(refs[:n], refs[n:2 * n], *refs[2 * n:])
        start()
        finish()

    return _exchange_call(body, ys, _chip_scatter_shapes(ys), _CHIP_SCATTER_COPIES, name)


_CHIP_SCATTER_COPIES = 3


def _chip_scatter_shapes(ys):
    return [y.shape for y in ys]


def _chip_scatter_steps(y_refs, o_refs, send_sems, recv_sems, local_sems):
    n = len(y_refs)

    def plan():
        mx, my, mc = lax.axis_index("x"), lax.axis_index("y"), lax.axis_index("c")
        mine = 2 * mx + my
        chips = [(1 - mx, my), (mx, 1 - my), (1 - mx, 1 - my)]
        locals_, sends, recvs = [], [], []
        for a in range(n):
            locals_.append(pltpu.make_async_copy(y_refs[a].at[mine], o_refs[a].at[mine], local_sems.at[a]))
            for j, (px, py) in enumerate(chips):
                theirs = 2 * px + py
                sends.append(_remote(y_refs[a].at[theirs], o_refs[a].at[mine], send_sems, recv_sems, (a, j), (px, py, mc)))
                recvs.append(_remote(y_refs[a].at[theirs], o_refs[a].at[theirs], send_sems, recv_sems, (a, j), (px, py, mc)))
        return locals_, sends, recvs

    def start():
        locals_, sends, _ = plan()
        for cp in locals_ + sends:
            cp.start()

    def finish():
        locals_, sends, recvs = plan()
        for cp in recvs:
            cp.wait_recv()
        for cp in sends:
            cp.wait_send()
        for cp in locals_:
            cp.wait()

    return start, finish


def _pair_add(x, r, *, name):
    _, _, rows, c = x.shape
    tr = _tile(rows, 512, 16)

    def body(core_ref, x_ref, r_ref, o_ref):
        o_ref[...] = (x_ref[...].astype(F32) + r_ref[...].astype(F32)).astype(o_ref.dtype)

    core = lax.axis_index("c").astype(jnp.int32).reshape(1)
    return pl.pallas_call(
        body, name=name,
        grid_spec=pltpu.PrefetchScalarGridSpec(
            num_scalar_prefetch=1, grid=(4, rows // tr),
            in_specs=[pl.BlockSpec((None, None, tr, c), lambda ch, i, core: (ch, core[0], i, 0)),
                      pl.BlockSpec((None, tr, c), lambda ch, i, core: (ch, i, 0))],
            out_specs=pl.BlockSpec((None, tr, c), lambda ch, i, core: (ch, i, 0))),
        out_shape=jax.ShapeDtypeStruct((4, rows, c), x.dtype), compiler_params=_cparams(2),
    )(core, x, r)


def _adamw_reduce(parts, w, m, v, *, name):
    r, c = w.shape
    n_parts = parts.shape[0]
    tr = _tile(r, 256, 16)
    bc1 = 1.0 - ADAM_B1 ** ADAM_STEP
    bc2 = 1.0 - ADAM_B2 ** ADAM_STEP

    def body(p_ref, w_ref, m_ref, v_ref, g_ref, d_ref, nm_ref, nv_ref):
        g = p_ref[0].astype(F32)
        for q in range(1, n_parts):
            g = g + p_ref[q].astype(F32)
        nm = ADAM_B1 * m_ref[...] + (1.0 - ADAM_B1) * g
        nv = ADAM_B2 * v_ref[...] + (1.0 - ADAM_B2) * (g * g)
        g_ref[...] = g
        nm_ref[...] = nm
        nv_ref[...] = nv
        d_ref[...] = -ADAM_LR * ((nm / bc1) / (jnp.sqrt(nv / bc2) + ADAM_EPS) + ADAM_WD * w_ref[...])

    blk = pl.BlockSpec((tr, c), lambda i: (i, 0))
    return pl.pallas_call(
        body, name=name, grid=(r // tr,),
        in_specs=[pl.BlockSpec((n_parts, tr, c), lambda i: (0, i, 0)), blk, blk, blk], out_specs=[blk] * 4,
        out_shape=[jax.ShapeDtypeStruct((r, c), F32)] * 4, compiler_params=_cparams(1),
    )(parts, w, m, v)


def _col_full(g):
    return g.transpose(1, 2, 0, 3).reshape(g.shape[1], g.shape[2], -1)


def _col_parts(f):
    n, k, c8 = f.shape
    return f.reshape(n, k, N_DEV, c8 // N_DEV).transpose(2, 0, 1, 3)


def _row_full(g):
    return g.transpose(1, 0, 2, 3).reshape(g.shape[1], -1, g.shape[3])


def _row_parts(f):
    n, r8, c = f.shape
    return f.reshape(n, N_DEV, r8 // N_DEV, c).transpose(1, 0, 2, 3)


def _gu_full(g):
    n, d, c = g.shape[1:]
    return g.reshape(2, 4, n, d, c).transpose(2, 0, 1, 3, 4)


def _gu_parts(f):
    n, _, _, d, c = f.shape
    return f.transpose(1, 2, 0, 3, 4).reshape(N_DEV, n, d, c)


def _ssd_in_full(g):
    return jnp.pad(_col_full(g), ((0, 0), (0, 0), (0, SSD_IN_PAD - SSD_IN_DIM)))


_MATMUL_WEIGHTS = (
    ("ffn1_w_gu", _gu_full, _gu_parts), ("ffn1_w_down", _row_full, _row_parts),
    ("ffn2_w_gu", _gu_full, _gu_parts), ("ffn2_w_down", _row_full, _row_parts),
    ("sb_w_qkv", _col_full, _col_parts), ("sb_w_o", _row_full, _row_parts),
    ("ssd_w_in", _ssd_in_full, _col_parts), ("ssd_w_out", _row_full, _row_parts),
    ("sc_w_in", _col_full, _col_parts), ("sc_w_out", _row_full, _row_parts),
)
_FIRST_WEIGHTS = ("ffn1_w_gu", "ffn1_w_down", "sb_w_qkv", "sb_w_o")
_CONV_WEIGHTS = ("ssd_conv_w", "sc_conv_w")
_REPLICATED = ("ffn1_norm", "mix_norm", "ffn2_norm", "final_norm", "ssd_conv_b", "ssd_norm",
               "ssd_dt_bias", "ssd_a_log", "ssd_d")
_ORDER = ("ffn1_norm", "ffn1_w_gu", "ffn1_w_down", "mix_norm", "ffn2_norm", "ffn2_w_gu", "ffn2_w_down",
          "sb_w_qkv", "sb_w_o", "ssd_w_in", "ssd_conv_w", "ssd_conv_b", "ssd_dt_bias", "ssd_a_log", "ssd_d",
          "ssd_norm", "ssd_w_out", "sc_w_in", "sc_conv_w", "sc_w_out", "final_norm")
_LANES = 1024


def _rows_of(a):
    flat = a.reshape(-1)
    pad = -flat.shape[0] % _LANES
    return jnp.pad(flat, (0, pad)).reshape(-1, _LANES)


def _pack_rows(arrays, mult):
    rows = [_rows_of(a) for a in arrays]
    packed = jnp.concatenate(rows, axis=0)
    pad = -packed.shape[0] % mult
    return jnp.pad(packed, ((0, pad), (0, 0))), [r.shape[0] for r in rows]


def _unpack_rows(packed, counts, shapes, lead=()):
    out, off = [], 0
    for n, shp in zip(counts, shapes):
        size = math.prod(shp)
        seg = packed[..., off:off + n, :].reshape(lead + (n * _LANES,))[..., :size]
        out.append(seg.reshape(lead + tuple(shp)))
        off += n
    return out


def kernel(x, ffn1_norm, ffn1_w_gu, ffn1_w_down, mix_norm, ffn2_norm, ffn2_w_gu, ffn2_w_down, sb_w_qkv, sb_w_o, ssd_w_in, ssd_conv_w, ssd_conv_b, ssd_dt_bias, ssd_a_log, ssd_d, ssd_norm, ssd_w_out, sc_w_in, sc_conv_w, sc_w_out, final_norm, loss_target, m_ffn1_norm, m_ffn1_w_gu, m_ffn1_w_down, m_mix_norm, m_ffn2_norm, m_ffn2_w_gu, m_ffn2_w_down, m_sb_w_qkv, m_sb_w_o, m_ssd_w_in, m_ssd_conv_w, m_ssd_conv_b, m_ssd_dt_bias, m_ssd_a_log, m_ssd_d, m_ssd_norm, m_ssd_w_out, m_sc_w_in, m_sc_conv_w, m_sc_w_out, m_final_norm, v_ffn1_norm, v_ffn1_w_gu, v_ffn1_w_down, v_mix_norm, v_ffn2_norm, v_ffn2_w_gu, v_ffn2_w_down, v_sb_w_qkv, v_sb_w_o, v_ssd_w_in, v_ssd_conv_w, v_ssd_conv_b, v_ssd_dt_bias, v_ssd_a_log, v_ssd_d, v_ssd_norm, v_ssd_w_out, v_sc_w_in, v_sc_conv_w, v_sc_w_out, v_final_norm):
    w = dict(ffn1_norm=ffn1_norm, ffn1_w_gu=ffn1_w_gu, ffn1_w_down=ffn1_w_down, mix_norm=mix_norm, ffn2_norm=ffn2_norm, ffn2_w_gu=ffn2_w_gu, ffn2_w_down=ffn2_w_down, sb_w_qkv=sb_w_qkv, sb_w_o=sb_w_o, ssd_w_in=ssd_w_in, ssd_conv_w=ssd_conv_w, ssd_conv_b=ssd_conv_b, ssd_dt_bias=ssd_dt_bias, ssd_a_log=ssd_a_log, ssd_d=ssd_d, ssd_norm=ssd_norm, ssd_w_out=ssd_w_out, sc_w_in=sc_w_in, sc_conv_w=sc_conv_w, sc_w_out=sc_w_out, final_norm=final_norm)
    mom = dict(ffn1_norm=m_ffn1_norm, ffn1_w_gu=m_ffn1_w_gu, ffn1_w_down=m_ffn1_w_down, mix_norm=m_mix_norm, ffn2_norm=m_ffn2_norm, ffn2_w_gu=m_ffn2_w_gu, ffn2_w_down=m_ffn2_w_down, sb_w_qkv=m_sb_w_qkv, sb_w_o=m_sb_w_o, ssd_w_in=m_ssd_w_in, ssd_conv_w=m_ssd_conv_w, ssd_conv_b=m_ssd_conv_b, ssd_dt_bias=m_ssd_dt_bias, ssd_a_log=m_ssd_a_log, ssd_d=m_ssd_d, ssd_norm=m_ssd_norm, ssd_w_out=m_ssd_w_out, sc_w_in=m_sc_w_in, sc_conv_w=m_sc_conv_w, sc_w_out=m_sc_w_out, final_norm=m_final_norm)
    var = dict(ffn1_norm=v_ffn1_norm, ffn1_w_gu=v_ffn1_w_gu, ffn1_w_down=v_ffn1_w_down, mix_norm=v_mix_norm, ffn2_norm=v_ffn2_norm, ffn2_w_gu=v_ffn2_w_gu, ffn2_w_down=v_ffn2_w_down, sb_w_qkv=v_sb_w_qkv, sb_w_o=v_sb_w_o, ssd_w_in=v_ssd_w_in, ssd_conv_w=v_ssd_conv_w, ssd_conv_b=v_ssd_conv_b, ssd_dt_bias=v_ssd_dt_bias, ssd_a_log=v_ssd_a_log, ssd_d=v_ssd_d, ssd_norm=v_ssd_norm, ssd_w_out=v_ssd_w_out, sc_w_in=v_sc_w_in, sc_conv_w=v_sc_conv_w, sc_w_out=v_sc_w_out, final_norm=v_final_norm)
    me = 4 * lax.axis_index("x") + 2 * lax.axis_index("y") + lax.axis_index("c")
    big = [n for n, _, _ in _MATMUL_WEIGHTS]
    two_d = lambda a: a.reshape(-1, a.shape[-1])

    to_full = {n: f for n, f, _ in _MATMUL_WEIGHTS}
    to_parts = {n: f for n, _, f in _MATMUL_WEIGHTS}
    first = [(n, 0) for n in _FIRST_WEIGHTS]
    later = [(n, i) for n in big for i in range(1 if n in _FIRST_WEIGHTS else 0, w[n].shape[0])]

    def shards(group):
        return [two_d(w[n][i].astype(BF16)) for n, i in group]

    def layers(group, gathered):
        out = {}
        for (n, i), g in zip(group, gathered):
            out.setdefault(n, []).append(to_full[n](g.reshape((N_DEV, 1) + w[n].shape[1:]))[0])
        return out

    gathered = _gather(shards(first) + [two_d(w[n]) for n in _CONV_WEIGHTS], name="gather_first")
    full = dict(w)
    full.update(layers(first, gathered))
    for n, g in zip(_CONV_WEIGHTS, gathered[len(first):]):
        full[n] = _col_full(g.reshape((N_DEV,) + w[n].shape))

    def host_of(n, i):
        if n.startswith("ffn2") and i == DEPTH - 1:
            return DEPTH - 1
        if (n.startswith("ffn1") and i == DEPTH - 1) or (n.startswith("sb_") and i == 1):
            return 1
        return 0

    gather_later = {}
    for host in (0, 1, DEPTH - 1):
        group = [(n, i) for n, i in later if host_of(n, i) == host]

        def merge(wd, gathered_group, group=group):
            wd = dict(wd)
            for n, ls in layers(group, gathered_group).items():
                have = wd[n] if isinstance(wd[n], list) else []
                wd[n] = have + ls
            return wd

        xs = shards(group)
        gather_later[host] = (_Hosted(_gather_steps, xs, _gather_shapes(xs), _GATHER_COPIES), merge)

    def chip_sums(group, grads, tag):
        parts = []
        for n, i in group:
            p8 = to_parts[n](grads[n][i][None].astype(BF16))
            parts.append(p8.reshape(4, 2, -1, p8.shape[-1]))
        from_sibling = _scatter_sibling(parts, name=f"scatter_sibling_{tag}")
        return [_pair_add(p, r, name=f"pair_add_{tag}_{n}{i}") for (n, i), p, r in zip(group, parts, from_sibling)]

    def scatter_early(grads):
        ys = chip_sums(later, grads, "later")
        return _Hosted(_chip_scatter_steps, ys, _chip_scatter_shapes(ys), _CHIP_SCATTER_COPIES)

    loss_part, dx, grads, recv_later = _local_step(x[0], loss_target[0], full, gather_later, scatter_early)
    loss = lax.psum(loss_part[0, 0], ("x", "y", "c"))

    recv_first = _scatter_chips(chip_sums(first, grads, "first"), name="scatter_chips_first")
    contrib = {n: [r] for (n, _), r in zip(first, recv_first)}
    for (n, _), r in zip(later, recv_later):
        contrib.setdefault(n, []).append(r)
    out_g, out_d, out_m, out_v = {}, {}, {}, {}

    def update(n, parts):
        res = _adamw_reduce(parts, two_d(w[n]), two_d(mom[n]), two_d(var[n]), name=f"adamw_{n}")
        out_g[n], out_d[n], out_m[n], out_v[n] = (r.reshape(w[n].shape) for r in res)

    for n in big:
        update(n, contrib[n][0] if len(contrib[n]) == 1 else jnp.concatenate(contrib[n], axis=1))

    small = list(_REPLICATED) + list(_CONV_WEIGHTS)
    small_shapes = [grads[n].shape for n in small]
    spacked, scounts = _pack_rows([grads[n].astype(F32) for n in small], 8)
    sg = _unpack_rows(_gather([spacked], name="gather_small_grads")[0], scounts, small_shapes, (N_DEV,))
    sg = dict(zip(small, sg))
    rep_w, rcounts = _pack_rows([w[n] for n in _REPLICATED], 8)
    rep_m, _ = _pack_rows([mom[n] for n in _REPLICATED], 8)
    rep_v, _ = _pack_rows([var[n] for n in _REPLICATED], 8)
    rep_p = jnp.concatenate([_rows_of(sg[n].reshape(N_DEV, -1)[q]) for q in range(N_DEV) for n in _REPLICATED], axis=0)
    rep_p = rep_p.reshape(N_DEV, -1, _LANES)
    rep_p = jnp.pad(rep_p, ((0, 0), (0, rep_w.shape[0] - rep_p.shape[1]), (0, 0)))
    res = _adamw_reduce(rep_p, rep_w, rep_m, rep_v, name="adamw_replicated")
    rep_shapes = [w[n].shape for n in _REPLICATED]
    for tgt, r in zip((out_g, out_d, out_m, out_v), res):
        for n, a in zip(_REPLICATED, _unpack_rows(r, rcounts, rep_shapes)):
            tgt[n] = a
    for n in _CONV_WEIGHTS:
        c = w[n].shape[-1]
        mine = lax.dynamic_slice_in_dim(sg[n], me * c, c, axis=sg[n].ndim - 1)
        update(n, mine.reshape(N_DEV, -1, c))

    return (loss, dx[None], *[out_g[n] for n in _ORDER], *[out_d[n] for n in _ORDER],
            *[out_m[n] for n in _ORDER], *[out_v[n] for n in _ORDER])
```

```python
import functools
import math

import jax
import jax.numpy as jnp
from jax import lax
from jax.experimental import pallas as pl
from jax.experimental.pallas import tpu as pltpu

F32 = jnp.float32
BF16 = jnp.bfloat16

D_MODEL = 1024
D_FF = 2816
DEPTH = 4
N_DEV = 8
SB_HEADS = 16
SB_HEAD_DIM = 64
SB_TILE = 256
SB_HEADS_PER_STEP = 2
SB_FWD_GROUPS = (2, 1)
SB_BWD_GROUPS = (4, 2, 1)
SSD_HEADS = 32
SSD_HEAD_DIM = 64
SSD_GROUPS = 8
SSD_HPG = 4
SSD_STATE = 128
SSD_CHUNK = 128
SSD_GROUPS_PER_STEP = 2
SSD_D_INNER = 2048
SSD_CONV_DIM = 4096
SSD_IN_DIM = 6176
SSD_IN_PAD = 6272
SSD_NORM_GROUP = 256
RMS_EPS = 1e-6
ADAM_LR = 0.001
ADAM_B1 = 0.9
ADAM_B2 = 0.999
ADAM_EPS = 1e-08
ADAM_WD = 0.01
ADAM_STEP = 10
VMEM_LIMIT = 60 * 1024 * 1024

NT = (((1,), (1,)), ((), ()))
TN = (((0,), (0,)), ((), ()))
NN = (((1,), (0,)), ((), ()))


def _cparams(n_axes):
    return pltpu.CompilerParams(dimension_semantics=("arbitrary",) * n_axes, vmem_limit_bytes=VMEM_LIMIT)


def _tile(n, want, mult=8):
    if n <= want:
        return n
    for t in range(want, 0, -1):
        if n % t == 0 and t % mult == 0:
            return t
    return n


def _sigmoid(x):
    return 1.0 / (1.0 + jnp.exp(-x))


def _dot(a, b, dn=NN):
    return lax.dot_general(a, b, dn, preferred_element_type=F32)


def _split3(x):
    x1 = x.astype(BF16)
    r1 = x - x1.astype(F32)
    x2 = r1.astype(BF16)
    x3 = (r1 - x2.astype(F32)).astype(BF16)
    return x1, x2, x3


def _dot_exact(x, t):
    x1, x2, x3 = _split3(x)
    return _dot(x1, t) + _dot(x2, t) + _dot(x3, t)


ROW_SUM_LANES = 1


def _cumsum_operand(tri):
    return jnp.concatenate([tri, tri], axis=0)


def _cumsum_rowsum(x, tri2):
    x1 = x.astype(BF16)
    x2 = (x - x1.astype(F32)).astype(BF16)
    return _dot(jnp.concatenate([x1, x2], axis=1), tri2), jnp.sum(x, axis=1, keepdims=True)


def _across_lanes(c, t):
    return c


def _mm(a, b, *, name, ta=False, tb=False, sa=False, sb=False, so=False, tm=512, tn=1024, tk=1024,
        out_dtype=F32, epilogue=None, extras=(), outs=None, pair=None, col_chunk=None):
    s_n = pair or (a.shape[0] if sa else (b.shape[0] if sb else 1))
    ash, bsh = a.shape[-2:], b.shape[-2:]
    m, k = (ash[1], ash[0]) if ta else ash
    n = bsh[0] if tb else bsh[1]
    tm, tn, tk = _tile(m, tm), _tile(n, tn, 128), _tile(k, tk, 128)
    nk = k // tk
    if outs is None:
        outs = [(out_dtype, "stile" if so else "tile")]
    if epilogue is None:
        epilogue = lambda acc: (acc,)

    if ta:
        a_blk, a_idx = (tk, tm), (lambda j, i, kk: (kk, i))
    else:
        a_blk, a_idx = (tm, tk), (lambda j, i, kk: (i, kk))
    if tb:
        b_blk, b_idx = (tn, tk), (lambda j, i, kk: (j, kk))
    else:
        b_blk, b_idx = (tk, tn), (lambda j, i, kk: (kk, j))

    def lead(blk, idx, has_s):
        if not has_s:
            return pl.BlockSpec(blk, idx)
        return pl.BlockSpec((s_n,) + blk, lambda j, i, kk: (0,) + idx(j, i, kk))

    kinds = {
        "tile": lambda: pl.BlockSpec((tm, tn), lambda j, i, kk: (i, j)),
        "stile": lambda: pl.BlockSpec((s_n, tm, tn), lambda j, i, kk: (0, i, j)),
        "row": lambda: pl.BlockSpec((1, tn), lambda j, i, kk: (0, j)),
        "colsum": lambda: pl.BlockSpec((1, tn), lambda j, i, kk: (0, j)),
    }
    shapes = {"tile": (m, n), "stile": (s_n, m, n), "colsum": (1, n)}
    in_specs = [lead(a_blk, a_idx, sa), lead(b_blk, b_idx, sb)] + [kinds[kd]() for _, kd in extras]
    out_specs = [kinds[kd]() for _, kd in outs]
    out_shape = [jax.ShapeDtypeStruct(shapes[kd], dt) for dt, kd in outs]
    n_ex, n_out = len(extras), len(outs)
    dn = ((((0,) if ta else (1,)), ((1,) if tb else (0,))), ((), ()))
    acc_shape = (s_n, tm, tn) if so else (tm, tn)

    def body(*refs):
        a_ref, b_ref = refs[0], refs[1]
        ex_refs = refs[2:2 + n_ex]
        o_refs = refs[2 + n_ex:2 + n_ex + n_out]
        i = pl.program_id(1)
        kk = pl.program_id(2)

        def products():
            for s in range(s_n if (sa or sb) else 1):
                av = (a_ref[s] if sa else a_ref[...]).astype(BF16)
                bv = (b_ref[s] if sb else b_ref[...]).astype(BF16)
                yield s, lax.dot_general(av, bv, dn, preferred_element_type=F32)

        def finish(accv):
            vals = epilogue(accv, *[r[...] for r in ex_refs])
            for (dt, kd), o_ref, val in zip(outs, o_refs, vals):
                if kd == "colsum":
                    _accumulate(o_ref, val, i == 0)
                elif kd == "stile":
                    for s in range(s_n):
                        o_ref[s] = val[s].astype(dt)
                else:
                    o_ref[...] = val.astype(dt)

        if nk == 1 and col_chunk:
            bounds = [(c0, min(col_chunk, tn - c0)) for c0 in range(0, tn, col_chunk)]
            n_s = s_n if (sa or sb) else 1
            a_vals = [(a_ref[s] if sa else a_ref[...]).astype(BF16) for s in range(n_s if sa else 1)]
            accs = []
            for c0, cw in bounds:
                ds = []
                for s in range(n_s):
                    idx = ((s,) if sb else ()) + ((pl.ds(c0, cw), slice(None)) if tb else (slice(None), pl.ds(c0, cw)))
                    ds.append(lax.dot_general(a_vals[s if sa else 0], b_ref[idx].astype(BF16), dn,
                                              preferred_element_type=F32))
                accs.append(tuple(ds) if so else functools.reduce(jnp.add, ds))
            for (c0, cw), accv in zip(bounds, accs):
                cols = pl.ds(c0, cw)
                exv = [r[:, :, cols] if kd == "stile" else r[:, cols] for r, (_, kd) in zip(ex_refs, extras)]
                vals = epilogue(accv, *exv)
                for (dt, kd), o_ref, val in zip(outs, o_refs, vals):
                    if kd == "stile":
                        for s in range(s_n):
                            o_ref[s, :, cols] = val[s].astype(dt)
                    else:
                        o_ref[:, cols] = val.astype(dt)
            return

        if nk == 1:
            ds = [d for _, d in products()]
            finish(tuple(ds) if so else functools.reduce(jnp.add, ds))
            return

        acc = refs[-1]

        @pl.when(kk == 0)
        def _():
            acc[...] = jnp.zeros_like(acc)

        for s, d in products():
            if so:
                acc[s] += d
            else:
                acc[...] += d

        @pl.when(kk == nk - 1)
        def _():
            finish(tuple(acc[s] for s in range(s_n)) if so else acc[...])

    res = pl.pallas_call(
        body, name=name, grid=(n // tn, m // tm, nk),
        in_specs=in_specs, out_specs=out_specs, out_shape=out_shape,
        scratch_shapes=[pltpu.VMEM(acc_shape, F32)] if nk > 1 else [], compiler_params=_cparams(3),
    )(a, b, *[e for e, _ in extras])
    return res[0] if len(res) == 1 else res


def _accumulate(o_ref, val, first):
    @pl.when(first)
    def _():
        o_ref[...] = val

    @pl.when(jnp.logical_not(first))
    def _():
        o_ref[...] += val


def _rmsnorm(x, g, *, name):
    l, d = x.shape
    tm = _tile(l, 512)

    def body(x_ref, g_ref, o_ref):
        xv = x_ref[...]
        r = lax.rsqrt(jnp.mean(xv * xv, axis=1, keepdims=True) + RMS_EPS)
        o_ref[...] = (xv * r * g_ref[...]).astype(BF16)

    return pl.pallas_call(
        body, name=name, grid=(l // tm,),
        in_specs=[pl.BlockSpec((tm, d), lambda i: (i, 0)), pl.BlockSpec((1, d), lambda i: (0, 0))],
        out_specs=pl.BlockSpec((tm, d), lambda i: (i, 0)),
        out_shape=jax.ShapeDtypeStruct((l, d), BF16), compiler_params=_cparams(1),
    )(x, g)


def _norm_bwd_epilogue(dh, x, g, dres):
    r = lax.rsqrt(jnp.mean(x * x, axis=1, keepdims=True) + RMS_EPS)
    xh = x * r
    dg = jnp.sum(dh * xh, axis=0, keepdims=True)
    dxh = dh * g
    dx = r * (dxh - xh * jnp.mean(dxh * xh, axis=1, keepdims=True))
    return dres + dx, dg


def _final_loss(x, g, tgt, *, name):
    l, d = x.shape
    tm = _tile(l, 512)

    def body(x_ref, g_ref, t_ref, loss_ref, dx_ref, dg_ref):
        i = pl.program_id(0)
        xv, gv = x_ref[...], g_ref[...]
        r = lax.rsqrt(jnp.mean(xv * xv, axis=1, keepdims=True) + RMS_EPS)
        xh = xv * r
        e = xh * gv - t_ref[...]
        part = 0.5 * jnp.sum(jnp.mean(e * e, axis=1, keepdims=True), axis=0, keepdims=True)
        dy = e * (1.0 / d)
        dg = jnp.sum(dy * xh, axis=0, keepdims=True)
        dxh = dy * gv
        dx_ref[...] = r * (dxh - xh * jnp.mean(dxh * xh, axis=1, keepdims=True))
        _accumulate(dg_ref, dg, i == 0)
        _accumulate(loss_ref, jnp.broadcast_to(part, (1, 128)), i == 0)

    return pl.pallas_call(
        body, name=name, grid=(l // tm,),
        in_specs=[pl.BlockSpec((tm, d), lambda i: (i, 0)), pl.BlockSpec((1, d), lambda i: (0, 0)),
                  pl.BlockSpec((tm, d), lambda i: (i, 0))],
        out_specs=[pl.BlockSpec((1, 128), lambda i: (0, 0)), pl.BlockSpec((tm, d), lambda i: (i, 0)),
                   pl.BlockSpec((1, d), lambda i: (0, 0))],
        out_shape=[jax.ShapeDtypeStruct((1, 128), F32), jax.ShapeDtypeStruct((l, d), F32),
                   jax.ShapeDtypeStruct((1, d), F32)],
        compiler_params=_cparams(1),
    )(x, g, tgt)


def _ffn_fwd(x, g, wgu, wd, tag):
    h = _rmsnorm(x, g, name=f"{tag}_norm")

    def act(acc):
        gate, up = acc
        return acc, gate * _sigmoid(gate) * up

    gu, a = _mm(h, wgu, sb=True, so=True, tm=512, tn=1408, tk=1024, name=f"{tag}_up",
                outs=[(BF16, "stile"), (BF16, "tile")], epilogue=act)
    xo = _mm(a, wd, tm=512, tn=1024, tk=2816, name=f"{tag}_down", extras=[(x, "tile")],
             epilogue=lambda acc, xt: (xt + 0.5 * acc,))
    return xo, (x, h, gu, a)


def _ffn_bwd(dout, saved, g, wgu, wd, tag):
    x, h, gu, a = saved

    def act_bwd(acc, guv):
        da = 0.5 * acc
        gate, up = guv[0].astype(F32), guv[1].astype(F32)
        s = _sigmoid(gate)
        return ((da * up * s * (1.0 + gate * (1.0 - s)), da * gate * s),)

    dgu = _mm(dout, wd, tb=True, pair=2, tm=512, tn=1408, tk=1024, col_chunk=384, name=f"{tag}_dact", extras=[(gu, "stile")],
              outs=[(BF16, "stile")], epilogue=act_bwd)
    dwd = _mm(a, dout, ta=True, tm=1408, tn=1024, tk=2048, name=f"{tag}_dwd", out_dtype=BF16,
              epilogue=lambda acc: (0.5 * acc,))
    dwgu = _mm(h, dgu, ta=True, sb=True, so=True, tm=512, tn=1408, tk=2048, name=f"{tag}_dwgu", out_dtype=BF16)
    dx, dg = _mm(dgu, wgu, tb=True, sa=True, sb=True, tm=512, tn=1024, tk=2816, name=f"{tag}_dx",
                 extras=[(x, "tile"), (g, "row"), (dout, "tile")], outs=[(F32, "tile"), (F32, "colsum")],
                 epilogue=_norm_bwd_epilogue)
    return dx, dg, dwgu, dwd


def _sb_plan(n, sizes):
    digits = [n // sizes[0]] + [(n // s) % 2 for s in sizes[1:]]
    plan, none_smaller = [], 1
    for size, d in reversed(list(zip(sizes, digits))):
        has = jnp.minimum(d, 1)
        with_diag = none_smaller * has
        plan.append((size, True, with_diag))
        if size > 1:
            plan.append((size, False, d - with_diag))
        none_smaller = none_smaller * (1 - has)
    return plan


def _sb_sweep(plan, start, step, fn, carry):
    pos = start
    for size, with_diag, trips in plan:
        diag = 0 if step < 0 else size - 1

        def trip(it, cr, size=size, with_diag=with_diag, pos=pos, diag=diag):
            base = pos + step * size * it
            return fn([base + step * b for b in range(size)], cr, [with_diag and b == diag for b in range(size)])

        carry = lax.fori_loop(0, trips, trip, carry)
        pos = pos + step * size * trips
    return carry


def _sb_logs(z):
    lb = jnp.minimum(z, 0.0) - jnp.log(1.0 + jnp.exp(-jnp.abs(z)))
    return lb, lb - z


class _Hosted:
    def __init__(self, steps, xs, out_shapes, copies):
        self.steps, self.xs, self.n, self.copies = steps, list(xs), len(xs), copies
        self.out_shape = [jax.ShapeDtypeStruct(s, x.dtype) for s, x in zip(out_shapes, xs)]
        self.specs = [_HBM] * self.n
        self.sems = [pltpu.SemaphoreType.DMA((self.n, copies)), pltpu.SemaphoreType.DMA((self.n, copies)),
                     pltpu.SemaphoreType.DMA((self.n,))]

    def run(self, x_refs, o_refs, sems, grid):
        ids = [pl.program_id(a) for a in range(len(grid))]
        first = functools.reduce(jnp.logical_and, [p == 0 for p in ids])
        last = functools.reduce(jnp.logical_and, [p == g - 1 for p, g in zip(ids, grid)])
        start, finish = self.steps(x_refs, o_refs, *sems)
        pl.when(first)(start)
        return lambda: pl.when(last)(finish)


def _head_masks(hs):
    lane = lax.broadcasted_iota(jnp.int32, (1, hs * SB_HEAD_DIM), 1)
    return [jnp.logical_and(lane >= hh * SB_HEAD_DIM, lane < (hh + 1) * SB_HEAD_DIM) for hh in range(hs)]


def _sb_fwd(qkv, *, name, hosted=None):
    l = qkv.shape[0]
    d_model = qkv.shape[1] // 3
    dh = SB_HEAD_DIM
    t = _tile(l, SB_TILE)
    hs = 2 * SB_HEADS_PER_STEP
    w = hs * dh
    n_grp = d_model // w
    scale = dh ** -0.5
    grid = (n_grp, l // t)
    nh = hosted.n if hosted else 0

    def body(q_ref, k_ref, v_ref, *rest):
        o_ref = rest[nh]
        at_end = hosted.run(rest[:nh], rest[nh + 1:2 * nh + 1], rest[2 * nh + 1:], grid) if hosted else None
        i = pl.program_id(1)
        heads = _head_masks(hs)
        q_all = (q_ref[...].astype(F32) * scale).astype(BF16)
        qs = [jnp.where(heads[hh], q_all, jnp.zeros_like(q_all)) for hh in range(hs)]
        row = lax.broadcasted_iota(jnp.int32, (t, t), 0)
        col = lax.broadcasted_iota(jnp.int32, (t, t), 1)
        strict = col < row
        tri = _cumsum_operand(strict.astype(BF16))

        def block(jbs, carry, masks):
            sls = [pl.ds(pl.multiple_of(jb * t, t), t) for jb in jbs]
            chains = [(hh, b) for b in range(len(jbs)) for hh in range(hs)]
            ks = [k_ref[sl, :] for sl in sls]
            zs = {(hh, b): _dot(qs[hh], ks[b], NT) for hh, b in chains}
            lbs, tails, sums = {}, {}, {}
            for hh, b in chains:
                lb, lk = _sb_logs(zs[hh, b])
                if masks[b]:
                    lk = jnp.where(strict, lk, 0.0)
                lbs[hh, b] = lb
                tails[hh, b], sums[hh, b] = _cumsum_rowsum(lk, tri)
            cs, o = list(carry[0]), carry[1]
            for b in range(len(jbs)):
                atts = []
                for hh in range(hs):
                    att = jnp.exp(lbs[hh, b] + tails[hh, b] + _across_lanes(cs[hh], t))
                    if masks[b]:
                        att = jnp.where(strict, att, 0.0)
                    atts.append(att.astype(BF16))
                    cs[hh] = cs[hh] + sums[hh, b]
                vb = v_ref[sls[b], :]
                v_heads = jnp.concatenate([jnp.where(heads[hh], vb, jnp.zeros_like(vb)) for hh in range(hs)], axis=0)
                o = o + _dot(jnp.concatenate(atts, axis=1), v_heads)
            return tuple(cs), o

        carry = (tuple(jnp.zeros((t, ROW_SUM_LANES), F32) for _ in range(hs)), jnp.zeros((t, w), F32))
        carry = _sb_sweep(_sb_plan(i + 1, SB_FWD_GROUPS), i, -1, block, carry)
        o_ref[...] = carry[1].astype(o_ref.dtype)
        if hosted:
            at_end()

    blocks = d_model // w
    res = pl.pallas_call(
        body, name=name, grid=grid,
        in_specs=[pl.BlockSpec((t, w), lambda g, i: (i, g)), pl.BlockSpec((l, w), lambda g, i: (0, blocks + g)),
                  pl.BlockSpec((l, w), lambda g, i: (0, 2 * blocks + g))] + (hosted.specs if hosted else []),
        out_specs=[pl.BlockSpec((t, w), lambda g, i: (i, g))] + (hosted.specs if hosted else []),
        out_shape=[jax.ShapeDtypeStruct((l, d_model), BF16)] + (hosted.out_shape if hosted else []),
        scratch_shapes=hosted.sems if hosted else [], compiler_params=_cparams(2),
    )(qkv, qkv, qkv, *(hosted.xs if hosted else []))
    return (res[0], res[1:]) if hosted else res[0]


def _sb_bwd(qkv, do, *, name, hosted=None):
    l = qkv.shape[0]
    d_model = qkv.shape[1] // 3
    dh = SB_HEAD_DIM
    t = _tile(l, SB_TILE)
    nq = l // t
    hs = SB_HEADS_PER_STEP
    w = hs * dh
    blocks = d_model // w
    scale = dh ** -0.5
    grid = (blocks, nq)
    nh = hosted.n if hosted else 0

    def body(q_ref, k_ref, v_ref, do_ref, *rest):
        dq_ref, dk_ref, dv_ref = rest[nh:nh + 3]
        e_scr, s_scr = rest[2 * nh + 3:2 * nh + 5]
        at_end = hosted.run(rest[:nh], rest[nh + 3:2 * nh + 3], rest[2 * nh + 5:], grid) if hosted else None
        i = pl.program_id(1)

        @pl.when(i == 0)
        def _():
            dk_ref[...] = jnp.zeros_like(dk_ref)
            dv_ref[...] = jnp.zeros_like(dv_ref)

        heads = _head_masks(hs)
        q_all = (q_ref[...].astype(F32) * scale).astype(BF16)
        do_all = do_ref[...]
        qs = [jnp.where(heads[hh], q_all, jnp.zeros_like(q_all)) for hh in range(hs)]
        dos = [jnp.where(heads[hh], do_all, jnp.zeros_like(do_all)) for hh in range(hs)]
        row = lax.broadcasted_iota(jnp.int32, (t, t), 0)
        col = lax.broadcasted_iota(jnp.int32, (t, t), 1)
        strict = col < row
        tri_suffix = _cumsum_operand(strict.astype(BF16))
        tri_prefix = _cumsum_operand((row < col).astype(BF16))

        def sweep1(jbs, cs, masks):
            sls = [pl.ds(pl.multiple_of(jb * t, t), t) for jb in jbs]
            chains = [(hh, b) for b in range(len(jbs)) for hh in range(hs)]
            zs = {(hh, b): _dot(qs[hh], k_ref[sls[b], :], NT) for hh, b in chains}
            datts = {(hh, b): _dot(dos[hh], v_ref[sls[b], :], NT) for hh, b in chains}
            lbs, tails, sums = {}, {}, {}
            for hh, b in chains:
                lb, lk = _sb_logs(zs[hh, b])
                if masks[b]:
                    lk = jnp.where(strict, lk, 0.0)
                lbs[hh, b] = lb
                tails[hh, b], sums[hh, b] = _cumsum_rowsum(lk, tri_suffix)
                s_scr[hh, jbs[b]] = jnp.exp(lb)
            cs = list(cs)
            for hh, b in chains:
                att = jnp.exp(lbs[hh, b] + tails[hh, b] + _across_lanes(cs[hh], t))
                if masks[b]:
                    att = jnp.where(strict, att, 0.0)
                e_scr[hh, jbs[b]] = att * datts[hh, b]
                dv_ref[sls[b], :] += _dot(att.astype(BF16), dos[hh], TN)
                cs[hh] = cs[hh] + sums[hh, b]
            return tuple(cs)

        plan = _sb_plan(i + 1, SB_BWD_GROUPS)
        _sb_sweep(plan, i, -1, sweep1, tuple(jnp.zeros((t, ROW_SUM_LANES), F32) for _ in range(hs)))

        def sweep2(jbs, carry, masks):
            sls = [pl.ds(pl.multiple_of(jb * t, t), t) for jb in jbs]
            chains = [(hh, b) for b in range(len(jbs)) for hh in range(hs)]
            des = {(hh, b): e_scr[hh, jbs[b]] for hh, b in chains}
            pres = {(hh, b): _cumsum_rowsum(des[hh, b], tri_prefix) for hh, b in chains}
            carry = [list(c) for c in carry]
            for hh, b in chains:
                p, dq = carry[hh]
                de, sg = des[hh, b], s_scr[hh, jbs[b]]
                dlk = _across_lanes(p, t) + pres[hh, b][0]
                if masks[b]:
                    dlk = jnp.where(strict, dlk, 0.0)
                dz = (de - sg * (de + dlk)).astype(BF16)
                dk_ref[sls[b], :] += _dot(dz, qs[hh], TN)
                carry[hh] = [p + pres[hh, b][1], dq + _dot(dz, k_ref[sls[b], :])]
            return tuple(tuple(c) for c in carry)

        carry = tuple((jnp.zeros((t, ROW_SUM_LANES), F32), jnp.zeros((t, w), F32)) for _ in range(hs))
        carry = _sb_sweep(plan[::-1], 0, 1, sweep2, carry)
        dq = jnp.zeros((t, w), F32)
        for hh in range(hs):
            dq = jnp.where(heads[hh], carry[hh][1], dq)
        dq_ref[...] = (dq * scale).astype(dq_ref.dtype)
        if hosted:
            at_end()

    qspec = pl.BlockSpec((t, w), lambda g, i: (i, g))
    cols = lambda off: pl.BlockSpec((l, w), lambda g, i: (0, off + g))
    res = pl.pallas_call(
        body, name=name, grid=grid,
        in_specs=[qspec, cols(blocks), cols(2 * blocks), qspec] + (hosted.specs if hosted else []),
        out_specs=[qspec, cols(0), cols(0)] + (hosted.specs if hosted else []),
        out_shape=[jax.ShapeDtypeStruct((l, d_model), BF16), jax.ShapeDtypeStruct((l, d_model), F32),
                   jax.ShapeDtypeStruct((l, d_model), F32)] + (hosted.out_shape if hosted else []),
        scratch_shapes=[pltpu.VMEM((hs, nq, t, t), F32), pltpu.VMEM((hs, nq, t, t), F32)]
        + (hosted.sems if hosted else []),
        compiler_params=_cparams(2),
    )(qkv, qkv, qkv, do, *(hosted.xs if hosted else []))
    return (res[0], res[1], res[2], res[3:]) if hosted else res


def _sb_layer_fwd(x, g, wqkv, wo, tag, hosted=None):
    h = _rmsnorm(x, g, name=f"{tag}_norm")
    qkv = _mm(h, wqkv, tm=1024, tn=1024, tk=1024, name=f"{tag}_qkv", out_dtype=BF16)
    o = _sb_fwd(qkv, name=f"{tag}_attn", hosted=hosted)
    carried = None
    if hosted:
        o, carried = o
    xo = _mm(o, wo, tm=1024, tn=1024, tk=1024, name=f"{tag}_out", extras=[(x, "tile")],
             epilogue=lambda acc, xt: (xt + acc,))
    return xo, (x, h, qkv, o), carried


def _sb_layer_bwd(dout, saved, g, wqkv, wo, tag, hosted=None):
    x, h, qkv, o = saved
    do = _mm(dout, wo, tb=True, tm=1024, tn=1024, tk=1024, name=f"{tag}_do", out_dtype=BF16)
    dwo = _mm(o, dout, ta=True, tm=1024, tn=1024, tk=2048, name=f"{tag}_dwo", out_dtype=BF16)
    res = _sb_bwd(qkv, do, name=f"{tag}_attn_bwd", hosted=hosted)
    dq, dk, dv = res[:3]
    carried = res[3] if hosted else None
    dqkv = jnp.concatenate([dq, dk.astype(BF16), dv.astype(BF16)], axis=1)
    dwqkv = _mm(h, dqkv, ta=True, tm=1024, tn=1024, tk=2048, name=f"{tag}_dwqkv", out_dtype=BF16)
    dx, dg = _mm(dqkv, wqkv, tb=True, tm=512, tn=1024, tk=3072, name=f"{tag}_dx",
                 extras=[(x, "tile"), (g, "row"), (dout, "tile")], outs=[(F32, "tile"), (F32, "colsum")],
                 epilogue=_norm_bwd_epilogue)
    return dx, dg, dwqkv, dwo, carried


def _shift_down(x, s, t_idx):
    return jnp.where(t_idx >= s, pltpu.roll(x, s, 0), 0.0)


def _shift_up(x, s, t_idx):
    n = x.shape[0]
    return jnp.where(t_idx < n - s, pltpu.roll(x, n - s, 0), 0.0)


def _sc_fwd(p, cw, *, name):
    l = p.shape[0]
    d = cw.shape[1]
    tc = 128
    nb = d // tc

    def body(b_ref, c_ref, h_ref, w_ref, o_ref):
        v = c_ref[...] * h_ref[...]
        t_idx = lax.broadcasted_iota(jnp.int32, v.shape, 0)
        u = v * w_ref[2:3, :] + _shift_down(v, 1, t_idx) * w_ref[1:2, :] + _shift_down(v, 2, t_idx) * w_ref[0:1, :]
        o_ref[...] = (b_ref[...] * u).astype(BF16)

    return pl.pallas_call(
        body, name=name, grid=(nb,),
        in_specs=[pl.BlockSpec((l, tc), lambda j: (0, j)), pl.BlockSpec((l, tc), lambda j: (0, nb + j)),
                  pl.BlockSpec((l, tc), lambda j: (0, 2 * nb + j)), pl.BlockSpec((3, tc), lambda j: (0, j))],
        out_specs=pl.BlockSpec((l, tc), lambda j: (0, j)),
        out_shape=jax.ShapeDtypeStruct((l, d), BF16), compiler_params=_cparams(1),
    )(p, p, p, cw)


def _sc_bwd(p, cw, dbu, *, name):
    l = p.shape[0]
    d = cw.shape[1]
    tc = 128
    nb = d // tc

    def body(b_ref, c_ref, h_ref, w_ref, g_ref, db_ref, dc_ref, dh_ref, dw_ref):
        cv, hv = c_ref[...], h_ref[...]
        v = cv * hv
        t_idx = lax.broadcasted_iota(jnp.int32, v.shape, 0)
        v1, v2 = _shift_down(v, 1, t_idx), _shift_down(v, 2, t_idx)
        u = v * w_ref[2:3, :] + v1 * w_ref[1:2, :] + v2 * w_ref[0:1, :]
        dbu_v = g_ref[...]
        db_ref[...] = (dbu_v * u).astype(BF16)
        du = dbu_v * b_ref[...]
        dv = du * w_ref[2:3, :] + _shift_up(du, 1, t_idx) * w_ref[1:2, :] + _shift_up(du, 2, t_idx) * w_ref[0:1, :]
        dc_ref[...] = (dv * hv).astype(BF16)
        dh_ref[...] = (dv * cv).astype(BF16)
        dw_ref[...] = jnp.zeros_like(dw_ref)
        dw_ref[0:1, :] = jnp.sum(du * v2, axis=0, keepdims=True)
        dw_ref[1:2, :] = jnp.sum(du * v1, axis=0, keepdims=True)
        dw_ref[2:3, :] = jnp.sum(du * v, axis=0, keepdims=True)

    col = lambda off: pl.BlockSpec((l, tc), lambda j: (0, off + j))
    return pl.pallas_call(
        body, name=name, grid=(nb,),
        in_specs=[col(0), col(nb), col(2 * nb), pl.BlockSpec((3, tc), lambda j: (0, j)), col(0)],
        out_specs=[col(0), col(0), col(0), pl.BlockSpec((8, tc), lambda j: (0, j))],
        out_shape=[jax.ShapeDtypeStruct((l, d), BF16)] * 3 + [jax.ShapeDtypeStruct((8, d), F32)],
        compiler_params=_cparams(1),
    )(p, p, p, cw, dbu)


def _sc_layer_fwd(x, g, win, cw, wout, tag):
    h = _rmsnorm(x, g, name=f"{tag}_norm")
    p = _mm(h, win, tm=1024, tn=1024, tk=1024, name=f"{tag}_in")
    bu = _sc_fwd(p, cw, name=f"{tag}_conv")
    xo = _mm(bu, wout, tm=1024, tn=1024, tk=1024, name=f"{tag}_out", extras=[(x, "tile")],
             epilogue=lambda acc, xt: (xt + acc,))
    return xo, (x, h, p, bu)


def _sc_layer_bwd(dout, saved, g, win, cw, wout, tag):
    x, h, p, bu = saved
    dbu = _mm(dout, wout, tb=True, tm=1024, tn=1024, tk=1024, name=f"{tag}_dbu")
    dwout = _mm(bu, dout, ta=True, tm=1024, tn=1024, tk=2048, name=f"{tag}_dwout", out_dtype=BF16)
    db, dc, dh, dcw = _sc_bwd(p, cw, dbu, name=f"{tag}_conv_bwd")
    dp = jnp.concatenate([db, dc, dh], axis=1)
    dwin = _mm(h, dp, ta=True, tm=1024, tn=1024, tk=2048, name=f"{tag}_dwin", out_dtype=BF16)
    dx, dg = _mm(dp, win, tb=True, tm=512, tn=1024, tk=3072, name=f"{tag}_dx",
                 extras=[(x, "tile"), (g, "row"), (dout, "tile")], outs=[(F32, "tile"), (F32, "colsum")],
                 epilogue=_norm_bwd_epilogue)
    return dx, dg, dwin, dcw[:3], dwout


def _ssd_conv_fwd(p, cw, cb, *, name):
    l = p.shape[0]
    tc = 128
    nb = SSD_CONV_DIM // tc
    off = SSD_D_INNER // tc

    def body(x_ref, w_ref, b_ref, o_ref):
        xv = x_ref[...]
        t_idx = lax.broadcasted_iota(jnp.int32, xv.shape, 0)
        pre = xv * w_ref[3:4, :] + b_ref[...]
        for s in (1, 2, 3):
            pre = pre + _shift_down(xv, s, t_idx) * w_ref[3 - s:4 - s, :]
        o_ref[...] = pre * _sigmoid(pre)

    return pl.pallas_call(
        body, name=name, grid=(nb,),
        in_specs=[pl.BlockSpec((l, tc), lambda j: (0, off + j)), pl.BlockSpec((4, tc), lambda j: (0, j)),
                  pl.BlockSpec((1, tc), lambda j: (0, j))],
        out_specs=pl.BlockSpec((l, tc), lambda j: (0, j)),
        out_shape=jax.ShapeDtypeStruct((l, SSD_CONV_DIM), F32), compiler_params=_cparams(1),
    )(p, cw, cb)


def _ssd_conv_bwd(p, cw, cb, dxs_scan, dxs_skip, dbm, dcm, *, name):
    l = p.shape[0]
    tc = 128
    nb = SSD_CONV_DIM // tc
    off = SSD_D_INNER // tc
    n_xs, n_b = SSD_D_INNER // tc, SSD_GROUPS * SSD_STATE // tc

    def body(x_ref, w_ref, b_ref, ga_ref, gb_ref, gm_ref, gc_ref, dx_ref, dw_ref):
        j = pl.program_id(0)
        g_val = jnp.where(j < n_xs, ga_ref[...] + gb_ref[...], jnp.where(j < n_xs + n_b, gm_ref[...], gc_ref[...]))
        xv = x_ref[...]
        t_idx = lax.broadcasted_iota(jnp.int32, xv.shape, 0)
        xs = [xv] + [_shift_down(xv, s, t_idx) for s in (1, 2, 3)]
        pre = b_ref[...] + xs[0] * w_ref[3:4, :]
        for s in (1, 2, 3):
            pre = pre + xs[s] * w_ref[3 - s:4 - s, :]
        sg = _sigmoid(pre)
        dpre = g_val * sg * (1.0 + pre * (1.0 - sg))
        dx = dpre * w_ref[3:4, :]
        for s in (1, 2, 3):
            dx = dx + _shift_up(dpre, s, t_idx) * w_ref[3 - s:4 - s, :]
        dx_ref[...] = dx
        dw_ref[...] = jnp.zeros_like(dw_ref)
        for s in (0, 1, 2, 3):
            dw_ref[3 - s:4 - s, :] = jnp.sum(dpre * xs[s], axis=0, keepdims=True)
        dw_ref[4:5, :] = jnp.sum(dpre, axis=0, keepdims=True)

    return pl.pallas_call(
        body, name=name, grid=(nb,),
        in_specs=[pl.BlockSpec((l, tc), lambda j: (0, off + j)), pl.BlockSpec((4, tc), lambda j: (0, j)),
                  pl.BlockSpec((1, tc), lambda j: (0, j)),
                  pl.BlockSpec((l, tc), lambda j: (0, jnp.minimum(j, n_xs - 1))),
                  pl.BlockSpec((l, tc), lambda j: (0, jnp.minimum(j, n_xs - 1))),
                  pl.BlockSpec((l, tc), lambda j: (0, jnp.clip(j - n_xs, 0, n_b - 1))),
                  pl.BlockSpec((l, tc), lambda j: (0, jnp.clip(j - n_xs - n_b, 0, n_b - 1)))],
        out_specs=[pl.BlockSpec((l, tc), lambda j: (0, j)), pl.BlockSpec((8, tc), lambda j: (0, j))],
        out_shape=[jax.ShapeDtypeStruct((l, SSD_CONV_DIM), F32), jax.ShapeDtypeStruct((8, SSD_CONV_DIM), F32)],
        compiler_params=_cparams(1),
    )(p, cw, cb, dxs_scan, dxs_skip, dbm, dcm)


def _ssd_dt_fwd(p, bias, *, name):
    l = p.shape[0]
    tm = _tile(l, 1024)
    off = (SSD_D_INNER + SSD_CONV_DIM) // 128

    def body(x_ref, b_ref, o_ref):
        v = x_ref[...] + b_ref[...]
        o_ref[...] = jnp.maximum(v, 0.0) + jnp.log(1.0 + jnp.exp(-jnp.abs(v)))

    return pl.pallas_call(
        body, name=name, grid=(l // tm,),
        in_specs=[pl.BlockSpec((tm, 128), lambda i: (i, off)), pl.BlockSpec((1, 128), lambda i: (0, 0))],
        out_specs=pl.BlockSpec((tm, 128), lambda i: (i, 0)),
        out_shape=jax.ShapeDtypeStruct((l, 128), F32), compiler_params=_cparams(1),
    )(p, bias)


def _ssd_dt_bwd(p, bias, ddt, *, name):
    l = p.shape[0]
    tm = _tile(l, 1024)
    off = (SSD_D_INNER + SSD_CONV_DIM) // 128

    def body(x_ref, b_ref, g_ref, o_ref, db_ref):
        i = pl.program_id(0)
        d = g_ref[...] * _sigmoid(x_ref[...] + b_ref[...])
        o_ref[...] = d
        _accumulate(db_ref, jnp.sum(d, axis=0, keepdims=True), i == 0)

    return pl.pallas_call(
        body, name=name, grid=(l // tm,),
        in_specs=[pl.BlockSpec((tm, 128), lambda i: (i, off)), pl.BlockSpec((1, 128), lambda i: (0, 0)),
                  pl.BlockSpec((tm, 128), lambda i: (i, 0))],
        out_specs=[pl.BlockSpec((tm, 128), lambda i: (i, 0)), pl.BlockSpec((1, 128), lambda i: (0, 0))],
        out_shape=[jax.ShapeDtypeStruct((l, 128), F32), jax.ShapeDtypeStruct((1, 128), F32)],
        compiler_params=_cparams(1),
    )(p, bias, ddt)


def _row_to_col(r, eye):
    return jnp.sum(jnp.where(eye, r, 0.0), axis=1, keepdims=True)


def _col_to_row(c, eye):
    return jnp.sum(jnp.where(eye, c, 0.0), axis=0, keepdims=True)


def _ssd_chunk_common(b_ref, c_ref, dt_ref, a_ref, lam_scr):
    n = SSD_CHUNK
    row = lax.broadcasted_iota(jnp.int32, (n, n), 0)
    col = lax.broadcasted_iota(jnp.int32, (n, n), 1)
    bm, cm = b_ref[...].astype(BF16), c_ref[...].astype(BF16)
    g = _dot(cm, bm, NT)
    incl = (row <= col).astype(BF16)
    lam_scr[...] = _dot_exact(dt_ref[...] * a_ref[...], incl)
    return row, col, bm, cm, g


def _ssd_head_common(r, row, col, dt_ref, lam_scr):
    eye, tril = row == col, row >= col
    lam_r = lam_scr[r:r + 1, :]
    dt_r = dt_ref[r:r + 1, :]
    lam_c = _row_to_col(lam_r, eye)
    dt_c = _row_to_col(dt_r, eye)
    dk = jnp.where(tril, jnp.exp(jnp.minimum(lam_c - lam_r, 0.0)), 0.0)
    lam_last = jnp.sum(jnp.where(col[0:1, :] == SSD_CHUNK - 1, lam_r, 0.0), axis=1, keepdims=True)
    return eye, lam_r, dt_r, lam_c, dt_c, dk, lam_last


def _ssd_fwd(xh, act, dt_t, a_b, *, name, hosted=None):
    l = xh.shape[1]
    nc = l // SSD_CHUNK
    n, p_dim, hpg = SSD_CHUNK, SSD_HEAD_DIM, SSD_HPG

    gps = SSD_GROUPS_PER_STEP
    n_grp = SSD_GROUPS // gps
    grid = (n_grp, nc)
    nh = hosted.n if hosted else 0

    def body(x_ref, b_ref, c_ref, dt_ref, a_ref, *rest):
        y_ref, hp_ref = rest[nh:nh + 2]
        h_scr, lam_scr = rest[2 * nh + 2:2 * nh + 4]
        at_end = hosted.run(rest[:nh], rest[nh + 2:2 * nh + 2], rest[2 * nh + 4:], grid) if hosted else None

        @pl.when(pl.program_id(1) == 0)
        def _():
            h_scr[...] = jnp.zeros_like(h_scr)

        lanes = [pl.ds(gg * SSD_STATE, SSD_STATE) for gg in range(gps)]
        common = [_ssd_chunk_common(b_ref.at[:, lanes[gg]], c_ref.at[:, lanes[gg]], dt_ref.at[gg], a_ref.at[gg],
                                    lam_scr.at[gg]) for gg in range(gps)]
        for gg in range(gps):
            row, col, bm, cm, g = common[gg]
            for r in range(hpg):
                hd = gg * hpg + r
                _, _, dt_r, lam_c, dt_c, dk, lam_last = _ssd_head_common(r, row, col, dt_ref.at[gg], lam_scr.at[gg])
                xr = x_ref[hd]
                hr = h_scr[hd]
                w = (g * dk * dt_r).astype(BF16)
                y = _dot(w, xr.astype(BF16)) + _dot(cm, hr.astype(BF16), NT) * jnp.exp(lam_c)
                y_ref[hd] = y
                hp_ref[hd] = hr
                xw = (xr * (jnp.exp(lam_last - lam_c) * dt_c)).astype(BF16)
                h_scr[hd] = jnp.exp(lam_last) * hr + _dot(xw, bm, TN)
        if hosted:
            at_end()

    g_off = SSD_D_INNER // (gps * SSD_STATE)
    res = pl.pallas_call(
        body, name=name, grid=grid,
        in_specs=[pl.BlockSpec((gps * hpg, n, p_dim), lambda g, c: (g, c, 0)),
                  pl.BlockSpec((n, gps * SSD_STATE), lambda g, c: (c, g_off + g)),
                  pl.BlockSpec((n, gps * SSD_STATE), lambda g, c: (c, g_off + n_grp + g)),
                  pl.BlockSpec((gps, 8, n), lambda g, c: (g, 0, c)),
                  pl.BlockSpec((gps, 8, 128), lambda g, c: (g, 0, 0))] + (hosted.specs if hosted else []),
        out_specs=[pl.BlockSpec((gps * hpg, n, p_dim), lambda g, c: (g, c, 0)),
                   pl.BlockSpec((None, gps * hpg, p_dim, SSD_STATE), lambda g, c: (c, g, 0, 0))]
        + (hosted.specs if hosted else []),
        out_shape=[jax.ShapeDtypeStruct(xh.shape, F32),
                   jax.ShapeDtypeStruct((nc, SSD_HEADS, p_dim, SSD_STATE), F32)] + (hosted.out_shape if hosted else []),
        scratch_shapes=[pltpu.VMEM((gps * hpg, p_dim, SSD_STATE), F32), pltpu.VMEM((gps, 8, n), F32)]
        + (hosted.sems if hosted else []),
        compiler_params=_cparams(2),
    )(xh, act, act, dt_t, a_b, *(hosted.xs if hosted else []))
    return (res[0], res[1], res[2:]) if hosted else res


def _ssd_bwd(xh, act, dt_t, a_b, hprev, dyh, *, name):
    l = xh.shape[1]
    nc = l // SSD_CHUNK
    n, p_dim, hpg = SSD_CHUNK, SSD_HEAD_DIM, SSD_HPG
    gps = SSD_GROUPS_PER_STEP

    def body(x_ref, b_ref, c_ref, dt_ref, a_ref, hp_ref, dy_ref,
             dx_ref, db_ref, dc_ref, ddt_ref, da_ref, dh_scr, lam_scr, dlam_scr, ddt_scr):
        ci = pl.program_id(1)

        @pl.when(ci == 0)
        def _():
            dh_scr[...] = jnp.zeros_like(dh_scr)

        lanes = [pl.ds(gg * SSD_STATE, SSD_STATE) for gg in range(gps)]
        common = [_ssd_chunk_common(b_ref.at[:, lanes[gg]], c_ref.at[:, lanes[gg]], dt_ref.at[gg], a_ref.at[gg],
                                    lam_scr.at[gg]) for gg in range(gps)]
        dlam_scr[...] = jnp.zeros_like(dlam_scr)
        ddt_scr[...] = jnp.zeros_like(ddt_scr)
        for gg in range(gps):
            row, col, bm, cm, g = common[gg]
            dt_g, lam_g, dlam_g, ddt_g = dt_ref.at[gg], lam_scr.at[gg], dlam_scr.at[gg], ddt_scr.at[gg]
            dg_acc = jnp.zeros((n, n), F32)
            dc_acc = jnp.zeros((n, SSD_STATE), F32)
            db_acc = jnp.zeros((n, SSD_STATE), F32)
            for r in range(hpg):
                hd = gg * hpg + r
                eye, _, dt_r, lam_c, dt_c, dk, lam_last = _ssd_head_common(r, row, col, dt_g, lam_g)
                xr, dyr, hr, dhr = x_ref[hd], dy_ref[hd], hp_ref[hd], dh_scr[hd]
                xb, dyb, hb, dhb = xr.astype(BF16), dyr.astype(BF16), hr.astype(BF16), dhr.astype(BF16)
                e_l = jnp.exp(lam_c)
                e_last = jnp.exp(lam_last)
                decay_c = jnp.exp(lam_last - lam_c)
                w_c = decay_c * dt_c
                m = g * dk * dt_r
                dm = _dot(dyb, xb, NT)
                bdh = _dot(bm, dhb, NT)
                dx_ref[hd] = _dot(m.astype(BF16), dyb, TN) + w_c * bdh
                dg_acc = dg_acc + dm * dk * dt_r
                q_mat = dm * g * dk
                p_mat = q_mat * dt_r
                yoff = _dot(cm, hb, NT) * e_l
                q_c = jnp.sum(xr * bdh, axis=1, keepdims=True)
                dlam_c = (jnp.sum(p_mat, axis=1, keepdims=True) + jnp.sum(dyr * yoff, axis=1, keepdims=True)
                          - w_c * q_c)
                d_last = (jnp.sum(w_c * q_c, axis=0, keepdims=True)
                          + e_last * jnp.sum(jnp.sum(dhr * hr, axis=1, keepdims=True), axis=0, keepdims=True))
                dlam_g[r:r + 1, :] = (_col_to_row(dlam_c, eye) - jnp.sum(p_mat, axis=0, keepdims=True)
                                      + jnp.where(col[0:1, :] == n - 1, d_last, 0.0))
                ddt_g[r:r + 1, :] = jnp.sum(q_mat, axis=0, keepdims=True) + _col_to_row(decay_c * q_c, eye)
                dc_acc = dc_acc + e_l * _dot(dyb, hb)
                db_acc = db_acc + _dot((xr * w_c).astype(BF16), dhb)
                dh_scr[hd] = e_last * dhr + _dot((dyr * e_l).astype(BF16), cm, TN)

            dgb = dg_acc.astype(BF16)
            dc_ref[:, lanes[gg]] = _dot(dgb, bm) + dc_acc
            db_ref[:, lanes[gg]] = _dot(dgb, cm, TN) + db_acc
            rev = (row >= col).astype(BF16)
            da = _dot_exact(dlam_g[...], rev)
            ddt_ref[gg] = ddt_g[...] + da * a_ref[gg]
            _accumulate(da_ref.at[gg], da * dt_g[...], ci == 0)

        @pl.when(ci == nc - 1)
        def _():
            for gg in range(gps):
                da_ref[gg] = jnp.broadcast_to(jnp.sum(da_ref[gg], axis=1, keepdims=True), da_ref.shape[1:])

    g_off = SSD_D_INNER // (gps * SSD_STATE)
    n_grp = SSD_GROUPS // gps
    rc = lambda c: nc - 1 - c
    hspec = pl.BlockSpec((gps * hpg, n, p_dim), lambda g, c: (g, rc(c), 0))
    gspec = pl.BlockSpec((n, gps * SSD_STATE), lambda g, c: (rc(c), g))
    return pl.pallas_call(
        body, name=name, grid=(n_grp, nc),
        in_specs=[hspec,
                  pl.BlockSpec((n, gps * SSD_STATE), lambda g, c: (rc(c), g_off + g)),
                  pl.BlockSpec((n, gps * SSD_STATE), lambda g, c: (rc(c), g_off + n_grp + g)),
                  pl.BlockSpec((gps, 8, n), lambda g, c: (g, 0, rc(c))),
                  pl.BlockSpec((gps, 8, 128), lambda g, c: (g, 0, 0)),
                  pl.BlockSpec((None, gps * hpg, p_dim, SSD_STATE), lambda g, c: (rc(c), g, 0, 0)),
                  hspec],
        out_specs=[hspec, gspec, gspec,
                   pl.BlockSpec((gps, 8, n), lambda g, c: (g, 0, rc(c))),
                   pl.BlockSpec((gps, 8, 128), lambda g, c: (g, 0, 0))],
        out_shape=[jax.ShapeDtypeStruct(xh.shape, F32),
                   jax.ShapeDtypeStruct((l, SSD_GROUPS * SSD_STATE), F32),
                   jax.ShapeDtypeStruct((l, SSD_GROUPS * SSD_STATE), F32),
                   jax.ShapeDtypeStruct(dt_t.shape, F32),
                   jax.ShapeDtypeStruct(a_b.shape, F32)],
        scratch_shapes=[pltpu.VMEM((gps * hpg, p_dim, SSD_STATE), F32), pltpu.VMEM((gps, 8, n), F32),
                        pltpu.VMEM((gps, 8, n), F32), pltpu.VMEM((gps, 8, n), F32)],
        compiler_params=_cparams(2),
    )(xh, act, act, dt_t, a_b, hprev, dyh)


def _ssd_gate_fwd(y, act, p, d_vec, gn, *, name):
    l = y.shape[0]
    w = SSD_D_INNER
    tm = _tile(l, 256)

    def body(y_ref, xs_ref, z_ref, d_ref, g_ref, o_ref):
        for gi in range(SSD_GROUPS):
            sl = slice(gi * SSD_NORM_GROUP, (gi + 1) * SSD_NORM_GROUP)
            z = z_ref[:, sl]
            y2 = (y_ref[:, sl] + d_ref[:, sl] * xs_ref[:, sl]) * (z * _sigmoid(z))
            r = lax.rsqrt(jnp.mean(y2 * y2, axis=1, keepdims=True) + RMS_EPS)
            o_ref[:, sl] = (y2 * r * g_ref[:, sl]).astype(BF16)

    rows = pl.BlockSpec((tm, w), lambda i: (i, 0))
    vec = pl.BlockSpec((1, w), lambda i: (0, 0))
    return pl.pallas_call(
        body, name=name, grid=(l // tm,), in_specs=[rows, rows, rows, vec, vec], out_specs=rows,
        out_shape=jax.ShapeDtypeStruct((l, w), BF16), compiler_params=_cparams(1),
    )(y, act, p, d_vec, gn)


def _ssd_gate_bwd(dyn, y, act, p, d_vec, gn, *, name):
    l = y.shape[0]
    w = SSD_D_INNER
    tm = _tile(l, 256)

    def body(dyn_ref, y_ref, xs_ref, z_ref, d_ref, g_ref, dy_ref, dz_ref, dxs_ref, dd_ref, dg_ref):
        i = pl.program_id(0)
        for gi in range(SSD_GROUPS):
            sl = slice(gi * SSD_NORM_GROUP, (gi + 1) * SSD_NORM_GROUP)
            z, xs, dv = z_ref[:, sl], xs_ref[:, sl], d_ref[:, sl]
            s = _sigmoid(z)
            sz = z * s
            y1 = y_ref[:, sl] + dv * xs
            y2 = y1 * sz
            r = lax.rsqrt(jnp.mean(y2 * y2, axis=1, keepdims=True) + RMS_EPS)
            y2h = y2 * r
            dyn_v = dyn_ref[:, sl]
            d2h = dyn_v * g_ref[:, sl]
            dy2 = r * (d2h - y2h * jnp.mean(d2h * y2h, axis=1, keepdims=True))
            dy1 = dy2 * sz
            dy_ref[:, sl] = dy1
            dz_ref[:, sl] = dy2 * y1 * s * (1.0 + z * (1.0 - s))
            dxs_ref[:, sl] = dv * dy1
            _accumulate(dd_ref.at[:, sl], jnp.sum(dy1 * xs, axis=0, keepdims=True), i == 0)
            _accumulate(dg_ref.at[:, sl], jnp.sum(dyn_v * y2h, axis=0, keepdims=True), i == 0)

    rows = pl.BlockSpec((tm, w), lambda i: (i, 0))
    vec = pl.BlockSpec((1, w), lambda i: (0, 0))
    return pl.pallas_call(
        body, name=name, grid=(l // tm,), in_specs=[rows, rows, rows, rows, vec, vec],
        out_specs=[rows, rows, rows, vec, vec],
        out_shape=[jax.ShapeDtypeStruct((l, w), F32)] * 3 + [jax.ShapeDtypeStruct((1, w), F32)] * 2,
        compiler_params=_cparams(1),
    )(dyn, y, act, p, d_vec, gn)


def _heads_major(x):
    return x.reshape(x.shape[0], SSD_HEADS, SSD_HEAD_DIM).transpose(1, 0, 2)


def _ssd_layer_fwd(x, g, win, cw, cb, dt_bias, a_log, d_skip, gn, wout, tag, hosted=None):
    l = x.shape[0]
    h = _rmsnorm(x, g, name=f"{tag}_norm")
    p = _mm(h, win, tm=1024, tn=896, tk=1024, name=f"{tag}_in")
    act = _ssd_conv_fwd(p, cw, cb, name=f"{tag}_conv")
    bias = jnp.pad(dt_bias, (0, 128 - SSD_HEADS)).reshape(1, 128)
    dt = _ssd_dt_fwd(p, bias, name=f"{tag}_dt")
    xh = _heads_major(act[:, :SSD_D_INNER])
    dt_t = jnp.pad(dt[:, :SSD_HEADS].T.reshape(SSD_GROUPS, SSD_HPG, l), ((0, 0), (0, 8 - SSD_HPG), (0, 0)))
    a = -jnp.exp(a_log).reshape(SSD_GROUPS, SSD_HPG, 1)
    a_b = jnp.broadcast_to(jnp.pad(a, ((0, 0), (0, 8 - SSD_HPG), (0, 0))), (SSD_GROUPS, 8, 128))
    res = _ssd_fwd(xh, act, dt_t, a_b, name=f"{tag}_scan", hosted=hosted)
    yh, hprev = res[:2]
    carried = res[2] if hosted else None
    y = yh.transpose(1, 0, 2).reshape(l, SSD_D_INNER)
    d_vec = jnp.repeat(d_skip, SSD_HEAD_DIM).reshape(1, SSD_D_INNER)
    yn = _ssd_gate_fwd(y, act, p, d_vec, gn, name=f"{tag}_gate")
    xo = _mm(yn, wout, tm=1024, tn=1024, tk=2048, name=f"{tag}_out", extras=[(x, "tile")],
             epilogue=lambda acc, xt: (xt + acc,))
    return xo, (x, h, p, act, bias, xh, dt_t, a_b, hprev, y, d_vec, yn), carried


def _ssd_layer_bwd(dout, saved, g, win, cw, cb, gn, wout, tag):
    x, h, p, act, bias, xh, dt_t, a_b, hprev, y, d_vec, yn = saved
    l = x.shape[0]
    dyn = _mm(dout, wout, tb=True, tm=1024, tn=1024, tk=1024, name=f"{tag}_dyn")
    dwout = _mm(yn, dout, ta=True, tm=1024, tn=1024, tk=2048, name=f"{tag}_dwout", out_dtype=BF16)
    dy, dz, dxs_d, dd_vec, dgn = _ssd_gate_bwd(dyn, y, act, p, d_vec, gn, name=f"{tag}_gate_bwd")
    dxh, dbm, dcm, ddt_t, da_b = _ssd_bwd(xh, act, dt_t, a_b, hprev, _heads_major(dy), name=f"{tag}_scan_bwd")
    dxs_scan = dxh.transpose(1, 0, 2).reshape(l, SSD_D_INNER)
    dxbc, dcw8 = _ssd_conv_bwd(p, cw, cb, dxs_scan, dxs_d, dbm, dcm, name=f"{tag}_conv_bwd")
    ddt = jnp.pad(ddt_t[:, :SSD_HPG, :].reshape(SSD_HEADS, l).T, ((0, 0), (0, 128 - SSD_HEADS)))
    ddt_raw, dbias = _ssd_dt_bwd(p, bias, ddt, name=f"{tag}_dt_bwd")
    dp = jnp.concatenate([dz, dxbc, ddt_raw], axis=1)
    dwin = _mm(h, dp, ta=True, tm=1024, tn=896, tk=2048, name=f"{tag}_dwin", out_dtype=BF16)
    dx, dg = _mm(dp, win, tb=True, tm=256, tn=1024, tk=6272, name=f"{tag}_dx",
                 extras=[(x, "tile"), (g, "row"), (dout, "tile")], outs=[(F32, "tile"), (F32, "colsum")],
                 epilogue=_norm_bwd_epilogue)
    a_heads = a_b[:, :SSD_HPG, 0].reshape(SSD_HEADS)
    grads = dict(
        ssd_w_in=dwin[:, :SSD_IN_DIM], ssd_conv_w=dcw8[:4], ssd_conv_b=dcw8[4],
        ssd_dt_bias=dbias[0, :SSD_HEADS], ssd_a_log=da_b[:, :SSD_HPG, 0].reshape(SSD_HEADS) * a_heads,
        ssd_d=dd_vec.reshape(SSD_HEADS, SSD_HEAD_DIM).sum(axis=1), ssd_norm=dgn[0], ssd_w_out=dwout)
    return dx, dg, grads


def _local_step(x, tgt, w, gather_later=None, scatter_early=None):
    row = lambda v: v.reshape(1, -1)
    saved = []
    for i in range(DEPTH):
        kind, j = i % 3, i // 3
        x, s1 = _ffn_fwd(x, row(w["ffn1_norm"][i]), w["ffn1_w_gu"][i], w["ffn1_w_down"][i], f"l{i}f1")
        gm = row(w["mix_norm"][i])
        hook = (gather_later or {}).get(i)
        hosted = hook[0] if hook else None
        if kind == 0:
            x, sm, carried = _sb_layer_fwd(x, gm, w["sb_w_qkv"][j], w["sb_w_o"][j], f"l{i}sb", hosted=hosted)
        elif kind == 1:
            x, sm, carried = _ssd_layer_fwd(x, gm, w["ssd_w_in"][j], w["ssd_conv_w"][j], row(w["ssd_conv_b"][j]),
                                            w["ssd_dt_bias"][j], w["ssd_a_log"][j], w["ssd_d"][j],
                                            row(w["ssd_norm"][j]), w["ssd_w_out"][j], f"l{i}ssd", hosted=hosted)
        if hook:
            w = hook[1](w, carried)
        if kind == 2:
            x, sm = _sc_layer_fwd(x, gm, w["sc_w_in"][j], w["sc_conv_w"][j], w["sc_w_out"][j], f"l{i}sc")
        x, s2 = _ffn_fwd(x, row(w["ffn2_norm"][i]), w["ffn2_w_gu"][i], w["ffn2_w_down"][i], f"l{i}f2")
        saved.append((s1, sm, s2))

    loss, dx, dfinal = _final_loss(x, row(w["final_norm"]), tgt, name="final_loss")
    per_layer = {k: [None] * DEPTH for k in ("ffn1_norm", "ffn1_w_gu", "ffn1_w_down", "mix_norm",
                                             "ffn2_norm", "ffn2_w_gu", "ffn2_w_down")}
    per_layer.update({"sb_w_qkv": [None, None], "sb_w_o": [None, None]})
    grads = {"final_norm": dfinal[0]}
    early = None
    for i in reversed(range(DEPTH)):
        kind, j = i % 3, i // 3
        s1, sm, s2 = saved[i]
        dx, dg, dwgu, dwd = _ffn_bwd(dx, s2, row(w["ffn2_norm"][i]), w["ffn2_w_gu"][i], w["ffn2_w_down"][i], f"l{i}f2")
        per_layer["ffn2_norm"][i], per_layer["ffn2_w_gu"][i], per_layer["ffn2_w_down"][i] = dg[0], dwgu, dwd
        gm = row(w["mix_norm"][i])
        if kind == 0:
            hosted = scatter_early({**grads, **per_layer}) if (scatter_early and i == 0) else None
            dx, dg, dwqkv, dwo, carried = _sb_layer_bwd(dx, sm, gm, w["sb_w_qkv"][j], w["sb_w_o"][j], f"l{i}sb",
                                                        hosted=hosted)
            per_layer["sb_w_qkv"][j], per_layer["sb_w_o"][j] = dwqkv, dwo
            if hosted:
                early = carried
        elif kind == 1:
            dx, dg, sg = _ssd_layer_bwd(dx, sm, gm, w["ssd_w_in"][j], w["ssd_conv_w"][j], row(w["ssd_conv_b"][j]),
                                        row(w["ssd_norm"][j]), w["ssd_w_out"][j], f"l{i}ssd")
            sg["ssd_w_in"], sg["ssd_w_out"] = [sg["ssd_w_in"]], [sg["ssd_w_out"]]
            grads.update({k: (v if isinstance(v, list) else v[None]) for k, v in sg.items()})
        else:
            dx, dg, dwin, dcw, dwout = _sc_layer_bwd(dx, sm, gm, w["sc_w_in"][j], w["sc_conv_w"][j],
                                                     w["sc_w_out"][j], f"l{i}sc")
            grads.update(sc_w_in=[dwin], sc_conv_w=dcw[None], sc_w_out=[dwout])
        per_layer["mix_norm"][i] = dg[0]
        dx, dg, dwgu, dwd = _ffn_bwd(dx, s1, row(w["ffn1_norm"][i]), w["ffn1_w_gu"][i], w["ffn1_w_down"][i], f"l{i}f1")
        per_layer["ffn1_norm"][i], per_layer["ffn1_w_gu"][i], per_layer["ffn1_w_down"][i] = dg[0], dwgu, dwd
    for k, v in per_layer.items():
        grads[k] = jnp.stack(v) if k.endswith("_norm") else v
    return loss, dx, grads, early


_HBM = pl.BlockSpec(memory_space=pltpu.HBM)


def _remote(src, dst, send_sems, recv_sems, idx, dev):
    return pltpu.make_async_remote_copy(src_ref=src, dst_ref=dst, send_sem=send_sems.at[idx], recv_sem=recv_sems.at[idx],
                                        device_id=dev, device_id_type=pl.DeviceIdType.MESH)


def _exchange_call(body, xs, out_shapes, n_copies, name):
    n = len(xs)
    return pl.pallas_call(
        body, name=name, in_specs=[_HBM] * n, out_specs=[_HBM] * n,
        out_shape=[jax.ShapeDtypeStruct(s, x.dtype) for s, x in zip(out_shapes, xs)],
        scratch_shapes=[pltpu.SemaphoreType.DMA((n, n_copies)), pltpu.SemaphoreType.DMA((n, n_copies)),
                        pltpu.SemaphoreType.DMA((n,))],
    )(*xs)


def _gather(xs, *, name):
    n = len(xs)

    def body(*refs):
        start, finish = _gather_steps(refs[:n], refs[n:2 * n], *refs[2 * n:])
        start()
        finish()

    return _exchange_call(body, xs, _gather_shapes(xs), _GATHER_COPIES, name)


_GATHER_COPIES = 7


def _gather_shapes(xs):
    return [(N_DEV,) + x.shape for x in xs]


def _gather_steps(x_refs, o_refs, send_sems, recv_sems, local_sems):
    n = len(x_refs)

    def plan():
        mx, my, mc = lax.axis_index("x"), lax.axis_index("y"), lax.axis_index("c")
        slot = lambda px, py, pc: 4 * px + 2 * py + pc
        me, sibling = (mx, my, mc), (mx, my, 1 - mc)
        chips = [(1 - mx, my), (mx, 1 - my), (1 - mx, 1 - my)]
        locals_, first = [], []
        for a in range(n):
            x_ref, o_ref = x_refs[a], o_refs[a]
            locals_.append(pltpu.make_async_copy(x_ref, o_ref.at[slot(*me)], local_sems.at[a]))
            first.append(_remote(x_ref, o_ref.at[slot(*me)], send_sems, recv_sems, (a, 0), sibling))
            for j, chip in enumerate(chips):
                first.append(_remote(x_ref, o_ref.at[slot(*me)], send_sems, recv_sems, (a, 1 + j), (*chip, mc)))
        return locals_, first, slot, me, sibling, chips, mc

    def start():
        locals_, first = plan()[:2]
        for cp in locals_ + first:
            cp.start()

    def finish():
        locals_, first, slot, me, sibling, chips, mc = plan()
        passed = []
        for j, chip in enumerate(chips):
            for a in range(n):
                landed = o_refs[a].at[slot(*chip, mc)]
                _remote(landed, landed, send_sems, recv_sems, (a, 1 + j), me).wait_recv()
                fwd = _remote(landed, landed, send_sems, recv_sems, (a, 4 + j), sibling)
                fwd.start()
                passed.append(fwd)
        for a in range(n):
            from_sib = o_refs[a].at[slot(*sibling)]
            _remote(from_sib, from_sib, send_sems, recv_sems, (a, 0), me).wait_recv()
            for j, chip in enumerate(chips):
                via_sib = o_refs[a].at[slot(*chip, 1 - mc)]
                _remote(via_sib, via_sib, send_sems, recv_sems, (a, 4 + j), me).wait_recv()
        for cp in first + passed:
            cp.wait_send()
        for cp in locals_:
            cp.wait()

    return start, finish


def _scatter_sibling(xs, *, name):
    n = len(xs)

    def body(*refs):
        x_refs, o_refs = refs[:n], refs[n:2 * n]
        send_sems, recv_sems, _ = refs[2 * n:]
        mx, my, mc = lax.axis_index("x"), lax.axis_index("y"), lax.axis_index("c")
        sibling = (mx, my, 1 - mc)
        sends = []
        for a in range(n):
            for ch in range(4):
                sends.append(_remote(x_refs[a].at[ch, 1 - mc], o_refs[a].at[ch], send_sems, recv_sems, (a, ch), sibling))
        for cp in sends:
            cp.start()
        for cp in sends:
            cp.wait_recv()
        for cp in sends:
            cp.wait_send()

    return _exchange_call(body, xs, [(4,) + x.shape[2:] for x in xs], 4, name)


def _scatter_chips(ys, *, name):
    n = len(ys)

    def body(*refs):
        start, finish = _chip_scatter_steps(refs[:n], refs[n:2 * n], *refs[2 * n:])
        start()
        finish()

    return _exchange_call(body, ys, _chip_scatter_shapes(ys), _CHIP_SCATTER_COPIES, name)


_CHIP_SCATTER_COPIES = 3


def _chip_scatter_shapes(ys):
    return [y.shape for y in ys]


def _chip_scatter_steps(y_refs, o_refs, send_sems, recv_sems, local_sems):
    n = len(y_refs)

    def plan():
        mx, my, mc = lax.axis_index("x"), lax.axis_index("y"), lax.axis_index("c")
        mine = 2 * mx + my
        chips = [(1 - mx, my), (mx, 1 - my), (1 - mx, 1 - my)]
        locals_, sends, recvs = [], [], []
        for a in range(n):
            locals_.append(pltpu.make_async_copy(y_refs[a].at[mine], o_refs[a].at[mine], local_sems.at[a]))
            for j, (px, py) in enumerate(chips):
                theirs = 2 * px + py
                sends.append(_remote(y_refs[a].at[theirs], o_refs[a].at[mine], send_sems, recv_sems, (a, j), (px, py, mc)))
                recvs.append(_remote(y_refs[a].at[theirs], o_refs[a].at[theirs], send_sems, recv_sems, (a, j), (px, py, mc)))
        return locals_, sends, recvs

    def start():
        locals_, sends, _ = plan()
        for cp in locals_ + sends:
            cp.start()

    def finish():
        locals_, sends, recvs = plan()
        for cp in recvs:
            cp.wait_recv()
        for cp in sends:
            cp.wait_send()
        for cp in locals_:
            cp.wait()

    return start, finish


def _pair_add(x, r, *, name):
    _, _, rows, c = x.shape
    tr = _tile(rows, 512, 16)

    def body(core_ref, x_ref, r_ref, o_ref):
        o_ref[...] = (x_ref[...].astype(F32) + r_ref[...].astype(F32)).astype(o_ref.dtype)

    core = lax.axis_index("c").astype(jnp.int32).reshape(1)
    return pl.pallas_call(
        body, name=name,
        grid_spec=pltpu.PrefetchScalarGridSpec(
            num_scalar_prefetch=1, grid=(4, rows // tr),
            in_specs=[pl.BlockSpec((None, None, tr, c), lambda ch, i, core: (ch, core[0], i, 0)),
                      pl.BlockSpec((None, tr, c), lambda ch, i, core: (ch, i, 0))],
            out_specs=pl.BlockSpec((None, tr, c), lambda ch, i, core: (ch, i, 0))),
        out_shape=jax.ShapeDtypeStruct((4, rows, c), x.dtype), compiler_params=_cparams(2),
    )(core, x, r)


def _adamw_reduce(parts, w, m, v, *, name):
    r, c = w.shape
    n_parts = parts.shape[0]
    tr = _tile(r, 256, 16)
    bc1 = 1.0 - ADAM_B1 ** ADAM_STEP
    bc2 = 1.0 - ADAM_B2 ** ADAM_STEP

    def body(p_ref, w_ref, m_ref, v_ref, g_ref, d_ref, nm_ref, nv_ref):
        g = p_ref[0].astype(F32)
        for q in range(1, n_parts):
            g = g + p_ref[q].astype(F32)
        nm = ADAM_B1 * m_ref[...] + (1.0 - ADAM_B1) * g
        nv = ADAM_B2 * v_ref[...] + (1.0 - ADAM_B2) * (g * g)
        g_ref[...] = g
        nm_ref[...] = nm
        nv_ref[...] = nv
        d_ref[...] = -ADAM_LR * ((nm / bc1) / (jnp.sqrt(nv / bc2) + ADAM_EPS) + ADAM_WD * w_ref[...])

    blk = pl.BlockSpec((tr, c), lambda i: (i, 0))
    return pl.pallas_call(
        body, name=name, grid=(r // tr,),
        in_specs=[pl.BlockSpec((n_parts, tr, c), lambda i: (0, i, 0)), blk, blk, blk], out_specs=[blk] * 4,
        out_shape=[jax.ShapeDtypeStruct((r, c), F32)] * 4, compiler_params=_cparams(1),
    )(parts, w, m, v)


def _col_full(g):
    return g.transpose(1, 2, 0, 3).reshape(g.shape[1], g.shape[2], -1)


def _col_parts(f):
    n, k, c8 = f.shape
    return f.reshape(n, k, N_DEV, c8 // N_DEV).transpose(2, 0, 1, 3)


def _row_full(g):
    return g.transpose(1, 0, 2, 3).reshape(g.shape[1], -1, g.shape[3])


def _row_parts(f):
    n, r8, c = f.shape
    return f.reshape(n, N_DEV, r8 // N_DEV, c).transpose(1, 0, 2, 3)


def _gu_full(g):
    n, d, c = g.shape[1:]
    return g.reshape(2, 4, n, d, c).transpose(2, 0, 3, 1, 4).reshape(n, 2, d, 4 * c)


def _gu_parts(f):
    n, _, d, c4 = f.shape
    return f.reshape(n, 2, d, 4, c4 // 4).transpose(1, 3, 0, 2, 4).reshape(N_DEV, n, d, c4 // 4)


def _ssd_in_full(g):
    return jnp.pad(_col_full(g), ((0, 0), (0, 0), (0, SSD_IN_PAD - SSD_IN_DIM)))


_MATMUL_WEIGHTS = (
    ("ffn1_w_gu", _gu_full, _gu_parts), ("ffn1_w_down", _row_full, _row_parts),
    ("ffn2_w_gu", _gu_full, _gu_parts), ("ffn2_w_down", _row_full, _row_parts),
    ("sb_w_qkv", _col_full, _col_parts), ("sb_w_o", _row_full, _row_parts),
    ("ssd_w_in", _ssd_in_full, _col_parts), ("ssd_w_out", _row_full, _row_parts),
    ("sc_w_in", _col_full, _col_parts), ("sc_w_out", _row_full, _row_parts),
)
_FIRST_WEIGHTS = ("ffn1_w_gu", "ffn1_w_down", "sb_w_qkv", "sb_w_o")
_CONV_WEIGHTS = ("ssd_conv_w", "sc_conv_w")
_REPLICATED = ("ffn1_norm", "mix_norm", "ffn2_norm", "final_norm", "ssd_conv_b", "ssd_norm",
               "ssd_dt_bias", "ssd_a_log", "ssd_d")
_ORDER = ("ffn1_norm", "ffn1_w_gu", "ffn1_w_down", "mix_norm", "ffn2_norm", "ffn2_w_gu", "ffn2_w_down",
          "sb_w_qkv", "sb_w_o", "ssd_w_in", "ssd_conv_w", "ssd_conv_b", "ssd_dt_bias", "ssd_a_log", "ssd_d",
          "ssd_norm", "ssd_w_out", "sc_w_in", "sc_conv_w", "sc_w_out", "final_norm")
_LANES = 1024


def _rows_of(a):
    flat = a.reshape(-1)
    pad = -flat.shape[0] % _LANES
    return jnp.pad(flat, (0, pad)).reshape(-1, _LANES)


def _pack_rows(arrays, mult):
    rows = [_rows_of(a) for a in arrays]
    packed = jnp.concatenate(rows, axis=0)
    pad = -packed.shape[0] % mult
    return jnp.pad(packed, ((0, pad), (0, 0))), [r.shape[0] for r in rows]


def _unpack_rows(packed, counts, shapes, lead=()):
    out, off = [], 0
    for n, shp in zip(counts, shapes):
        size = math.prod(shp)
        seg = packed[..., off:off + n, :].reshape(lead + (n * _LANES,))[..., :size]
        out.append(seg.reshape(lead + tuple(shp)))
        off += n
    return out


def kernel(x, ffn1_norm, ffn1_w_gu, ffn1_w_down, mix_norm, ffn2_norm, ffn2_w_gu, ffn2_w_down, sb_w_qkv, sb_w_o, ssd_w_in, ssd_conv_w, ssd_conv_b, ssd_dt_bias, ssd_a_log, ssd_d, ssd_norm, ssd_w_out, sc_w_in, sc_conv_w, sc_w_out, final_norm, loss_target, m_ffn1_norm, m_ffn1_w_gu, m_ffn1_w_down, m_mix_norm, m_ffn2_norm, m_ffn2_w_gu, m_ffn2_w_down, m_sb_w_qkv, m_sb_w_o, m_ssd_w_in, m_ssd_conv_w, m_ssd_conv_b, m_ssd_dt_bias, m_ssd_a_log, m_ssd_d, m_ssd_norm, m_ssd_w_out, m_sc_w_in, m_sc_conv_w, m_sc_w_out, m_final_norm, v_ffn1_norm, v_ffn1_w_gu, v_ffn1_w_down, v_mix_norm, v_ffn2_norm, v_ffn2_w_gu, v_ffn2_w_down, v_sb_w_qkv, v_sb_w_o, v_ssd_w_in, v_ssd_conv_w, v_ssd_conv_b, v_ssd_dt_bias, v_ssd_a_log, v_ssd_d, v_ssd_norm, v_ssd_w_out, v_sc_w_in, v_sc_conv_w, v_sc_w_out, v_final_norm):
    w = dict(ffn1_norm=ffn1_norm, ffn1_w_gu=ffn1_w_gu, ffn1_w_down=ffn1_w_down, mix_norm=mix_norm, ffn2_norm=ffn2_norm, ffn2_w_gu=ffn2_w_gu, ffn2_w_down=ffn2_w_down, sb_w_qkv=sb_w_qkv, sb_w_o=sb_w_o, ssd_w_in=ssd_w_in, ssd_conv_w=ssd_conv_w, ssd_conv_b=ssd_conv_b, ssd_dt_bias=ssd_dt_bias, ssd_a_log=ssd_a_log, ssd_d=ssd_d, ssd_norm=ssd_norm, ssd_w_out=ssd_w_out, sc_w_in=sc_w_in, sc_conv_w=sc_conv_w, sc_w_out=sc_w_out, final_norm=final_norm)
    mom = dict(ffn1_norm=m_ffn1_norm, ffn1_w_gu=m_ffn1_w_gu, ffn1_w_down=m_ffn1_w_down, mix_norm=m_mix_norm, ffn2_norm=m_ffn2_norm, ffn2_w_gu=m_ffn2_w_gu, ffn2_w_down=m_ffn2_w_down, sb_w_qkv=m_sb_w_qkv, sb_w_o=m_sb_w_o, ssd_w_in=m_ssd_w_in, ssd_conv_w=m_ssd_conv_w, ssd_conv_b=m_ssd_conv_b, ssd_dt_bias=m_ssd_dt_bias, ssd_a_log=m_ssd_a_log, ssd_d=m_ssd_d, ssd_norm=m_ssd_norm, ssd_w_out=m_ssd_w_out, sc_w_in=m_sc_w_in, sc_conv_w=m_sc_conv_w, sc_w_out=m_sc_w_out, final_norm=m_final_norm)
    var = dict(ffn1_norm=v_ffn1_norm, ffn1_w_gu=v_ffn1_w_gu, ffn1_w_down=v_ffn1_w_down, mix_norm=v_mix_norm, ffn2_norm=v_ffn2_norm, ffn2_w_gu=v_ffn2_w_gu, ffn2_w_down=v_ffn2_w_down, sb_w_qkv=v_sb_w_qkv, sb_w_o=v_sb_w_o, ssd_w_in=v_ssd_w_in, ssd_conv_w=v_ssd_conv_w, ssd_conv_b=v_ssd_conv_b, ssd_dt_bias=v_ssd_dt_bias, ssd_a_log=v_ssd_a_log, ssd_d=v_ssd_d, ssd_norm=v_ssd_norm, ssd_w_out=v_ssd_w_out, sc_w_in=v_sc_w_in, sc_conv_w=v_sc_conv_w, sc_w_out=v_sc_w_out, final_norm=v_final_norm)
    me = 4 * lax.axis_index("x") + 2 * lax.axis_index("y") + lax.axis_index("c")
    big = [n for n, _, _ in _MATMUL_WEIGHTS]
    two_d = lambda a: a.reshape(-1, a.shape[-1])

    to_full = {n: f for n, f, _ in _MATMUL_WEIGHTS}
    to_parts = {n: f for n, _, f in _MATMUL_WEIGHTS}
    first = [(n, 0) for n in _FIRST_WEIGHTS]
    later = [(n, i) for n in big for i in range(1 if n in _FIRST_WEIGHTS else 0, w[n].shape[0])]

    def shards(group):
        return [two_d(w[n][i].astype(BF16)) for n, i in group]

    def layers(group, gathered):
        out = {}
        for (n, i), g in zip(group, gathered):
            out.setdefault(n, []).append(to_full[n](g.reshape((N_DEV, 1) + w[n].shape[1:]))[0])
        return out

    gathered = _gather(shards(first) + [two_d(w[n]) for n in _CONV_WEIGHTS], name="gather_first")
    full = dict(w)
    full.update(layers(first, gathered))
    for n, g in zip(_CONV_WEIGHTS, gathered[len(first):]):
        full[n] = _col_full(g.reshape((N_DEV,) + w[n].shape))

    def host_of(n, i):
        if n.startswith("ffn2") and i == DEPTH - 1:
            return DEPTH - 1
        if (n.startswith("ffn1") and i == DEPTH - 1) or (n.startswith("sb_") and i == 1):
            return 1
        return 0

    gather_later = {}
    for host in (0, 1, DEPTH - 1):
        group = [(n, i) for n, i in later if host_of(n, i) == host]

        def merge(wd, gathered_group, group=group):
            wd = dict(wd)
            for n, ls in layers(group, gathered_group).items():
                have = wd[n] if isinstance(wd[n], list) else []
                wd[n] = have + ls
            return wd

        xs = shards(group)
        gather_later[host] = (_Hosted(_gather_steps, xs, _gather_shapes(xs), _GATHER_COPIES), merge)

    def chip_sums(group, grads, tag):
        parts = []
        for n, i in group:
            p8 = to_parts[n](grads[n][i][None].astype(BF16))
            parts.append(p8.reshape(4, 2, -1, p8.shape[-1]))
        from_sibling = _scatter_sibling(parts, name=f"scatter_sibling_{tag}")
        return [_pair_add(p, r, name=f"pair_add_{tag}_{n}{i}") for (n, i), p, r in zip(group, parts, from_sibling)]

    def scatter_early(grads):
        ys = chip_sums(later, grads, "later")
        return _Hosted(_chip_scatter_steps, ys, _chip_scatter_shapes(ys), _CHIP_SCATTER_COPIES)

    loss_part, dx, grads, recv_later = _local_step(x[0], loss_target[0], full, gather_later, scatter_early)
    loss = lax.psum(loss_part[0, 0], ("x", "y", "c"))

    recv_first = _scatter_chips(chip_sums(first, grads, "first"), name="scatter_chips_first")
    contrib = {n: [r] for (n, _), r in zip(first, recv_first)}
    for (n, _), r in zip(later, recv_later):
        contrib.setdefault(n, []).append(r)
    out_g, out_d, out_m, out_v = {}, {}, {}, {}

    def update(n, parts):
        res = _adamw_reduce(parts, two_d(w[n]), two_d(mom[n]), two_d(var[n]), name=f"adamw_{n}")
        out_g[n], out_d[n], out_m[n], out_v[n] = (r.reshape(w[n].shape) for r in res)

    for n in big:
        update(n, contrib[n][0] if len(contrib[n]) == 1 else jnp.concatenate(contrib[n], axis=1))

    small = list(_REPLICATED) + list(_CONV_WEIGHTS)
    small_shapes = [grads[n].shape for n in small]
    spacked, scounts = _pack_rows([grads[n].astype(F32) for n in small], 8)
    sg = _unpack_rows(_gather([spacked], name="gather_small_grads")[0], scounts, small_shapes, (N_DEV,))
    sg = dict(zip(small, sg))
    rep_w, rcounts = _pack_rows([w[n] for n in _REPLICATED], 8)
    rep_m, _ = _pack_rows([mom[n] for n in _REPLICATED], 8)
    rep_v, _ = _pack_rows([var[n] for n in _REPLICATED], 8)
    rep_p = jnp.concatenate([_rows_of(sg[n].reshape(N_DEV, -1)[q]) for q in range(N_DEV) for n in _REPLICATED], axis=0)
    rep_p = rep_p.reshape(N_DEV, -1, _LANES)
    rep_p = jnp.pad(rep_p, ((0, 0), (0, rep_w.shape[0] - rep_p.shape[1]), (0, 0)))
    res = _adamw_reduce(rep_p, rep_w, rep_m, rep_v, name="adamw_replicated")
    rep_shapes = [w[n].shape for n in _REPLICATED]
    for tgt, r in zip((out_g, out_d, out_m, out_v), res):
        for n, a in zip(_REPLICATED, _unpack_rows(r, rcounts, rep_shapes)):
            tgt[n] = a
    for n in _CONV_WEIGHTS:
        c = w[n].shape[-1]
        mine = lax.dynamic_slice_in_dim(sg[n], me * c, c, axis=sg[n].ndim - 1)
        update(n, mine.reshape(N_DEV, -1, c))

    return (loss, dx[None], *[out_g[n] for n in _ORDER], *[out_d[n] for n in _ORDER],
            *[out_m[n] for n in _ORDER], *[out_v[n] for n in _ORDER])
```

```python
import functools
import math

import jax
import jax.numpy as jnp
from jax import lax
from jax.experimental import pallas as pl
from jax.experimental.pallas import tpu as pltpu

F32 = jnp.float32
BF16 = jnp.bfloat16

D_MODEL = 1024
D_FF = 2816
DEPTH = 4
N_DEV = 8
SB_HEADS = 16
SB_HEAD_DIM = 64
SB_TILE = 256
SB_HEADS_PER_STEP = 2
SB_FWD_GROUPS = (2, 1)
SB_BWD_GROUPS = (4, 2, 1)
SSD_HEADS = 32
SSD_HEAD_DIM = 64
SSD_GROUPS = 8
SSD_HPG = 4
SSD_STATE = 128
SSD_CHUNK = 128
SSD_GROUPS_PER_STEP = 4
SSD_D_INNER = 2048
SSD_CONV_DIM = 4096
SSD_IN_DIM = 6176
SSD_IN_PAD = 6272
SSD_NORM_GROUP = 256
RMS_EPS = 1e-6
ADAM_LR = 0.001
ADAM_B1 = 0.9
ADAM_B2 = 0.999
ADAM_EPS = 1e-08
ADAM_WD = 0.01
ADAM_STEP = 10
VMEM_LIMIT = 60 * 1024 * 1024

NT = (((1,), (1,)), ((), ()))
TN = (((0,), (0,)), ((), ()))
NN = (((1,), (0,)), ((), ()))


def _cparams(n_axes):
    return pltpu.CompilerParams(dimension_semantics=("arbitrary",) * n_axes, vmem_limit_bytes=VMEM_LIMIT)


def _tile(n, want, mult=8):
    if n <= want:
        return n
    for t in range(want, 0, -1):
        if n % t == 0 and t % mult == 0:
            return t
    return n


def _sigmoid(x):
    return 1.0 / (1.0 + jnp.exp(-x))


def _dot(a, b, dn=NN):
    return lax.dot_general(a, b, dn, preferred_element_type=F32)


def _split3(x):
    x1 = x.astype(BF16)
    r1 = x - x1.astype(F32)
    x2 = r1.astype(BF16)
    x3 = (r1 - x2.astype(F32)).astype(BF16)
    return x1, x2, x3


def _dot_exact(x, t):
    x1, x2, x3 = _split3(x)
    return _dot(x1, t) + _dot(x2, t) + _dot(x3, t)


ROW_SUM_LANES = 1


def _cumsum_operand(tri):
    return jnp.concatenate([tri, tri], axis=0)


def _cumsum_rowsum(x, tri2):
    x1 = x.astype(BF16)
    x2 = (x - x1.astype(F32)).astype(BF16)
    return _dot(jnp.concatenate([x1, x2], axis=1), tri2), jnp.sum(x, axis=1, keepdims=True)


def _across_lanes(c, t):
    return c


def _mm(a, b, *, name, ta=False, tb=False, sa=False, sb=False, so=False, tm=512, tn=1024, tk=1024,
        out_dtype=F32, epilogue=None, extras=(), outs=None, pair=None, col_chunk=None):
    s_n = pair or (a.shape[0] if sa else (b.shape[0] if sb else 1))
    ash, bsh = a.shape[-2:], b.shape[-2:]
    m, k = (ash[1], ash[0]) if ta else ash
    n = bsh[0] if tb else bsh[1]
    tm, tn, tk = _tile(m, tm), _tile(n, tn, 128), _tile(k, tk, 128)
    nk = k // tk
    if outs is None:
        outs = [(out_dtype, "stile" if so else "tile")]
    if epilogue is None:
        epilogue = lambda acc: (acc,)

    if ta:
        a_blk, a_idx = (tk, tm), (lambda j, i, kk: (kk, i))
    else:
        a_blk, a_idx = (tm, tk), (lambda j, i, kk: (i, kk))
    if tb:
        b_blk, b_idx = (tn, tk), (lambda j, i, kk: (j, kk))
    else:
        b_blk, b_idx = (tk, tn), (lambda j, i, kk: (kk, j))

    def lead(blk, idx, has_s):
        if not has_s:
            return pl.BlockSpec(blk, idx)
        return pl.BlockSpec((s_n,) + blk, lambda j, i, kk: (0,) + idx(j, i, kk))

    kinds = {
        "tile": lambda: pl.BlockSpec((tm, tn), lambda j, i, kk: (i, j)),
        "stile": lambda: pl.BlockSpec((s_n, tm, tn), lambda j, i, kk: (0, i, j)),
        "row": lambda: pl.BlockSpec((1, tn), lambda j, i, kk: (0, j)),
        "colsum": lambda: pl.BlockSpec((1, tn), lambda j, i, kk: (0, j)),
    }
    shapes = {"tile": (m, n), "stile": (s_n, m, n), "colsum": (1, n)}
    in_specs = [lead(a_blk, a_idx, sa), lead(b_blk, b_idx, sb)] + [kinds[kd]() for _, kd in extras]
    out_specs = [kinds[kd]() for _, kd in outs]
    out_shape = [jax.ShapeDtypeStruct(shapes[kd], dt) for dt, kd in outs]
    n_ex, n_out = len(extras), len(outs)
    dn = ((((0,) if ta else (1,)), ((1,) if tb else (0,))), ((), ()))
    acc_shape = (s_n, tm, tn) if so else (tm, tn)

    def body(*refs):
        a_ref, b_ref = refs[0], refs[1]
        ex_refs = refs[2:2 + n_ex]
        o_refs = refs[2 + n_ex:2 + n_ex + n_out]
        i = pl.program_id(1)
        kk = pl.program_id(2)

        def products():
            for s in range(s_n if (sa or sb) else 1):
                av = (a_ref[s] if sa else a_ref[...]).astype(BF16)
                bv = (b_ref[s] if sb else b_ref[...]).astype(BF16)
                yield s, lax.dot_general(av, bv, dn, preferred_element_type=F32)

        def finish(accv):
            vals = epilogue(accv, *[r[...] for r in ex_refs])
            for (dt, kd), o_ref, val in zip(outs, o_refs, vals):
                if kd == "colsum":
                    _accumulate(o_ref, val, i == 0)
                elif kd == "stile":
                    for s in range(s_n):
                        o_ref[s] = val[s].astype(dt)
                else:
                    o_ref[...] = val.astype(dt)

        if nk == 1 and col_chunk:
            bounds = [(c0, min(col_chunk, tn - c0)) for c0 in range(0, tn, col_chunk)]
            n_s = s_n if (sa or sb) else 1
            a_vals = [(a_ref[s] if sa else a_ref[...]).astype(BF16) for s in range(n_s if sa else 1)]
            accs = []
            for c0, cw in bounds:
                ds = []
                for s in range(n_s):
                    idx = ((s,) if sb else ()) + ((pl.ds(c0, cw), slice(None)) if tb else (slice(None), pl.ds(c0, cw)))
                    ds.append(lax.dot_general(a_vals[s if sa else 0], b_ref[idx].astype(BF16), dn,
                                              preferred_element_type=F32))
                accs.append(tuple(ds) if so else functools.reduce(jnp.add, ds))
            for (c0, cw), accv in zip(bounds, accs):
                cols = pl.ds(c0, cw)
                exv = [r[:, :, cols] if kd == "stile" else r[:, cols] for r, (_, kd) in zip(ex_refs, extras)]
                vals = epilogue(accv, *exv)
                for (dt, kd), o_ref, val in zip(outs, o_refs, vals):
                    if kd == "stile":
                        for s in range(s_n):
                            o_ref[s, :, cols] = val[s].astype(dt)
                    else:
                        o_ref[:, cols] = val.astype(dt)
            return

        if nk == 1:
            ds = [d for _, d in products()]
            finish(tuple(ds) if so else functools.reduce(jnp.add, ds))
            return

        acc = refs[-1]

        @pl.when(kk == 0)
        def _():
            acc[...] = jnp.zeros_like(acc)

        for s, d in products():
            if so:
                acc[s] += d
            else:
                acc[...] += d

        @pl.when(kk == nk - 1)
        def _():
            finish(tuple(acc[s] for s in range(s_n)) if so else acc[...])

    res = pl.pallas_call(
        body, name=name, grid=(n // tn, m // tm, nk),
        in_specs=in_specs, out_specs=out_specs, out_shape=out_shape,
        scratch_shapes=[pltpu.VMEM(acc_shape, F32)] if nk > 1 else [], compiler_params=_cparams(3),
    )(a, b, *[e for e, _ in extras])
    return res[0] if len(res) == 1 else res


def _accumulate(o_ref, val, first):
    @pl.when(first)
    def _():
        o_ref[...] = val

    @pl.when(jnp.logical_not(first))
    def _():
        o_ref[...] += val


def _rmsnorm(x, g, *, name):
    l, d = x.shape
    tm = _tile(l, 512)

    def body(x_ref, g_ref, o_ref):
        xv = x_ref[...]
        r = lax.rsqrt(jnp.mean(xv * xv, axis=1, keepdims=True) + RMS_EPS)
        o_ref[...] = (xv * r * g_ref[...]).astype(BF16)

    return pl.pallas_call(
        body, name=name, grid=(l // tm,),
        in_specs=[pl.BlockSpec((tm, d), lambda i: (i, 0)), pl.BlockSpec((1, d), lambda i: (0, 0))],
        out_specs=pl.BlockSpec((tm, d), lambda i: (i, 0)),
        out_shape=jax.ShapeDtypeStruct((l, d), BF16), compiler_params=_cparams(1),
    )(x, g)


def _norm_bwd_epilogue(dh, x, g, dres):
    r = lax.rsqrt(jnp.mean(x * x, axis=1, keepdims=True) + RMS_EPS)
    xh = x * r
    dg = jnp.sum(dh * xh, axis=0, keepdims=True)
    dxh = dh * g
    dx = r * (dxh - xh * jnp.mean(dxh * xh, axis=1, keepdims=True))
    return dres + dx, dg


def _final_loss(x, g, tgt, *, name):
    l, d = x.shape
    tm = _tile(l, 512)

    def body(x_ref, g_ref, t_ref, loss_ref, dx_ref, dg_ref):
        i = pl.program_id(0)
        xv, gv = x_ref[...], g_ref[...]
        r = lax.rsqrt(jnp.mean(xv * xv, axis=1, keepdims=True) + RMS_EPS)
        xh = xv * r
        e = xh * gv - t_ref[...]
        part = 0.5 * jnp.sum(jnp.mean(e * e, axis=1, keepdims=True), axis=0, keepdims=True)
        dy = e * (1.0 / d)
        dg = jnp.sum(dy * xh, axis=0, keepdims=True)
        dxh = dy * gv
        dx_ref[...] = r * (dxh - xh * jnp.mean(dxh * xh, axis=1, keepdims=True))
        _accumulate(dg_ref, dg, i == 0)
        _accumulate(loss_ref, jnp.broadcast_to(part, (1, 128)), i == 0)

    return pl.pallas_call(
        body, name=name, grid=(l // tm,),
        in_specs=[pl.BlockSpec((tm, d), lambda i: (i, 0)), pl.BlockSpec((1, d), lambda i: (0, 0)),
                  pl.BlockSpec((tm, d), lambda i: (i, 0))],
        out_specs=[pl.BlockSpec((1, 128), lambda i: (0, 0)), pl.BlockSpec((tm, d), lambda i: (i, 0)),
                   pl.BlockSpec((1, d), lambda i: (0, 0))],
        out_shape=[jax.ShapeDtypeStruct((1, 128), F32), jax.ShapeDtypeStruct((l, d), F32),
                   jax.ShapeDtypeStruct((1, d), F32)],
        compiler_params=_cparams(1),
    )(x, g, tgt)


def _ffn_fwd(x, g, wgu, wd, tag):
    h = _rmsnorm(x, g, name=f"{tag}_norm")

    def act(acc):
        gate, up = acc
        return acc, gate * _sigmoid(gate) * up

    gu, a = _mm(h, wgu, sb=True, so=True, tm=512, tn=1408, tk=1024, name=f"{tag}_up",
                outs=[(BF16, "stile"), (BF16, "tile")], epilogue=act)
    xo = _mm(a, wd, tm=512, tn=1024, tk=2816, name=f"{tag}_down", extras=[(x, "tile")],
             epilogue=lambda acc, xt: (xt + 0.5 * acc,))
    return xo, (x, h, gu, a)


def _ffn_bwd(dout, saved, g, wgu, wd, tag):
    x, h, gu, a = saved

    def act_bwd(acc, guv):
        da = 0.5 * acc
        gate, up = guv[0].astype(F32), guv[1].astype(F32)
        s = _sigmoid(gate)
        return ((da * up * s * (1.0 + gate * (1.0 - s)), da * gate * s),)

    dgu = _mm(dout, wd, tb=True, pair=2, tm=512, tn=1408, tk=1024, col_chunk=384, name=f"{tag}_dact", extras=[(gu, "stile")],
              outs=[(BF16, "stile")], epilogue=act_bwd)
    dwd = _mm(a, dout, ta=True, tm=1408, tn=1024, tk=2048, name=f"{tag}_dwd", out_dtype=BF16,
              epilogue=lambda acc: (0.5 * acc,))
    dwgu = _mm(h, dgu, ta=True, sb=True, so=True, tm=512, tn=1408, tk=2048, name=f"{tag}_dwgu", out_dtype=BF16)
    dx, dg = _mm(dgu, wgu, tb=True, sa=True, sb=True, tm=512, tn=1024, tk=2816, name=f"{tag}_dx",
                 extras=[(x, "tile"), (g, "row"), (dout, "tile")], outs=[(F32, "tile"), (F32, "colsum")],
                 epilogue=_norm_bwd_epilogue)
    return dx, dg, dwgu, dwd


def _sb_plan(n, sizes):
    digits = [n // sizes[0]] + [(n // s) % 2 for s in sizes[1:]]
    plan, none_smaller = [], 1
    for size, d in reversed(list(zip(sizes, digits))):
        has = jnp.minimum(d, 1)
        with_diag = none_smaller * has
        plan.append((size, True, with_diag))
        if size > 1:
            plan.append((size, False, d - with_diag))
        none_smaller = none_smaller * (1 - has)
    return plan


def _sb_sweep(plan, start, step, fn, carry):
    pos = start
    for size, with_diag, trips in plan:
        diag = 0 if step < 0 else size - 1

        def trip(it, cr, size=size, with_diag=with_diag, pos=pos, diag=diag):
            base = pos + step * size * it
            return fn([base + step * b for b in range(size)], cr, [with_diag and b == diag for b in range(size)])

        carry = lax.fori_loop(0, trips, trip, carry)
        pos = pos + step * size * trips
    return carry


def _sb_logs(z):
    lb = jnp.minimum(z, 0.0) - jnp.log(1.0 + jnp.exp(-jnp.abs(z)))
    return lb, lb - z


class _Hosted:
    def __init__(self, steps, xs, out_shapes, copies):
        self.steps, self.xs, self.n, self.copies = steps, list(xs), len(xs), copies
        self.out_shape = [jax.ShapeDtypeStruct(s, x.dtype) for s, x in zip(out_shapes, xs)]
        self.specs = [_HBM] * self.n
        self.sems = [pltpu.SemaphoreType.DMA((self.n, copies)), pltpu.SemaphoreType.DMA((self.n, copies)),
                     pltpu.SemaphoreType.DMA((self.n,))]

    def run(self, x_refs, o_refs, sems, grid):
        ids = [pl.program_id(a) for a in range(len(grid))]
        first = functools.reduce(jnp.logical_and, [p == 0 for p in ids])
        last = functools.reduce(jnp.logical_and, [p == g - 1 for p, g in zip(ids, grid)])
        start, finish = self.steps(x_refs, o_refs, *sems)
        pl.when(first)(start)
        return lambda: pl.when(last)(finish)


def _head_masks(hs):
    lane = lax.broadcasted_iota(jnp.int32, (1, hs * SB_HEAD_DIM), 1)
    return [jnp.logical_and(lane >= hh * SB_HEAD_DIM, lane < (hh + 1) * SB_HEAD_DIM) for hh in range(hs)]


def _sb_fwd(qkv, *, name, hosted=None):
    l = qkv.shape[0]
    d_model = qkv.shape[1] // 3
    dh = SB_HEAD_DIM
    t = _tile(l, SB_TILE)
    hs = 2 * SB_HEADS_PER_STEP
    w = hs * dh
    n_grp = d_model // w
    scale = dh ** -0.5
    grid = (n_grp, l // t)
    nh = hosted.n if hosted else 0

    def body(q_ref, k_ref, v_ref, *rest):
        o_ref = rest[nh]
        at_end = hosted.run(rest[:nh], rest[nh + 1:2 * nh + 1], rest[2 * nh + 1:], grid) if hosted else None
        i = pl.program_id(1)
        heads = _head_masks(hs)
        q_all = (q_ref[...].astype(F32) * scale).astype(BF16)
        qs = [jnp.where(heads[hh], q_all, jnp.zeros_like(q_all)) for hh in range(hs)]
        row = lax.broadcasted_iota(jnp.int32, (t, t), 0)
        col = lax.broadcasted_iota(jnp.int32, (t, t), 1)
        strict = col < row
        tri = _cumsum_operand(strict.astype(BF16))

        def block(jbs, carry, masks):
            sls = [pl.ds(pl.multiple_of(jb * t, t), t) for jb in jbs]
            chains = [(hh, b) for b in range(len(jbs)) for hh in range(hs)]
            ks = [k_ref[sl, :] for sl in sls]
            zs = {(hh, b): _dot(qs[hh], ks[b], NT) for hh, b in chains}
            lbs, tails, sums = {}, {}, {}
            for hh, b in chains:
                lb, lk = _sb_logs(zs[hh, b])
                if masks[b]:
                    lk = jnp.where(strict, lk, 0.0)
                lbs[hh, b] = lb
                tails[hh, b], sums[hh, b] = _cumsum_rowsum(lk, tri)
            cs, o = list(carry[0]), carry[1]
            for b in range(len(jbs)):
                atts = []
                for hh in range(hs):
                    att = jnp.exp(lbs[hh, b] + tails[hh, b] + _across_lanes(cs[hh], t))
                    if masks[b]:
                        att = jnp.where(strict, att, 0.0)
                    atts.append(att.astype(BF16))
                    cs[hh] = cs[hh] + sums[hh, b]
                vb = v_ref[sls[b], :]
                v_heads = jnp.concatenate([jnp.where(heads[hh], vb, jnp.zeros_like(vb)) for hh in range(hs)], axis=0)
                o = o + _dot(jnp.concatenate(atts, axis=1), v_heads)
            return tuple(cs), o

        carry = (tuple(jnp.zeros((t, ROW_SUM_LANES), F32) for _ in range(hs)), jnp.zeros((t, w), F32))
        carry = _sb_sweep(_sb_plan(i + 1, SB_FWD_GROUPS), i, -1, block, carry)
        o_ref[...] = carry[1].astype(o_ref.dtype)
        if hosted:
            at_end()

    blocks = d_model // w
    res = pl.pallas_call(
        body, name=name, grid=grid,
        in_specs=[pl.BlockSpec((t, w), lambda g, i: (i, g)), pl.BlockSpec((l, w), lambda g, i: (0, blocks + g)),
                  pl.BlockSpec((l, w), lambda g, i: (0, 2 * blocks + g))] + (hosted.specs if hosted else []),
        out_specs=[pl.BlockSpec((t, w), lambda g, i: (i, g))] + (hosted.specs if hosted else []),
        out_shape=[jax.ShapeDtypeStruct((l, d_model), BF16)] + (hosted.out_shape if hosted else []),
        scratch_shapes=hosted.sems if hosted else [], compiler_params=_cparams(2),
    )(qkv, qkv, qkv, *(hosted.xs if hosted else []))
    return (res[0], res[1:]) if hosted else res[0]


def _sb_bwd(qkv, do, *, name, hosted=None):
    l = qkv.shape[0]
    d_model = qkv.shape[1] // 3
    dh = SB_HEAD_DIM
    t = _tile(l, SB_TILE)
    nq = l // t
    hs = SB_HEADS_PER_STEP
    w = hs * dh
    blocks = d_model // w
    scale = dh ** -0.5
    grid = (blocks, nq)
    nh = hosted.n if hosted else 0

    def body(q_ref, k_ref, v_ref, do_ref, *rest):
        dq_ref, dk_ref, dv_ref = rest[nh:nh + 3]
        e_scr, s_scr = rest[2 * nh + 3:2 * nh + 5]
        at_end = hosted.run(rest[:nh], rest[nh + 3:2 * nh + 3], rest[2 * nh + 5:], grid) if hosted else None
        i = pl.program_id(1)

        @pl.when(i == 0)
        def _():
            dk_ref[...] = jnp.zeros_like(dk_ref)
            dv_ref[...] = jnp.zeros_like(dv_ref)

        heads = _head_masks(hs)
        q_all = (q_ref[...].astype(F32) * scale).astype(BF16)
        do_all = do_ref[...]
        qs = [jnp.where(heads[hh], q_all, jnp.zeros_like(q_all)) for hh in range(hs)]
        dos = [jnp.where(heads[hh], do_all, jnp.zeros_like(do_all)) for hh in range(hs)]
        row = lax.broadcasted_iota(jnp.int32, (t, t), 0)
        col = lax.broadcasted_iota(jnp.int32, (t, t), 1)
        strict = col < row
        tri_suffix = _cumsum_operand(strict.astype(BF16))
        tri_prefix = _cumsum_operand((row < col).astype(BF16))

        def sweep1(jbs, cs, masks):
            sls = [pl.ds(pl.multiple_of(jb * t, t), t) for jb in jbs]
            chains = [(hh, b) for b in range(len(jbs)) for hh in range(hs)]
            zs = {(hh, b): _dot(qs[hh], k_ref[sls[b], :], NT) for hh, b in chains}
            datts = {(hh, b): _dot(dos[hh], v_ref[sls[b], :], NT) for hh, b in chains}
            lbs, tails, sums = {}, {}, {}
            for hh, b in chains:
                lb, lk = _sb_logs(zs[hh, b])
                if masks[b]:
                    lk = jnp.where(strict, lk, 0.0)
                lbs[hh, b] = lb
                tails[hh, b], sums[hh, b] = _cumsum_rowsum(lk, tri_suffix)
                s_scr[hh, jbs[b]] = jnp.exp(lb)
            cs = list(cs)
            for hh, b in chains:
                att = jnp.exp(lbs[hh, b] + tails[hh, b] + _across_lanes(cs[hh], t))
                if masks[b]:
                    att = jnp.where(strict, att, 0.0)
                e_scr[hh, jbs[b]] = att * datts[hh, b]
                dv_ref[sls[b], :] += _dot(att.astype(BF16), dos[hh], TN)
                cs[hh] = cs[hh] + sums[hh, b]
            return tuple(cs)

        plan = _sb_plan(i + 1, SB_BWD_GROUPS)
        _sb_sweep(plan, i, -1, sweep1, tuple(jnp.zeros((t, ROW_SUM_LANES), F32) for _ in range(hs)))

        def sweep2(jbs, carry, masks):
            sls = [pl.ds(pl.multiple_of(jb * t, t), t) for jb in jbs]
            chains = [(hh, b) for b in range(len(jbs)) for hh in range(hs)]
            des = {(hh, b): e_scr[hh, jbs[b]] for hh, b in chains}
            pres = {(hh, b): _cumsum_rowsum(des[hh, b], tri_prefix) for hh, b in chains}
            carry = [list(c) for c in carry]
            for hh, b in chains:
                p, dq = carry[hh]
                de, sg = des[hh, b], s_scr[hh, jbs[b]]
                dlk = _across_lanes(p, t) + pres[hh, b][0]
                if masks[b]:
                    dlk = jnp.where(strict, dlk, 0.0)
                dz = (de - sg * (de + dlk)).astype(BF16)
                dk_ref[sls[b], :] += _dot(dz, qs[hh], TN)
                carry[hh] = [p + pres[hh, b][1], dq + _dot(dz, k_ref[sls[b], :])]
            return tuple(tuple(c) for c in carry)

        carry = tuple((jnp.zeros((t, ROW_SUM_LANES), F32), jnp.zeros((t, w), F32)) for _ in range(hs))
        carry = _sb_sweep(plan[::-1], 0, 1, sweep2, carry)
        dq = jnp.zeros((t, w), F32)
        for hh in range(hs):
            dq = jnp.where(heads[hh], carry[hh][1], dq)
        dq_ref[...] = (dq * scale).astype(dq_ref.dtype)
        if hosted:
            at_end()

    qspec = pl.BlockSpec((t, w), lambda g, i: (i, g))
    cols = lambda off: pl.BlockSpec((l, w), lambda g, i: (0, off + g))
    res = pl.pallas_call(
        body, name=name, grid=grid,
        in_specs=[qspec, cols(blocks), cols(2 * blocks), qspec] + (hosted.specs if hosted else []),
        out_specs=[qspec, cols(0), cols(0)] + (hosted.specs if hosted else []),
        out_shape=[jax.ShapeDtypeStruct((l, d_model), BF16), jax.ShapeDtypeStruct((l, d_model), F32),
                   jax.ShapeDtypeStruct((l, d_model), F32)] + (hosted.out_shape if hosted else []),
        scratch_shapes=[pltpu.VMEM((hs, nq, t, t), F32), pltpu.VMEM((hs, nq, t, t), F32)]
        + (hosted.sems if hosted else []),
        compiler_params=_cparams(2),
    )(qkv, qkv, qkv, do, *(hosted.xs if hosted else []))
    return (res[0], res[1], res[2], res[3:]) if hosted else res


def _sb_layer_fwd(x, g, wqkv, wo, tag, hosted=None):
    h = _rmsnorm(x, g, name=f"{tag}_norm")
    qkv = _mm(h, wqkv, tm=1024, tn=1024, tk=1024, name=f"{tag}_qkv", out_dtype=BF16)
    o = _sb_fwd(qkv, name=f"{tag}_attn", hosted=hosted)
    carried = None
    if hosted:
        o, carried = o
    xo = _mm(o, wo, tm=1024, tn=1024, tk=1024, name=f"{tag}_out", extras=[(x, "tile")],
             epilogue=lambda acc, xt: (xt + acc,))
    return xo, (x, h, qkv, o), carried


def _sb_layer_bwd(dout, saved, g, wqkv, wo, tag, hosted=None):
    x, h, qkv, o = saved
    do = _mm(dout, wo, tb=True, tm=1024, tn=1024, tk=1024, name=f"{tag}_do", out_dtype=BF16)
    dwo = _mm(o, dout, ta=True, tm=1024, tn=1024, tk=2048, name=f"{tag}_dwo", out_dtype=BF16)
    res = _sb_bwd(qkv, do, name=f"{tag}_attn_bwd", hosted=hosted)
    dq, dk, dv = res[:3]
    carried = res[3] if hosted else None
    dqkv = jnp.concatenate([dq, dk.astype(BF16), dv.astype(BF16)], axis=1)
    dwqkv = _mm(h, dqkv, ta=True, tm=1024, tn=1024, tk=2048, name=f"{tag}_dwqkv", out_dtype=BF16)
    dx, dg = _mm(dqkv, wqkv, tb=True, tm=512, tn=1024, tk=3072, name=f"{tag}_dx",
                 extras=[(x, "tile"), (g, "row"), (dout, "tile")], outs=[(F32, "tile"), (F32, "colsum")],
                 epilogue=_norm_bwd_epilogue)
    return dx, dg, dwqkv, dwo, carried


def _shift_down(x, s, t_idx):
    return jnp.where(t_idx >= s, pltpu.roll(x, s, 0), 0.0)


def _shift_up(x, s, t_idx):
    n = x.shape[0]
    return jnp.where(t_idx < n - s, pltpu.roll(x, n - s, 0), 0.0)


def _sc_fwd(p, cw, *, name):
    l = p.shape[0]
    d = cw.shape[1]
    tc = 128
    nb = d // tc

    def body(b_ref, c_ref, h_ref, w_ref, o_ref):
        v = c_ref[...] * h_ref[...]
        t_idx = lax.broadcasted_iota(jnp.int32, v.shape, 0)
        u = v * w_ref[2:3, :] + _shift_down(v, 1, t_idx) * w_ref[1:2, :] + _shift_down(v, 2, t_idx) * w_ref[0:1, :]
        o_ref[...] = (b_ref[...] * u).astype(BF16)

    return pl.pallas_call(
        body, name=name, grid=(nb,),
        in_specs=[pl.BlockSpec((l, tc), lambda j: (0, j)), pl.BlockSpec((l, tc), lambda j: (0, nb + j)),
                  pl.BlockSpec((l, tc), lambda j: (0, 2 * nb + j)), pl.BlockSpec((3, tc), lambda j: (0, j))],
        out_specs=pl.BlockSpec((l, tc), lambda j: (0, j)),
        out_shape=jax.ShapeDtypeStruct((l, d), BF16), compiler_params=_cparams(1),
    )(p, p, p, cw)


def _sc_bwd(p, cw, dbu, *, name):
    l = p.shape[0]
    d = cw.shape[1]
    tc = 128
    nb = d // tc

    def body(b_ref, c_ref, h_ref, w_ref, g_ref, db_ref, dc_ref, dh_ref, dw_ref):
        cv, hv = c_ref[...], h_ref[...]
        v = cv * hv
        t_idx = lax.broadcasted_iota(jnp.int32, v.shape, 0)
        v1, v2 = _shift_down(v, 1, t_idx), _shift_down(v, 2, t_idx)
        u = v * w_ref[2:3, :] + v1 * w_ref[1:2, :] + v2 * w_ref[0:1, :]
        dbu_v = g_ref[...]
        db_ref[...] = (dbu_v * u).astype(BF16)
        du = dbu_v * b_ref[...]
        dv = du * w_ref[2:3, :] + _shift_up(du, 1, t_idx) * w_ref[1:2, :] + _shift_up(du, 2, t_idx) * w_ref[0:1, :]
        dc_ref[...] = (dv * hv).astype(BF16)
        dh_ref[...] = (dv * cv).astype(BF16)
        dw_ref[...] = jnp.zeros_like(dw_ref)
        dw_ref[0:1, :] = jnp.sum(du * v2, axis=0, keepdims=True)
        dw_ref[1:2, :] = jnp.sum(du * v1, axis=0, keepdims=True)
        dw_ref[2:3, :] = jnp.sum(du * v, axis=0, keepdims=True)

    col = lambda off: pl.BlockSpec((l, tc), lambda j: (0, off + j))
    return pl.pallas_call(
        body, name=name, grid=(nb,),
        in_specs=[col(0), col(nb), col(2 * nb), pl.BlockSpec((3, tc), lambda j: (0, j)), col(0)],
        out_specs=[col(0), col(0), col(0), pl.BlockSpec((8, tc), lambda j: (0, j))],
        out_shape=[jax.ShapeDtypeStruct((l, d), BF16)] * 3 + [jax.ShapeDtypeStruct((8, d), F32)],
        compiler_params=_cparams(1),
    )(p, p, p, cw, dbu)


def _sc_layer_fwd(x, g, win, cw, wout, tag):
    h = _rmsnorm(x, g, name=f"{tag}_norm")
    p = _mm(h, win, tm=1024, tn=1024, tk=1024, name=f"{tag}_in")
    bu = _sc_fwd(p, cw, name=f"{tag}_conv")
    xo = _mm(bu, wout, tm=1024, tn=1024, tk=1024, name=f"{tag}_out", extras=[(x, "tile")],
             epilogue=lambda acc, xt: (xt + acc,))
    return xo, (x, h, p, bu)


def _sc_layer_bwd(dout, saved, g, win, cw, wout, tag):
    x, h, p, bu = saved
    dbu = _mm(dout, wout, tb=True, tm=1024, tn=1024, tk=1024, name=f"{tag}_dbu")
    dwout = _mm(bu, dout, ta=True, tm=1024, tn=1024, tk=2048, name=f"{tag}_dwout", out_dtype=BF16)
    db, dc, dh, dcw = _sc_bwd(p, cw, dbu, name=f"{tag}_conv_bwd")
    dp = jnp.concatenate([db, dc, dh], axis=1)
    dwin = _mm(h, dp, ta=True, tm=1024, tn=1024, tk=2048, name=f"{tag}_dwin", out_dtype=BF16)
    dx, dg = _mm(dp, win, tb=True, tm=512, tn=1024, tk=3072, name=f"{tag}_dx",
                 extras=[(x, "tile"), (g, "row"), (dout, "tile")], outs=[(F32, "tile"), (F32, "colsum")],
                 epilogue=_norm_bwd_epilogue)
    return dx, dg, dwin, dcw[:3], dwout


def _ssd_conv_fwd(p, cw, cb, *, name):
    l = p.shape[0]
    tc = 128
    nb = SSD_CONV_DIM // tc
    off = SSD_D_INNER // tc

    def body(x_ref, w_ref, b_ref, o_ref):
        xv = x_ref[...]
        t_idx = lax.broadcasted_iota(jnp.int32, xv.shape, 0)
        pre = xv * w_ref[3:4, :] + b_ref[...]
        for s in (1, 2, 3):
            pre = pre + _shift_down(xv, s, t_idx) * w_ref[3 - s:4 - s, :]
        o_ref[...] = pre * _sigmoid(pre)

    return pl.pallas_call(
        body, name=name, grid=(nb,),
        in_specs=[pl.BlockSpec((l, tc), lambda j: (0, off + j)), pl.BlockSpec((4, tc), lambda j: (0, j)),
                  pl.BlockSpec((1, tc), lambda j: (0, j))],
        out_specs=pl.BlockSpec((l, tc), lambda j: (0, j)),
        out_shape=jax.ShapeDtypeStruct((l, SSD_CONV_DIM), F32), compiler_params=_cparams(1),
    )(p, cw, cb)


def _ssd_conv_bwd(p, cw, cb, dxs_scan, dxs_skip, dbm, dcm, *, name):
    l = p.shape[0]
    tc = 128
    nb = SSD_CONV_DIM // tc
    off = SSD_D_INNER // tc
    n_xs, n_b = SSD_D_INNER // tc, SSD_GROUPS * SSD_STATE // tc

    def body(x_ref, w_ref, b_ref, ga_ref, gb_ref, gm_ref, gc_ref, dx_ref, dw_ref):
        j = pl.program_id(0)
        g_val = jnp.where(j < n_xs, ga_ref[...] + gb_ref[...], jnp.where(j < n_xs + n_b, gm_ref[...], gc_ref[...]))
        xv = x_ref[...]
        t_idx = lax.broadcasted_iota(jnp.int32, xv.shape, 0)
        xs = [xv] + [_shift_down(xv, s, t_idx) for s in (1, 2, 3)]
        pre = b_ref[...] + xs[0] * w_ref[3:4, :]
        for s in (1, 2, 3):
            pre = pre + xs[s] * w_ref[3 - s:4 - s, :]
        sg = _sigmoid(pre)
        dpre = g_val * sg * (1.0 + pre * (1.0 - sg))
        dx = dpre * w_ref[3:4, :]
        for s in (1, 2, 3):
            dx = dx + _shift_up(dpre, s, t_idx) * w_ref[3 - s:4 - s, :]
        dx_ref[...] = dx
        dw_ref[...] = jnp.zeros_like(dw_ref)
        for s in (0, 1, 2, 3):
            dw_ref[3 - s:4 - s, :] = jnp.sum(dpre * xs[s], axis=0, keepdims=True)
        dw_ref[4:5, :] = jnp.sum(dpre, axis=0, keepdims=True)

    return pl.pallas_call(
        body, name=name, grid=(nb,),
        in_specs=[pl.BlockSpec((l, tc), lambda j: (0, off + j)), pl.BlockSpec((4, tc), lambda j: (0, j)),
                  pl.BlockSpec((1, tc), lambda j: (0, j)),
                  pl.BlockSpec((l, tc), lambda j: (0, jnp.minimum(j, n_xs - 1))),
                  pl.BlockSpec((l, tc), lambda j: (0, jnp.minimum(j, n_xs - 1))),
                  pl.BlockSpec((l, tc), lambda j: (0, jnp.clip(j - n_xs, 0, n_b - 1))),
                  pl.BlockSpec((l, tc), lambda j: (0, jnp.clip(j - n_xs - n_b, 0, n_b - 1)))],
        out_specs=[pl.BlockSpec((l, tc), lambda j: (0, j)), pl.BlockSpec((8, tc), lambda j: (0, j))],
        out_shape=[jax.ShapeDtypeStruct((l, SSD_CONV_DIM), F32), jax.ShapeDtypeStruct((8, SSD_CONV_DIM), F32)],
        compiler_params=_cparams(1),
    )(p, cw, cb, dxs_scan, dxs_skip, dbm, dcm)


def _ssd_dt_fwd(p, bias, *, name):
    l = p.shape[0]
    tm = _tile(l, 1024)
    off = (SSD_D_INNER + SSD_CONV_DIM) // 128

    def body(x_ref, b_ref, o_ref):
        v = x_ref[...] + b_ref[...]
        o_ref[...] = jnp.maximum(v, 0.0) + jnp.log(1.0 + jnp.exp(-jnp.abs(v)))

    return pl.pallas_call(
        body, name=name, grid=(l // tm,),
        in_specs=[pl.BlockSpec((tm, 128), lambda i: (i, off)), pl.BlockSpec((1, 128), lambda i: (0, 0))],
        out_specs=pl.BlockSpec((tm, 128), lambda i: (i, 0)),
        out_shape=jax.ShapeDtypeStruct((l, 128), F32), compiler_params=_cparams(1),
    )(p, bias)


def _ssd_dt_bwd(p, bias, ddt, *, name):
    l = p.shape[0]
    tm = _tile(l, 1024)
    off = (SSD_D_INNER + SSD_CONV_DIM) // 128

    def body(x_ref, b_ref, g_ref, o_ref, db_ref):
        i = pl.program_id(0)
        d = g_ref[...] * _sigmoid(x_ref[...] + b_ref[...])
        o_ref[...] = d
        _accumulate(db_ref, jnp.sum(d, axis=0, keepdims=True), i == 0)

    return pl.pallas_call(
        body, name=name, grid=(l // tm,),
        in_specs=[pl.BlockSpec((tm, 128), lambda i: (i, off)), pl.BlockSpec((1, 128), lambda i: (0, 0)),
                  pl.BlockSpec((tm, 128), lambda i: (i, 0))],
        out_specs=[pl.BlockSpec((tm, 128), lambda i: (i, 0)), pl.BlockSpec((1, 128), lambda i: (0, 0))],
        out_shape=[jax.ShapeDtypeStruct((l, 128), F32), jax.ShapeDtypeStruct((1, 128), F32)],
        compiler_params=_cparams(1),
    )(p, bias, ddt)


def _row_to_col(r, eye):
    return jnp.sum(jnp.where(eye, r, 0.0), axis=1, keepdims=True)


def _col_to_row(c, eye):
    return jnp.sum(jnp.where(eye, c, 0.0), axis=0, keepdims=True)


def _ssd_chunk_common(b_ref, c_ref, dt_ref, a_ref, lam_scr):
    n = SSD_CHUNK
    row = lax.broadcasted_iota(jnp.int32, (n, n), 0)
    col = lax.broadcasted_iota(jnp.int32, (n, n), 1)
    bm, cm = b_ref[...].astype(BF16), c_ref[...].astype(BF16)
    g = _dot(cm, bm, NT)
    incl = (row <= col).astype(BF16)
    lam_scr[...] = _dot_exact(dt_ref[...] * a_ref[...], incl)
    return row, col, bm, cm, g


def _ssd_head_common(r, row, col, dt_ref, lam_scr):
    eye, tril = row == col, row >= col
    lam_r = lam_scr[r:r + 1, :]
    dt_r = dt_ref[r:r + 1, :]
    lam_c = _row_to_col(lam_r, eye)
    dt_c = _row_to_col(dt_r, eye)
    dk = jnp.where(tril, jnp.exp(jnp.minimum(lam_c - lam_r, 0.0)), 0.0)
    lam_last = jnp.sum(jnp.where(col[0:1, :] == SSD_CHUNK - 1, lam_r, 0.0), axis=1, keepdims=True)
    return eye, lam_r, dt_r, lam_c, dt_c, dk, lam_last


def _ssd_fwd(xh, act, dt_t, a_b, *, name, hosted=None):
    l = xh.shape[1]
    nc = l // SSD_CHUNK
    n, p_dim, hpg = SSD_CHUNK, SSD_HEAD_DIM, SSD_HPG

    gps = SSD_GROUPS_PER_STEP
    n_grp = SSD_GROUPS // gps
    grid = (n_grp, nc)
    nh = hosted.n if hosted else 0

    def body(x_ref, b_ref, c_ref, dt_ref, a_ref, *rest):
        y_ref, hp_ref = rest[nh:nh + 2]
        h_scr, lam_scr = rest[2 * nh + 2:2 * nh + 4]
        at_end = hosted.run(rest[:nh], rest[nh + 2:2 * nh + 2], rest[2 * nh + 4:], grid) if hosted else None

        @pl.when(pl.program_id(1) == 0)
        def _():
            h_scr[...] = jnp.zeros_like(h_scr)

        lanes = [pl.ds(gg * SSD_STATE, SSD_STATE) for gg in range(gps)]
        common = [_ssd_chunk_common(b_ref.at[:, lanes[gg]], c_ref.at[:, lanes[gg]], dt_ref.at[gg], a_ref.at[gg],
                                    lam_scr.at[gg]) for gg in range(gps)]
        for gg in range(gps):
            row, col, bm, cm, g = common[gg]
            for r in range(hpg):
                hd = gg * hpg + r
                _, _, dt_r, lam_c, dt_c, dk, lam_last = _ssd_head_common(r, row, col, dt_ref.at[gg], lam_scr.at[gg])
                xr = x_ref[hd]
                hr = h_scr[hd]
                w = (g * dk * dt_r).astype(BF16)
                y = _dot(w, xr.astype(BF16)) + _dot(cm, hr.astype(BF16), NT) * jnp.exp(lam_c)
                y_ref[hd] = y
                hp_ref[hd] = hr
                xw = (xr * (jnp.exp(lam_last - lam_c) * dt_c)).astype(BF16)
                h_scr[hd] = jnp.exp(lam_last) * hr + _dot(xw, bm, TN)
        if hosted:
            at_end()

    g_off = SSD_D_INNER // (gps * SSD_STATE)
    res = pl.pallas_call(
        body, name=name, grid=grid,
        in_specs=[pl.BlockSpec((gps * hpg, n, p_dim), lambda g, c: (g, c, 0)),
                  pl.BlockSpec((n, gps * SSD_STATE), lambda g, c: (c, g_off + g)),
                  pl.BlockSpec((n, gps * SSD_STATE), lambda g, c: (c, g_off + n_grp + g)),
                  pl.BlockSpec((gps, 8, n), lambda g, c: (g, 0, c)),
                  pl.BlockSpec((gps, 8, 128), lambda g, c: (g, 0, 0))] + (hosted.specs if hosted else []),
        out_specs=[pl.BlockSpec((gps * hpg, n, p_dim), lambda g, c: (g, c, 0)),
                   pl.BlockSpec((None, gps * hpg, p_dim, SSD_STATE), lambda g, c: (c, g, 0, 0))]
        + (hosted.specs if hosted else []),
        out_shape=[jax.ShapeDtypeStruct(xh.shape, F32),
                   jax.ShapeDtypeStruct((nc, SSD_HEADS, p_dim, SSD_STATE), F32)] + (hosted.out_shape if hosted else []),
        scratch_shapes=[pltpu.VMEM((gps * hpg, p_dim, SSD_STATE), F32), pltpu.VMEM((gps, 8, n), F32)]
        + (hosted.sems if hosted else []),
        compiler_params=_cparams(2),
    )(xh, act, act, dt_t, a_b, *(hosted.xs if hosted else []))
    return (res[0], res[1], res[2:]) if hosted else res


def _ssd_bwd(xh, act, dt_t, a_b, hprev, dyh, *, name):
    l = xh.shape[1]
    nc = l // SSD_CHUNK
    n, p_dim, hpg = SSD_CHUNK, SSD_HEAD_DIM, SSD_HPG
    gps = SSD_GROUPS_PER_STEP

    def body(x_ref, b_ref, c_ref, dt_ref, a_ref, hp_ref, dy_ref,
             dx_ref, db_ref, dc_ref, ddt_ref, da_ref, dh_scr, lam_scr, dlam_scr, ddt_scr):
        ci = pl.program_id(1)

        @pl.when(ci == 0)
        def _():
            dh_scr[...] = jnp.zeros_like(dh_scr)

        lanes = [pl.ds(gg * SSD_STATE, SSD_STATE) for gg in range(gps)]
        common = [_ssd_chunk_common(b_ref.at[:, lanes[gg]], c_ref.at[:, lanes[gg]], dt_ref.at[gg], a_ref.at[gg],
                                    lam_scr.at[gg]) for gg in range(gps)]
        dlam_scr[...] = jnp.zeros_like(dlam_scr)
        ddt_scr[...] = jnp.zeros_like(ddt_scr)
        for gg in range(gps):
            row, col, bm, cm, g = common[gg]
            dt_g, lam_g, dlam_g, ddt_g = dt_ref.at[gg], lam_scr.at[gg], dlam_scr.at[gg], ddt_scr.at[gg]
            dg_acc = jnp.zeros((n, n), F32)
            dc_acc = jnp.zeros((n, SSD_STATE), F32)
            db_acc = jnp.zeros((n, SSD_STATE), F32)
            for r in range(hpg):
                hd = gg * hpg + r
                eye, _, dt_r, lam_c, dt_c, dk, lam_last = _ssd_head_common(r, row, col, dt_g, lam_g)
                xr, dyr, hr, dhr = x_ref[hd], dy_ref[hd], hp_ref[hd], dh_scr[hd]
                xb, dyb, hb, dhb = xr.astype(BF16), dyr.astype(BF16), hr.astype(BF16), dhr.astype(BF16)
                e_l = jnp.exp(lam_c)
                e_last = jnp.exp(lam_last)
                decay_c = jnp.exp(lam_last - lam_c)
                w_c = decay_c * dt_c
                m = g * dk * dt_r
                dm = _dot(dyb, xb, NT)
                bdh = _dot(bm, dhb, NT)
                dx_ref[hd] = _dot(m.astype(BF16), dyb, TN) + w_c * bdh
                dg_acc = dg_acc + dm * dk * dt_r
                q_mat = dm * g * dk
                p_mat = q_mat * dt_r
                yoff = _dot(cm, hb, NT) * e_l
                q_c = jnp.sum(xr * bdh, axis=1, keepdims=True)
                dlam_c = (jnp.sum(p_mat, axis=1, keepdims=True) + jnp.sum(dyr * yoff, axis=1, keepdims=True)
                          - w_c * q_c)
                d_last = (jnp.sum(w_c * q_c, axis=0, keepdims=True)
                          + e_last * jnp.sum(jnp.sum(dhr * hr, axis=1, keepdims=True), axis=0, keepdims=True))
                dlam_g[r:r + 1, :] = (_col_to_row(dlam_c, eye) - jnp.sum(p_mat, axis=0, keepdims=True)
                                      + jnp.where(col[0:1, :] == n - 1, d_last, 0.0))
                ddt_g[r:r + 1, :] = jnp.sum(q_mat, axis=0, keepdims=True) + _col_to_row(decay_c * q_c, eye)
                dc_acc = dc_acc + e_l * _dot(dyb, hb)
                db_acc = db_acc + _dot((xr * w_c).astype(BF16), dhb)
                dh_scr[hd] = e_last * dhr + _dot((dyr * e_l).astype(BF16), cm, TN)

            dgb = dg_acc.astype(BF16)
            dc_ref[:, lanes[gg]] = _dot(dgb, bm) + dc_acc
            db_ref[:, lanes[gg]] = _dot(dgb, cm, TN) + db_acc
            rev = (row >= col).astype(BF16)
            da = _dot_exact(dlam_g[...], rev)
            ddt_ref[gg] = ddt_g[...] + da * a_ref[gg]
            _accumulate(da_ref.at[gg], da * dt_g[...], ci == 0)

        @pl.when(ci == nc - 1)
        def _():
            for gg in range(gps):
                da_ref[gg] = jnp.broadcast_to(jnp.sum(da_ref[gg], axis=1, keepdims=True), da_ref.shape[1:])

    g_off = SSD_D_INNER // (gps * SSD_STATE)
    n_grp = SSD_GROUPS // gps
    rc = lambda c: nc - 1 - c
    hspec = pl.BlockSpec((gps * hpg, n, p_dim), lambda g, c: (g, rc(c), 0))
    gspec = pl.BlockSpec((n, gps * SSD_STATE), lambda g, c: (rc(c), g))
    return pl.pallas_call(
        body, name=name, grid=(n_grp, nc),
        in_specs=[hspec,
                  pl.BlockSpec((n, gps * SSD_STATE), lambda g, c: (rc(c), g_off + g)),
                  pl.BlockSpec((n, gps * SSD_STATE), lambda g, c: (rc(c), g_off + n_grp + g)),
                  pl.BlockSpec((gps, 8, n), lambda g, c: (g, 0, rc(c))),
                  pl.BlockSpec((gps, 8, 128), lambda g, c: (g, 0, 0)),
                  pl.BlockSpec((None, gps * hpg, p_dim, SSD_STATE), lambda g, c: (rc(c), g, 0, 0)),
                  hspec],
        out_specs=[hspec, gspec, gspec,
                   pl.BlockSpec((gps, 8, n), lambda g, c: (g, 0, rc(c))),
                   pl.BlockSpec((gps, 8, 128), lambda g, c: (g, 0, 0))],
        out_shape=[jax.ShapeDtypeStruct(xh.shape, F32),
                   jax.ShapeDtypeStruct((l, SSD_GROUPS * SSD_STATE), F32),
                   jax.ShapeDtypeStruct((l, SSD_GROUPS * SSD_STATE), F32),
                   jax.ShapeDtypeStruct(dt_t.shape, F32),
                   jax.ShapeDtypeStruct(a_b.shape, F32)],
        scratch_shapes=[pltpu.VMEM((gps * hpg, p_dim, SSD_STATE), F32), pltpu.VMEM((gps, 8, n), F32),
                        pltpu.VMEM((gps, 8, n), F32), pltpu.VMEM((gps, 8, n), F32)],
        compiler_params=_cparams(2),
    )(xh, act, act, dt_t, a_b, hprev, dyh)


def _ssd_gate_fwd(y, act, p, d_vec, gn, *, name):
    l = y.shape[0]
    w = SSD_D_INNER
    tm = _tile(l, 256)

    def body(y_ref, xs_ref, z_ref, d_ref, g_ref, o_ref):
        for gi in range(SSD_GROUPS):
            sl = slice(gi * SSD_NORM_GROUP, (gi + 1) * SSD_NORM_GROUP)
            z = z_ref[:, sl]
            y2 = (y_ref[:, sl] + d_ref[:, sl] * xs_ref[:, sl]) * (z * _sigmoid(z))
            r = lax.rsqrt(jnp.mean(y2 * y2, axis=1, keepdims=True) + RMS_EPS)
            o_ref[:, sl] = (y2 * r * g_ref[:, sl]).astype(BF16)

    rows = pl.BlockSpec((tm, w), lambda i: (i, 0))
    vec = pl.BlockSpec((1, w), lambda i: (0, 0))
    return pl.pallas_call(
        body, name=name, grid=(l // tm,), in_specs=[rows, rows, rows, vec, vec], out_specs=rows,
        out_shape=jax.ShapeDtypeStruct((l, w), BF16), compiler_params=_cparams(1),
    )(y, act, p, d_vec, gn)


def _ssd_gate_bwd(dyn, y, act, p, d_vec, gn, *, name):
    l = y.shape[0]
    w = SSD_D_INNER
    tm = _tile(l, 256)

    def body(dyn_ref, y_ref, xs_ref, z_ref, d_ref, g_ref, dy_ref, dz_ref, dxs_ref, dd_ref, dg_ref):
        i = pl.program_id(0)
        for gi in range(SSD_GROUPS):
            sl = slice(gi * SSD_NORM_GROUP, (gi + 1) * SSD_NORM_GROUP)
            z, xs, dv = z_ref[:, sl], xs_ref[:, sl], d_ref[:, sl]
            s = _sigmoid(z)
            sz = z * s
            y1 = y_ref[:, sl] + dv * xs
            y2 = y1 * sz
            r = lax.rsqrt(jnp.mean(y2 * y2, axis=1, keepdims=True) + RMS_EPS)
            y2h = y2 * r
            dyn_v = dyn_ref[:, sl]
            d2h = dyn_v * g_ref[:, sl]
            dy2 = r * (d2h - y2h * jnp.mean(d2h * y2h, axis=1, keepdims=True))
            dy1 = dy2 * sz
            dy_ref[:, sl] = dy1
            dz_ref[:, sl] = dy2 * y1 * s * (1.0 + z * (1.0 - s))
            dxs_ref[:, sl] = dv * dy1
            _accumulate(dd_ref.at[:, sl], jnp.sum(dy1 * xs, axis=0, keepdims=True), i == 0)
            _accumulate(dg_ref.at[:, sl], jnp.sum(dyn_v * y2h, axis=0, keepdims=True), i == 0)

    rows = pl.BlockSpec((tm, w), lambda i: (i, 0))
    vec = pl.BlockSpec((1, w), lambda i: (0, 0))
    return pl.pallas_call(
        body, name=name, grid=(l // tm,), in_specs=[rows, rows, rows, rows, vec, vec],
        out_specs=[rows, rows, rows, vec, vec],
        out_shape=[jax.ShapeDtypeStruct((l, w), F32)] * 3 + [jax.ShapeDtypeStruct((1, w), F32)] * 2,
        compiler_params=_cparams(1),
    )(dyn, y, act, p, d_vec, gn)


def _heads_major(x):
    return x.reshape(x.shape[0], SSD_HEADS, SSD_HEAD_DIM).transpose(1, 0, 2)


def _ssd_layer_fwd(x, g, win, cw, cb, dt_bias, a_log, d_skip, gn, wout, tag, hosted=None):
    l = x.shape[0]
    h = _rmsnorm(x, g, name=f"{tag}_norm")
    p = _mm(h, win, tm=1024, tn=896, tk=1024, name=f"{tag}_in")
    act = _ssd_conv_fwd(p, cw, cb, name=f"{tag}_conv")
    bias = jnp.pad(dt_bias, (0, 128 - SSD_HEADS)).reshape(1, 128)
    dt = _ssd_dt_fwd(p, bias, name=f"{tag}_dt")
    xh = _heads_major(act[:, :SSD_D_INNER])
    dt_t = jnp.pad(dt[:, :SSD_HEADS].T.reshape(SSD_GROUPS, SSD_HPG, l), ((0, 0), (0, 8 - SSD_HPG), (0, 0)))
    a = -jnp.exp(a_log).reshape(SSD_GROUPS, SSD_HPG, 1)
    a_b = jnp.broadcast_to(jnp.pad(a, ((0, 0), (0, 8 - SSD_HPG), (0, 0))), (SSD_GROUPS, 8, 128))
    res = _ssd_fwd(xh, act, dt_t, a_b, name=f"{tag}_scan", hosted=hosted)
    yh, hprev = res[:2]
    carried = res[2] if hosted else None
    y = yh.transpose(1, 0, 2).reshape(l, SSD_D_INNER)
    d_vec = jnp.repeat(d_skip, SSD_HEAD_DIM).reshape(1, SSD_D_INNER)
    yn = _ssd_gate_fwd(y, act, p, d_vec, gn, name=f"{tag}_gate")
    xo = _mm(yn, wout, tm=1024, tn=1024, tk=2048, name=f"{tag}_out", extras=[(x, "tile")],
             epilogue=lambda acc, xt: (xt + acc,))
    return xo, (x, h, p, act, bias, xh, dt_t, a_b, hprev, y, d_vec, yn), carried


def _ssd_layer_bwd(dout, saved, g, win, cw, cb, gn, wout, tag):
    x, h, p, act, bias, xh, dt_t, a_b, hprev, y, d_vec, yn = saved
    l = x.shape[0]
    dyn = _mm(dout, wout, tb=True, tm=1024, tn=1024, tk=1024, name=f"{tag}_dyn")
    dwout = _mm(yn, dout, ta=True, tm=1024, tn=1024, tk=2048, name=f"{tag}_dwout", out_dtype=BF16)
    dy, dz, dxs_d, dd_vec, dgn = _ssd_gate_bwd(dyn, y, act, p, d_vec, gn, name=f"{tag}_gate_bwd")
    dxh, dbm, dcm, ddt_t, da_b = _ssd_bwd(xh, act, dt_t, a_b, hprev, _heads_major(dy), name=f"{tag}_scan_bwd")
    dxs_scan = dxh.transpose(1, 0, 2).reshape(l, SSD_D_INNER)
    dxbc, dcw8 = _ssd_conv_bwd(p, cw, cb, dxs_scan, dxs_d, dbm, dcm, name=f"{tag}_conv_bwd")
    ddt = jnp.pad(ddt_t[:, :SSD_HPG, :].reshape(SSD_HEADS, l).T, ((0, 0), (0, 128 - SSD_HEADS)))
    ddt_raw, dbias = _ssd_dt_bwd(p, bias, ddt, name=f"{tag}_dt_bwd")
    dp = jnp.concatenate([dz, dxbc, ddt_raw], axis=1)
    dwin = _mm(h, dp, ta=True, tm=1024, tn=896, tk=2048, name=f"{tag}_dwin", out_dtype=BF16)
    dx, dg = _mm(dp, win, tb=True, tm=256, tn=1024, tk=6272, name=f"{tag}_dx",
                 extras=[(x, "tile"), (g, "row"), (dout, "tile")], outs=[(F32, "tile"), (F32, "colsum")],
                 epilogue=_norm_bwd_epilogue)
    a_heads = a_b[:, :SSD_HPG, 0].reshape(SSD_HEADS)
    grads = dict(
        ssd_w_in=dwin[:, :SSD_IN_DIM], ssd_conv_w=dcw8[:4], ssd_conv_b=dcw8[4],
        ssd_dt_bias=dbias[0, :SSD_HEADS], ssd_a_log=da_b[:, :SSD_HPG, 0].reshape(SSD_HEADS) * a_heads,
        ssd_d=dd_vec.reshape(SSD_HEADS, SSD_HEAD_DIM).sum(axis=1), ssd_norm=dgn[0], ssd_w_out=dwout)
    return dx, dg, grads


def _local_step(x, tgt, w, gather_later=None, scatter_early=None):
    row = lambda v: v.reshape(1, -1)
    saved = []
    for i in range(DEPTH):
        kind, j = i % 3, i // 3
        x, s1 = _ffn_fwd(x, row(w["ffn1_norm"][i]), w["ffn1_w_gu"][i], w["ffn1_w_down"][i], f"l{i}f1")
        gm = row(w["mix_norm"][i])
        hook = (gather_later or {}).get(i)
        hosted = hook[0] if hook else None
        if kind == 0:
            x, sm, carried = _sb_layer_fwd(x, gm, w["sb_w_qkv"][j], w["sb_w_o"][j], f"l{i}sb", hosted=hosted)
        elif kind == 1:
            x, sm, carried = _ssd_layer_fwd(x, gm, w["ssd_w_in"][j], w["ssd_conv_w"][j], row(w["ssd_conv_b"][j]),
                                            w["ssd_dt_bias"][j], w["ssd_a_log"][j], w["ssd_d"][j],
                                            row(w["ssd_norm"][j]), w["ssd_w_out"][j], f"l{i}ssd", hosted=hosted)
        if hook:
            w = hook[1](w, carried)
        if kind == 2:
            x, sm = _sc_layer_fwd(x, gm, w["sc_w_in"][j], w["sc_conv_w"][j], w["sc_w_out"][j], f"l{i}sc")
        x, s2 = _ffn_fwd(x, row(w["ffn2_norm"][i]), w["ffn2_w_gu"][i], w["ffn2_w_down"][i], f"l{i}f2")
        saved.append((s1, sm, s2))

    loss, dx, dfinal = _final_loss(x, row(w["final_norm"]), tgt, name="final_loss")
    per_layer = {k: [None] * DEPTH for k in ("ffn1_norm", "ffn1_w_gu", "ffn1_w_down", "mix_norm",
                                             "ffn2_norm", "ffn2_w_gu", "ffn2_w_down")}
    per_layer.update({"sb_w_qkv": [None, None], "sb_w_o": [None, None]})
    grads = {"final_norm": dfinal[0]}
    early = None
    for i in reversed(range(DEPTH)):
        kind, j = i % 3, i // 3
        s1, sm, s2 = saved[i]
        dx, dg, dwgu, dwd = _ffn_bwd(dx, s2, row(w["ffn2_norm"][i]), w["ffn2_w_gu"][i], w["ffn2_w_down"][i], f"l{i}f2")
        per_layer["ffn2_norm"][i], per_layer["ffn2_w_gu"][i], per_layer["ffn2_w_down"][i] = dg[0], dwgu, dwd
        gm = row(w["mix_norm"][i])
        if kind == 0:
            hosted = scatter_early({**grads, **per_layer}) if (scatter_early and i == 0) else None
            dx, dg, dwqkv, dwo, carried = _sb_layer_bwd(dx, sm, gm, w["sb_w_qkv"][j], w["sb_w_o"][j], f"l{i}sb",
                                                        hosted=hosted)
            per_layer["sb_w_qkv"][j], per_layer["sb_w_o"][j] = dwqkv, dwo
            if hosted:
                early = carried
        elif kind == 1:
            dx, dg, sg = _ssd_layer_bwd(dx, sm, gm, w["ssd_w_in"][j], w["ssd_conv_w"][j], row(w["ssd_conv_b"][j]),
                                        row(w["ssd_norm"][j]), w["ssd_w_out"][j], f"l{i}ssd")
            sg["ssd_w_in"], sg["ssd_w_out"] = [sg["ssd_w_in"]], [sg["ssd_w_out"]]
            grads.update({k: (v if isinstance(v, list) else v[None]) for k, v in sg.items()})
        else:
            dx, dg, dwin, dcw, dwout = _sc_layer_bwd(dx, sm, gm, w["sc_w_in"][j], w["sc_conv_w"][j],
                                                     w["sc_w_out"][j], f"l{i}sc")
            grads.update(sc_w_in=[dwin], sc_conv_w=dcw[None], sc_w_out=[dwout])
        per_layer["mix_norm"][i] = dg[0]
        dx, dg, dwgu, dwd = _ffn_bwd(dx, s1, row(w["ffn1_norm"][i]), w["ffn1_w_gu"][i], w["ffn1_w_down"][i], f"l{i}f1")
        per_layer["ffn1_norm"][i], per_layer["ffn1_w_gu"][i], per_layer["ffn1_w_down"][i] = dg[0], dwgu, dwd
    for k, v in per_layer.items():
        grads[k] = jnp.stack(v) if k.endswith("_norm") else v
    return loss, dx, grads, early


_HBM = pl.BlockSpec(memory_space=pltpu.HBM)


def _remote(src, dst, send_sems, recv_sems, idx, dev):
    return pltpu.make_async_remote_copy(src_ref=src, dst_ref=dst, send_sem=send_sems.at[idx], recv_sem=recv_sems.at[idx],
                                        device_id=dev, device_id_type=pl.DeviceIdType.MESH)


def _exchange_call(body, xs, out_shapes, n_copies, name):
    n = len(xs)
    return pl.pallas_call(
        body, name=name, in_specs=[_HBM] * n, out_specs=[_HBM] * n,
        out_shape=[jax.ShapeDtypeStruct(s, x.dtype) for s, x in zip(out_shapes, xs)],
        scratch_shapes=[pltpu.SemaphoreType.DMA((n, n_copies)), pltpu.SemaphoreType.DMA((n, n_copies)),
                        pltpu.SemaphoreType.DMA((n,))],
    )(*xs)


def _gather(xs, *, name):
    n = len(xs)

    def body(*refs):
        start, finish = _gather_steps(refs[:n], refs[n:2 * n], *refs[2 * n:])
        start()
        finish()

    return _exchange_call(body, xs, _gather_shapes(xs), _GATHER_COPIES, name)


_GATHER_COPIES = 7


def _gather_shapes(xs):
    return [(N_DEV,) + x.shape for x in xs]


def _gather_steps(x_refs, o_refs, send_sems, recv_sems, local_sems):
    n = len(x_refs)

    def plan():
        mx, my, mc = lax.axis_index("x"), lax.axis_index("y"), lax.axis_index("c")
        slot = lambda px, py, pc: 4 * px + 2 * py + pc
        me, sibling = (mx, my, mc), (mx, my, 1 - mc)
        chips = [(1 - mx, my), (mx, 1 - my), (1 - mx, 1 - my)]
        locals_, first = [], []
        for a in range(n):
            x_ref, o_ref = x_refs[a], o_refs[a]
            locals_.append(pltpu.make_async_copy(x_ref, o_ref.at[slot(*me)], local_sems.at[a]))
            first.append(_remote(x_ref, o_ref.at[slot(*me)], send_sems, recv_sems, (a, 0), sibling))
            for j, chip in enumerate(chips):
                first.append(_remote(x_ref, o_ref.at[slot(*me)], send_sems, recv_sems, (a, 1 + j), (*chip, mc)))
        return locals_, first, slot, me, sibling, chips, mc

    def start():
        locals_, first = plan()[:2]
        for cp in locals_ + first:
            cp.start()

    def finish():
        locals_, first, slot, me, sibling, chips, mc = plan()
        passed = []
        for j, chip in enumerate(chips):
            for a in range(n):
                landed = o_refs[a].at[slot(*chip, mc)]
                _remote(landed, landed, send_sems, recv_sems, (a, 1 + j), me).wait_recv()
                fwd = _remote(landed, landed, send_sems, recv_sems, (a, 4 + j), sibling)
                fwd.start()
                passed.append(fwd)
        for a in range(n):
            from_sib = o_refs[a].at[slot(*sibling)]
            _remote(from_sib, from_sib, send_sems, recv_sems, (a, 0), me).wait_recv()
            for j, chip in enumerate(chips):
                via_sib = o_refs[a].at[slot(*chip, 1 - mc)]
                _remote(via_sib, via_sib, send_sems, recv_sems, (a, 4 + j), me).wait_recv()
        for cp in first + passed:
            cp.wait_send()
        for cp in locals_:
            cp.wait()

    return start, finish


def _scatter_sibling(xs, *, name):
    n = len(xs)

    def body(*refs):
        x_refs, o_refs = refs[:n], refs[n:2 * n]
        send_sems, recv_sems, _ = refs[2 * n:]
        mx, my, mc = lax.axis_index("x"), lax.axis_index("y"), lax.axis_index("c")
        sibling = (mx, my, 1 - mc)
        sends = []
        for a in range(n):
            for ch in range(4):
                sends.append(_remote(x_refs[a].at[ch, 1 - mc], o_refs[a].at[ch], send_sems, recv_sems, (a, ch), sibling))
        for cp in sends:
            cp.start()
        for cp in sends:
            cp.wait_recv()
        for cp in sends:
            cp.wait_send()

    return _exchange_call(body, xs, [(4,) + x.shape[2:] for x in xs], 4, name)


def _scatter_chips(ys, *, name):
    n = len(ys)

    def body(*refs):
        start, finish = _chip_scatter_steps(refs[:n], refs[n:2 * n], *refs[2 * n:])
        start()
        finish()

    return _exchange_call(body, ys, _chip_scatter_shapes(ys), _CHIP_SCATTER_COPIES, name)


_CHIP_SCATTER_COPIES = 3


def _chip_scatter_shapes(ys):
    return [y.shape for y in ys]


def _chip_scatter_steps(y_refs, o_refs, send_sems, recv_sems, local_sems):
    n = len(y_refs)

    def plan():
        mx, my, mc = lax.axis_index("x"), lax.axis_index("y"), lax.axis_index("c")
        mine = 2 * mx + my
        chips = [(1 - mx, my), (mx, 1 - my), (1 - mx, 1 - my)]
        locals_, sends, recvs = [], [], []
        for a in range(n):
            locals_.append(pltpu.make_async_copy(y_refs[a].at[mine], o_refs[a].at[mine], local_sems.at[a]))
            for j, (px, py) in enumerate(chips):
                theirs = 2 * px + py
                sends.append(_remote(y_refs[a].at[theirs], o_refs[a].at[mine], send_sems, recv_sems, (a, j), (px, py, mc)))
                recvs.append(_remote(y_refs[a].at[theirs], o_refs[a].at[theirs], send_sems, recv_sems, (a, j), (px, py, mc)))
        return locals_, sends, recvs

    def start():
        locals_, sends, _ = plan()
        for cp in locals_ + sends:
            cp.start()

    def finish():
        locals_, sends, recvs = plan()
        for cp in recvs:
            cp.wait_recv()
        for cp in sends:
            cp.wait_send()
        for cp in locals_:
            cp.wait()

    return start, finish


def _pair_add(x, r, *, name):
    _, _, rows, c = x.shape
    tr = _tile(rows, 512, 16)

    def body(core_ref, x_ref, r_ref, o_ref):
        o_ref[...] = (x_ref[...].astype(F32) + r_ref[...].astype(F32)).astype(o_ref.dtype)

    core = lax.axis_index("c").astype(jnp.int32).reshape(1)
    return pl.pallas_call(
        body, name=name,
        grid_spec=pltpu.PrefetchScalarGridSpec(
            num_scalar_prefetch=1, grid=(4, rows // tr),
            in_specs=[pl.BlockSpec((None, None, tr, c), lambda ch, i, core: (ch, core[0], i, 0)),
                      pl.BlockSpec((None, tr, c), lambda ch, i, core: (ch, i, 0))],
            out_specs=pl.BlockSpec((None, tr, c), lambda ch, i, core: (ch, i, 0))),
        out_shape=jax.ShapeDtypeStruct((4, rows, c), x.dtype), compiler_params=_cparams(2),
    )(core, x, r)


def _adamw_reduce(parts, w, m, v, *, name):
    r, c = w.shape
    n_parts = parts.shape[0]
    tr = _tile(r, 256, 16)
    bc1 = 1.0 - ADAM_B1 ** ADAM_STEP
    bc2 = 1.0 - ADAM_B2 ** ADAM_STEP

    def body(p_ref, w_ref, m_ref, v_ref, g_ref, d_ref, nm_ref, nv_ref):
        g = p_ref[0].astype(F32)
        for q in range(1, n_parts):
            g = g + p_ref[q].astype(F32)
        nm = ADAM_B1 * m_ref[...] + (1.0 - ADAM_B1) * g
        nv = ADAM_B2 * v_ref[...] + (1.0 - ADAM_B2) * (g * g)
        g_ref[...] = g
        nm_ref[...] = nm
        nv_ref[...] = nv
        d_ref[...] = -ADAM_LR * ((nm / bc1) / (jnp.sqrt(nv / bc2) + ADAM_EPS) + ADAM_WD * w_ref[...])

    blk = pl.BlockSpec((tr, c), lambda i: (i, 0))
    return pl.pallas_call(
        body, name=name, grid=(r // tr,),
        in_specs=[pl.BlockSpec((n_parts, tr, c), lambda i: (0, i, 0)), blk, blk, blk], out_specs=[blk] * 4,
        out_shape=[jax.ShapeDtypeStruct((r, c), F32)] * 4, compiler_params=_cparams(1),
    )(parts, w, m, v)


def _col_full(g):
    return g.transpose(1, 2, 0, 3).reshape(g.shape[1], g.shape[2], -1)


def _col_parts(f):
    n, k, c8 = f.shape
    return f.reshape(n, k, N_DEV, c8 // N_DEV).transpose(2, 0, 1, 3)


def _row_full(g):
    return g.transpose(1, 0, 2, 3).reshape(g.shape[1], -1, g.shape[3])


def _row_parts(f):
    n, r8, c = f.shape
    return f.reshape(n, N_DEV, r8 // N_DEV, c).transpose(1, 0, 2, 3)


def _gu_full(g):
    n, d, c = g.shape[1:]
    return g.reshape(2, 4, n, d, c).transpose(2, 0, 3, 1, 4).reshape(n, 2, d, 4 * c)


def _gu_parts(f):
    n, _, d, c4 = f.shape
    return f.reshape(n, 2, d, 4, c4 // 4).transpose(1, 3, 0, 2, 4).reshape(N_DEV, n, d, c4 // 4)


def _ssd_in_full(g):
    return jnp.pad(_col_full(g), ((0, 0), (0, 0), (0, SSD_IN_PAD - SSD_IN_DIM)))


_MATMUL_WEIGHTS = (
    ("ffn1_w_gu", _gu_full, _gu_parts), ("ffn1_w_down", _row_full, _row_parts),
    ("ffn2_w_gu", _gu_full, _gu_parts), ("ffn2_w_down", _row_full, _row_parts),
    ("sb_w_qkv", _col_full, _col_parts), ("sb_w_o", _row_full, _row_parts),
    ("ssd_w_in", _ssd_in_full, _col_parts), ("ssd_w_out", _row_full, _row_parts),
    ("sc_w_in", _col_full, _col_parts), ("sc_w_out", _row_full, _row_parts),
)
_FIRST_WEIGHTS = ("ffn1_w_gu", "ffn1_w_down", "sb_w_qkv", "sb_w_o")
_CONV_WEIGHTS = ("ssd_conv_w", "sc_conv_w")
_REPLICATED = ("ffn1_norm", "mix_norm", "ffn2_norm", "final_norm", "ssd_conv_b", "ssd_norm",
               "ssd_dt_bias", "ssd_a_log", "ssd_d")
_ORDER = ("ffn1_norm", "ffn1_w_gu", "ffn1_w_down", "mix_norm", "ffn2_norm", "ffn2_w_gu", "ffn2_w_down",
          "sb_w_qkv", "sb_w_o", "ssd_w_in", "ssd_conv_w", "ssd_conv_b", "ssd_dt_bias", "ssd_a_log", "ssd_d",
          "ssd_norm", "ssd_w_out", "sc_w_in", "sc_conv_w", "sc_w_out", "final_norm")
_LANES = 1024


def _rows_of(a):
    flat = a.reshape(-1)
    pad = -flat.shape[0] % _LANES
    return jnp.pad(flat, (0, pad)).reshape(-1, _LANES)


def _pack_rows(arrays, mult):
    rows = [_rows_of(a) for a in arrays]
    packed = jnp.concatenate(rows, axis=0)
    pad = -packed.shape[0] % mult
    return jnp.pad(packed, ((0, pad), (0, 0))), [r.shape[0] for r in rows]


def _unpack_rows(packed, counts, shapes, lead=()):
    out, off = [], 0
    for n, shp in zip(counts, shapes):
        size = math.prod(shp)
        seg = packed[..., off:off + n, :].reshape(lead + (n * _LANES,))[..., :size]
        out.append(seg.reshape(lead + tuple(shp)))
        off += n
    return out


def kernel(x, ffn1_norm, ffn1_w_gu, ffn1_w_down, mix_norm, ffn2_norm, ffn2_w_gu, ffn2_w_down, sb_w_qkv, sb_w_o, ssd_w_in, ssd_conv_w, ssd_conv_b, ssd_dt_bias, ssd_a_log, ssd_d, ssd_norm, ssd_w_out, sc_w_in, sc_conv_w, sc_w_out, final_norm, loss_target, m_ffn1_norm, m_ffn1_w_gu, m_ffn1_w_down, m_mix_norm, m_ffn2_norm, m_ffn2_w_gu, m_ffn2_w_down, m_sb_w_qkv, m_sb_w_o, m_ssd_w_in, m_ssd_conv_w, m_ssd_conv_b, m_ssd_dt_bias, m_ssd_a_log, m_ssd_d, m_ssd_norm, m_ssd_w_out, m_sc_w_in, m_sc_conv_w, m_sc_w_out, m_final_norm, v_ffn1_norm, v_ffn1_w_gu, v_ffn1_w_down, v_mix_norm, v_ffn2_norm, v_ffn2_w_gu, v_ffn2_w_down, v_sb_w_qkv, v_sb_w_o, v_ssd_w_in, v_ssd_conv_w, v_ssd_conv_b, v_ssd_dt_bias, v_ssd_a_log, v_ssd_d, v_ssd_norm, v_ssd_w_out, v_sc_w_in, v_sc_conv_w, v_sc_w_out, v_final_norm):
    w = dict(ffn1_norm=ffn1_norm, ffn1_w_gu=ffn1_w_gu, ffn1_w_down=ffn1_w_down, mix_norm=mix_norm, ffn2_norm=ffn2_norm, ffn2_w_gu=ffn2_w_gu, ffn2_w_down=ffn2_w_down, sb_w_qkv=sb_w_qkv, sb_w_o=sb_w_o, ssd_w_in=ssd_w_in, ssd_conv_w=ssd_conv_w, ssd_conv_b=ssd_conv_b, ssd_dt_bias=ssd_dt_bias, ssd_a_log=ssd_a_log, ssd_d=ssd_d, ssd_norm=ssd_norm, ssd_w_out=ssd_w_out, sc_w_in=sc_w_in, sc_conv_w=sc_conv_w, sc_w_out=sc_w_out, final_norm=final_norm)
    mom = dict(ffn1_norm=m_ffn1_norm, ffn1_w_gu=m_ffn1_w_gu, ffn1_w_down=m_ffn1_w_down, mix_norm=m_mix_norm, ffn2_norm=m_ffn2_norm, ffn2_w_gu=m_ffn2_w_gu, ffn2_w_down=m_ffn2_w_down, sb_w_qkv=m_sb_w_qkv, sb_w_o=m_sb_w_o, ssd_w_in=m_ssd_w_in, ssd_conv_w=m_ssd_conv_w, ssd_conv_b=m_ssd_conv_b, ssd_dt_bias=m_ssd_dt_bias, ssd_a_log=m_ssd_a_log, ssd_d=m_ssd_d, ssd_norm=m_ssd_norm, ssd_w_out=m_ssd_w_out, sc_w_in=m_sc_w_in, sc_conv_w=m_sc_conv_w, sc_w_out=m_sc_w_out, final_norm=m_final_norm)
    var = dict(ffn1_norm=v_ffn1_norm, ffn1_w_gu=v_ffn1_w_gu, ffn1_w_down=v_ffn1_w_down, mix_norm=v_mix_norm, ffn2_norm=v_ffn2_norm, ffn2_w_gu=v_ffn2_w_gu, ffn2_w_down=v_ffn2_w_down, sb_w_qkv=v_sb_w_qkv, sb_w_o=v_sb_w_o, ssd_w_in=v_ssd_w_in, ssd_conv_w=v_ssd_conv_w, ssd_conv_b=v_ssd_conv_b, ssd_dt_bias=v_ssd_dt_bias, ssd_a_log=v_ssd_a_log, ssd_d=v_ssd_d, ssd_norm=v_ssd_norm, ssd_w_out=v_ssd_w_out, sc_w_in=v_sc_w_in, sc_conv_w=v_sc_conv_w, sc_w_out=v_sc_w_out, final_norm=v_final_norm)
    me = 4 * lax.axis_index("x") + 2 * lax.axis_index("y") + lax.axis_index("c")
    big = [n for n, _, _ in _MATMUL_WEIGHTS]
    two_d = lambda a: a.reshape(-1, a.shape[-1])

    to_full = {n: f for n, f, _ in _MATMUL_WEIGHTS}
    to_parts = {n: f for n, _, f in _MATMUL_WEIGHTS}
    first = [(n, 0) for n in _FIRST_WEIGHTS]
    later = [(n, i) for n in big for i in range(1 if n in _FIRST_WEIGHTS else 0, w[n].shape[0])]

    def shards(group):
        return [two_d(w[n][i].astype(BF16)) for n, i in group]

    def layers(group, gathered):
        out = {}
        for (n, i), g in zip(group, gathered):
            out.setdefault(n, []).append(to_full[n](g.reshape((N_DEV, 1) + w[n].shape[1:]))[0])
        return out

    gathered = _gather(shards(first) + [two_d(w[n]) for n in _CONV_WEIGHTS], name="gather_first")
    full = dict(w)
    full.update(layers(first, gathered))
    for n, g in zip(_CONV_WEIGHTS, gathered[len(first):]):
        full[n] = _col_full(g.reshape((N_DEV,) + w[n].shape))

    def host_of(n, i):
        if n.startswith("ffn2") and i == DEPTH - 1:
            return DEPTH - 1
        if (n.startswith("ffn1") and i == DEPTH - 1) or (n.startswith("sb_") and i == 1):
            return 1
        return 0

    gather_later = {}
    for host in (0, 1, DEPTH - 1):
        group = [(n, i) for n, i in later if host_of(n, i) == host]

        def merge(wd, gathered_group, group=group):
            wd = dict(wd)
            for n, ls in layers(group, gathered_group).items():
                have = wd[n] if isinstance(wd[n], list) else []
                wd[n] = have + ls
            return wd

        xs = shards(group)
        gather_later[host] = (_Hosted(_gather_steps, xs, _gather_shapes(xs), _GATHER_COPIES), merge)

    def chip_sums(group, grads, tag):
        parts = []
        for n, i in group:
            p8 = to_parts[n](grads[n][i][None].astype(BF16))
            parts.append(p8.reshape(4, 2, -1, p8.shape[-1]))
        from_sibling = _scatter_sibling(parts, name=f"scatter_sibling_{tag}")
        return [_pair_add(p, r, name=f"pair_add_{tag}_{n}{i}") for (n, i), p, r in zip(group, parts, from_sibling)]

    def scatter_early(grads):
        ys = chip_sums(later, grads, "later")
        return _Hosted(_chip_scatter_steps, ys, _chip_scatter_shapes(ys), _CHIP_SCATTER_COPIES)

    loss_part, dx, grads, recv_later = _local_step(x[0], loss_target[0], full, gather_later, scatter_early)
    loss = lax.psum(loss_part[0, 0], ("x", "y", "c"))

    recv_first = _scatter_chips(chip_sums(first, grads, "first"), name="scatter_chips_first")
    contrib = {n: [r] for (n, _), r in zip(first, recv_first)}
    for (n, _), r in zip(later, recv_later):
        contrib.setdefault(n, []).append(r)
    out_g, out_d, out_m, out_v = {}, {}, {}, {}

    def update(n, parts):
        res = _adamw_reduce(parts, two_d(w[n]), two_d(mom[n]), two_d(var[n]), name=f"adamw_{n}")
        out_g[n], out_d[n], out_m[n], out_v[n] = (r.reshape(w[n].shape) for r in res)

    for n in big:
        update(n, contrib[n][0] if len(contrib[n]) == 1 else jnp.concatenate(contrib[n], axis=1))

    small = list(_REPLICATED) + list(_CONV_WEIGHTS)
    small_shapes = [grads[n].shape for n in small]
    spacked, scounts = _pack_rows([grads[n].astype(F32) for n in small], 8)
    sg = _unpack_rows(_gather([spacked], name="gather_small_grads")[0], scounts, small_shapes, (N_DEV,))
    sg = dict(zip(small, sg))
    rep_w, rcounts = _pack_rows([w[n] for n in _REPLICATED], 8)
    rep_m, _ = _pack_rows([mom[n] for n in _REPLICATED], 8)
    rep_v, _ = _pack_rows([var[n] for n in _REPLICATED], 8)
    rep_p = jnp.concatenate([_rows_of(sg[n].reshape(N_DEV, -1)[q]) for q in range(N_DEV) for n in _REPLICATED], axis=0)
    rep_p = rep_p.reshape(N_DEV, -1, _LANES)
    rep_p = jnp.pad(rep_p, ((0, 0), (0, rep_w.shape[0] - rep_p.shape[1]), (0, 0)))
    res = _adamw_reduce(rep_p, rep_w, rep_m, rep_v, name="adamw_replicated")
    rep_shapes = [w[n].shape for n in _REPLICATED]
    for tgt, r in zip((out_g, out_d, out_m, out_v), res):
        for n, a in zip(_REPLICATED, _unpack_rows(r, rcounts, rep_shapes)):
            tgt[n] = a
    for n in _CONV_WEIGHTS:
        c = w[n].shape[-1]
        mine = lax.dynamic_slice_in_dim(sg[n], me * c, c, axis=sg[n].ndim - 1)
        update(n, mine.reshape(N_DEV, -1, c))

    return (loss, dx[None], *[out_g[n] for n in _ORDER], *[out_d[n] for n in _ORDER],
            *[out_m[n] for n in _ORDER], *[out_v[n] for n in _ORDER])
```

```python
import functools
import math

import jax
import jax.numpy as jnp
from jax import lax
from jax.experimental import pallas as pl
from jax.experimental.pallas import tpu as pltpu

F32 = jnp.float32
BF16 = jnp.bfloat16

D_MODEL = 1024
D_FF = 2816
DEPTH = 4
N_DEV = 8
SB_HEADS = 16
SB_HEAD_DIM = 64
SB_TILE = 256
SB_HEADS_PER_STEP = 2
SB_FWD_GROUPS = (2, 1)
SB_BWD_GROUPS = (4, 2, 1)
SSD_HEADS = 32
SSD_HEAD_DIM = 64
SSD_GROUPS = 8
SSD_HPG = 4
SSD_STATE = 128
SSD_CHUNK = 128
SSD_GROUPS_PER_STEP = 8
SSD_D_INNER = 2048
SSD_CONV_DIM = 4096
SSD_IN_DIM = 6176
SSD_IN_PAD = 6272
SSD_NORM_GROUP = 256
RMS_EPS = 1e-6
ADAM_LR = 0.001
ADAM_B1 = 0.9
ADAM_B2 = 0.999
ADAM_EPS = 1e-08
ADAM_WD = 0.01
ADAM_STEP = 10
VMEM_LIMIT = 60 * 1024 * 1024

NT = (((1,), (1,)), ((), ()))
TN = (((0,), (0,)), ((), ()))
NN = (((1,), (0,)), ((), ()))


def _cparams(n_axes):
    return pltpu.CompilerParams(dimension_semantics=("arbitrary",) * n_axes, vmem_limit_bytes=VMEM_LIMIT)


def _tile(n, want, mult=8):
    if n <= want:
        return n
    for t in range(want, 0, -1):
        if n % t == 0 and t % mult == 0:
            return t
    return n


def _sigmoid(x):
    return 1.0 / (1.0 + jnp.exp(-x))


def _dot(a, b, dn=NN):
    return lax.dot_general(a, b, dn, preferred_element_type=F32)


def _split3(x):
    x1 = x.astype(BF16)
    r1 = x - x1.astype(F32)
    x2 = r1.astype(BF16)
    x3 = (r1 - x2.astype(F32)).astype(BF16)
    return x1, x2, x3


def _dot_exact(x, t):
    x1, x2, x3 = _split3(x)
    return _dot(x1, t) + _dot(x2, t) + _dot(x3, t)


ROW_SUM_LANES = 1


def _cumsum_operand(tri):
    return jnp.concatenate([tri, tri], axis=0)


def _cumsum_rowsum(x, tri2):
    x1 = x.astype(BF16)
    x2 = (x - x1.astype(F32)).astype(BF16)
    return _dot(jnp.concatenate([x1, x2], axis=1), tri2), jnp.sum(x, axis=1, keepdims=True)


def _across_lanes(c, t):
    return c


def _mm(a, b, *, name, ta=False, tb=False, sa=False, sb=False, so=False, tm=512, tn=1024, tk=1024,
        out_dtype=F32, epilogue=None, extras=(), outs=None, pair=None, col_chunk=None):
    s_n = pair or (a.shape[0] if sa else (b.shape[0] if sb else 1))
    ash, bsh = a.shape[-2:], b.shape[-2:]
    m, k = (ash[1], ash[0]) if ta else ash
    n = bsh[0] if tb else bsh[1]
    tm, tn, tk = _tile(m, tm), _tile(n, tn, 128), _tile(k, tk, 128)
    nk = k // tk
    if outs is None:
        outs = [(out_dtype, "stile" if so else "tile")]
    if epilogue is None:
        epilogue = lambda acc: (acc,)

    if ta:
        a_blk, a_idx = (tk, tm), (lambda j, i, kk: (kk, i))
    else:
        a_blk, a_idx = (tm, tk), (lambda j, i, kk: (i, kk))
    if tb:
        b_blk, b_idx = (tn, tk), (lambda j, i, kk: (j, kk))
    else:
        b_blk, b_idx = (tk, tn), (lambda j, i, kk: (kk, j))

    def lead(blk, idx, has_s):
        if not has_s:
            return pl.BlockSpec(blk, idx)
        return pl.BlockSpec((s_n,) + blk, lambda j, i, kk: (0,) + idx(j, i, kk))

    kinds = {
        "tile": lambda: pl.BlockSpec((tm, tn), lambda j, i, kk: (i, j)),
        "stile": lambda: pl.BlockSpec((s_n, tm, tn), lambda j, i, kk: (0, i, j)),
        "row": lambda: pl.BlockSpec((1, tn), lambda j, i, kk: (0, j)),
        "colsum": lambda: pl.BlockSpec((1, tn), lambda j, i, kk: (0, j)),
    }
    shapes = {"tile": (m, n), "stile": (s_n, m, n), "colsum": (1, n)}
    in_specs = [lead(a_blk, a_idx, sa), lead(b_blk, b_idx, sb)] + [kinds[kd]() for _, kd in extras]
    out_specs = [kinds[kd]() for _, kd in outs]
    out_shape = [jax.ShapeDtypeStruct(shapes[kd], dt) for dt, kd in outs]
    n_ex, n_out = len(extras), len(outs)
    dn = ((((0,) if ta else (1,)), ((1,) if tb else (0,))), ((), ()))
    acc_shape = (s_n, tm, tn) if so else (tm, tn)

    def body(*refs):
        a_ref, b_ref = refs[0], refs[1]
        ex_refs = refs[2:2 + n_ex]
        o_refs = refs[2 + n_ex:2 + n_ex + n_out]
        i = pl.program_id(1)
        kk = pl.program_id(2)

        def products():
            for s in range(s_n if (sa or sb) else 1):
                av = (a_ref[s] if sa else a_ref[...]).astype(BF16)
                bv = (b_ref[s] if sb else b_ref[...]).astype(BF16)
                yield s, lax.dot_general(av, bv, dn, preferred_element_type=F32)

        def finish(accv):
            vals = epilogue(accv, *[r[...] for r in ex_refs])
            for (dt, kd), o_ref, val in zip(outs, o_refs, vals):
                if kd == "colsum":
                    _accumulate(o_ref, val, i == 0)
                elif kd == "stile":
                    for s in range(s_n):
                        o_ref[s] = val[s].astype(dt)
                else:
                    o_ref[...] = val.astype(dt)

        if nk == 1 and col_chunk:
            bounds = [(c0, min(col_chunk, tn - c0)) for c0 in range(0, tn, col_chunk)]
            n_s = s_n if (sa or sb) else 1
            a_vals = [(a_ref[s] if sa else a_ref[...]).astype(BF16) for s in range(n_s if sa else 1)]
            accs = []
            for c0, cw in bounds:
                ds = []
                for s in range(n_s):
                    idx = ((s,) if sb else ()) + ((pl.ds(c0, cw), slice(None)) if tb else (slice(None), pl.ds(c0, cw)))
                    ds.append(lax.dot_general(a_vals[s if sa else 0], b_ref[idx].astype(BF16), dn,
                                              preferred_element_type=F32))
                accs.append(tuple(ds) if so else functools.reduce(jnp.add, ds))
            for (c0, cw), accv in zip(bounds, accs):
                cols = pl.ds(c0, cw)
                exv = [r[:, :, cols] if kd == "stile" else r[:, cols] for r, (_, kd) in zip(ex_refs, extras)]
                vals = epilogue(accv, *exv)
                for (dt, kd), o_ref, val in zip(outs, o_refs, vals):
                    if kd == "stile":
                        for s in range(s_n):
                            o_ref[s, :, cols] = val[s].astype(dt)
                    else:
                        o_ref[:, cols] = val.astype(dt)
            return

        if nk == 1:
            ds = [d for _, d in products()]
            finish(tuple(ds) if so else functools.reduce(jnp.add, ds))
            return

        acc = refs[-1]

        @pl.when(kk == 0)
        def _():
            acc[...] = jnp.zeros_like(acc)

        for s, d in products():
            if so:
                acc[s] += d
            else:
                acc[...] += d

        @pl.when(kk == nk - 1)
        def _():
            finish(tuple(acc[s] for s in range(s_n)) if so else acc[...])

    res = pl.pallas_call(
        body, name=name, grid=(n // tn, m // tm, nk),
        in_specs=in_specs, out_specs=out_specs, out_shape=out_shape,
        scratch_shapes=[pltpu.VMEM(acc_shape, F32)] if nk > 1 else [], compiler_params=_cparams(3),
    )(a, b, *[e for e, _ in extras])
    return res[0] if len(res) == 1 else res


def _accumulate(o_ref, val, first):
    @pl.when(first)
    def _():
        o_ref[...] = val

    @pl.when(jnp.logical_not(first))
    def _():
        o_ref[...] += val


def _rmsnorm(x, g, *, name):
    l, d = x.shape
    tm = _tile(l, 512)

    def body(x_ref, g_ref, o_ref):
        xv = x_ref[...]
        r = lax.rsqrt(jnp.mean(xv * xv, axis=1, keepdims=True) + RMS_EPS)
        o_ref[...] = (xv * r * g_ref[...]).astype(BF16)

    return pl.pallas_call(
        body, name=name, grid=(l // tm,),
        in_specs=[pl.BlockSpec((tm, d), lambda i: (i, 0)), pl.BlockSpec((1, d), lambda i: (0, 0))],
        out_specs=pl.BlockSpec((tm, d), lambda i: (i, 0)),
        out_shape=jax.ShapeDtypeStruct((l, d), BF16), compiler_params=_cparams(1),
    )(x, g)


def _norm_bwd_epilogue(dh, x, g, dres):
    r = lax.rsqrt(jnp.mean(x * x, axis=1, keepdims=True) + RMS_EPS)
    xh = x * r
    dg = jnp.sum(dh * xh, axis=0, keepdims=True)
    dxh = dh * g
    dx = r * (dxh - xh * jnp.mean(dxh * xh, axis=1, keepdims=True))
    return dres + dx, dg


def _final_loss(x, g, tgt, *, name):
    l, d = x.shape
    tm = _tile(l, 512)

    def body(x_ref, g_ref, t_ref, loss_ref, dx_ref, dg_ref):
        i = pl.program_id(0)
        xv, gv = x_ref[...], g_ref[...]
        r = lax.rsqrt(jnp.mean(xv * xv, axis=1, keepdims=True) + RMS_EPS)
        xh = xv * r
        e = xh * gv - t_ref[...]
        part = 0.5 * jnp.sum(jnp.mean(e * e, axis=1, keepdims=True), axis=0, keepdims=True)
        dy = e * (1.0 / d)
        dg = jnp.sum(dy * xh, axis=0, keepdims=True)
        dxh = dy * gv
        dx_ref[...] = r * (dxh - xh * jnp.mean(dxh * xh, axis=1, keepdims=True))
        _accumulate(dg_ref, dg, i == 0)
        _accumulate(loss_ref, jnp.broadcast_to(part, (1, 128)), i == 0)

    return pl.pallas_call(
        body, name=name, grid=(l // tm,),
        in_specs=[pl.BlockSpec((tm, d), lambda i: (i, 0)), pl.BlockSpec((1, d), lambda i: (0, 0)),
                  pl.BlockSpec((tm, d), lambda i: (i, 0))],
        out_specs=[pl.BlockSpec((1, 128), lambda i: (0, 0)), pl.BlockSpec((tm, d), lambda i: (i, 0)),
                   pl.BlockSpec((1, d), lambda i: (0, 0))],
        out_shape=[jax.ShapeDtypeStruct((1, 128), F32), jax.ShapeDtypeStruct((l, d), F32),
                   jax.ShapeDtypeStruct((1, d), F32)],
        compiler_params=_cparams(1),
    )(x, g, tgt)


def _ffn_fwd(x, g, wgu, wd, tag):
    h = _rmsnorm(x, g, name=f"{tag}_norm")

    def act(acc):
        gate, up = acc
        return acc, gate * _sigmoid(gate) * up

    gu, a = _mm(h, wgu, sb=True, so=True, tm=512, tn=1408, tk=1024, name=f"{tag}_up",
                outs=[(BF16, "stile"), (BF16, "tile")], epilogue=act)
    xo = _mm(a, wd, tm=512, tn=1024, tk=2816, name=f"{tag}_down", extras=[(x, "tile")],
             epilogue=lambda acc, xt: (xt + 0.5 * acc,))
    return xo, (x, h, gu, a)


def _ffn_bwd(dout, saved, g, wgu, wd, tag):
    x, h, gu, a = saved

    def act_bwd(acc, guv):
        da = 0.5 * acc
        gate, up = guv[0].astype(F32), guv[1].astype(F32)
        s = _sigmoid(gate)
        return ((da * up * s * (1.0 + gate * (1.0 - s)), da * gate * s),)

    dgu = _mm(dout, wd, tb=True, pair=2, tm=512, tn=1408, tk=1024, col_chunk=384, name=f"{tag}_dact", extras=[(gu, "stile")],
              outs=[(BF16, "stile")], epilogue=act_bwd)
    dwd = _mm(a, dout, ta=True, tm=1408, tn=1024, tk=2048, name=f"{tag}_dwd", out_dtype=BF16,
              epilogue=lambda acc: (0.5 * acc,))
    dwgu = _mm(h, dgu, ta=True, sb=True, so=True, tm=512, tn=1408, tk=2048, name=f"{tag}_dwgu", out_dtype=BF16)
    dx, dg = _mm(dgu, wgu, tb=True, sa=True, sb=True, tm=512, tn=1024, tk=2816, name=f"{tag}_dx",
                 extras=[(x, "tile"), (g, "row"), (dout, "tile")], outs=[(F32, "tile"), (F32, "colsum")],
                 epilogue=_norm_bwd_epilogue)
    return dx, dg, dwgu, dwd


def _sb_plan(n, sizes):
    digits = [n // sizes[0]] + [(n // s) % 2 for s in sizes[1:]]
    plan, none_smaller = [], 1
    for size, d in reversed(list(zip(sizes, digits))):
        has = jnp.minimum(d, 1)
        with_diag = none_smaller * has
        plan.append((size, True, with_diag))
        if size > 1:
            plan.append((size, False, d - with_diag))
        none_smaller = none_smaller * (1 - has)
    return plan


def _sb_sweep(plan, start, step, fn, carry):
    pos = start
    for size, with_diag, trips in plan:
        diag = 0 if step < 0 else size - 1

        def trip(it, cr, size=size, with_diag=with_diag, pos=pos, diag=diag):
            base = pos + step * size * it
            return fn([base + step * b for b in range(size)], cr, [with_diag and b == diag for b in range(size)])

        carry = lax.fori_loop(0, trips, trip, carry)
        pos = pos + step * size * trips
    return carry


def _sb_logs(z):
    lb = jnp.minimum(z, 0.0) - jnp.log(1.0 + jnp.exp(-jnp.abs(z)))
    return lb, lb - z


class _Hosted:
    def __init__(self, steps, xs, out_shapes, copies):
        self.steps, self.xs, self.n, self.copies = steps, list(xs), len(xs), copies
        self.out_shape = [jax.ShapeDtypeStruct(s, x.dtype) for s, x in zip(out_shapes, xs)]
        self.specs = [_HBM] * self.n
        self.sems = [pltpu.SemaphoreType.DMA((self.n, copies)), pltpu.SemaphoreType.DMA((self.n, copies)),
                     pltpu.SemaphoreType.DMA((self.n,))]

    def run(self, x_refs, o_refs, sems, grid):
        ids = [pl.program_id(a) for a in range(len(grid))]
        first = functools.reduce(jnp.logical_and, [p == 0 for p in ids])
        last = functools.reduce(jnp.logical_and, [p == g - 1 for p, g in zip(ids, grid)])
        start, finish = self.steps(x_refs, o_refs, *sems)
        pl.when(first)(start)
        return lambda: pl.when(last)(finish)


def _head_masks(hs):
    lane = lax.broadcasted_iota(jnp.int32, (1, hs * SB_HEAD_DIM), 1)
    return [jnp.logical_and(lane >= hh * SB_HEAD_DIM, lane < (hh + 1) * SB_HEAD_DIM) for hh in range(hs)]


def _sb_fwd(qkv, *, name, hosted=None):
    l = qkv.shape[0]
    d_model = qkv.shape[1] // 3
    dh = SB_HEAD_DIM
    t = _tile(l, SB_TILE)
    hs = 2 * SB_HEADS_PER_STEP
    w = hs * dh
    n_grp = d_model // w
    scale = dh ** -0.5
    grid = (n_grp, l // t)
    nh = hosted.n if hosted else 0

    def body(q_ref, k_ref, v_ref, *rest):
        o_ref = rest[nh]
        at_end = hosted.run(rest[:nh], rest[nh + 1:2 * nh + 1], rest[2 * nh + 1:], grid) if hosted else None
        i = pl.program_id(1)
        heads = _head_masks(hs)
        q_all = (q_ref[...].astype(F32) * scale).astype(BF16)
        qs = [jnp.where(heads[hh], q_all, jnp.zeros_like(q_all)) for hh in range(hs)]
        row = lax.broadcasted_iota(jnp.int32, (t, t), 0)
        col = lax.broadcasted_iota(jnp.int32, (t, t), 1)
        strict = col < row
        tri = _cumsum_operand(strict.astype(BF16))

        def block(jbs, carry, masks):
            sls = [pl.ds(pl.multiple_of(jb * t, t), t) for jb in jbs]
            chains = [(hh, b) for b in range(len(jbs)) for hh in range(hs)]
            ks = [k_ref[sl, :] for sl in sls]
            zs = {(hh, b): _dot(qs[hh], ks[b], NT) for hh, b in chains}
            lbs, tails, sums = {}, {}, {}
            for hh, b in chains:
                lb, lk = _sb_logs(zs[hh, b])
                if masks[b]:
                    lk = jnp.where(strict, lk, 0.0)
                lbs[hh, b] = lb
                tails[hh, b], sums[hh, b] = _cumsum_rowsum(lk, tri)
            cs, o = list(carry[0]), carry[1]
            for b in range(len(jbs)):
                atts = []
                for hh in range(hs):
                    att = jnp.exp(lbs[hh, b] + tails[hh, b] + _across_lanes(cs[hh], t))
                    if masks[b]:
                        att = jnp.where(strict, att, 0.0)
                    atts.append(att.astype(BF16))
                    cs[hh] = cs[hh] + sums[hh, b]
                vb = v_ref[sls[b], :]
                v_heads = jnp.concatenate([jnp.where(heads[hh], vb, jnp.zeros_like(vb)) for hh in range(hs)], axis=0)
                o = o + _dot(jnp.concatenate(atts, axis=1), v_heads)
            return tuple(cs), o

        carry = (tuple(jnp.zeros((t, ROW_SUM_LANES), F32) for _ in range(hs)), jnp.zeros((t, w), F32))
        carry = _sb_sweep(_sb_plan(i + 1, SB_FWD_GROUPS), i, -1, block, carry)
        o_ref[...] = carry[1].astype(o_ref.dtype)
        if hosted:
            at_end()

    blocks = d_model // w
    res = pl.pallas_call(
        body, name=name, grid=grid,
        in_specs=[pl.BlockSpec((t, w), lambda g, i: (i, g)), pl.BlockSpec((l, w), lambda g, i: (0, blocks + g)),
                  pl.BlockSpec((l, w), lambda g, i: (0, 2 * blocks + g))] + (hosted.specs if hosted else []),
        out_specs=[pl.BlockSpec((t, w), lambda g, i: (i, g))] + (hosted.specs if hosted else []),
        out_shape=[jax.ShapeDtypeStruct((l, d_model), BF16)] + (hosted.out_shape if hosted else []),
        scratch_shapes=hosted.sems if hosted else [], compiler_params=_cparams(2),
    )(qkv, qkv, qkv, *(hosted.xs if hosted else []))
    return (res[0], res[1:]) if hosted else res[0]


def _sb_bwd(qkv, do, *, name, hosted=None):
    l = qkv.shape[0]
    d_model = qkv.shape[1] // 3
    dh = SB_HEAD_DIM
    t = _tile(l, SB_TILE)
    nq = l // t
    hs = SB_HEADS_PER_STEP
    w = hs * dh
    blocks = d_model // w
    scale = dh ** -0.5
    grid = (blocks, nq)
    nh = hosted.n if hosted else 0

    def body(q_ref, k_ref, v_ref, do_ref, *rest):
        dq_ref, dk_ref, dv_ref = rest[nh:nh + 3]
        e_scr, s_scr = rest[2 * nh + 3:2 * nh + 5]
        at_end = hosted.run(rest[:nh], rest[nh + 3:2 * nh + 3], rest[2 * nh + 5:], grid) if hosted else None
        i = pl.program_id(1)

        @pl.when(i == 0)
        def _():
            dk_ref[...] = jnp.zeros_like(dk_ref)
            dv_ref[...] = jnp.zeros_like(dv_ref)

        heads = _head_masks(hs)
        q_all = (q_ref[...].astype(F32) * scale).astype(BF16)
        do_all = do_ref[...]
        qs = [jnp.where(heads[hh], q_all, jnp.zeros_like(q_all)) for hh in range(hs)]
        dos = [jnp.where(heads[hh], do_all, jnp.zeros_like(do_all)) for hh in range(hs)]
        row = lax.broadcasted_iota(jnp.int32, (t, t), 0)
        col = lax.broadcasted_iota(jnp.int32, (t, t), 1)
        strict = col < row
        tri_suffix = _cumsum_operand(strict.astype(BF16))
        tri_prefix = _cumsum_operand((row < col).astype(BF16))

        def sweep1(jbs, cs, masks):
            sls = [pl.ds(pl.multiple_of(jb * t, t), t) for jb in jbs]
            chains = [(hh, b) for b in range(len(jbs)) for hh in range(hs)]
            zs = {(hh, b): _dot(qs[hh], k_ref[sls[b], :], NT) for hh, b in chains}
            datts = {(hh, b): _dot(dos[hh], v_ref[sls[b], :], NT) for hh, b in chains}
            lbs, tails, sums = {}, {}, {}
            for hh, b in chains:
                lb, lk = _sb_logs(zs[hh, b])
                if masks[b]:
                    lk = jnp.where(strict, lk, 0.0)
                lbs[hh, b] = lb
                tails[hh, b], sums[hh, b] = _cumsum_rowsum(lk, tri_suffix)
                s_scr[hh, jbs[b]] = jnp.exp(lb)
            cs = list(cs)
            for hh, b in chains:
                att = jnp.exp(lbs[hh, b] + tails[hh, b] + _across_lanes(cs[hh], t))
                if masks[b]:
                    att = jnp.where(strict, att, 0.0)
                e_scr[hh, jbs[b]] = att * datts[hh, b]
                dv_ref[sls[b], :] += _dot(att.astype(BF16), dos[hh], TN)
                cs[hh] = cs[hh] + sums[hh, b]
            return tuple(cs)

        plan = _sb_plan(i + 1, SB_BWD_GROUPS)
        _sb_sweep(plan, i, -1, sweep1, tuple(jnp.zeros((t, ROW_SUM_LANES), F32) for _ in range(hs)))

        def sweep2(jbs, carry, masks):
            sls = [pl.ds(pl.multiple_of(jb * t, t), t) for jb in jbs]
            chains = [(hh, b) for b in range(len(jbs)) for hh in range(hs)]
            des = {(hh, b): e_scr[hh, jbs[b]] for hh, b in chains}
            pres = {(hh, b): _cumsum_rowsum(des[hh, b], tri_prefix) for hh, b in chains}
            carry = [list(c) for c in carry]
            for hh, b in chains:
                p, dq = carry[hh]
                de, sg = des[hh, b], s_scr[hh, jbs[b]]
                dlk = _across_lanes(p, t) + pres[hh, b][0]
                if masks[b]:
                    dlk = jnp.where(strict, dlk, 0.0)
                dz = (de - sg * (de + dlk)).astype(BF16)
                dk_ref[sls[b], :] += _dot(dz, qs[hh], TN)
                carry[hh] = [p + pres[hh, b][1], dq + _dot(dz, k_ref[sls[b], :])]
            return tuple(tuple(c) for c in carry)

        carry = tuple((jnp.zeros((t, ROW_SUM_LANES), F32), jnp.zeros((t, w), F32)) for _ in range(hs))
        carry = _sb_sweep(plan[::-1], 0, 1, sweep2, carry)
        dq = jnp.zeros((t, w), F32)
        for hh in range(hs):
            dq = jnp.where(heads[hh], carry[hh][1], dq)
        dq_ref[...] = (dq * scale).astype(dq_ref.dtype)
        if hosted:
            at_end()

    qspec = pl.BlockSpec((t, w), lambda g, i: (i, g))
    cols = lambda off: pl.BlockSpec((l, w), lambda g, i: (0, off + g))
    res = pl.pallas_call(
        body, name=name, grid=grid,
        in_specs=[qspec, cols(blocks), cols(2 * blocks), qspec] + (hosted.specs if hosted else []),
        out_specs=[qspec, cols(0), cols(0)] + (hosted.specs if hosted else []),
        out_shape=[jax.ShapeDtypeStruct((l, d_model), BF16), jax.ShapeDtypeStruct((l, d_model), F32),
                   jax.ShapeDtypeStruct((l, d_model), F32)] + (hosted.out_shape if hosted else []),
        scratch_shapes=[pltpu.VMEM((hs, nq, t, t), F32), pltpu.VMEM((hs, nq, t, t), F32)]
        + (hosted.sems if hosted else []),
        compiler_params=_cparams(2),
    )(qkv, qkv, qkv, do, *(hosted.xs if hosted else []))
    return (res[0], res[1], res[2], res[3:]) if hosted else res


def _sb_layer_fwd(x, g, wqkv, wo, tag, hosted=None):
    h = _rmsnorm(x, g, name=f"{tag}_norm")
    qkv = _mm(h, wqkv, tm=1024, tn=1024, tk=1024, name=f"{tag}_qkv", out_dtype=BF16)
    o = _sb_fwd(qkv, name=f"{tag}_attn", hosted=hosted)
    carried = None
    if hosted:
        o, carried = o
    xo = _mm(o, wo, tm=1024, tn=1024, tk=1024, name=f"{tag}_out", extras=[(x, "tile")],
             epilogue=lambda acc, xt: (xt + acc,))
    return xo, (x, h, qkv, o), carried


def _sb_layer_bwd(dout, saved, g, wqkv, wo, tag, hosted=None):
    x, h, qkv, o = saved
    do = _mm(dout, wo, tb=True, tm=1024, tn=1024, tk=1024, name=f"{tag}_do", out_dtype=BF16)
    dwo = _mm(o, dout, ta=True, tm=1024, tn=1024, tk=2048, name=f"{tag}_dwo", out_dtype=BF16)
    res = _sb_bwd(qkv, do, name=f"{tag}_attn_bwd", hosted=hosted)
    dq, dk, dv = res[:3]
    carried = res[3] if hosted else None
    dqkv = jnp.concatenate([dq, dk.astype(BF16), dv.astype(BF16)], axis=1)
    dwqkv = _mm(h, dqkv, ta=True, tm=1024, tn=1024, tk=2048, name=f"{tag}_dwqkv", out_dtype=BF16)
    dx, dg = _mm(dqkv, wqkv, tb=True, tm=512, tn=1024, tk=3072, name=f"{tag}_dx",
                 extras=[(x, "tile"), (g, "row"), (dout, "tile")], outs=[(F32, "tile"), (F32, "colsum")],
                 epilogue=_norm_bwd_epilogue)
    return dx, dg, dwqkv, dwo, carried


def _shift_down(x, s, t_idx):
    return jnp.where(t_idx >= s, pltpu.roll(x, s, 0), 0.0)


def _shift_up(x, s, t_idx):
    n = x.shape[0]
    return jnp.where(t_idx < n - s, pltpu.roll(x, n - s, 0), 0.0)


def _sc_fwd(p, cw, *, name):
    l = p.shape[0]
    d = cw.shape[1]
    tc = 128
    nb = d // tc

    def body(b_ref, c_ref, h_ref, w_ref, o_ref):
        v = c_ref[...] * h_ref[...]
        t_idx = lax.broadcasted_iota(jnp.int32, v.shape, 0)
        u = v * w_ref[2:3, :] + _shift_down(v, 1, t_idx) * w_ref[1:2, :] + _shift_down(v, 2, t_idx) * w_ref[0:1, :]
        o_ref[...] = (b_ref[...] * u).astype(BF16)

    return pl.pallas_call(
        body, name=name, grid=(nb,),
        in_specs=[pl.BlockSpec((l, tc), lambda j: (0, j)), pl.BlockSpec((l, tc), lambda j: (0, nb + j)),
                  pl.BlockSpec((l, tc), lambda j: (0, 2 * nb + j)), pl.BlockSpec((3, tc), lambda j: (0, j))],
        out_specs=pl.BlockSpec((l, tc), lambda j: (0, j)),
        out_shape=jax.ShapeDtypeStruct((l, d), BF16), compiler_params=_cparams(1),
    )(p, p, p, cw)


def _sc_bwd(p, cw, dbu, *, name):
    l = p.shape[0]
    d = cw.shape[1]
    tc = 128
    nb = d // tc

    def body(b_ref, c_ref, h_ref, w_ref, g_ref, db_ref, dc_ref, dh_ref, dw_ref):
        cv, hv = c_ref[...], h_ref[...]
        v = cv * hv
        t_idx = lax.broadcasted_iota(jnp.int32, v.shape, 0)
        v1, v2 = _shift_down(v, 1, t_idx), _shift_down(v, 2, t_idx)
        u = v * w_ref[2:3, :] + v1 * w_ref[1:2, :] + v2 * w_ref[0:1, :]
        dbu_v = g_ref[...]
        db_ref[...] = (dbu_v * u).astype(BF16)
        du = dbu_v * b_ref[...]
        dv = du * w_ref[2:3, :] + _shift_up(du, 1, t_idx) * w_ref[1:2, :] + _shift_up(du, 2, t_idx) * w_ref[0:1, :]
        dc_ref[...] = (dv * hv).astype(BF16)
        dh_ref[...] = (dv * cv).astype(BF16)
        dw_ref[...] = jnp.zeros_like(dw_ref)
        dw_ref[0:1, :] = jnp.sum(du * v2, axis=0, keepdims=True)
        dw_ref[1:2, :] = jnp.sum(du * v1, axis=0, keepdims=True)
        dw_ref[2:3, :] = jnp.sum(du * v, axis=0, keepdims=True)

    col = lambda off: pl.BlockSpec((l, tc), lambda j: (0, off + j))
    return pl.pallas_call(
        body, name=name, grid=(nb,),
        in_specs=[col(0), col(nb), col(2 * nb), pl.BlockSpec((3, tc), lambda j: (0, j)), col(0)],
        out_specs=[col(0), col(0), col(0), pl.BlockSpec((8, tc), lambda j: (0, j))],
        out_shape=[jax.ShapeDtypeStruct((l, d), BF16)] * 3 + [jax.ShapeDtypeStruct((8, d), F32)],
        compiler_params=_cparams(1),
    )(p, p, p, cw, dbu)


def _sc_layer_fwd(x, g, win, cw, wout, tag):
    h = _rmsnorm(x, g, name=f"{tag}_norm")
    p = _mm(h, win, tm=1024, tn=1024, tk=1024, name=f"{tag}_in")
    bu = _sc_fwd(p, cw, name=f"{tag}_conv")
    xo = _mm(bu, wout, tm=1024, tn=1024, tk=1024, name=f"{tag}_out", extras=[(x, "tile")],
             epilogue=lambda acc, xt: (xt + acc,))
    return xo, (x, h, p, bu)


def _sc_layer_bwd(dout, saved, g, win, cw, wout, tag):
    x, h, p, bu = saved
    dbu = _mm(dout, wout, tb=True, tm=1024, tn=1024, tk=1024, name=f"{tag}_dbu")
    dwout = _mm(bu, dout, ta=True, tm=1024, tn=1024, tk=2048, name=f"{tag}_dwout", out_dtype=BF16)
    db, dc, dh, dcw = _sc_bwd(p, cw, dbu, name=f"{tag}_conv_bwd")
    dp = jnp.concatenate([db, dc, dh], axis=1)
    dwin = _mm(h, dp, ta=True, tm=1024, tn=1024, tk=2048, name=f"{tag}_dwin", out_dtype=BF16)
    dx, dg = _mm(dp, win, tb=True, tm=512, tn=1024, tk=3072, name=f"{tag}_dx",
                 extras=[(x, "tile"), (g, "row"), (dout, "tile")], outs=[(F32, "tile"), (F32, "colsum")],
                 epilogue=_norm_bwd_epilogue)
    return dx, dg, dwin, dcw[:3], dwout


def _ssd_conv_fwd(p, cw, cb, *, name):
    l = p.shape[0]
    tc = 128
    nb = SSD_CONV_DIM // tc
    off = SSD_D_INNER // tc

    def body(x_ref, w_ref, b_ref, o_ref):
        xv = x_ref[...]
        t_idx = lax.broadcasted_iota(jnp.int32, xv.shape, 0)
        pre = xv * w_ref[3:4, :] + b_ref[...]
        for s in (1, 2, 3):
            pre = pre + _shift_down(xv, s, t_idx) * w_ref[3 - s:4 - s, :]
        o_ref[...] = pre * _sigmoid(pre)

    return pl.pallas_call(
        body, name=name, grid=(nb,),
        in_specs=[pl.BlockSpec((l, tc), lambda j: (0, off + j)), pl.BlockSpec((4, tc), lambda j: (0, j)),
                  pl.BlockSpec((1, tc), lambda j: (0, j))],
        out_specs=pl.BlockSpec((l, tc), lambda j: (0, j)),
        out_shape=jax.ShapeDtypeStruct((l, SSD_CONV_DIM), F32), compiler_params=_cparams(1),
    )(p, cw, cb)


def _ssd_conv_bwd(p, cw, cb, dxs_scan, dxs_skip, dbm, dcm, *, name):
    l = p.shape[0]
    tc = 128
    nb = SSD_CONV_DIM // tc
    off = SSD_D_INNER // tc
    n_xs, n_b = SSD_D_INNER // tc, SSD_GROUPS * SSD_STATE // tc

    def body(x_ref, w_ref, b_ref, ga_ref, gb_ref, gm_ref, gc_ref, dx_ref, dw_ref):
        j = pl.program_id(0)
        g_val = jnp.where(j < n_xs, ga_ref[...] + gb_ref[...], jnp.where(j < n_xs + n_b, gm_ref[...], gc_ref[...]))
        xv = x_ref[...]
        t_idx = lax.broadcasted_iota(jnp.int32, xv.shape, 0)
        xs = [xv] + [_shift_down(xv, s, t_idx) for s in (1, 2, 3)]
        pre = b_ref[...] + xs[0] * w_ref[3:4, :]
        for s in (1, 2, 3):
            pre = pre + xs[s] * w_ref[3 - s:4 - s, :]
        sg = _sigmoid(pre)
        dpre = g_val * sg * (1.0 + pre * (1.0 - sg))
        dx = dpre * w_ref[3:4, :]
        for s in (1, 2, 3):
            dx = dx + _shift_up(dpre, s, t_idx) * w_ref[3 - s:4 - s, :]
        dx_ref[...] = dx
        dw_ref[...] = jnp.zeros_like(dw_ref)
        for s in (0, 1, 2, 3):
            dw_ref[3 - s:4 - s, :] = jnp.sum(dpre * xs[s], axis=0, keepdims=True)
        dw_ref[4:5, :] = jnp.sum(dpre, axis=0, keepdims=True)

    return pl.pallas_call(
        body, name=name, grid=(nb,),
        in_specs=[pl.BlockSpec((l, tc), lambda j: (0, off + j)), pl.BlockSpec((4, tc), lambda j: (0, j)),
                  pl.BlockSpec((1, tc), lambda j: (0, j)),
                  pl.BlockSpec((l, tc), lambda j: (0, jnp.minimum(j, n_xs - 1))),
                  pl.BlockSpec((l, tc), lambda j: (0, jnp.minimum(j, n_xs - 1))),
                  pl.BlockSpec((l, tc), lambda j: (0, jnp.clip(j - n_xs, 0, n_b - 1))),
                  pl.BlockSpec((l, tc), lambda j: (0, jnp.clip(j - n_xs - n_b, 0, n_b - 1)))],
        out_specs=[pl.BlockSpec((l, tc), lambda j: (0, j)), pl.BlockSpec((8, tc), lambda j: (0, j))],
        out_shape=[jax.ShapeDtypeStruct((l, SSD_CONV_DIM), F32), jax.ShapeDtypeStruct((8, SSD_CONV_DIM), F32)],
        compiler_params=_cparams(1),
    )(p, cw, cb, dxs_scan, dxs_skip, dbm, dcm)


def _ssd_dt_fwd(p, bias, *, name):
    l = p.shape[0]
    tm = _tile(l, 1024)
    off = (SSD_D_INNER + SSD_CONV_DIM) // 128

    def body(x_ref, b_ref, o_ref):
        v = x_ref[...] + b_ref[...]
        o_ref[...] = jnp.maximum(v, 0.0) + jnp.log(1.0 + jnp.exp(-jnp.abs(v)))

    return pl.pallas_call(
        body, name=name, grid=(l // tm,),
        in_specs=[pl.BlockSpec((tm, 128), lambda i: (i, off)), pl.BlockSpec((1, 128), lambda i: (0, 0))],
        out_specs=pl.BlockSpec((tm, 128), lambda i: (i, 0)),
        out_shape=jax.ShapeDtypeStruct((l, 128), F32), compiler_params=_cparams(1),
    )(p, bias)


def _ssd_dt_bwd(p, bias, ddt, *, name):
    l = p.shape[0]
    tm = _tile(l, 1024)
    off = (SSD_D_INNER + SSD_CONV_DIM) // 128

    def body(x_ref, b_ref, g_ref, o_ref, db_ref):
        i = pl.program_id(0)
        d = g_ref[...] * _sigmoid(x_ref[...] + b_ref[...])
        o_ref[...] = d
        _accumulate(db_ref, jnp.sum(d, axis=0, keepdims=True), i == 0)

    return pl.pallas_call(
        body, name=name, grid=(l // tm,),
        in_specs=[pl.BlockSpec((tm, 128), lambda i: (i, off)), pl.BlockSpec((1, 128), lambda i: (0, 0)),
                  pl.BlockSpec((tm, 128), lambda i: (i, 0))],
        out_specs=[pl.BlockSpec((tm, 128), lambda i: (i, 0)), pl.BlockSpec((1, 128), lambda i: (0, 0))],
        out_shape=[jax.ShapeDtypeStruct((l, 128), F32), jax.ShapeDtypeStruct((1, 128), F32)],
        compiler_params=_cparams(1),
    )(p, bias, ddt)


def _row_to_col(r, eye):
    return jnp.sum(jnp.where(eye, r, 0.0), axis=1, keepdims=True)


def _col_to_row(c, eye):
    return jnp.sum(jnp.where(eye, c, 0.0), axis=0, keepdims=True)


def _ssd_chunk_common(b_ref, c_ref, dt_ref, a_ref, lam_scr):
    n = SSD_CHUNK
    row = lax.broadcasted_iota(jnp.int32, (n, n), 0)
    col = lax.broadcasted_iota(jnp.int32, (n, n), 1)
    bm, cm = b_ref[...].astype(BF16), c_ref[...].astype(BF16)
    g = _dot(cm, bm, NT)
    incl = (row <= col).astype(BF16)
    lam_scr[...] = _dot_exact(dt_ref[...] * a_ref[...], incl)
    return row, col, bm, cm, g


def _ssd_head_common(r, row, col, dt_ref, lam_scr):
    eye, tril = row == col, row >= col
    lam_r = lam_scr[r:r + 1, :]
    dt_r = dt_ref[r:r + 1, :]
    lam_c = _row_to_col(lam_r, eye)
    dt_c = _row_to_col(dt_r, eye)
    dk = jnp.where(tril, jnp.exp(jnp.minimum(lam_c - lam_r, 0.0)), 0.0)
    lam_last = jnp.sum(jnp.where(col[0:1, :] == SSD_CHUNK - 1, lam_r, 0.0), axis=1, keepdims=True)
    return eye, lam_r, dt_r, lam_c, dt_c, dk, lam_last


def _ssd_fwd(xh, act, dt_t, a_b, *, name, hosted=None):
    l = xh.shape[1]
    nc = l // SSD_CHUNK
    n, p_dim, hpg = SSD_CHUNK, SSD_HEAD_DIM, SSD_HPG

    gps = SSD_GROUPS_PER_STEP
    n_grp = SSD_GROUPS // gps
    grid = (n_grp, nc)
    nh = hosted.n if hosted else 0

    def body(x_ref, b_ref, c_ref, dt_ref, a_ref, *rest):
        y_ref, hp_ref = rest[nh:nh + 2]
        h_scr, lam_scr = rest[2 * nh + 2:2 * nh + 4]
        at_end = hosted.run(rest[:nh], rest[nh + 2:2 * nh + 2], rest[2 * nh + 4:], grid) if hosted else None

        @pl.when(pl.program_id(1) == 0)
        def _():
            h_scr[...] = jnp.zeros_like(h_scr)

        lanes = [pl.ds(gg * SSD_STATE, SSD_STATE) for gg in range(gps)]
        common = [_ssd_chunk_common(b_ref.at[:, lanes[gg]], c_ref.at[:, lanes[gg]], dt_ref.at[gg], a_ref.at[gg],
                                    lam_scr.at[gg]) for gg in range(gps)]
        for gg in range(gps):
            row, col, bm, cm, g = common[gg]
            for r in range(hpg):
                hd = gg * hpg + r
                _, _, dt_r, lam_c, dt_c, dk, lam_last = _ssd_head_common(r, row, col, dt_ref.at[gg], lam_scr.at[gg])
                xr = x_ref[hd]
                hr = h_scr[hd]
                w = (g * dk * dt_r).astype(BF16)
                y = _dot(w, xr.astype(BF16)) + _dot(cm, hr.astype(BF16), NT) * jnp.exp(lam_c)
                y_ref[hd] = y
                hp_ref[hd] = hr
                xw = (xr * (jnp.exp(lam_last - lam_c) * dt_c)).astype(BF16)
                h_scr[hd] = jnp.exp(lam_last) * hr + _dot(xw, bm, TN)
        if hosted:
            at_end()

    g_off = SSD_D_INNER // (gps * SSD_STATE)
    res = pl.pallas_call(
        body, name=name, grid=grid,
        in_specs=[pl.BlockSpec((gps * hpg, n, p_dim), lambda g, c: (g, c, 0)),
                  pl.BlockSpec((n, gps * SSD_STATE), lambda g, c: (c, g_off + g)),
                  pl.BlockSpec((n, gps * SSD_STATE), lambda g, c: (c, g_off + n_grp + g)),
                  pl.BlockSpec((gps, 8, n), lambda g, c: (g, 0, c)),
                  pl.BlockSpec((gps, 8, 128), lambda g, c: (g, 0, 0))] + (hosted.specs if hosted else []),
        out_specs=[pl.BlockSpec((gps * hpg, n, p_dim), lambda g, c: (g, c, 0)),
                   pl.BlockSpec((None, gps * hpg, p_dim, SSD_STATE), lambda g, c: (c, g, 0, 0))]
        + (hosted.specs if hosted else []),
        out_shape=[jax.ShapeDtypeStruct(xh.shape, F32),
                   jax.ShapeDtypeStruct((nc, SSD_HEADS, p_dim, SSD_STATE), F32)] + (hosted.out_shape if hosted else []),
        scratch_shapes=[pltpu.VMEM((gps * hpg, p_dim, SSD_STATE), F32), pltpu.VMEM((gps, 8, n), F32)]
        + (hosted.sems if hosted else []),
        compiler_params=_cparams(2),
    )(xh, act, act, dt_t, a_b, *(hosted.xs if hosted else []))
    return (res[0], res[1], res[2:]) if hosted else res


def _ssd_bwd(xh, act, dt_t, a_b, hprev, dyh, *, name):
    l = xh.shape[1]
    nc = l // SSD_CHUNK
    n, p_dim, hpg = SSD_CHUNK, SSD_HEAD_DIM, SSD_HPG
    gps = SSD_GROUPS_PER_STEP

    def body(x_ref, b_ref, c_ref, dt_ref, a_ref, hp_ref, dy_ref,
             dx_ref, db_ref, dc_ref, ddt_ref, da_ref, dh_scr, lam_scr, dlam_scr, ddt_scr):
        ci = pl.program_id(1)

        @pl.when(ci == 0)
        def _():
            dh_scr[...] = jnp.zeros_like(dh_scr)

        lanes = [pl.ds(gg * SSD_STATE, SSD_STATE) for gg in range(gps)]
        common = [_ssd_chunk_common(b_ref.at[:, lanes[gg]], c_ref.at[:, lanes[gg]], dt_ref.at[gg], a_ref.at[gg],
                                    lam_scr.at[gg]) for gg in range(gps)]
        dlam_scr[...] = jnp.zeros_like(dlam_scr)
        ddt_scr[...] = jnp.zeros_like(ddt_scr)
        for gg in range(gps):
            row, col, bm, cm, g = common[gg]
            dt_g, lam_g, dlam_g, ddt_g = dt_ref.at[gg], lam_scr.at[gg], dlam_scr.at[gg], ddt_scr.at[gg]
            dg_acc = jnp.zeros((n, n), F32)
            dc_acc = jnp.zeros((n, SSD_STATE), F32)
            db_acc = jnp.zeros((n, SSD_STATE), F32)
            for r in range(hpg):
                hd = gg * hpg + r
                eye, _, dt_r, lam_c, dt_c, dk, lam_last = _ssd_head_common(r, row, col, dt_g, lam_g)
                xr, dyr, hr, dhr = x_ref[hd], dy_ref[hd], hp_ref[hd], dh_scr[hd]
                xb, dyb, hb, dhb = xr.astype(BF16), dyr.astype(BF16), hr.astype(BF16), dhr.astype(BF16)
                e_l = jnp.exp(lam_c)
                e_last = jnp.exp(lam_last)
                decay_c = jnp.exp(lam_last - lam_c)
                w_c = decay_c * dt_c
                m = g * dk * dt_r
                dm = _dot(dyb, xb, NT)
                bdh = _dot(bm, dhb, NT)
                dx_ref[hd] = _dot(m.astype(BF16), dyb, TN) + w_c * bdh
                dg_acc = dg_acc + dm * dk * dt_r
                q_mat = dm * g * dk
                p_mat = q_mat * dt_r
                yoff = _dot(cm, hb, NT) * e_l
                q_c = jnp.sum(xr * bdh, axis=1, keepdims=True)
                dlam_c = (jnp.sum(p_mat, axis=1, keepdims=True) + jnp.sum(dyr * yoff, axis=1, keepdims=True)
                          - w_c * q_c)
                d_last = (jnp.sum(w_c * q_c, axis=0, keepdims=True)
                          + e_last * jnp.sum(jnp.sum(dhr * hr, axis=1, keepdims=True), axis=0, keepdims=True))
                dlam_g[r:r + 1, :] = (_col_to_row(dlam_c, eye) - jnp.sum(p_mat, axis=0, keepdims=True)
                                      + jnp.where(col[0:1, :] == n - 1, d_last, 0.0))
                ddt_g[r:r + 1, :] = jnp.sum(q_mat, axis=0, keepdims=True) + _col_to_row(decay_c * q_c, eye)
                dc_acc = dc_acc + e_l * _dot(dyb, hb)
                db_acc = db_acc + _dot((xr * w_c).astype(BF16), dhb)
                dh_scr[hd] = e_last * dhr + _dot((dyr * e_l).astype(BF16), cm, TN)

            dgb = dg_acc.astype(BF16)
            dc_ref[:, lanes[gg]] = _dot(dgb, bm) + dc_acc
            db_ref[:, lanes[gg]] = _dot(dgb, cm, TN) + db_acc
            rev = (row >= col).astype(BF16)
            da = _dot_exact(dlam_g[...], rev)
            ddt_ref[gg] = ddt_g[...] + da * a_ref[gg]
            _accumulate(da_ref.at[gg], da * dt_g[...], ci == 0)

        @pl.when(ci == nc - 1)
        def _():
            for gg in range(gps):
                da_ref[gg] = jnp.broadcast_to(jnp.sum(da_ref[gg], axis=1, keepdims=True), da_ref.shape[1:])

    g_off = SSD_D_INNER // (gps * SSD_STATE)
    n_grp = SSD_GROUPS // gps
    rc = lambda c: nc - 1 - c
    hspec = pl.BlockSpec((gps * hpg, n, p_dim), lambda g, c: (g, rc(c), 0))
    gspec = pl.BlockSpec((n, gps * SSD_STATE), lambda g, c: (rc(c), g))
    return pl.pallas_call(
        body, name=name, grid=(n_grp, nc),
        in_specs=[hspec,
                  pl.BlockSpec((n, gps * SSD_STATE), lambda g, c: (rc(c), g_off + g)),
                  pl.BlockSpec((n, gps * SSD_STATE), lambda g, c: (rc(c), g_off + n_grp + g)),
                  pl.BlockSpec((gps, 8, n), lambda g, c: (g, 0, rc(c))),
                  pl.BlockSpec((gps, 8, 128), lambda g, c: (g, 0, 0)),
                  pl.BlockSpec((None, gps * hpg, p_dim, SSD_STATE), lambda g, c: (rc(c), g, 0, 0)),
                  hspec],
        out_specs=[hspec, gspec, gspec,
                   pl.BlockSpec((gps, 8, n), lambda g, c: (g, 0, rc(c))),
                   pl.BlockSpec((gps, 8, 128), lambda g, c: (g, 0, 0))],
        out_shape=[jax.ShapeDtypeStruct(xh.shape, F32),
                   jax.ShapeDtypeStruct((l, SSD_GROUPS * SSD_STATE), F32),
                   jax.ShapeDtypeStruct((l, SSD_GROUPS * SSD_STATE), F32),
                   jax.ShapeDtypeStruct(dt_t.shape, F32),
                   jax.ShapeDtypeStruct(a_b.shape, F32)],
        scratch_shapes=[pltpu.VMEM((gps * hpg, p_dim, SSD_STATE), F32), pltpu.VMEM((gps, 8, n), F32),
                        pltpu.VMEM((gps, 8, n), F32), pltpu.VMEM((gps, 8, n), F32)],
        compiler_params=_cparams(2),
    )(xh, act, act, dt_t, a_b, hprev, dyh)


def _ssd_gate_fwd(y, act, p, d_vec, gn, *, name):
    l = y.shape[0]
    w = SSD_D_INNER
    tm = _tile(l, 256)

    def body(y_ref, xs_ref, z_ref, d_ref, g_ref, o_ref):
        for gi in range(SSD_GROUPS):
            sl = slice(gi * SSD_NORM_GROUP, (gi + 1) * SSD_NORM_GROUP)
            z = z_ref[:, sl]
            y2 = (y_ref[:, sl] + d_ref[:, sl] * xs_ref[:, sl]) * (z * _sigmoid(z))
            r = lax.rsqrt(jnp.mean(y2 * y2, axis=1, keepdims=True) + RMS_EPS)
            o_ref[:, sl] = (y2 * r * g_ref[:, sl]).astype(BF16)

    rows = pl.BlockSpec((tm, w), lambda i: (i, 0))
    vec = pl.BlockSpec((1, w), lambda i: (0, 0))
    return pl.pallas_call(
        body, name=name, grid=(l // tm,), in_specs=[rows, rows, rows, vec, vec], out_specs=rows,
        out_shape=jax.ShapeDtypeStruct((l, w), BF16), compiler_params=_cparams(1),
    )(y, act, p, d_vec, gn)


def _ssd_gate_bwd(dyn, y, act, p, d_vec, gn, *, name):
    l = y.shape[0]
    w = SSD_D_INNER
    tm = _tile(l, 256)

    def body(dyn_ref, y_ref, xs_ref, z_ref, d_ref, g_ref, dy_ref, dz_ref, dxs_ref, dd_ref, dg_ref):
        i = pl.program_id(0)
        for gi in range(SSD_GROUPS):
            sl = slice(gi * SSD_NORM_GROUP, (gi + 1) * SSD_NORM_GROUP)
            z, xs, dv = z_ref[:, sl], xs_ref[:, sl], d_ref[:, sl]
            s = _sigmoid(z)
            sz = z * s
            y1 = y_ref[:, sl] + dv * xs
            y2 = y1 * sz
            r = lax.rsqrt(jnp.mean(y2 * y2, axis=1, keepdims=True) + RMS_EPS)
            y2h = y2 * r
            dyn_v = dyn_ref[:, sl]
            d2h = dyn_v * g_ref[:, sl]
            dy2 = r * (d2h - y2h * jnp.mean(d2h * y2h, axis=1, keepdims=True))
            dy1 = dy2 * sz
            dy_ref[:, sl] = dy1
            dz_ref[:, sl] = dy2 * y1 * s * (1.0 + z * (1.0 - s))
            dxs_ref[:, sl] = dv * dy1
            _accumulate(dd_ref.at[:, sl], jnp.sum(dy1 * xs, axis=0, keepdims=True), i == 0)
            _accumulate(dg_ref.at[:, sl], jnp.sum(dyn_v * y2h, axis=0, keepdims=True), i == 0)

    rows = pl.BlockSpec((tm, w), lambda i: (i, 0))
    vec = pl.BlockSpec((1, w), lambda i: (0, 0))
    return pl.pallas_call(
        body, name=name, grid=(l // tm,), in_specs=[rows, rows, rows, rows, vec, vec],
        out_specs=[rows, rows, rows, vec, vec],
        out_shape=[jax.ShapeDtypeStruct((l, w), F32)] * 3 + [jax.ShapeDtypeStruct((1, w), F32)] * 2,
        compiler_params=_cparams(1),
    )(dyn, y, act, p, d_vec, gn)


def _heads_major(x):
    return x.reshape(x.shape[0], SSD_HEADS, SSD_HEAD_DIM).transpose(1, 0, 2)


def _ssd_layer_fwd(x, g, win, cw, cb, dt_bias, a_log, d_skip, gn, wout, tag, hosted=None):
    l = x.shape[0]
    h = _rmsnorm(x, g, name=f"{tag}_norm")
    p = _mm(h, win, tm=1024, tn=896, tk=1024, name=f"{tag}_in")
    act = _ssd_conv_fwd(p, cw, cb, name=f"{tag}_conv")
    bias = jnp.pad(dt_bias, (0, 128 - SSD_HEADS)).reshape(1, 128)
    dt = _ssd_dt_fwd(p, bias, name=f"{tag}_dt")
    xh = _heads_major(act[:, :SSD_D_INNER])
    dt_t = jnp.pad(dt[:, :SSD_HEADS].T.reshape(SSD_GROUPS, SSD_HPG, l), ((0, 0), (0, 8 - SSD_HPG), (0, 0)))
    a = -jnp.exp(a_log).reshape(SSD_GROUPS, SSD_HPG, 1)
    a_b = jnp.broadcast_to(jnp.pad(a, ((0, 0), (0, 8 - SSD_HPG), (0, 0))), (SSD_GROUPS, 8, 128))
    res = _ssd_fwd(xh, act, dt_t, a_b, name=f"{tag}_scan", hosted=hosted)
    yh, hprev = res[:2]
    carried = res[2] if hosted else None
    y = yh.transpose(1, 0, 2).reshape(l, SSD_D_INNER)
    d_vec = jnp.repeat(d_skip, SSD_HEAD_DIM).reshape(1, SSD_D_INNER)
    yn = _ssd_gate_fwd(y, act, p, d_vec, gn, name=f"{tag}_gate")
    xo = _mm(yn, wout, tm=1024, tn=1024, tk=2048, name=f"{tag}_out", extras=[(x, "tile")],
             epilogue=lambda acc, xt: (xt + acc,))
    return xo, (x, h, p, act, bias, xh, dt_t, a_b, hprev, y, d_vec, yn), carried


def _ssd_layer_bwd(dout, saved, g, win, cw, cb, gn, wout, tag):
    x, h, p, act, bias, xh, dt_t, a_b, hprev, y, d_vec, yn = saved
    l = x.shape[0]
    dyn = _mm(dout, wout, tb=True, tm=1024, tn=1024, tk=1024, name=f"{tag}_dyn")
    dwout = _mm(yn, dout, ta=True, tm=1024, tn=1024, tk=2048, name=f"{tag}_dwout", out_dtype=BF16)
    dy, dz, dxs_d, dd_vec, dgn = _ssd_gate_bwd(dyn, y, act, p, d_vec, gn, name=f"{tag}_gate_bwd")
    dxh, dbm, dcm, ddt_t, da_b = _ssd_bwd(xh, act, dt_t, a_b, hprev, _heads_major(dy), name=f"{tag}_scan_bwd")
    dxs_scan = dxh.transpose(1, 0, 2).reshape(l, SSD_D_INNER)
    dxbc, dcw8 = _ssd_conv_bwd(p, cw, cb, dxs_scan, dxs_d, dbm, dcm, name=f"{tag}_conv_bwd")
    ddt = jnp.pad(ddt_t[:, :SSD_HPG, :].reshape(SSD_HEADS, l).T, ((0, 0), (0, 128 - SSD_HEADS)))
    ddt_raw, dbias = _ssd_dt_bwd(p, bias, ddt, name=f"{tag}_dt_bwd")
    dp = jnp.concatenate([dz, dxbc, ddt_raw], axis=1)
    dwin = _mm(h, dp, ta=True, tm=1024, tn=896, tk=2048, name=f"{tag}_dwin", out_dtype=BF16)
    dx, dg = _mm(dp, win, tb=True, tm=256, tn=1024, tk=6272, name=f"{tag}_dx",
                 extras=[(x, "tile"), (g, "row"), (dout, "tile")], outs=[(F32, "tile"), (F32, "colsum")],
                 epilogue=_norm_bwd_epilogue)
    a_heads = a_b[:, :SSD_HPG, 0].reshape(SSD_HEADS)
    grads = dict(
        ssd_w_in=dwin[:, :SSD_IN_DIM], ssd_conv_w=dcw8[:4], ssd_conv_b=dcw8[4],
        ssd_dt_bias=dbias[0, :SSD_HEADS], ssd_a_log=da_b[:, :SSD_HPG, 0].reshape(SSD_HEADS) * a_heads,
        ssd_d=dd_vec.reshape(SSD_HEADS, SSD_HEAD_DIM).sum(axis=1), ssd_norm=dgn[0], ssd_w_out=dwout)
    return dx, dg, grads


def _local_step(x, tgt, w, gather_later=None, scatter_early=None):
    row = lambda v: v.reshape(1, -1)
    saved = []
    for i in range(DEPTH):
        kind, j = i % 3, i // 3
        x, s1 = _ffn_fwd(x, row(w["ffn1_norm"][i]), w["ffn1_w_gu"][i], w["ffn1_w_down"][i], f"l{i}f1")
        gm = row(w["mix_norm"][i])
        hook = (gather_later or {}).get(i)
        hosted = hook[0] if hook else None
        if kind == 0:
            x, sm, carried = _sb_layer_fwd(x, gm, w["sb_w_qkv"][j], w["sb_w_o"][j], f"l{i}sb", hosted=hosted)
        elif kind == 1:
            x, sm, carried = _ssd_layer_fwd(x, gm, w["ssd_w_in"][j], w["ssd_conv_w"][j], row(w["ssd_conv_b"][j]),
                                            w["ssd_dt_bias"][j], w["ssd_a_log"][j], w["ssd_d"][j],
                                            row(w["ssd_norm"][j]), w["ssd_w_out"][j], f"l{i}ssd", hosted=hosted)
        if hook:
            w = hook[1](w, carried)
        if kind == 2:
            x, sm = _sc_layer_fwd(x, gm, w["sc_w_in"][j], w["sc_conv_w"][j], w["sc_w_out"][j], f"l{i}sc")
        x, s2 = _ffn_fwd(x, row(w["ffn2_norm"][i]), w["ffn2_w_gu"][i], w["ffn2_w_down"][i], f"l{i}f2")
        saved.append((s1, sm, s2))

    loss, dx, dfinal = _final_loss(x, row(w["final_norm"]), tgt, name="final_loss")
    per_layer = {k: [None] * DEPTH for k in ("ffn1_norm", "ffn1_w_gu", "ffn1_w_down", "mix_norm",
                                             "ffn2_norm", "ffn2_w_gu", "ffn2_w_down")}
    per_layer.update({"sb_w_qkv": [None, None], "sb_w_o": [None, None]})
    grads = {"final_norm": dfinal[0]}
    early = None
    for i in reversed(range(DEPTH)):
        kind, j = i % 3, i // 3
        s1, sm, s2 = saved[i]
        dx, dg, dwgu, dwd = _ffn_bwd(dx, s2, row(w["ffn2_norm"][i]), w["ffn2_w_gu"][i], w["ffn2_w_down"][i], f"l{i}f2")
        per_layer["ffn2_norm"][i], per_layer["ffn2_w_gu"][i], per_layer["ffn2_w_down"][i] = dg[0], dwgu, dwd
        gm = row(w["mix_norm"][i])
        if kind == 0:
            hosted = scatter_early({**grads, **per_layer}) if (scatter_early and i == 0) else None
            dx, dg, dwqkv, dwo, carried = _sb_layer_bwd(dx, sm, gm, w["sb_w_qkv"][j], w["sb_w_o"][j], f"l{i}sb",
                                                        hosted=hosted)
            per_layer["sb_w_qkv"][j], per_layer["sb_w_o"][j] = dwqkv, dwo
            if hosted:
                early = carried
        elif kind == 1:
            dx, dg, sg = _ssd_layer_bwd(dx, sm, gm, w["ssd_w_in"][j], w["ssd_conv_w"][j], row(w["ssd_conv_b"][j]),
                                        row(w["ssd_norm"][j]), w["ssd_w_out"][j], f"l{i}ssd")
            sg["ssd_w_in"], sg["ssd_w_out"] = [sg["ssd_w_in"]], [sg["ssd_w_out"]]
            grads.update({k: (v if isinstance(v, list) else v[None]) for k, v in sg.items()})
        else:
            dx, dg, dwin, dcw, dwout = _sc_layer_bwd(dx, sm, gm, w["sc_w_in"][j], w["sc_conv_w"][j],
                                                     w["sc_w_out"][j], f"l{i}sc")
            grads.update(sc_w_in=[dwin], sc_conv_w=dcw[None], sc_w_out=[dwout])
        per_layer["mix_norm"][i] = dg[0]
        dx, dg, dwgu, dwd = _ffn_bwd(dx, s1, row(w["ffn1_norm"][i]), w["ffn1_w_gu"][i], w["ffn1_w_down"][i], f"l{i}f1")
        per_layer["ffn1_norm"][i], per_layer["ffn1_w_gu"][i], per_layer["ffn1_w_down"][i] = dg[0], dwgu, dwd
    for k, v in per_layer.items():
        grads[k] = jnp.stack(v) if k.endswith("_norm") else v
    return loss, dx, grads, early


_HBM = pl.BlockSpec(memory_space=pltpu.HBM)


def _remote(src, dst, send_sems, recv_sems, idx, dev):
    return pltpu.make_async_remote_copy(src_ref=src, dst_ref=dst, send_sem=send_sems.at[idx], recv_sem=recv_sems.at[idx],
                                        device_id=dev, device_id_type=pl.DeviceIdType.MESH)


def _exchange_call(body, xs, out_shapes, n_copies, name):
    n = len(xs)
    return pl.pallas_call(
        body, name=name, in_specs=[_HBM] * n, out_specs=[_HBM] * n,
        out_shape=[jax.ShapeDtypeStruct(s, x.dtype) for s, x in zip(out_shapes, xs)],
        scratch_shapes=[pltpu.SemaphoreType.DMA((n, n_copies)), pltpu.SemaphoreType.DMA((n, n_copies)),
                        pltpu.SemaphoreType.DMA((n,))],
    )(*xs)


def _gather(xs, *, name):
    n = len(xs)

    def body(*refs):
        start, finish = _gather_steps(refs[:n], refs[n:2 * n], *refs[2 * n:])
        start()
        finish()

    return _exchange_call(body, xs, _gather_shapes(xs), _GATHER_COPIES, name)


_GATHER_COPIES = 7


def _gather_shapes(xs):
    return [(N_DEV,) + x.shape for x in xs]


def _gather_steps(x_refs, o_refs, send_sems, recv_sems, local_sems):
    n = len(x_refs)

    def plan():
        mx, my, mc = lax.axis_index("x"), lax.axis_index("y"), lax.axis_index("c")
        slot = lambda px, py, pc: 4 * px + 2 * py + pc
        me, sibling = (mx, my, mc), (mx, my, 1 - mc)
        chips = [(1 - mx, my), (mx, 1 - my), (1 - mx, 1 - my)]
        locals_, first = [], []
        for a in range(n):
            x_ref, o_ref = x_refs[a], o_refs[a]
            locals_.append(pltpu.make_async_copy(x_ref, o_ref.at[slot(*me)], local_sems.at[a]))
            first.append(_remote(x_ref, o_ref.at[slot(*me)], send_sems, recv_sems, (a, 0), sibling))
            for j, chip in enumerate(chips):
                first.append(_remote(x_ref, o_ref.at[slot(*me)], send_sems, recv_sems, (a, 1 + j), (*chip, mc)))
        return locals_, first, slot, me, sibling, chips, mc

    def start():
        locals_, first = plan()[:2]
        for cp in locals_ + first:
            cp.start()

    def finish():
        locals_, first, slot, me, sibling, chips, mc = plan()
        passed = []
        for j, chip in enumerate(chips):
            for a in range(n):
                landed = o_refs[a].at[slot(*chip, mc)]
                _remote(landed, landed, send_sems, recv_sems, (a, 1 + j), me).wait_recv()
                fwd = _remote(landed, landed, send_sems, recv_sems, (a, 4 + j), sibling)
                fwd.start()
                passed.append(fwd)
        for a in range(n):
            from_sib = o_refs[a].at[slot(*sibling)]
            _remote(from_sib, from_sib, send_sems, recv_sems, (a, 0), me).wait_recv()
            for j, chip in enumerate(chips):
                via_sib = o_refs[a].at[slot(*chip, 1 - mc)]
                _remote(via_sib, via_sib, send_sems, recv_sems, (a, 4 + j), me).wait_recv()
        for cp in first + passed:
            cp.wait_send()
        for cp in locals_:
            cp.wait()

    return start, finish


def _scatter_sibling(xs, *, name):
    n = len(xs)

    def body(*refs):
        x_refs, o_refs = refs[:n], refs[n:2 * n]
        send_sems, recv_sems, _ = refs[2 * n:]
        mx, my, mc = lax.axis_index("x"), lax.axis_index("y"), lax.axis_index("c")
        sibling = (mx, my, 1 - mc)
        sends = []
        for a in range(n):
            for ch in range(4):
                sends.append(_remote(x_refs[a].at[ch, 1 - mc], o_refs[a].at[ch], send_sems, recv_sems, (a, ch), sibling))
        for cp in sends:
            cp.start()
        for cp in sends:
            cp.wait_recv()
        for cp in sends:
            cp.wait_send()

    return _exchange_call(body, xs, [(4,) + x.shape[2:] for x in xs], 4, name)


def _scatter_chips(ys, *, name):
    n = len(ys)

    def body(*refs):
        start, finish = _chip_scatter_steps(refs[:n], refs[n:2 * n], *refs[2 * n:])
        start()
        finish()

    return _exchange_call(body, ys, _chip_scatter_shapes(ys), _CHIP_SCATTER_COPIES, name)


_CHIP_SCATTER_COPIES = 3


def _chip_scatter_shapes(ys):
    return [y.shape for y in ys]


def _chip_scatter_steps(y_refs, o_refs, send_sems, recv_sems, local_sems):
    n = len(y_refs)

    def plan():
        mx, my, mc = lax.axis_index("x"), lax.axis_index("y"), lax.axis_index("c")
        mine = 2 * mx + my
        chips = [(1 - mx, my), (mx, 1 - my), (1 - mx, 1 - my)]
        locals_, sends, recvs = [], [], []
        for a in range(n):
            locals_.append(pltpu.make_async_copy(y_refs[a].at[mine], o_refs[a].at[mine], local_sems.at[a]))
            for j, (px, py) in enumerate(chips):
                theirs = 2 * px + py
                sends.append(_remote(y_refs[a].at[theirs], o_refs[a].at[mine], send_sems, recv_sems, (a, j), (px, py, mc)))
                recvs.append(_remote(y_refs[a].at[theirs], o_refs[a].at[theirs], send_sems, recv_sems, (a, j), (px, py, mc)))
        return locals_, sends, recvs

    def start():
        locals_, sends, _ = plan()
        for cp in locals_ + sends:
            cp.start()

    def finish():
        locals_, sends, recvs = plan()
        for cp in recvs:
            cp.wait_recv()
        for cp in sends:
            cp.wait_send()
        for cp in locals_:
            cp.wait()

    return start, finish


def _pair_add(x, r, *, name):
    _, _, rows, c = x.shape
    tr = _tile(rows, 512, 16)

    def body(core_ref, x_ref, r_ref, o_ref):
        o_ref[...] = (x_ref[...].astype(F32) + r_ref[...].astype(F32)).astype(o_ref.dtype)

    core = lax.axis_index("c").astype(jnp.int32).reshape(1)
    return pl.pallas_call(
        body, name=name,
        grid_spec=pltpu.PrefetchScalarGridSpec(
            num_scalar_prefetch=1, grid=(4, rows // tr),
            in_specs=[pl.BlockSpec((None, None, tr, c), lambda ch, i, core: (ch, core[0], i, 0)),
                      pl.BlockSpec((None, tr, c), lambda ch, i, core: (ch, i, 0))],
            out_specs=pl.BlockSpec((None, tr, c), lambda ch, i, core: (ch, i, 0))),
        out_shape=jax.ShapeDtypeStruct((4, rows, c), x.dtype), compiler_params=_cparams(2),
    )(core, x, r)


def _adamw_reduce(parts, w, m, v, *, name):
    r, c = w.shape
    n_parts = parts.shape[0]
    tr = _tile(r, 256, 16)
    bc1 = 1.0 - ADAM_B1 ** ADAM_STEP
    bc2 = 1.0 - ADAM_B2 ** ADAM_STEP

    def body(p_ref, w_ref, m_ref, v_ref, g_ref, d_ref, nm_ref, nv_ref):
        g = p_ref[0].astype(F32)
        for q in range(1, n_parts):
            g = g + p_ref[q].astype(F32)
        nm = ADAM_B1 * m_ref[...] + (1.0 - ADAM_B1) * g
        nv = ADAM_B2 * v_ref[...] + (1.0 - ADAM_B2) * (g * g)
        g_ref[...] = g
        nm_ref[...] = nm
        nv_ref[...] = nv
        d_ref[...] = -ADAM_LR * ((nm / bc1) / (jnp.sqrt(nv / bc2) + ADAM_EPS) + ADAM_WD * w_ref[...])

    blk = pl.BlockSpec((tr, c), lambda i: (i, 0))
    return pl.pallas_call(
        body, name=name, grid=(r // tr,),
        in_specs=[pl.BlockSpec((n_parts, tr, c), lambda i: (0, i, 0)), blk, blk, blk], out_specs=[blk] * 4,
        out_shape=[jax.ShapeDtypeStruct((r, c), F32)] * 4, compiler_params=_cparams(1),
    )(parts, w, m, v)


def _col_full(g):
    return g.transpose(1, 2, 0, 3).reshape(g.shape[1], g.shape[2], -1)


def _col_parts(f):
    n, k, c8 = f.shape
    return f.reshape(n, k, N_DEV, c8 // N_DEV).transpose(2, 0, 1, 3)


def _row_full(g):
    return g.transpose(1, 0, 2, 3).reshape(g.shape[1], -1, g.shape[3])


def _row_parts(f):
    n, r8, c = f.shape
    return f.reshape(n, N_DEV, r8 // N_DEV, c).transpose(1, 0, 2, 3)


def _gu_full(g):
    n, d, c = g.shape[1:]
    return g.reshape(2, 4, n, d, c).transpose(2, 0, 3, 1, 4).reshape(n, 2, d, 4 * c)


def _gu_parts(f):
    n, _, d, c4 = f.shape
    return f.reshape(n, 2, d, 4, c4 // 4).transpose(1, 3, 0, 2, 4).reshape(N_DEV, n, d, c4 // 4)


def _ssd_in_full(g):
    return jnp.pad(_col_full(g), ((0, 0), (0, 0), (0, SSD_IN_PAD - SSD_IN_DIM)))


_MATMUL_WEIGHTS = (
    ("ffn1_w_gu", _gu_full, _gu_parts), ("ffn1_w_down", _row_full, _row_parts),
    ("ffn2_w_gu", _gu_full, _gu_parts), ("ffn2_w_down", _row_full, _row_parts),
    ("sb_w_qkv", _col_full, _col_parts), ("sb_w_o", _row_full, _row_parts),
    ("ssd_w_in", _ssd_in_full, _col_parts), ("ssd_w_out", _row_full, _row_parts),
    ("sc_w_in", _col_full, _col_parts), ("sc_w_out", _row_full, _row_parts),
)
_FIRST_WEIGHTS = ("ffn1_w_gu", "ffn1_w_down", "sb_w_qkv", "sb_w_o")
_CONV_WEIGHTS = ("ssd_conv_w", "sc_conv_w")
_REPLICATED = ("ffn1_norm", "mix_norm", "ffn2_norm", "final_norm", "ssd_conv_b", "ssd_norm",
               "ssd_dt_bias", "ssd_a_log", "ssd_d")
_ORDER = ("ffn1_norm", "ffn1_w_gu", "ffn1_w_down", "mix_norm", "ffn2_norm", "ffn2_w_gu", "ffn2_w_down",
          "sb_w_qkv", "sb_w_o", "ssd_w_in", "ssd_conv_w", "ssd_conv_b", "ssd_dt_bias", "ssd_a_log", "ssd_d",
          "ssd_norm", "ssd_w_out", "sc_w_in", "sc_conv_w", "sc_w_out", "final_norm")
_LANES = 1024


def _rows_of(a):
    flat = a.reshape(-1)
    pad = -flat.shape[0] % _LANES
    return jnp.pad(flat, (0, pad)).reshape(-1, _LANES)


def _pack_rows(arrays, mult):
    rows = [_rows_of(a) for a in arrays]
    packed = jnp.concatenate(rows, axis=0)
    pad = -packed.shape[0] % mult
    return jnp.pad(packed, ((0, pad), (0, 0))), [r.shape[0] for r in rows]


def _unpack_rows(packed, counts, shapes, lead=()):
    out, off = [], 0
    for n, shp in zip(counts, shapes):
        size = math.prod(shp)
        seg = packed[..., off:off + n, :].reshape(lead + (n * _LANES,))[..., :size]
        out.append(seg.reshape(lead + tuple(shp)))
        off += n
    return out


def kernel(x, ffn1_norm, ffn1_w_gu, ffn1_w_down, mix_norm, ffn2_norm, ffn2_w_gu, ffn2_w_down, sb_w_qkv, sb_w_o, ssd_w_in, ssd_conv_w, ssd_conv_b, ssd_dt_bias, ssd_a_log, ssd_d, ssd_norm, ssd_w_out, sc_w_in, sc_conv_w, sc_w_out, final_norm, loss_target, m_ffn1_norm, m_ffn1_w_gu, m_ffn1_w_down, m_mix_norm, m_ffn2_norm, m_ffn2_w_gu, m_ffn2_w_down, m_sb_w_qkv, m_sb_w_o, m_ssd_w_in, m_ssd_conv_w, m_ssd_conv_b, m_ssd_dt_bias, m_ssd_a_log, m_ssd_d, m_ssd_norm, m_ssd_w_out, m_sc_w_in, m_sc_conv_w, m_sc_w_out, m_final_norm, v_ffn1_norm, v_ffn1_w_gu, v_ffn1_w_down, v_mix_norm, v_ffn2_norm, v_ffn2_w_gu, v_ffn2_w_down, v_sb_w_qkv, v_sb_w_o, v_ssd_w_in, v_ssd_conv_w, v_ssd_conv_b, v_ssd_dt_bias, v_ssd_a_log, v_ssd_d, v_ssd_norm, v_ssd_w_out, v_sc_w_in, v_sc_conv_w, v_sc_w_out, v_final_norm):
    w = dict(ffn1_norm=ffn1_norm, ffn1_w_gu=ffn1_w_gu, ffn1_w_down=ffn1_w_down, mix_norm=mix_norm, ffn2_norm=ffn2_norm, ffn2_w_gu=ffn2_w_gu, ffn2_w_down=ffn2_w_down, sb_w_qkv=sb_w_qkv, sb_w_o=sb_w_o, ssd_w_in=ssd_w_in, ssd_conv_w=ssd_conv_w, ssd_conv_b=ssd_conv_b, ssd_dt_bias=ssd_dt_bias, ssd_a_log=ssd_a_log, ssd_d=ssd_d, ssd_norm=ssd_norm, ssd_w_out=ssd_w_out, sc_w_in=sc_w_in, sc_conv_w=sc_conv_w, sc_w_out=sc_w_out, final_norm=final_norm)
    mom = dict(ffn1_norm=m_ffn1_norm, ffn1_w_gu=m_ffn1_w_gu, ffn1_w_down=m_ffn1_w_down, mix_norm=m_mix_norm, ffn2_norm=m_ffn2_norm, ffn2_w_gu=m_ffn2_w_gu, ffn2_w_down=m_ffn2_w_down, sb_w_qkv=m_sb_w_qkv, sb_w_o=m_sb_w_o, ssd_w_in=m_ssd_w_in, ssd_conv_w=m_ssd_conv_w, ssd_conv_b=m_ssd_conv_b, ssd_dt_bias=m_ssd_dt_bias, ssd_a_log=m_ssd_a_log, ssd_d=m_ssd_d, ssd_norm=m_ssd_norm, ssd_w_out=m_ssd_w_out, sc_w_in=m_sc_w_in, sc_conv_w=m_sc_conv_w, sc_w_out=m_sc_w_out, final_norm=m_final_norm)
    var = dict(ffn1_norm=v_ffn1_norm, ffn1_w_gu=v_ffn1_w_gu, ffn1_w_down=v_ffn1_w_down, mix_norm=v_mix_norm, ffn2_norm=v_ffn2_norm, ffn2_w_gu=v_ffn2_w_gu, ffn2_w_down=v_ffn2_w_down, sb_w_qkv=v_sb_w_qkv, sb_w_o=v_sb_w_o, ssd_w_in=v_ssd_w_in, ssd_conv_w=v_ssd_conv_w, ssd_conv_b=v_ssd_conv_b, ssd_dt_bias=v_ssd_dt_bias, ssd_a_log=v_ssd_a_log, ssd_d=v_ssd_d, ssd_norm=v_ssd_norm, ssd_w_out=v_ssd_w_out, sc_w_in=v_sc_w_in, sc_conv_w=v_sc_conv_w, sc_w_out=v_sc_w_out, final_norm=v_final_norm)
    me = 4 * lax.axis_index("x") + 2 * lax.axis_index("y") + lax.axis_index("c")
    big = [n for n, _, _ in _MATMUL_WEIGHTS]
    two_d = lambda a: a.reshape(-1, a.shape[-1])

    to_full = {n: f for n, f, _ in _MATMUL_WEIGHTS}
    to_parts = {n: f for n, _, f in _MATMUL_WEIGHTS}
    first = [(n, 0) for n in _FIRST_WEIGHTS]
    later = [(n, i) for n in big for i in range(1 if n in _FIRST_WEIGHTS else 0, w[n].shape[0])]

    def shards(group):
        return [two_d(w[n][i].astype(BF16)) for n, i in group]

    def layers(group, gathered):
        out = {}
        for (n, i), g in zip(group, gathered):
            out.setdefault(n, []).append(to_full[n](g.reshape((N_DEV, 1) + w[n].shape[1:]))[0])
        return out

    gathered = _gather(shards(first) + [two_d(w[n]) for n in _CONV_WEIGHTS], name="gather_first")
    full = dict(w)
    full.update(layers(first, gathered))
    for n, g in zip(_CONV_WEIGHTS, gathered[len(first):]):
        full[n] = _col_full(g.reshape((N_DEV,) + w[n].shape))

    def host_of(n, i):
        if n.startswith("ffn2") and i == DEPTH - 1:
            return DEPTH - 1
        if (n.startswith("ffn1") and i == DEPTH - 1) or (n.startswith("sb_") and i == 1):
            return 1
        return 0

    gather_later = {}
    for host in (0, 1, DEPTH - 1):
        group = [(n, i) for n, i in later if host_of(n, i) == host]

        def merge(wd, gathered_group, group=group):
            wd = dict(wd)
            for n, ls in layers(group, gathered_group).items():
                have = wd[n] if isinstance(wd[n], list) else []
                wd[n] = have + ls
            return wd

        xs = shards(group)
        gather_later[host] = (_Hosted(_gather_steps, xs, _gather_shapes(xs), _GATHER_COPIES), merge)

    def chip_sums(group, grads, tag):
        parts = []
        for n, i in group:
            p8 = to_parts[n](grads[n][i][None].astype(BF16))
            parts.append(p8.reshape(4, 2, -1, p8.shape[-1]))
        from_sibling = _scatter_sibling(parts, name=f"scatter_sibling_{tag}")
        return [_pair_add(p, r, name=f"pair_add_{tag}_{n}{i}") for (n, i), p, r in zip(group, parts, from_sibling)]

    def scatter_early(grads):
        ys = chip_sums(later, grads, "later")
        return _Hosted(_chip_scatter_steps, ys, _chip_scatter_shapes(ys), _CHIP_SCATTER_COPIES)

    loss_part, dx, grads, recv_later = _local_step(x[0], loss_target[0], full, gather_later, scatter_early)
    loss = lax.psum(loss_part[0, 0], ("x", "y", "c"))

    recv_first = _scatter_chips(chip_sums(first, grads, "first"), name="scatter_chips_first")
    contrib = {n: [r] for (n, _), r in zip(first, recv_first)}
    for (n, _), r in zip(later, recv_later):
        contrib.setdefault(n, []).append(r)
    out_g, out_d, out_m, out_v = {}, {}, {}, {}

    def update(n, parts):
        res = _adamw_reduce(parts, two_d(w[n]), two_d(mom[n]), two_d(var[n]), name=f"adamw_{n}")
        out_g[n], out_d[n], out_m[n], out_v[n] = (r.reshape(w[n].shape) for r in res)

    for n in big:
        update(n, contrib[n][0] if len(contrib[n]) == 1 else jnp.concatenate(contrib[n], axis=1))

    small = list(_REPLICATED) + list(_CONV_WEIGHTS)
    small_shapes = [grads[n].shape for n in small]
    spacked, scounts = _pack_rows([grads[n].astype(F32) for n in small], 8)
    sg = _unpack_rows(_gather([spacked], name="gather_small_grads")[0], scounts, small_shapes, (N_DEV,))
    sg = dict(zip(small, sg))
    rep_w, rcounts = _pack_rows([w[n] for n in _REPLICATED], 8)
    rep_m, _ = _pack_rows([mom[n] for n in _REPLICATED], 8)
    rep_v, _ = _pack_rows([var[n] for n in _REPLICATED], 8)
    rep_p = jnp.concatenate([_rows_of(sg[n].reshape(N_DEV, -1)[q]) for q in range(N_DEV) for n in _REPLICATED], axis=0)
    rep_p = rep_p.reshape(N_DEV, -1, _LANES)
    rep_p = jnp.pad(rep_p, ((0, 0), (0, rep_w.shape[0] - rep_p.shape[1]), (0, 0)))
    res = _adamw_reduce(rep_p, rep_w, rep_m, rep_v, name="adamw_replicated")
    rep_shapes = [w[n].shape for n in _REPLICATED]
    for tgt, r in zip((out_g, out_d, out_m, out_v), res):
        for n, a in zip(_REPLICATED, _unpack_rows(r, rcounts, rep_shapes)):
            tgt[n] = a
    for n in _CONV_WEIGHTS:
        c = w[n].shape[-1]
        mine = lax.dynamic_slice_in_dim(sg[n], me * c, c, axis=sg[n].ndim - 1)
        update(n, mine.reshape(N_DEV, -1, c))

    return (loss, dx[None], *[out_g[n] for n in _ORDER], *[out_d[n] for n in _ORDER],
            *[out_m[n] for n in _ORDER], *[out_v[n] for n in _ORDER])
```

```python
import functools
import math

import jax
import jax.numpy as jnp
from jax import lax
from jax.experimental import pallas as pl
from jax.experimental.pallas import tpu as pltpu

F32 = jnp.float32
BF16 = jnp.bfloat16

D_MODEL = 1024
D_FF = 2816
DEPTH = 4
N_DEV = 8
SB_HEADS = 16
SB_HEAD_DIM = 64
SB_TILE = 256
SB_HEADS_PER_STEP = 2
SB_FWD_GROUPS = (2, 1)
SB_BWD_GROUPS = (4, 2, 1)
SSD_HEADS = 32
SSD_HEAD_DIM = 64
SSD_GROUPS = 8
SSD_HPG = 4
SSD_STATE = 128
SSD_CHUNK = 128
SSD_GROUPS_PER_STEP = 8
SSD_D_INNER = 2048
SSD_CONV_DIM = 4096
SSD_IN_DIM = 6176
SSD_IN_PAD = 6272
SSD_NORM_GROUP = 256
RMS_EPS = 1e-6
ADAM_LR = 0.001
ADAM_B1 = 0.9
ADAM_B2 = 0.999
ADAM_EPS = 1e-08
ADAM_WD = 0.01
ADAM_STEP = 10
VMEM_LIMIT = 60 * 1024 * 1024

NT = (((1,), (1,)), ((), ()))
TN = (((0,), (0,)), ((), ()))
NN = (((1,), (0,)), ((), ()))


def _cparams(n_axes):
    return pltpu.CompilerParams(dimension_semantics=("arbitrary",) * n_axes, vmem_limit_bytes=VMEM_LIMIT)


def _tile(n, want, mult=8):
    if n <= want:
        return n
    for t in range(want, 0, -1):
        if n % t == 0 and t % mult == 0:
            return t
    return n


def _sigmoid(x):
    return 1.0 / (1.0 + jnp.exp(-x))


def _dot(a, b, dn=NN):
    return lax.dot_general(a, b, dn, preferred_element_type=F32)


def _split3(x):
    x1 = x.astype(BF16)
    r1 = x - x1.astype(F32)
    x2 = r1.astype(BF16)
    x3 = (r1 - x2.astype(F32)).astype(BF16)
    return x1, x2, x3


def _dot_exact(x, t):
    x1, x2, x3 = _split3(x)
    return _dot(x1, t) + _dot(x2, t) + _dot(x3, t)


ROW_SUM_LANES = 1


def _cumsum_operand(tri):
    return jnp.concatenate([tri, tri], axis=0)


def _cumsum_rowsum(x, tri2):
    x1 = x.astype(BF16)
    x2 = (x - x1.astype(F32)).astype(BF16)
    return _dot(jnp.concatenate([x1, x2], axis=1), tri2), jnp.sum(x, axis=1, keepdims=True)


def _across_lanes(c, t):
    return c


def _mm(a, b, *, name, ta=False, tb=False, sa=False, sb=False, so=False, tm=512, tn=1024, tk=1024,
        out_dtype=F32, epilogue=None, extras=(), outs=None, pair=None, col_chunk=None):
    s_n = pair or (a.shape[0] if sa else (b.shape[0] if sb else 1))
    ash, bsh = a.shape[-2:], b.shape[-2:]
    m, k = (ash[1], ash[0]) if ta else ash
    n = bsh[0] if tb else bsh[1]
    tm, tn, tk = _tile(m, tm), _tile(n, tn, 128), _tile(k, tk, 128)
    nk = k // tk
    if outs is None:
        outs = [(out_dtype, "stile" if so else "tile")]
    if epilogue is None:
        epilogue = lambda acc: (acc,)

    if ta:
        a_blk, a_idx = (tk, tm), (lambda j, i, kk: (kk, i))
    else:
        a_blk, a_idx = (tm, tk), (lambda j, i, kk: (i, kk))
    if tb:
        b_blk, b_idx = (tn, tk), (lambda j, i, kk: (j, kk))
    else:
        b_blk, b_idx = (tk, tn), (lambda j, i, kk: (kk, j))

    def lead(blk, idx, has_s):
        if not has_s:
            return pl.BlockSpec(blk, idx)
        return pl.BlockSpec((s_n,) + blk, lambda j, i, kk: (0,) + idx(j, i, kk))

    kinds = {
        "tile": lambda: pl.BlockSpec((tm, tn), lambda j, i, kk: (i, j)),
        "stile": lambda: pl.BlockSpec((s_n, tm, tn), lambda j, i, kk: (0, i, j)),
        "row": lambda: pl.BlockSpec((1, tn), lambda j, i, kk: (0, j)),
        "colsum": lambda: pl.BlockSpec((1, tn), lambda j, i, kk: (0, j)),
    }
    shapes = {"tile": (m, n), "stile": (s_n, m, n), "colsum": (1, n)}
    in_specs = [lead(a_blk, a_idx, sa), lead(b_blk, b_idx, sb)] + [kinds[kd]() for _, kd in extras]
    out_specs = [kinds[kd]() for _, kd in outs]
    out_shape = [jax.ShapeDtypeStruct(shapes[kd], dt) for dt, kd in outs]
    n_ex, n_out = len(extras), len(outs)
    dn = ((((0,) if ta else (1,)), ((1,) if tb else (0,))), ((), ()))
    acc_shape = (s_n, tm, tn) if so else (tm, tn)

    def body(*refs):
        a_ref, b_ref = refs[0], refs[1]
        ex_refs = refs[2:2 + n_ex]
        o_refs = refs[2 + n_ex:2 + n_ex + n_out]
        i = pl.program_id(1)
        kk = pl.program_id(2)

        def products():
            for s in range(s_n if (sa or sb) else 1):
                av = (a_ref[s] if sa else a_ref[...]).astype(BF16)
                bv = (b_ref[s] if sb else b_ref[...]).astype(BF16)
                yield s, lax.dot_general(av, bv, dn, preferred_element_type=F32)

        def finish(accv):
            vals = epilogue(accv, *[r[...] for r in ex_refs])
            for (dt, kd), o_ref, val in zip(outs, o_refs, vals):
                if kd == "colsum":
                    _accumulate(o_ref, val, i == 0)
                elif kd == "stile":
                    for s in range(s_n):
                        o_ref[s] = val[s].astype(dt)
                else:
                    o_ref[...] = val.astype(dt)

        if nk == 1 and col_chunk:
            bounds = [(c0, min(col_chunk, tn - c0)) for c0 in range(0, tn, col_chunk)]
            n_s = s_n if (sa or sb) else 1
            a_vals = [(a_ref[s] if sa else a_ref[...]).astype(BF16) for s in range(n_s if sa else 1)]
            accs = []
            for c0, cw in bounds:
                ds = []
                for s in range(n_s):
                    idx = ((s,) if sb else ()) + ((pl.ds(c0, cw), slice(None)) if tb else (slice(None), pl.ds(c0, cw)))
                    ds.append(lax.dot_general(a_vals[s if sa else 0], b_ref[idx].astype(BF16), dn,
                                              preferred_element_type=F32))
                accs.append(tuple(ds) if so else functools.reduce(jnp.add, ds))
            for (c0, cw), accv in zip(bounds, accs):
                cols = pl.ds(c0, cw)
                exv = [r[:, :, cols] if kd == "stile" else r[:, cols] for r, (_, kd) in zip(ex_refs, extras)]
                vals = epilogue(accv, *exv)
                for (dt, kd), o_ref, val in zip(outs, o_refs, vals):
                    if kd == "stile":
                        for s in range(s_n):
                            o_ref[s, :, cols] = val[s].astype(dt)
                    else:
                        o_ref[:, cols] = val.astype(dt)
            return

        if nk == 1:
            ds = [d for _, d in products()]
            finish(tuple(ds) if so else functools.reduce(jnp.add, ds))
            return

        acc = refs[-1]

        @pl.when(kk == 0)
        def _():
            acc[...] = jnp.zeros_like(acc)

        for s, d in products():
            if so:
                acc[s] += d
            else:
                acc[...] += d

        @pl.when(kk == nk - 1)
        def _():
            finish(tuple(acc[s] for s in range(s_n)) if so else acc[...])

    res = pl.pallas_call(
        body, name=name, grid=(n // tn, m // tm, nk),
        in_specs=in_specs, out_specs=out_specs, out_shape=out_shape,
        scratch_shapes=[pltpu.VMEM(acc_shape, F32)] if nk > 1 else [], compiler_params=_cparams(3),
    )(a, b, *[e for e, _ in extras])
    return res[0] if len(res) == 1 else res


def _accumulate(o_ref, val, first):
    @pl.when(first)
    def _():
        o_ref[...] = val

    @pl.when(jnp.logical_not(first))
    def _():
        o_ref[...] += val


def _rmsnorm(x, g, *, name):
    l, d = x.shape
    tm = _tile(l, 512)

    def body(x_ref, g_ref, o_ref):
        xv = x_ref[...]
        r = lax.rsqrt(jnp.mean(xv * xv, axis=1, keepdims=True) + RMS_EPS)
        o_ref[...] = (xv * r * g_ref[...]).astype(BF16)

    return pl.pallas_call(
        body, name=name, grid=(l // tm,),
        in_specs=[pl.BlockSpec((tm, d), lambda i: (i, 0)), pl.BlockSpec((1, d), lambda i: (0, 0))],
        out_specs=pl.BlockSpec((tm, d), lambda i: (i, 0)),
        out_shape=jax.ShapeDtypeStruct((l, d), BF16), compiler_params=_cparams(1),
    )(x, g)


def _norm_bwd_epilogue(dh, x, g, dres):
    r = lax.rsqrt(jnp.mean(x * x, axis=1, keepdims=True) + RMS_EPS)
    xh = x * r
    dg = jnp.sum(dh * xh, axis=0, keepdims=True)
    dxh = dh * g
    dx = r * (dxh - xh * jnp.mean(dxh * xh, axis=1, keepdims=True))
    return dres + dx, dg


def _final_loss(x, g, tgt, *, name):
    l, d = x.shape
    tm = _tile(l, 512)

    def body(x_ref, g_ref, t_ref, loss_ref, dx_ref, dg_ref):
        i = pl.program_id(0)
        xv, gv = x_ref[...], g_ref[...]
        r = lax.rsqrt(jnp.mean(xv * xv, axis=1, keepdims=True) + RMS_EPS)
        xh = xv * r
        e = xh * gv - t_ref[...]
        part = 0.5 * jnp.sum(jnp.mean(e * e, axis=1, keepdims=True), axis=0, keepdims=True)
        dy = e * (1.0 / d)
        dg = jnp.sum(dy * xh, axis=0, keepdims=True)
        dxh = dy * gv
        dx_ref[...] = r * (dxh - xh * jnp.mean(dxh * xh, axis=1, keepdims=True))
        _accumulate(dg_ref, dg, i == 0)
        _accumulate(loss_ref, jnp.broadcast_to(part, (1, 128)), i == 0)

    return pl.pallas_call(
        body, name=name, grid=(l // tm,),
        in_specs=[pl.BlockSpec((tm, d), lambda i: (i, 0)), pl.BlockSpec((1, d), lambda i: (0, 0)),
                  pl.BlockSpec((tm, d), lambda i: (i, 0))],
        out_specs=[pl.BlockSpec((1, 128), lambda i: (0, 0)), pl.BlockSpec((tm, d), lambda i: (i, 0)),
                   pl.BlockSpec((1, d), lambda i: (0, 0))],
        out_shape=[jax.ShapeDtypeStruct((1, 128), F32), jax.ShapeDtypeStruct((l, d), F32),
                   jax.ShapeDtypeStruct((1, d), F32)],
        compiler_params=_cparams(1),
    )(x, g, tgt)


def _ffn_fwd(x, g, wgu, wd, tag):
    h = _rmsnorm(x, g, name=f"{tag}_norm")

    def act(acc):
        gate, up = acc
        return acc, gate * _sigmoid(gate) * up

    gu, a = _mm(h, wgu, sb=True, so=True, tm=512, tn=1408, tk=1024, name=f"{tag}_up",
                outs=[(BF16, "stile"), (BF16, "tile")], epilogue=act)
    xo = _mm(a, wd, tm=512, tn=1024, tk=2816, name=f"{tag}_down", extras=[(x, "tile")],
             epilogue=lambda acc, xt: (xt + 0.5 * acc,))
    return xo, (x, h, gu, a)


def _ffn_bwd(dout, saved, g, wgu, wd, tag):
    x, h, gu, a = saved

    def act_bwd(acc, guv):
        da = 0.5 * acc
        gate, up = guv[0].astype(F32), guv[1].astype(F32)
        s = _sigmoid(gate)
        return ((da * up * s * (1.0 + gate * (1.0 - s)), da * gate * s),)

    dgu = _mm(dout, wd, tb=True, pair=2, tm=512, tn=1408, tk=1024, col_chunk=384, name=f"{tag}_dact", extras=[(gu, "stile")],
              outs=[(BF16, "stile")], epilogue=act_bwd)
    dwd = _mm(a, dout, ta=True, tm=1408, tn=1024, tk=2048, name=f"{tag}_dwd", out_dtype=BF16,
              epilogue=lambda acc: (0.5 * acc,))
    dwgu = _mm(h, dgu, ta=True, sb=True, so=True, tm=512, tn=1408, tk=2048, name=f"{tag}_dwgu", out_dtype=BF16)
    dx, dg = _mm(dgu, wgu, tb=True, sa=True, sb=True, tm=512, tn=1024, tk=2816, name=f"{tag}_dx",
                 extras=[(x, "tile"), (g, "row"), (dout, "tile")], outs=[(F32, "tile"), (F32, "colsum")],
                 epilogue=_norm_bwd_epilogue)
    return dx, dg, dwgu, dwd


def _sb_plan(n, sizes):
    digits = [n // sizes[0]] + [(n // s) % 2 for s in sizes[1:]]
    plan, none_smaller = [], 1
    for size, d in reversed(list(zip(sizes, digits))):
        has = jnp.minimum(d, 1)
        with_diag = none_smaller * has
        plan.append((size, True, with_diag))
        if size > 1:
            plan.append((size, False, d - with_diag))
        none_smaller = none_smaller * (1 - has)
    return plan


def _sb_sweep(plan, start, step, fn, carry):
    pos = start
    for size, with_diag, trips in plan:
        diag = 0 if step < 0 else size - 1

        def trip(it, cr, size=size, with_diag=with_diag, pos=pos, diag=diag):
            base = pos + step * size * it
            return fn([base + step * b for b in range(size)], cr, [with_diag and b == diag for b in range(size)])

        carry = lax.fori_loop(0, trips, trip, carry)
        pos = pos + step * size * trips
    return carry


def _sb_logs(z):
    lb = jnp.minimum(z, 0.0) - jnp.log(1.0 + jnp.exp(-jnp.abs(z)))
    return lb, lb - z


class _Hosted:
    def __init__(self, steps, xs, out_shapes, copies):
        self.steps, self.xs, self.n, self.copies = steps, list(xs), len(xs), copies
        self.out_shape = [jax.ShapeDtypeStruct(s, x.dtype) for s, x in zip(out_shapes, xs)]
        self.specs = [_HBM] * self.n
        self.sems = [pltpu.SemaphoreType.DMA((self.n, copies)), pltpu.SemaphoreType.DMA((self.n, copies)),
                     pltpu.SemaphoreType.DMA((self.n,))]

    def run(self, x_refs, o_refs, sems, grid):
        ids = [pl.program_id(a) for a in range(len(grid))]
        first = functools.reduce(jnp.logical_and, [p == 0 for p in ids])
        last = functools.reduce(jnp.logical_and, [p == g - 1 for p, g in zip(ids, grid)])
        start, finish = self.steps(x_refs, o_refs, *sems)
        pl.when(first)(start)
        return lambda: pl.when(last)(finish)


def _head_masks(hs):
    lane = lax.broadcasted_iota(jnp.int32, (1, hs * SB_HEAD_DIM), 1)
    return [jnp.logical_and(lane >= hh * SB_HEAD_DIM, lane < (hh + 1) * SB_HEAD_DIM) for hh in range(hs)]


def _sb_fwd(qkv, *, name, hosted=None):
    l = qkv.shape[0]
    d_model = qkv.shape[1] // 3
    dh = SB_HEAD_DIM
    t = _tile(l, SB_TILE)
    hs = 2 * SB_HEADS_PER_STEP
    w = hs * dh
    n_grp = d_model // w
    scale = dh ** -0.5
    grid = (n_grp, l // t)
    nh = hosted.n if hosted else 0

    def body(q_ref, k_ref, v_ref, *rest):
        o_ref = rest[nh]
        at_end = hosted.run(rest[:nh], rest[nh + 1:2 * nh + 1], rest[2 * nh + 1:], grid) if hosted else None
        i = pl.program_id(1)
        heads = _head_masks(hs)
        q_all = (q_ref[...].astype(F32) * scale).astype(BF16)
        qs = [jnp.where(heads[hh], q_all, jnp.zeros_like(q_all)) for hh in range(hs)]
        row = lax.broadcasted_iota(jnp.int32, (t, t), 0)
        col = lax.broadcasted_iota(jnp.int32, (t, t), 1)
        strict = col < row
        tri = _cumsum_operand(strict.astype(BF16))

        def block(jbs, carry, masks):
            sls = [pl.ds(pl.multiple_of(jb * t, t), t) for jb in jbs]
            chains = [(hh, b) for b in range(len(jbs)) for hh in range(hs)]
            ks = [k_ref[sl, :] for sl in sls]
            zs = {(hh, b): _dot(qs[hh], ks[b], NT) for hh, b in chains}
            lbs, tails, sums = {}, {}, {}
            for hh, b in chains:
                lb, lk = _sb_logs(zs[hh, b])
                if masks[b]:
                    lk = jnp.where(strict, lk, 0.0)
                lbs[hh, b] = lb
                tails[hh, b], sums[hh, b] = _cumsum_rowsum(lk, tri)
            cs, o = list(carry[0]), carry[1]
            for b in range(len(jbs)):
                atts = []
                for hh in range(hs):
                    att = jnp.exp(lbs[hh, b] + tails[hh, b] + _across_lanes(cs[hh], t))
                    if masks[b]:
                        att = jnp.where(strict, att, 0.0)
                    atts.append(att.astype(BF16))
                    cs[hh] = cs[hh] + sums[hh, b]
                vb = v_ref[sls[b], :]
                v_heads = jnp.concatenate([jnp.where(heads[hh], vb, jnp.zeros_like(vb)) for hh in range(hs)], axis=0)
                o = o + _dot(jnp.concatenate(atts, axis=1), v_heads)
            return tuple(cs), o

        carry = (tuple(jnp.zeros((t, ROW_SUM_LANES), F32) for _ in range(hs)), jnp.zeros((t, w), F32))
        carry = _sb_sweep(_sb_plan(i + 1, SB_FWD_GROUPS), i, -1, block, carry)
        o_ref[...] = carry[1].astype(o_ref.dtype)
        if hosted:
            at_end()

    blocks = d_model // w
    res = pl.pallas_call(
        body, name=name, grid=grid,
        in_specs=[pl.BlockSpec((t, w), lambda g, i: (i, g)), pl.BlockSpec((l, w), lambda g, i: (0, blocks + g)),
                  pl.BlockSpec((l, w), lambda g, i: (0, 2 * blocks + g))] + (hosted.specs if hosted else []),
        out_specs=[pl.BlockSpec((t, w), lambda g, i: (i, g))] + (hosted.specs if hosted else []),
        out_shape=[jax.ShapeDtypeStruct((l, d_model), BF16)] + (hosted.out_shape if hosted else []),
        scratch_shapes=hosted.sems if hosted else [], compiler_params=_cparams(2),
    )(qkv, qkv, qkv, *(hosted.xs if hosted else []))
    return (res[0], res[1:]) if hosted else res[0]


def _sb_bwd(qkv, do, *, name, hosted=None):
    l = qkv.shape[0]
    d_model = qkv.shape[1] // 3
    dh = SB_HEAD_DIM
    t = _tile(l, SB_TILE)
    nq = l // t
    hs = 2 * SB_HEADS_PER_STEP
    w = hs * dh
    blocks = d_model // w
    scale = dh ** -0.5
    grid = (blocks, nq)
    nh = hosted.n if hosted else 0

    def body(q_ref, k_ref, v_ref, do_ref, *rest):
        dq_ref, dk_ref, dv_ref = rest[nh:nh + 3]
        e_scr, s_scr = rest[2 * nh + 3:2 * nh + 5]
        at_end = hosted.run(rest[:nh], rest[nh + 3:2 * nh + 3], rest[2 * nh + 5:], grid) if hosted else None
        i = pl.program_id(1)

        @pl.when(i == 0)
        def _():
            dk_ref[...] = jnp.zeros_like(dk_ref)
            dv_ref[...] = jnp.zeros_like(dv_ref)

        heads = _head_masks(hs)
        q_all = (q_ref[...].astype(F32) * scale).astype(BF16)
        do_all = do_ref[...]
        qs = [jnp.where(heads[hh], q_all, jnp.zeros_like(q_all)) for hh in range(hs)]
        dos = [jnp.where(heads[hh], do_all, jnp.zeros_like(do_all)) for hh in range(hs)]
        row = lax.broadcasted_iota(jnp.int32, (t, t), 0)
        col = lax.broadcasted_iota(jnp.int32, (t, t), 1)
        strict = col < row
        tri_suffix = _cumsum_operand(strict.astype(BF16))
        tri_prefix = _cumsum_operand((row < col).astype(BF16))

        def sweep1(jbs, cs, masks):
            sls = [pl.ds(pl.multiple_of(jb * t, t), t) for jb in jbs]
            chains = [(hh, b) for b in range(len(jbs)) for hh in range(hs)]
            zs = {(hh, b): _dot(qs[hh], k_ref[sls[b], :], NT) for hh, b in chains}
            datts = {(hh, b): _dot(dos[hh], v_ref[sls[b], :], NT) for hh, b in chains}
            lbs, tails, sums = {}, {}, {}
            for hh, b in chains:
                lb, lk = _sb_logs(zs[hh, b])
                if masks[b]:
                    lk = jnp.where(strict, lk, 0.0)
                lbs[hh, b] = lb
                tails[hh, b], sums[hh, b] = _cumsum_rowsum(lk, tri_suffix)
                s_scr[hh, jbs[b]] = jnp.exp(lb).astype(BF16)
            cs = list(cs)
            for hh, b in chains:
                att = jnp.exp(lbs[hh, b] + tails[hh, b] + _across_lanes(cs[hh], t))
                if masks[b]:
                    att = jnp.where(strict, att, 0.0)
                e_scr[hh, jbs[b]] = att * datts[hh, b]
                dv_ref[sls[b], :] += _dot(att.astype(BF16), dos[hh], TN)
                cs[hh] = cs[hh] + sums[hh, b]
            return tuple(cs)

        plan = _sb_plan(i + 1, SB_BWD_GROUPS)
        _sb_sweep(plan, i, -1, sweep1, tuple(jnp.zeros((t, ROW_SUM_LANES), F32) for _ in range(hs)))

        def sweep2(jbs, carry, masks):
            sls = [pl.ds(pl.multiple_of(jb * t, t), t) for jb in jbs]
            chains = [(hh, b) for b in range(len(jbs)) for hh in range(hs)]
            des = {(hh, b): e_scr[hh, jbs[b]] for hh, b in chains}
            pres = {(hh, b): _cumsum_rowsum(des[hh, b], tri_prefix) for hh, b in chains}
            carry = [list(c) for c in carry]
            for hh, b in chains:
                p, dq = carry[hh]
                de, sg = des[hh, b], s_scr[hh, jbs[b]].astype(F32)
                dlk = _across_lanes(p, t) + pres[hh, b][0]
                if masks[b]:
                    dlk = jnp.where(strict, dlk, 0.0)
                dz = (de - sg * (de + dlk)).astype(BF16)
                dk_ref[sls[b], :] += _dot(dz, qs[hh], TN)
                carry[hh] = [p + pres[hh, b][1], dq + _dot(dz, k_ref[sls[b], :])]
            return tuple(tuple(c) for c in carry)

        carry = tuple((jnp.zeros((t, ROW_SUM_LANES), F32), jnp.zeros((t, w), F32)) for _ in range(hs))
        carry = _sb_sweep(plan[::-1], 0, 1, sweep2, carry)
        dq = jnp.zeros((t, w), F32)
        for hh in range(hs):
            dq = jnp.where(heads[hh], carry[hh][1], dq)
        dq_ref[...] = (dq * scale).astype(dq_ref.dtype)
        if hosted:
            at_end()

    qspec = pl.BlockSpec((t, w), lambda g, i: (i, g))
    cols = lambda off: pl.BlockSpec((l, w), lambda g, i: (0, off + g), pipeline_mode=pl.Buffered(1))
    res = pl.pallas_call(
        body, name=name, grid=grid,
        in_specs=[qspec, cols(blocks), cols(2 * blocks), qspec] + (hosted.specs if hosted else []),
        out_specs=[qspec, cols(0), cols(0)] + (hosted.specs if hosted else []),
        out_shape=[jax.ShapeDtypeStruct((l, d_model), BF16), jax.ShapeDtypeStruct((l, d_model), F32),
                   jax.ShapeDtypeStruct((l, d_model), F32)] + (hosted.out_shape if hosted else []),
        scratch_shapes=[pltpu.VMEM((hs, nq, t, t), F32), pltpu.VMEM((hs, nq, t, t), BF16)]
        + (hosted.sems if hosted else []),
        compiler_params=_cparams(2),
    )(qkv, qkv, qkv, do, *(hosted.xs if hosted else []))
    return (res[0], res[1], res[2], res[3:]) if hosted else res


def _sb_layer_fwd(x, g, wqkv, wo, tag, hosted=None):
    h = _rmsnorm(x, g, name=f"{tag}_norm")
    qkv = _mm(h, wqkv, tm=1024, tn=1024, tk=1024, name=f"{tag}_qkv", out_dtype=BF16)
    o = _sb_fwd(qkv, name=f"{tag}_attn", hosted=hosted)
    carried = None
    if hosted:
        o, carried = o
    xo = _mm(o, wo, tm=1024, tn=1024, tk=1024, name=f"{tag}_out", extras=[(x, "tile")],
             epilogue=lambda acc, xt: (xt + acc,))
    return xo, (x, h, qkv, o), carried


def _sb_layer_bwd(dout, saved, g, wqkv, wo, tag, hosted=None):
    x, h, qkv, o = saved
    do = _mm(dout, wo, tb=True, tm=1024, tn=1024, tk=1024, name=f"{tag}_do", out_dtype=BF16)
    dwo = _mm(o, dout, ta=True, tm=1024, tn=1024, tk=2048, name=f"{tag}_dwo", out_dtype=BF16)
    res = _sb_bwd(qkv, do, name=f"{tag}_attn_bwd", hosted=hosted)
    dq, dk, dv = res[:3]
    carried = res[3] if hosted else None
    dqkv = jnp.concatenate([dq, dk.astype(BF16), dv.astype(BF16)], axis=1)
    dwqkv = _mm(h, dqkv, ta=True, tm=1024, tn=1024, tk=2048, name=f"{tag}_dwqkv", out_dtype=BF16)
    dx, dg = _mm(dqkv, wqkv, tb=True, tm=512, tn=1024, tk=3072, name=f"{tag}_dx",
                 extras=[(x, "tile"), (g, "row"), (dout, "tile")], outs=[(F32, "tile"), (F32, "colsum")],
                 epilogue=_norm_bwd_epilogue)
    return dx, dg, dwqkv, dwo, carried


def _shift_down(x, s, t_idx):
    return jnp.where(t_idx >= s, pltpu.roll(x, s, 0), 0.0)


def _shift_up(x, s, t_idx):
    n = x.shape[0]
    return jnp.where(t_idx < n - s, pltpu.roll(x, n - s, 0), 0.0)


def _sc_fwd(p, cw, *, name):
    l = p.shape[0]
    d = cw.shape[1]
    tc = 128
    nb = d // tc

    def body(b_ref, c_ref, h_ref, w_ref, o_ref):
        v = c_ref[...] * h_ref[...]
        t_idx = lax.broadcasted_iota(jnp.int32, v.shape, 0)
        u = v * w_ref[2:3, :] + _shift_down(v, 1, t_idx) * w_ref[1:2, :] + _shift_down(v, 2, t_idx) * w_ref[0:1, :]
        o_ref[...] = (b_ref[...] * u).astype(BF16)

    return pl.pallas_call(
        body, name=name, grid=(nb,),
        in_specs=[pl.BlockSpec((l, tc), lambda j: (0, j)), pl.BlockSpec((l, tc), lambda j: (0, nb + j)),
                  pl.BlockSpec((l, tc), lambda j: (0, 2 * nb + j)), pl.BlockSpec((3, tc), lambda j: (0, j))],
        out_specs=pl.BlockSpec((l, tc), lambda j: (0, j)),
        out_shape=jax.ShapeDtypeStruct((l, d), BF16), compiler_params=_cparams(1),
    )(p, p, p, cw)


def _sc_bwd(p, cw, dbu, *, name):
    l = p.shape[0]
    d = cw.shape[1]
    tc = 128
    nb = d // tc

    def body(b_ref, c_ref, h_ref, w_ref, g_ref, db_ref, dc_ref, dh_ref, dw_ref):
        cv, hv = c_ref[...], h_ref[...]
        v = cv * hv
        t_idx = lax.broadcasted_iota(jnp.int32, v.shape, 0)
        v1, v2 = _shift_down(v, 1, t_idx), _shift_down(v, 2, t_idx)
        u = v * w_ref[2:3, :] + v1 * w_ref[1:2, :] + v2 * w_ref[0:1, :]
        dbu_v = g_ref[...]
        db_ref[...] = (dbu_v * u).astype(BF16)
        du = dbu_v * b_ref[...]
        dv = du * w_ref[2:3, :] + _shift_up(du, 1, t_idx) * w_ref[1:2, :] + _shift_up(du, 2, t_idx) * w_ref[0:1, :]
        dc_ref[...] = (dv * hv).astype(BF16)
        dh_ref[...] = (dv * cv).astype(BF16)
        dw_ref[...] = jnp.zeros_like(dw_ref)
        dw_ref[0:1, :] = jnp.sum(du * v2, axis=0, keepdims=True)
        dw_ref[1:2, :] = jnp.sum(du * v1, axis=0, keepdims=True)
        dw_ref[2:3, :] = jnp.sum(du * v, axis=0, keepdims=True)

    col = lambda off: pl.BlockSpec((l, tc), lambda j: (0, off + j))
    return pl.pallas_call(
        body, name=name, grid=(nb,),
        in_specs=[col(0), col(nb), col(2 * nb), pl.BlockSpec((3, tc), lambda j: (0, j)), col(0)],
        out_specs=[col(0), col(0), col(0), pl.BlockSpec((8, tc), lambda j: (0, j))],
        out_shape=[jax.ShapeDtypeStruct((l, d), BF16)] * 3 + [jax.ShapeDtypeStruct((8, d), F32)],
        compiler_params=_cparams(1),
    )(p, p, p, cw, dbu)


def _sc_layer_fwd(x, g, win, cw, wout, tag):
    h = _rmsnorm(x, g, name=f"{tag}_norm")
    p = _mm(h, win, tm=1024, tn=1024, tk=1024, name=f"{tag}_in")
    bu = _sc_fwd(p, cw, name=f"{tag}_conv")
    xo = _mm(bu, wout, tm=1024, tn=1024, tk=1024, name=f"{tag}_out", extras=[(x, "tile")],
             epilogue=lambda acc, xt: (xt + acc,))
    return xo, (x, h, p, bu)


def _sc_layer_bwd(dout, saved, g, win, cw, wout, tag):
    x, h, p, bu = saved
    dbu = _mm(dout, wout, tb=True, tm=1024, tn=1024, tk=1024, name=f"{tag}_dbu")
    dwout = _mm(bu, dout, ta=True, tm=1024, tn=1024, tk=2048, name=f"{tag}_dwout", out_dtype=BF16)
    db, dc, dh, dcw = _sc_bwd(p, cw, dbu, name=f"{tag}_conv_bwd")
    dp = jnp.concatenate([db, dc, dh], axis=1)
    dwin = _mm(h, dp, ta=True, tm=1024, tn=1024, tk=2048, name=f"{tag}_dwin", out_dtype=BF16)
    dx, dg = _mm(dp, win, tb=True, tm=512, tn=1024, tk=3072, name=f"{tag}_dx",
                 extras=[(x, "tile"), (g, "row"), (dout, "tile")], outs=[(F32, "tile"), (F32, "colsum")],
                 epilogue=_norm_bwd_epilogue)
    return dx, dg, dwin, dcw[:3], dwout


def _ssd_conv_fwd(p, cw, cb, *, name):
    l = p.shape[0]
    tc = 128
    nb = SSD_CONV_DIM // tc
    off = SSD_D_INNER // tc

    def body(x_ref, w_ref, b_ref, o_ref):
        xv = x_ref[...]
        t_idx = lax.broadcasted_iota(jnp.int32, xv.shape, 0)
        pre = xv * w_ref[3:4, :] + b_ref[...]
        for s in (1, 2, 3):
            pre = pre + _shift_down(xv, s, t_idx) * w_ref[3 - s:4 - s, :]
        o_ref[...] = pre * _sigmoid(pre)

    return pl.pallas_call(
        body, name=name, grid=(nb,),
        in_specs=[pl.BlockSpec((l, tc), lambda j: (0, off + j)), pl.BlockSpec((4, tc), lambda j: (0, j)),
                  pl.BlockSpec((1, tc), lambda j: (0, j))],
        out_specs=pl.BlockSpec((l, tc), lambda j: (0, j)),
        out_shape=jax.ShapeDtypeStruct((l, SSD_CONV_DIM), F32), compiler_params=_cparams(1),
    )(p, cw, cb)


def _ssd_conv_bwd(p, cw, cb, dxs_scan, dxs_skip, dbm, dcm, *, name):
    l = p.shape[0]
    tc = 128
    nb = SSD_CONV_DIM // tc
    off = SSD_D_INNER // tc
    n_xs, n_b = SSD_D_INNER // tc, SSD_GROUPS * SSD_STATE // tc

    def body(x_ref, w_ref, b_ref, ga_ref, gb_ref, gm_ref, gc_ref, dx_ref, dw_ref):
        j = pl.program_id(0)
        g_val = jnp.where(j < n_xs, ga_ref[...] + gb_ref[...], jnp.where(j < n_xs + n_b, gm_ref[...], gc_ref[...]))
        xv = x_ref[...]
        t_idx = lax.broadcasted_iota(jnp.int32, xv.shape, 0)
        xs = [xv] + [_shift_down(xv, s, t_idx) for s in (1, 2, 3)]
        pre = b_ref[...] + xs[0] * w_ref[3:4, :]
        for s in (1, 2, 3):
            pre = pre + xs[s] * w_ref[3 - s:4 - s, :]
        sg = _sigmoid(pre)
        dpre = g_val * sg * (1.0 + pre * (1.0 - sg))
        dx = dpre * w_ref[3:4, :]
        for s in (1, 2, 3):
            dx = dx + _shift_up(dpre, s, t_idx) * w_ref[3 - s:4 - s, :]
        dx_ref[...] = dx
        dw_ref[...] = jnp.zeros_like(dw_ref)
        for s in (0, 1, 2, 3):
            dw_ref[3 - s:4 - s, :] = jnp.sum(dpre * xs[s], axis=0, keepdims=True)
        dw_ref[4:5, :] = jnp.sum(dpre, axis=0, keepdims=True)

    return pl.pallas_call(
        body, name=name, grid=(nb,),
        in_specs=[pl.BlockSpec((l, tc), lambda j: (0, off + j)), pl.BlockSpec((4, tc), lambda j: (0, j)),
                  pl.BlockSpec((1, tc), lambda j: (0, j)),
                  pl.BlockSpec((l, tc), lambda j: (0, jnp.minimum(j, n_xs - 1))),
                  pl.BlockSpec((l, tc), lambda j: (0, jnp.minimum(j, n_xs - 1))),
                  pl.BlockSpec((l, tc), lambda j: (0, jnp.clip(j - n_xs, 0, n_b - 1))),
                  pl.BlockSpec((l, tc), lambda j: (0, jnp.clip(j - n_xs - n_b, 0, n_b - 1)))],
        out_specs=[pl.BlockSpec((l, tc), lambda j: (0, j)), pl.BlockSpec((8, tc), lambda j: (0, j))],
        out_shape=[jax.ShapeDtypeStruct((l, SSD_CONV_DIM), F32), jax.ShapeDtypeStruct((8, SSD_CONV_DIM), F32)],
        compiler_params=_cparams(1),
    )(p, cw, cb, dxs_scan, dxs_skip, dbm, dcm)


def _ssd_dt_fwd(p, bias, *, name):
    l = p.shape[0]
    tm = _tile(l, 1024)
    off = (SSD_D_INNER + SSD_CONV_DIM) // 128

    def body(x_ref, b_ref, o_ref):
        v = x_ref[...] + b_ref[...]
        o_ref[...] = jnp.maximum(v, 0.0) + jnp.log(1.0 + jnp.exp(-jnp.abs(v)))

    return pl.pallas_call(
        body, name=name, grid=(l // tm,),
        in_specs=[pl.BlockSpec((tm, 128), lambda i: (i, off)), pl.BlockSpec((1, 128), lambda i: (0, 0))],
        out_specs=pl.BlockSpec((tm, 128), lambda i: (i, 0)),
        out_shape=jax.ShapeDtypeStruct((l, 128), F32), compiler_params=_cparams(1),
    )(p, bias)


def _ssd_dt_bwd(p, bias, ddt, *, name):
    l = p.shape[0]
    tm = _tile(l, 1024)
    off = (SSD_D_INNER + SSD_CONV_DIM) // 128

    def body(x_ref, b_ref, g_ref, o_ref, db_ref):
        i = pl.program_id(0)
        d = g_ref[...] * _sigmoid(x_ref[...] + b_ref[...])
        o_ref[...] = d
        _accumulate(db_ref, jnp.sum(d, axis=0, keepdims=True), i == 0)

    return pl.pallas_call(
        body, name=name, grid=(l // tm,),
        in_specs=[pl.BlockSpec((tm, 128), lambda i: (i, off)), pl.BlockSpec((1, 128), lambda i: (0, 0)),
                  pl.BlockSpec((tm, 128), lambda i: (i, 0))],
        out_specs=[pl.BlockSpec((tm, 128), lambda i: (i, 0)), pl.BlockSpec((1, 128), lambda i: (0, 0))],
        out_shape=[jax.ShapeDtypeStruct((l, 128), F32), jax.ShapeDtypeStruct((1, 128), F32)],
        compiler_params=_cparams(1),
    )(p, bias, ddt)


def _row_to_col(r, eye):
    return jnp.sum(jnp.where(eye, r, 0.0), axis=1, keepdims=True)


def _col_to_row(c, eye):
    return jnp.sum(jnp.where(eye, c, 0.0), axis=0, keepdims=True)


def _ssd_chunk_common(b_ref, c_ref, dt_ref, a_ref, lam_scr):
    n = SSD_CHUNK
    row = lax.broadcasted_iota(jnp.int32, (n, n), 0)
    col = lax.broadcasted_iota(jnp.int32, (n, n), 1)
    bm, cm = b_ref[...].astype(BF16), c_ref[...].astype(BF16)
    g = _dot(cm, bm, NT)
    incl = (row <= col).astype(BF16)
    lam_scr[...] = _dot_exact(dt_ref[...] * a_ref[...], incl)
    return row, col, bm, cm, g


def _ssd_head_common(r, row, col, dt_ref, lam_scr):
    eye, tril = row == col, row >= col
    lam_r = lam_scr[r:r + 1, :]
    dt_r = dt_ref[r:r + 1, :]
    lam_c = _row_to_col(lam_r, eye)
    dt_c = _row_to_col(dt_r, eye)
    dk = jnp.where(tril, jnp.exp(jnp.minimum(lam_c - lam_r, 0.0)), 0.0)
    lam_last = jnp.sum(jnp.where(col[0:1, :] == SSD_CHUNK - 1, lam_r, 0.0), axis=1, keepdims=True)
    return eye, lam_r, dt_r, lam_c, dt_c, dk, lam_last


def _ssd_fwd(xh, act, dt_t, a_b, *, name, hosted=None):
    l = xh.shape[1]
    nc = l // SSD_CHUNK
    n, p_dim, hpg = SSD_CHUNK, SSD_HEAD_DIM, SSD_HPG

    gps = SSD_GROUPS_PER_STEP
    n_grp = SSD_GROUPS // gps
    grid = (n_grp, nc)
    nh = hosted.n if hosted else 0

    def body(x_ref, b_ref, c_ref, dt_ref, a_ref, *rest):
        y_ref, hp_ref = rest[nh:nh + 2]
        h_scr, lam_scr = rest[2 * nh + 2:2 * nh + 4]
        at_end = hosted.run(rest[:nh], rest[nh + 2:2 * nh + 2], rest[2 * nh + 4:], grid) if hosted else None

        @pl.when(pl.program_id(1) == 0)
        def _():
            h_scr[...] = jnp.zeros_like(h_scr)

        lanes = [pl.ds(gg * SSD_STATE, SSD_STATE) for gg in range(gps)]
        common = [_ssd_chunk_common(b_ref.at[:, lanes[gg]], c_ref.at[:, lanes[gg]], dt_ref.at[gg], a_ref.at[gg],
                                    lam_scr.at[gg]) for gg in range(gps)]
        for gg in range(gps):
            row, col, bm, cm, g = common[gg]
            for r in range(hpg):
                hd = gg * hpg + r
                _, _, dt_r, lam_c, dt_c, dk, lam_last = _ssd_head_common(r, row, col, dt_ref.at[gg], lam_scr.at[gg])
                xr = x_ref[hd]
                hr = h_scr[hd]
                w = (g * dk * dt_r).astype(BF16)
                y = _dot(w, xr.astype(BF16)) + _dot(cm, hr.astype(BF16), NT) * jnp.exp(lam_c)
                y_ref[hd] = y
                hp_ref[hd] = hr
                xw = (xr * (jnp.exp(lam_last - lam_c) * dt_c)).astype(BF16)
                h_scr[hd] = jnp.exp(lam_last) * hr + _dot(xw, bm, TN)
        if hosted:
            at_end()

    g_off = SSD_D_INNER // (gps * SSD_STATE)
    res = pl.pallas_call(
        body, name=name, grid=grid,
        in_specs=[pl.BlockSpec((gps * hpg, n, p_dim), lambda g, c: (g, c, 0)),
                  pl.BlockSpec((n, gps * SSD_STATE), lambda g, c: (c, g_off + g)),
                  pl.BlockSpec((n, gps * SSD_STATE), lambda g, c: (c, g_off + n_grp + g)),
                  pl.BlockSpec((gps, 8, n), lambda g, c: (g, 0, c)),
                  pl.BlockSpec((gps, 8, 128), lambda g, c: (g, 0, 0))] + (hosted.specs if hosted else []),
        out_specs=[pl.BlockSpec((gps * hpg, n, p_dim), lambda g, c: (g, c, 0)),
                   pl.BlockSpec((None, gps * hpg, p_dim, SSD_STATE), lambda g, c: (c, g, 0, 0))]
        + (hosted.specs if hosted else []),
        out_shape=[jax.ShapeDtypeStruct(xh.shape, F32),
                   jax.ShapeDtypeStruct((nc, SSD_HEADS, p_dim, SSD_STATE), F32)] + (hosted.out_shape if hosted else []),
        scratch_shapes=[pltpu.VMEM((gps * hpg, p_dim, SSD_STATE), F32), pltpu.VMEM((gps, 8, n), F32)]
        + (hosted.sems if hosted else []),
        compiler_params=_cparams(2),
    )(xh, act, act, dt_t, a_b, *(hosted.xs if hosted else []))
    return (res[0], res[1], res[2:]) if hosted else res


def _ssd_bwd(xh, act, dt_t, a_b, hprev, dyh, *, name):
    l = xh.shape[1]
    nc = l // SSD_CHUNK
    n, p_dim, hpg = SSD_CHUNK, SSD_HEAD_DIM, SSD_HPG
    gps = SSD_GROUPS_PER_STEP

    def body(x_ref, b_ref, c_ref, dt_ref, a_ref, hp_ref, dy_ref,
             dx_ref, db_ref, dc_ref, ddt_ref, da_ref, dh_scr, lam_scr, dlam_scr, ddt_scr):
        ci = pl.program_id(1)

        @pl.when(ci == 0)
        def _():
            dh_scr[...] = jnp.zeros_like(dh_scr)

        lanes = [pl.ds(gg * SSD_STATE, SSD_STATE) for gg in range(gps)]
        common = [_ssd_chunk_common(b_ref.at[:, lanes[gg]], c_ref.at[:, lanes[gg]], dt_ref.at[gg], a_ref.at[gg],
                                    lam_scr.at[gg]) for gg in range(gps)]
        dlam_scr[...] = jnp.zeros_like(dlam_scr)
        ddt_scr[...] = jnp.zeros_like(ddt_scr)
        for gg in range(gps):
            row, col, bm, cm, g = common[gg]
            dt_g, lam_g, dlam_g, ddt_g = dt_ref.at[gg], lam_scr.at[gg], dlam_scr.at[gg], ddt_scr.at[gg]
            dg_acc = jnp.zeros((n, n), F32)
            dc_acc = jnp.zeros((n, SSD_STATE), F32)
            db_acc = jnp.zeros((n, SSD_STATE), F32)
            for r in range(hpg):
                hd = gg * hpg + r
                eye, _, dt_r, lam_c, dt_c, dk, lam_last = _ssd_head_common(r, row, col, dt_g, lam_g)
                xr, dyr, hr, dhr = x_ref[hd], dy_ref[hd], hp_ref[hd], dh_scr[hd]
                xb, dyb, hb, dhb = xr.astype(BF16), dyr.astype(BF16), hr.astype(BF16), dhr.astype(BF16)
                e_l = jnp.exp(lam_c)
                e_last = jnp.exp(lam_last)
                decay_c = jnp.exp(lam_last - lam_c)
                w_c = decay_c * dt_c
                m = g * dk * dt_r
                dm = _dot(dyb, xb, NT)
                bdh = _dot(bm, dhb, NT)
                dx_ref[hd] = _dot(m.astype(BF16), dyb, TN) + w_c * bdh
                dg_acc = dg_acc + dm * dk * dt_r
                q_mat = dm * g * dk
                p_mat = q_mat * dt_r
                yoff = _dot(cm, hb, NT) * e_l
                q_c = jnp.sum(xr * bdh, axis=1, keepdims=True)
                dlam_c = (jnp.sum(p_mat, axis=1, keepdims=True) + jnp.sum(dyr * yoff, axis=1, keepdims=True)
                          - w_c * q_c)
                d_last = (jnp.sum(w_c * q_c, axis=0, keepdims=True)
                          + e_last * jnp.sum(jnp.sum(dhr * hr, axis=1, keepdims=True), axis=0, keepdims=True))
                dlam_g[r:r + 1, :] = (_col_to_row(dlam_c, eye) - jnp.sum(p_mat, axis=0, keepdims=True)
                                      + jnp.where(col[0:1, :] == n - 1, d_last, 0.0))
                ddt_g[r:r + 1, :] = jnp.sum(q_mat, axis=0, keepdims=True) + _col_to_row(decay_c * q_c, eye)
                dc_acc = dc_acc + e_l * _dot(dyb, hb)
                db_acc = db_acc + _dot((xr * w_c).astype(BF16), dhb)
                dh_scr[hd] = e_last * dhr + _dot((dyr * e_l).astype(BF16), cm, TN)

            dgb = dg_acc.astype(BF16)
            dc_ref[:, lanes[gg]] = _dot(dgb, bm) + dc_acc
            db_ref[:, lanes[gg]] = _dot(dgb, cm, TN) + db_acc
            rev = (row >= col).astype(BF16)
            da = _dot_exact(dlam_g[...], rev)
            ddt_ref[gg] = ddt_g[...] + da * a_ref[gg]
            _accumulate(da_ref.at[gg], da * dt_g[...], ci == 0)

        @pl.when(ci == nc - 1)
        def _():
            for gg in range(gps):
                da_ref[gg] = jnp.broadcast_to(jnp.sum(da_ref[gg], axis=1, keepdims=True), da_ref.shape[1:])

    g_off = SSD_D_INNER // (gps * SSD_STATE)
    n_grp = SSD_GROUPS // gps
    rc = lambda c: nc - 1 - c
    hspec = pl.BlockSpec((gps * hpg, n, p_dim), lambda g, c: (g, rc(c), 0))
    gspec = pl.BlockSpec((n, gps * SSD_STATE), lambda g, c: (rc(c), g))
    return pl.pallas_call(
        body, name=name, grid=(n_grp, nc),
        in_specs=[hspec,
                  pl.BlockSpec((n, gps * SSD_STATE), lambda g, c: (rc(c), g_off + g)),
                  pl.BlockSpec((n, gps * SSD_STATE), lambda g, c: (rc(c), g_off + n_grp + g)),
                  pl.BlockSpec((gps, 8, n), lambda g, c: (g, 0, rc(c))),
                  pl.BlockSpec((gps, 8, 128), lambda g, c: (g, 0, 0)),
                  pl.BlockSpec((None, gps * hpg, p_dim, SSD_STATE), lambda g, c: (rc(c), g, 0, 0)),
                  hspec],
        out_specs=[hspec, gspec, gspec,
                   pl.BlockSpec((gps, 8, n), lambda g, c: (g, 0, rc(c))),
                   pl.BlockSpec((gps, 8, 128), lambda g, c: (g, 0, 0))],
        out_shape=[jax.ShapeDtypeStruct(xh.shape, F32),
                   jax.ShapeDtypeStruct((l, SSD_GROUPS * SSD_STATE), F32),
                   jax.ShapeDtypeStruct((l, SSD_GROUPS * SSD_STATE), F32),
                   jax.ShapeDtypeStruct(dt_t.shape, F32),
                   jax.ShapeDtypeStruct(a_b.shape, F32)],
        scratch_shapes=[pltpu.VMEM((gps * hpg, p_dim, SSD_STATE), F32), pltpu.VMEM((gps, 8, n), F32),
                        pltpu.VMEM((gps, 8, n), F32), pltpu.VMEM((gps, 8, n), F32)],
        compiler_params=_cparams(2),
    )(xh, act, act, dt_t, a_b, hprev, dyh)


def _ssd_gate_fwd(y, act, p, d_vec, gn, *, name):
    l = y.shape[0]
    w = SSD_D_INNER
    tm = _tile(l, 256)

    def body(y_ref, xs_ref, z_ref, d_ref, g_ref, o_ref):
        for gi in range(SSD_GROUPS):
            sl = slice(gi * SSD_NORM_GROUP, (gi + 1) * SSD_NORM_GROUP)
            z = z_ref[:, sl]
            y2 = (y_ref[:, sl] + d_ref[:, sl] * xs_ref[:, sl]) * (z * _sigmoid(z))
            r = lax.rsqrt(jnp.mean(y2 * y2, axis=1, keepdims=True) + RMS_EPS)
            o_ref[:, sl] = (y2 * r * g_ref[:, sl]).astype(BF16)

    rows = pl.BlockSpec((tm, w), lambda i: (i, 0))
    vec = pl.BlockSpec((1, w), lambda i: (0, 0))
    return pl.pallas_call(
        body, name=name, grid=(l // tm,), in_specs=[rows, rows, rows, vec, vec], out_specs=rows,
        out_shape=jax.ShapeDtypeStruct((l, w), BF16), compiler_params=_cparams(1),
    )(y, act, p, d_vec, gn)


def _ssd_gate_bwd(dyn, y, act, p, d_vec, gn, *, name):
    l = y.shape[0]
    w = SSD_D_INNER
    tm = _tile(l, 256)

    def body(dyn_ref, y_ref, xs_ref, z_ref, d_ref, g_ref, dy_ref, dz_ref, dxs_ref, dd_ref, dg_ref):
        i = pl.program_id(0)
        for gi in range(SSD_GROUPS):
            sl = slice(gi * SSD_NORM_GROUP, (gi + 1) * SSD_NORM_GROUP)
            z, xs, dv = z_ref[:, sl], xs_ref[:, sl], d_ref[:, sl]
            s = _sigmoid(z)
            sz = z * s
            y1 = y_ref[:, sl] + dv * xs
            y2 = y1 * sz
            r = lax.rsqrt(jnp.mean(y2 * y2, axis=1, keepdims=True) + RMS_EPS)
            y2h = y2 * r
            dyn_v = dyn_ref[:, sl]
            d2h = dyn_v * g_ref[:, sl]
            dy2 = r * (d2h - y2h * jnp.mean(d2h * y2h, axis=1, keepdims=True))
            dy1 = dy2 * sz
            dy_ref[:, sl] = dy1
            dz_ref[:, sl] = dy2 * y1 * s * (1.0 + z * (1.0 - s))
            dxs_ref[:, sl] = dv * dy1
            _accumulate(dd_ref.at[:, sl], jnp.sum(dy1 * xs, axis=0, keepdims=True), i == 0)
            _accumulate(dg_ref.at[:, sl], jnp.sum(dyn_v * y2h, axis=0, keepdims=True), i == 0)

    rows = pl.BlockSpec((tm, w), lambda i: (i, 0))
    vec = pl.BlockSpec((1, w), lambda i: (0, 0))
    return pl.pallas_call(
        body, name=name, grid=(l // tm,), in_specs=[rows, rows, rows, rows, vec, vec],
        out_specs=[rows, rows, rows, vec, vec],
        out_shape=[jax.ShapeDtypeStruct((l, w), F32)] * 3 + [jax.ShapeDtypeStruct((1, w), F32)] * 2,
        compiler_params=_cparams(1),
    )(dyn, y, act, p, d_vec, gn)


def _heads_major(x):
    return x.reshape(x.shape[0], SSD_HEADS, SSD_HEAD_DIM).transpose(1, 0, 2)


def _ssd_layer_fwd(x, g, win, cw, cb, dt_bias, a_log, d_skip, gn, wout, tag, hosted=None):
    l = x.shape[0]
    h = _rmsnorm(x, g, name=f"{tag}_norm")
    p = _mm(h, win, tm=1024, tn=896, tk=1024, name=f"{tag}_in")
    act = _ssd_conv_fwd(p, cw, cb, name=f"{tag}_conv")
    bias = jnp.pad(dt_bias, (0, 128 - SSD_HEADS)).reshape(1, 128)
    dt = _ssd_dt_fwd(p, bias, name=f"{tag}_dt")
    xh = _heads_major(act[:, :SSD_D_INNER])
    dt_t = jnp.pad(dt[:, :SSD_HEADS].T.reshape(SSD_GROUPS, SSD_HPG, l), ((0, 0), (0, 8 - SSD_HPG), (0, 0)))
    a = -jnp.exp(a_log).reshape(SSD_GROUPS, SSD_HPG, 1)
    a_b = jnp.broadcast_to(jnp.pad(a, ((0, 0), (0, 8 - SSD_HPG), (0, 0))), (SSD_GROUPS, 8, 128))
    res = _ssd_fwd(xh, act, dt_t, a_b, name=f"{tag}_scan", hosted=hosted)
    yh, hprev = res[:2]
    carried = res[2] if hosted else None
    y = yh.transpose(1, 0, 2).reshape(l, SSD_D_INNER)
    d_vec = jnp.repeat(d_skip, SSD_HEAD_DIM).reshape(1, SSD_D_INNER)
    yn = _ssd_gate_fwd(y, act, p, d_vec, gn, name=f"{tag}_gate")
    xo = _mm(yn, wout, tm=1024, tn=1024, tk=2048, name=f"{tag}_out", extras=[(x, "tile")],
             epilogue=lambda acc, xt: (xt + acc,))
    return xo, (x, h, p, act, bias, xh, dt_t, a_b, hprev, y, d_vec, yn), carried


def _ssd_layer_bwd(dout, saved, g, win, cw, cb, gn, wout, tag):
    x, h, p, act, bias, xh, dt_t, a_b, hprev, y, d_vec, yn = saved
    l = x.shape[0]
    dyn = _mm(dout, wout, tb=True, tm=1024, tn=1024, tk=1024, name=f"{tag}_dyn")
    dwout = _mm(yn, dout, ta=True, tm=1024, tn=1024, tk=2048, name=f"{tag}_dwout", out_dtype=BF16)
    dy, dz, dxs_d, dd_vec, dgn = _ssd_gate_bwd(dyn, y, act, p, d_vec, gn, name=f"{tag}_gate_bwd")
    dxh, dbm, dcm, ddt_t, da_b = _ssd_bwd(xh, act, dt_t, a_b, hprev, _heads_major(dy), name=f"{tag}_scan_bwd")
    dxs_scan = dxh.transpose(1, 0, 2).reshape(l, SSD_D_INNER)
    dxbc, dcw8 = _ssd_conv_bwd(p, cw, cb, dxs_scan, dxs_d, dbm, dcm, name=f"{tag}_conv_bwd")
    ddt = jnp.pad(ddt_t[:, :SSD_HPG, :].reshape(SSD_HEADS, l).T, ((0, 0), (0, 128 - SSD_HEADS)))
    ddt_raw, dbias = _ssd_dt_bwd(p, bias, ddt, name=f"{tag}_dt_bwd")
    dp = jnp.concatenate([dz, dxbc, ddt_raw], axis=1)
    dwin = _mm(h, dp, ta=True, tm=1024, tn=896, tk=2048, name=f"{tag}_dwin", out_dtype=BF16)
    dx, dg = _mm(dp, win, tb=True, tm=256, tn=1024, tk=6272, name=f"{tag}_dx",
                 extras=[(x, "tile"), (g, "row"), (dout, "tile")], outs=[(F32, "tile"), (F32, "colsum")],
                 epilogue=_norm_bwd_epilogue)
    a_heads = a_b[:, :SSD_HPG, 0].reshape(SSD_HEADS)
    grads = dict(
        ssd_w_in=dwin[:, :SSD_IN_DIM], ssd_conv_w=dcw8[:4], ssd_conv_b=dcw8[4],
        ssd_dt_bias=dbias[0, :SSD_HEADS], ssd_a_log=da_b[:, :SSD_HPG, 0].reshape(SSD_HEADS) * a_heads,
        ssd_d=dd_vec.reshape(SSD_HEADS, SSD_HEAD_DIM).sum(axis=1), ssd_norm=dgn[0], ssd_w_out=dwout)
    return dx, dg, grads


def _local_step(x, tgt, w, gather_later=None, scatter_early=None):
    row = lambda v: v.reshape(1, -1)
    saved = []
    for i in range(DEPTH):
        kind, j = i % 3, i // 3
        x, s1 = _ffn_fwd(x, row(w["ffn1_norm"][i]), w["ffn1_w_gu"][i], w["ffn1_w_down"][i], f"l{i}f1")
        gm = row(w["mix_norm"][i])
        hook = (gather_later or {}).get(i)
        hosted = hook[0] if hook else None
        if kind == 0:
            x, sm, carried = _sb_layer_fwd(x, gm, w["sb_w_qkv"][j], w["sb_w_o"][j], f"l{i}sb", hosted=hosted)
        elif kind == 1:
            x, sm, carried = _ssd_layer_fwd(x, gm, w["ssd_w_in"][j], w["ssd_conv_w"][j], row(w["ssd_conv_b"][j]),
                                            w["ssd_dt_bias"][j], w["ssd_a_log"][j], w["ssd_d"][j],
                                            row(w["ssd_norm"][j]), w["ssd_w_out"][j], f"l{i}ssd", hosted=hosted)
        if hook:
            w = hook[1](w, carried)
        if kind == 2:
            x, sm = _sc_layer_fwd(x, gm, w["sc_w_in"][j], w["sc_conv_w"][j], w["sc_w_out"][j], f"l{i}sc")
        x, s2 = _ffn_fwd(x, row(w["ffn2_norm"][i]), w["ffn2_w_gu"][i], w["ffn2_w_down"][i], f"l{i}f2")
        saved.append((s1, sm, s2))

    loss, dx, dfinal = _final_loss(x, row(w["final_norm"]), tgt, name="final_loss")
    per_layer = {k: [None] * DEPTH for k in ("ffn1_norm", "ffn1_w_gu", "ffn1_w_down", "mix_norm",
                                             "ffn2_norm", "ffn2_w_gu", "ffn2_w_down")}
    per_layer.update({"sb_w_qkv": [None, None], "sb_w_o": [None, None]})
    grads = {"final_norm": dfinal[0]}
    early = None
    for i in reversed(range(DEPTH)):
        kind, j = i % 3, i // 3
        s1, sm, s2 = saved[i]
        dx, dg, dwgu, dwd = _ffn_bwd(dx, s2, row(w["ffn2_norm"][i]), w["ffn2_w_gu"][i], w["ffn2_w_down"][i], f"l{i}f2")
        per_layer["ffn2_norm"][i], per_layer["ffn2_w_gu"][i], per_layer["ffn2_w_down"][i] = dg[0], dwgu, dwd
        gm = row(w["mix_norm"][i])
        if kind == 0:
            hosted = scatter_early({**grads, **per_layer}) if (scatter_early and i == 0) else None
            dx, dg, dwqkv, dwo, carried = _sb_layer_bwd(dx, sm, gm, w["sb_w_qkv"][j], w["sb_w_o"][j], f"l{i}sb",
                                                        hosted=hosted)
            per_layer["sb_w_qkv"][j], per_layer["sb_w_o"][j] = dwqkv, dwo
            if hosted:
                early = carried
        elif kind == 1:
            dx, dg, sg = _ssd_layer_bwd(dx, sm, gm, w["ssd_w_in"][j], w["ssd_conv_w"][j], row(w["ssd_conv_b"][j]),
                                        row(w["ssd_norm"][j]), w["ssd_w_out"][j], f"l{i}ssd")
            sg["ssd_w_in"], sg["ssd_w_out"] = [sg["ssd_w_in"]], [sg["ssd_w_out"]]
            grads.update({k: (v if isinstance(v, list) else v[None]) for k, v in sg.items()})
        else:
            dx, dg, dwin, dcw, dwout = _sc_layer_bwd(dx, sm, gm, w["sc_w_in"][j], w["sc_conv_w"][j],
                                                     w["sc_w_out"][j], f"l{i}sc")
            grads.update(sc_w_in=[dwin], sc_conv_w=dcw[None], sc_w_out=[dwout])
        per_layer["mix_norm"][i] = dg[0]
        dx, dg, dwgu, dwd = _ffn_bwd(dx, s1, row(w["ffn1_norm"][i]), w["ffn1_w_gu"][i], w["ffn1_w_down"][i], f"l{i}f1")
        per_layer["ffn1_norm"][i], per_layer["ffn1_w_gu"][i], per_layer["ffn1_w_down"][i] = dg[0], dwgu, dwd
    for k, v in per_layer.items():
        grads[k] = jnp.stack(v) if k.endswith("_norm") else v
    return loss, dx, grads, early


_HBM = pl.BlockSpec(memory_space=pltpu.HBM)


def _remote(src, dst, send_sems, recv_sems, idx, dev):
    return pltpu.make_async_remote_copy(src_ref=src, dst_ref=dst, send_sem=send_sems.at[idx], recv_sem=recv_sems.at[idx],
                                        device_id=dev, device_id_type=pl.DeviceIdType.MESH)


def _exchange_call(body, xs, out_shapes, n_copies, name):
    n = len(xs)
    return pl.pallas_call(
        body, name=name, in_specs=[_HBM] * n, out_specs=[_HBM] * n,
        out_shape=[jax.ShapeDtypeStruct(s, x.dtype) for s, x in zip(out_shapes, xs)],
        scratch_shapes=[pltpu.SemaphoreType.DMA((n, n_copies)), pltpu.SemaphoreType.DMA((n, n_copies)),
                        pltpu.SemaphoreType.DMA((n,))],
    )(*xs)


def _gather(xs, *, name):
    n = len(xs)

    def body(*refs):
        start, finish = _gather_steps(refs[:n], refs[n:2 * n], *refs[2 * n:])
        start()
        finish()

    return _exchange_call(body, xs, _gather_shapes(xs), _GATHER_COPIES, name)


_GATHER_COPIES = 7


def _gather_shapes(xs):
    return [(N_DEV,) + x.shape for x in xs]


def _gather_steps(x_refs, o_refs, send_sems, recv_sems, local_sems):
    n = len(x_refs)

    def plan():
        mx, my, mc = lax.axis_index("x"), lax.axis_index("y"), lax.axis_index("c")
        slot = lambda px, py, pc: 4 * px + 2 * py + pc
        me, sibling = (mx, my, mc), (mx, my, 1 - mc)
        chips = [(1 - mx, my), (mx, 1 - my), (1 - mx, 1 - my)]
        locals_, first = [], []
        for a in range(n):
            x_ref, o_ref = x_refs[a], o_refs[a]
            locals_.append(pltpu.make_async_copy(x_ref, o_ref.at[slot(*me)], local_sems.at[a]))
            first.append(_remote(x_ref, o_ref.at[slot(*me)], send_sems, recv_sems, (a, 0), sibling))
            for j, chip in enumerate(chips):
                first.append(_remote(x_ref, o_ref.at[slot(*me)], send_sems, recv_sems, (a, 1 + j), (*chip, mc)))
        return locals_, first, slot, me, sibling, chips, mc

    def start():
        locals_, first = plan()[:2]
        for cp in locals_ + first:
            cp.start()

    def finish():
        locals_, first, slot, me, sibling, chips, mc = plan()
        passed = []
        for j, chip in enumerate(chips):
            for a in range(n):
                landed = o_refs[a].at[slot(*chip, mc)]
                _remote(landed, landed, send_sems, recv_sems, (a, 1 + j), me).wait_recv()
                fwd = _remote(landed, landed, send_sems, recv_sems, (a, 4 + j), sibling)
                fwd.start()
                passed.append(fwd)
        for a in range(n):
            from_sib = o_refs[a].at[slot(*sibling)]
            _remote(from_sib, from_sib, send_sems, recv_sems, (a, 0), me).wait_recv()
            for j, chip in enumerate(chips):
                via_sib = o_refs[a].at[slot(*chip, 1 - mc)]
                _remote(via_sib, via_sib, send_sems, recv_sems, (a, 4 + j), me).wait_recv()
        for cp in first + passed:
            cp.wait_send()
        for cp in locals_:
            cp.wait()

    return start, finish


def _scatter_sibling(xs, *, name):
    n = len(xs)

    def body(*refs):
        x_refs, o_refs = refs[:n], refs[n:2 * n]
        send_sems, recv_sems, _ = refs[2 * n:]
        mx, my, mc = lax.axis_index("x"), lax.axis_index("y"), lax.axis_index("c")
        sibling = (mx, my, 1 - mc)
        sends = []
        for a in range(n):
            for ch in range(4):
                sends.append(_remote(x_refs[a].at[ch, 1 - mc], o_refs[a].at[ch], send_sems, recv_sems, (a, ch), sibling))
        for cp in sends:
            cp.start()
        for cp in sends:
            cp.wait_recv()
        for cp in sends:
            cp.wait_send()

    return _exchange_call(body, xs, [(4,) + x.shape[2:] for x in xs], 4, name)


def _scatter_chips(ys, *, name):
    n = len(ys)

    def body(*refs):
        start, finish = _chip_scatter_steps(refs[:n], refs[n:2 * n], *refs[2 * n:])
        start()
        finish()

    return _exchange_call(body, ys, _chip_scatter_shapes(ys), _CHIP_SCATTER_COPIES, name)


_CHIP_SCATTER_COPIES = 3


def _chip_scatter_shapes(ys):
    return [y.shape for y in ys]


def _chip_scatter_steps(y_refs, o_refs, send_sems, recv_sems, local_sems):
    n = len(y_refs)

    def plan():
        mx, my, mc = lax.axis_index("x"), lax.axis_index("y"), lax.axis_index("c")
        mine = 2 * mx + my
        chips = [(1 - mx, my), (mx, 1 - my), (1 - mx, 1 - my)]
        locals_, sends, recvs = [], [], []
        for a in range(n):
            locals_.append(pltpu.make_async_copy(y_refs[a].at[mine], o_refs[a].at[mine], local_sems.at[a]))
            for j, (px, py) in enumerate(chips):
                theirs = 2 * px + py
                sends.append(_remote(y_refs[a].at[theirs], o_refs[a].at[mine], send_sems, recv_sems, (a, j), (px, py, mc)))
                recvs.append(_remote(y_refs[a].at[theirs], o_refs[a].at[theirs], send_sems, recv_sems, (a, j), (px, py, mc)))
        return locals_, sends, recvs

    def start():
        locals_, sends, _ = plan()
        for cp in locals_ + sends:
            cp.start()

    def finish():
        locals_, sends, recvs = plan()
        for cp in recvs:
            cp.wait_recv()
        for cp in sends:
            cp.wait_send()
        for cp in locals_:
            cp.wait()

    return start, finish


def _pair_add(x, r, *, name):
    _, _, rows, c = x.shape
    tr = _tile(rows, 512, 16)

    def body(core_ref, x_ref, r_ref, o_ref):
        o_ref[...] = (x_ref[...].astype(F32) + r_ref[...].astype(F32)).astype(o_ref.dtype)

    core = lax.axis_index("c").astype(jnp.int32).reshape(1)
    return pl.pallas_call(
        body, name=name,
        grid_spec=pltpu.PrefetchScalarGridSpec(
            num_scalar_prefetch=1, grid=(4, rows // tr),
            in_specs=[pl.BlockSpec((None, None, tr, c), lambda ch, i, core: (ch, core[0], i, 0)),
                      pl.BlockSpec((None, tr, c), lambda ch, i, core: (ch, i, 0))],
            out_specs=pl.BlockSpec((None, tr, c), lambda ch, i, core: (ch, i, 0))),
        out_shape=jax.ShapeDtypeStruct((4, rows, c), x.dtype), compiler_params=_cparams(2),
    )(core, x, r)


def _adamw_reduce(parts, w, m, v, *, name):
    r, c = w.shape
    n_parts = parts.shape[0]
    tr = _tile(r, 256, 16)
    bc1 = 1.0 - ADAM_B1 ** ADAM_STEP
    bc2 = 1.0 - ADAM_B2 ** ADAM_STEP

    def body(p_ref, w_ref, m_ref, v_ref, g_ref, d_ref, nm_ref, nv_ref):
        g = p_ref[0].astype(F32)
        for q in range(1, n_parts):
            g = g + p_ref[q].astype(F32)
        nm = ADAM_B1 * m_ref[...] + (1.0 - ADAM_B1) * g
        nv = ADAM_B2 * v_ref[...] + (1.0 - ADAM_B2) * (g * g)
        g_ref[...] = g
        nm_ref[...] = nm
        nv_ref[...] = nv
        d_ref[...] = -ADAM_LR * ((nm / bc1) / (jnp.sqrt(nv / bc2) + ADAM_EPS) + ADAM_WD * w_ref[...])

    blk = pl.BlockSpec((tr, c), lambda i: (i, 0))
    return pl.pallas_call(
        body, name=name, grid=(r // tr,),
        in_specs=[pl.BlockSpec((n_parts, tr, c), lambda i: (0, i, 0)), blk, blk, blk], out_specs=[blk] * 4,
        out_shape=[jax.ShapeDtypeStruct((r, c), F32)] * 4, compiler_params=_cparams(1),
    )(parts, w, m, v)


def _col_full(g):
    return g.transpose(1, 2, 0, 3).reshape(g.shape[1], g.shape[2], -1)


def _col_parts(f):
    n, k, c8 = f.shape
    return f.reshape(n, k, N_DEV, c8 // N_DEV).transpose(2, 0, 1, 3)


def _row_full(g):
    return g.transpose(1, 0, 2, 3).reshape(g.shape[1], -1, g.shape[3])


def _row_parts(f):
    n, r8, c = f.shape
    return f.reshape(n, N_DEV, r8 // N_DEV, c).transpose(1, 0, 2, 3)


def _gu_full(g):
    n, d, c = g.shape[1:]
    return g.reshape(2, 4, n, d, c).transpose(2, 0, 3, 1, 4).reshape(n, 2, d, 4 * c)


def _gu_parts(f):
    n, _, d, c4 = f.shape
    return f.reshape(n, 2, d, 4, c4 // 4).transpose(1, 3, 0, 2, 4).reshape(N_DEV, n, d, c4 // 4)


def _ssd_in_full(g):
    return jnp.pad(_col_full(g), ((0, 0), (0, 0), (0, SSD_IN_PAD - SSD_IN_DIM)))


_MATMUL_WEIGHTS = (
    ("ffn1_w_gu", _gu_full, _gu_parts), ("ffn1_w_down", _row_full, _row_parts),
    ("ffn2_w_gu", _gu_full, _gu_parts), ("ffn2_w_down", _row_full, _row_parts),
    ("sb_w_qkv", _col_full, _col_parts), ("sb_w_o", _row_full, _row_parts),
    ("ssd_w_in", _ssd_in_full, _col_parts), ("ssd_w_out", _row_full, _row_parts),
    ("sc_w_in", _col_full, _col_parts), ("sc_w_out", _row_full, _row_parts),
)
_FIRST_WEIGHTS = ("ffn1_w_gu", "ffn1_w_down", "sb_w_qkv", "sb_w_o")
_CONV_WEIGHTS = ("ssd_conv_w", "sc_conv_w")
_REPLICATED = ("ffn1_norm", "mix_norm", "ffn2_norm", "final_norm", "ssd_conv_b", "ssd_norm",
               "ssd_dt_bias", "ssd_a_log", "ssd_d")
_ORDER = ("ffn1_norm", "ffn1_w_gu", "ffn1_w_down", "mix_norm", "ffn2_norm", "ffn2_w_gu", "ffn2_w_down",
          "sb_w_qkv", "sb_w_o", "ssd_w_in", "ssd_conv_w", "ssd_conv_b", "ssd_dt_bias", "ssd_a_log", "ssd_d",
          "ssd_norm", "ssd_w_out", "sc_w_in", "sc_conv_w", "sc_w_out", "final_norm")
_LANES = 1024


def _rows_of(a):
    flat = a.reshape(-1)
    pad = -flat.shape[0] % _LANES
    return jnp.pad(flat, (0, pad)).reshape(-1, _LANES)


def _pack_rows(arrays, mult):
    rows = [_rows_of(a) for a in arrays]
    packed = jnp.concatenate(rows, axis=0)
    pad = -packed.shape[0] % mult
    return jnp.pad(packed, ((0, pad), (0, 0))), [r.shape[0] for r in rows]


def _unpack_rows(packed, counts, shapes, lead=()):
    out, off = [], 0
    for n, shp in zip(counts, shapes):
        size = math.prod(shp)
        seg = packed[..., off:off + n, :].reshape(lead + (n * _LANES,))[..., :size]
        out.append(seg.reshape(lead + tuple(shp)))
        off += n
    return out


def kernel(x, ffn1_norm, ffn1_w_gu, ffn1_w_down, mix_norm, ffn2_norm, ffn2_w_gu, ffn2_w_down, sb_w_qkv, sb_w_o, ssd_w_in, ssd_conv_w, ssd_conv_b, ssd_dt_bias, ssd_a_log, ssd_d, ssd_norm, ssd_w_out, sc_w_in, sc_conv_w, sc_w_out, final_norm, loss_target, m_ffn1_norm, m_ffn1_w_gu, m_ffn1_w_down, m_mix_norm, m_ffn2_norm, m_ffn2_w_gu, m_ffn2_w_down, m_sb_w_qkv, m_sb_w_o, m_ssd_w_in, m_ssd_conv_w, m_ssd_conv_b, m_ssd_dt_bias, m_ssd_a_log, m_ssd_d, m_ssd_norm, m_ssd_w_out, m_sc_w_in, m_sc_conv_w, m_sc_w_out, m_final_norm, v_ffn1_norm, v_ffn1_w_gu, v_ffn1_w_down, v_mix_norm, v_ffn2_norm, v_ffn2_w_gu, v_ffn2_w_down, v_sb_w_qkv, v_sb_w_o, v_ssd_w_in, v_ssd_conv_w, v_ssd_conv_b, v_ssd_dt_bias, v_ssd_a_log, v_ssd_d, v_ssd_norm, v_ssd_w_out, v_sc_w_in, v_sc_conv_w, v_sc_w_out, v_final_norm):
    w = dict(ffn1_norm=ffn1_norm, ffn1_w_gu=ffn1_w_gu, ffn1_w_down=ffn1_w_down, mix_norm=mix_norm, ffn2_norm=ffn2_norm, ffn2_w_gu=ffn2_w_gu, ffn2_w_down=ffn2_w_down, sb_w_qkv=sb_w_qkv, sb_w_o=sb_w_o, ssd_w_in=ssd_w_in, ssd_conv_w=ssd_conv_w, ssd_conv_b=ssd_conv_b, ssd_dt_bias=ssd_dt_bias, ssd_a_log=ssd_a_log, ssd_d=ssd_d, ssd_norm=ssd_norm, ssd_w_out=ssd_w_out, sc_w_in=sc_w_in, sc_conv_w=sc_conv_w, sc_w_out=sc_w_out, final_norm=final_norm)
    mom = dict(ffn1_norm=m_ffn1_norm, ffn1_w_gu=m_ffn1_w_gu, ffn1_w_down=m_ffn1_w_down, mix_norm=m_mix_norm, ffn2_norm=m_ffn2_norm, ffn2_w_gu=m_ffn2_w_gu, ffn2_w_down=m_ffn2_w_down, sb_w_qkv=m_sb_w_qkv, sb_w_o=m_sb_w_o, ssd_w_in=m_ssd_w_in, ssd_conv_w=m_ssd_conv_w, ssd_conv_b=m_ssd_conv_b, ssd_dt_bias=m_ssd_dt_bias, ssd_a_log=m_ssd_a_log, ssd_d=m_ssd_d, ssd_norm=m_ssd_norm, ssd_w_out=m_ssd_w_out, sc_w_in=m_sc_w_in, sc_conv_w=m_sc_conv_w, sc_w_out=m_sc_w_out, final_norm=m_final_norm)
    var = dict(ffn1_norm=v_ffn1_norm, ffn1_w_gu=v_ffn1_w_gu, ffn1_w_down=v_ffn1_w_down, mix_norm=v_mix_norm, ffn2_norm=v_ffn2_norm, ffn2_w_gu=v_ffn2_w_gu, ffn2_w_down=v_ffn2_w_down, sb_w_qkv=v_sb_w_qkv, sb_w_o=v_sb_w_o, ssd_w_in=v_ssd_w_in, ssd_conv_w=v_ssd_conv_w, ssd_conv_b=v_ssd_conv_b, ssd_dt_bias=v_ssd_dt_bias, ssd_a_log=v_ssd_a_log, ssd_d=v_ssd_d, ssd_norm=v_ssd_norm, ssd_w_out=v_ssd_w_out, sc_w_in=v_sc_w_in, sc_conv_w=v_sc_conv_w, sc_w_out=v_sc_w_out, final_norm=v_final_norm)
    me = 4 * lax.axis_index("x") + 2 * lax.axis_index("y") + lax.axis_index("c")
    big = [n for n, _, _ in _MATMUL_WEIGHTS]
    two_d = lambda a: a.reshape(-1, a.shape[-1])

    to_full = {n: f for n, f, _ in _MATMUL_WEIGHTS}
    to_parts = {n: f for n, _, f in _MATMUL_WEIGHTS}
    first = [(n, 0) for n in _FIRST_WEIGHTS]
    later = [(n, i) for n in big for i in range(1 if n in _FIRST_WEIGHTS else 0, w[n].shape[0])]

    def shards(group):
        return [two_d(w[n][i].astype(BF16)) for n, i in group]

    def layers(group, gathered):
        out = {}
        for (n, i), g in zip(group, gathered):
            out.setdefault(n, []).append(to_full[n](g.reshape((N_DEV, 1) + w[n].shape[1:]))[0])
        return out

    gathered = _gather(shards(first) + [two_d(w[n]) for n in _CONV_WEIGHTS], name="gather_first")
    full = dict(w)
    full.update(layers(first, gathered))
    for n, g in zip(_CONV_WEIGHTS, gathered[len(first):]):
        full[n] = _col_full(g.reshape((N_DEV,) + w[n].shape))

    def host_of(n, i):
        if n.startswith("ffn2") and i == DEPTH - 1:
            return DEPTH - 1
        if (n.startswith("ffn1") and i == DEPTH - 1) or (n.startswith("sb_") and i == 1):
            return 1
        return 0

    gather_later = {}
    for host in (0, 1, DEPTH - 1):
        group = [(n, i) for n, i in later if host_of(n, i) == host]

        def merge(wd, gathered_group, group=group):
            wd = dict(wd)
            for n, ls in layers(group, gathered_group).items():
                have = wd[n] if isinstance(wd[n], list) else []
                wd[n] = have + ls
            return wd

        xs = shards(group)
        gather_later[host] = (_Hosted(_gather_steps, xs, _gather_shapes(xs), _GATHER_COPIES), merge)

    def chip_sums(group, grads, tag):
        parts = []
        for n, i in group:
            p8 = to_parts[n](grads[n][i][None].astype(BF16))
            parts.append(p8.reshape(4, 2, -1, p8.shape[-1]))
        from_sibling = _scatter_sibling(parts, name=f"scatter_sibling_{tag}")
        return [_pair_add(p, r, name=f"pair_add_{tag}_{n}{i}") for (n, i), p, r in zip(group, parts, from_sibling)]

    def scatter_early(grads):
        ys = chip_sums(later, grads, "later")
        return _Hosted(_chip_scatter_steps, ys, _chip_scatter_shapes(ys), _CHIP_SCATTER_COPIES)

    loss_part, dx, grads, recv_later = _local_step(x[0], loss_target[0], full, gather_later, scatter_early)
    loss = lax.psum(loss_part[0, 0], ("x", "y", "c"))

    recv_first = _scatter_chips(chip_sums(first, grads, "first"), name="scatter_chips_first")
    contrib = {n: [r] for (n, _), r in zip(first, recv_first)}
    for (n, _), r in zip(later, recv_later):
        contrib.setdefault(n, []).append(r)
    out_g, out_d, out_m, out_v = {}, {}, {}, {}

    def update(n, parts):
        res = _adamw_reduce(parts, two_d(w[n]), two_d(mom[n]), two_d(var[n]), name=f"adamw_{n}")
        out_g[n], out_d[n], out_m[n], out_v[n] = (r.reshape(w[n].shape) for r in res)

    for n in big:
        update(n, contrib[n][0] if len(contrib[n]) == 1 else jnp.concatenate(contrib[n], axis=1))

    small = list(_REPLICATED) + list(_CONV_WEIGHTS)
    small_shapes = [grads[n].shape for n in small]
    spacked, scounts = _pack_rows([grads[n].astype(F32) for n in small], 8)
    sg = _unpack_rows(_gather([spacked], name="gather_small_grads")[0], scounts, small_shapes, (N_DEV,))
    sg = dict(zip(small, sg))
    rep_w, rcounts = _pack_rows([w[n] for n in _REPLICATED], 8)
    rep_m, _ = _pack_rows([mom[n] for n in _REPLICATED], 8)
    rep_v, _ = _pack_rows([var[n] for n in _REPLICATED], 8)
    rep_p = jnp.concatenate([_rows_of(sg[n].reshape(N_DEV, -1)[q]) for q in range(N_DEV) for n in _REPLICATED], axis=0)
    rep_p = rep_p.reshape(N_DEV, -1, _LANES)
    rep_p = jnp.pad(rep_p, ((0, 0), (0, rep_w.shape[0] - rep_p.shape[1]), (0, 0)))
    res = _adamw_reduce(rep_p, rep_w, rep_m, rep_v, name="adamw_replicated")
    rep_shapes = [w[n].shape for n in _REPLICATED]
    for tgt, r in zip((out_g, out_d, out_m, out_v), res):
        for n, a in zip(_REPLICATED, _unpack_rows(r, rcounts, rep_shapes)):
            tgt[n] = a
    for n in _CONV_WEIGHTS:
        c = w[n].shape[-1]
        mine = lax.dynamic_slice_in_dim(sg[n], me * c, c, axis=sg[n].ndim - 1)
        update(n, mine.reshape(N_DEV, -1, c))

    return (loss, dx[None], *[out_g[n] for n in _ORDER], *[out_d[n] for n in _ORDER],
            *[out_m[n] for n in _ORDER], *[out_v[n] for n in _ORDER])
```

```python
import functools
import math

import jax
import jax.numpy as jnp
from jax import lax
from jax.experimental import pallas as pl
from jax.experimental.pallas import tpu as pltpu

F32 = jnp.float32
BF16 = jnp.bfloat16

D_MODEL = 1024
D_FF = 2816
DEPTH = 4
N_DEV = 8
SB_HEADS = 16
SB_HEAD_DIM = 64
SB_TILE = 256
SB_HEADS_PER_STEP = 2
SB_FWD_GROUPS = (2, 1)
SB_BWD_GROUPS = (4, 2, 1)
SSD_HEADS = 32
SSD_HEAD_DIM = 64
SSD_GROUPS = 8
SSD_HPG = 4
SSD_STATE = 128
SSD_CHUNK = 128
SSD_GROUPS_PER_STEP = 8
SSD_D_INNER = 2048
SSD_CONV_DIM = 4096
SSD_IN_DIM = 6176
SSD_IN_PAD = 6272
SSD_NORM_GROUP = 256
RMS_EPS = 1e-6
ADAM_LR = 0.001
ADAM_B1 = 0.9
ADAM_B2 = 0.999
ADAM_EPS = 1e-08
ADAM_WD = 0.01
ADAM_STEP = 10
VMEM_LIMIT = 60 * 1024 * 1024

NT = (((1,), (1,)), ((), ()))
TN = (((0,), (0,)), ((), ()))
NN = (((1,), (0,)), ((), ()))


def _cparams(n_axes):
    return pltpu.CompilerParams(dimension_semantics=("arbitrary",) * n_axes, vmem_limit_bytes=VMEM_LIMIT)


def _tile(n, want, mult=8):
    if n <= want:
        return n
    for t in range(want, 0, -1):
        if n % t == 0 and t % mult == 0:
            return t
    return n


def _sigmoid(x):
    return 1.0 / (1.0 + jnp.exp(-x))


def _dot(a, b, dn=NN):
    return lax.dot_general(a, b, dn, preferred_element_type=F32)


def _split3(x):
    x1 = x.astype(BF16)
    r1 = x - x1.astype(F32)
    x2 = r1.astype(BF16)
    x3 = (r1 - x2.astype(F32)).astype(BF16)
    return x1, x2, x3


def _dot_exact(x, t):
    x1, x2, x3 = _split3(x)
    return _dot(x1, t) + _dot(x2, t) + _dot(x3, t)


ROW_SUM_LANES = 1


def _cumsum_operand(tri):
    return jnp.concatenate([tri, tri], axis=0)


def _cumsum_rowsum(x, tri2):
    x1 = x.astype(BF16)
    x2 = (x - x1.astype(F32)).astype(BF16)
    return _dot(jnp.concatenate([x1, x2], axis=1), tri2), jnp.sum(x, axis=1, keepdims=True)


def _across_lanes(c, t):
    return c


def _mm(a, b, *, name, ta=False, tb=False, sa=False, sb=False, so=False, tm=512, tn=1024, tk=1024,
        out_dtype=F32, epilogue=None, extras=(), outs=None, pair=None, col_chunk=None):
    s_n = pair or (a.shape[0] if sa else (b.shape[0] if sb else 1))
    ash, bsh = a.shape[-2:], b.shape[-2:]
    m, k = (ash[1], ash[0]) if ta else ash
    n = bsh[0] if tb else bsh[1]
    tm, tn, tk = _tile(m, tm), _tile(n, tn, 128), _tile(k, tk, 128)
    nk = k // tk
    if outs is None:
        outs = [(out_dtype, "stile" if so else "tile")]
    if epilogue is None:
        epilogue = lambda acc: (acc,)

    if ta:
        a_blk, a_idx = (tk, tm), (lambda j, i, kk: (kk, i))
    else:
        a_blk, a_idx = (tm, tk), (lambda j, i, kk: (i, kk))
    if tb:
        b_blk, b_idx = (tn, tk), (lambda j, i, kk: (j, kk))
    else:
        b_blk, b_idx = (tk, tn), (lambda j, i, kk: (kk, j))

    def lead(blk, idx, has_s):
        if not has_s:
            return pl.BlockSpec(blk, idx)
        return pl.BlockSpec((s_n,) + blk, lambda j, i, kk: (0,) + idx(j, i, kk))

    kinds = {
        "tile": lambda: pl.BlockSpec((tm, tn), lambda j, i, kk: (i, j)),
        "stile": lambda: pl.BlockSpec((s_n, tm, tn), lambda j, i, kk: (0, i, j)),
        "row": lambda: pl.BlockSpec((1, tn), lambda j, i, kk: (0, j)),
        "colsum": lambda: pl.BlockSpec((1, tn), lambda j, i, kk: (0, j)),
    }
    shapes = {"tile": (m, n), "stile": (s_n, m, n), "colsum": (1, n)}
    in_specs = [lead(a_blk, a_idx, sa), lead(b_blk, b_idx, sb)] + [kinds[kd]() for _, kd in extras]
    out_specs = [kinds[kd]() for _, kd in outs]
    out_shape = [jax.ShapeDtypeStruct(shapes[kd], dt) for dt, kd in outs]
    n_ex, n_out = len(extras), len(outs)
    dn = ((((0,) if ta else (1,)), ((1,) if tb else (0,))), ((), ()))
    acc_shape = (s_n, tm, tn) if so else (tm, tn)

    def body(*refs):
        a_ref, b_ref = refs[0], refs[1]
        ex_refs = refs[2:2 + n_ex]
        o_refs = refs[2 + n_ex:2 + n_ex + n_out]
        i = pl.program_id(1)
        kk = pl.program_id(2)

        def products():
            for s in range(s_n if (sa or sb) else 1):
                av = (a_ref[s] if sa else a_ref[...]).astype(BF16)
                bv = (b_ref[s] if sb else b_ref[...]).astype(BF16)
                yield s, lax.dot_general(av, bv, dn, preferred_element_type=F32)

        def finish(accv):
            vals = epilogue(accv, *[r[...] for r in ex_refs])
            for (dt, kd), o_ref, val in zip(outs, o_refs, vals):
                if kd == "colsum":
                    _accumulate(o_ref, val, i == 0)
                elif kd == "stile":
                    for s in range(s_n):
                        o_ref[s] = val[s].astype(dt)
                else:
                    o_ref[...] = val.astype(dt)

        if nk == 1 and col_chunk:
            bounds = [(c0, min(col_chunk, tn - c0)) for c0 in range(0, tn, col_chunk)]
            n_s = s_n if (sa or sb) else 1
            a_vals = [(a_ref[s] if sa else a_ref[...]).astype(BF16) for s in range(n_s if sa else 1)]
            accs = []
            for c0, cw in bounds:
                ds = []
                for s in range(n_s):
                    idx = ((s,) if sb else ()) + ((pl.ds(c0, cw), slice(None)) if tb else (slice(None), pl.ds(c0, cw)))
                    ds.append(lax.dot_general(a_vals[s if sa else 0], b_ref[idx].astype(BF16), dn,
                                              preferred_element_type=F32))
                accs.append(tuple(ds) if so else functools.reduce(jnp.add, ds))
            for (c0, cw), accv in zip(bounds, accs):
                cols = pl.ds(c0, cw)
                exv = [r[:, :, cols] if kd == "stile" else r[:, cols] for r, (_, kd) in zip(ex_refs, extras)]
                vals = epilogue(accv, *exv)
                for (dt, kd), o_ref, val in zip(outs, o_refs, vals):
                    if kd == "stile":
                        for s in range(s_n):
                            o_ref[s, :, cols] = val[s].astype(dt)
                    else:
                        o_ref[:, cols] = val.astype(dt)
            return

        if nk == 1:
            ds = [d for _, d in products()]
            finish(tuple(ds) if so else functools.reduce(jnp.add, ds))
            return

        acc = refs[-1]

        @pl.when(kk == 0)
        def _():
            acc[...] = jnp.zeros_like(acc)

        for s, d in products():
            if so:
                acc[s] += d
            else:
                acc[...] += d

        @pl.when(kk == nk - 1)
        def _():
            finish(tuple(acc[s] for s in range(s_n)) if so else acc[...])

    res = pl.pallas_call(
        body, name=name, grid=(n // tn, m // tm, nk),
        in_specs=in_specs, out_specs=out_specs, out_shape=out_shape,
        scratch_shapes=[pltpu.VMEM(acc_shape, F32)] if nk > 1 else [], compiler_params=_cparams(3),
    )(a, b, *[e for e, _ in extras])
    return res[0] if len(res) == 1 else res


def _accumulate(o_ref, val, first):
    @pl.when(first)
    def _():
        o_ref[...] = val

    @pl.when(jnp.logical_not(first))
    def _():
        o_ref[...] += val


def _rmsnorm(x, g, *, name):
    l, d = x.shape
    tm = _tile(l, 512)

    def body(x_ref, g_ref, o_ref):
        xv = x_ref[...]
        r = lax.rsqrt(jnp.mean(xv * xv, axis=1, keepdims=True) + RMS_EPS)
        o_ref[...] = (xv * r * g_ref[...]).astype(BF16)

    return pl.pallas_call(
        body, name=name, grid=(l // tm,),
        in_specs=[pl.BlockSpec((tm, d), lambda i: (i, 0)), pl.BlockSpec((1, d), lambda i: (0, 0))],
        out_specs=pl.BlockSpec((tm, d), lambda i: (i, 0)),
        out_shape=jax.ShapeDtypeStruct((l, d), BF16), compiler_params=_cparams(1),
    )(x, g)


def _norm_bwd_epilogue(dh, x, g, dres):
    r = lax.rsqrt(jnp.mean(x * x, axis=1, keepdims=True) + RMS_EPS)
    xh = x * r
    dg = jnp.sum(dh * xh, axis=0, keepdims=True)
    dxh = dh * g
    dx = r * (dxh - xh * jnp.mean(dxh * xh, axis=1, keepdims=True))
    return dres + dx, dg


def _final_loss(x, g, tgt, *, name):
    l, d = x.shape
    tm = _tile(l, 512)

    def body(x_ref, g_ref, t_ref, loss_ref, dx_ref, dg_ref):
        i = pl.program_id(0)
        xv, gv = x_ref[...], g_ref[...]
        r = lax.rsqrt(jnp.mean(xv * xv, axis=1, keepdims=True) + RMS_EPS)
        xh = xv * r
        e = xh * gv - t_ref[...]
        part = 0.5 * jnp.sum(jnp.mean(e * e, axis=1, keepdims=True), axis=0, keepdims=True)
        dy = e * (1.0 / d)
        dg = jnp.sum(dy * xh, axis=0, keepdims=True)
        dxh = dy * gv
        dx_ref[...] = r * (dxh - xh * jnp.mean(dxh * xh, axis=1, keepdims=True))
        _accumulate(dg_ref, dg, i == 0)
        _accumulate(loss_ref, jnp.broadcast_to(part, (1, 128)), i == 0)

    return pl.pallas_call(
        body, name=name, grid=(l // tm,),
        in_specs=[pl.BlockSpec((tm, d), lambda i: (i, 0)), pl.BlockSpec((1, d), lambda i: (0, 0)),
                  pl.BlockSpec((tm, d), lambda i: (i, 0))],
        out_specs=[pl.BlockSpec((1, 128), lambda i: (0, 0)), pl.BlockSpec((tm, d), lambda i: (i, 0)),
                   pl.BlockSpec((1, d), lambda i: (0, 0))],
        out_shape=[jax.ShapeDtypeStruct((1, 128), F32), jax.ShapeDtypeStruct((l, d), F32),
                   jax.ShapeDtypeStruct((1, d), F32)],
        compiler_params=_cparams(1),
    )(x, g, tgt)


def _ffn_fwd(x, g, wgu, wd, tag):
    h = _rmsnorm(x, g, name=f"{tag}_norm")

    def act(acc):
        gate, up = acc
        return acc, gate * _sigmoid(gate) * up

    gu, a = _mm(h, wgu, sb=True, so=True, tm=512, tn=1408, tk=1024, name=f"{tag}_up",
                outs=[(BF16, "stile"), (BF16, "tile")], epilogue=act)
    xo = _mm(a, wd, tm=512, tn=1024, tk=2816, name=f"{tag}_down", extras=[(x, "tile")],
             epilogue=lambda acc, xt: (xt + 0.5 * acc,))
    return xo, (x, h, gu, a)


def _ffn_bwd(dout, saved, g, wgu, wd, tag):
    x, h, gu, a = saved

    def act_bwd(acc, guv):
        da = 0.5 * acc
        gate, up = guv[0].astype(F32), guv[1].astype(F32)
        s = _sigmoid(gate)
        return ((da * up * s * (1.0 + gate * (1.0 - s)), da * gate * s),)

    dgu = _mm(dout, wd, tb=True, pair=2, tm=512, tn=1408, tk=1024, col_chunk=384, name=f"{tag}_dact", extras=[(gu, "stile")],
              outs=[(BF16, "stile")], epilogue=act_bwd)
    dwd = _mm(a, dout, ta=True, tm=1408, tn=1024, tk=2048, name=f"{tag}_dwd", out_dtype=BF16,
              epilogue=lambda acc: (0.5 * acc,))
    dwgu = _mm(h, dgu, ta=True, sb=True, so=True, tm=512, tn=1408, tk=2048, name=f"{tag}_dwgu", out_dtype=BF16)
    dx, dg = _mm(dgu, wgu, tb=True, sa=True, sb=True, tm=512, tn=1024, tk=2816, name=f"{tag}_dx",
                 extras=[(x, "tile"), (g, "row"), (dout, "tile")], outs=[(F32, "tile"), (F32, "colsum")],
                 epilogue=_norm_bwd_epilogue)
    return dx, dg, dwgu, dwd


def _sb_plan(n, sizes):
    digits = [n // sizes[0]] + [(n // s) % 2 for s in sizes[1:]]
    plan, none_smaller = [], 1
    for size, d in reversed(list(zip(sizes, digits))):
        has = jnp.minimum(d, 1)
        with_diag = none_smaller * has
        plan.append((size, True, with_diag))
        if size > 1:
            plan.append((size, False, d - with_diag))
        none_smaller = none_smaller * (1 - has)
    return plan


def _sb_sweep(plan, start, step, fn, carry):
    pos = start
    for size, with_diag, trips in plan:
        diag = 0 if step < 0 else size - 1

        def trip(it, cr, size=size, with_diag=with_diag, pos=pos, diag=diag):
            base = pos + step * size * it
            return fn([base + step * b for b in range(size)], cr, [with_diag and b == diag for b in range(size)])

        carry = lax.fori_loop(0, trips, trip, carry)
        pos = pos + step * size * trips
    return carry


def _sb_logs(z):
    lb = jnp.minimum(z, 0.0) - jnp.log(1.0 + jnp.exp(-jnp.abs(z)))
    return lb, lb - z


class _Hosted:
    def __init__(self, steps, xs, out_shapes, copies):
        self.steps, self.xs, self.n, self.copies = steps, list(xs), len(xs), copies
        self.out_shape = [jax.ShapeDtypeStruct(s, x.dtype) for s, x in zip(out_shapes, xs)]
        self.specs = [_HBM] * self.n
        self.sems = [pltpu.SemaphoreType.DMA((self.n, copies)), pltpu.SemaphoreType.DMA((self.n, copies)),
                     pltpu.SemaphoreType.DMA((self.n,))]

    def run(self, x_refs, o_refs, sems, grid):
        ids = [pl.program_id(a) for a in range(len(grid))]
        first = functools.reduce(jnp.logical_and, [p == 0 for p in ids])
        last = functools.reduce(jnp.logical_and, [p == g - 1 for p, g in zip(ids, grid)])
        start, finish = self.steps(x_refs, o_refs, *sems)
        pl.when(first)(start)
        return lambda: pl.when(last)(finish)


def _head_masks(hs):
    lane = lax.broadcasted_iota(jnp.int32, (1, hs * SB_HEAD_DIM), 1)
    return [jnp.logical_and(lane >= hh * SB_HEAD_DIM, lane < (hh + 1) * SB_HEAD_DIM) for hh in range(hs)]


def _sb_fwd(qkv, *, name, hosted=None):
    l = qkv.shape[0]
    d_model = qkv.shape[1] // 3
    dh = SB_HEAD_DIM
    t = _tile(l, SB_TILE)
    hs = 4 * SB_HEADS_PER_STEP
    w = hs * dh
    n_grp = d_model // w
    scale = dh ** -0.5
    grid = (n_grp, l // t)
    nh = hosted.n if hosted else 0

    def body(q_ref, k_ref, v_ref, *rest):
        o_ref = rest[nh]
        at_end = hosted.run(rest[:nh], rest[nh + 1:2 * nh + 1], rest[2 * nh + 1:], grid) if hosted else None
        i = pl.program_id(1)
        heads = _head_masks(hs)
        q_all = (q_ref[...].astype(F32) * scale).astype(BF16)
        qs = [jnp.where(heads[hh], q_all, jnp.zeros_like(q_all)) for hh in range(hs)]
        row = lax.broadcasted_iota(jnp.int32, (t, t), 0)
        col = lax.broadcasted_iota(jnp.int32, (t, t), 1)
        strict = col < row
        tri = _cumsum_operand(strict.astype(BF16))

        def block(jbs, carry, masks):
            sls = [pl.ds(pl.multiple_of(jb * t, t), t) for jb in jbs]
            chains = [(hh, b) for b in range(len(jbs)) for hh in range(hs)]
            ks = [k_ref[sl, :] for sl in sls]
            zs = {(hh, b): _dot(qs[hh], ks[b], NT) for hh, b in chains}
            lbs, tails, sums = {}, {}, {}
            for hh, b in chains:
                lb, lk = _sb_logs(zs[hh, b])
                if masks[b]:
                    lk = jnp.where(strict, lk, 0.0)
                lbs[hh, b] = lb
                tails[hh, b], sums[hh, b] = _cumsum_rowsum(lk, tri)
            cs, o = list(carry[0]), carry[1]
            for b in range(len(jbs)):
                atts = []
                for hh in range(hs):
                    att = jnp.exp(lbs[hh, b] + tails[hh, b] + _across_lanes(cs[hh], t))
                    if masks[b]:
                        att = jnp.where(strict, att, 0.0)
                    atts.append(att.astype(BF16))
                    cs[hh] = cs[hh] + sums[hh, b]
                vb = v_ref[sls[b], :]
                v_heads = jnp.concatenate([jnp.where(heads[hh], vb, jnp.zeros_like(vb)) for hh in range(hs)], axis=0)
                o = o + _dot(jnp.concatenate(atts, axis=1), v_heads)
            return tuple(cs), o

        carry = (tuple(jnp.zeros((t, ROW_SUM_LANES), F32) for _ in range(hs)), jnp.zeros((t, w), F32))
        carry = _sb_sweep(_sb_plan(i + 1, SB_FWD_GROUPS), i, -1, block, carry)
        o_ref[...] = carry[1].astype(o_ref.dtype)
        if hosted:
            at_end()

    blocks = d_model // w
    res = pl.pallas_call(
        body, name=name, grid=grid,
        in_specs=[pl.BlockSpec((t, w), lambda g, i: (i, g)),
                  pl.BlockSpec((l, w), lambda g, i: (0, blocks + g), pipeline_mode=pl.Buffered(1)),
                  pl.BlockSpec((l, w), lambda g, i: (0, 2 * blocks + g), pipeline_mode=pl.Buffered(1))]
        + (hosted.specs if hosted else []),
        out_specs=[pl.BlockSpec((t, w), lambda g, i: (i, g))] + (hosted.specs if hosted else []),
        out_shape=[jax.ShapeDtypeStruct((l, d_model), BF16)] + (hosted.out_shape if hosted else []),
        scratch_shapes=hosted.sems if hosted else [], compiler_params=_cparams(2),
    )(qkv, qkv, qkv, *(hosted.xs if hosted else []))
    return (res[0], res[1:]) if hosted else res[0]


def _sb_bwd(qkv, do, *, name, hosted=None):
    l = qkv.shape[0]
    d_model = qkv.shape[1] // 3
    dh = SB_HEAD_DIM
    t = _tile(l, SB_TILE)
    nq = l // t
    hs = 2 * SB_HEADS_PER_STEP
    w = hs * dh
    blocks = d_model // w
    scale = dh ** -0.5
    grid = (blocks, nq)
    nh = hosted.n if hosted else 0

    def body(q_ref, k_ref, v_ref, do_ref, *rest):
        dq_ref, dk_ref, dv_ref = rest[nh:nh + 3]
        e_scr, s_scr = rest[2 * nh + 3:2 * nh + 5]
        at_end = hosted.run(rest[:nh], rest[nh + 3:2 * nh + 3], rest[2 * nh + 5:], grid) if hosted else None
        i = pl.program_id(1)

        @pl.when(i == 0)
        def _():
            dk_ref[...] = jnp.zeros_like(dk_ref)
            dv_ref[...] = jnp.zeros_like(dv_ref)

        heads = _head_masks(hs)
        q_all = (q_ref[...].astype(F32) * scale).astype(BF16)
        do_all = do_ref[...]
        qs = [jnp.where(heads[hh], q_all, jnp.zeros_like(q_all)) for hh in range(hs)]
        dos = [jnp.where(heads[hh], do_all, jnp.zeros_like(do_all)) for hh in range(hs)]
        row = lax.broadcasted_iota(jnp.int32, (t, t), 0)
        col = lax.broadcasted_iota(jnp.int32, (t, t), 1)
        strict = col < row
        tri_suffix = _cumsum_operand(strict.astype(BF16))
        tri_prefix = _cumsum_operand((row < col).astype(BF16))

        def sweep1(jbs, cs, masks):
            sls = [pl.ds(pl.multiple_of(jb * t, t), t) for jb in jbs]
            chains = [(hh, b) for b in range(len(jbs)) for hh in range(hs)]
            zs = {(hh, b): _dot(qs[hh], k_ref[sls[b], :], NT) for hh, b in chains}
            datts = {(hh, b): _dot(dos[hh], v_ref[sls[b], :], NT) for hh, b in chains}
            lbs, tails, sums = {}, {}, {}
            for hh, b in chains:
                lb, lk = _sb_logs(zs[hh, b])
                if masks[b]:
                    lk = jnp.where(strict, lk, 0.0)
                lbs[hh, b] = lb
                tails[hh, b], sums[hh, b] = _cumsum_rowsum(lk, tri_suffix)
                s_scr[hh, jbs[b]] = jnp.exp(lb).astype(BF16)
            cs = list(cs)
            for hh, b in chains:
                att = jnp.exp(lbs[hh, b] + tails[hh, b] + _across_lanes(cs[hh], t))
                if masks[b]:
                    att = jnp.where(strict, att, 0.0)
                e_scr[hh, jbs[b]] = att * datts[hh, b]
                dv_ref[sls[b], :] += _dot(att.astype(BF16), dos[hh], TN)
                cs[hh] = cs[hh] + sums[hh, b]
            return tuple(cs)

        plan = _sb_plan(i + 1, SB_BWD_GROUPS)
        _sb_sweep(plan, i, -1, sweep1, tuple(jnp.zeros((t, ROW_SUM_LANES), F32) for _ in range(hs)))

        def sweep2(jbs, carry, masks):
            sls = [pl.ds(pl.multiple_of(jb * t, t), t) for jb in jbs]
            chains = [(hh, b) for b in range(len(jbs)) for hh in range(hs)]
            des = {(hh, b): e_scr[hh, jbs[b]] for hh, b in chains}
            pres = {(hh, b): _cumsum_rowsum(des[hh, b], tri_prefix) for hh, b in chains}
            carry = [list(c) for c in carry]
            for hh, b in chains:
                p, dq = carry[hh]
                de, sg = des[hh, b], s_scr[hh, jbs[b]].astype(F32)
                dlk = _across_lanes(p, t) + pres[hh, b][0]
                if masks[b]:
                    dlk = jnp.where(strict, dlk, 0.0)
                dz = (de - sg * (de + dlk)).astype(BF16)
                dk_ref[sls[b], :] += _dot(dz, qs[hh], TN)
                carry[hh] = [p + pres[hh, b][1], dq + _dot(dz, k_ref[sls[b], :])]
            return tuple(tuple(c) for c in carry)

        carry = tuple((jnp.zeros((t, ROW_SUM_LANES), F32), jnp.zeros((t, w), F32)) for _ in range(hs))
        carry = _sb_sweep(plan[::-1], 0, 1, sweep2, carry)
        dq = jnp.zeros((t, w), F32)
        for hh in range(hs):
            dq = jnp.where(heads[hh], carry[hh][1], dq)
        dq_ref[...] = (dq * scale).astype(dq_ref.dtype)
        if hosted:
            at_end()

    qspec = pl.BlockSpec((t, w), lambda g, i: (i, g))
    cols = lambda off: pl.BlockSpec((l, w), lambda g, i: (0, off + g), pipeline_mode=pl.Buffered(1))
    res = pl.pallas_call(
        body, name=name, grid=grid,
        in_specs=[qspec, cols(blocks), cols(2 * blocks), qspec] + (hosted.specs if hosted else []),
        out_specs=[qspec, cols(0), cols(0)] + (hosted.specs if hosted else []),
        out_shape=[jax.ShapeDtypeStruct((l, d_model), BF16), jax.ShapeDtypeStruct((l, d_model), F32),
                   jax.ShapeDtypeStruct((l, d_model), F32)] + (hosted.out_shape if hosted else []),
        scratch_shapes=[pltpu.VMEM((hs, nq, t, t), F32), pltpu.VMEM((hs, nq, t, t), BF16)]
        + (hosted.sems if hosted else []),
        compiler_params=_cparams(2),
    )(qkv, qkv, qkv, do, *(hosted.xs if hosted else []))
    return (res[0], res[1], res[2], res[3:]) if hosted else res


def _sb_layer_fwd(x, g, wqkv, wo, tag, hosted=None):
    h = _rmsnorm(x, g, name=f"{tag}_norm")
    qkv = _mm(h, wqkv, tm=1024, tn=1024, tk=1024, name=f"{tag}_qkv", out_dtype=BF16)
    o = _sb_fwd(qkv, name=f"{tag}_attn", hosted=hosted)
    carried = None
    if hosted:
        o, carried = o
    xo = _mm(o, wo, tm=1024, tn=1024, tk=1024, name=f"{tag}_out", extras=[(x, "tile")],
             epilogue=lambda acc, xt: (xt + acc,))
    return xo, (x, h, qkv, o), carried


def _sb_layer_bwd(dout, saved, g, wqkv, wo, tag, hosted=None):
    x, h, qkv, o = saved
    do = _mm(dout, wo, tb=True, tm=1024, tn=1024, tk=1024, name=f"{tag}_do", out_dtype=BF16)
    dwo = _mm(o, dout, ta=True, tm=1024, tn=1024, tk=2048, name=f"{tag}_dwo", out_dtype=BF16)
    res = _sb_bwd(qkv, do, name=f"{tag}_attn_bwd", hosted=hosted)
    dq, dk, dv = res[:3]
    carried = res[3] if hosted else None
    dqkv = jnp.concatenate([dq, dk.astype(BF16), dv.astype(BF16)], axis=1)
    dwqkv = _mm(h, dqkv, ta=True, tm=1024, tn=1024, tk=2048, name=f"{tag}_dwqkv", out_dtype=BF16)
    dx, dg = _mm(dqkv, wqkv, tb=True, tm=512, tn=1024, tk=3072, name=f"{tag}_dx",
                 extras=[(x, "tile"), (g, "row"), (dout, "tile")], outs=[(F32, "tile"), (F32, "colsum")],
                 epilogue=_norm_bwd_epilogue)
    return dx, dg, dwqkv, dwo, carried


def _shift_down(x, s, t_idx):
    return jnp.where(t_idx >= s, pltpu.roll(x, s, 0), 0.0)


def _shift_up(x, s, t_idx):
    n = x.shape[0]
    return jnp.where(t_idx < n - s, pltpu.roll(x, n - s, 0), 0.0)


def _sc_fwd(p, cw, *, name):
    l = p.shape[0]
    d = cw.shape[1]
    tc = 128
    nb = d // tc

    def body(b_ref, c_ref, h_ref, w_ref, o_ref):
        v = c_ref[...] * h_ref[...]
        t_idx = lax.broadcasted_iota(jnp.int32, v.shape, 0)
        u = v * w_ref[2:3, :] + _shift_down(v, 1, t_idx) * w_ref[1:2, :] + _shift_down(v, 2, t_idx) * w_ref[0:1, :]
        o_ref[...] = (b_ref[...] * u).astype(BF16)

    return pl.pallas_call(
        body, name=name, grid=(nb,),
        in_specs=[pl.BlockSpec((l, tc), lambda j: (0, j)), pl.BlockSpec((l, tc), lambda j: (0, nb + j)),
                  pl.BlockSpec((l, tc), lambda j: (0, 2 * nb + j)), pl.BlockSpec((3, tc), lambda j: (0, j))],
        out_specs=pl.BlockSpec((l, tc), lambda j: (0, j)),
        out_shape=jax.ShapeDtypeStruct((l, d), BF16), compiler_params=_cparams(1),
    )(p, p, p, cw)


def _sc_bwd(p, cw, dbu, *, name):
    l = p.shape[0]
    d = cw.shape[1]
    tc = 128
    nb = d // tc

    def body(b_ref, c_ref, h_ref, w_ref, g_ref, db_ref, dc_ref, dh_ref, dw_ref):
        cv, hv = c_ref[...], h_ref[...]
        v = cv * hv
        t_idx = lax.broadcasted_iota(jnp.int32, v.shape, 0)
        v1, v2 = _shift_down(v, 1, t_idx), _shift_down(v, 2, t_idx)
        u = v * w_ref[2:3, :] + v1 * w_ref[1:2, :] + v2 * w_ref[0:1, :]
        dbu_v = g_ref[...]
        db_ref[...] = (dbu_v * u).astype(BF16)
        du = dbu_v * b_ref[...]
        dv = du * w_ref[2:3, :] + _shift_up(du, 1, t_idx) * w_ref[1:2, :] + _shift_up(du, 2, t_idx) * w_ref[0:1, :]
        dc_ref[...] = (dv * hv).astype(BF16)
        dh_ref[...] = (dv * cv).astype(BF16)
        dw_ref[...] = jnp.zeros_like(dw_ref)
        dw_ref[0:1, :] = jnp.sum(du * v2, axis=0, keepdims=True)
        dw_ref[1:2, :] = jnp.sum(du * v1, axis=0, keepdims=True)
        dw_ref[2:3, :] = jnp.sum(du * v, axis=0, keepdims=True)

    col = lambda off: pl.BlockSpec((l, tc), lambda j: (0, off + j))
    return pl.pallas_call(
        body, name=name, grid=(nb,),
        in_specs=[col(0), col(nb), col(2 * nb), pl.BlockSpec((3, tc), lambda j: (0, j)), col(0)],
        out_specs=[col(0), col(0), col(0), pl.BlockSpec((8, tc), lambda j: (0, j))],
        out_shape=[jax.ShapeDtypeStruct((l, d), BF16)] * 3 + [jax.ShapeDtypeStruct((8, d), F32)],
        compiler_params=_cparams(1),
    )(p, p, p, cw, dbu)


def _sc_layer_fwd(x, g, win, cw, wout, tag):
    h = _rmsnorm(x, g, name=f"{tag}_norm")
    p = _mm(h, win, tm=1024, tn=1024, tk=1024, name=f"{tag}_in")
    bu = _sc_fwd(p, cw, name=f"{tag}_conv")
    xo = _mm(bu, wout, tm=1024, tn=1024, tk=1024, name=f"{tag}_out", extras=[(x, "tile")],
             epilogue=lambda acc, xt: (xt + acc,))
    return xo, (x, h, p, bu)


def _sc_layer_bwd(dout, saved, g, win, cw, wout, tag):
    x, h, p, bu = saved
    dbu = _mm(dout, wout, tb=True, tm=1024, tn=1024, tk=1024, name=f"{tag}_dbu")
    dwout = _mm(bu, dout, ta=True, tm=1024, tn=1024, tk=2048, name=f"{tag}_dwout", out_dtype=BF16)
    db, dc, dh, dcw = _sc_bwd(p, cw, dbu, name=f"{tag}_conv_bwd")
    dp = jnp.concatenate([db, dc, dh], axis=1)
    dwin = _mm(h, dp, ta=True, tm=1024, tn=1024, tk=2048, name=f"{tag}_dwin", out_dtype=BF16)
    dx, dg = _mm(dp, win, tb=True, tm=512, tn=1024, tk=3072, name=f"{tag}_dx",
                 extras=[(x, "tile"), (g, "row"), (dout, "tile")], outs=[(F32, "tile"), (F32, "colsum")],
                 epilogue=_norm_bwd_epilogue)
    return dx, dg, dwin, dcw[:3], dwout


def _ssd_conv_fwd(p, cw, cb, *, name):
    l = p.shape[0]
    tc = 128
    nb = SSD_CONV_DIM // tc
    off = SSD_D_INNER // tc

    def body(x_ref, w_ref, b_ref, o_ref):
        xv = x_ref[...]
        t_idx = lax.broadcasted_iota(jnp.int32, xv.shape, 0)
        pre = xv * w_ref[3:4, :] + b_ref[...]
        for s in (1, 2, 3):
            pre = pre + _shift_down(xv, s, t_idx) * w_ref[3 - s:4 - s, :]
        o_ref[...] = pre * _sigmoid(pre)

    return pl.pallas_call(
        body, name=name, grid=(nb,),
        in_specs=[pl.BlockSpec((l, tc), lambda j: (0, off + j)), pl.BlockSpec((4, tc), lambda j: (0, j)),
                  pl.BlockSpec((1, tc), lambda j: (0, j))],
        out_specs=pl.BlockSpec((l, tc), lambda j: (0, j)),
        out_shape=jax.ShapeDtypeStruct((l, SSD_CONV_DIM), F32), compiler_params=_cparams(1),
    )(p, cw, cb)


def _ssd_conv_bwd(p, cw, cb, dxs_scan, dxs_skip, dbm, dcm, *, name):
    l = p.shape[0]
    tc = 128
    nb = SSD_CONV_DIM // tc
    off = SSD_D_INNER // tc
    n_xs, n_b = SSD_D_INNER // tc, SSD_GROUPS * SSD_STATE // tc

    def body(x_ref, w_ref, b_ref, ga_ref, gb_ref, gm_ref, gc_ref, dx_ref, dw_ref):
        j = pl.program_id(0)
        g_val = jnp.where(j < n_xs, ga_ref[...] + gb_ref[...], jnp.where(j < n_xs + n_b, gm_ref[...], gc_ref[...]))
        xv = x_ref[...]
        t_idx = lax.broadcasted_iota(jnp.int32, xv.shape, 0)
        xs = [xv] + [_shift_down(xv, s, t_idx) for s in (1, 2, 3)]
        pre = b_ref[...] + xs[0] * w_ref[3:4, :]
        for s in (1, 2, 3):
            pre = pre + xs[s] * w_ref[3 - s:4 - s, :]
        sg = _sigmoid(pre)
        dpre = g_val * sg * (1.0 + pre * (1.0 - sg))
        dx = dpre * w_ref[3:4, :]
        for s in (1, 2, 3):
            dx = dx + _shift_up(dpre, s, t_idx) * w_ref[3 - s:4 - s, :]
        dx_ref[...] = dx
        dw_ref[...] = jnp.zeros_like(dw_ref)
        for s in (0, 1, 2, 3):
            dw_ref[3 - s:4 - s, :] = jnp.sum(dpre * xs[s], axis=0, keepdims=True)
        dw_ref[4:5, :] = jnp.sum(dpre, axis=0, keepdims=True)

    return pl.pallas_call(
        body, name=name, grid=(nb,),
        in_specs=[pl.BlockSpec((l, tc), lambda j: (0, off + j)), pl.BlockSpec((4, tc), lambda j: (0, j)),
                  pl.BlockSpec((1, tc), lambda j: (0, j)),
                  pl.BlockSpec((l, tc), lambda j: (0, jnp.minimum(j, n_xs - 1))),
                  pl.BlockSpec((l, tc), lambda j: (0, jnp.minimum(j, n_xs - 1))),
                  pl.BlockSpec((l, tc), lambda j: (0, jnp.clip(j - n_xs, 0, n_b - 1))),
                  pl.BlockSpec((l, tc), lambda j: (0, jnp.clip(j - n_xs - n_b, 0, n_b - 1)))],
        out_specs=[pl.BlockSpec((l, tc), lambda j: (0, j)), pl.BlockSpec((8, tc), lambda j: (0, j))],
        out_shape=[jax.ShapeDtypeStruct((l, SSD_CONV_DIM), F32), jax.ShapeDtypeStruct((8, SSD_CONV_DIM), F32)],
        compiler_params=_cparams(1),
    )(p, cw, cb, dxs_scan, dxs_skip, dbm, dcm)


def _ssd_dt_fwd(p, bias, *, name):
    l = p.shape[0]
    tm = _tile(l, 1024)
    off = (SSD_D_INNER + SSD_CONV_DIM) // 128

    def body(x_ref, b_ref, o_ref):
        v = x_ref[...] + b_ref[...]
        o_ref[...] = jnp.maximum(v, 0.0) + jnp.log(1.0 + jnp.exp(-jnp.abs(v)))

    return pl.pallas_call(
        body, name=name, grid=(l // tm,),
        in_specs=[pl.BlockSpec((tm, 128), lambda i: (i, off)), pl.BlockSpec((1, 128), lambda i: (0, 0))],
        out_specs=pl.BlockSpec((tm, 128), lambda i: (i, 0)),
        out_shape=jax.ShapeDtypeStruct((l, 128), F32), compiler_params=_cparams(1),
    )(p, bias)


def _ssd_dt_bwd(p, bias, ddt, *, name):
    l = p.shape[0]
    tm = _tile(l, 1024)
    off = (SSD_D_INNER + SSD_CONV_DIM) // 128

    def body(x_ref, b_ref, g_ref, o_ref, db_ref):
        i = pl.program_id(0)
        d = g_ref[...] * _sigmoid(x_ref[...] + b_ref[...])
        o_ref[...] = d
        _accumulate(db_ref, jnp.sum(d, axis=0, keepdims=True), i == 0)

    return pl.pallas_call(
        body, name=name, grid=(l // tm,),
        in_specs=[pl.BlockSpec((tm, 128), lambda i: (i, off)), pl.BlockSpec((1, 128), lambda i: (0, 0)),
                  pl.BlockSpec((tm, 128), lambda i: (i, 0))],
        out_specs=[pl.BlockSpec((tm, 128), lambda i: (i, 0)), pl.BlockSpec((1, 128), lambda i: (0, 0))],
        out_shape=[jax.ShapeDtypeStruct((l, 128), F32), jax.ShapeDtypeStruct((1, 128), F32)],
        compiler_params=_cparams(1),
    )(p, bias, ddt)


def _row_to_col(r, eye):
    return jnp.sum(jnp.where(eye, r, 0.0), axis=1, keepdims=True)


def _col_to_row(c, eye):
    return jnp.sum(jnp.where(eye, c, 0.0), axis=0, keepdims=True)


def _ssd_chunk_common(b_ref, c_ref, dt_ref, a_ref, lam_scr):
    n = SSD_CHUNK
    row = lax.broadcasted_iota(jnp.int32, (n, n), 0)
    col = lax.broadcasted_iota(jnp.int32, (n, n), 1)
    bm, cm = b_ref[...].astype(BF16), c_ref[...].astype(BF16)
    g = _dot(cm, bm, NT)
    incl = (row <= col).astype(BF16)
    lam_scr[...] = _dot_exact(dt_ref[...] * a_ref[...], incl)
    return row, col, bm, cm, g


def _ssd_head_common(r, row, col, dt_ref, lam_scr):
    eye, tril = row == col, row >= col
    lam_r = lam_scr[r:r + 1, :]
    dt_r = dt_ref[r:r + 1, :]
    lam_c = _row_to_col(lam_r, eye)
    dt_c = _row_to_col(dt_r, eye)
    dk = jnp.where(tril, jnp.exp(jnp.minimum(lam_c - lam_r, 0.0)), 0.0)
    lam_last = jnp.sum(jnp.where(col[0:1, :] == SSD_CHUNK - 1, lam_r, 0.0), axis=1, keepdims=True)
    return eye, lam_r, dt_r, lam_c, dt_c, dk, lam_last


def _ssd_fwd(xh, act, dt_t, a_b, *, name, hosted=None):
    l = xh.shape[1]
    nc = l // SSD_CHUNK
    n, p_dim, hpg = SSD_CHUNK, SSD_HEAD_DIM, SSD_HPG

    gps = SSD_GROUPS_PER_STEP
    n_grp = SSD_GROUPS // gps
    grid = (n_grp, nc)
    nh = hosted.n if hosted else 0

    def body(x_ref, b_ref, c_ref, dt_ref, a_ref, *rest):
        y_ref, hp_ref = rest[nh:nh + 2]
        h_scr, lam_scr = rest[2 * nh + 2:2 * nh + 4]
        at_end = hosted.run(rest[:nh], rest[nh + 2:2 * nh + 2], rest[2 * nh + 4:], grid) if hosted else None

        @pl.when(pl.program_id(1) == 0)
        def _():
            h_scr[...] = jnp.zeros_like(h_scr)

        lanes = [pl.ds(gg * SSD_STATE, SSD_STATE) for gg in range(gps)]
        common = [_ssd_chunk_common(b_ref.at[:, lanes[gg]], c_ref.at[:, lanes[gg]], dt_ref.at[gg], a_ref.at[gg],
                                    lam_scr.at[gg]) for gg in range(gps)]
        for gg in range(gps):
            row, col, bm, cm, g = common[gg]
            for r in range(hpg):
                hd = gg * hpg + r
                _, _, dt_r, lam_c, dt_c, dk, lam_last = _ssd_head_common(r, row, col, dt_ref.at[gg], lam_scr.at[gg])
                xr = x_ref[hd]
                hr = h_scr[hd]
                w = (g * dk * dt_r).astype(BF16)
                y = _dot(w, xr.astype(BF16)) + _dot(cm, hr.astype(BF16), NT) * jnp.exp(lam_c)
                y_ref[hd] = y
                hp_ref[hd] = hr
                xw = (xr * (jnp.exp(lam_last - lam_c) * dt_c)).astype(BF16)
                h_scr[hd] = jnp.exp(lam_last) * hr + _dot(xw, bm, TN)
        if hosted:
            at_end()

    g_off = SSD_D_INNER // (gps * SSD_STATE)
    res = pl.pallas_call(
        body, name=name, grid=grid,
        in_specs=[pl.BlockSpec((gps * hpg, n, p_dim), lambda g, c: (g, c, 0)),
                  pl.BlockSpec((n, gps * SSD_STATE), lambda g, c: (c, g_off + g)),
                  pl.BlockSpec((n, gps * SSD_STATE), lambda g, c: (c, g_off + n_grp + g)),
                  pl.BlockSpec((gps, 8, n), lambda g, c: (g, 0, c)),
                  pl.BlockSpec((gps, 8, 128), lambda g, c: (g, 0, 0))] + (hosted.specs if hosted else []),
        out_specs=[pl.BlockSpec((gps * hpg, n, p_dim), lambda g, c: (g, c, 0)),
                   pl.BlockSpec((None, gps * hpg, p_dim, SSD_STATE), lambda g, c: (c, g, 0, 0))]
        + (hosted.specs if hosted else []),
        out_shape=[jax.ShapeDtypeStruct(xh.shape, F32),
                   jax.ShapeDtypeStruct((nc, SSD_HEADS, p_dim, SSD_STATE), F32)] + (hosted.out_shape if hosted else []),
        scratch_shapes=[pltpu.VMEM((gps * hpg, p_dim, SSD_STATE), F32), pltpu.VMEM((gps, 8, n), F32)]
        + (hosted.sems if hosted else []),
        compiler_params=_cparams(2),
    )(xh, act, act, dt_t, a_b, *(hosted.xs if hosted else []))
    return (res[0], res[1], res[2:]) if hosted else res


def _ssd_bwd(xh, act, dt_t, a_b, hprev, dyh, *, name):
    l = xh.shape[1]
    nc = l // SSD_CHUNK
    n, p_dim, hpg = SSD_CHUNK, SSD_HEAD_DIM, SSD_HPG
    gps = SSD_GROUPS_PER_STEP

    def body(x_ref, b_ref, c_ref, dt_ref, a_ref, hp_ref, dy_ref,
             dx_ref, db_ref, dc_ref, ddt_ref, da_ref, dh_scr, lam_scr, dlam_scr, ddt_scr):
        ci = pl.program_id(1)

        @pl.when(ci == 0)
        def _():
            dh_scr[...] = jnp.zeros_like(dh_scr)

        lanes = [pl.ds(gg * SSD_STATE, SSD_STATE) for gg in range(gps)]
        common = [_ssd_chunk_common(b_ref.at[:, lanes[gg]], c_ref.at[:, lanes[gg]], dt_ref.at[gg], a_ref.at[gg],
                                    lam_scr.at[gg]) for gg in range(gps)]
        dlam_scr[...] = jnp.zeros_like(dlam_scr)
        ddt_scr[...] = jnp.zeros_like(ddt_scr)
        for gg in range(gps):
            row, col, bm, cm, g = common[gg]
            dt_g, lam_g, dlam_g, ddt_g = dt_ref.at[gg], lam_scr.at[gg], dlam_scr.at[gg], ddt_scr.at[gg]
            dg_acc = jnp.zeros((n, n), F32)
            dc_acc = jnp.zeros((n, SSD_STATE), F32)
            db_acc = jnp.zeros((n, SSD_STATE), F32)
            for r in range(hpg):
                hd = gg * hpg + r
                eye, _, dt_r, lam_c, dt_c, dk, lam_last = _ssd_head_common(r, row, col, dt_g, lam_g)
                xr, dyr, hr, dhr = x_ref[hd], dy_ref[hd], hp_ref[hd], dh_scr[hd]
                xb, dyb, hb, dhb = xr.astype(BF16), dyr.astype(BF16), hr.astype(BF16), dhr.astype(BF16)
                e_l = jnp.exp(lam_c)
                e_last = jnp.exp(lam_last)
                decay_c = jnp.exp(lam_last - lam_c)
                w_c = decay_c * dt_c
                m = g * dk * dt_r
                dm = _dot(dyb, xb, NT)
                bdh = _dot(bm, dhb, NT)
                dx_ref[hd] = _dot(m.astype(BF16), dyb, TN) + w_c * bdh
                dg_acc = dg_acc + dm * dk * dt_r
                q_mat = dm * g * dk
                p_mat = q_mat * dt_r
                yoff = _dot(cm, hb, NT) * e_l
                q_c = jnp.sum(xr * bdh, axis=1, keepdims=True)
                dlam_c = (jnp.sum(p_mat, axis=1, keepdims=True) + jnp.sum(dyr * yoff, axis=1, keepdims=True)
                          - w_c * q_c)
                d_last = (jnp.sum(w_c * q_c, axis=0, keepdims=True)
                          + e_last * jnp.sum(jnp.sum(dhr * hr, axis=1, keepdims=True), axis=0, keepdims=True))
                dlam_g[r:r + 1, :] = (_col_to_row(dlam_c, eye) - jnp.sum(p_mat, axis=0, keepdims=True)
                                      + jnp.where(col[0:1, :] == n - 1, d_last, 0.0))
                ddt_g[r:r + 1, :] = jnp.sum(q_mat, axis=0, keepdims=True) + _col_to_row(decay_c * q_c, eye)
                dc_acc = dc_acc + e_l * _dot(dyb, hb)
                db_acc = db_acc + _dot((xr * w_c).astype(BF16), dhb)
                dh_scr[hd] = e_last * dhr + _dot((dyr * e_l).astype(BF16), cm, TN)

            dgb = dg_acc.astype(BF16)
            dc_ref[:, lanes[gg]] = _dot(dgb, bm) + dc_acc
            db_ref[:, lanes[gg]] = _dot(dgb, cm, TN) + db_acc
            rev = (row >= col).astype(BF16)
            da = _dot_exact(dlam_g[...], rev)
            ddt_ref[gg] = ddt_g[...] + da * a_ref[gg]
            _accumulate(da_ref.at[gg], da * dt_g[...], ci == 0)

        @pl.when(ci == nc - 1)
        def _():
            for gg in range(gps):
                da_ref[gg] = jnp.broadcast_to(jnp.sum(da_ref[gg], axis=1, keepdims=True), da_ref.shape[1:])

    g_off = SSD_D_INNER // (gps * SSD_STATE)
    n_grp = SSD_GROUPS // gps
    rc = lambda c: nc - 1 - c
    hspec = pl.BlockSpec((gps * hpg, n, p_dim), lambda g, c: (g, rc(c), 0))
    gspec = pl.BlockSpec((n, gps * SSD_STATE), lambda g, c: (rc(c), g))
    return pl.pallas_call(
        body, name=name, grid=(n_grp, nc),
        in_specs=[hspec,
                  pl.BlockSpec((n, gps * SSD_STATE), lambda g, c: (rc(c), g_off + g)),
                  pl.BlockSpec((n, gps * SSD_STATE), lambda g, c: (rc(c), g_off + n_grp + g)),
                  pl.BlockSpec((gps, 8, n), lambda g, c: (g, 0, rc(c))),
                  pl.BlockSpec((gps, 8, 128), lambda g, c: (g, 0, 0)),
                  pl.BlockSpec((None, gps * hpg, p_dim, SSD_STATE), lambda g, c: (rc(c), g, 0, 0)),
                  hspec],
        out_specs=[hspec, gspec, gspec,
                   pl.BlockSpec((gps, 8, n), lambda g, c: (g, 0, rc(c))),
                   pl.BlockSpec((gps, 8, 128), lambda g, c: (g, 0, 0))],
        out_shape=[jax.ShapeDtypeStruct(xh.shape, F32),
                   jax.ShapeDtypeStruct((l, SSD_GROUPS * SSD_STATE), F32),
                   jax.ShapeDtypeStruct((l, SSD_GROUPS * SSD_STATE), F32),
                   jax.ShapeDtypeStruct(dt_t.shape, F32),
                   jax.ShapeDtypeStruct(a_b.shape, F32)],
        scratch_shapes=[pltpu.VMEM((gps * hpg, p_dim, SSD_STATE), F32), pltpu.VMEM((gps, 8, n), F32),
                        pltpu.VMEM((gps, 8, n), F32), pltpu.VMEM((gps, 8, n), F32)],
        compiler_params=_cparams(2),
    )(xh, act, act, dt_t, a_b, hprev, dyh)


def _ssd_gate_fwd(y, act, p, d_vec, gn, *, name):
    l = y.shape[0]
    w = SSD_D_INNER
    tm = _tile(l, 256)

    def body(y_ref, xs_ref, z_ref, d_ref, g_ref, o_ref):
        for gi in range(SSD_GROUPS):
            sl = slice(gi * SSD_NORM_GROUP, (gi + 1) * SSD_NORM_GROUP)
            z = z_ref[:, sl]
            y2 = (y_ref[:, sl] + d_ref[:, sl] * xs_ref[:, sl]) * (z * _sigmoid(z))
            r = lax.rsqrt(jnp.mean(y2 * y2, axis=1, keepdims=True) + RMS_EPS)
            o_ref[:, sl] = (y2 * r * g_ref[:, sl]).astype(BF16)

    rows = pl.BlockSpec((tm, w), lambda i: (i, 0))
    vec = pl.BlockSpec((1, w), lambda i: (0, 0))
    return pl.pallas_call(
        body, name=name, grid=(l // tm,), in_specs=[rows, rows, rows, vec, vec], out_specs=rows,
        out_shape=jax.ShapeDtypeStruct((l, w), BF16), compiler_params=_cparams(1),
    )(y, act, p, d_vec, gn)


def _ssd_gate_bwd(dyn, y, act, p, d_vec, gn, *, name):
    l = y.shape[0]
    w = SSD_D_INNER
    tm = _tile(l, 256)

    def body(dyn_ref, y_ref, xs_ref, z_ref, d_ref, g_ref, dy_ref, dz_ref, dxs_ref, dd_ref, dg_ref):
        i = pl.program_id(0)
        for gi in range(SSD_GROUPS):
            sl = slice(gi * SSD_NORM_GROUP, (gi + 1) * SSD_NORM_GROUP)
            z, xs, dv = z_ref[:, sl], xs_ref[:, sl], d_ref[:, sl]
            s = _sigmoid(z)
            sz = z * s
            y1 = y_ref[:, sl] + dv * xs
            y2 = y1 * sz
            r = lax.rsqrt(jnp.mean(y2 * y2, axis=1, keepdims=True) + RMS_EPS)
            y2h = y2 * r
            dyn_v = dyn_ref[:, sl]
            d2h = dyn_v * g_ref[:, sl]
            dy2 = r * (d2h - y2h * jnp.mean(d2h * y2h, axis=1, keepdims=True))
            dy1 = dy2 * sz
            dy_ref[:, sl] = dy1
            dz_ref[:, sl] = dy2 * y1 * s * (1.0 + z * (1.0 - s))
            dxs_ref[:, sl] = dv * dy1
            _accumulate(dd_ref.at[:, sl], jnp.sum(dy1 * xs, axis=0, keepdims=True), i == 0)
            _accumulate(dg_ref.at[:, sl], jnp.sum(dyn_v * y2h, axis=0, keepdims=True), i == 0)

    rows = pl.BlockSpec((tm, w), lambda i: (i, 0))
    vec = pl.BlockSpec((1, w), lambda i: (0, 0))
    return pl.pallas_call(
        body, name=name, grid=(l // tm,), in_specs=[rows, rows, rows, rows, vec, vec],
        out_specs=[rows, rows, rows, vec, vec],
        out_shape=[jax.ShapeDtypeStruct((l, w), F32)] * 3 + [jax.ShapeDtypeStruct((1, w), F32)] * 2,
        compiler_params=_cparams(1),
    )(dyn, y, act, p, d_vec, gn)


def _heads_major(x):
    return x.reshape(x.shape[0], SSD_HEADS, SSD_HEAD_DIM).transpose(1, 0, 2)


def _ssd_layer_fwd(x, g, win, cw, cb, dt_bias, a_log, d_skip, gn, wout, tag, hosted=None):
    l = x.shape[0]
    h = _rmsnorm(x, g, name=f"{tag}_norm")
    p = _mm(h, win, tm=1024, tn=896, tk=1024, name=f"{tag}_in")
    act = _ssd_conv_fwd(p, cw, cb, name=f"{tag}_conv")
    bias = jnp.pad(dt_bias, (0, 128 - SSD_HEADS)).reshape(1, 128)
    dt = _ssd_dt_fwd(p, bias, name=f"{tag}_dt")
    xh = _heads_major(act[:, :SSD_D_INNER])
    dt_t = jnp.pad(dt[:, :SSD_HEADS].T.reshape(SSD_GROUPS, SSD_HPG, l), ((0, 0), (0, 8 - SSD_HPG), (0, 0)))
    a = -jnp.exp(a_log).reshape(SSD_GROUPS, SSD_HPG, 1)
    a_b = jnp.broadcast_to(jnp.pad(a, ((0, 0), (0, 8 - SSD_HPG), (0, 0))), (SSD_GROUPS, 8, 128))
    res = _ssd_fwd(xh, act, dt_t, a_b, name=f"{tag}_scan", hosted=hosted)
    yh, hprev = res[:2]
    carried = res[2] if hosted else None
    y = yh.transpose(1, 0, 2).reshape(l, SSD_D_INNER)
    d_vec = jnp.repeat(d_skip, SSD_HEAD_DIM).reshape(1, SSD_D_INNER)
    yn = _ssd_gate_fwd(y, act, p, d_vec, gn, name=f"{tag}_gate")
    xo = _mm(yn, wout, tm=1024, tn=1024, tk=2048, name=f"{tag}_out", extras=[(x, "tile")],
             epilogue=lambda acc, xt: (xt + acc,))
    return xo, (x, h, p, act, bias, xh, dt_t, a_b, hprev, y, d_vec, yn), carried


def _ssd_layer_bwd(dout, saved, g, win, cw, cb, gn, wout, tag):
    x, h, p, act, bias, xh, dt_t, a_b, hprev, y, d_vec, yn = saved
    l = x.shape[0]
    dyn = _mm(dout, wout, tb=True, tm=1024, tn=1024, tk=1024, name=f"{tag}_dyn")
    dwout = _mm(yn, dout, ta=True, tm=1024, tn=1024, tk=2048, name=f"{tag}_dwout", out_dtype=BF16)
    dy, dz, dxs_d, dd_vec, dgn = _ssd_gate_bwd(dyn, y, act, p, d_vec, gn, name=f"{tag}_gate_bwd")
    dxh, dbm, dcm, ddt_t, da_b = _ssd_bwd(xh, act, dt_t, a_b, hprev, _heads_major(dy), name=f"{tag}_scan_bwd")
    dxs_scan = dxh.transpose(1, 0, 2).reshape(l, SSD_D_INNER)
    dxbc, dcw8 = _ssd_conv_bwd(p, cw, cb, dxs_scan, dxs_d, dbm, dcm, name=f"{tag}_conv_bwd")
    ddt = jnp.pad(ddt_t[:, :SSD_HPG, :].reshape(SSD_HEADS, l).T, ((0, 0), (0, 128 - SSD_HEADS)))
    ddt_raw, dbias = _ssd_dt_bwd(p, bias, ddt, name=f"{tag}_dt_bwd")
    dp = jnp.concatenate([dz, dxbc, ddt_raw], axis=1)
    dwin = _mm(h, dp, ta=True, tm=1024, tn=896, tk=2048, name=f"{tag}_dwin", out_dtype=BF16)
    dx, dg = _mm(dp, win, tb=True, tm=256, tn=1024, tk=6272, name=f"{tag}_dx",
                 extras=[(x, "tile"), (g, "row"), (dout, "tile")], outs=[(F32, "tile"), (F32, "colsum")],
                 epilogue=_norm_bwd_epilogue)
    a_heads = a_b[:, :SSD_HPG, 0].reshape(SSD_HEADS)
    grads = dict(
        ssd_w_in=dwin[:, :SSD_IN_DIM], ssd_conv_w=dcw8[:4], ssd_conv_b=dcw8[4],
        ssd_dt_bias=dbias[0, :SSD_HEADS], ssd_a_log=da_b[:, :SSD_HPG, 0].reshape(SSD_HEADS) * a_heads,
        ssd_d=dd_vec.reshape(SSD_HEADS, SSD_HEAD_DIM).sum(axis=1), ssd_norm=dgn[0], ssd_w_out=dwout)
    return dx, dg, grads


def _local_step(x, tgt, w, gather_later=None, scatter_early=None):
    row = lambda v: v.reshape(1, -1)
    saved = []
    for i in range(DEPTH):
        kind, j = i % 3, i // 3
        x, s1 = _ffn_fwd(x, row(w["ffn1_norm"][i]), w["ffn1_w_gu"][i], w["ffn1_w_down"][i], f"l{i}f1")
        gm = row(w["mix_norm"][i])
        hook = (gather_later or {}).get(i)
        hosted = hook[0] if hook else None
        if kind == 0:
            x, sm, carried = _sb_layer_fwd(x, gm, w["sb_w_qkv"][j], w["sb_w_o"][j], f"l{i}sb", hosted=hosted)
        elif kind == 1:
            x, sm, carried = _ssd_layer_fwd(x, gm, w["ssd_w_in"][j], w["ssd_conv_w"][j], row(w["ssd_conv_b"][j]),
                                            w["ssd_dt_bias"][j], w["ssd_a_log"][j], w["ssd_d"][j],
                                            row(w["ssd_norm"][j]), w["ssd_w_out"][j], f"l{i}ssd", hosted=hosted)
        if hook:
            w = hook[1](w, carried)
        if kind == 2:
            x, sm = _sc_layer_fwd(x, gm, w["sc_w_in"][j], w["sc_conv_w"][j], w["sc_w_out"][j], f"l{i}sc")
        x, s2 = _ffn_fwd(x, row(w["ffn2_norm"][i]), w["ffn2_w_gu"][i], w["ffn2_w_down"][i], f"l{i}f2")
        saved.append((s1, sm, s2))

    loss, dx, dfinal = _final_loss(x, row(w["final_norm"]), tgt, name="final_loss")
    per_layer = {k: [None] * DEPTH for k in ("ffn1_norm", "ffn1_w_gu", "ffn1_w_down", "mix_norm",
                                             "ffn2_norm", "ffn2_w_gu", "ffn2_w_down")}
    per_layer.update({"sb_w_qkv": [None, None], "sb_w_o": [None, None]})
    grads = {"final_norm": dfinal[0]}
    early = None
    for i in reversed(range(DEPTH)):
        kind, j = i % 3, i // 3
        s1, sm, s2 = saved[i]
        dx, dg, dwgu, dwd = _ffn_bwd(dx, s2, row(w["ffn2_norm"][i]), w["ffn2_w_gu"][i], w["ffn2_w_down"][i], f"l{i}f2")
        per_layer["ffn2_norm"][i], per_layer["ffn2_w_gu"][i], per_layer["ffn2_w_down"][i] = dg[0], dwgu, dwd
        gm = row(w["mix_norm"][i])
        if kind == 0:
            hosted = scatter_early({**grads, **per_layer}) if (scatter_early and i == 0) else None
            dx, dg, dwqkv, dwo, carried = _sb_layer_bwd(dx, sm, gm, w["sb_w_qkv"][j], w["sb_w_o"][j], f"l{i}sb",
                                                        hosted=hosted)
            per_layer["sb_w_qkv"][j], per_layer["sb_w_o"][j] = dwqkv, dwo
            if hosted:
                early = carried
        elif kind == 1:
            dx, dg, sg = _ssd_layer_bwd(dx, sm, gm, w["ssd_w_in"][j], w["ssd_conv_w"][j], row(w["ssd_conv_b"][j]),
                                        row(w["ssd_norm"][j]), w["ssd_w_out"][j], f"l{i}ssd")
            sg["ssd_w_in"], sg["ssd_w_out"] = [sg["ssd_w_in"]], [sg["ssd_w_out"]]
            grads.update({k: (v if isinstance(v, list) else v[None]) for k, v in sg.items()})
        else:
            dx, dg, dwin, dcw, dwout = _sc_layer_bwd(dx, sm, gm, w["sc_w_in"][j], w["sc_conv_w"][j],
                                                     w["sc_w_out"][j], f"l{i}sc")
            grads.update(sc_w_in=[dwin], sc_conv_w=dcw[None], sc_w_out=[dwout])
        per_layer["mix_norm"][i] = dg[0]
        dx, dg, dwgu, dwd = _ffn_bwd(dx, s1, row(w["ffn1_norm"][i]), w["ffn1_w_gu"][i], w["ffn1_w_down"][i], f"l{i}f1")
        per_layer["ffn1_norm"][i], per_layer["ffn1_w_gu"][i], per_layer["ffn1_w_down"][i] = dg[0], dwgu, dwd
    for k, v in per_layer.items():
        grads[k] = jnp.stack(v) if k.endswith("_norm") else v
    return loss, dx, grads, early


_HBM = pl.BlockSpec(memory_space=pltpu.HBM)


def _remote(src, dst, send_sems, recv_sems, idx, dev):
    return pltpu.make_async_remote_copy(src_ref=src, dst_ref=dst, send_sem=send_sems.at[idx], recv_sem=recv_sems.at[idx],
                                        device_id=dev, device_id_type=pl.DeviceIdType.MESH)


def _exchange_call(body, xs, out_shapes, n_copies, name):
    n = len(xs)
    return pl.pallas_call(
        body, name=name, in_specs=[_HBM] * n, out_specs=[_HBM] * n,
        out_shape=[jax.ShapeDtypeStruct(s, x.dtype) for s, x in zip(out_shapes, xs)],
        scratch_shapes=[pltpu.SemaphoreType.DMA((n, n_copies)), pltpu.SemaphoreType.DMA((n, n_copies)),
                        pltpu.SemaphoreType.DMA((n,))],
    )(*xs)


def _gather(xs, *, name):
    n = len(xs)

    def body(*refs):
        start, finish = _gather_steps(refs[:n], refs[n:2 * n], *refs[2 * n:])
        start()
        finish()

    return _exchange_call(body, xs, _gather_shapes(xs), _GATHER_COPIES, name)


_GATHER_COPIES = 7


def _gather_shapes(xs):
    return [(N_DEV,) + x.shape for x in xs]


def _gather_steps(x_refs, o_refs, send_sems, recv_sems, local_sems):
    n = len(x_refs)

    def plan():
        mx, my, mc = lax.axis_index("x"), lax.axis_index("y"), lax.axis_index("c")
        slot = lambda px, py, pc: 4 * px + 2 * py + pc
        me, sibling = (mx, my, mc), (mx, my, 1 - mc)
        chips = [(1 - mx, my), (mx, 1 - my), (1 - mx, 1 - my)]
        locals_, first = [], []
        for a in range(n):
            x_ref, o_ref = x_refs[a], o_refs[a]
            locals_.append(pltpu.make_async_copy(x_ref, o_ref.at[slot(*me)], local_sems.at[a]))
            first.append(_remote(x_ref, o_ref.at[slot(*me)], send_sems, recv_sems, (a, 0), sibling))
            for j, chip in enumerate(chips):
                first.append(_remote(x_ref, o_ref.at[slot(*me)], send_sems, recv_sems, (a, 1 + j), (*chip, mc)))
        return locals_, first, slot, me, sibling, chips, mc

    def start():
        locals_, first = plan()[:2]
        for cp in locals_ + first:
            cp.start()

    def finish():
        locals_, first, slot, me, sibling, chips, mc = plan()
        passed = []
        for j, chip in enumerate(chips):
            for a in range(n):
                landed = o_refs[a].at[slot(*chip, mc)]
                _remote(landed, landed, send_sems, recv_sems, (a, 1 + j), me).wait_recv()
                fwd = _remote(landed, landed, send_sems, recv_sems, (a, 4 + j), sibling)
                fwd.start()
                passed.append(fwd)
        for a in range(n):
            from_sib = o_refs[a].at[slot(*sibling)]
            _remote(from_sib, from_sib, send_sems, recv_sems, (a, 0), me).wait_recv()
            for j, chip in enumerate(chips):
                via_sib = o_refs[a].at[slot(*chip, 1 - mc)]
                _remote(via_sib, via_sib, send_sems, recv_sems, (a, 4 + j), me).wait_recv()
        for cp in first + passed:
            cp.wait_send()
        for cp in locals_:
            cp.wait()

    return start, finish


def _scatter_sibling(xs, *, name):
    n = len(xs)

    def body(*refs):
        x_refs, o_refs = refs[:n], refs[n:2 * n]
        send_sems, recv_sems, _ = refs[2 * n:]
        mx, my, mc = lax.axis_index("x"), lax.axis_index("y"), lax.axis_index("c")
        sibling = (mx, my, 1 - mc)
        sends = []
        for a in range(n):
            for ch in range(4):
                sends.append(_remote(x_refs[a].at[ch, 1 - mc], o_refs[a].at[ch], send_sems, recv_sems, (a, ch), sibling))
        for cp in sends:
            cp.start()
        for cp in sends:
            cp.wait_recv()
        for cp in sends:
            cp.wait_send()

    return _exchange_call(body, xs, [(4,) + x.shape[2:] for x in xs], 4, name)


def _scatter_chips(ys, *, name):
    n = len(ys)

    def body(*refs):
        start, finish = _chip_scatter_steps(refs[:n], refs[n:2 * n], *refs[2 * n:])
        start()
        finish()

    return _exchange_call(body, ys, _chip_scatter_shapes(ys), _CHIP_SCATTER_COPIES, name)


_CHIP_SCATTER_COPIES = 3


def _chip_scatter_shapes(ys):
    return [y.shape for y in ys]


def _chip_scatter_steps(y_refs, o_refs, send_sems, recv_sems, local_sems):
    n = len(y_refs)

    def plan():
        mx, my, mc = lax.axis_index("x"), lax.axis_index("y"), lax.axis_index("c")
        mine = 2 * mx + my
        chips = [(1 - mx, my), (mx, 1 - my), (1 - mx, 1 - my)]
        locals_, sends, recvs = [], [], []
        for a in range(n):
            locals_.append(pltpu.make_async_copy(y_refs[a].at[mine], o_refs[a].at[mine], local_sems.at[a]))
            for j, (px, py) in enumerate(chips):
                theirs = 2 * px + py
                sends.append(_remote(y_refs[a].at[theirs], o_refs[a].at[mine], send_sems, recv_sems, (a, j), (px, py, mc)))
                recvs.append(_remote(y_refs[a].at[theirs], o_refs[a].at[theirs], send_sems, recv_sems, (a, j), (px, py, mc)))
        return locals_, sends, recvs

    def start():
        locals_, sends, _ = plan()
        for cp in locals_ + sends:
            cp.start()

    def finish():
        locals_, sends, recvs = plan()
        for cp in recvs:
            cp.wait_recv()
        for cp in sends:
            cp.wait_send()
        for cp in locals_:
            cp.wait()

    return start, finish


def _pair_add(x, r, *, name):
    _, _, rows, c = x.shape
    tr = _tile(rows, 512, 16)

    def body(core_ref, x_ref, r_ref, o_ref):
        o_ref[...] = (x_ref[...].astype(F32) + r_ref[...].astype(F32)).astype(o_ref.dtype)

    core = lax.axis_index("c").astype(jnp.int32).reshape(1)
    return pl.pallas_call(
        body, name=name,
        grid_spec=pltpu.PrefetchScalarGridSpec(
            num_scalar_prefetch=1, grid=(4, rows // tr),
            in_specs=[pl.BlockSpec((None, None, tr, c), lambda ch, i, core: (ch, core[0], i, 0)),
                      pl.BlockSpec((None, tr, c), lambda ch, i, core: (ch, i, 0))],
            out_specs=pl.BlockSpec((None, tr, c), lambda ch, i, core: (ch, i, 0))),
        out_shape=jax.ShapeDtypeStruct((4, rows, c), x.dtype), compiler_params=_cparams(2),
    )(core, x, r)


def _adamw_reduce(parts, w, m, v, *, name):
    r, c = w.shape
    n_parts = parts.shape[0]
    tr = _tile(r, 256, 16)
    bc1 = 1.0 - ADAM_B1 ** ADAM_STEP
    bc2 = 1.0 - ADAM_B2 ** ADAM_STEP

    def body(p_ref, w_ref, m_ref, v_ref, g_ref, d_ref, nm_ref, nv_ref):
        g = p_ref[0].astype(F32)
        for q in range(1, n_parts):
            g = g + p_ref[q].astype(F32)
        nm = ADAM_B1 * m_ref[...] + (1.0 - ADAM_B1) * g
        nv = ADAM_B2 * v_ref[...] + (1.0 - ADAM_B2) * (g * g)
        g_ref[...] = g
        nm_ref[...] = nm
        nv_ref[...] = nv
        d_ref[...] = -ADAM_LR * ((nm / bc1) / (jnp.sqrt(nv / bc2) + ADAM_EPS) + ADAM_WD * w_ref[...])

    blk = pl.BlockSpec((tr, c), lambda i: (i, 0))
    return pl.pallas_call(
        body, name=name, grid=(r // tr,),
        in_specs=[pl.BlockSpec((n_parts, tr, c), lambda i: (0, i, 0)), blk, blk, blk], out_specs=[blk] * 4,
        out_shape=[jax.ShapeDtypeStruct((r, c), F32)] * 4, compiler_params=_cparams(1),
    )(parts, w, m, v)


def _col_full(g):
    return g.transpose(1, 2, 0, 3).reshape(g.shape[1], g.shape[2], -1)


def _col_parts(f):
    n, k, c8 = f.shape
    return f.reshape(n, k, N_DEV, c8 // N_DEV).transpose(2, 0, 1, 3)


def _row_full(g):
    return g.transpose(1, 0, 2, 3).reshape(g.shape[1], -1, g.shape[3])


def _row_parts(f):
    n, r8, c = f.shape
    return f.reshape(n, N_DEV, r8 // N_DEV, c).transpose(1, 0, 2, 3)


def _gu_full(g):
    n, d, c = g.shape[1:]
    return g.reshape(2, 4, n, d, c).transpose(2, 0, 3, 1, 4).reshape(n, 2, d, 4 * c)


def _gu_parts(f):
    n, _, d, c4 = f.shape
    return f.reshape(n, 2, d, 4, c4 // 4).transpose(1, 3, 0, 2, 4).reshape(N_DEV, n, d, c4 // 4)


def _ssd_in_full(g):
    return jnp.pad(_col_full(g), ((0, 0), (0, 0), (0, SSD_IN_PAD - SSD_IN_DIM)))


_MATMUL_WEIGHTS = (
    ("ffn1_w_gu", _gu_full, _gu_parts), ("ffn1_w_down", _row_full, _row_parts),
    ("ffn2_w_gu", _gu_full, _gu_parts), ("ffn2_w_down", _row_full, _row_parts),
    ("sb_w_qkv", _col_full, _col_parts), ("sb_w_o", _row_full, _row_parts),
    ("ssd_w_in", _ssd_in_full, _col_parts), ("ssd_w_out", _row_full, _row_parts),
    ("sc_w_in", _col_full, _col_parts), ("sc_w_out", _row_full, _row_parts),
)
_FIRST_WEIGHTS = ("ffn1_w_gu", "ffn1_w_down", "sb_w_qkv", "sb_w_o")
_CONV_WEIGHTS = ("ssd_conv_w", "sc_conv_w")
_REPLICATED = ("ffn1_norm", "mix_norm", "ffn2_norm", "final_norm", "ssd_conv_b", "ssd_norm",
               "ssd_dt_bias", "ssd_a_log", "ssd_d")
_ORDER = ("ffn1_norm", "ffn1_w_gu", "ffn1_w_down", "mix_norm", "ffn2_norm", "ffn2_w_gu", "ffn2_w_down",
          "sb_w_qkv", "sb_w_o", "ssd_w_in", "ssd_conv_w", "ssd_conv_b", "ssd_dt_bias", "ssd_a_log", "ssd_d",
          "ssd_norm", "ssd_w_out", "sc_w_in", "sc_conv_w", "sc_w_out", "final_norm")
_LANES = 1024


def _rows_of(a):
    flat = a.reshape(-1)
    pad = -flat.shape[0] % _LANES
    return jnp.pad(flat, (0, pad)).reshape(-1, _LANES)


def _pack_rows(arrays, mult):
    rows = [_rows_of(a) for a in arrays]
    packed = jnp.concatenate(rows, axis=0)
    pad = -packed.shape[0] % mult
    return jnp.pad(packed, ((0, pad), (0, 0))), [r.shape[0] for r in rows]


def _unpack_rows(packed, counts, shapes, lead=()):
    out, off = [], 0
    for n, shp in zip(counts, shapes):
        size = math.prod(shp)
        seg = packed[..., off:off + n, :].reshape(lead + (n * _LANES,))[..., :size]
        out.append(seg.reshape(lead + tuple(shp)))
        off += n
    return out


def kernel(x, ffn1_norm, ffn1_w_gu, ffn1_w_down, mix_norm, ffn2_norm, ffn2_w_gu, ffn2_w_down, sb_w_qkv, sb_w_o, ssd_w_in, ssd_conv_w, ssd_conv_b, ssd_dt_bias, ssd_a_log, ssd_d, ssd_norm, ssd_w_out, sc_w_in, sc_conv_w, sc_w_out, final_norm, loss_target, m_ffn1_norm, m_ffn1_w_gu, m_ffn1_w_down, m_mix_norm, m_ffn2_norm, m_ffn2_w_gu, m_ffn2_w_down, m_sb_w_qkv, m_sb_w_o, m_ssd_w_in, m_ssd_conv_w, m_ssd_conv_b, m_ssd_dt_bias, m_ssd_a_log, m_ssd_d, m_ssd_norm, m_ssd_w_out, m_sc_w_in, m_sc_conv_w, m_sc_w_out, m_final_norm, v_ffn1_norm, v_ffn1_w_gu, v_ffn1_w_down, v_mix_norm, v_ffn2_norm, v_ffn2_w_gu, v_ffn2_w_down, v_sb_w_qkv, v_sb_w_o, v_ssd_w_in, v_ssd_conv_w, v_ssd_conv_b, v_ssd_dt_bias, v_ssd_a_log, v_ssd_d, v_ssd_norm, v_ssd_w_out, v_sc_w_in, v_sc_conv_w, v_sc_w_out, v_final_norm):
    w = dict(ffn1_norm=ffn1_norm, ffn1_w_gu=ffn1_w_gu, ffn1_w_down=ffn1_w_down, mix_norm=mix_norm, ffn2_norm=ffn2_norm, ffn2_w_gu=ffn2_w_gu, ffn2_w_down=ffn2_w_down, sb_w_qkv=sb_w_qkv, sb_w_o=sb_w_o, ssd_w_in=ssd_w_in, ssd_conv_w=ssd_conv_w, ssd_conv_b=ssd_conv_b, ssd_dt_bias=ssd_dt_bias, ssd_a_log=ssd_a_log, ssd_d=ssd_d, ssd_norm=ssd_norm, ssd_w_out=ssd_w_out, sc_w_in=sc_w_in, sc_conv_w=sc_conv_w, sc_w_out=sc_w_out, final_norm=final_norm)
    mom = dict(ffn1_norm=m_ffn1_norm, ffn1_w_gu=m_ffn1_w_gu, ffn1_w_down=m_ffn1_w_down, mix_norm=m_mix_norm, ffn2_norm=m_ffn2_norm, ffn2_w_gu=m_ffn2_w_gu, ffn2_w_down=m_ffn2_w_down, sb_w_qkv=m_sb_w_qkv, sb_w_o=m_sb_w_o, ssd_w_in=m_ssd_w_in, ssd_conv_w=m_ssd_conv_w, ssd_conv_b=m_ssd_conv_b, ssd_dt_bias=m_ssd_dt_bias, ssd_a_log=m_ssd_a_log, ssd_d=m_ssd_d, ssd_norm=m_ssd_norm, ssd_w_out=m_ssd_w_out, sc_w_in=m_sc_w_in, sc_conv_w=m_sc_conv_w, sc_w_out=m_sc_w_out, final_norm=m_final_norm)
    var = dict(ffn1_norm=v_ffn1_norm, ffn1_w_gu=v_ffn1_w_gu, ffn1_w_down=v_ffn1_w_down, mix_norm=v_mix_norm, ffn2_norm=v_ffn2_norm, ffn2_w_gu=v_ffn2_w_gu, ffn2_w_down=v_ffn2_w_down, sb_w_qkv=v_sb_w_qkv, sb_w_o=v_sb_w_o, ssd_w_in=v_ssd_w_in, ssd_conv_w=v_ssd_conv_w, ssd_conv_b=v_ssd_conv_b, ssd_dt_bias=v_ssd_dt_bias, ssd_a_log=v_ssd_a_log, ssd_d=v_ssd_d, ssd_norm=v_ssd_norm, ssd_w_out=v_ssd_w_out, sc_w_in=v_sc_w_in, sc_conv_w=v_sc_conv_w, sc_w_out=v_sc_w_out, final_norm=v_final_norm)
    me = 4 * lax.axis_index("x") + 2 * lax.axis_index("y") + lax.axis_index("c")
    big = [n for n, _, _ in _MATMUL_WEIGHTS]
    two_d = lambda a: a.reshape(-1, a.shape[-1])

    to_full = {n: f for n, f, _ in _MATMUL_WEIGHTS}
    to_parts = {n: f for n, _, f in _MATMUL_WEIGHTS}
    first = [(n, 0) for n in _FIRST_WEIGHTS]
    later = [(n, i) for n in big for i in range(1 if n in _FIRST_WEIGHTS else 0, w[n].shape[0])]

    def shards(group):
        return [two_d(w[n][i].astype(BF16)) for n, i in group]

    def layers(group, gathered):
        out = {}
        for (n, i), g in zip(group, gathered):
            out.setdefault(n, []).append(to_full[n](g.reshape((N_DEV, 1) + w[n].shape[1:]))[0])
        return out

    gathered = _gather(shards(first) + [two_d(w[n]) for n in _CONV_WEIGHTS], name="gather_first")
    full = dict(w)
    full.update(layers(first, gathered))
    for n, g in zip(_CONV_WEIGHTS, gathered[len(first):]):
        full[n] = _col_full(g.reshape((N_DEV,) + w[n].shape))

    def host_of(n, i):
        if n.startswith("ffn2") and i == DEPTH - 1:
            return DEPTH - 1
        if (n.startswith("ffn1") and i == DEPTH - 1) or (n.startswith("sb_") and i == 1):
            return 1
        return 0

    gather_later = {}
    for host in (0, 1, DEPTH - 1):
        group = [(n, i) for n, i in later if host_of(n, i) == host]

        def merge(wd, gathered_group, group=group):
            wd = dict(wd)
            for n, ls in layers(group, gathered_group).items():
                have = wd[n] if isinstance(wd[n], list) else []
                wd[n] = have + ls
            return wd

        xs = shards(group)
        gather_later[host] = (_Hosted(_gather_steps, xs, _gather_shapes(xs), _GATHER_COPIES), merge)

    def chip_sums(group, grads, tag):
        parts = []
        for n, i in group:
            p8 = to_parts[n](grads[n][i][None].astype(BF16))
            parts.append(p8.reshape(4, 2, -1, p8.shape[-1]))
        from_sibling = _scatter_sibling(parts, name=f"scatter_sibling_{tag}")
        return [_pair_add(p, r, name=f"pair_add_{tag}_{n}{i}") for (n, i), p, r in zip(group, parts, from_sibling)]

    def scatter_early(grads):
        ys = chip_sums(later, grads, "later")
        return _Hosted(_chip_scatter_steps, ys, _chip_scatter_shapes(ys), _CHIP_SCATTER_COPIES)

    loss_part, dx, grads, recv_later = _local_step(x[0], loss_target[0], full, gather_later, scatter_early)
    loss = lax.psum(loss_part[0, 0], ("x", "y", "c"))

    recv_first = _scatter_chips(chip_sums(first, grads, "first"), name="scatter_chips_first")
    contrib = {n: [r] for (n, _), r in zip(first, recv_first)}
    for (n, _), r in zip(later, recv_later):
        contrib.setdefault(n, []).append(r)
    out_g, out_d, out_m, out_v = {}, {}, {}, {}

    def update(n, parts):
        res = _adamw_reduce(parts, two_d(w[n]), two_d(mom[n]), two_d(var[n]), name=f"adamw_{n}")
        out_g[n], out_d[n], out_m[n], out_v[n] = (r.reshape(w[n].shape) for r in res)

    for n in big:
        update(n, contrib[n][0] if len(contrib[n]) == 1 else jnp.concatenate(contrib[n], axis=1))

    small = list(_REPLICATED) + list(_CONV_WEIGHTS)
    small_shapes = [grads[n].shape for n in small]
    spacked, scounts = _pack_rows([grads[n].astype(F32) for n in small], 8)
    sg = _unpack_rows(_gather([spacked], name="gather_small_grads")[0], scounts, small_shapes, (N_DEV,))
    sg = dict(zip(small, sg))
    rep_w, rcounts = _pack_rows([w[n] for n in _REPLICATED], 8)
    rep_m, _ = _pack_rows([mom[n] for n in _REPLICATED], 8)
    rep_v, _ = _pack_rows([var[n] for n in _REPLICATED], 8)
    rep_p = jnp.concatenate([_rows_of(sg[n].reshape(N_DEV, -1)[q]) for q in range(N_DEV) for n in _REPLICATED], axis=0)
    rep_p = rep_p.reshape(N_DEV, -1, _LANES)
    rep_p = jnp.pad(rep_p, ((0, 0), (0, rep_w.shape[0] - rep_p.shape[1]), (0, 0)))
    res = _adamw_reduce(rep_p, rep_w, rep_m, rep_v, name="adamw_replicated")
    rep_shapes = [w[n].shape for n in _REPLICATED]
    for tgt, r in zip((out_g, out_d, out_m, out_v), res):
        for n, a in zip(_REPLICATED, _unpack_rows(r, rcounts, rep_shapes)):
            tgt[n] = a
    for n in _CONV_WEIGHTS:
        c = w[n].shape[-1]
        mine = lax.dynamic_slice_in_dim(sg[n], me * c, c, axis=sg[n].ndim - 1)
        update(n, mine.reshape(N_DEV, -1, c))

    return (loss, dx[None], *[out_g[n] for n in _ORDER], *[out_d[n] for n in _ORDER],
            *[out_m[n] for n in _ORDER], *[out_v[n] for n in _ORDER])
```
